```python
import jax, jax.numpy as jnp
from jax import lax
import numpy as np

D_MODEL = 1024
BATCH = 8
SEQ = 2048
DEPTH = 4

N_MIXERS = 3
N_SUB = 3
FFN_RES_WEIGHT = 0.5
D_FF = 2816
RMS_EPS = 1e-6
FOX_HEADS = 16
FOX_HEAD_DIM = D_MODEL // FOX_HEADS
FOX_BLOCK = 128
SCONV_WIDTH = 3
LRU_WIDTH = D_MODEL
LRU_BLOCKS = 16
LRU_BLOCK_DIM = LRU_WIDTH // LRU_BLOCKS
LRU_CONV_WIDTH = 4
LRU_C = 8.0
N_FOX = len(range(0, DEPTH, N_MIXERS))
N_SCONV = len(range(1, DEPTH, N_MIXERS))
N_LRU = len(range(2, DEPTH, N_MIXERS))

kernel_name = "hybrid_fox_shortconv_rglru_macaron"


def rmsnorm(x, g):
    x32 = x.astype(jnp.float32)
    y = x32 * lax.rsqrt(jnp.mean(x32 * x32, axis=-1, keepdims=True) + RMS_EPS)
    return y.astype(x.dtype) * g


def causal_depthwise_conv(u, w, b=None):
    k_w, ch = w.shape
    out = lax.conv_general_dilated(
        u, w[:, None, :].astype(u.dtype), window_strides=(1,),
        padding=[(k_w - 1, 0)], dimension_numbers=("NWC", "WIO", "NWC"),
        feature_group_count=ch)
    if b is not None:
        out = out + b
    return out


def swiglu(h, w_in, w_out):
    g, u = jnp.split(h @ w_in, 2, axis=-1)
    return (jax.nn.silu(g) * u) @ w_out


def fox_mixer(h, w_in, b_f, w_out):
    bsz, seq, _ = h.shape
    proj = h @ w_in
    q, k, v, f_logit = jnp.split(proj, [D_MODEL, 2 * D_MODEL, 3 * D_MODEL], axis=-1)
    q = q.reshape(bsz, seq, FOX_HEADS, FOX_HEAD_DIM)
    k = k.reshape(bsz, seq, FOX_HEADS, FOX_HEAD_DIM)
    v = v.reshape(bsz, seq, FOX_HEADS, FOX_HEAD_DIM)
    log_f = jax.nn.log_sigmoid((f_logit + b_f).astype(jnp.float32))
    cum = jnp.cumsum(log_f, axis=1).transpose(0, 2, 1)
    scale = FOX_HEAD_DIM ** -0.5
    outs = []
    for blk in range(seq // FOX_BLOCK):
        s0 = blk * FOX_BLOCK
        s1 = s0 + FOX_BLOCK
        logits = jnp.einsum("bqhd,bkhd->bhqk", q[:, s0:s1], k[:, :s1]).astype(jnp.float32) * scale
        logits = logits + cum[:, :, s0:s1, None] - cum[:, :, None, :s1]
        q_pos = jnp.arange(s0, s1)[:, None]
        k_pos = jnp.arange(s1)[None, :]
        logits = jnp.where(k_pos <= q_pos, logits, -jnp.inf)
        p = jax.nn.softmax(logits, axis=-1).astype(v.dtype)
        outs.append(jnp.einsum("bhqk,bkhd->bqhd", p, v[:, :s1]))
    o = jnp.concatenate(outs, axis=1).reshape(bsz, seq, D_MODEL)
    return o @ w_out


def sconv_mixer(h, w_in, conv_w, w_out):
    b_gate, c_gate, xv = jnp.split(h @ w_in, 3, axis=-1)
    y = b_gate * causal_depthwise_conv(c_gate * xv, conv_w)
    return y @ w_out


def lru_mixer(h, w_in, conv_w, conv_b, w_a, b_a, w_x, b_x, lam, w_out):
    bsz, seq, _ = h.shape
    gate, xb = jnp.split(h @ w_in, 2, axis=-1)
    xb = causal_depthwise_conv(xb, conv_w, conv_b)
    xh = xb.reshape(bsz, seq, LRU_BLOCKS, LRU_BLOCK_DIM)
    r = jax.nn.sigmoid(jnp.einsum("bsni,nij->bsnj", xh, w_a) + b_a).reshape(bsz, seq, LRU_WIDTH)
    i = jax.nn.sigmoid(jnp.einsum("bsni,nij->bsnj", xh, w_x) + b_x).reshape(bsz, seq, LRU_WIDTH)
    log_a = -LRU_C * r.astype(jnp.float32) * jax.nn.softplus(-lam.astype(jnp.float32))
    a = jnp.exp(log_a)
    mult = jnp.sqrt(-jnp.expm1(2.0 * log_a))
    b_term = mult * (i * xb).astype(jnp.float32)

    def combine(left, right):
        a1, b1 = left
        a2, b2 = right
        return a1 * a2, a2 * b1 + b2

    _, hs = lax.associative_scan(combine, (a, b_term), axis=1)
    y = hs.astype(h.dtype) * jax.nn.gelu(gate)
    return y @ w_out


def _fwd_setup_inputs(seed: int = 0) -> dict:
    key = jax.random.key(seed)
    ks = jax.random.split(key, 24)
    f32 = jnp.float32

    def nrm(k, shape, fan_in):
        return jax.random.normal(k, shape, f32) * (fan_in ** -0.5)

    x = jax.random.normal(ks[0], (BATCH, SEQ, D_MODEL), f32)
    c = jax.random.normal(ks[1], (BATCH, D_MODEL), f32)
    w_cond = nrm(ks[2], (DEPTH, D_MODEL, N_SUB * 3 * D_MODEL), D_MODEL)
    b_cond = 0.02 * jax.random.normal(ks[3], (DEPTH, N_SUB * 3 * D_MODEL), f32)
    norm_pre = 1.0 + 0.05 * jax.random.normal(ks[4], (DEPTH, N_SUB, D_MODEL), f32)
    norm_post = 1.0 + 0.05 * jax.random.normal(ks[5], (DEPTH, N_SUB, D_MODEL), f32)
    w_ffn_in = nrm(ks[6], (DEPTH, 2, D_MODEL, 2 * D_FF), D_MODEL)
    w_ffn_out = nrm(ks[7], (DEPTH, 2, D_FF, D_MODEL), D_FF)
    fox_w_in = nrm(ks[8], (N_FOX, D_MODEL, 3 * D_MODEL + FOX_HEADS), D_MODEL)
    fox_b_f = jax.random.uniform(ks[9], (N_FOX, FOX_HEADS), f32, 1.0, 4.0)
    fox_w_out = nrm(ks[10], (N_FOX, D_MODEL, D_MODEL), D_MODEL)
    sconv_w_in = nrm(ks[11], (N_SCONV, D_MODEL, 3 * D_MODEL), D_MODEL)
    sconv_conv_w = nrm(ks[12], (N_SCONV, SCONV_WIDTH, D_MODEL), SCONV_WIDTH)
    sconv_w_out = nrm(ks[13], (N_SCONV, D_MODEL, D_MODEL), D_MODEL)
    lru_w_in = nrm(ks[14], (N_LRU, D_MODEL, 2 * LRU_WIDTH), D_MODEL)
    lru_conv_w = nrm(ks[15], (N_LRU, LRU_CONV_WIDTH, LRU_WIDTH), LRU_CONV_WIDTH)
    lru_conv_b = 0.02 * jax.random.normal(ks[16], (N_LRU, LRU_WIDTH), f32)
    lru_w_a = nrm(ks[17], (N_LRU, LRU_BLOCKS, LRU_BLOCK_DIM, LRU_BLOCK_DIM), LRU_BLOCK_DIM)
    lru_b_a = 0.02 * jax.random.normal(ks[18], (N_LRU, LRU_BLOCKS, LRU_BLOCK_DIM), f32)
    lru_w_x = nrm(ks[19], (N_LRU, LRU_BLOCKS, LRU_BLOCK_DIM, LRU_BLOCK_DIM), LRU_BLOCK_DIM)
    lru_b_x = 0.02 * jax.random.normal(ks[20], (N_LRU, LRU_BLOCKS, LRU_BLOCK_DIM), f32)
    a_c = jax.random.uniform(ks[21], (N_LRU, LRU_WIDTH), f32, 0.9, 0.999)
    s = a_c ** (1.0 / LRU_C)
    lru_lambda = jnp.log(s) - jnp.log1p(-s)
    lru_w_out = nrm(ks[22], (N_LRU, LRU_WIDTH, D_MODEL), LRU_WIDTH)
    return {
        "x": x, "c": c, "w_cond": w_cond, "b_cond": b_cond,
        "norm_pre": norm_pre, "norm_post": norm_post,
        "w_ffn_in": w_ffn_in, "w_ffn_out": w_ffn_out,
        "fox_w_in": fox_w_in, "fox_b_f": fox_b_f, "fox_w_out": fox_w_out,
        "sconv_w_in": sconv_w_in, "sconv_conv_w": sconv_conv_w, "sconv_w_out": sconv_w_out,
        "lru_w_in": lru_w_in, "lru_conv_w": lru_conv_w, "lru_conv_b": lru_conv_b,
        "lru_w_a": lru_w_a, "lru_b_a": lru_b_a, "lru_w_x": lru_w_x, "lru_b_x": lru_b_x,
        "lru_lambda": lru_lambda, "lru_w_out": lru_w_out,
    }


def _fwd_reference(x, c, w_cond, b_cond, norm_pre, norm_post, w_ffn_in, w_ffn_out,
              fox_w_in, fox_b_f, fox_w_out, sconv_w_in, sconv_conv_w, sconv_w_out,
              lru_w_in, lru_conv_w, lru_conv_b, lru_w_a, lru_b_a, lru_w_x, lru_b_x,
              lru_lambda, lru_w_out):
    bsz = x.shape[0]
    c_act = jax.nn.silu(c)
    for i in range(DEPTH):
        mod = (c_act @ w_cond[i] + b_cond[i]).reshape(bsz, N_SUB, 3, D_MODEL)
        shift = mod[:, :, 0, None, :]
        scale = mod[:, :, 1, None, :]
        gate = mod[:, :, 2, None, :]

        def pre(h, s):
            return rmsnorm(h, norm_pre[i, s]) * (1.0 + scale[:, s]) + shift[:, s]

        y = swiglu(pre(x, 0), w_ffn_in[i, 0], w_ffn_out[i, 0])
        x = x + FFN_RES_WEIGHT * gate[:, 0] * rmsnorm(y, norm_post[i, 0])

        h = pre(x, 1)
        kind = i % N_MIXERS
        j = i // N_MIXERS
        if kind == 0:
            y = fox_mixer(h, fox_w_in[j], fox_b_f[j], fox_w_out[j])
        elif kind == 1:
            y = sconv_mixer(h, sconv_w_in[j], sconv_conv_w[j], sconv_w_out[j])
        else:
            y = lru_mixer(h, lru_w_in[j], lru_conv_w[j], lru_conv_b[j], lru_w_a[j], lru_b_a[j],
                          lru_w_x[j], lru_b_x[j], lru_lambda[j], lru_w_out[j])
        x = x + gate[:, 1] * rmsnorm(y, norm_post[i, 1])

        y = swiglu(pre(x, 2), w_ffn_in[i, 1], w_ffn_out[i, 1])
        x = x + FFN_RES_WEIGHT * gate[:, 2] * rmsnorm(y, norm_post[i, 2])
    return x


import jax as _jax
import jax.numpy as _jnp

TWIN_FORMAT = 'train_step'
FWD_PARAMS = ['x', 'c', 'w_cond', 'b_cond', 'norm_pre', 'norm_post', 'w_ffn_in', 'w_ffn_out', 'fox_w_in', 'fox_b_f', 'fox_w_out', 'sconv_w_in', 'sconv_conv_w', 'sconv_w_out', 'lru_w_in', 'lru_conv_w', 'lru_conv_b', 'lru_w_a', 'lru_b_a', 'lru_w_x', 'lru_b_x', 'lru_lambda', 'lru_w_out']
TWIN_WEIGHTS = ['w_cond', 'b_cond', 'norm_pre', 'norm_post', 'w_ffn_in', 'w_ffn_out', 'fox_w_in', 'fox_b_f', 'fox_w_out', 'sconv_w_in', 'sconv_conv_w', 'sconv_w_out', 'lru_w_in', 'lru_conv_w', 'lru_conv_b', 'lru_w_a', 'lru_b_a', 'lru_w_x', 'lru_b_x', 'lru_lambda', 'lru_w_out']
TWIN_DIFF_INPUT = 'x'
TWIN_INPUTS = ['x', 'c', 'w_cond', 'b_cond', 'norm_pre', 'norm_post', 'w_ffn_in', 'w_ffn_out', 'fox_w_in', 'fox_b_f', 'fox_w_out', 'sconv_w_in', 'sconv_conv_w', 'sconv_w_out', 'lru_w_in', 'lru_conv_w', 'lru_conv_b', 'lru_w_a', 'lru_b_a', 'lru_w_x', 'lru_b_x', 'lru_lambda', 'lru_w_out', 'loss_target', 'm_w_cond', 'm_b_cond', 'm_norm_pre', 'm_norm_post', 'm_w_ffn_in', 'm_w_ffn_out', 'm_fox_w_in', 'm_fox_b_f', 'm_fox_w_out', 'm_sconv_w_in', 'm_sconv_conv_w', 'm_sconv_w_out', 'm_lru_w_in', 'm_lru_conv_w', 'm_lru_conv_b', 'm_lru_w_a', 'm_lru_b_a', 'm_lru_w_x', 'm_lru_b_x', 'm_lru_lambda', 'm_lru_w_out', 'v_w_cond', 'v_b_cond', 'v_norm_pre', 'v_norm_post', 'v_w_ffn_in', 'v_w_ffn_out', 'v_fox_w_in', 'v_fox_b_f', 'v_fox_w_out', 'v_sconv_w_in', 'v_sconv_conv_w', 'v_sconv_w_out', 'v_lru_w_in', 'v_lru_conv_w', 'v_lru_conv_b', 'v_lru_w_a', 'v_lru_b_a', 'v_lru_w_x', 'v_lru_b_x', 'v_lru_lambda', 'v_lru_w_out']
TWIN_OUTPUTS = ['loss', 'grad_x', 'grad_w_cond', 'grad_b_cond', 'grad_norm_pre', 'grad_norm_post', 'grad_w_ffn_in', 'grad_w_ffn_out', 'grad_fox_w_in', 'grad_fox_b_f', 'grad_fox_w_out', 'grad_sconv_w_in', 'grad_sconv_conv_w', 'grad_sconv_w_out', 'grad_lru_w_in', 'grad_lru_conv_w', 'grad_lru_conv_b', 'grad_lru_w_a', 'grad_lru_b_a', 'grad_lru_w_x', 'grad_lru_b_x', 'grad_lru_lambda', 'grad_lru_w_out', 'delta_w_cond', 'delta_b_cond', 'delta_norm_pre', 'delta_norm_post', 'delta_w_ffn_in', 'delta_w_ffn_out', 'delta_fox_w_in', 'delta_fox_b_f', 'delta_fox_w_out', 'delta_sconv_w_in', 'delta_sconv_conv_w', 'delta_sconv_w_out', 'delta_lru_w_in', 'delta_lru_conv_w', 'delta_lru_conv_b', 'delta_lru_w_a', 'delta_lru_b_a', 'delta_lru_w_x', 'delta_lru_b_x', 'delta_lru_lambda', 'delta_lru_w_out', 'new_m_w_cond', 'new_m_b_cond', 'new_m_norm_pre', 'new_m_norm_post', 'new_m_w_ffn_in', 'new_m_w_ffn_out', 'new_m_fox_w_in', 'new_m_fox_b_f', 'new_m_fox_w_out', 'new_m_sconv_w_in', 'new_m_sconv_conv_w', 'new_m_sconv_w_out', 'new_m_lru_w_in', 'new_m_lru_conv_w', 'new_m_lru_conv_b', 'new_m_lru_w_a', 'new_m_lru_b_a', 'new_m_lru_w_x', 'new_m_lru_b_x', 'new_m_lru_lambda', 'new_m_lru_w_out', 'new_v_w_cond', 'new_v_b_cond', 'new_v_norm_pre', 'new_v_norm_post', 'new_v_w_ffn_in', 'new_v_w_ffn_out', 'new_v_fox_w_in', 'new_v_fox_b_f', 'new_v_fox_w_out', 'new_v_sconv_w_in', 'new_v_sconv_conv_w', 'new_v_sconv_w_out', 'new_v_lru_w_in', 'new_v_lru_conv_w', 'new_v_lru_conv_b', 'new_v_lru_w_a', 'new_v_lru_b_a', 'new_v_lru_w_x', 'new_v_lru_b_x', 'new_v_lru_lambda', 'new_v_lru_w_out']
TWIN_LEAF_KINDS = {'loss': 'loss', 'grad_x': 'grad_x', 'grad_w_cond': 'grad_w', 'grad_b_cond': 'grad_w', 'grad_norm_pre': 'grad_w', 'grad_norm_post': 'grad_w', 'grad_w_ffn_in': 'grad_w', 'grad_w_ffn_out': 'grad_w', 'grad_fox_w_in': 'grad_w', 'grad_fox_b_f': 'grad_w', 'grad_fox_w_out': 'grad_w', 'grad_sconv_w_in': 'grad_w', 'grad_sconv_conv_w': 'grad_w', 'grad_sconv_w_out': 'grad_w', 'grad_lru_w_in': 'grad_w', 'grad_lru_conv_w': 'grad_w', 'grad_lru_conv_b': 'grad_w', 'grad_lru_w_a': 'grad_w', 'grad_lru_b_a': 'grad_w', 'grad_lru_w_x': 'grad_w', 'grad_lru_b_x': 'grad_w', 'grad_lru_lambda': 'grad_w', 'grad_lru_w_out': 'grad_w', 'delta_w_cond': 'delta_w', 'delta_b_cond': 'delta_w', 'delta_norm_pre': 'delta_w', 'delta_norm_post': 'delta_w', 'delta_w_ffn_in': 'delta_w', 'delta_w_ffn_out': 'delta_w', 'delta_fox_w_in': 'delta_w', 'delta_fox_b_f': 'delta_w', 'delta_fox_w_out': 'delta_w', 'delta_sconv_w_in': 'delta_w', 'delta_sconv_conv_w': 'delta_w', 'delta_sconv_w_out': 'delta_w', 'delta_lru_w_in': 'delta_w', 'delta_lru_conv_w': 'delta_w', 'delta_lru_conv_b': 'delta_w', 'delta_lru_w_a': 'delta_w', 'delta_lru_b_a': 'delta_w', 'delta_lru_w_x': 'delta_w', 'delta_lru_b_x': 'delta_w', 'delta_lru_lambda': 'delta_w', 'delta_lru_w_out': 'delta_w', 'new_m_w_cond': 'new_m', 'new_m_b_cond': 'new_m', 'new_m_norm_pre': 'new_m', 'new_m_norm_post': 'new_m', 'new_m_w_ffn_in': 'new_m', 'new_m_w_ffn_out': 'new_m', 'new_m_fox_w_in': 'new_m', 'new_m_fox_b_f': 'new_m', 'new_m_fox_w_out': 'new_m', 'new_m_sconv_w_in': 'new_m', 'new_m_sconv_conv_w': 'new_m', 'new_m_sconv_w_out': 'new_m', 'new_m_lru_w_in': 'new_m', 'new_m_lru_conv_w': 'new_m', 'new_m_lru_conv_b': 'new_m', 'new_m_lru_w_a': 'new_m', 'new_m_lru_b_a': 'new_m', 'new_m_lru_w_x': 'new_m', 'new_m_lru_b_x': 'new_m', 'new_m_lru_lambda': 'new_m', 'new_m_lru_w_out': 'new_m', 'new_v_w_cond': 'new_v', 'new_v_b_cond': 'new_v', 'new_v_norm_pre': 'new_v', 'new_v_norm_post': 'new_v', 'new_v_w_ffn_in': 'new_v', 'new_v_w_ffn_out': 'new_v', 'new_v_fox_w_in': 'new_v', 'new_v_fox_b_f': 'new_v', 'new_v_fox_w_out': 'new_v', 'new_v_sconv_w_in': 'new_v', 'new_v_sconv_conv_w': 'new_v', 'new_v_sconv_w_out': 'new_v', 'new_v_lru_w_in': 'new_v', 'new_v_lru_conv_w': 'new_v', 'new_v_lru_conv_b': 'new_v', 'new_v_lru_w_a': 'new_v', 'new_v_lru_b_a': 'new_v', 'new_v_lru_w_x': 'new_v', 'new_v_lru_b_x': 'new_v', 'new_v_lru_lambda': 'new_v', 'new_v_lru_w_out': 'new_v'}


def _forward(args):
    return _fwd_reference(*[args[k] for k in FWD_PARAMS])


def _output_shape():
    out = _jax.eval_shape(lambda: _forward(_fwd_setup_inputs(0)))
    return out.shape, out.dtype

N_MICROBATCH = 1
ADAM_LR = 0.001
ADAM_B1 = 0.9
ADAM_B2 = 0.999
ADAM_EPS = 1e-08
ADAM_WD = 0.01
ADAM_STEP = 10
PER_EXAMPLE_BATCH_AXIS = {'x': 0, 'c': 0, 'loss_target': 0}
SHARED_INPUTS = []
_WEIGHT_DTYPES = {'w_cond': _jnp.float32, 'b_cond': _jnp.float32, 'norm_pre': _jnp.float32, 'norm_post': _jnp.float32, 'w_ffn_in': _jnp.float32, 'w_ffn_out': _jnp.float32, 'fox_w_in': _jnp.float32, 'fox_b_f': _jnp.float32, 'fox_w_out': _jnp.float32, 'sconv_w_in': _jnp.float32, 'sconv_conv_w': _jnp.float32, 'sconv_w_out': _jnp.float32, 'lru_w_in': _jnp.float32, 'lru_conv_w': _jnp.float32, 'lru_conv_b': _jnp.float32, 'lru_w_a': _jnp.float32, 'lru_b_a': _jnp.float32, 'lru_w_x': _jnp.float32, 'lru_b_x': _jnp.float32, 'lru_lambda': _jnp.float32, 'lru_w_out': _jnp.float32}
MOMENT_SCALE = {'w_cond': 2.091920e+00, 'b_cond': 3.549910e+00, 'norm_pre': 1.084236e+00, 'norm_post': 5.230814e+00, 'w_ffn_in': 5.044853e-01, 'w_ffn_out': 9.303347e-01, 'fox_w_in': 2.478617e+00, 'fox_b_f': 6.513043e-01, 'fox_w_out': 4.312166e+00, 'sconv_w_in': 9.391645e-01, 'sconv_conv_w': 1.001462e+00, 'sconv_w_out': 9.385969e-01, 'lru_w_in': 2.763874e+00, 'lru_conv_w': 3.109511e+00, 'lru_conv_b': 4.610961e+00, 'lru_w_a': 3.514635e-01, 'lru_b_a': 3.045943e-01, 'lru_w_x': 8.946844e-01, 'lru_b_x': 9.266934e-01, 'lru_lambda': 8.751325e-01, 'lru_w_out': 3.028874e+00}


def _to_microbatches(a, axis):
    t = _jnp.moveaxis(a, axis, 0)
    t = t.reshape((N_MICROBATCH, t.shape[0] // N_MICROBATCH) + t.shape[1:])
    return _jnp.moveaxis(t, 1, axis + 1)


def setup_inputs(seed: int = 0) -> dict:
    inp = _fwd_setup_inputs(seed)
    key = _jax.random.fold_in(_jax.random.key(seed), 7919)
    shape, _ = _output_shape()
    out = dict(inp)
    out["loss_target"] = _jax.random.normal(_jax.random.fold_in(key, 0), shape, _jnp.float32)
    for i, name in enumerate(TWIN_WEIGHTS):
        w = inp[name].astype(_jnp.float32)
        if MOMENT_SCALE is None:
            s = _jnp.sqrt(_jnp.mean(_jnp.square(w)) + 1e-30)
        else:
            s = MOMENT_SCALE[name]
        km, kv = _jax.random.split(_jax.random.fold_in(key, i + 1))
        out[name] = w
        out["m_" + name] = s * _jax.random.normal(km, w.shape, _jnp.float32)
        out["v_" + name] = (s * s) * _jax.random.uniform(kv, w.shape, _jnp.float32, 0.5, 1.5)
    if N_MICROBATCH > 1:
        for name, axis in PER_EXAMPLE_BATCH_AXIS.items():
            out[name] = _to_microbatches(out[name], axis)
    return {'x': out['x'], 'c': out['c'], 'w_cond': out['w_cond'], 'b_cond': out['b_cond'], 'norm_pre': out['norm_pre'], 'norm_post': out['norm_post'], 'w_ffn_in': out['w_ffn_in'], 'w_ffn_out': out['w_ffn_out'], 'fox_w_in': out['fox_w_in'], 'fox_b_f': out['fox_b_f'], 'fox_w_out': out['fox_w_out'], 'sconv_w_in': out['sconv_w_in'], 'sconv_conv_w': out['sconv_conv_w'], 'sconv_w_out': out['sconv_w_out'], 'lru_w_in': out['lru_w_in'], 'lru_conv_w': out['lru_conv_w'], 'lru_conv_b': out['lru_conv_b'], 'lru_w_a': out['lru_w_a'], 'lru_b_a': out['lru_b_a'], 'lru_w_x': out['lru_w_x'], 'lru_b_x': out['lru_b_x'], 'lru_lambda': out['lru_lambda'], 'lru_w_out': out['lru_w_out'], 'loss_target': out['loss_target'], 'm_w_cond': out['m_w_cond'], 'm_b_cond': out['m_b_cond'], 'm_norm_pre': out['m_norm_pre'], 'm_norm_post': out['m_norm_post'], 'm_w_ffn_in': out['m_w_ffn_in'], 'm_w_ffn_out': out['m_w_ffn_out'], 'm_fox_w_in': out['m_fox_w_in'], 'm_fox_b_f': out['m_fox_b_f'], 'm_fox_w_out': out['m_fox_w_out'], 'm_sconv_w_in': out['m_sconv_w_in'], 'm_sconv_conv_w': out['m_sconv_conv_w'], 'm_sconv_w_out': out['m_sconv_w_out'], 'm_lru_w_in': out['m_lru_w_in'], 'm_lru_conv_w': out['m_lru_conv_w'], 'm_lru_conv_b': out['m_lru_conv_b'], 'm_lru_w_a': out['m_lru_w_a'], 'm_lru_b_a': out['m_lru_b_a'], 'm_lru_w_x': out['m_lru_w_x'], 'm_lru_b_x': out['m_lru_b_x'], 'm_lru_lambda': out['m_lru_lambda'], 'm_lru_w_out': out['m_lru_w_out'], 'v_w_cond': out['v_w_cond'], 'v_b_cond': out['v_b_cond'], 'v_norm_pre': out['v_norm_pre'], 'v_norm_post': out['v_norm_post'], 'v_w_ffn_in': out['v_w_ffn_in'], 'v_w_ffn_out': out['v_w_ffn_out'], 'v_fox_w_in': out['v_fox_w_in'], 'v_fox_b_f': out['v_fox_b_f'], 'v_fox_w_out': out['v_fox_w_out'], 'v_sconv_w_in': out['v_sconv_w_in'], 'v_sconv_conv_w': out['v_sconv_conv_w'], 'v_sconv_w_out': out['v_sconv_w_out'], 'v_lru_w_in': out['v_lru_w_in'], 'v_lru_conv_w': out['v_lru_conv_w'], 'v_lru_conv_b': out['v_lru_conv_b'], 'v_lru_w_a': out['v_lru_w_a'], 'v_lru_b_a': out['v_lru_b_a'], 'v_lru_w_x': out['v_lru_w_x'], 'v_lru_b_x': out['v_lru_b_x'], 'v_lru_lambda': out['v_lru_lambda'], 'v_lru_w_out': out['v_lru_w_out']}


def _loss(weights, diff, rest, loss_target):
    with _jax.named_scope("forward"):
        args = {**rest, TWIN_DIFF_INPUT: diff, **{k: w.astype(_WEIGHT_DTYPES[k]) for k, w in weights.items()}}
        y = _forward(args)
    with _jax.named_scope("loss_head"):
        err = _jnp.square(y.astype(_jnp.float32) - loss_target)
        return 0.5 * _jnp.sum(_jnp.mean(err, axis=-1)) if err.ndim else 0.5 * err


def _adamw(w, g, m, v):
    m = ADAM_B1 * m + (1.0 - ADAM_B1) * g
    v = ADAM_B2 * v + (1.0 - ADAM_B2) * _jnp.square(g)
    m_hat = m / (1.0 - ADAM_B1 ** ADAM_STEP)
    v_hat = v / (1.0 - ADAM_B2 ** ADAM_STEP)
    delta = -ADAM_LR * (m_hat / (_jnp.sqrt(v_hat) + ADAM_EPS) + ADAM_WD * w)
    return delta, m, v


def reference(x, c, w_cond, b_cond, norm_pre, norm_post, w_ffn_in, w_ffn_out, fox_w_in, fox_b_f, fox_w_out, sconv_w_in, sconv_conv_w, sconv_w_out, lru_w_in, lru_conv_w, lru_conv_b, lru_w_a, lru_b_a, lru_w_x, lru_b_x, lru_lambda, lru_w_out, loss_target, m_w_cond, m_b_cond, m_norm_pre, m_norm_post, m_w_ffn_in, m_w_ffn_out, m_fox_w_in, m_fox_b_f, m_fox_w_out, m_sconv_w_in, m_sconv_conv_w, m_sconv_w_out, m_lru_w_in, m_lru_conv_w, m_lru_conv_b, m_lru_w_a, m_lru_b_a, m_lru_w_x, m_lru_b_x, m_lru_lambda, m_lru_w_out, v_w_cond, v_b_cond, v_norm_pre, v_norm_post, v_w_ffn_in, v_w_ffn_out, v_fox_w_in, v_fox_b_f, v_fox_w_out, v_sconv_w_in, v_sconv_conv_w, v_sconv_w_out, v_lru_w_in, v_lru_conv_w, v_lru_conv_b, v_lru_w_a, v_lru_b_a, v_lru_w_x, v_lru_b_x, v_lru_lambda, v_lru_w_out):
    given = dict(x=x, c=c, w_cond=w_cond, b_cond=b_cond, norm_pre=norm_pre, norm_post=norm_post, w_ffn_in=w_ffn_in, w_ffn_out=w_ffn_out, fox_w_in=fox_w_in, fox_b_f=fox_b_f, fox_w_out=fox_w_out, sconv_w_in=sconv_w_in, sconv_conv_w=sconv_conv_w, sconv_w_out=sconv_w_out, lru_w_in=lru_w_in, lru_conv_w=lru_conv_w, lru_conv_b=lru_conv_b, lru_w_a=lru_w_a, lru_b_a=lru_b_a, lru_w_x=lru_w_x, lru_b_x=lru_b_x, lru_lambda=lru_lambda, lru_w_out=lru_w_out, loss_target=loss_target, m_w_cond=m_w_cond, m_b_cond=m_b_cond, m_norm_pre=m_norm_pre, m_norm_post=m_norm_post, m_w_ffn_in=m_w_ffn_in, m_w_ffn_out=m_w_ffn_out, m_fox_w_in=m_fox_w_in, m_fox_b_f=m_fox_b_f, m_fox_w_out=m_fox_w_out, m_sconv_w_in=m_sconv_w_in, m_sconv_conv_w=m_sconv_conv_w, m_sconv_w_out=m_sconv_w_out, m_lru_w_in=m_lru_w_in, m_lru_conv_w=m_lru_conv_w, m_lru_conv_b=m_lru_conv_b, m_lru_w_a=m_lru_w_a, m_lru_b_a=m_lru_b_a, m_lru_w_x=m_lru_w_x, m_lru_b_x=m_lru_b_x, m_lru_lambda=m_lru_lambda, m_lru_w_out=m_lru_w_out, v_w_cond=v_w_cond, v_b_cond=v_b_cond, v_norm_pre=v_norm_pre, v_norm_post=v_norm_post, v_w_ffn_in=v_w_ffn_in, v_w_ffn_out=v_w_ffn_out, v_fox_w_in=v_fox_w_in, v_fox_b_f=v_fox_b_f, v_fox_w_out=v_fox_w_out, v_sconv_w_in=v_sconv_w_in, v_sconv_conv_w=v_sconv_conv_w, v_sconv_w_out=v_sconv_w_out, v_lru_w_in=v_lru_w_in, v_lru_conv_w=v_lru_conv_w, v_lru_conv_b=v_lru_conv_b, v_lru_w_a=v_lru_w_a, v_lru_b_a=v_lru_b_a, v_lru_w_x=v_lru_w_x, v_lru_b_x=v_lru_b_x, v_lru_lambda=v_lru_lambda, v_lru_w_out=v_lru_w_out)
    weights = {n: given[n] for n in TWIN_WEIGHTS}
    shared = {n: given[n] for n in SHARED_INPUTS}
    per_example = {n: given[n] for n in ['x', 'c']}
    grad_fn = _jax.value_and_grad(_loss, argnums=(0, 1))

    def one_microbatch(ex, loss_target):
        ex = dict(ex)
        diff = ex.pop(TWIN_DIFF_INPUT)
        return grad_fn(weights, diff, {**shared, **ex}, loss_target)

    if N_MICROBATCH == 1:
        loss, (grad_w, grad_x) = one_microbatch(per_example, given["loss_target"])
    else:
        def body(carry, xs):
            loss_sum, grad_sum = carry
            l_k, (gw_k, gx_k) = one_microbatch(xs[0], xs[1])
            with _jax.named_scope("update"):
                return (loss_sum + l_k, _jax.tree.map(_jnp.add, grad_sum, gw_k)), gx_k

        init = (_jnp.zeros((), _jnp.float32), _jax.tree.map(_jnp.zeros_like, weights))
        (loss, grad_w), grad_x = _jax.lax.scan(body, init, (per_example, given["loss_target"]))
    with _jax.named_scope("update"):
        delta_w, new_m, new_v = {}, {}, {}
        for n in TWIN_WEIGHTS:
            delta_w[n], new_m[n], new_v[n] = _adamw(weights[n], grad_w[n], given["m_" + n], given["v_" + n])
    return (loss, grad_x, *[grad_w[n] for n in TWIN_WEIGHTS], *[delta_w[n] for n in TWIN_WEIGHTS],
            *[new_m[n] for n in TWIN_WEIGHTS], *[new_v[n] for n in TWIN_WEIGHTS])
```

```python
import functools
import math
from typing import NamedTuple

import jax
import jax.numpy as jnp
from jax import lax
from jax.experimental import pallas as pl
from jax.experimental.pallas import tpu as pltpu

F32 = jnp.float32
BF16 = jnp.bfloat16

D_MODEL = 1024
DEPTH = 4
N_SUB = 3
D_FF = 2816
RMS_EPS = 1e-6
FOX_HEADS = 16
FOX_HEAD_DIM = 64
FOX_PAD = 3200
LRU_BLOCKS = 16
LRU_BLOCK_DIM = 64
LRU_C = 8.0
N_CHIPS = 4
N_DEV = 8

ADAM_LR = 0.001
ADAM_B1 = 0.9
ADAM_B2 = 0.999
ADAM_EPS = 1e-08
ADAM_WD = 0.01
ADAM_STEP = 10

VMEM_LIMIT_V7X = 56 * 1024 * 1024
ROW_TILE = 256
COL_TILE = 256
ATT_TILE = 256
MM_ROWS = 256


def _cparams(sem=None):
    return pltpu.CompilerParams(vmem_limit_bytes=VMEM_LIMIT_V7X, dimension_semantics=sem)


def _sigmoid(z):
    return 1.0 / (1.0 + jnp.exp(-z))


def _softplus(z):
    return jnp.maximum(z, 0.0) + jnp.log(1.0 + jnp.exp(-jnp.abs(z)))


def _rows_sum(v):
    return jnp.sum(v, axis=0, keepdims=True)


class _W(NamedTuple):
    arr: jax.Array
    prefix: tuple = ()
    blocked: bool = False


def _w_spec(w, block2, pos):
    lead = (None,) * (len(w.prefix) + (1 if w.blocked else 0))
    if w.blocked:
        return pl.BlockSpec(lead + block2, lambda *g: (pos(*g)[0], *w.prefix, pos(*g)[1], pos(*g)[2]))
    return pl.BlockSpec(lead + block2, lambda *g: (*w.prefix, pos(*g)[1], pos(*g)[2]))


def _mm_nn(a, b, name, tn=None):
    m, k = a.shape
    if b.blocked:
        steps, bn = b.arr.shape[0], b.arr.shape[-1]
        b_spec = _w_spec(b, (k, bn), lambda n: (n, 0, 0))
    else:
        n_total = b.arr.shape[-1]
        bn = n_total if tn is None else tn
        steps = n_total // bn
        assert steps * bn == n_total
        b_spec = _w_spec(b, (k, bn), lambda n: (0, 0, n))
    tm = min(MM_ROWS, m)

    def body(a_ref, b_ref, o_ref):
        def step(i, carry):
            r = pl.ds(pl.multiple_of(i * tm, tm), tm)
            o_ref[r, :] = jnp.dot(a_ref[r, :], b_ref[...], preferred_element_type=F32)
            return carry
        lax.fori_loop(0, m // tm, step, 0)

    return pl.pallas_call(
        body, name=name, grid=(steps,),
        in_specs=[pl.BlockSpec((m, k), lambda n: (0, 0)), b_spec],
        out_specs=pl.BlockSpec((m, bn), lambda n: (0, n)),
        out_shape=jax.ShapeDtypeStruct((m, steps * bn), F32),
        compiler_params=_cparams(("arbitrary",)),
    )(a, b.arr)


def _mm_nt(dy, w, name, tk=None, tn=None):
    m, n_total = dy.shape
    k = w.arr.shape[-2]
    if w.blocked:
        bk, bn = k, w.arr.shape[-1]
        grid = (1, w.arr.shape[0])
        w_spec = _w_spec(w, (k, bn), lambda kt, n: (n, 0, 0))
    else:
        bk = k if tk is None else tk
        bn = n_total if tn is None else tn
        grid = (k // bk, n_total // bn)
        assert grid[0] * bk == k and grid[1] * bn == n_total
        w_spec = _w_spec(w, (bk, bn), lambda kt, n: (0, kt, n))
    tm = min(MM_ROWS, m)

    def body(dy_ref, w_ref, o_ref):
        def step(i, carry):
            r = pl.ds(pl.multiple_of(i * tm, tm), tm)
            o_ref[r, :] += lax.dot_general(dy_ref[r, :], w_ref[...], (((1,), (1,)), ((), ())),
                                           preferred_element_type=F32)
            return carry

        @pl.when(pl.program_id(1) == 0)
        def _():
            o_ref[...] = jnp.zeros_like(o_ref)
        lax.fori_loop(0, m // tm, step, 0)

    return pl.pallas_call(
        body, name=name, grid=grid,
        in_specs=[pl.BlockSpec((m, bn), lambda kt, n: (0, n)), w_spec],
        out_specs=pl.BlockSpec((m, bk), lambda kt, n: (0, kt)),
        out_shape=jax.ShapeDtypeStruct((m, k), F32),
        compiler_params=_cparams(("arbitrary", "arbitrary")),
    )(dy, w.arr)


def _mm_tn(x, dy, name, tk=None, tn=None, blocked_out=False):
    s, k = x.shape
    n_total = dy.shape[1]
    bk = k if tk is None else tk
    bn = n_total if tn is None else tn
    grid = (k // bk, n_total // bn)
    assert grid[0] * bk == k and grid[1] * bn == n_total
    ck = 256 if bk % 256 == 0 else 128

    def body(x_ref, dy_ref, o_ref):
        def step(i, carry):
            c = pl.ds(pl.multiple_of(i * ck, ck), ck)
            o_ref[c, :] = lax.dot_general(x_ref[:, c], dy_ref[...], (((0,), (0,)), ((), ())),
                                          preferred_element_type=F32)
            return carry
        lax.fori_loop(0, bk // ck, step, 0)

    if blocked_out:
        assert grid[0] == 1
        out_spec = pl.BlockSpec((None, bk, bn), lambda kt, n: (n, 0, 0))
        out_shape = jax.ShapeDtypeStruct((grid[1], k, bn), F32)
    else:
        out_spec = pl.BlockSpec((bk, bn), lambda kt, n: (kt, n))
        out_shape = jax.ShapeDtypeStruct((k, n_total), F32)
    return pl.pallas_call(
        body, name=name, grid=grid,
        in_specs=[pl.BlockSpec((s, bk), lambda kt, n: (0, kt)), pl.BlockSpec((s, bn), lambda kt, n: (0, n))],
        out_specs=out_spec, out_shape=out_shape,
        compiler_params=_cparams(("arbitrary", "arbitrary")),
    )(x, dy)


def _row_call(name, body, rows, fulls, row_outs, acc_outs, tr=ROW_TILE):
    s = rows[0].shape[0]
    tr = min(tr, s)
    in_specs = [pl.BlockSpec((tr, a.shape[1]), lambda i: (i, 0)) for a in rows]
    in_specs += [pl.BlockSpec(a.shape, lambda i: (0, 0)) for a in fulls]
    out_specs = [pl.BlockSpec((tr, c), lambda i: (i, 0)) for c, _ in row_outs]
    out_specs += [pl.BlockSpec((1, c), lambda i: (0, 0)) for c, _ in acc_outs]
    out_shape = [jax.ShapeDtypeStruct((s, c), dt) for c, dt in row_outs]
    out_shape += [jax.ShapeDtypeStruct((1, c), dt) for c, dt in acc_outs]
    n_acc = len(acc_outs)

    def wrapped(*refs):
        if n_acc:
            @pl.when(pl.program_id(0) == 0)
            def _():
                for r in refs[len(refs) - n_acc:]:
                    r[...] = jnp.zeros_like(r)
        body(*refs)

    return pl.pallas_call(
        wrapped, name=name, grid=(s // tr,), in_specs=in_specs, out_specs=out_specs, out_shape=out_shape,
        compiler_params=_cparams(("arbitrary",)),
    )(*rows, *fulls)


def _rms(v):
    return lax.rsqrt(jnp.mean(v * v, axis=-1, keepdims=True) + RMS_EPS)


def _pre_norm(x, g_pre, scale, shift, name):
    def body(x_ref, g_ref, sc_ref, sh_ref, h_ref):
        xv = x_ref[...]
        h = (xv * _rms(xv)) * g_ref[...] * (1.0 + sc_ref[...]) + sh_ref[...]
        h_ref[...] = h.astype(BF16)
    return _row_call(name, body, [x], [g_pre, scale, shift], [(D_MODEL, BF16)], [])[0]


def _post_norm(x, y, g_post, gate, coef, name):
    def body(x_ref, y_ref, g_ref, gate_ref, o_ref):
        yv = y_ref[...]
        o_ref[...] = x_ref[...] + (coef * gate_ref[...]) * ((yv * _rms(yv)) * g_ref[...])
    return _row_call(name, body, [x, y], [g_post, gate], [(D_MODEL, F32)], [])[0]


def _post_norm_bwd(dxo, y, g_post, gate, coef, name):
    def body(dxo_ref, y_ref, g_ref, gate_ref, dy_ref, dgate_ref, dg_ref):
        yv = y_ref[...]
        r2 = _rms(yv)
        yn = yv * r2
        dxo_v = dxo_ref[...]
        dgate_ref[...] += _rows_sum(dxo_v * (yn * g_ref[...])) * coef
        dz = dxo_v * (coef * gate_ref[...])
        dg_ref[...] += _rows_sum(dz * yn)
        dyn = dz * g_ref[...]
        dy = r2 * (dyn - yn * jnp.mean(dyn * yn, axis=-1, keepdims=True))
        dy_ref[...] = dy.astype(BF16)
    return _row_call(name, body, [dxo, y], [g_post, gate], [(D_MODEL, BF16)], [(D_MODEL, F32), (D_MODEL, F32)])


def _pre_norm_bwd(dxo, dh, x, g_pre, scale, name):
    def body(dxo_ref, dh_ref, x_ref, g_ref, sc_ref, dx_ref, dshift_ref, dscale_ref, dg_ref):
        xv = x_ref[...]
        r = _rms(xv)
        xn = xv * r
        dh_v = dh_ref[...]
        one_sc = 1.0 + sc_ref[...]
        dshift_ref[...] += _rows_sum(dh_v)
        dscale_ref[...] += _rows_sum(dh_v * (xn * g_ref[...]))
        dg_ref[...] += _rows_sum(dh_v * xn * one_sc)
        dxn = dh_v * (g_ref[...] * one_sc)
        dx_ref[...] = dxo_ref[...] + r * (dxn - xn * jnp.mean(dxn * xn, axis=-1, keepdims=True))
    return _row_call(name, body, [dxo, dh, x], [g_pre, scale], [(D_MODEL, F32)],
                     [(D_MODEL, F32), (D_MODEL, F32), (D_MODEL, F32)])


def _swiglu_act(gu, name):
    def body(gu_ref, a_ref):
        g = gu_ref[:, :D_FF]
        a_ref[...] = (g * _sigmoid(g) * gu_ref[:, D_FF:]).astype(BF16)
    return _row_call(name, body, [gu], [], [(D_FF, BF16)], [])[0]


def _swiglu_act_bwd(da, gu, name):
    def body(da_ref, gu_ref, dgu_ref):
        g = gu_ref[:, :D_FF]
        sg = _sigmoid(g)
        da_v = da_ref[...]
        dgu_ref[:, :D_FF] = (da_v * gu_ref[:, D_FF:] * (sg * (1.0 + g * (1.0 - sg)))).astype(BF16)
        dgu_ref[:, D_FF:] = (da_v * (g * sg)).astype(BF16)
    return _row_call(name, body, [da, gu], [], [(2 * D_FF, BF16)], [])[0]


def _loss_head(y, target, name):
    def body(y_ref, t_ref, dy_ref, loss_ref):
        e = y_ref[...] - t_ref[...]
        dy_ref[...] = e * (1.0 / D_MODEL)
        part = jnp.sum(jnp.mean(e * e, axis=-1, keepdims=True), axis=0, keepdims=True) * 0.5
        loss_ref[...] += jnp.broadcast_to(part, loss_ref.shape)
    return _row_call(name, body, [y, target], [], [(D_MODEL, F32)], [(128, F32)])


def _lane_scan(v, reverse):
    s = v.shape[1]
    lane = lax.broadcasted_iota(jnp.int32, v.shape, 1)
    d = 1
    while d < s:
        if reverse:
            v = v + jnp.where(lane < s - d, pltpu.roll(v, s - d, 1), 0.0)
        else:
            v = v + jnp.where(lane >= d, pltpu.roll(v, d, 1), 0.0)
        d *= 2
    return v


def _fox_gate(flt, b_f, name):
    def body(f_ref, b_ref, cum_ref):
        z = f_ref[...] + b_ref[...]
        cum_ref[...] = _lane_scan(-_softplus(-z), reverse=False)
    return pl.pallas_call(body, name=name, out_shape=jax.ShapeDtypeStruct(flt.shape, F32),
                          compiler_params=_cparams())(flt, b_f)


def _fox_gate_bwd(dcum_q, dcum_k, flt, b_f, name):
    def body(dq_ref, dk_ref, f_ref, b_ref, df_ref, db_ref):
        z = f_ref[...] + b_ref[...]
        df = _lane_scan(dq_ref[...] + dk_ref[...], reverse=True) * _sigmoid(-z)
        df_ref[...] = df
        db_ref[...] = jnp.sum(df, axis=1, keepdims=True)
    h = flt.shape[0]
    return pl.pallas_call(body, name=name,
                          out_shape=(jax.ShapeDtypeStruct(flt.shape, F32), jax.ShapeDtypeStruct((h, 1), F32)),
                          compiler_params=_cparams())(dcum_q, dcum_k, flt, b_f)


def _pick_head(block, h):
    lane = lax.broadcasted_iota(jnp.int32, block.shape, 1)
    return jnp.sum(jnp.where(lane == h, block, 0.0), axis=1, keepdims=True)


def _put_head(ref, col, h):
    @pl.when(h == 0)
    def _():
        ref[...] = jnp.zeros_like(ref)
    lane = lax.broadcasted_iota(jnp.int32, ref.shape, 1)
    ref[...] = jnp.where(lane == h, col, ref[...])


_NT = (((1,), (1,)), ((), ()))
_FOX_SCALE = FOX_HEAD_DIM ** -0.5


def _causal(s_tile, t):
    row = lax.broadcasted_iota(jnp.int32, (t, t), 0)
    col = lax.broadcasted_iota(jnp.int32, (t, t), 1)
    return jnp.where(col <= row, s_tile, -jnp.inf)


def _fox_attn_fwd(q, k, v, cum, cum_t, name):
    nh, s, dh = q.shape
    t = min(ATT_TILE, s)

    def body(q_ref, k_ref, v_ref, cum_ref, cumt_ref, o_ref, lse_ref):
        i = pl.program_id(0)
        h = pl.program_id(1)
        qv = q_ref[...]
        cq = _pick_head(cum_ref[...], h)

        def step(j, carry, masked):
            m, l, acc = carry
            ks = pl.ds(pl.multiple_of(j * t, t), t)
            sc = lax.dot_general(qv, k_ref[ks, :], _NT, preferred_element_type=F32) * _FOX_SCALE
            sc = sc + cq - cumt_ref[:, ks]
            if masked:
                sc = _causal(sc, t)
            m_new = jnp.maximum(m, jnp.max(sc, axis=1, keepdims=True))
            alpha = jnp.exp(m - m_new)
            p = jnp.exp(sc - m_new)
            l = alpha * l + jnp.sum(p, axis=1, keepdims=True)
            acc = alpha * acc + jnp.dot(p.astype(BF16), v_ref[ks, :], preferred_element_type=F32)
            return m_new, l, acc

        init = (jnp.full((t, 1), -jnp.inf, F32), jnp.zeros((t, 1), F32), jnp.zeros((t, dh), F32))
        carry = lax.fori_loop(0, i, lambda j, c: step(j, c, False), init)
        m, l, acc = step(i, carry, True)
        o_ref[...] = acc / l
        _put_head(lse_ref, m + jnp.log(l), h)

    return pl.pallas_call(
        body, name=name, grid=(s // t, nh),
        in_specs=[pl.BlockSpec((None, t, dh), lambda i, h: (h, i, 0)),
                  pl.BlockSpec((None, s, dh), lambda i, h: (h, 0, 0)),
                  pl.BlockSpec((None, s, dh), lambda i, h: (h, 0, 0)),
                  pl.BlockSpec((t, nh), lambda i, h: (i, 0)),
                  pl.BlockSpec((None, 1, s), lambda i, h: (h, 0, 0))],
        out_specs=[pl.BlockSpec((None, t, dh), lambda i, h: (h, i, 0)),
                   pl.BlockSpec((t, nh), lambda i, h: (i, 0))],
        out_shape=[jax.ShapeDtypeStruct((nh, s, dh), F32), jax.ShapeDtypeStruct((s, nh), F32)],
        compiler_params=_cparams(("arbitrary", "arbitrary")),
    )(q, k, v, cum, cum_t)


def _fox_attn_dq(q, k, v, do, o, cum, cum_t, lse, name):
    nh, s, dh = q.shape
    t = min(ATT_TILE, s)

    def body(q_ref, k_ref, v_ref, do_ref, o_ref, cum_ref, cumt_ref, lse_ref, dq_ref, delta_ref, dcq_ref):
        i = pl.program_id(0)
        h = pl.program_id(1)
        qv = q_ref[...]
        do_v = do_ref[...]
        do_b = do_v.astype(BF16)
        cq = _pick_head(cum_ref[...], h)
        lse_c = _pick_head(lse_ref[...], h)
        delta = jnp.sum(do_v * o_ref[...], axis=1, keepdims=True)

        def step(j, carry, masked):
            dq, rs = carry
            ks = pl.ds(pl.multiple_of(j * t, t), t)
            kj = k_ref[ks, :]
            sc = lax.dot_general(qv, kj, _NT, preferred_element_type=F32) * _FOX_SCALE
            sc = sc + cq - cumt_ref[:, ks]
            if masked:
                sc = _causal(sc, t)
            p = jnp.exp(sc - lse_c)
            dp = lax.dot_general(do_b, v_ref[ks, :], _NT, preferred_element_type=F32)
            ds = p * (dp - delta)
            return (dq + jnp.dot(ds.astype(BF16), kj, preferred_element_type=F32),
                    rs + jnp.sum(ds, axis=1, keepdims=True))

        carry = lax.fori_loop(0, i, lambda j, c: step(j, c, False), (jnp.zeros((t, dh), F32), jnp.zeros((t, 1), F32)))
        dq, rs = step(i, carry, True)
        dq_ref[...] = dq * _FOX_SCALE
        _put_head(delta_ref, delta, h)
        _put_head(dcq_ref, rs, h)

    head_tile = pl.BlockSpec((None, t, dh), lambda i, h: (h, i, 0))
    head_full = pl.BlockSpec((None, s, dh), lambda i, h: (h, 0, 0))
    nat_tile = pl.BlockSpec((t, nh), lambda i, h: (i, 0))
    return pl.pallas_call(
        body, name=name, grid=(s // t, nh),
        in_specs=[head_tile, head_full, head_full, head_tile, head_tile, nat_tile,
                  pl.BlockSpec((None, 1, s), lambda i, h: (h, 0, 0)), nat_tile],
        out_specs=[head_tile, nat_tile, nat_tile],
        out_shape=[jax.ShapeDtypeStruct((nh, s, dh), F32), jax.ShapeDtypeStruct((s, nh), F32),
                   jax.ShapeDtypeStruct((s, nh), F32)],
        compiler_params=_cparams(("arbitrary", "arbitrary")),
    )(q, k, v, do, o, cum, cum_t, lse)


def _fox_attn_dkv(q, k, v, do, cum, cum_t, lse_t, delta_t, name):
    nh, s, dh = q.shape
    t = min(ATT_TILE, s)
    nq = s // t

    def body(q_ref, k_ref, v_ref, do_ref, cum_ref, cumt_ref, lset_ref, deltat_ref, dk_ref, dv_ref, dcum_ref):
        j = pl.program_id(0)
        h = pl.program_id(1)
        kv = k_ref[...]
        vv = v_ref[...]
        ck = _pick_head(cum_ref[...], h)

        def step(i, carry, masked):
            dk, dv, dck = carry
            qs = pl.ds(pl.multiple_of(i * t, t), t)
            qi = q_ref[qs, :]
            do_i = do_ref[qs, :]
            st = lax.dot_general(kv, qi, _NT, preferred_element_type=F32) * _FOX_SCALE
            st = st + cumt_ref[:, qs] - ck
            if masked:
                row = lax.broadcasted_iota(jnp.int32, (t, t), 0)
                col = lax.broadcasted_iota(jnp.int32, (t, t), 1)
                st = jnp.where(row <= col, st, -jnp.inf)
            pt = jnp.exp(st - lset_ref[:, qs])
            dv = dv + jnp.dot(pt.astype(BF16), do_i, preferred_element_type=F32)
            dpt = lax.dot_general(vv, do_i, _NT, preferred_element_type=F32)
            dst = pt * (dpt - deltat_ref[:, qs])
            dk = dk + jnp.dot(dst.astype(BF16), qi, preferred_element_type=F32)
            dck = dck - jnp.sum(dst, axis=1, keepdims=True)
            return dk, dv, dck

        init = (jnp.zeros((t, dh), F32), jnp.zeros((t, dh), F32), jnp.zeros((t, 1), F32))
        carry = step(j, init, True)
        dk, dv, dck = lax.fori_loop(j + 1, nq, lambda i, c: step(i, c, False), carry)
        dk_ref[...] = dk * _FOX_SCALE
        dv_ref[...] = dv
        _put_head(dcum_ref, dck, h)

    head_tile = pl.BlockSpec((None, t, dh), lambda j, h: (h, j, 0))
    head_full = pl.BlockSpec((None, s, dh), lambda j, h: (h, 0, 0))
    row_full = pl.BlockSpec((None, 1, s), lambda j, h: (h, 0, 0))
    nat_tile = pl.BlockSpec((t, nh), lambda j, h: (j, 0))
    return pl.pallas_call(
        body, name=name, grid=(nq, nh),
        in_specs=[head_full, head_tile, head_tile, head_full, nat_tile, row_full, row_full, row_full],
        out_specs=[head_tile, head_tile, nat_tile],
        out_shape=[jax.ShapeDtypeStruct((nh, s, dh), F32), jax.ShapeDtypeStruct((nh, s, dh), F32),
                   jax.ShapeDtypeStruct((s, nh), F32)],
        compiler_params=_cparams(("arbitrary", "arbitrary")),
    )(q, k, v, do, cum, cum_t, lse_t, delta_t)


def _shift_down(v, d):
    row = lax.broadcasted_iota(jnp.int32, v.shape, 0)
    return jnp.where(row >= d, pltpu.roll(v, d, 0), 0.0)


def _shift_up(v, d):
    s = v.shape[0]
    row = lax.broadcasted_iota(jnp.int32, v.shape, 0)
    return jnp.where(row < s - d, pltpu.roll(v, s - d, 0), 0.0)


def _conv_taps(v, cw_ref, width):
    out = cw_ref[width - 1:width, :] * v
    for k in range(width - 1):
        out = out + cw_ref[k:k + 1, :] * _shift_down(v, width - 1 - k)
    return out


def _conv_taps_bwd(dout, v, cw_ref, dcw_ref, width):
    dv = cw_ref[width - 1:width, :] * dout
    dcw_ref[width - 1:width, :] = _rows_sum(dout * v)
    for k in range(width - 1):
        d = width - 1 - k
        dv = dv + cw_ref[k:k + 1, :] * _shift_up(dout, d)
        dcw_ref[k:k + 1, :] = _rows_sum(dout * _shift_down(v, d))
    return dv


def _col_spec(s, tc, part=0):
    off = part * (D_MODEL // tc)
    return pl.BlockSpec((s, tc), lambda c: (0, c + off))


def _small_spec(rows, tc):
    return pl.BlockSpec((rows, tc), lambda c: (0, c))


def _col_call(name, body, in_arrays, in_specs, out_rows, s, tc):
    return pl.pallas_call(
        body, name=name, grid=(D_MODEL // tc,), in_specs=in_specs,
        out_specs=[pl.BlockSpec((r, tc), lambda c: (0, c)) for r, _ in out_rows],
        out_shape=[jax.ShapeDtypeStruct((r, D_MODEL), dt) for r, dt in out_rows],
        compiler_params=_cparams(("arbitrary",)),
    )(*in_arrays)


def _sconv_fwd(proj, conv_w, name):
    s = proj.shape[0]
    tc = COL_TILE

    def body(b_ref, c_ref, x_ref, cw_ref, y_ref):
        y_ref[...] = (b_ref[...] * _conv_taps(c_ref[...] * x_ref[...], cw_ref, 3)).astype(BF16)

    return _col_call(name, body, [proj, proj, proj, conv_w],
                     [_col_spec(s, tc, 0), _col_spec(s, tc, 1), _col_spec(s, tc, 2), _small_spec(3, tc)],
                     [(s, BF16)], s, tc)[0]


def _sconv_bwd(dy, proj, conv_w, name):
    s = proj.shape[0]
    tc = COL_TILE

    def body(dy_ref, b_ref, c_ref, x_ref, cw_ref, db_ref, dc_ref, dx_ref, dcw_ref):
        w = c_ref[...] * x_ref[...]
        dy_v = dy_ref[...]
        db_ref[...] = (dy_v * _conv_taps(w, cw_ref, 3)).astype(BF16)
        dw = _conv_taps_bwd(dy_v * b_ref[...], w, cw_ref, dcw_ref, 3)
        dc_ref[...] = (dw * x_ref[...]).astype(BF16)
        dx_ref[...] = (dw * c_ref[...]).astype(BF16)

    return _col_call(name, body, [dy, proj, proj, proj, conv_w],
                     [_col_spec(s, tc), _col_spec(s, tc, 0), _col_spec(s, tc, 1), _col_spec(s, tc, 2),
                      _small_spec(3, tc)],
                     [(s, BF16), (s, BF16), (s, BF16), (3, F32)], s, tc)


def _lru_conv(proj, conv_w, conv_b, name):
    s = proj.shape[0]
    tc = COL_TILE

    def body(x_ref, cw_ref, cb_ref, xb_ref, xbb_ref):
        xb = _conv_taps(x_ref[...], cw_ref, 4) + cb_ref[...]
        xb_ref[...] = xb
        xbb_ref[...] = xb.astype(BF16)

    return _col_call(name, body, [proj, conv_w, conv_b],
                     [_col_spec(s, tc, 1), _small_spec(4, tc), _small_spec(1, tc)],
                     [(s, F32), (s, BF16)], s, tc)


def _lru_conv_bwd(dxb1, dxb2, proj, conv_w, name):
    s = proj.shape[0]
    tc = COL_TILE

    def body(d1_ref, d2_ref, x_ref, cw_ref, dx_ref, dcw_ref, dcb_ref):
        dxb = d1_ref[...] + d2_ref[...]
        dcb_ref[...] = _rows_sum(dxb)
        dx_ref[...] = _conv_taps_bwd(dxb, x_ref[...], cw_ref, dcw_ref, 4).astype(BF16)

    return _col_call(name, body, [dxb1, dxb2, proj, conv_w],
                     [_col_spec(s, tc), _col_spec(s, tc), _col_spec(s, tc, 1), _small_spec(4, tc)],
                     [(s, BF16), (4, F32), (1, F32)], s, tc)


_GELU_C = math.sqrt(2.0 / math.pi)


def _gelu_parts(g):
    inner = _GELU_C * (g + 0.044715 * g * g * g)
    th = jnp.tanh(inner)
    val = 0.5 * g * (1.0 + th)
    der = 0.5 * (1.0 + th) + 0.5 * g * (1.0 - th * th) * (_GELU_C * (1.0 + 3.0 * 0.044715 * g * g))
    return val, der


def _lru_gates(pa_ref, px_ref, ba_ref, bx_ref, lam_ref):
    r = _sigmoid(pa_ref[...] + ba_ref[...])
    ig = _sigmoid(px_ref[...] + bx_ref[...])
    sp = _softplus(-lam_ref[...])
    log_a = (-LRU_C) * r * sp
    a = jnp.exp(log_a)
    z = 2.0 * log_a
    one_m_a2 = jnp.where(z > -1e-3, -(z * (1.0 + z * (0.5 + z * (1.0 / 6.0)))), 1.0 - jnp.exp(z))
    return r, ig, sp, a, jnp.sqrt(one_m_a2)


def _lru_scan(pre, xb, proj, b_a, b_x, lam, name):
    s = xb.shape[0]
    tc = COL_TILE

    def body(pa_ref, px_ref, xb_ref, g_ref, ba_ref, bx_ref, lam_ref, y_ref, hs_ref):
        _, ig, _, a, mult = _lru_gates(pa_ref, px_ref, ba_ref, bx_ref, lam_ref)
        b = mult * (ig * xb_ref[...])
        d = 1
        while d < s:
            row = lax.broadcasted_iota(jnp.int32, a.shape, 0)
            keep = row >= d
            b = b + a * jnp.where(keep, pltpu.roll(b, d, 0), 0.0)
            a = a * jnp.where(keep, pltpu.roll(a, d, 0), 1.0)
            d *= 2
        hs_ref[...] = b
        y_ref[...] = (b * _gelu_parts(g_ref[...])[0]).astype(BF16)

    return _col_call(name, body, [pre, pre, xb, proj, b_a, b_x, lam],
                     [_col_spec(s, tc, 0), _col_spec(s, tc, 1), _col_spec(s, tc), _col_spec(s, tc, 0),
                      _small_spec(1, tc), _small_spec(1, tc), _small_spec(1, tc)],
                     [(s, BF16), (s, F32)], s, tc)


def _lru_scan_bwd(dy, pre, xb, proj, hs, b_a, b_x, lam, name):
    s = xb.shape[0]
    tc = COL_TILE

    def body(dy_ref, pa_ref, px_ref, xb_ref, g_ref, hs_ref, ba_ref, bx_ref, lam_ref,
             dg_ref, dpa_ref, dpx_ref, dxb_ref, dba_ref, dbx_ref, dlam_ref):
        r, ig, sp, a, mult = _lru_gates(pa_ref, px_ref, ba_ref, bx_ref, lam_ref)
        gl, gl_der = _gelu_parts(g_ref[...])
        dy_v = dy_ref[...]
        hs_v = hs_ref[...]
        dg_ref[...] = (dy_v * hs_v * gl_der).astype(BF16)
        lam_t = dy_v * gl
        coef = _shift_up(a, 1)
        d = 1
        while d < s:
            row = lax.broadcasted_iota(jnp.int32, coef.shape, 0)
            keep = row < s - d
            lam_t = lam_t + coef * jnp.where(keep, pltpu.roll(lam_t, s - d, 0), 0.0)
            coef = coef * jnp.where(keep, pltpu.roll(coef, s - d, 0), 1.0)
            d *= 2
        xb_v = xb_ref[...]
        da = lam_t * _shift_down(hs_v, 1)
        dmult = lam_t * (ig * xb_v)
        dig = lam_t * mult * xb_v
        dxb_ref[...] = lam_t * mult * ig
        dlog_a = da * a - dmult * (a * a) / mult
        dr = dlog_a * ((-LRU_C) * sp)
        dsp = _rows_sum(dlog_a * ((-LRU_C) * r))
        dlam_ref[...] = -dsp * _sigmoid(-lam_ref[...])
        dpa = dr * r * (1.0 - r)
        dpx = dig * ig * (1.0 - ig)
        dba_ref[...] = _rows_sum(dpa)
        dbx_ref[...] = _rows_sum(dpx)
        dpa_ref[...] = dpa.astype(BF16)
        dpx_ref[...] = dpx.astype(BF16)

    return _col_call(name, body, [dy, pre, pre, xb, proj, hs, b_a, b_x, lam],
                     [_col_spec(s, tc), _col_spec(s, tc, 0), _col_spec(s, tc, 1), _col_spec(s, tc),
                      _col_spec(s, tc, 0), _col_spec(s, tc),
                      _small_spec(1, tc), _small_spec(1, tc), _small_spec(1, tc)],
                     [(s, BF16), (s, BF16), (s, BF16), (s, F32), (1, F32), (1, F32), (1, F32)], s, tc)


def _to_heads(t, dtype):
    s = t.shape[0]
    return t.reshape(s, FOX_HEADS, FOX_HEAD_DIM).transpose(1, 0, 2).astype(dtype)


def _from_heads(t):
    s = t.shape[1]
    return t.transpose(1, 0, 2).reshape(s, D_MODEL)


def _ffn_fwd(x, w_in, w_out, g_pre, g_post, shift, scale, gate, tag):
    h = _pre_norm(x, g_pre, scale, shift, tag + "_pre")
    gu = _mm_nn(h, w_in, tag + "_in")
    a = _swiglu_act(gu, tag + "_act")
    y = _mm_nn(a, w_out, tag + "_out", tn=512)
    xo = _post_norm(x, y, g_post, gate, 0.5, tag + "_post")
    return xo, (x, h, gu, a, y)


def _ffn_bwd(dxo, saved, w_in, w_out, g_pre, g_post, scale, gate, tag):
    x, h, gu, a, y = saved
    dy, dgate, dg_post = _post_norm_bwd(dxo, y, g_post, gate, 0.5, tag + "_post_b")
    da = _mm_nt(dy, w_out, tag + "_out_bx", tk=D_FF // 2)
    dw_out = _mm_tn(a, dy, tag + "_out_bw", tk=D_FF // 2)
    dgu = _swiglu_act_bwd(da, gu, tag + "_act_b")
    dh = _mm_nt(dgu, w_in, tag + "_in_bx")
    dw_in = _mm_tn(h, dgu, tag + "_in_bw", tn=w_in.arr.shape[-1], blocked_out=True)
    dx, dshift, dscale, dg_pre = _pre_norm_bwd(dxo, dh, x, g_pre, scale, tag + "_pre_b")
    return dx, dw_in, dw_out, (dshift, dscale, dgate), dg_pre, dg_post


def _fox_fwd(h, p, tag):
    s = h.shape[0]
    proj = _mm_nn(h, p["w_in"], tag + "_in", tn=640)
    qh = _to_heads(proj[:, :D_MODEL], BF16)
    kh = _to_heads(proj[:, D_MODEL:2 * D_MODEL], BF16)
    vh = _to_heads(proj[:, 2 * D_MODEL:3 * D_MODEL], BF16)
    flt = proj[:, 3 * D_MODEL:3 * D_MODEL + FOX_HEADS].T
    cum_t = _fox_gate(flt, p["b_f"], tag + "_gate")
    cum = cum_t.T
    cum_t3 = cum_t[:, None, :]
    oh, lse = _fox_attn_fwd(qh, kh, vh, cum, cum_t3, tag + "_attn")
    o = _from_heads(oh).astype(BF16)
    y = _mm_nn(o, p["w_out"], tag + "_out")
    return y, (qh, kh, vh, flt, cum, cum_t3, oh, lse, o)


def _fox_bwd(dy, h, saved, p, tag):
    qh, kh, vh, flt, cum, cum_t3, oh, lse, o = saved
    s = h.shape[0]
    do = _mm_nt(dy, p["w_out"], tag + "_out_bx")
    dw_out = _mm_tn(o, dy, tag + "_out_bw")
    doh = _to_heads(do, F32)
    dqh, delta, dcum_q = _fox_attn_dq(qh, kh, vh, doh, oh, cum, cum_t3, lse, tag + "_attn_dq")
    dkh, dvh, dcum_k = _fox_attn_dkv(qh, kh, vh, doh.astype(BF16), cum, cum_t3, lse.T[:, None, :],
                                     delta.T[:, None, :], tag + "_attn_dkv")
    dflt, db_f = _fox_gate_bwd(dcum_q.T, dcum_k.T, flt, p["b_f"], tag + "_gate_b")
    dproj = jnp.concatenate(
        [_from_heads(dqh), _from_heads(dkh), _from_heads(dvh), dflt.T,
         jnp.zeros((s, FOX_PAD - 3 * D_MODEL - FOX_HEADS), F32)], axis=1).astype(BF16)
    dh = _mm_nt(dproj, p["w_in"], tag + "_in_bx", tn=640)
    dw_in = _mm_tn(h, dproj, tag + "_in_bw", tn=640)
    return dh, {"w_in": dw_in, "w_out": dw_out, "b_f": db_f}


def _sconv_mix_fwd(h, p, tag):
    proj = _mm_nn(h, p["w_in"], tag + "_in")
    yb = _sconv_fwd(proj, p["conv_w"], tag + "_conv")
    y = _mm_nn(yb, p["w_out"], tag + "_out")
    return y, (proj, yb)


def _sconv_mix_bwd(dy, h, saved, p, tag):
    proj, yb = saved
    dyb = _mm_nt(dy, p["w_out"], tag + "_out_bx")
    dw_out = _mm_tn(yb, dy, tag + "_out_bw")
    db, dc, dxv, dcw = _sconv_bwd(dyb, proj, p["conv_w"], tag + "_conv_b")
    dproj = jnp.concatenate([db, dc, dxv], axis=1)
    dh = _mm_nt(dproj, p["w_in"], tag + "_in_bx")
    dw_in = _mm_tn(h, dproj, tag + "_in_bw", tn=p["w_in"].arr.shape[-1], blocked_out=True)
    return dh, {"w_in": dw_in, "w_out": dw_out, "conv_w": dcw}


def _lru_mix_fwd(h, p, tag):
    proj = _mm_nn(h, p["w_in"], tag + "_in")
    xb, xbb = _lru_conv(proj, p["conv_w"], p["conv_b"], tag + "_conv")
    pre = _mm_nn(xbb, p["w_ax"], tag + "_gates", tn=D_MODEL)
    yb, hs = _lru_scan(pre, xb, proj, p["b_a"], p["b_x"], p["lam"], tag + "_scan")
    y = _mm_nn(yb, p["w_out"], tag + "_out")
    return y, (proj, xb, xbb, pre, yb, hs)


def _diag_blocks(m):
    return jnp.stack([m[LRU_BLOCK_DIM * n:LRU_BLOCK_DIM * (n + 1), LRU_BLOCK_DIM * n:LRU_BLOCK_DIM * (n + 1)]
                      for n in range(LRU_BLOCKS)])


def _lru_mix_bwd(dy, h, saved, p, tag):
    proj, xb, xbb, pre, yb, hs = saved
    dyb = _mm_nt(dy, p["w_out"], tag + "_out_bx")
    dw_out = _mm_tn(yb, dy, tag + "_out_bw")
    dg, dpa, dpx, dxb1, dba, dbx, dlam = _lru_scan_bwd(dyb, pre, xb, proj, hs, p["b_a"], p["b_x"], p["lam"],
                                                       tag + "_scan_b")
    dpre = jnp.concatenate([dpa, dpx], axis=1)
    dxb2 = _mm_nt(dpre, p["w_ax"], tag + "_gates_bx", tn=D_MODEL)
    dw_ax = _mm_tn(xbb, dpre, tag + "_gates_bw", tn=D_MODEL)
    dx0, dcw, dcb = _lru_conv_bwd(dxb1, dxb2, proj, p["conv_w"], tag + "_conv_b")
    dproj = jnp.concatenate([dg, dx0], axis=1)
    dh = _mm_nt(dproj, p["w_in"], tag + "_in_bx")
    dw_in = _mm_tn(h, dproj, tag + "_in_bw", tn=p["w_in"].arr.shape[-1], blocked_out=True)
    grads = {"w_in": dw_in, "w_out": dw_out, "conv_w": dcw, "conv_b": dcb,
             "w_a": _diag_blocks(dw_ax[:, :D_MODEL]), "w_x": _diag_blocks(dw_ax[:, D_MODEL:]),
             "b_a": dba, "b_x": dbx, "lam": dlam}
    return dh, grads


_MIXERS = ((_fox_fwd, _fox_bwd), (_sconv_mix_fwd, _sconv_mix_bwd), (_lru_mix_fwd, _lru_mix_bwd))


def _local_step(x, target, mod, params):
    layers = params["layers"]
    tape = []
    for i, lp in enumerate(layers):
        row = lambda v: v[None, :]
        m = lambda sub, what: mod[i, sub, what][None, :]
        x, sv0 = _ffn_fwd(x, lp["ffn_in"][0], lp["ffn_out"][0], row(lp["norm_pre"][0]), row(lp["norm_post"][0]),
                          m(0, 0), m(0, 1), m(0, 2), f"l{i}_ffn0")
        h = _pre_norm(x, row(lp["norm_pre"][1]), m(1, 1), m(1, 0), f"l{i}_mix_pre")
        y, svm = _MIXERS[i % 3][0](h, lp["mixer"], f"l{i}_mix")
        x1 = _post_norm(x, y, row(lp["norm_post"][1]), m(1, 2), 1.0, f"l{i}_mix_post")
        x2, sv2 = _ffn_fwd(x1, lp["ffn_in"][1], lp["ffn_out"][1], row(lp["norm_pre"][2]), row(lp["norm_post"][2]),
                           m(2, 0), m(2, 1), m(2, 2), f"l{i}_ffn1")
        tape.append((sv0, (x, h, y, svm), sv2))
        x = x2
    dx, loss_row = _loss_head(x, target, "loss_head")

    layer_grads = [None] * len(layers)
    dmod = [None] * len(layers)
    for i in reversed(range(len(layers))):
        lp = layers[i]
        row = lambda v: v[None, :]
        m = lambda sub, what: mod[i, sub, what][None, :]
        sv0, (xm, h, y, svm), sv2 = tape[i]
        dx, dw_in1, dw_out1, dm2, dgp2, dgq2 = _ffn_bwd(dx, sv2, lp["ffn_in"][1], lp["ffn_out"][1],
                                                        row(lp["norm_pre"][2]), row(lp["norm_post"][2]),
                                                        m(2, 1), m(2, 2), f"l{i}_ffn1")
        dy, dgate1, dgq1 = _post_norm_bwd(dx, y, row(lp["norm_post"][1]), m(1, 2), 1.0, f"l{i}_mix_post_b")
        dh, mg = _MIXERS[i % 3][1](dy, h, svm, lp["mixer"], f"l{i}_mix")
        dx, dshift1, dscale1, dgp1 = _pre_norm_bwd(dx, dh, xm, row(lp["norm_pre"][1]), m(1, 1), f"l{i}_mix_pre_b")
        dx, dw_in0, dw_out0, dm0, dgp0, dgq0 = _ffn_bwd(dx, sv0, lp["ffn_in"][0], lp["ffn_out"][0],
                                                        row(lp["norm_pre"][0]), row(lp["norm_post"][0]),
                                                        m(0, 1), m(0, 2), f"l{i}_ffn0")
        dmod[i] = jnp.concatenate([*dm0, dshift1, dscale1, dgate1, *dm2], axis=0).reshape(N_SUB, 3, D_MODEL)
        layer_grads[i] = {"ffn_in": (dw_in0, dw_in1), "ffn_out": (dw_out0, dw_out1),
                          "norm_pre": jnp.concatenate([dgp0, dgp1, dgp2], axis=0),
                          "norm_post": jnp.concatenate([dgq0, dgq1, dgq2], axis=0), "mixer": mg}
    return loss_row, dx, jnp.stack(dmod), layer_grads


COND_ROWS = 16
COND_PAD = 128


def _cond_fwd(c_pad, w_cond, b_shard, name):
    nl, d, n = w_cond.shape
    tn = 768

    def body(c_ref, w_ref, b_ref, o_ref):
        cv = c_ref[...]
        act = (cv * _sigmoid(cv)).astype(BF16)
        o_ref[...] = jnp.dot(act, w_ref[...].astype(BF16), preferred_element_type=F32) + b_ref[...]

    return pl.pallas_call(
        body, name=name, grid=(nl, n // tn),
        in_specs=[pl.BlockSpec((COND_ROWS, d), lambda i, j: (0, 0)),
                  pl.BlockSpec((None, d, tn), lambda i, j: (i, 0, j)),
                  pl.BlockSpec((None, 1, tn), lambda i, j: (i, 0, j))],
        out_specs=pl.BlockSpec((None, COND_ROWS, tn), lambda i, j: (i, 0, j)),
        out_shape=jax.ShapeDtypeStruct((nl, COND_ROWS, n), F32),
        compiler_params=_cparams(("arbitrary", "arbitrary")),
    )(c_pad, w_cond, b_shard)


def _adam_math(w, g, m, v):
    nm = ADAM_B1 * m + (1.0 - ADAM_B1) * g
    nv = ADAM_B2 * v + (1.0 - ADAM_B2) * (g * g)
    m_hat = nm / (1.0 - ADAM_B1 ** ADAM_STEP)
    v_hat = nv / (1.0 - ADAM_B2 ** ADAM_STEP)
    delta = (-ADAM_LR) * (m_hat / (jnp.sqrt(v_hat) + ADAM_EPS) + ADAM_WD * w)
    return delta, nm, nv


def _cond_bwd_adamw(c_t, dmod_s, w, m, v, name):
    nl, d, n = w.shape
    tn = 384
    blk = pl.BlockSpec((None, d, tn), lambda i, j: (i, 0, j))

    def body(c_ref, dm_ref, w_ref, m_ref, v_ref, g_ref, d_ref, nm_ref, nv_ref):
        cv = c_ref[...]
        g = jnp.dot((cv * _sigmoid(cv)).astype(BF16), dm_ref[...], preferred_element_type=F32)
        g_ref[...] = g
        d_ref[...], nm_ref[...], nv_ref[...] = _adam_math(w_ref[...], g, m_ref[...], v_ref[...])

    return pl.pallas_call(
        body, name=name, grid=(nl, n // tn),
        in_specs=[pl.BlockSpec((d, COND_PAD), lambda i, j: (0, 0)),
                  pl.BlockSpec((None, COND_PAD, tn), lambda i, j: (i, 0, j)), blk, blk, blk],
        out_specs=[blk] * 4, out_shape=[jax.ShapeDtypeStruct(w.shape, F32)] * 4,
        compiler_params=_cparams(("arbitrary", "arbitrary")),
    )(c_t, dmod_s, w, m, v)


def _adamw(w, g, m, v, name):
    rows, cols = w.shape
    tr = next(t for t in (256, 176, 128, 64, 32, 16, 8) if rows % t == 0)
    blk = pl.BlockSpec((tr, cols), lambda i: (i, 0))

    def body(w_ref, g_ref, m_ref, v_ref, d_ref, nm_ref, nv_ref):
        d_ref[...], nm_ref[...], nv_ref[...] = _adam_math(w_ref[...], g_ref[...], m_ref[...], v_ref[...])

    return pl.pallas_call(
        body, name=name, grid=(rows // tr,), in_specs=[blk] * 4, out_specs=[blk] * 3,
        out_shape=[jax.ShapeDtypeStruct(w.shape, F32)] * 3, compiler_params=_cparams(("arbitrary",)),
    )(w, g, m, v)


_MESH = pl.DeviceIdType.MESH
_ANY = pl.BlockSpec(memory_space=pl.ANY)


def _place():
    return lax.axis_index("x"), lax.axis_index("y"), lax.axis_index("c")


def _other_chips(x, y):
    return [(1 - x, y), (x, 1 - y), (1 - x, 1 - y)]


def _allgather8(block, name):
    m_per, n = block.shape

    def body(x_ref, out_ref, send_sems, recv_sems, local_sem):
        x, y, c = _place()
        me, sibling = (x, y, c), (x, y, 1 - c)
        chips = _other_chips(x, y)

        def rows(px, py, pc):
            return out_ref.at[pl.ds((4 * px + 2 * py + pc) * m_per, m_per), :]

        def copy(k, blk, to, src=None):
            return pltpu.make_async_remote_copy(
                src_ref=rows(*blk) if src is None else src, dst_ref=rows(*blk),
                send_sem=send_sems.at[k], recv_sem=recv_sems.at[k], device_id=to, device_id_type=_MESH)

        mine = pltpu.make_async_copy(x_ref, rows(*me), local_sem)
        mine.start()
        first = [copy(0, me, sibling, src=x_ref)]
        first += [copy(1 + j, me, (*chip, c), src=x_ref) for j, chip in enumerate(chips)]
        for cp in first:
            cp.start()
        passed = [copy(4 + j, (*chip, c), sibling) for j, chip in enumerate(chips)]
        for j, chip in enumerate(chips):
            copy(1 + j, (*chip, c), me).wait_recv()
            passed[j].start()
        copy(0, sibling, me).wait_recv()
        for j, chip in enumerate(chips):
            copy(4 + j, (*chip, 1 - c), me).wait_recv()
        for cp in first + passed:
            cp.wait_send()
        mine.wait()

    return pl.pallas_call(
        body, name=name, out_shape=jax.ShapeDtypeStruct((N_DEV * m_per, n), block.dtype),
        in_specs=[pl.BlockSpec(memory_space=pltpu.VMEM)], out_specs=pl.BlockSpec(memory_space=pltpu.VMEM),
        scratch_shapes=[pltpu.SemaphoreType.DMA((7,)), pltpu.SemaphoreType.DMA((7,)), pltpu.SemaphoreType.DMA],
        compiler_params=_cparams(),
    )(block)


def _split_axis(shape):
    return next(a for a, n in enumerate(shape) if n > 1)


def _weights_allgather(shards, chip_axes, name):
    nt = len(shards)
    cut = [_split_axis(s.shape) for s in shards]
    out_shapes = [s.shape[:a] + (N_CHIPS,) + s.shape[a:] for s, a in zip(shards, chip_axes)]

    def body(*refs):
        ins, outs = refs[:nt], refs[nt:2 * nt]
        send_sems, recv_sems, local_sems = refs[2 * nt:]
        x, y, c = _place()
        me, sibling = (x, y, c), (x, y, 1 - c)
        chips = _other_chips(x, y)

        def half_of_shard(t, pc):
            n = shards[t].shape[cut[t]] // 2
            idx = [slice(None)] * shards[t].ndim
            idx[cut[t]] = pl.ds(pc * n, n)
            return ins[t].at[tuple(idx)]

        def place(t, px, py, pc):
            n = shards[t].shape[cut[t]] // 2
            idx = [slice(None)] * shards[t].ndim
            idx[cut[t]] = pl.ds(pc * n, n)
            idx.insert(chip_axes[t], 2 * px + py)
            return outs[t].at[tuple(idx)]

        def copy(t, k, blk, to, src=None):
            return pltpu.make_async_remote_copy(
                src_ref=place(t, *blk) if src is None else src, dst_ref=place(t, *blk),
                send_sem=send_sems.at[7 * t + k], recv_sem=recv_sems.at[7 * t + k],
                device_id=to, device_id_type=_MESH)

        mine = [pltpu.make_async_copy(half_of_shard(t, c), place(t, *me), local_sems.at[t]) for t in range(nt)]
        for cp in mine:
            cp.start()
        first = []
        for t in range(nt):
            src = half_of_shard(t, c)
            first.append(copy(t, 0, me, sibling, src=src))
            first += [copy(t, 1 + j, me, (*chip, c), src=src) for j, chip in enumerate(chips)]
        for cp in first:
            cp.start()
        passed = []
        for t in range(nt):
            for j, chip in enumerate(chips):
                copy(t, 1 + j, (*chip, c), me).wait_recv()
                fwd = copy(t, 4 + j, (*chip, c), sibling)
                fwd.start()
                passed.append(fwd)
        for t in range(nt):
            copy(t, 0, sibling, me).wait_recv()
            for j, chip in enumerate(chips):
                copy(t, 4 + j, (*chip, 1 - c), me).wait_recv()
        for cp in first + passed:
            cp.wait_send()
        for cp in mine:
            cp.wait()

    return pl.pallas_call(
        body, name=name, out_shape=[jax.ShapeDtypeStruct(s, t.dtype) for s, t in zip(out_shapes, shards)],
        in_specs=[_ANY] * nt, out_specs=[_ANY] * nt,
        scratch_shapes=[pltpu.SemaphoreType.DMA((7 * nt,)), pltpu.SemaphoreType.DMA((7 * nt,)),
                        pltpu.SemaphoreType.DMA((nt,))],
        compiler_params=_cparams(),
    )(*shards)


def _pair_exchange(grads, name):
    nt = len(grads)

    def body(*refs):
        ins, outs = refs[:nt], refs[nt:2 * nt]
        send_sems, recv_sems = refs[2 * nt:]
        x, y, c = _place()
        copies = []
        for t in range(nt):
            h = grads[t].shape[1] // 2
            copies.append(pltpu.make_async_remote_copy(
                src_ref=ins[t].at[:, pl.ds((1 - c) * h, h), :], dst_ref=outs[t],
                send_sem=send_sems.at[t], recv_sem=recv_sems.at[t], device_id=(x, y, 1 - c), device_id_type=_MESH))
        for cp in copies:
            cp.start()
        for cp in copies:
            cp.wait()

    return pl.pallas_call(
        body, name=name,
        out_shape=[jax.ShapeDtypeStruct((N_CHIPS, g.shape[1] // 2, g.shape[2]), g.dtype) for g in grads],
        in_specs=[_ANY] * nt, out_specs=[_ANY] * nt,
        scratch_shapes=[pltpu.SemaphoreType.DMA((nt,)), pltpu.SemaphoreType.DMA((nt,))],
        compiler_params=_cparams(),
    )(*grads)


def _pair_sum(own, recv, c_idx, name):
    _, h, cols = recv.shape

    def body(c_ref, own_ref, recv_ref, o_ref):
        o_ref[...] = (own_ref[...] + recv_ref[...]).astype(BF16)

    return pl.pallas_call(
        body, name=name,
        grid_spec=pltpu.PrefetchScalarGridSpec(
            num_scalar_prefetch=1, grid=(N_CHIPS,),
            in_specs=[pl.BlockSpec((None, h, cols), lambda k, c_ref: (k, c_ref[0], 0)),
                      pl.BlockSpec((None, h, cols), lambda k, c_ref: (k, 0, 0))],
            out_specs=pl.BlockSpec((None, h, cols), lambda k, c_ref: (k, 0, 0))),
        out_shape=jax.ShapeDtypeStruct(recv.shape, BF16), compiler_params=_cparams(("arbitrary",)),
    )(c_idx, own, recv)


def _chip_exchange(parts, name):
    nt = len(parts)

    def body(*refs):
        ins, outs = refs[:nt], refs[nt:2 * nt]
        send_sems, recv_sems, local_sems = refs[2 * nt:]
        x, y, c = _place()
        my_chip = 2 * x + y
        chips = _other_chips(x, y)
        local = [pltpu.make_async_copy(ins[t].at[my_chip], outs[t].at[my_chip], local_sems.at[t]) for t in range(nt)]
        for cp in local:
            cp.start()
        sends, recvs = [], []
        for t in range(nt):
            for j, (px, py) in enumerate(chips):
                k = 3 * t + j
                sends.append(pltpu.make_async_remote_copy(
                    src_ref=ins[t].at[2 * px + py], dst_ref=outs[t].at[my_chip],
                    send_sem=send_sems.at[k], recv_sem=recv_sems.at[k], device_id=(px, py, c), device_id_type=_MESH))
                recvs.append(pltpu.make_async_remote_copy(
                    src_ref=ins[t].at[my_chip], dst_ref=outs[t].at[2 * px + py],
                    send_sem=send_sems.at[k], recv_sem=recv_sems.at[k], device_id=(px, py, c), device_id_type=_MESH))
        for cp in sends:
            cp.start()
        for cp in recvs:
            cp.wait_recv()
        for cp in sends:
            cp.wait_send()
        for cp in local:
            cp.wait()

    return pl.pallas_call(
        body, name=name, out_shape=[jax.ShapeDtypeStruct(p.shape, p.dtype) for p in parts],
        in_specs=[_ANY] * nt, out_specs=[_ANY] * nt,
        scratch_shapes=[pltpu.SemaphoreType.DMA((3 * nt,)), pltpu.SemaphoreType.DMA((3 * nt,)),
                        pltpu.SemaphoreType.DMA((nt,))],
        compiler_params=_cparams(),
    )(*parts)


def _chip_sum(q, name):
    def body(q_ref, o_ref):
        acc = q_ref[0].astype(F32)
        for k in range(1, N_CHIPS):
            acc = acc + q_ref[k].astype(F32)
        o_ref[...] = acc
    return pl.pallas_call(body, name=name, out_shape=jax.ShapeDtypeStruct(q.shape[1:], F32),
                          compiler_params=_cparams())(q)


def _pair_gather(halves, homes, out_shapes, name):
    nt = len(halves)
    n_out = len(out_shapes)

    def body(*refs):
        ins, outs = refs[:nt], refs[nt:nt + n_out]
        send_sems, recv_sems, local_sems = refs[nt + n_out:]
        x, y, c = _place()

        def home(t, pc):
            o, lead = homes[t]
            h = halves[t].shape[0]
            return outs[o].at[(*lead, pl.ds(pc * h, h), slice(None))]

        local = [pltpu.make_async_copy(ins[t], home(t, c), local_sems.at[t]) for t in range(nt)]
        sends = [pltpu.make_async_remote_copy(src_ref=ins[t], dst_ref=home(t, c), send_sem=send_sems.at[t],
                                              recv_sem=recv_sems.at[t], device_id=(x, y, 1 - c), device_id_type=_MESH)
                 for t in range(nt)]
        recvs = [pltpu.make_async_remote_copy(src_ref=ins[t], dst_ref=home(t, 1 - c), send_sem=send_sems.at[t],
                                              recv_sem=recv_sems.at[t], device_id=(x, y, 1 - c), device_id_type=_MESH)
                 for t in range(nt)]
        for cp in local + sends:
            cp.start()
        for cp in recvs:
            cp.wait_recv()
        for cp in sends:
            cp.wait_send()
        for cp in local:
            cp.wait()

    return pl.pallas_call(
        body, name=name, out_shape=[jax.ShapeDtypeStruct(s, F32) for s in out_shapes],
        in_specs=[_ANY] * nt, out_specs=[_ANY] * n_out,
        scratch_shapes=[pltpu.SemaphoreType.DMA((nt,)), pltpu.SemaphoreType.DMA((nt,)),
                        pltpu.SemaphoreType.DMA((nt,))],
        compiler_params=_cparams(),
    )(*halves)


def _sum_devices(g, name):
    def body(g_ref, o_ref):
        acc = g_ref[0:1, :]
        for d in range(1, N_DEV):
            acc = acc + g_ref[d:d + 1, :]
        o_ref[...] = acc
    return pl.pallas_call(body, name=name, out_shape=jax.ShapeDtypeStruct((1, g.shape[1]), F32),
                          compiler_params=_cparams())(g)


_WEIGHTS = ("w_cond", "b_cond", "norm_pre", "norm_post", "w_ffn_in", "w_ffn_out", "fox_w_in", "fox_b_f",
            "fox_w_out", "sconv_w_in", "sconv_conv_w", "sconv_w_out", "lru_w_in", "lru_conv_w", "lru_conv_b",
            "lru_w_a", "lru_b_a", "lru_w_x", "lru_b_x", "lru_lambda", "lru_w_out")
_BIG = (("w_ffn_in", False), ("w_ffn_out", True), ("fox_w_in", False), ("fox_w_out", True),
        ("sconv_w_in", False), ("sconv_w_out", True), ("lru_w_in", False), ("lru_w_out", True))
_SMALL = tuple(n for n in _WEIGHTS if n != "w_cond" and n not in dict(_BIG))
_COL_SHARDED_SMALL = ("norm_pre", "norm_post", "sconv_conv_w", "lru_conv_w", "lru_conv_b", "lru_lambda")


def _pack_rows(parts, rows=8):
    flat = jnp.concatenate([p.reshape(-1) for p in parts])
    width = -(-flat.size // (rows * 128)) * 128
    return jnp.pad(flat, (0, rows * width - flat.size)).reshape(rows, width)


def _unpack(flat, shapes):
    out, off = [], 0
    for shp in shapes:
        n = math.prod(shp)
        out.append(flat[off:off + n].reshape(shp))
        off += n
    return out


def _join_chips(g):
    g = jnp.moveaxis(g, 0, -2)
    return g.reshape(g.shape[:-2] + (g.shape[-2] * g.shape[-1],))


def _my_columns(full, chip):
    n = full.shape[-1] // N_CHIPS
    return lax.dynamic_slice_in_dim(full, chip * n, n, axis=full.ndim - 1)


def _block_diag(w):
    eye = jnp.eye(LRU_BLOCKS, dtype=w.dtype)
    return jnp.einsum("nij,nm->nimj", w, eye).reshape(D_MODEL, D_MODEL)


def _step(x, c, target, wts, mom, var):
    ix, iy, ic = _place()
    chip = 2 * ix + iy
    dev = 2 * chip + ic
    n_cond = wts["w_cond"].shape[2]

    small_shapes = [(D_MODEL,)] + [wts[n].shape for n in _COL_SHARDED_SMALL]
    g1 = _allgather8(_pack_rows([c[0]] + [wts[n] for n in _COL_SHARDED_SMALL]), "gather_small").reshape(N_DEV, -1)
    c_all = g1[:, :D_MODEL]
    per_chip = [jnp.stack(col) for col in zip(*[_unpack(g1[2 * k], small_shapes) for k in range(N_CHIPS)])]
    small_full = {n: _join_chips(v) for n, v in zip(_COL_SHARDED_SMALL, per_chip[1:])}

    c_pad = jnp.pad(c_all, ((0, COND_ROWS - N_DEV), (0, 0)))
    b_shard = _my_columns(wts["b_cond"], chip)[:, None, :]
    mod_part = _cond_fwd(c_pad, wts["w_cond"], b_shard, "cond_fwd")
    g2 = _allgather8(mod_part[:, :N_DEV].transpose(1, 0, 2).reshape(N_DEV, DEPTH * n_cond), "gather_mod")
    g2 = g2.reshape(N_DEV, N_DEV, DEPTH, n_cond)[0::2]
    mod = _join_chips(lax.dynamic_index_in_dim(g2, dev, axis=1, keepdims=False)).reshape(DEPTH, N_SUB, 3, D_MODEL)

    chip_axes = [wts[n].ndim - 2 if by_rows else 0 for n, by_rows in _BIG]
    gathered = dict(zip([n for n, _ in _BIG],
                        _weights_allgather([wts[n].astype(BF16) for n, _ in _BIG], chip_axes, "gather_weights")))

    def rows_joined(n):
        g = gathered[n]
        return g.reshape(g.shape[:-3] + (g.shape[-3] * g.shape[-2], g.shape[-1]))

    ffn_out, fox_out, sconv_out, lru_out = (rows_joined(n) for n in ("w_ffn_out", "fox_w_out", "sconv_w_out", "lru_w_out"))
    fox_in = jnp.pad(_join_chips(gathered["fox_w_in"]), ((0, 0), (0, 0), (0, FOX_PAD - 3 * D_MODEL - FOX_HEADS)))
    lru_ax = jnp.concatenate([_block_diag(wts["lru_w_a"][0]), _block_diag(wts["lru_w_x"][0])], axis=1).astype(BF16)
    mixers = [
        lambda j: {"w_in": _W(fox_in, (j,)), "w_out": _W(fox_out, (j,)), "b_f": wts["fox_b_f"][j][:, None]},
        lambda j: {"w_in": _W(gathered["sconv_w_in"], (j,), True), "w_out": _W(sconv_out, (j,)),
                   "conv_w": small_full["sconv_conv_w"][j]},
        lambda j: {"w_in": _W(gathered["lru_w_in"], (j,), True), "w_out": _W(lru_out, (j,)),
                   "conv_w": small_full["lru_conv_w"][j], "conv_b": small_full["lru_conv_b"],
                   "w_ax": _W(lru_ax), "b_a": wts["lru_b_a"].reshape(1, D_MODEL),
                   "b_x": wts["lru_b_x"].reshape(1, D_MODEL), "lam": small_full["lru_lambda"]},
    ]
    layers = [{"ffn_in": [_W(gathered["w_ffn_in"], (i, j), True) for j in range(2)],
               "ffn_out": [_W(ffn_out, (i, j)) for j in range(2)],
               "norm_pre": small_full["norm_pre"][i], "norm_post": small_full["norm_post"][i],
               "mixer": mixers[i % 3](i // 3)} for i in range(DEPTH)]

    loss_row, grad_x, dmod, lg = _local_step(x[0], target[0], mod, {"layers": layers})
    loss = lax.psum(loss_row[0, 0], ("x", "y", "c"))

    fox_layers = [i for i in range(DEPTH) if i % 3 == 0]
    sconv_g, lru_g = lg[1]["mixer"], lg[2]["mixer"]
    small_g = {
        "dmod": dmod, "norm_pre": jnp.stack([g["norm_pre"] for g in lg]), "norm_post": jnp.stack([g["norm_post"] for g in lg]),
        "fox_b_f": jnp.stack([lg[i]["mixer"]["b_f"][:, 0] for i in fox_layers]),
        "sconv_conv_w": sconv_g["conv_w"][None], "lru_conv_w": lru_g["conv_w"][None], "lru_conv_b": lru_g["conv_b"],
        "lru_w_a": lru_g["w_a"][None], "lru_b_a": lru_g["b_a"].reshape(1, LRU_BLOCKS, LRU_BLOCK_DIM),
        "lru_w_x": lru_g["w_x"][None], "lru_b_x": lru_g["b_x"].reshape(1, LRU_BLOCKS, LRU_BLOCK_DIM),
        "lru_lambda": lru_g["lam"]}
    g4 = _allgather8(_pack_rows(list(small_g.values())), "gather_small_grads").reshape(N_DEV, -1)
    summed = dict(zip(small_g, _unpack(_sum_devices(g4, "sum_small_grads")[0], [v.shape for v in small_g.values()])))
    grads = {n: (_my_columns(summed[n], chip) if n in _COL_SHARDED_SMALL else summed[n]) for n in _SMALL if n != "b_cond"}
    grads["b_cond"] = summed["dmod"].reshape(DEPTH, N_SUB * 3 * D_MODEL)

    dmod_all = g4[:, :dmod.size].reshape(N_DEV, DEPTH, N_SUB * 3 * D_MODEL)
    dmod_s = jnp.pad(_my_columns(dmod_all, chip).transpose(1, 0, 2), ((0, 0), (0, COND_PAD - N_DEV), (0, 0))).astype(BF16)
    c_t = jnp.pad(c_all.T, ((0, 0), (0, COND_PAD - N_DEV)))
    grads["w_cond"], d_cond, m_cond, v_cond = _cond_bwd_adamw(c_t, dmod_s, wts["w_cond"], mom["w_cond"],
                                                              var["w_cond"], "cond_bwd_adamw")

    def chip_blocks(g, by_rows, width):
        if by_rows:
            return g.reshape(N_CHIPS, g.shape[0] // N_CHIPS, g.shape[1])
        if g.ndim == 3:
            return g
        return g[:, :width * N_CHIPS].reshape(g.shape[0], N_CHIPS, width).transpose(1, 0, 2)

    tensors, homes = [], []
    for o, (n, by_rows) in enumerate(_BIG):
        width = wts[n].shape[-1]
        if n.startswith("w_ffn"):
            key = "ffn_out" if by_rows else "ffn_in"
            items = [((i, j), lg[i][key][j]) for i in range(DEPTH) for j in range(2)]
        else:
            kind = ("fox", "sconv", "lru").index(n.split("_")[0])
            key = "w_out" if by_rows else "w_in"
            items = [((i // 3,), lg[i]["mixer"][key]) for i in range(DEPTH) if i % 3 == kind]
        for lead, g in items:
            tensors.append(chip_blocks(g, by_rows, width))
            homes.append((o, lead))
    recv = _pair_exchange(tensors, "grads_pair_exchange")
    c_idx = ic.astype(jnp.int32).reshape(1)
    parts = [_pair_sum(t, r, c_idx, f"grads_pair_sum{k}") for k, (t, r) in enumerate(zip(tensors, recv))]
    arrived = _chip_exchange(parts, "grads_chip_exchange")
    halves = [_chip_sum(q, f"grads_chip_sum{k}") for k, q in enumerate(arrived)]
    finals = _pair_gather(halves, homes, [wts[n].shape for n, _ in _BIG], "grads_pair_gather")
    grads.update(zip([n for n, _ in _BIG], finals))

    delta, new_m, new_v = {"w_cond": d_cond}, {"w_cond": m_cond}, {"w_cond": v_cond}
    for n, _ in _BIG:
        two_d = lambda a: a.reshape(-1, a.shape[-1])
        d, nm, nv = _adamw(two_d(wts[n]), two_d(grads[n]), two_d(mom[n]), two_d(var[n]), "adamw_" + n)
        delta[n], new_m[n], new_v[n] = (a.reshape(wts[n].shape) for a in (d, nm, nv))
    shapes = [wts[n].shape for n in _SMALL]
    packed = [_pack_rows([src[n] for n in _SMALL]) for src in (wts, grads, mom, var)]
    for dst, out in zip((delta, new_m, new_v), _adamw(*packed, "adamw_small")):
        dst.update(zip(_SMALL, _unpack(out.reshape(-1), shapes)))

    return (loss, grad_x[None], *[grads[n] for n in _WEIGHTS], *[delta[n] for n in _WEIGHTS],
            *[new_m[n] for n in _WEIGHTS], *[new_v[n] for n in _WEIGHTS])


def kernel(x, c, w_cond, b_cond, norm_pre, norm_post, w_ffn_in, w_ffn_out, fox_w_in, fox_b_f, fox_w_out, sconv_w_in, sconv_conv_w, sconv_w_out, lru_w_in, lru_conv_w, lru_conv_b, lru_w_a, lru_b_a, lru_w_x, lru_b_x, lru_lambda, lru_w_out, loss_target, m_w_cond, m_b_cond, m_norm_pre, m_norm_post, m_w_ffn_in, m_w_ffn_out, m_fox_w_in, m_fox_b_f, m_fox_w_out, m_sconv_w_in, m_sconv_conv_w, m_sconv_w_out, m_lru_w_in, m_lru_conv_w, m_lru_conv_b, m_lru_w_a, m_lru_b_a, m_lru_w_x, m_lru_b_x, m_lru_lambda, m_lru_w_out, v_w_cond, v_b_cond, v_norm_pre, v_norm_post, v_w_ffn_in, v_w_ffn_out, v_fox_w_in, v_fox_b_f, v_fox_w_out, v_sconv_w_in, v_sconv_conv_w, v_sconv_w_out, v_lru_w_in, v_lru_conv_w, v_lru_conv_b, v_lru_w_a, v_lru_b_a, v_lru_w_x, v_lru_b_x, v_lru_lambda, v_lru_w_out):
    given = dict(locals())
    wts = {n: given[n] for n in _WEIGHTS}
    mom = {n: given["m_" + n] for n in _WEIGHTS}
    var = {n: given["v_" + n] for n in _WEIGHTS}
    return _step(x, c, loss_target, wts, mom, var)
```

```python
import functools
import math
from typing import NamedTuple

import jax
import jax.numpy as jnp
from jax import lax
from jax.experimental import pallas as pl
from jax.experimental.pallas import tpu as pltpu

F32 = jnp.float32
BF16 = jnp.bfloat16

D_MODEL = 1024
DEPTH = 4
N_SUB = 3
D_FF = 2816
RMS_EPS = 1e-6
FOX_HEADS = 16
FOX_HEAD_DIM = 64
FOX_PAD = 3200
LRU_BLOCKS = 16
LRU_BLOCK_DIM = 64
LRU_C = 8.0
N_CHIPS = 4
N_DEV = 8

ADAM_LR = 0.001
ADAM_B1 = 0.9
ADAM_B2 = 0.999
ADAM_EPS = 1e-08
ADAM_WD = 0.01
ADAM_STEP = 10

VMEM_LIMIT_V7X = 56 * 1024 * 1024
ROW_TILE = 256
COL_TILE = 256
ATT_TILE = 256
MM_ROWS = 256


def _cparams(sem=None):
    return pltpu.CompilerParams(vmem_limit_bytes=VMEM_LIMIT_V7X, dimension_semantics=sem)


def _sigmoid(z):
    return 1.0 / (1.0 + jnp.exp(-z))


def _softplus(z):
    return jnp.maximum(z, 0.0) + jnp.log(1.0 + jnp.exp(-jnp.abs(z)))


def _rows_sum(v):
    return jnp.sum(v, axis=0, keepdims=True)


class _W(NamedTuple):
    arr: jax.Array
    prefix: tuple = ()
    blocked: bool = False


def _w_spec(w, block2, pos):
    lead = (None,) * (len(w.prefix) + (1 if w.blocked else 0))
    if w.blocked:
        return pl.BlockSpec(lead + block2, lambda *g: (pos(*g)[0], *w.prefix, pos(*g)[1], pos(*g)[2]))
    return pl.BlockSpec(lead + block2, lambda *g: (*w.prefix, pos(*g)[1], pos(*g)[2]))


def _mm_nn(a, b, name, tn=None):
    m, k = a.shape
    if b.blocked:
        steps, bn = b.arr.shape[0], b.arr.shape[-1]
        b_spec = _w_spec(b, (k, bn), lambda n: (n, 0, 0))
    else:
        n_total = b.arr.shape[-1]
        bn = n_total if tn is None else tn
        steps = n_total // bn
        assert steps * bn == n_total
        b_spec = _w_spec(b, (k, bn), lambda n: (0, 0, n))
    tm = min(MM_ROWS, m)

    def body(a_ref, b_ref, o_ref):
        def step(i, carry):
            r = pl.ds(pl.multiple_of(i * tm, tm), tm)
            o_ref[r, :] = jnp.dot(a_ref[r, :], b_ref[...], preferred_element_type=F32)
            return carry
        lax.fori_loop(0, m // tm, step, 0)

    return pl.pallas_call(
        body, name=name, grid=(steps,),
        in_specs=[pl.BlockSpec((m, k), lambda n: (0, 0)), b_spec],
        out_specs=pl.BlockSpec((m, bn), lambda n: (0, n)),
        out_shape=jax.ShapeDtypeStruct((m, steps * bn), F32),
        compiler_params=_cparams(("arbitrary",)),
    )(a, b.arr)


def _mm_nt(dy, w, name, tk=None, tn=None):
    m, n_total = dy.shape
    k = w.arr.shape[-2]
    if w.blocked:
        bk, bn = k, w.arr.shape[-1]
        grid = (1, w.arr.shape[0])
        w_spec = _w_spec(w, (k, bn), lambda kt, n: (n, 0, 0))
    else:
        bk = k if tk is None else tk
        bn = n_total if tn is None else tn
        grid = (k // bk, n_total // bn)
        assert grid[0] * bk == k and grid[1] * bn == n_total
        w_spec = _w_spec(w, (bk, bn), lambda kt, n: (0, kt, n))
    tm = min(MM_ROWS, m)

    def body(dy_ref, w_ref, o_ref):
        def step(i, carry):
            r = pl.ds(pl.multiple_of(i * tm, tm), tm)
            o_ref[r, :] += lax.dot_general(dy_ref[r, :], w_ref[...], (((1,), (1,)), ((), ())),
                                           preferred_element_type=F32)
            return carry

        @pl.when(pl.program_id(1) == 0)
        def _():
            o_ref[...] = jnp.zeros_like(o_ref)
        lax.fori_loop(0, m // tm, step, 0)

    return pl.pallas_call(
        body, name=name, grid=grid,
        in_specs=[pl.BlockSpec((m, bn), lambda kt, n: (0, n)), w_spec],
        out_specs=pl.BlockSpec((m, bk), lambda kt, n: (0, kt)),
        out_shape=jax.ShapeDtypeStruct((m, k), F32),
        compiler_params=_cparams(("arbitrary", "arbitrary")),
    )(dy, w.arr)


def _mm_tn(x, dy, name, tk=None, tn=None, blocked_out=False):
    s, k = x.shape
    n_total = dy.shape[1]
    bk = k if tk is None else tk
    bn = n_total if tn is None else tn
    grid = (k // bk, n_total // bn)
    assert grid[0] * bk == k and grid[1] * bn == n_total
    ck = 256 if bk % 256 == 0 else 128

    def body(x_ref, dy_ref, o_ref):
        def step(i, carry):
            c = pl.ds(pl.multiple_of(i * ck, ck), ck)
            o_ref[c, :] = lax.dot_general(x_ref[:, c], dy_ref[...], (((0,), (0,)), ((), ())),
                                          preferred_element_type=F32)
            return carry
        lax.fori_loop(0, bk // ck, step, 0)

    if blocked_out:
        assert grid[0] == 1
        out_spec = pl.BlockSpec((None, bk, bn), lambda kt, n: (n, 0, 0))
        out_shape = jax.ShapeDtypeStruct((grid[1], k, bn), F32)
    else:
        out_spec = pl.BlockSpec((bk, bn), lambda kt, n: (kt, n))
        out_shape = jax.ShapeDtypeStruct((k, n_total), F32)
    return pl.pallas_call(
        body, name=name, grid=grid,
        in_specs=[pl.BlockSpec((s, bk), lambda kt, n: (0, kt)), pl.BlockSpec((s, bn), lambda kt, n: (0, n))],
        out_specs=out_spec, out_shape=out_shape,
        compiler_params=_cparams(("arbitrary", "arbitrary")),
    )(x, dy)


def _row_call(name, body, rows, fulls, row_outs, acc_outs, tr=ROW_TILE):
    s = rows[0].shape[0]
    tr = min(tr, s)
    in_specs = [pl.BlockSpec((tr, a.shape[1]), lambda i: (i, 0)) for a in rows]
    in_specs += [pl.BlockSpec(a.shape, lambda i: (0, 0)) for a in fulls]
    out_specs = [pl.BlockSpec((tr, c), lambda i: (i, 0)) for c, _ in row_outs]
    out_specs += [pl.BlockSpec((1, c), lambda i: (0, 0)) for c, _ in acc_outs]
    out_shape = [jax.ShapeDtypeStruct((s, c), dt) for c, dt in row_outs]
    out_shape += [jax.ShapeDtypeStruct((1, c), dt) for c, dt in acc_outs]
    n_acc = len(acc_outs)

    def wrapped(*refs):
        if n_acc:
            @pl.when(pl.program_id(0) == 0)
            def _():
                for r in refs[len(refs) - n_acc:]:
                    r[...] = jnp.zeros_like(r)
        body(*refs)

    return pl.pallas_call(
        wrapped, name=name, grid=(s // tr,), in_specs=in_specs, out_specs=out_specs, out_shape=out_shape,
        compiler_params=_cparams(("arbitrary",)),
    )(*rows, *fulls)


def _rms(v):
    return lax.rsqrt(jnp.mean(v * v, axis=-1, keepdims=True) + RMS_EPS)


def _pre_norm(x, g_pre, scale, shift, name):
    def body(x_ref, g_ref, sc_ref, sh_ref, h_ref):
        xv = x_ref[...]
        h = (xv * _rms(xv)) * g_ref[...] * (1.0 + sc_ref[...]) + sh_ref[...]
        h_ref[...] = h.astype(BF16)
    return _row_call(name, body, [x], [g_pre, scale, shift], [(D_MODEL, BF16)], [])[0]


def _post_norm(x, y, g_post, gate, coef, name):
    def body(x_ref, y_ref, g_ref, gate_ref, o_ref):
        yv = y_ref[...]
        o_ref[...] = x_ref[...] + (coef * gate_ref[...]) * ((yv * _rms(yv)) * g_ref[...])
    return _row_call(name, body, [x, y], [g_post, gate], [(D_MODEL, F32)], [])[0]


def _post_norm_bwd(dxo, y, g_post, gate, coef, name):
    def body(dxo_ref, y_ref, g_ref, gate_ref, dy_ref, dgate_ref, dg_ref):
        yv = y_ref[...]
        r2 = _rms(yv)
        yn = yv * r2
        dxo_v = dxo_ref[...]
        dgate_ref[...] += _rows_sum(dxo_v * (yn * g_ref[...])) * coef
        dz = dxo_v * (coef * gate_ref[...])
        dg_ref[...] += _rows_sum(dz * yn)
        dyn = dz * g_ref[...]
        dy = r2 * (dyn - yn * jnp.mean(dyn * yn, axis=-1, keepdims=True))
        dy_ref[...] = dy.astype(BF16)
    return _row_call(name, body, [dxo, y], [g_post, gate], [(D_MODEL, BF16)], [(D_MODEL, F32), (D_MODEL, F32)])


def _pre_norm_bwd(dxo, dh, x, g_pre, scale, name):
    def body(dxo_ref, dh_ref, x_ref, g_ref, sc_ref, dx_ref, dshift_ref, dscale_ref, dg_ref):
        xv = x_ref[...]
        r = _rms(xv)
        xn = xv * r
        dh_v = dh_ref[...]
        one_sc = 1.0 + sc_ref[...]
        dshift_ref[...] += _rows_sum(dh_v)
        dscale_ref[...] += _rows_sum(dh_v * (xn * g_ref[...]))
        dg_ref[...] += _rows_sum(dh_v * xn * one_sc)
        dxn = dh_v * (g_ref[...] * one_sc)
        dx_ref[...] = dxo_ref[...] + r * (dxn - xn * jnp.mean(dxn * xn, axis=-1, keepdims=True))
    return _row_call(name, body, [dxo, dh, x], [g_pre, scale], [(D_MODEL, F32)],
                     [(D_MODEL, F32), (D_MODEL, F32), (D_MODEL, F32)])


def _swiglu_act(gu, name):
    def body(gu_ref, a_ref):
        g = gu_ref[:, :D_FF]
        a_ref[...] = (g * _sigmoid(g) * gu_ref[:, D_FF:]).astype(BF16)
    return _row_call(name, body, [gu], [], [(D_FF, BF16)], [])[0]


def _swiglu_act_bwd(da, gu, name):
    def body(da_ref, gu_ref, dgu_ref):
        g = gu_ref[:, :D_FF]
        sg = _sigmoid(g)
        da_v = da_ref[...]
        dgu_ref[:, :D_FF] = (da_v * gu_ref[:, D_FF:] * (sg * (1.0 + g * (1.0 - sg)))).astype(BF16)
        dgu_ref[:, D_FF:] = (da_v * (g * sg)).astype(BF16)
    return _row_call(name, body, [da, gu], [], [(2 * D_FF, BF16)], [])[0]


def _loss_head(y, target, name):
    def body(y_ref, t_ref, dy_ref, loss_ref):
        e = y_ref[...] - t_ref[...]
        dy_ref[...] = e * (1.0 / D_MODEL)
        part = jnp.sum(jnp.mean(e * e, axis=-1, keepdims=True), axis=0, keepdims=True) * 0.5
        loss_ref[...] += jnp.broadcast_to(part, loss_ref.shape)
    return _row_call(name, body, [y, target], [], [(D_MODEL, F32)], [(128, F32)])


def _lane_scan(v, reverse):
    s = v.shape[1]
    lane = lax.broadcasted_iota(jnp.int32, v.shape, 1)
    d = 1
    while d < s:
        if reverse:
            v = v + jnp.where(lane < s - d, pltpu.roll(v, s - d, 1), 0.0)
        else:
            v = v + jnp.where(lane >= d, pltpu.roll(v, d, 1), 0.0)
        d *= 2
    return v


def _fox_gate(flt, b_f, name):
    def body(f_ref, b_ref, cum_ref):
        z = f_ref[...] + b_ref[...]
        cum_ref[...] = _lane_scan(-_softplus(-z), reverse=False)
    return pl.pallas_call(body, name=name, out_shape=jax.ShapeDtypeStruct(flt.shape, F32),
                          compiler_params=_cparams())(flt, b_f)


def _fox_gate_bwd(dcum_q, dcum_k, flt, b_f, name):
    def body(dq_ref, dk_ref, f_ref, b_ref, df_ref, db_ref):
        z = f_ref[...] + b_ref[...]
        df = _lane_scan(dq_ref[...] + dk_ref[...], reverse=True) * _sigmoid(-z)
        df_ref[...] = df
        db_ref[...] = jnp.sum(df, axis=1, keepdims=True)
    h = flt.shape[0]
    return pl.pallas_call(body, name=name,
                          out_shape=(jax.ShapeDtypeStruct(flt.shape, F32), jax.ShapeDtypeStruct((h, 1), F32)),
                          compiler_params=_cparams())(dcum_q, dcum_k, flt, b_f)


def _pick_head(block, h):
    lane = lax.broadcasted_iota(jnp.int32, block.shape, 1)
    return jnp.sum(jnp.where(lane == h, block, 0.0), axis=1, keepdims=True)


def _put_head(ref, col, h):
    @pl.when(h == 0)
    def _():
        ref[...] = jnp.zeros_like(ref)
    lane = lax.broadcasted_iota(jnp.int32, ref.shape, 1)
    ref[...] = jnp.where(lane == h, col, ref[...])


_NT = (((1,), (1,)), ((), ()))
_FOX_SCALE = FOX_HEAD_DIM ** -0.5


def _causal(s_tile, t):
    row = lax.broadcasted_iota(jnp.int32, (t, t), 0)
    col = lax.broadcasted_iota(jnp.int32, (t, t), 1)
    return jnp.where(col <= row, s_tile, -jnp.inf)


HEAD_PAIRS = FOX_HEADS // 2
PAIR_W = 2 * FOX_HEAD_DIM


def _low_half(shape):
    return lax.broadcasted_iota(jnp.int32, shape, 1) < FOX_HEAD_DIM


def _fox_attn_fwd(qkv, cum, cum_t, name):
    s = qkv.shape[0]
    t = min(ATT_TILE, s)

    def body(q_ref, k_ref, v_ref, cum_ref, cumt_ref, o_ref, ob_ref, lse_ref):
        i = pl.program_id(0)
        hp = pl.program_id(1)
        lo = _low_half((t, PAIR_W))
        qv = q_ref[...]
        zero = jnp.zeros_like(qv)
        q2 = (jnp.where(lo, qv, zero), jnp.where(lo, zero, qv))
        cum_v = cum_ref[...]
        cq2 = (_pick_head(cum_v, 2 * hp), _pick_head(cum_v, 2 * hp + 1))

        def step(j, carry, masked):
            ks = pl.ds(pl.multiple_of(j * t, t), t)
            kj = k_ref[ks, :]
            vj = v_ref[ks, :]
            out = []
            for e in range(2):
                m, l, acc = carry[e]
                sc = lax.dot_general(q2[e], kj, _NT, preferred_element_type=F32) * _FOX_SCALE
                sc = sc + cq2[e] - cumt_ref[e:e + 1, ks]
                if masked:
                    sc = _causal(sc, t)
                m_new = jnp.maximum(m, jnp.max(sc, axis=1, keepdims=True))
                alpha = jnp.exp(m - m_new)
                p = jnp.exp(sc - m_new)
                l = alpha * l + jnp.sum(p, axis=1, keepdims=True)
                acc = alpha * acc + jnp.dot(p.astype(BF16), vj, preferred_element_type=F32)
                out.append((m_new, l, acc))
            return tuple(out)

        one = (jnp.full((t, 1), -jnp.inf, F32), jnp.zeros((t, 1), F32), jnp.zeros((t, PAIR_W), F32))
        carry = lax.fori_loop(0, i, lambda j, c: step(j, c, False), (one, one))
        (m0, l0, a0), (m1, l1, a1) = step(i, carry, True)
        o = jnp.where(lo, a0 / l0, a1 / l1)
        o_ref[...] = o
        ob_ref[...] = o.astype(BF16)
        _put_head(lse_ref, m0 + jnp.log(l0), 2 * hp)
        _put_head(lse_ref, m1 + jnp.log(l1), 2 * hp + 1)

    nat_tile = pl.BlockSpec((t, FOX_HEADS), lambda i, hp: (i, 0))
    out_tile = pl.BlockSpec((t, PAIR_W), lambda i, hp: (i, hp))
    return pl.pallas_call(
        body, name=name, grid=(s // t, HEAD_PAIRS),
        in_specs=[pl.BlockSpec((t, PAIR_W), lambda i, hp: (i, hp)),
                  pl.BlockSpec((s, PAIR_W), lambda i, hp: (0, HEAD_PAIRS + hp)),
                  pl.BlockSpec((s, PAIR_W), lambda i, hp: (0, 2 * HEAD_PAIRS + hp)),
                  nat_tile, pl.BlockSpec((None, 2, s), lambda i, hp: (hp, 0, 0))],
        out_specs=[out_tile, out_tile, nat_tile],
        out_shape=[jax.ShapeDtypeStruct((s, D_MODEL), F32), jax.ShapeDtypeStruct((s, D_MODEL), BF16),
                   jax.ShapeDtypeStruct((s, FOX_HEADS), F32)],
        compiler_params=_cparams(("arbitrary", "arbitrary")),
    )(qkv, qkv, qkv, cum, cum_t)


def _fox_delta(do, o, expand, name):
    def body(do_ref, o_ref, e_ref, d_ref):
        prod = do_ref[...] * o_ref[...]
        hi = prod.astype(BF16)
        lo = (prod - hi.astype(F32)).astype(BF16)
        tot = (jnp.dot(hi, e_ref[...], preferred_element_type=F32)
               + jnp.dot(lo, e_ref[...], preferred_element_type=F32))
        d_ref[...] = tot[:, :FOX_HEADS]
    return _row_call(name, body, [do, o], [expand], [(FOX_HEADS, F32)], [])[0]


def _fox_attn_bwd(qkv, do, cum, cum_t, lse_t, delta_t, name):
    s = qkv.shape[0]
    t = min(ATT_TILE, s)
    nq = s // t
    tn_dims = (((0,), (0,)), ((), ()))

    def body(q_ref, k_ref, v_ref, do_ref, cum_ref, cumt_ref, lset_ref, deltat_ref,
             dq_ref, dk_ref, dv_ref, dck_ref, dcq_ref):
        hp = pl.program_id(0)
        j = pl.program_id(1)

        @pl.when(j == 0)
        def _():
            dq_ref[...] = jnp.zeros_like(dq_ref)
            dcq_ref[...] = jnp.zeros_like(dcq_ref)
        dk_ref[...] = jnp.zeros_like(dk_ref)
        dv_ref[...] = jnp.zeros_like(dv_ref)

        lo = _low_half((t, PAIR_W))
        lane = lax.broadcasted_iota(jnp.int32, (t, PAIR_W), 1)
        kv = k_ref[...]
        vv = v_ref[...]
        zero = jnp.zeros_like(kv)
        k2 = (jnp.where(lo, kv, zero), jnp.where(lo, zero, kv))
        v2 = (jnp.where(lo, vv, zero), jnp.where(lo, zero, vv))
        cum_v = cum_ref[...]
        ck2 = (_pick_head(cum_v, 2 * hp), _pick_head(cum_v, 2 * hp + 1))

        def step(i, dck, masked):
            qs = pl.ds(pl.multiple_of(i * t, t), t)
            qi = q_ref[qs, :]
            do_i = do_ref[qs, :].astype(BF16)
            dv_p, dk_p, dq_p = [], [], []
            for e in range(2):
                st = lax.dot_general(k2[e], qi, _NT, preferred_element_type=F32) * _FOX_SCALE
                st = st + cumt_ref[e:e + 1, qs] - ck2[e]
                if masked:
                    row = lax.broadcasted_iota(jnp.int32, (t, t), 0)
                    col = lax.broadcasted_iota(jnp.int32, (t, t), 1)
                    st = jnp.where(row <= col, st, -jnp.inf)
                pt = jnp.exp(st - lset_ref[e:e + 1, qs])
                dv_p.append(jnp.dot(pt.astype(BF16), do_i, preferred_element_type=F32))
                dpt = lax.dot_general(v2[e], do_i, _NT, preferred_element_type=F32)
                dst = pt * (dpt - deltat_ref[e:e + 1, qs])
                dsb = dst.astype(BF16)
                dk_p.append(jnp.dot(dsb, qi, preferred_element_type=F32))
                dq_p.append(lax.dot_general(dsb, kv, tn_dims, preferred_element_type=F32))
                dck = dck - jnp.where(lane == e, jnp.sum(dst, axis=1, keepdims=True), 0.0)
                dcq_ref[e:e + 1, qs] += jnp.sum(dst, axis=0, keepdims=True)
            dv_ref[...] += jnp.where(lo, dv_p[0], dv_p[1])
            dk_ref[...] += jnp.where(lo, dk_p[0], dk_p[1])
            dq_ref[qs, :] += jnp.where(lo, dq_p[0], dq_p[1]) * _FOX_SCALE
            return dck

        dck = step(j, jnp.zeros((t, PAIR_W), F32), True)
        dck = lax.fori_loop(j + 1, nq, lambda i, c: step(i, c, False), dck)
        dk_ref[...] = dk_ref[...] * _FOX_SCALE
        dck_ref[...] = dck

    pair_full = lambda part: pl.BlockSpec((s, PAIR_W), lambda hp, j: (0, part * HEAD_PAIRS + hp))
    pair_tile = lambda part: pl.BlockSpec((t, PAIR_W), lambda hp, j: (j, part * HEAD_PAIRS + hp))
    rows = pl.BlockSpec((None, 2, s), lambda hp, j: (hp, 0, 0))
    return pl.pallas_call(
        body, name=name, grid=(HEAD_PAIRS, nq),
        in_specs=[pair_full(0), pair_tile(1), pair_tile(2), pair_full(0),
                  pl.BlockSpec((t, FOX_HEADS), lambda hp, j: (j, 0)), rows, rows, rows],
        out_specs=[pair_full(0), pair_tile(0), pair_tile(0),
                   pl.BlockSpec((None, t, PAIR_W), lambda hp, j: (hp, j, 0)), rows],
        out_shape=[jax.ShapeDtypeStruct((s, D_MODEL), F32)] * 3
        + [jax.ShapeDtypeStruct((HEAD_PAIRS, s, PAIR_W), F32), jax.ShapeDtypeStruct((HEAD_PAIRS, 2, s), F32)],
        compiler_params=_cparams(("arbitrary", "arbitrary")),
    )(qkv, qkv, qkv, do, cum, cum_t, lse_t, delta_t)


def _shift_down(v, d):
    row = lax.broadcasted_iota(jnp.int32, v.shape, 0)
    return jnp.where(row >= d, pltpu.roll(v, d, 0), 0.0)


def _shift_up(v, d):
    s = v.shape[0]
    row = lax.broadcasted_iota(jnp.int32, v.shape, 0)
    return jnp.where(row < s - d, pltpu.roll(v, s - d, 0), 0.0)


def _conv_taps(v, cw_ref, width):
    out = cw_ref[width - 1:width, :] * v
    for k in range(width - 1):
        out = out + cw_ref[k:k + 1, :] * _shift_down(v, width - 1 - k)
    return out


def _conv_taps_bwd(dout, v, cw_ref, dcw_ref, width):
    dv = cw_ref[width - 1:width, :] * dout
    dcw_ref[width - 1:width, :] = _rows_sum(dout * v)
    for k in range(width - 1):
        d = width - 1 - k
        dv = dv + cw_ref[k:k + 1, :] * _shift_up(dout, d)
        dcw_ref[k:k + 1, :] = _rows_sum(dout * _shift_down(v, d))
    return dv


def _col_spec(s, tc, part=0):
    off = part * (D_MODEL // tc)
    return pl.BlockSpec((s, tc), lambda c: (0, c + off))


def _small_spec(rows, tc):
    return pl.BlockSpec((rows, tc), lambda c: (0, c))


def _col_call(name, body, in_arrays, in_specs, out_rows, s, tc):
    return pl.pallas_call(
        body, name=name, grid=(D_MODEL // tc,), in_specs=in_specs,
        out_specs=[pl.BlockSpec((r, tc), lambda c: (0, c)) for r, _ in out_rows],
        out_shape=[jax.ShapeDtypeStruct((r, D_MODEL), dt) for r, dt in out_rows],
        compiler_params=_cparams(("arbitrary",)),
    )(*in_arrays)


def _sconv_fwd(proj, conv_w, name):
    s = proj.shape[0]
    tc = COL_TILE

    def body(b_ref, c_ref, x_ref, cw_ref, y_ref):
        y_ref[...] = (b_ref[...] * _conv_taps(c_ref[...] * x_ref[...], cw_ref, 3)).astype(BF16)

    return _col_call(name, body, [proj, proj, proj, conv_w],
                     [_col_spec(s, tc, 0), _col_spec(s, tc, 1), _col_spec(s, tc, 2), _small_spec(3, tc)],
                     [(s, BF16)], s, tc)[0]


def _sconv_bwd(dy, proj, conv_w, name):
    s = proj.shape[0]
    tc = COL_TILE

    def body(dy_ref, b_ref, c_ref, x_ref, cw_ref, db_ref, dc_ref, dx_ref, dcw_ref):
        w = c_ref[...] * x_ref[...]
        dy_v = dy_ref[...]
        db_ref[...] = (dy_v * _conv_taps(w, cw_ref, 3)).astype(BF16)
        dw = _conv_taps_bwd(dy_v * b_ref[...], w, cw_ref, dcw_ref, 3)
        dc_ref[...] = (dw * x_ref[...]).astype(BF16)
        dx_ref[...] = (dw * c_ref[...]).astype(BF16)

    return _col_call(name, body, [dy, proj, proj, proj, conv_w],
                     [_col_spec(s, tc), _col_spec(s, tc, 0), _col_spec(s, tc, 1), _col_spec(s, tc, 2),
                      _small_spec(3, tc)],
                     [(s, BF16), (s, BF16), (s, BF16), (3, F32)], s, tc)


def _lru_conv(proj, conv_w, conv_b, name):
    s = proj.shape[0]
    tc = COL_TILE

    def body(x_ref, cw_ref, cb_ref, xb_ref, xbb_ref):
        xb = _conv_taps(x_ref[...], cw_ref, 4) + cb_ref[...]
        xb_ref[...] = xb
        xbb_ref[...] = xb.astype(BF16)

    return _col_call(name, body, [proj, conv_w, conv_b],
                     [_col_spec(s, tc, 1), _small_spec(4, tc), _small_spec(1, tc)],
                     [(s, F32), (s, BF16)], s, tc)


def _lru_conv_bwd(dxb1, dxb2, proj, conv_w, name):
    s = proj.shape[0]
    tc = COL_TILE

    def body(d1_ref, d2_ref, x_ref, cw_ref, dx_ref, dcw_ref, dcb_ref):
        dxb = d1_ref[...] + d2_ref[...]
        dcb_ref[...] = _rows_sum(dxb)
        dx_ref[...] = _conv_taps_bwd(dxb, x_ref[...], cw_ref, dcw_ref, 4).astype(BF16)

    return _col_call(name, body, [dxb1, dxb2, proj, conv_w],
                     [_col_spec(s, tc), _col_spec(s, tc), _col_spec(s, tc, 1), _small_spec(4, tc)],
                     [(s, BF16), (4, F32), (1, F32)], s, tc)


_GELU_C = math.sqrt(2.0 / math.pi)


def _gelu_parts(g):
    inner = _GELU_C * (g + 0.044715 * g * g * g)
    th = jnp.tanh(inner)
    val = 0.5 * g * (1.0 + th)
    der = 0.5 * (1.0 + th) + 0.5 * g * (1.0 - th * th) * (_GELU_C * (1.0 + 3.0 * 0.044715 * g * g))
    return val, der


def _lru_gates(pa_ref, px_ref, ba_ref, bx_ref, lam_ref):
    r = _sigmoid(pa_ref[...] + ba_ref[...])
    ig = _sigmoid(px_ref[...] + bx_ref[...])
    sp = _softplus(-lam_ref[...])
    log_a = (-LRU_C) * r * sp
    a = jnp.exp(log_a)
    z = 2.0 * log_a
    one_m_a2 = jnp.where(z > -1e-3, -(z * (1.0 + z * (0.5 + z * (1.0 / 6.0)))), 1.0 - jnp.exp(z))
    return r, ig, sp, a, jnp.sqrt(one_m_a2)


def _lru_scan(pre, xb, proj, b_a, b_x, lam, name):
    s = xb.shape[0]
    tc = COL_TILE

    def body(pa_ref, px_ref, xb_ref, g_ref, ba_ref, bx_ref, lam_ref, y_ref, hs_ref):
        _, ig, _, a, mult = _lru_gates(pa_ref, px_ref, ba_ref, bx_ref, lam_ref)
        b = mult * (ig * xb_ref[...])
        d = 1
        while d < s:
            row = lax.broadcasted_iota(jnp.int32, a.shape, 0)
            keep = row >= d
            b = b + a * jnp.where(keep, pltpu.roll(b, d, 0), 0.0)
            a = a * jnp.where(keep, pltpu.roll(a, d, 0), 1.0)
            d *= 2
        hs_ref[...] = b
        y_ref[...] = (b * _gelu_parts(g_ref[...])[0]).astype(BF16)

    return _col_call(name, body, [pre, pre, xb, proj, b_a, b_x, lam],
                     [_col_spec(s, tc, 0), _col_spec(s, tc, 1), _col_spec(s, tc), _col_spec(s, tc, 0),
                      _small_spec(1, tc), _small_spec(1, tc), _small_spec(1, tc)],
                     [(s, BF16), (s, F32)], s, tc)


def _lru_scan_bwd(dy, pre, xb, proj, hs, b_a, b_x, lam, name):
    s = xb.shape[0]
    tc = COL_TILE

    def body(dy_ref, pa_ref, px_ref, xb_ref, g_ref, hs_ref, ba_ref, bx_ref, lam_ref,
             dg_ref, dpa_ref, dpx_ref, dxb_ref, dba_ref, dbx_ref, dlam_ref):
        r, ig, sp, a, mult = _lru_gates(pa_ref, px_ref, ba_ref, bx_ref, lam_ref)
        gl, gl_der = _gelu_parts(g_ref[...])
        dy_v = dy_ref[...]
        hs_v = hs_ref[...]
        dg_ref[...] = (dy_v * hs_v * gl_der).astype(BF16)
        lam_t = dy_v * gl
        coef = _shift_up(a, 1)
        d = 1
        while d < s:
            row = lax.broadcasted_iota(jnp.int32, coef.shape, 0)
            keep = row < s - d
            lam_t = lam_t + coef * jnp.where(keep, pltpu.roll(lam_t, s - d, 0), 0.0)
            coef = coef * jnp.where(keep, pltpu.roll(coef, s - d, 0), 1.0)
            d *= 2
        xb_v = xb_ref[...]
        da = lam_t * _shift_down(hs_v, 1)
        dmult = lam_t * (ig * xb_v)
        dig = lam_t * mult * xb_v
        dxb_ref[...] = lam_t * mult * ig
        dlog_a = da * a - dmult * (a * a) / mult
        dr = dlog_a * ((-LRU_C) * sp)
        dsp = _rows_sum(dlog_a * ((-LRU_C) * r))
        dlam_ref[...] = -dsp * _sigmoid(-lam_ref[...])
        dpa = dr * r * (1.0 - r)
        dpx = dig * ig * (1.0 - ig)
        dba_ref[...] = _rows_sum(dpa)
        dbx_ref[...] = _rows_sum(dpx)
        dpa_ref[...] = dpa.astype(BF16)
        dpx_ref[...] = dpx.astype(BF16)

    return _col_call(name, body, [dy, pre, pre, xb, proj, hs, b_a, b_x, lam],
                     [_col_spec(s, tc), _col_spec(s, tc, 0), _col_spec(s, tc, 1), _col_spec(s, tc),
                      _col_spec(s, tc, 0), _col_spec(s, tc),
                      _small_spec(1, tc), _small_spec(1, tc), _small_spec(1, tc)],
                     [(s, BF16), (s, BF16), (s, BF16), (s, F32), (1, F32), (1, F32), (1, F32)], s, tc)


def _ffn_fwd(x, w_in, w_out, g_pre, g_post, shift, scale, gate, tag):
    h = _pre_norm(x, g_pre, scale, shift, tag + "_pre")
    gu = _mm_nn(h, w_in, tag + "_in")
    a = _swiglu_act(gu, tag + "_act")
    y = _mm_nn(a, w_out, tag + "_out", tn=512)
    xo = _post_norm(x, y, g_post, gate, 0.5, tag + "_post")
    return xo, (x, h, gu, a, y)


def _ffn_bwd(dxo, saved, w_in, w_out, g_pre, g_post, scale, gate, tag):
    x, h, gu, a, y = saved
    dy, dgate, dg_post = _post_norm_bwd(dxo, y, g_post, gate, 0.5, tag + "_post_b")
    da = _mm_nt(dy, w_out, tag + "_out_bx", tk=D_FF // 2)
    dw_out = _mm_tn(a, dy, tag + "_out_bw", tk=D_FF // 2)
    dgu = _swiglu_act_bwd(da, gu, tag + "_act_b")
    dh = _mm_nt(dgu, w_in, tag + "_in_bx")
    dw_in = _mm_tn(h, dgu, tag + "_in_bw", tn=w_in.arr.shape[-1], blocked_out=True)
    dx, dshift, dscale, dg_pre = _pre_norm_bwd(dxo, dh, x, g_pre, scale, tag + "_pre_b")
    return dx, dw_in, dw_out, (dshift, dscale, dgate), dg_pre, dg_post


def _pair_rows(v):
    return v.T.reshape(HEAD_PAIRS, 2, v.shape[0])


def _fox_fwd(h, p, tag):
    s = h.shape[0]
    proj = _mm_nn(h, p["w_in"], tag + "_in", tn=640)
    qkv = proj[:, :3 * D_MODEL].astype(BF16)
    flt = proj[:, 3 * D_MODEL:3 * D_MODEL + FOX_HEADS].T
    cum_t = _fox_gate(flt, p["b_f"], tag + "_gate")
    cum = cum_t.T
    cum_t2 = cum_t.reshape(HEAD_PAIRS, 2, s)
    o, ob, lse = _fox_attn_fwd(qkv, cum, cum_t2, tag + "_attn")
    y = _mm_nn(ob, p["w_out"], tag + "_out")
    return y, (qkv, flt, cum, cum_t2, o, ob, lse)


def _fox_bwd(dy, h, saved, p, tag):
    qkv, flt, cum, cum_t2, o, ob, lse = saved
    s = h.shape[0]
    do = _mm_nt(dy, p["w_out"], tag + "_out_bx")
    dw_out = _mm_tn(ob, dy, tag + "_out_bw")
    expand = jnp.pad(jnp.repeat(jnp.eye(FOX_HEADS, dtype=BF16), FOX_HEAD_DIM, axis=0),
                     ((0, 0), (0, PAIR_W - FOX_HEADS)))
    delta = _fox_delta(do, o, expand, tag + "_attn_delta")
    dq, dk, dv, dck, dcq = _fox_attn_bwd(qkv, do, cum, cum_t2, _pair_rows(lse), _pair_rows(delta), tag + "_attn_b")
    dcum_k = dck[:, :, :2].transpose(0, 2, 1).reshape(FOX_HEADS, s)
    dflt, db_f = _fox_gate_bwd(dcq.reshape(FOX_HEADS, s), dcum_k, flt, p["b_f"], tag + "_gate_b")
    dproj = jnp.concatenate(
        [dq, dk, dv, dflt.T, jnp.zeros((s, FOX_PAD - 3 * D_MODEL - FOX_HEADS), F32)], axis=1).astype(BF16)
    dh = _mm_nt(dproj, p["w_in"], tag + "_in_bx", tn=640)
    dw_in = _mm_tn(h, dproj, tag + "_in_bw", tn=640)
    return dh, {"w_in": dw_in, "w_out": dw_out, "b_f": db_f}


def _sconv_mix_fwd(h, p, tag):
    proj = _mm_nn(h, p["w_in"], tag + "_in")
    yb = _sconv_fwd(proj, p["conv_w"], tag + "_conv")
    y = _mm_nn(yb, p["w_out"], tag + "_out")
    return y, (proj, yb)


def _sconv_mix_bwd(dy, h, saved, p, tag):
    proj, yb = saved
    dyb = _mm_nt(dy, p["w_out"], tag + "_out_bx")
    dw_out = _mm_tn(yb, dy, tag + "_out_bw")
    db, dc, dxv, dcw = _sconv_bwd(dyb, proj, p["conv_w"], tag + "_conv_b")
    dproj = jnp.concatenate([db, dc, dxv], axis=1)
    dh = _mm_nt(dproj, p["w_in"], tag + "_in_bx")
    dw_in = _mm_tn(h, dproj, tag + "_in_bw", tn=p["w_in"].arr.shape[-1], blocked_out=True)
    return dh, {"w_in": dw_in, "w_out": dw_out, "conv_w": dcw}


def _lru_mix_fwd(h, p, tag):
    proj = _mm_nn(h, p["w_in"], tag + "_in")
    xb, xbb = _lru_conv(proj, p["conv_w"], p["conv_b"], tag + "_conv")
    pre = _mm_nn(xbb, p["w_ax"], tag + "_gates", tn=D_MODEL)
    yb, hs = _lru_scan(pre, xb, proj, p["b_a"], p["b_x"], p["lam"], tag + "_scan")
    y = _mm_nn(yb, p["w_out"], tag + "_out")
    return y, (proj, xb, xbb, pre, yb, hs)


def _diag_blocks(m):
    return jnp.stack([m[LRU_BLOCK_DIM * n:LRU_BLOCK_DIM * (n + 1), LRU_BLOCK_DIM * n:LRU_BLOCK_DIM * (n + 1)]
                      for n in range(LRU_BLOCKS)])


def _lru_mix_bwd(dy, h, saved, p, tag):
    proj, xb, xbb, pre, yb, hs = saved
    dyb = _mm_nt(dy, p["w_out"], tag + "_out_bx")
    dw_out = _mm_tn(yb, dy, tag + "_out_bw")
    dg, dpa, dpx, dxb1, dba, dbx, dlam = _lru_scan_bwd(dyb, pre, xb, proj, hs, p["b_a"], p["b_x"], p["lam"],
                                                       tag + "_scan_b")
    dpre = jnp.concatenate([dpa, dpx], axis=1)
    dxb2 = _mm_nt(dpre, p["w_ax"], tag + "_gates_bx", tn=D_MODEL)
    dw_ax = _mm_tn(xbb, dpre, tag + "_gates_bw", tn=D_MODEL)
    dx0, dcw, dcb = _lru_conv_bwd(dxb1, dxb2, proj, p["conv_w"], tag + "_conv_b")
    dproj = jnp.concatenate([dg, dx0], axis=1)
    dh = _mm_nt(dproj, p["w_in"], tag + "_in_bx")
    dw_in = _mm_tn(h, dproj, tag + "_in_bw", tn=p["w_in"].arr.shape[-1], blocked_out=True)
    grads = {"w_in": dw_in, "w_out": dw_out, "conv_w": dcw, "conv_b": dcb,
             "w_a": _diag_blocks(dw_ax[:, :D_MODEL]), "w_x": _diag_blocks(dw_ax[:, D_MODEL:]),
             "b_a": dba, "b_x": dbx, "lam": dlam}
    return dh, grads


_MIXERS = ((_fox_fwd, _fox_bwd), (_sconv_mix_fwd, _sconv_mix_bwd), (_lru_mix_fwd, _lru_mix_bwd))


def _local_step(x, target, mod, params):
    layers = params["layers"]
    tape = []
    for i, lp in enumerate(layers):
        row = lambda v: v[None, :]
        m = lambda sub, what: mod[i, sub, what][None, :]
        x, sv0 = _ffn_fwd(x, lp["ffn_in"][0], lp["ffn_out"][0], row(lp["norm_pre"][0]), row(lp["norm_post"][0]),
                          m(0, 0), m(0, 1), m(0, 2), f"l{i}_ffn0")
        h = _pre_norm(x, row(lp["norm_pre"][1]), m(1, 1), m(1, 0), f"l{i}_mix_pre")
        y, svm = _MIXERS[i % 3][0](h, lp["mixer"], f"l{i}_mix")
        x1 = _post_norm(x, y, row(lp["norm_post"][1]), m(1, 2), 1.0, f"l{i}_mix_post")
        x2, sv2 = _ffn_fwd(x1, lp["ffn_in"][1], lp["ffn_out"][1], row(lp["norm_pre"][2]), row(lp["norm_post"][2]),
                           m(2, 0), m(2, 1), m(2, 2), f"l{i}_ffn1")
        tape.append((sv0, (x, h, y, svm), sv2))
        x = x2
    dx, loss_row = _loss_head(x, target, "loss_head")

    layer_grads = [None] * len(layers)
    dmod = [None] * len(layers)
    for i in reversed(range(len(layers))):
        lp = layers[i]
        row = lambda v: v[None, :]
        m = lambda sub, what: mod[i, sub, what][None, :]
        sv0, (xm, h, y, svm), sv2 = tape[i]
        dx, dw_in1, dw_out1, dm2, dgp2, dgq2 = _ffn_bwd(dx, sv2, lp["ffn_in"][1], lp["ffn_out"][1],
                                                        row(lp["norm_pre"][2]), row(lp["norm_post"][2]),
                                                        m(2, 1), m(2, 2), f"l{i}_ffn1")
        dy, dgate1, dgq1 = _post_norm_bwd(dx, y, row(lp["norm_post"][1]), m(1, 2), 1.0, f"l{i}_mix_post_b")
        dh, mg = _MIXERS[i % 3][1](dy, h, svm, lp["mixer"], f"l{i}_mix")
        dx, dshift1, dscale1, dgp1 = _pre_norm_bwd(dx, dh, xm, row(lp["norm_pre"][1]), m(1, 1), f"l{i}_mix_pre_b")
        dx, dw_in0, dw_out0, dm0, dgp0, dgq0 = _ffn_bwd(dx, sv0, lp["ffn_in"][0], lp["ffn_out"][0],
                                                        row(lp["norm_pre"][0]), row(lp["norm_post"][0]),
                                                        m(0, 1), m(0, 2), f"l{i}_ffn0")
        dmod[i] = jnp.concatenate([*dm0, dshift1, dscale1, dgate1, *dm2], axis=0).reshape(N_SUB, 3, D_MODEL)
        layer_grads[i] = {"ffn_in": (dw_in0, dw_in1), "ffn_out": (dw_out0, dw_out1),
                          "norm_pre": jnp.concatenate([dgp0, dgp1, dgp2], axis=0),
                          "norm_post": jnp.concatenate([dgq0, dgq1, dgq2], axis=0), "mixer": mg}
    return loss_row, dx, jnp.stack(dmod), layer_grads


COND_ROWS = 16
COND_PAD = 128


def _cond_fwd(c_pad, w_cond, b_shard, name):
    nl, d, n = w_cond.shape
    tn = 768

    def body(c_ref, w_ref, b_ref, o_ref):
        cv = c_ref[...]
        act = (cv * _sigmoid(cv)).astype(BF16)
        o_ref[...] = jnp.dot(act, w_ref[...].astype(BF16), preferred_element_type=F32) + b_ref[...]

    return pl.pallas_call(
        body, name=name, grid=(nl, n // tn),
        in_specs=[pl.BlockSpec((COND_ROWS, d), lambda i, j: (0, 0)),
                  pl.BlockSpec((None, d, tn), lambda i, j: (i, 0, j)),
                  pl.BlockSpec((None, 1, tn), lambda i, j: (i, 0, j))],
        out_specs=pl.BlockSpec((None, COND_ROWS, tn), lambda i, j: (i, 0, j)),
        out_shape=jax.ShapeDtypeStruct((nl, COND_ROWS, n), F32),
        compiler_params=_cparams(("arbitrary", "arbitrary")),
    )(c_pad, w_cond, b_shard)


def _adam_math(w, g, m, v):
    nm = ADAM_B1 * m + (1.0 - ADAM_B1) * g
    nv = ADAM_B2 * v + (1.0 - ADAM_B2) * (g * g)
    m_hat = nm / (1.0 - ADAM_B1 ** ADAM_STEP)
    v_hat = nv / (1.0 - ADAM_B2 ** ADAM_STEP)
    delta = (-ADAM_LR) * (m_hat / (jnp.sqrt(v_hat) + ADAM_EPS) + ADAM_WD * w)
    return delta, nm, nv


def _cond_bwd_adamw(c_t, dmod_s, w, m, v, name):
    nl, d, n = w.shape
    tn = 384
    blk = pl.BlockSpec((None, d, tn), lambda i, j: (i, 0, j))

    def body(c_ref, dm_ref, w_ref, m_ref, v_ref, g_ref, d_ref, nm_ref, nv_ref):
        cv = c_ref[...]
        g = jnp.dot((cv * _sigmoid(cv)).astype(BF16), dm_ref[...], preferred_element_type=F32)
        g_ref[...] = g
        d_ref[...], nm_ref[...], nv_ref[...] = _adam_math(w_ref[...], g, m_ref[...], v_ref[...])

    return pl.pallas_call(
        body, name=name, grid=(nl, n // tn),
        in_specs=[pl.BlockSpec((d, COND_PAD), lambda i, j: (0, 0)),
                  pl.BlockSpec((None, COND_PAD, tn), lambda i, j: (i, 0, j)), blk, blk, blk],
        out_specs=[blk] * 4, out_shape=[jax.ShapeDtypeStruct(w.shape, F32)] * 4,
        compiler_params=_cparams(("arbitrary", "arbitrary")),
    )(c_t, dmod_s, w, m, v)


def _adamw(w, g, m, v, name):
    rows, cols = w.shape
    tr = next(t for t in (256, 176, 128, 64, 32, 16, 8) if rows % t == 0)
    blk = pl.BlockSpec((tr, cols), lambda i: (i, 0))

    def body(w_ref, g_ref, m_ref, v_ref, d_ref, nm_ref, nv_ref):
        d_ref[...], nm_ref[...], nv_ref[...] = _adam_math(w_ref[...], g_ref[...], m_ref[...], v_ref[...])

    return pl.pallas_call(
        body, name=name, grid=(rows // tr,), in_specs=[blk] * 4, out_specs=[blk] * 3,
        out_shape=[jax.ShapeDtypeStruct(w.shape, F32)] * 3, compiler_params=_cparams(("arbitrary",)),
    )(w, g, m, v)


_MESH = pl.DeviceIdType.MESH
_ANY = pl.BlockSpec(memory_space=pl.ANY)


def _place():
    return lax.axis_index("x"), lax.axis_index("y"), lax.axis_index("c")


def _other_chips(x, y):
    return [(1 - x, y), (x, 1 - y), (1 - x, 1 - y)]


def _allgather8(block, name):
    m_per, n = block.shape

    def body(x_ref, out_ref, send_sems, recv_sems, local_sem):
        x, y, c = _place()
        me, sibling = (x, y, c), (x, y, 1 - c)
        chips = _other_chips(x, y)

        def rows(px, py, pc):
            return out_ref.at[pl.ds((4 * px + 2 * py + pc) * m_per, m_per), :]

        def copy(k, blk, to, src=None):
            return pltpu.make_async_remote_copy(
                src_ref=rows(*blk) if src is None else src, dst_ref=rows(*blk),
                send_sem=send_sems.at[k], recv_sem=recv_sems.at[k], device_id=to, device_id_type=_MESH)

        mine = pltpu.make_async_copy(x_ref, rows(*me), local_sem)
        mine.start()
        first = [copy(0, me, sibling, src=x_ref)]
        first += [copy(1 + j, me, (*chip, c), src=x_ref) for j, chip in enumerate(chips)]
        for cp in first:
            cp.start()
        passed = [copy(4 + j, (*chip, c), sibling) for j, chip in enumerate(chips)]
        for j, chip in enumerate(chips):
            copy(1 + j, (*chip, c), me).wait_recv()
            passed[j].start()
        copy(0, sibling, me).wait_recv()
        for j, chip in enumerate(chips):
            copy(4 + j, (*chip, 1 - c), me).wait_recv()
        for cp in first + passed:
            cp.wait_send()
        mine.wait()

    return pl.pallas_call(
        body, name=name, out_shape=jax.ShapeDtypeStruct((N_DEV * m_per, n), block.dtype),
        in_specs=[pl.BlockSpec(memory_space=pltpu.VMEM)], out_specs=pl.BlockSpec(memory_space=pltpu.VMEM),
        scratch_shapes=[pltpu.SemaphoreType.DMA((7,)), pltpu.SemaphoreType.DMA((7,)), pltpu.SemaphoreType.DMA],
        compiler_params=_cparams(),
    )(block)


def _split_axis(shape):
    return next(a for a, n in enumerate(shape) if n > 1)


def _weights_allgather(shards, chip_axes, name):
    nt = len(shards)
    cut = [_split_axis(s.shape) for s in shards]
    out_shapes = [s.shape[:a] + (N_CHIPS,) + s.shape[a:] for s, a in zip(shards, chip_axes)]

    def body(*refs):
        ins, outs = refs[:nt], refs[nt:2 * nt]
        send_sems, recv_sems, local_sems = refs[2 * nt:]
        x, y, c = _place()
        me, sibling = (x, y, c), (x, y, 1 - c)
        chips = _other_chips(x, y)

        def half_of_shard(t, pc):
            n = shards[t].shape[cut[t]] // 2
            idx = [slice(None)] * shards[t].ndim
            idx[cut[t]] = pl.ds(pc * n, n)
            return ins[t].at[tuple(idx)]

        def place(t, px, py, pc):
            n = shards[t].shape[cut[t]] // 2
            idx = [slice(None)] * shards[t].ndim
            idx[cut[t]] = pl.ds(pc * n, n)
            idx.insert(chip_axes[t], 2 * px + py)
            return outs[t].at[tuple(idx)]

        def copy(t, k, blk, to, src=None):
            return pltpu.make_async_remote_copy(
                src_ref=place(t, *blk) if src is None else src, dst_ref=place(t, *blk),
                send_sem=send_sems.at[7 * t + k], recv_sem=recv_sems.at[7 * t + k],
                device_id=to, device_id_type=_MESH)

        mine = [pltpu.make_async_copy(half_of_shard(t, c), place(t, *me), local_sems.at[t]) for t in range(nt)]
        for cp in mine:
            cp.start()
        first = []
        for t in range(nt):
            src = half_of_shard(t, c)
            first.append(copy(t, 0, me, sibling, src=src))
            first += [copy(t, 1 + j, me, (*chip, c), src=src) for j, chip in enumerate(chips)]
        for cp in first:
            cp.start()
        passed = []
        for t in range(nt):
            for j, chip in enumerate(chips):
                copy(t, 1 + j, (*chip, c), me).wait_recv()
                fwd = copy(t, 4 + j, (*chip, c), sibling)
                fwd.start()
                passed.append(fwd)
        for t in range(nt):
            copy(t, 0, sibling, me).wait_recv()
            for j, chip in enumerate(chips):
                copy(t, 4 + j, (*chip, 1 - c), me).wait_recv()
        for cp in first + passed:
            cp.wait_send()
        for cp in mine:
            cp.wait()

    return pl.pallas_call(
        body, name=name, out_shape=[jax.ShapeDtypeStruct(s, t.dtype) for s, t in zip(out_shapes, shards)],
        in_specs=[_ANY] * nt, out_specs=[_ANY] * nt,
        scratch_shapes=[pltpu.SemaphoreType.DMA((7 * nt,)), pltpu.SemaphoreType.DMA((7 * nt,)),
                        pltpu.SemaphoreType.DMA((nt,))],
        compiler_params=_cparams(),
    )(*shards)


def _pair_exchange(grads, name):
    nt = len(grads)

    def body(*refs):
        ins, outs = refs[:nt], refs[nt:2 * nt]
        send_sems, recv_sems = refs[2 * nt:]
        x, y, c = _place()
        copies = []
        for t in range(nt):
            h = grads[t].shape[1] // 2
            copies.append(pltpu.make_async_remote_copy(
                src_ref=ins[t].at[:, pl.ds((1 - c) * h, h), :], dst_ref=outs[t],
                send_sem=send_sems.at[t], recv_sem=recv_sems.at[t], device_id=(x, y, 1 - c), device_id_type=_MESH))
        for cp in copies:
            cp.start()
        for cp in copies:
            cp.wait()

    return pl.pallas_call(
        body, name=name,
        out_shape=[jax.ShapeDtypeStruct((N_CHIPS, g.shape[1] // 2, g.shape[2]), g.dtype) for g in grads],
        in_specs=[_ANY] * nt, out_specs=[_ANY] * nt,
        scratch_shapes=[pltpu.SemaphoreType.DMA((nt,)), pltpu.SemaphoreType.DMA((nt,))],
        compiler_params=_cparams(),
    )(*grads)


def _pair_sum(own, recv, c_idx, name):
    _, h, cols = recv.shape

    def body(c_ref, own_ref, recv_ref, o_ref):
        o_ref[...] = (own_ref[...] + recv_ref[...]).astype(BF16)

    return pl.pallas_call(
        body, name=name,
        grid_spec=pltpu.PrefetchScalarGridSpec(
            num_scalar_prefetch=1, grid=(N_CHIPS,),
            in_specs=[pl.BlockSpec((None, h, cols), lambda k, c_ref: (k, c_ref[0], 0)),
                      pl.BlockSpec((None, h, cols), lambda k, c_ref: (k, 0, 0))],
            out_specs=pl.BlockSpec((None, h, cols), lambda k, c_ref: (k, 0, 0))),
        out_shape=jax.ShapeDtypeStruct(recv.shape, BF16), compiler_params=_cparams(("arbitrary",)),
    )(c_idx, own, recv)


def _chip_exchange(parts, name):
    nt = len(parts)

    def body(*refs):
        ins, outs = refs[:nt], refs[nt:2 * nt]
        send_sems, recv_sems, local_sems = refs[2 * nt:]
        x, y, c = _place()
        my_chip = 2 * x + y
        chips = _other_chips(x, y)
        local = [pltpu.make_async_copy(ins[t].at[my_chip], outs[t].at[my_chip], local_sems.at[t]) for t in range(nt)]
        for cp in local:
            cp.start()
        sends, recvs = [], []
        for t in range(nt):
            for j, (px, py) in enumerate(chips):
                k = 3 * t + j
                sends.append(pltpu.make_async_remote_copy(
                    src_ref=ins[t].at[2 * px + py], dst_ref=outs[t].at[my_chip],
                    send_sem=send_sems.at[k], recv_sem=recv_sems.at[k], device_id=(px, py, c), device_id_type=_MESH))
                recvs.append(pltpu.make_async_remote_copy(
                    src_ref=ins[t].at[my_chip], dst_ref=outs[t].at[2 * px + py],
                    send_sem=send_sems.at[k], recv_sem=recv_sems.at[k], device_id=(px, py, c), device_id_type=_MESH))
        for cp in sends:
            cp.start()
        for cp in recvs:
            cp.wait_recv()
        for cp in sends:
            cp.wait_send()
        for cp in local:
            cp.wait()

    return pl.pallas_call(
        body, name=name, out_shape=[jax.ShapeDtypeStruct(p.shape, p.dtype) for p in parts],
        in_specs=[_ANY] * nt, out_specs=[_ANY] * nt,
        scratch_shapes=[pltpu.SemaphoreType.DMA((3 * nt,)), pltpu.SemaphoreType.DMA((3 * nt,)),
                        pltpu.SemaphoreType.DMA((nt,))],
        compiler_params=_cparams(),
    )(*parts)


def _chip_sum(q, into, lead, c_idx, name):
    _, h, cols = q.shape

    def body(c_ref, q_ref, into_ref, o_ref):
        acc = q_ref[0].astype(F32)
        for k in range(1, N_CHIPS):
            acc = acc + q_ref[k].astype(F32)
        o_ref[...] = acc

    return pl.pallas_call(
        body, name=name,
        grid_spec=pltpu.PrefetchScalarGridSpec(
            num_scalar_prefetch=1, grid=(1,),
            in_specs=[pl.BlockSpec((N_CHIPS, h, cols), lambda g, c_ref: (0, 0, 0)), _ANY],
            out_specs=pl.BlockSpec((None,) * len(lead) + (h, cols), lambda g, c_ref: (*lead, c_ref[0], 0))),
        out_shape=jax.ShapeDtypeStruct(into.shape, F32), input_output_aliases={2: 0},
        compiler_params=_cparams(("arbitrary",)),
    )(c_idx, q, into)


def _pair_gather(bufs, homes, name):
    nt, nb = len(homes), len(bufs)

    def body(*refs):
        outs = refs[nb:2 * nb]
        send_sems, recv_sems = refs[2 * nb:]
        x, y, c = _place()

        def home(t, pc):
            o, lead, rows = homes[t]
            return outs[o].at[(*lead, pl.ds(pc * (rows // 2), rows // 2), slice(None))]

        def copy(t, pc):
            return pltpu.make_async_remote_copy(src_ref=home(t, pc), dst_ref=home(t, pc), send_sem=send_sems.at[t],
                                                recv_sem=recv_sems.at[t], device_id=(x, y, 1 - c), device_id_type=_MESH)

        sends = [copy(t, c) for t in range(nt)]
        for cp in sends:
            cp.start()
        for t in range(nt):
            copy(t, 1 - c).wait_recv()
        for cp in sends:
            cp.wait_send()

    return pl.pallas_call(
        body, name=name, out_shape=[jax.ShapeDtypeStruct(b.shape, b.dtype) for b in bufs],
        in_specs=[_ANY] * nb, out_specs=[_ANY] * nb, input_output_aliases={o: o for o in range(nb)},
        scratch_shapes=[pltpu.SemaphoreType.DMA((nt,)), pltpu.SemaphoreType.DMA((nt,))],
        compiler_params=_cparams(),
    )(*bufs)


def _sum_devices(g, name):
    def body(g_ref, o_ref):
        acc = g_ref[0:1, :]
        for d in range(1, N_DEV):
            acc = acc + g_ref[d:d + 1, :]
        o_ref[...] = acc
    return pl.pallas_call(body, name=name, out_shape=jax.ShapeDtypeStruct((1, g.shape[1]), F32),
                          compiler_params=_cparams())(g)


_WEIGHTS = ("w_cond", "b_cond", "norm_pre", "norm_post", "w_ffn_in", "w_ffn_out", "fox_w_in", "fox_b_f",
            "fox_w_out", "sconv_w_in", "sconv_conv_w", "sconv_w_out", "lru_w_in", "lru_conv_w", "lru_conv_b",
            "lru_w_a", "lru_b_a", "lru_w_x", "lru_b_x", "lru_lambda", "lru_w_out")
_BIG = (("w_ffn_in", False), ("w_ffn_out", True), ("fox_w_in", False), ("fox_w_out", True),
        ("sconv_w_in", False), ("sconv_w_out", True), ("lru_w_in", False), ("lru_w_out", True))
_SMALL = tuple(n for n in _WEIGHTS if n != "w_cond" and n not in dict(_BIG))
_COL_SHARDED_SMALL = ("norm_pre", "norm_post", "sconv_conv_w", "lru_conv_w", "lru_conv_b", "lru_lambda")


def _pack_rows(parts, rows=8):
    flat = jnp.concatenate([p.reshape(-1) for p in parts])
    width = -(-flat.size // (rows * 128)) * 128
    return jnp.pad(flat, (0, rows * width - flat.size)).reshape(rows, width)


def _unpack(flat, shapes):
    out, off = [], 0
    for shp in shapes:
        n = math.prod(shp)
        out.append(flat[off:off + n].reshape(shp))
        off += n
    return out


def _join_chips(g):
    g = jnp.moveaxis(g, 0, -2)
    return g.reshape(g.shape[:-2] + (g.shape[-2] * g.shape[-1],))


def _my_columns(full, chip):
    n = full.shape[-1] // N_CHIPS
    return lax.dynamic_slice_in_dim(full, chip * n, n, axis=full.ndim - 1)


def _block_diag(w):
    eye = jnp.eye(LRU_BLOCKS, dtype=w.dtype)
    return jnp.einsum("nij,nm->nimj", w, eye).reshape(D_MODEL, D_MODEL)


def _step(x, c, target, wts, mom, var):
    ix, iy, ic = _place()
    chip = 2 * ix + iy
    dev = 2 * chip + ic
    n_cond = wts["w_cond"].shape[2]

    small_shapes = [(D_MODEL,)] + [wts[n].shape for n in _COL_SHARDED_SMALL]
    g1 = _allgather8(_pack_rows([c[0]] + [wts[n] for n in _COL_SHARDED_SMALL]), "gather_small").reshape(N_DEV, -1)
    c_all = g1[:, :D_MODEL]
    per_chip = [jnp.stack(col) for col in zip(*[_unpack(g1[2 * k], small_shapes) for k in range(N_CHIPS)])]
    small_full = {n: _join_chips(v) for n, v in zip(_COL_SHARDED_SMALL, per_chip[1:])}

    c_pad = jnp.pad(c_all, ((0, COND_ROWS - N_DEV), (0, 0)))
    b_shard = _my_columns(wts["b_cond"], chip)[:, None, :]
    mod_part = _cond_fwd(c_pad, wts["w_cond"], b_shard, "cond_fwd")
    g2 = _allgather8(mod_part[:, :N_DEV].transpose(1, 0, 2).reshape(N_DEV, DEPTH * n_cond), "gather_mod")
    g2 = g2.reshape(N_DEV, N_DEV, DEPTH, n_cond)[0::2]
    mod = _join_chips(lax.dynamic_index_in_dim(g2, dev, axis=1, keepdims=False)).reshape(DEPTH, N_SUB, 3, D_MODEL)

    chip_axes = [wts[n].ndim - 2 if by_rows else 0 for n, by_rows in _BIG]
    gathered = dict(zip([n for n, _ in _BIG],
                        _weights_allgather([wts[n].astype(BF16) for n, _ in _BIG], chip_axes, "gather_weights")))

    def rows_joined(n):
        g = gathered[n]
        return g.reshape(g.shape[:-3] + (g.shape[-3] * g.shape[-2], g.shape[-1]))

    ffn_out, fox_out, sconv_out, lru_out = (rows_joined(n) for n in ("w_ffn_out", "fox_w_out", "sconv_w_out", "lru_w_out"))
    fox_in = jnp.pad(_join_chips(gathered["fox_w_in"]), ((0, 0), (0, 0), (0, FOX_PAD - 3 * D_MODEL - FOX_HEADS)))
    lru_ax = jnp.concatenate([_block_diag(wts["lru_w_a"][0]), _block_diag(wts["lru_w_x"][0])], axis=1).astype(BF16)
    mixers = [
        lambda j: {"w_in": _W(fox_in, (j,)), "w_out": _W(fox_out, (j,)), "b_f": wts["fox_b_f"][j][:, None]},
        lambda j: {"w_in": _W(gathered["sconv_w_in"], (j,), True), "w_out": _W(sconv_out, (j,)),
                   "conv_w": small_full["sconv_conv_w"][j]},
        lambda j: {"w_in": _W(gathered["lru_w_in"], (j,), True), "w_out": _W(lru_out, (j,)),
                   "conv_w": small_full["lru_conv_w"][j], "conv_b": small_full["lru_conv_b"],
                   "w_ax": _W(lru_ax), "b_a": wts["lru_b_a"].reshape(1, D_MODEL),
                   "b_x": wts["lru_b_x"].reshape(1, D_MODEL), "lam": small_full["lru_lambda"]},
    ]
    layers = [{"ffn_in": [_W(gathered["w_ffn_in"], (i, j), True) for j in range(2)],
               "ffn_out": [_W(ffn_out, (i, j)) for j in range(2)],
               "norm_pre": small_full["norm_pre"][i], "norm_post": small_full["norm_post"][i],
               "mixer": mixers[i % 3](i // 3)} for i in range(DEPTH)]

    loss_row, grad_x, dmod, lg = _local_step(x[0], target[0], mod, {"layers": layers})
    loss = lax.psum(loss_row[0, 0], ("x", "y", "c"))

    fox_layers = [i for i in range(DEPTH) if i % 3 == 0]
    sconv_g, lru_g = lg[1]["mixer"], lg[2]["mixer"]
    small_g = {
        "dmod": dmod, "norm_pre": jnp.stack([g["norm_pre"] for g in lg]), "norm_post": jnp.stack([g["norm_post"] for g in lg]),
        "fox_b_f": jnp.stack([lg[i]["mixer"]["b_f"][:, 0] for i in fox_layers]),
        "sconv_conv_w": sconv_g["conv_w"][None], "lru_conv_w": lru_g["conv_w"][None], "lru_conv_b": lru_g["conv_b"],
        "lru_w_a": lru_g["w_a"][None], "lru_b_a": lru_g["b_a"].reshape(1, LRU_BLOCKS, LRU_BLOCK_DIM),
        "lru_w_x": lru_g["w_x"][None], "lru_b_x": lru_g["b_x"].reshape(1, LRU_BLOCKS, LRU_BLOCK_DIM),
        "lru_lambda": lru_g["lam"]}
    g4 = _allgather8(_pack_rows(list(small_g.values())), "gather_small_grads").reshape(N_DEV, -1)
    summed = dict(zip(small_g, _unpack(_sum_devices(g4, "sum_small_grads")[0], [v.shape for v in small_g.values()])))
    grads = {n: (_my_columns(summed[n], chip) if n in _COL_SHARDED_SMALL else summed[n]) for n in _SMALL if n != "b_cond"}
    grads["b_cond"] = summed["dmod"].reshape(DEPTH, N_SUB * 3 * D_MODEL)

    dmod_all = g4[:, :dmod.size].reshape(N_DEV, DEPTH, N_SUB * 3 * D_MODEL)
    dmod_s = jnp.pad(_my_columns(dmod_all, chip).transpose(1, 0, 2), ((0, 0), (0, COND_PAD - N_DEV), (0, 0))).astype(BF16)
    c_t = jnp.pad(c_all.T, ((0, 0), (0, COND_PAD - N_DEV)))
    grads["w_cond"], d_cond, m_cond, v_cond = _cond_bwd_adamw(c_t, dmod_s, wts["w_cond"], mom["w_cond"],
                                                              var["w_cond"], "cond_bwd_adamw")

    def chip_blocks(g, by_rows, width):
        if by_rows:
            return g.reshape(N_CHIPS, g.shape[0] // N_CHIPS, g.shape[1])
        if g.ndim == 3:
            return g
        return g[:, :width * N_CHIPS].reshape(g.shape[0], N_CHIPS, width).transpose(1, 0, 2)

    tensors, homes = [], []
    for o, (n, by_rows) in enumerate(_BIG):
        width = wts[n].shape[-1]
        if n.startswith("w_ffn"):
            key = "ffn_out" if by_rows else "ffn_in"
            items = [((i, j), lg[i][key][j]) for i in range(DEPTH) for j in range(2)]
        else:
            kind = ("fox", "sconv", "lru").index(n.split("_")[0])
            key = "w_out" if by_rows else "w_in"
            items = [((i // 3,), lg[i]["mixer"][key]) for i in range(DEPTH) if i % 3 == kind]
        for lead, g in items:
            tensors.append(chip_blocks(g, by_rows, width))
            homes.append((o, lead, wts[n].shape[-2]))
    recv = _pair_exchange(tensors, "grads_pair_exchange")
    c_idx = ic.astype(jnp.int32).reshape(1)
    parts = [_pair_sum(t, r, c_idx, f"grads_pair_sum{k}") for k, (t, r) in enumerate(zip(tensors, recv))]
    arrived = _chip_exchange(parts, "grads_chip_exchange")
    bufs = [lax.empty(wts[n].shape, F32) for n, _ in _BIG]
    for k, (q, (o, lead, _)) in enumerate(zip(arrived, homes)):
        bufs[o] = _chip_sum(q, bufs[o], lead, c_idx, f"grads_chip_sum{k}")
    grads.update(zip([n for n, _ in _BIG], _pair_gather(bufs, homes, "grads_pair_gather")))

    delta, new_m, new_v = {"w_cond": d_cond}, {"w_cond": m_cond}, {"w_cond": v_cond}
    for n, _ in _BIG:
        two_d = lambda a: a.reshape(-1, a.shape[-1])
        d, nm, nv = _adamw(two_d(wts[n]), two_d(grads[n]), two_d(mom[n]), two_d(var[n]), "adamw_" + n)
        delta[n], new_m[n], new_v[n] = (a.reshape(wts[n].shape) for a in (d, nm, nv))
    shapes = [wts[n].shape for n in _SMALL]
    packed = [_pack_rows([src[n] for n in _SMALL]) for src in (wts, grads, mom, var)]
    for dst, out in zip((delta, new_m, new_v), _adamw(*packed, "adamw_small")):
        dst.update(zip(_SMALL, _unpack(out.reshape(-1), shapes)))

    return (loss, grad_x[None], *[grads[n] for n in _WEIGHTS], *[delta[n] for n in _WEIGHTS],
            *[new_m[n] for n in _WEIGHTS], *[new_v[n] for n in _WEIGHTS])


def kernel(x, c, w_cond, b_cond, norm_pre, norm_post, w_ffn_in, w_ffn_out, fox_w_in, fox_b_f, fox_w_out, sconv_w_in, sconv_conv_w, sconv_w_out, lru_w_in, lru_conv_w, lru_conv_b, lru_w_a, lru_b_a, lru_w_x, lru_b_x, lru_lambda, lru_w_out, loss_target, m_w_cond, m_b_cond, m_norm_pre, m_norm_post, m_w_ffn_in, m_w_ffn_out, m_fox_w_in, m_fox_b_f, m_fox_w_out, m_sconv_w_in, m_sconv_conv_w, m_sconv_w_out, m_lru_w_in, m_lru_conv_w, m_lru_conv_b, m_lru_w_a, m_lru_b_a, m_lru_w_x, m_lru_b_x, m_lru_lambda, m_lru_w_out, v_w_cond, v_b_cond, v_norm_pre, v_norm_post, v_w_ffn_in, v_w_ffn_out, v_fox_w_in, v_fox_b_f, v_fox_w_out, v_sconv_w_in, v_sconv_conv_w, v_sconv_w_out, v_lru_w_in, v_lru_conv_w, v_lru_conv_b, v_lru_w_a, v_lru_b_a, v_lru_w_x, v_lru_b_x, v_lru_lambda, v_lru_w_out):
    given = dict(locals())
    wts = {n: given[n] for n in _WEIGHTS}
    mom = {n: given["m_" + n] for n in _WEIGHTS}
    var = {n: given["v_" + n] for n in _WEIGHTS}
    return _step(x, c, loss_target, wts, mom, var)
```

```python
import functools
import math
from typing import NamedTuple

import jax
import jax.numpy as jnp
from jax import lax
from jax.experimental import pallas as pl
from jax.experimental.pallas import tpu as pltpu

F32 = jnp.float32
BF16 = jnp.bfloat16

D_MODEL = 1024
DEPTH = 4
N_SUB = 3
D_FF = 2816
RMS_EPS = 1e-6
FOX_HEADS = 16
FOX_HEAD_DIM = 64
FOX_PAD = 3200
LRU_BLOCKS = 16
LRU_BLOCK_DIM = 64
LRU_C = 8.0
N_CHIPS = 4
N_DEV = 8

ADAM_LR = 0.001
ADAM_B1 = 0.9
ADAM_B2 = 0.999
ADAM_EPS = 1e-08
ADAM_WD = 0.01
ADAM_STEP = 10

VMEM_LIMIT_V7X = 56 * 1024 * 1024
ROW_TILE = 256
COL_TILE = 256
ATT_TILE = 256
MM_ROWS = 256


def _cparams(sem=None):
    return pltpu.CompilerParams(vmem_limit_bytes=VMEM_LIMIT_V7X, dimension_semantics=sem)


def _sigmoid(z):
    return 1.0 / (1.0 + jnp.exp(-z))


def _softplus(z):
    return jnp.maximum(z, 0.0) + jnp.log(1.0 + jnp.exp(-jnp.abs(z)))


def _rows_sum(v):
    return jnp.sum(v, axis=0, keepdims=True)


class _W(NamedTuple):
    arr: jax.Array
    prefix: tuple = ()
    blocked: bool = False


def _w_spec(w, block2, pos):
    lead = (None,) * (len(w.prefix) + (1 if w.blocked else 0))
    if w.blocked:
        return pl.BlockSpec(lead + block2, lambda *g: (pos(*g)[0], *w.prefix, pos(*g)[1], pos(*g)[2]))
    return pl.BlockSpec(lead + block2, lambda *g: (*w.prefix, pos(*g)[1], pos(*g)[2]))


def _mm_nn(a, b, name, tn=None):
    m, k = a.shape
    if b.blocked:
        steps, bn = b.arr.shape[0], b.arr.shape[-1]
        b_spec = _w_spec(b, (k, bn), lambda n: (n, 0, 0))
    else:
        n_total = b.arr.shape[-1]
        bn = n_total if tn is None else tn
        steps = n_total // bn
        assert steps * bn == n_total
        b_spec = _w_spec(b, (k, bn), lambda n: (0, 0, n))
    tm = min(MM_ROWS, m)

    def body(a_ref, b_ref, o_ref):
        def step(i, carry):
            r = pl.ds(pl.multiple_of(i * tm, tm), tm)
            o_ref[r, :] = jnp.dot(a_ref[r, :], b_ref[...], preferred_element_type=F32)
            return carry
        lax.fori_loop(0, m // tm, step, 0)

    return pl.pallas_call(
        body, name=name, grid=(steps,),
        in_specs=[pl.BlockSpec((m, k), lambda n: (0, 0)), b_spec],
        out_specs=pl.BlockSpec((m, bn), lambda n: (0, n)),
        out_shape=jax.ShapeDtypeStruct((m, steps * bn), F32),
        compiler_params=_cparams(("arbitrary",)),
    )(a, b.arr)


def _mm_nt(dy, w, name, tk=None, tn=None):
    m, n_total = dy.shape
    k = w.arr.shape[-2]
    if w.blocked:
        bk, bn = k, w.arr.shape[-1]
        grid = (1, w.arr.shape[0])
        w_spec = _w_spec(w, (k, bn), lambda kt, n: (n, 0, 0))
    else:
        bk = k if tk is None else tk
        bn = n_total if tn is None else tn
        grid = (k // bk, n_total // bn)
        assert grid[0] * bk == k and grid[1] * bn == n_total
        w_spec = _w_spec(w, (bk, bn), lambda kt, n: (0, kt, n))
    tm = min(MM_ROWS, m)

    def body(dy_ref, w_ref, o_ref):
        def step(i, carry):
            r = pl.ds(pl.multiple_of(i * tm, tm), tm)
            o_ref[r, :] += lax.dot_general(dy_ref[r, :], w_ref[...], (((1,), (1,)), ((), ())),
                                           preferred_element_type=F32)
            return carry

        @pl.when(pl.program_id(1) == 0)
        def _():
            o_ref[...] = jnp.zeros_like(o_ref)
        lax.fori_loop(0, m // tm, step, 0)

    return pl.pallas_call(
        body, name=name, grid=grid,
        in_specs=[pl.BlockSpec((m, bn), lambda kt, n: (0, n)), w_spec],
        out_specs=pl.BlockSpec((m, bk), lambda kt, n: (0, kt)),
        out_shape=jax.ShapeDtypeStruct((m, k), F32),
        compiler_params=_cparams(("arbitrary", "arbitrary")),
    )(dy, w.arr)


def _mm_tn(x, dy, name, tk=None, tn=None, blocked_out=False):
    s, k = x.shape
    n_total = dy.shape[1]
    bk = k if tk is None else tk
    bn = n_total if tn is None else tn
    grid = (k // bk, n_total // bn)
    assert grid[0] * bk == k and grid[1] * bn == n_total
    ck = 256 if bk % 256 == 0 else 128

    def body(x_ref, dy_ref, o_ref):
        def step(i, carry):
            c = pl.ds(pl.multiple_of(i * ck, ck), ck)
            o_ref[c, :] = lax.dot_general(x_ref[:, c], dy_ref[...], (((0,), (0,)), ((), ())),
                                          preferred_element_type=F32)
            return carry
        lax.fori_loop(0, bk // ck, step, 0)

    if blocked_out:
        assert grid[0] == 1
        out_spec = pl.BlockSpec((None, bk, bn), lambda kt, n: (n, 0, 0))
        out_shape = jax.ShapeDtypeStruct((grid[1], k, bn), F32)
    else:
        out_spec = pl.BlockSpec((bk, bn), lambda kt, n: (kt, n))
        out_shape = jax.ShapeDtypeStruct((k, n_total), F32)
    return pl.pallas_call(
        body, name=name, grid=grid,
        in_specs=[pl.BlockSpec((s, bk), lambda kt, n: (0, kt)), pl.BlockSpec((s, bn), lambda kt, n: (0, n))],
        out_specs=out_spec, out_shape=out_shape,
        compiler_params=_cparams(("arbitrary", "arbitrary")),
    )(x, dy)


def _row_call(name, body, rows, fulls, row_outs, acc_outs, tr=ROW_TILE, after=None):
    s = rows[0].shape[0]
    tr = min(tr, s)
    in_specs = [pl.BlockSpec((tr, a.shape[1]), lambda i: (i, 0)) for a in rows]
    in_specs += [pl.BlockSpec(a.shape, lambda i: (0, 0)) for a in fulls]
    n_in = len(in_specs)
    order = [] if after is None else [after]
    in_specs += [pl.BlockSpec(memory_space=pl.ANY)] * len(order)
    out_specs = [pl.BlockSpec((tr, c), lambda i: (i, 0)) for c, _ in row_outs]
    out_specs += [pl.BlockSpec((1, c), lambda i: (0, 0)) for c, _ in acc_outs]
    out_shape = [jax.ShapeDtypeStruct((s, c), dt) for c, dt in row_outs]
    out_shape += [jax.ShapeDtypeStruct((1, c), dt) for c, dt in acc_outs]
    n_acc = len(acc_outs)

    def wrapped(*refs):
        refs = refs[:n_in] + refs[n_in + len(order):]
        if n_acc:
            @pl.when(pl.program_id(0) == 0)
            def _():
                for r in refs[len(refs) - n_acc:]:
                    r[...] = jnp.zeros_like(r)
        body(*refs)

    return pl.pallas_call(
        wrapped, name=name, grid=(s // tr,), in_specs=in_specs, out_specs=out_specs, out_shape=out_shape,
        compiler_params=_cparams(("arbitrary",)),
    )(*rows, *fulls, *order)


def _rms(v):
    return lax.rsqrt(jnp.mean(v * v, axis=-1, keepdims=True) + RMS_EPS)


def _pre_norm(x, g_pre, scale, shift, name, after=None):
    def body(x_ref, g_ref, sc_ref, sh_ref, h_ref):
        xv = x_ref[...]
        h = (xv * _rms(xv)) * g_ref[...] * (1.0 + sc_ref[...]) + sh_ref[...]
        h_ref[...] = h.astype(BF16)
    return _row_call(name, body, [x], [g_pre, scale, shift], [(D_MODEL, BF16)], [], after=after)[0]


def _post_norm(x, y, g_post, gate, coef, name):
    def body(x_ref, y_ref, g_ref, gate_ref, o_ref):
        yv = y_ref[...]
        o_ref[...] = x_ref[...] + (coef * gate_ref[...]) * ((yv * _rms(yv)) * g_ref[...])
    return _row_call(name, body, [x, y], [g_post, gate], [(D_MODEL, F32)], [])[0]


def _post_norm_bwd(dxo, y, g_post, gate, coef, name, after=None):
    def body(dxo_ref, y_ref, g_ref, gate_ref, dy_ref, dgate_ref, dg_ref):
        yv = y_ref[...]
        r2 = _rms(yv)
        yn = yv * r2
        dxo_v = dxo_ref[...]
        dgate_ref[...] += _rows_sum(dxo_v * (yn * g_ref[...])) * coef
        dz = dxo_v * (coef * gate_ref[...])
        dg_ref[...] += _rows_sum(dz * yn)
        dyn = dz * g_ref[...]
        dy = r2 * (dyn - yn * jnp.mean(dyn * yn, axis=-1, keepdims=True))
        dy_ref[...] = dy.astype(BF16)
    return _row_call(name, body, [dxo, y], [g_post, gate], [(D_MODEL, BF16)], [(D_MODEL, F32), (D_MODEL, F32)],
                     after=after)


def _pre_norm_bwd(dxo, dh, x, g_pre, scale, name):
    def body(dxo_ref, dh_ref, x_ref, g_ref, sc_ref, dx_ref, dshift_ref, dscale_ref, dg_ref):
        xv = x_ref[...]
        r = _rms(xv)
        xn = xv * r
        dh_v = dh_ref[...]
        one_sc = 1.0 + sc_ref[...]
        dshift_ref[...] += _rows_sum(dh_v)
        dscale_ref[...] += _rows_sum(dh_v * (xn * g_ref[...]))
        dg_ref[...] += _rows_sum(dh_v * xn * one_sc)
        dxn = dh_v * (g_ref[...] * one_sc)
        dx_ref[...] = dxo_ref[...] + r * (dxn - xn * jnp.mean(dxn * xn, axis=-1, keepdims=True))
    return _row_call(name, body, [dxo, dh, x], [g_pre, scale], [(D_MODEL, F32)],
                     [(D_MODEL, F32), (D_MODEL, F32), (D_MODEL, F32)])


def _swiglu_act(gu, name):
    def body(gu_ref, a_ref):
        g = gu_ref[:, :D_FF]
        a_ref[...] = (g * _sigmoid(g) * gu_ref[:, D_FF:]).astype(BF16)
    return _row_call(name, body, [gu], [], [(D_FF, BF16)], [])[0]


def _swiglu_act_bwd(da, gu, name):
    def body(da_ref, gu_ref, dgu_ref):
        g = gu_ref[:, :D_FF]
        sg = _sigmoid(g)
        da_v = da_ref[...]
        dgu_ref[:, :D_FF] = (da_v * gu_ref[:, D_FF:] * (sg * (1.0 + g * (1.0 - sg)))).astype(BF16)
        dgu_ref[:, D_FF:] = (da_v * (g * sg)).astype(BF16)
    return _row_call(name, body, [da, gu], [], [(2 * D_FF, BF16)], [])[0]


def _loss_head(y, target, name):
    def body(y_ref, t_ref, dy_ref, loss_ref):
        e = y_ref[...] - t_ref[...]
        dy_ref[...] = e * (1.0 / D_MODEL)
        part = jnp.sum(jnp.mean(e * e, axis=-1, keepdims=True), axis=0, keepdims=True) * 0.5
        loss_ref[...] += jnp.broadcast_to(part, loss_ref.shape)
    return _row_call(name, body, [y, target], [], [(D_MODEL, F32)], [(128, F32)])


def _lane_scan(v, reverse):
    s = v.shape[1]
    lane = lax.broadcasted_iota(jnp.int32, v.shape, 1)
    d = 1
    while d < s:
        if reverse:
            v = v + jnp.where(lane < s - d, pltpu.roll(v, s - d, 1), 0.0)
        else:
            v = v + jnp.where(lane >= d, pltpu.roll(v, d, 1), 0.0)
        d *= 2
    return v


def _fox_gate(flt, b_f, name):
    def body(f_ref, b_ref, cum_ref):
        z = f_ref[...] + b_ref[...]
        cum_ref[...] = _lane_scan(-_softplus(-z), reverse=False)
    return pl.pallas_call(body, name=name, out_shape=jax.ShapeDtypeStruct(flt.shape, F32),
                          compiler_params=_cparams())(flt, b_f)


def _fox_gate_bwd(dcum_q, dcum_k, flt, b_f, name):
    def body(dq_ref, dk_ref, f_ref, b_ref, df_ref, db_ref):
        z = f_ref[...] + b_ref[...]
        df = _lane_scan(dq_ref[...] + dk_ref[...], reverse=True) * _sigmoid(-z)
        df_ref[...] = df
        db_ref[...] = jnp.sum(df, axis=1, keepdims=True)
    h = flt.shape[0]
    return pl.pallas_call(body, name=name,
                          out_shape=(jax.ShapeDtypeStruct(flt.shape, F32), jax.ShapeDtypeStruct((h, 1), F32)),
                          compiler_params=_cparams())(dcum_q, dcum_k, flt, b_f)


def _pick_head(block, h):
    lane = lax.broadcasted_iota(jnp.int32, block.shape, 1)
    return jnp.sum(jnp.where(lane == h, block, 0.0), axis=1, keepdims=True)


def _put_head(ref, col, h):
    @pl.when(h == 0)
    def _():
        ref[...] = jnp.zeros_like(ref)
    lane = lax.broadcasted_iota(jnp.int32, ref.shape, 1)
    ref[...] = jnp.where(lane == h, col, ref[...])


_NT = (((1,), (1,)), ((), ()))
_FOX_SCALE = FOX_HEAD_DIM ** -0.5


def _causal(s_tile, t):
    row = lax.broadcasted_iota(jnp.int32, (t, t), 0)
    col = lax.broadcasted_iota(jnp.int32, (t, t), 1)
    return jnp.where(col <= row, s_tile, -jnp.inf)


HEAD_PAIRS = FOX_HEADS // 2
PAIR_W = 2 * FOX_HEAD_DIM


def _low_half(shape):
    return lax.broadcasted_iota(jnp.int32, shape, 1) < FOX_HEAD_DIM


def _fox_attn_fwd(qkv, cum, cum_t, name):
    s = qkv.shape[0]
    t = min(ATT_TILE, s)

    def body(q_ref, k_ref, v_ref, cum_ref, cumt_ref, o_ref, ob_ref, lse_ref):
        i = pl.program_id(0)
        hp = pl.program_id(1)
        lo = _low_half((t, PAIR_W))
        qv = q_ref[...]
        zero = jnp.zeros_like(qv)
        q2 = (jnp.where(lo, qv, zero), jnp.where(lo, zero, qv))
        cum_v = cum_ref[...]
        cq2 = (_pick_head(cum_v, 2 * hp), _pick_head(cum_v, 2 * hp + 1))

        def step(j, carry, masked):
            ks = pl.ds(pl.multiple_of(j * t, t), t)
            kj = k_ref[ks, :]
            vj = v_ref[ks, :]
            out = []
            for e in range(2):
                m, l, acc = carry[e]
                sc = lax.dot_general(q2[e], kj, _NT, preferred_element_type=F32) * _FOX_SCALE
                sc = sc + cq2[e] - cumt_ref[e:e + 1, ks]
                if masked:
                    sc = _causal(sc, t)
                m_new = jnp.maximum(m, jnp.max(sc, axis=1, keepdims=True))
                alpha = jnp.exp(m - m_new)
                p = jnp.exp(sc - m_new)
                l = alpha * l + jnp.sum(p, axis=1, keepdims=True)
                acc = alpha * acc + jnp.dot(p.astype(BF16), vj, preferred_element_type=F32)
                out.append((m_new, l, acc))
            return tuple(out)

        one = (jnp.full((t, 1), -jnp.inf, F32), jnp.zeros((t, 1), F32), jnp.zeros((t, PAIR_W), F32))
        carry = lax.fori_loop(0, i, lambda j, c: step(j, c, False), (one, one))
        (m0, l0, a0), (m1, l1, a1) = step(i, carry, True)
        o = jnp.where(lo, a0 / l0, a1 / l1)
        o_ref[...] = o
        ob_ref[...] = o.astype(BF16)
        _put_head(lse_ref, m0 + jnp.log(l0), 2 * hp)
        _put_head(lse_ref, m1 + jnp.log(l1), 2 * hp + 1)

    nat_tile = pl.BlockSpec((t, FOX_HEADS), lambda i, hp: (i, 0))
    out_tile = pl.BlockSpec((t, PAIR_W), lambda i, hp: (i, hp))
    return pl.pallas_call(
        body, name=name, grid=(s // t, HEAD_PAIRS),
        in_specs=[pl.BlockSpec((t, PAIR_W), lambda i, hp: (i, hp)),
                  pl.BlockSpec((s, PAIR_W), lambda i, hp: (0, HEAD_PAIRS + hp)),
                  pl.BlockSpec((s, PAIR_W), lambda i, hp: (0, 2 * HEAD_PAIRS + hp)),
                  nat_tile, pl.BlockSpec((None, 2, s), lambda i, hp: (hp, 0, 0))],
        out_specs=[out_tile, out_tile, nat_tile],
        out_shape=[jax.ShapeDtypeStruct((s, D_MODEL), F32), jax.ShapeDtypeStruct((s, D_MODEL), BF16),
                   jax.ShapeDtypeStruct((s, FOX_HEADS), F32)],
        compiler_params=_cparams(("arbitrary", "arbitrary")),
    )(qkv, qkv, qkv, cum, cum_t)


def _fox_delta(do, o, expand, name):
    def body(do_ref, o_ref, e_ref, d_ref):
        prod = do_ref[...] * o_ref[...]
        hi = prod.astype(BF16)
        lo = (prod - hi.astype(F32)).astype(BF16)
        tot = (jnp.dot(hi, e_ref[...], preferred_element_type=F32)
               + jnp.dot(lo, e_ref[...], preferred_element_type=F32))
        d_ref[...] = tot[:, :FOX_HEADS]
    return _row_call(name, body, [do, o], [expand], [(FOX_HEADS, F32)], [])[0]


def _fox_attn_bwd(qkv, do, cum, cum_t, lse_t, delta_t, name):
    s = qkv.shape[0]
    t = min(ATT_TILE, s)
    nq = s // t
    tn_dims = (((0,), (0,)), ((), ()))

    def body(q_ref, k_ref, v_ref, do_ref, cum_ref, cumt_ref, lset_ref, deltat_ref,
             dq_ref, dk_ref, dv_ref, dck_ref, dcq_ref):
        hp = pl.program_id(0)
        j = pl.program_id(1)

        @pl.when(j == 0)
        def _():
            dq_ref[...] = jnp.zeros_like(dq_ref)
            dcq_ref[...] = jnp.zeros_like(dcq_ref)
        dk_ref[...] = jnp.zeros_like(dk_ref)
        dv_ref[...] = jnp.zeros_like(dv_ref)

        lo = _low_half((t, PAIR_W))
        lane = lax.broadcasted_iota(jnp.int32, (t, PAIR_W), 1)
        kv = k_ref[...]
        vv = v_ref[...]
        zero = jnp.zeros_like(kv)
        k2 = (jnp.where(lo, kv, zero), jnp.where(lo, zero, kv))
        v2 = (jnp.where(lo, vv, zero), jnp.where(lo, zero, vv))
        cum_v = cum_ref[...]
        ck2 = (_pick_head(cum_v, 2 * hp), _pick_head(cum_v, 2 * hp + 1))

        def step(i, dck, masked):
            qs = pl.ds(pl.multiple_of(i * t, t), t)
            qi = q_ref[qs, :]
            do_i = do_ref[qs, :].astype(BF16)
            dv_p, dk_p, dq_p = [], [], []
            for e in range(2):
                st = lax.dot_general(k2[e], qi, _NT, preferred_element_type=F32) * _FOX_SCALE
                st = st + cumt_ref[e:e + 1, qs] - ck2[e]
                if masked:
                    row = lax.broadcasted_iota(jnp.int32, (t, t), 0)
                    col = lax.broadcasted_iota(jnp.int32, (t, t), 1)
                    st = jnp.where(row <= col, st, -jnp.inf)
                pt = jnp.exp(st - lset_ref[e:e + 1, qs])
                dv_p.append(jnp.dot(pt.astype(BF16), do_i, preferred_element_type=F32))
                dpt = lax.dot_general(v2[e], do_i, _NT, preferred_element_type=F32)
                dst = pt * (dpt - deltat_ref[e:e + 1, qs])
                dsb = dst.astype(BF16)
                dk_p.append(jnp.dot(dsb, qi, preferred_element_type=F32))
                dq_p.append(lax.dot_general(dsb, kv, tn_dims, preferred_element_type=F32))
                dck = dck - jnp.where(lane == e, jnp.sum(dst, axis=1, keepdims=True), 0.0)
                dcq_ref[e:e + 1, qs] += jnp.sum(dst, axis=0, keepdims=True)
            dv_ref[...] += jnp.where(lo, dv_p[0], dv_p[1])
            dk_ref[...] += jnp.where(lo, dk_p[0], dk_p[1])
            dq_ref[qs, :] += jnp.where(lo, dq_p[0], dq_p[1]) * _FOX_SCALE
            return dck

        dck = step(j, jnp.zeros((t, PAIR_W), F32), True)
        dck = lax.fori_loop(j + 1, nq, lambda i, c: step(i, c, False), dck)
        dk_ref[...] = dk_ref[...] * _FOX_SCALE
        dck_ref[...] = dck

    pair_full = lambda part: pl.BlockSpec((s, PAIR_W), lambda hp, j: (0, part * HEAD_PAIRS + hp))
    pair_tile = lambda part: pl.BlockSpec((t, PAIR_W), lambda hp, j: (j, part * HEAD_PAIRS + hp))
    rows = pl.BlockSpec((None, 2, s), lambda hp, j: (hp, 0, 0))
    return pl.pallas_call(
        body, name=name, grid=(HEAD_PAIRS, nq),
        in_specs=[pair_full(0), pair_tile(1), pair_tile(2), pair_full(0),
                  pl.BlockSpec((t, FOX_HEADS), lambda hp, j: (j, 0)), rows, rows, rows],
        out_specs=[pair_full(0), pair_tile(0), pair_tile(0),
                   pl.BlockSpec((None, t, PAIR_W), lambda hp, j: (hp, j, 0)), rows],
        out_shape=[jax.ShapeDtypeStruct((s, D_MODEL), F32)] * 3
        + [jax.ShapeDtypeStruct((HEAD_PAIRS, s, PAIR_W), F32), jax.ShapeDtypeStruct((HEAD_PAIRS, 2, s), F32)],
        compiler_params=_cparams(("arbitrary", "arbitrary")),
    )(qkv, qkv, qkv, do, cum, cum_t, lse_t, delta_t)


def _shift_down(v, d):
    row = lax.broadcasted_iota(jnp.int32, v.shape, 0)
    return jnp.where(row >= d, pltpu.roll(v, d, 0), 0.0)


def _shift_up(v, d):
    s = v.shape[0]
    row = lax.broadcasted_iota(jnp.int32, v.shape, 0)
    return jnp.where(row < s - d, pltpu.roll(v, s - d, 0), 0.0)


def _conv_taps(v, cw_ref, width):
    out = cw_ref[width - 1:width, :] * v
    for k in range(width - 1):
        out = out + cw_ref[k:k + 1, :] * _shift_down(v, width - 1 - k)
    return out


def _conv_taps_bwd(dout, v, cw_ref, dcw_ref, width):
    dv = cw_ref[width - 1:width, :] * dout
    dcw_ref[width - 1:width, :] = _rows_sum(dout * v)
    for k in range(width - 1):
        d = width - 1 - k
        dv = dv + cw_ref[k:k + 1, :] * _shift_up(dout, d)
        dcw_ref[k:k + 1, :] = _rows_sum(dout * _shift_down(v, d))
    return dv


def _col_spec(s, tc, part=0):
    off = part * (D_MODEL // tc)
    return pl.BlockSpec((s, tc), lambda c: (0, c + off))


def _small_spec(rows, tc):
    return pl.BlockSpec((rows, tc), lambda c: (0, c))


def _col_call(name, body, in_arrays, in_specs, out_rows, s, tc):
    return pl.pallas_call(
        body, name=name, grid=(D_MODEL // tc,), in_specs=in_specs,
        out_specs=[pl.BlockSpec((r, tc), lambda c: (0, c)) for r, _ in out_rows],
        out_shape=[jax.ShapeDtypeStruct((r, D_MODEL), dt) for r, dt in out_rows],
        compiler_params=_cparams(("arbitrary",)),
    )(*in_arrays)


def _sconv_fwd(proj, conv_w, name):
    s = proj.shape[0]
    tc = COL_TILE

    def body(b_ref, c_ref, x_ref, cw_ref, y_ref):
        y_ref[...] = (b_ref[...] * _conv_taps(c_ref[...] * x_ref[...], cw_ref, 3)).astype(BF16)

    return _col_call(name, body, [proj, proj, proj, conv_w],
                     [_col_spec(s, tc, 0), _col_spec(s, tc, 1), _col_spec(s, tc, 2), _small_spec(3, tc)],
                     [(s, BF16)], s, tc)[0]


def _sconv_bwd(dy, proj, conv_w, name):
    s = proj.shape[0]
    tc = COL_TILE

    def body(dy_ref, b_ref, c_ref, x_ref, cw_ref, db_ref, dc_ref, dx_ref, dcw_ref):
        w = c_ref[...] * x_ref[...]
        dy_v = dy_ref[...]
        db_ref[...] = (dy_v * _conv_taps(w, cw_ref, 3)).astype(BF16)
        dw = _conv_taps_bwd(dy_v * b_ref[...], w, cw_ref, dcw_ref, 3)
        dc_ref[...] = (dw * x_ref[...]).astype(BF16)
        dx_ref[...] = (dw * c_ref[...]).astype(BF16)

    return _col_call(name, body, [dy, proj, proj, proj, conv_w],
                     [_col_spec(s, tc), _col_spec(s, tc, 0), _col_spec(s, tc, 1), _col_spec(s, tc, 2),
                      _small_spec(3, tc)],
                     [(s, BF16), (s, BF16), (s, BF16), (3, F32)], s, tc)


def _lru_conv(proj, conv_w, conv_b, name):
    s = proj.shape[0]
    tc = COL_TILE

    def body(x_ref, cw_ref, cb_ref, xb_ref, xbb_ref):
        xb = _conv_taps(x_ref[...], cw_ref, 4) + cb_ref[...]
        xb_ref[...] = xb
        xbb_ref[...] = xb.astype(BF16)

    return _col_call(name, body, [proj, conv_w, conv_b],
                     [_col_spec(s, tc, 1), _small_spec(4, tc), _small_spec(1, tc)],
                     [(s, F32), (s, BF16)], s, tc)


def _lru_conv_bwd(dxb1, dxb2, proj, conv_w, name):
    s = proj.shape[0]
    tc = COL_TILE

    def body(d1_ref, d2_ref, x_ref, cw_ref, dx_ref, dcw_ref, dcb_ref):
        dxb = d1_ref[...] + d2_ref[...]
        dcb_ref[...] = _rows_sum(dxb)
        dx_ref[...] = _conv_taps_bwd(dxb, x_ref[...], cw_ref, dcw_ref, 4).astype(BF16)

    return _col_call(name, body, [dxb1, dxb2, proj, conv_w],
                     [_col_spec(s, tc), _col_spec(s, tc), _col_spec(s, tc, 1), _small_spec(4, tc)],
                     [(s, BF16), (4, F32), (1, F32)], s, tc)


_GELU_C = math.sqrt(2.0 / math.pi)


def _gelu_parts(g):
    inner = _GELU_C * (g + 0.044715 * g * g * g)
    th = jnp.tanh(inner)
    val = 0.5 * g * (1.0 + th)
    der = 0.5 * (1.0 + th) + 0.5 * g * (1.0 - th * th) * (_GELU_C * (1.0 + 3.0 * 0.044715 * g * g))
    return val, der


def _lru_gates(pa_ref, px_ref, ba_ref, bx_ref, lam_ref):
    r = _sigmoid(pa_ref[...] + ba_ref[...])
    ig = _sigmoid(px_ref[...] + bx_ref[...])
    sp = _softplus(-lam_ref[...])
    log_a = (-LRU_C) * r * sp
    a = jnp.exp(log_a)
    z = 2.0 * log_a
    one_m_a2 = jnp.where(z > -1e-3, -(z * (1.0 + z * (0.5 + z * (1.0 / 6.0)))), 1.0 - jnp.exp(z))
    return r, ig, sp, a, jnp.sqrt(one_m_a2)


def _lru_scan(pre, xb, proj, b_a, b_x, lam, name):
    s = xb.shape[0]
    tc = COL_TILE

    def body(pa_ref, px_ref, xb_ref, g_ref, ba_ref, bx_ref, lam_ref, y_ref, hs_ref):
        _, ig, _, a, mult = _lru_gates(pa_ref, px_ref, ba_ref, bx_ref, lam_ref)
        b = mult * (ig * xb_ref[...])
        d = 1
        while d < s:
            row = lax.broadcasted_iota(jnp.int32, a.shape, 0)
            keep = row >= d
            b = b + a * jnp.where(keep, pltpu.roll(b, d, 0), 0.0)
            a = a * jnp.where(keep, pltpu.roll(a, d, 0), 1.0)
            d *= 2
        hs_ref[...] = b
        y_ref[...] = (b * _gelu_parts(g_ref[...])[0]).astype(BF16)

    return _col_call(name, body, [pre, pre, xb, proj, b_a, b_x, lam],
                     [_col_spec(s, tc, 0), _col_spec(s, tc, 1), _col_spec(s, tc), _col_spec(s, tc, 0),
                      _small_spec(1, tc), _small_spec(1, tc), _small_spec(1, tc)],
                     [(s, BF16), (s, F32)], s, tc)


def _lru_scan_bwd(dy, pre, xb, proj, hs, b_a, b_x, lam, name):
    s = xb.shape[0]
    tc = COL_TILE

    def body(dy_ref, pa_ref, px_ref, xb_ref, g_ref, hs_ref, ba_ref, bx_ref, lam_ref,
             dg_ref, dpa_ref, dpx_ref, dxb_ref, dba_ref, dbx_ref, dlam_ref):
        r, ig, sp, a, mult = _lru_gates(pa_ref, px_ref, ba_ref, bx_ref, lam_ref)
        gl, gl_der = _gelu_parts(g_ref[...])
        dy_v = dy_ref[...]
        hs_v = hs_ref[...]
        dg_ref[...] = (dy_v * hs_v * gl_der).astype(BF16)
        lam_t = dy_v * gl
        coef = _shift_up(a, 1)
        d = 1
        while d < s:
            row = lax.broadcasted_iota(jnp.int32, coef.shape, 0)
            keep = row < s - d
            lam_t = lam_t + coef * jnp.where(keep, pltpu.roll(lam_t, s - d, 0), 0.0)
            coef = coef * jnp.where(keep, pltpu.roll(coef, s - d, 0), 1.0)
            d *= 2
        xb_v = xb_ref[...]
        da = lam_t * _shift_down(hs_v, 1)
        dmult = lam_t * (ig * xb_v)
        dig = lam_t * mult * xb_v
        dxb_ref[...] = lam_t * mult * ig
        dlog_a = da * a - dmult * (a * a) / mult
        dr = dlog_a * ((-LRU_C) * sp)
        dsp = _rows_sum(dlog_a * ((-LRU_C) * r))
        dlam_ref[...] = -dsp * _sigmoid(-lam_ref[...])
        dpa = dr * r * (1.0 - r)
        dpx = dig * ig * (1.0 - ig)
        dba_ref[...] = _rows_sum(dpa)
        dbx_ref[...] = _rows_sum(dpx)
        dpa_ref[...] = dpa.astype(BF16)
        dpx_ref[...] = dpx.astype(BF16)

    return _col_call(name, body, [dy, pre, pre, xb, proj, hs, b_a, b_x, lam],
                     [_col_spec(s, tc), _col_spec(s, tc, 0), _col_spec(s, tc, 1), _col_spec(s, tc),
                      _col_spec(s, tc, 0), _col_spec(s, tc),
                      _small_spec(1, tc), _small_spec(1, tc), _small_spec(1, tc)],
                     [(s, BF16), (s, BF16), (s, BF16), (s, F32), (1, F32), (1, F32), (1, F32)], s, tc)


def _ffn_fwd(x, w_in, w_out, g_pre, g_post, shift, scale, gate, tag, after=None):
    h = _pre_norm(x, g_pre, scale, shift, tag + "_pre", after=after)
    gu = _mm_nn(h, w_in, tag + "_in")
    a = _swiglu_act(gu, tag + "_act")
    y = _mm_nn(a, w_out, tag + "_out", tn=512)
    xo = _post_norm(x, y, g_post, gate, 0.5, tag + "_post")
    return xo, (x, h, gu, a, y)


def _ffn_bwd(dxo, saved, w_in, w_out, g_pre, g_post, scale, gate, tag, after=None):
    x, h, gu, a, y = saved
    dy, dgate, dg_post = _post_norm_bwd(dxo, y, g_post, gate, 0.5, tag + "_post_b", after=after)
    da = _mm_nt(dy, w_out, tag + "_out_bx", tk=D_FF // 2)
    dw_out = _mm_tn(a, dy, tag + "_out_bw", tk=D_FF // 2)
    dgu = _swiglu_act_bwd(da, gu, tag + "_act_b")
    dh = _mm_nt(dgu, w_in, tag + "_in_bx")
    dw_in = _mm_tn(h, dgu, tag + "_in_bw", tn=w_in.arr.shape[-1], blocked_out=True)
    dx, dshift, dscale, dg_pre = _pre_norm_bwd(dxo, dh, x, g_pre, scale, tag + "_pre_b")
    return dx, dw_in, dw_out, (dshift, dscale, dgate), dg_pre, dg_post


def _pair_rows(v):
    return v.T.reshape(HEAD_PAIRS, 2, v.shape[0])


def _fox_fwd(h, p, tag):
    s = h.shape[0]
    proj = _mm_nn(h, p["w_in"], tag + "_in", tn=640)
    qkv = proj[:, :3 * D_MODEL].astype(BF16)
    flt = proj[:, 3 * D_MODEL:3 * D_MODEL + FOX_HEADS].T
    cum_t = _fox_gate(flt, p["b_f"], tag + "_gate")
    cum = cum_t.T
    cum_t2 = cum_t.reshape(HEAD_PAIRS, 2, s)
    o, ob, lse = _fox_attn_fwd(qkv, cum, cum_t2, tag + "_attn")
    y = _mm_nn(ob, p["w_out"], tag + "_out")
    return y, (qkv, flt, cum, cum_t2, o, ob, lse)


def _fox_bwd(dy, h, saved, p, tag):
    qkv, flt, cum, cum_t2, o, ob, lse = saved
    s = h.shape[0]
    do = _mm_nt(dy, p["w_out"], tag + "_out_bx")
    dw_out = _mm_tn(ob, dy, tag + "_out_bw")
    expand = jnp.pad(jnp.repeat(jnp.eye(FOX_HEADS, dtype=BF16), FOX_HEAD_DIM, axis=0),
                     ((0, 0), (0, PAIR_W - FOX_HEADS)))
    delta = _fox_delta(do, o, expand, tag + "_attn_delta")
    dq, dk, dv, dck, dcq = _fox_attn_bwd(qkv, do, cum, cum_t2, _pair_rows(lse), _pair_rows(delta), tag + "_attn_b")
    dcum_k = dck[:, :, :2].transpose(0, 2, 1).reshape(FOX_HEADS, s)
    dflt, db_f = _fox_gate_bwd(dcq.reshape(FOX_HEADS, s), dcum_k, flt, p["b_f"], tag + "_gate_b")
    dproj = jnp.concatenate(
        [dq, dk, dv, dflt.T, jnp.zeros((s, FOX_PAD - 3 * D_MODEL - FOX_HEADS), F32)], axis=1).astype(BF16)
    dh = _mm_nt(dproj, p["w_in"], tag + "_in_bx", tn=640)
    dw_in = _mm_tn(h, dproj, tag + "_in_bw", tn=640)
    return dh, {"w_in": dw_in, "w_out": dw_out, "b_f": db_f}


def _sconv_mix_fwd(h, p, tag):
    proj = _mm_nn(h, p["w_in"], tag + "_in")
    yb = _sconv_fwd(proj, p["conv_w"], tag + "_conv")
    y = _mm_nn(yb, p["w_out"], tag + "_out")
    return y, (proj, yb)


def _sconv_mix_bwd(dy, h, saved, p, tag):
    proj, yb = saved
    dyb = _mm_nt(dy, p["w_out"], tag + "_out_bx")
    dw_out = _mm_tn(yb, dy, tag + "_out_bw")
    db, dc, dxv, dcw = _sconv_bwd(dyb, proj, p["conv_w"], tag + "_conv_b")
    dproj = jnp.concatenate([db, dc, dxv], axis=1)
    dh = _mm_nt(dproj, p["w_in"], tag + "_in_bx")
    dw_in = _mm_tn(h, dproj, tag + "_in_bw", tn=p["w_in"].arr.shape[-1], blocked_out=True)
    return dh, {"w_in": dw_in, "w_out": dw_out, "conv_w": dcw}


def _lru_mix_fwd(h, p, tag):
    proj = _mm_nn(h, p["w_in"], tag + "_in")
    xb, xbb = _lru_conv(proj, p["conv_w"], p["conv_b"], tag + "_conv")
    pre = _mm_nn(xbb, p["w_ax"], tag + "_gates", tn=D_MODEL)
    yb, hs = _lru_scan(pre, xb, proj, p["b_a"], p["b_x"], p["lam"], tag + "_scan")
    y = _mm_nn(yb, p["w_out"], tag + "_out")
    return y, (proj, xb, xbb, pre, yb, hs)


def _diag_blocks(m):
    return jnp.stack([m[LRU_BLOCK_DIM * n:LRU_BLOCK_DIM * (n + 1), LRU_BLOCK_DIM * n:LRU_BLOCK_DIM * (n + 1)]
                      for n in range(LRU_BLOCKS)])


def _lru_mix_bwd(dy, h, saved, p, tag):
    proj, xb, xbb, pre, yb, hs = saved
    dyb = _mm_nt(dy, p["w_out"], tag + "_out_bx")
    dw_out = _mm_tn(yb, dy, tag + "_out_bw")
    dg, dpa, dpx, dxb1, dba, dbx, dlam = _lru_scan_bwd(dyb, pre, xb, proj, hs, p["b_a"], p["b_x"], p["lam"],
                                                       tag + "_scan_b")
    dpre = jnp.concatenate([dpa, dpx], axis=1)
    dxb2 = _mm_nt(dpre, p["w_ax"], tag + "_gates_bx", tn=D_MODEL)
    dw_ax = _mm_tn(xbb, dpre, tag + "_gates_bw", tn=D_MODEL)
    dx0, dcw, dcb = _lru_conv_bwd(dxb1, dxb2, proj, p["conv_w"], tag + "_conv_b")
    dproj = jnp.concatenate([dg, dx0], axis=1)
    dh = _mm_nt(dproj, p["w_in"], tag + "_in_bx")
    dw_in = _mm_tn(h, dproj, tag + "_in_bw", tn=p["w_in"].arr.shape[-1], blocked_out=True)
    grads = {"w_in": dw_in, "w_out": dw_out, "conv_w": dcw, "conv_b": dcb,
             "w_a": _diag_blocks(dw_ax[:, :D_MODEL]), "w_x": _diag_blocks(dw_ax[:, D_MODEL:]),
             "b_a": dba, "b_x": dbx, "lam": dlam}
    return dh, grads


_MIXERS = ((_fox_fwd, _fox_bwd), (_sconv_mix_fwd, _sconv_mix_bwd), (_lru_mix_fwd, _lru_mix_bwd))


def _local_step(x, target, mod, layer_params, on_grads=None, first_after=None):
    layers = []
    tape = []
    for i in range(DEPTH):
        lp = layer_params(i, x)
        layers.append(lp)
        row = lambda v: v[None, :]
        m = lambda sub, what: mod[i, sub, what][None, :]
        x, sv0 = _ffn_fwd(x, lp["ffn_in"][0], lp["ffn_out"][0], row(lp["norm_pre"][0]), row(lp["norm_post"][0]),
                          m(0, 0), m(0, 1), m(0, 2), f"l{i}_ffn0", after=first_after if i == 0 else None)
        h = _pre_norm(x, row(lp["norm_pre"][1]), m(1, 1), m(1, 0), f"l{i}_mix_pre")
        y, svm = _MIXERS[i % 3][0](h, lp["mixer"], f"l{i}_mix")
        x1 = _post_norm(x, y, row(lp["norm_post"][1]), m(1, 2), 1.0, f"l{i}_mix_post")
        x2, sv2 = _ffn_fwd(x1, lp["ffn_in"][1], lp["ffn_out"][1], row(lp["norm_pre"][2]), row(lp["norm_post"][2]),
                           m(2, 0), m(2, 1), m(2, 2), f"l{i}_ffn1")
        tape.append((sv0, (x, h, y, svm), sv2))
        x = x2
    dx, loss_row = _loss_head(x, target, "loss_head")

    layer_grads = [None] * DEPTH
    dmod = [None] * DEPTH
    after = None
    for i in reversed(range(DEPTH)):
        lp = layers[i]
        row = lambda v: v[None, :]
        m = lambda sub, what: mod[i, sub, what][None, :]
        sv0, (xm, h, y, svm), sv2 = tape[i]
        dx, dw_in1, dw_out1, dm2, dgp2, dgq2 = _ffn_bwd(dx, sv2, lp["ffn_in"][1], lp["ffn_out"][1],
                                                        row(lp["norm_pre"][2]), row(lp["norm_post"][2]),
                                                        m(2, 1), m(2, 2), f"l{i}_ffn1", after=after)
        dy, dgate1, dgq1 = _post_norm_bwd(dx, y, row(lp["norm_post"][1]), m(1, 2), 1.0, f"l{i}_mix_post_b")
        dh, mg = _MIXERS[i % 3][1](dy, h, svm, lp["mixer"], f"l{i}_mix")
        dx, dshift1, dscale1, dgp1 = _pre_norm_bwd(dx, dh, xm, row(lp["norm_pre"][1]), m(1, 1), f"l{i}_mix_pre_b")
        dx, dw_in0, dw_out0, dm0, dgp0, dgq0 = _ffn_bwd(dx, sv0, lp["ffn_in"][0], lp["ffn_out"][0],
                                                        row(lp["norm_pre"][0]), row(lp["norm_post"][0]),
                                                        m(0, 1), m(0, 2), f"l{i}_ffn0")
        dmod[i] = jnp.concatenate([*dm0, dshift1, dscale1, dgate1, *dm2], axis=0).reshape(N_SUB, 3, D_MODEL)
        layer_grads[i] = {"ffn_in": (dw_in0, dw_in1), "ffn_out": (dw_out0, dw_out1),
                          "norm_pre": jnp.concatenate([dgp0, dgp1, dgp2], axis=0),
                          "norm_post": jnp.concatenate([dgq0, dgq1, dgq2], axis=0), "mixer": mg}
        if on_grads is not None:
            after = on_grads(i, layer_grads[i])
    return loss_row, dx, jnp.stack(dmod), layer_grads


COND_ROWS = 16
COND_PAD = 128


def _cond_fwd(c_pad, w_cond, b_shard, name):
    nl, d, n = w_cond.shape
    tn = 768

    def body(c_ref, w_ref, b_ref, o_ref):
        cv = c_ref[...]
        act = (cv * _sigmoid(cv)).astype(BF16)
        o_ref[...] = jnp.dot(act, w_ref[...].astype(BF16), preferred_element_type=F32) + b_ref[...]

    return pl.pallas_call(
        body, name=name, grid=(nl, n // tn),
        in_specs=[pl.BlockSpec((COND_ROWS, d), lambda i, j: (0, 0)),
                  pl.BlockSpec((None, d, tn), lambda i, j: (i, 0, j)),
                  pl.BlockSpec((None, 1, tn), lambda i, j: (i, 0, j))],
        out_specs=pl.BlockSpec((None, COND_ROWS, tn), lambda i, j: (i, 0, j)),
        out_shape=jax.ShapeDtypeStruct((nl, COND_ROWS, n), F32),
        compiler_params=_cparams(("arbitrary", "arbitrary")),
    )(c_pad, w_cond, b_shard)


def _adam_math(w, g, m, v):
    nm = ADAM_B1 * m + (1.0 - ADAM_B1) * g
    nv = ADAM_B2 * v + (1.0 - ADAM_B2) * (g * g)
    m_hat = nm / (1.0 - ADAM_B1 ** ADAM_STEP)
    v_hat = nv / (1.0 - ADAM_B2 ** ADAM_STEP)
    delta = (-ADAM_LR) * (m_hat / (jnp.sqrt(v_hat) + ADAM_EPS) + ADAM_WD * w)
    return delta, nm, nv


def _cond_bwd_adamw(c_t, dmod_s, w, m, v, name):
    nl, d, n = w.shape
    tn = 384
    blk = pl.BlockSpec((None, d, tn), lambda i, j: (i, 0, j))

    def body(c_ref, dm_ref, w_ref, m_ref, v_ref, g_ref, d_ref, nm_ref, nv_ref):
        cv = c_ref[...]
        g = jnp.dot((cv * _sigmoid(cv)).astype(BF16), dm_ref[...], preferred_element_type=F32)
        g_ref[...] = g
        d_ref[...], nm_ref[...], nv_ref[...] = _adam_math(w_ref[...], g, m_ref[...], v_ref[...])

    return pl.pallas_call(
        body, name=name, grid=(nl, n // tn),
        in_specs=[pl.BlockSpec((d, COND_PAD), lambda i, j: (0, 0)),
                  pl.BlockSpec((None, COND_PAD, tn), lambda i, j: (i, 0, j)), blk, blk, blk],
        out_specs=[blk] * 4, out_shape=[jax.ShapeDtypeStruct(w.shape, F32)] * 4,
        compiler_params=_cparams(("arbitrary", "arbitrary")),
    )(c_t, dmod_s, w, m, v)


def _adamw(w, g, m, v, name):
    rows, cols = w.shape
    tr = next(t for t in (256, 176, 128, 64, 32, 16, 8) if rows % t == 0)
    blk = pl.BlockSpec((tr, cols), lambda i: (i, 0))

    def body(w_ref, g_ref, m_ref, v_ref, d_ref, nm_ref, nv_ref):
        d_ref[...], nm_ref[...], nv_ref[...] = _adam_math(w_ref[...], g_ref[...], m_ref[...], v_ref[...])

    return pl.pallas_call(
        body, name=name, grid=(rows // tr,), in_specs=[blk] * 4, out_specs=[blk] * 3,
        out_shape=[jax.ShapeDtypeStruct(w.shape, F32)] * 3, compiler_params=_cparams(("arbitrary",)),
    )(w, g, m, v)


_MESH = pl.DeviceIdType.MESH
_ANY = pl.BlockSpec(memory_space=pl.ANY)


def _place():
    return lax.axis_index("x"), lax.axis_index("y"), lax.axis_index("c")


def _other_chips(x, y):
    return [(1 - x, y), (x, 1 - y), (1 - x, 1 - y)]


def _allgather8(block, name):
    m_per, n = block.shape

    def body(x_ref, out_ref, send_sems, recv_sems, local_sem):
        x, y, c = _place()
        me, sibling = (x, y, c), (x, y, 1 - c)
        chips = _other_chips(x, y)

        def rows(px, py, pc):
            return out_ref.at[pl.ds((4 * px + 2 * py + pc) * m_per, m_per), :]

        def copy(k, blk, to, src=None):
            return pltpu.make_async_remote_copy(
                src_ref=rows(*blk) if src is None else src, dst_ref=rows(*blk),
                send_sem=send_sems.at[k], recv_sem=recv_sems.at[k], device_id=to, device_id_type=_MESH)

        mine = pltpu.make_async_copy(x_ref, rows(*me), local_sem)
        mine.start()
        first = [copy(0, me, sibling, src=x_ref)]
        first += [copy(1 + j, me, (*chip, c), src=x_ref) for j, chip in enumerate(chips)]
        for cp in first:
            cp.start()
        passed = [copy(4 + j, (*chip, c), sibling) for j, chip in enumerate(chips)]
        for j, chip in enumerate(chips):
            copy(1 + j, (*chip, c), me).wait_recv()
            passed[j].start()
        copy(0, sibling, me).wait_recv()
        for j, chip in enumerate(chips):
            copy(4 + j, (*chip, 1 - c), me).wait_recv()
        for cp in first + passed:
            cp.wait_send()
        mine.wait()

    return pl.pallas_call(
        body, name=name, out_shape=jax.ShapeDtypeStruct((N_DEV * m_per, n), block.dtype),
        in_specs=[pl.BlockSpec(memory_space=pltpu.VMEM)], out_specs=pl.BlockSpec(memory_space=pltpu.VMEM),
        scratch_shapes=[pltpu.SemaphoreType.DMA((7,)), pltpu.SemaphoreType.DMA((7,)), pltpu.SemaphoreType.DMA],
        compiler_params=_cparams(),
    )(block)


def _split_axis(shape):
    return next(a for a, n in enumerate(shape) if n > 1)


_HBM = pl.BlockSpec(memory_space=pltpu.HBM)
_SEM = pl.BlockSpec(memory_space=pltpu.SEMAPHORE)
_SPLIT_COPY = pltpu.CompilerParams(has_side_effects=pltpu.SideEffectType.DATAFLOW_SIDE_EFFECTING)
_TOKEN = jax.ShapeDtypeStruct((8, 128), F32)


def _in_hbm(arrays):
    return [pltpu.with_memory_space_constraint(a, pltpu.HBM) for a in arrays]


class _Gathered(NamedTuple):
    shard_shape: tuple
    chip_axis: int

    @property
    def shape(self):
        return self.shard_shape[:self.chip_axis] + (N_CHIPS,) + self.shard_shape[self.chip_axis:]

    def half(self, ref, chip, pc):
        cut = _split_axis(self.shard_shape)
        n = self.shard_shape[cut] // 2
        idx = [slice(None)] * len(self.shard_shape)
        idx[cut] = pl.ds(pc * n, n)
        idx.insert(self.chip_axis, chip)
        return ref.at[tuple(idx)]


def _own_block_placed(shard, layout, chip):
    return lax.dynamic_update_slice_in_dim(lax.empty(layout.shape, shard.dtype),
                                           jnp.expand_dims(shard, layout.chip_axis), chip, axis=layout.chip_axis)


def _gather_copies(lands, layouts, send_sems, recv_sems):
    x, y, c = _place()
    out = []
    for t, (land, lay) in enumerate(zip(lands, layouts)):
        for j, (px, py) in enumerate(_other_chips(x, y)):
            def copy(chip, t=t, j=j, px=px, py=py, land=land, lay=lay):
                return pltpu.make_async_remote_copy(
                    src_ref=lay.half(land, chip, c), dst_ref=lay.half(land, chip, c),
                    send_sem=send_sems.at[3 * t + j], recv_sem=recv_sems.at[3 * t + j],
                    device_id=(px, py, c), device_id_type=_MESH)
            out.append((copy(2 * x + y), copy(2 * px + py)))
    return out


def _gather_start(lands, layouts, after, name):
    nt = len(lands)
    order = [] if after is None else [after]

    def body(*refs):
        land_refs = refs[:nt]
        send_sems, recv_sems = refs[nt + len(order):nt + len(order) + 2]
        token = refs[-1]
        for send, _ in _gather_copies(land_refs, layouts, send_sems, recv_sems):
            send.start()
        token[...] = jnp.zeros_like(token)

    out = pl.pallas_call(
        body, name=name,
        out_shape=(pltpu.SemaphoreType.DMA((3 * nt,)), pltpu.SemaphoreType.DMA((3 * nt,)),
                   *[pltpu.HBM(a.shape, a.dtype) for a in lands], _TOKEN),
        in_specs=[_HBM] * nt + [_ANY] * len(order),
        out_specs=(_SEM, _SEM, *[_HBM] * nt, pl.BlockSpec(memory_space=pltpu.VMEM)),
        input_output_aliases={t: 2 + t for t in range(nt)}, compiler_params=_SPLIT_COPY,
    )(*_in_hbm(lands), *order)
    return out[0], out[1], list(out[2:2 + nt]), out[-1]


def _gather_wait(send_sems, recv_sems, lands, layouts, after, name):
    nt = len(lands)

    def body(*refs):
        land_refs = refs[:nt]
        sems = refs[nt:nt + 2]
        for send, arrival in _gather_copies(land_refs, layouts, *sems):
            send.wait_send()
            arrival.wait_recv()

    return list(pl.pallas_call(
        body, name=name, out_shape=tuple(pltpu.HBM(a.shape, a.dtype) for a in lands),
        in_specs=[_HBM] * nt + [_SEM, _SEM, _ANY], out_specs=tuple([_HBM] * nt),
        input_output_aliases={t: t for t in range(nt)}, compiler_params=_SPLIT_COPY,
    )(*lands, send_sems, recv_sems, after))


def _gather_forward(lands, layouts, name):
    nt = len(lands)

    def body(*refs):
        outs = refs[nt:2 * nt]
        send_sems, recv_sems = refs[2 * nt:]
        x, y, c = _place()
        sends, arrivals = [], []
        for t, lay in enumerate(layouts):
            for j, (px, py) in enumerate(_other_chips(x, y)):
                for pc, group in ((c, sends), (1 - c, arrivals)):
                    part = lay.half(outs[t], 2 * px + py, pc)
                    group.append(pltpu.make_async_remote_copy(
                        src_ref=part, dst_ref=part, send_sem=send_sems.at[3 * t + j], recv_sem=recv_sems.at[3 * t + j],
                        device_id=(x, y, 1 - c), device_id_type=_MESH))
        for cp in sends:
            cp.start()
        for cp in arrivals:
            cp.wait_recv()
        for cp in sends:
            cp.wait_send()

    return list(pl.pallas_call(
        body, name=name, out_shape=[jax.ShapeDtypeStruct(a.shape, a.dtype) for a in lands],
        in_specs=[_ANY] * nt, out_specs=[_ANY] * nt, input_output_aliases={t: t for t in range(nt)},
        scratch_shapes=[pltpu.SemaphoreType.DMA((3 * nt,)), pltpu.SemaphoreType.DMA((3 * nt,))],
        compiler_params=_cparams(),
    )(*lands))


def _pair_exchange(grads, name):
    nt = len(grads)

    def body(*refs):
        ins, outs = refs[:nt], refs[nt:2 * nt]
        send_sems, recv_sems = refs[2 * nt:]
        x, y, c = _place()
        copies = []
        for t in range(nt):
            h = grads[t].shape[1] // 2
            copies.append(pltpu.make_async_remote_copy(
                src_ref=ins[t].at[:, pl.ds((1 - c) * h, h), :], dst_ref=outs[t],
                send_sem=send_sems.at[t], recv_sem=recv_sems.at[t], device_id=(x, y, 1 - c), device_id_type=_MESH))
        for cp in copies:
            cp.start()
        for cp in copies:
            cp.wait()

    return pl.pallas_call(
        body, name=name,
        out_shape=[jax.ShapeDtypeStruct((N_CHIPS, g.shape[1] // 2, g.shape[2]), g.dtype) for g in grads],
        in_specs=[_ANY] * nt, out_specs=[_ANY] * nt,
        scratch_shapes=[pltpu.SemaphoreType.DMA((nt,)), pltpu.SemaphoreType.DMA((nt,))],
        compiler_params=_cparams(),
    )(*grads)


def _pair_sum(own, recv, c_idx, name):
    _, h, cols = recv.shape

    def body(c_ref, own_ref, recv_ref, o_ref):
        o_ref[...] = (own_ref[...] + recv_ref[...]).astype(BF16)

    return pl.pallas_call(
        body, name=name,
        grid_spec=pltpu.PrefetchScalarGridSpec(
            num_scalar_prefetch=1, grid=(N_CHIPS,),
            in_specs=[pl.BlockSpec((None, h, cols), lambda k, c_ref: (k, c_ref[0], 0)),
                      pl.BlockSpec((None, h, cols), lambda k, c_ref: (k, 0, 0))],
            out_specs=pl.BlockSpec((None, h, cols), lambda k, c_ref: (k, 0, 0))),
        out_shape=jax.ShapeDtypeStruct(recv.shape, BF16), compiler_params=_cparams(("arbitrary",)),
    )(c_idx, own, recv)


def _chip_copies(parts, lands, send_sems, recv_sems):
    x, y, c = _place()
    out = []
    for t, (part, land) in enumerate(zip(parts, lands)):
        for j, (px, py) in enumerate(_other_chips(x, y)):
            out.append(pltpu.make_async_remote_copy(
                src_ref=part.at[2 * px + py], dst_ref=land.at[j], send_sem=send_sems.at[3 * t + j],
                recv_sem=recv_sems.at[3 * t + j], device_id=(px, py, c), device_id_type=_MESH))
    return out


def _chip_send_start(parts, after, name):
    nt = len(parts)
    lands = [lax.empty((N_CHIPS - 1,) + p.shape[1:], p.dtype) for p in parts]
    order = [] if after is None else [after]

    def body(*refs):
        send_sems, recv_sems = refs[2 * nt + len(order):2 * nt + len(order) + 2]
        token = refs[-1]
        for cp in _chip_copies(refs[:nt], refs[nt:2 * nt], send_sems, recv_sems):
            cp.start()
        token[...] = jnp.zeros_like(token)

    out = pl.pallas_call(
        body, name=name,
        out_shape=(pltpu.SemaphoreType.DMA((3 * nt,)), pltpu.SemaphoreType.DMA((3 * nt,)),
                   *[pltpu.HBM(a.shape, a.dtype) for a in parts + lands], _TOKEN),
        in_specs=[_HBM] * (2 * nt) + [_ANY] * len(order),
        out_specs=(_SEM, _SEM, *[_HBM] * (2 * nt), pl.BlockSpec(memory_space=pltpu.VMEM)),
        input_output_aliases={t: 2 + t for t in range(2 * nt)}, compiler_params=_SPLIT_COPY,
    )(*_in_hbm(parts + lands), *order)
    return out[0], out[1], list(out[2:2 + nt]), list(out[2 + nt:2 + 2 * nt]), out[-1]


def _chip_send_wait(send_sems, recv_sems, parts, lands, after, name):
    nt = len(parts)

    def body(*refs):
        for cp in _chip_copies(refs[:nt], refs[nt:2 * nt], *refs[2 * nt:2 * nt + 2]):
            cp.wait_send()
            cp.wait_recv()

    out = pl.pallas_call(
        body, name=name, out_shape=tuple(pltpu.HBM(a.shape, a.dtype) for a in parts + lands),
        in_specs=[_HBM] * (2 * nt) + [_SEM, _SEM, _ANY], out_specs=tuple([_HBM] * (2 * nt)),
        input_output_aliases={t: t for t in range(2 * nt)}, compiler_params=_SPLIT_COPY,
    )(*parts, *lands, send_sems, recv_sems, after)
    return list(out[:nt]), list(out[nt:])


def _chip_sum(part, arrived, into, lead, place_idx, name):
    _, h, cols = part.shape

    def body(idx_ref, own_ref, arr_ref, into_ref, o_ref):
        acc = own_ref[...].astype(F32)
        for k in range(N_CHIPS - 1):
            acc = acc + arr_ref[k].astype(F32)
        o_ref[...] = acc

    return pl.pallas_call(
        body, name=name,
        grid_spec=pltpu.PrefetchScalarGridSpec(
            num_scalar_prefetch=1, grid=(1,),
            in_specs=[pl.BlockSpec((None, h, cols), lambda g, idx: (idx[1], 0, 0)),
                      pl.BlockSpec((N_CHIPS - 1, h, cols), lambda g, idx: (0, 0, 0)), _ANY],
            out_specs=pl.BlockSpec((None,) * len(lead) + (h, cols), lambda g, idx: (*lead, idx[0], 0))),
        out_shape=jax.ShapeDtypeStruct(into.shape, F32), input_output_aliases={3: 0},
        compiler_params=_cparams(("arbitrary",)),
    )(place_idx, part, arrived, into)


def _pair_gather(bufs, homes, name):
    nt, nb = len(homes), len(bufs)

    def body(*refs):
        outs = refs[nb:2 * nb]
        send_sems, recv_sems = refs[2 * nb:]
        x, y, c = _place()

        def home(t, pc):
            o, lead, rows = homes[t]
            return outs[o].at[(*lead, pl.ds(pc * (rows // 2), rows // 2), slice(None))]

        def copy(t, pc):
            return pltpu.make_async_remote_copy(src_ref=home(t, pc), dst_ref=home(t, pc), send_sem=send_sems.at[t],
                                                recv_sem=recv_sems.at[t], device_id=(x, y, 1 - c), device_id_type=_MESH)

        sends = [copy(t, c) for t in range(nt)]
        for cp in sends:
            cp.start()
        for t in range(nt):
            copy(t, 1 - c).wait_recv()
        for cp in sends:
            cp.wait_send()

    return pl.pallas_call(
        body, name=name, out_shape=[jax.ShapeDtypeStruct(b.shape, b.dtype) for b in bufs],
        in_specs=[_ANY] * nb, out_specs=[_ANY] * nb, input_output_aliases={o: o for o in range(nb)},
        scratch_shapes=[pltpu.SemaphoreType.DMA((nt,)), pltpu.SemaphoreType.DMA((nt,))],
        compiler_params=_cparams(),
    )(*bufs)


def _sum_devices(g, name):
    def body(g_ref, o_ref):
        acc = g_ref[0:1, :]
        for d in range(1, N_DEV):
            acc = acc + g_ref[d:d + 1, :]
        o_ref[...] = acc
    return pl.pallas_call(body, name=name, out_shape=jax.ShapeDtypeStruct((1, g.shape[1]), F32),
                          compiler_params=_cparams())(g)


_WEIGHTS = ("w_cond", "b_cond", "norm_pre", "norm_post", "w_ffn_in", "w_ffn_out", "fox_w_in", "fox_b_f",
            "fox_w_out", "sconv_w_in", "sconv_conv_w", "sconv_w_out", "lru_w_in", "lru_conv_w", "lru_conv_b",
            "lru_w_a", "lru_b_a", "lru_w_x", "lru_b_x", "lru_lambda", "lru_w_out")
_BIG = (("w_ffn_in", False), ("w_ffn_out", True), ("fox_w_in", False), ("fox_w_out", True),
        ("sconv_w_in", False), ("sconv_w_out", True), ("lru_w_in", False), ("lru_w_out", True))
_SMALL = tuple(n for n in _WEIGHTS if n != "w_cond" and n not in dict(_BIG))
_COL_SHARDED_SMALL = ("norm_pre", "norm_post", "sconv_conv_w", "lru_conv_w", "lru_conv_b", "lru_lambda")


def _pack_rows(parts, rows=8):
    flat = jnp.concatenate([p.reshape(-1) for p in parts])
    width = -(-flat.size // (rows * 128)) * 128
    return jnp.pad(flat, (0, rows * width - flat.size)).reshape(rows, width)


def _unpack(flat, shapes):
    out, off = [], 0
    for shp in shapes:
        n = math.prod(shp)
        out.append(flat[off:off + n].reshape(shp))
        off += n
    return out


def _join_chips(g):
    g = jnp.moveaxis(g, 0, -2)
    return g.reshape(g.shape[:-2] + (g.shape[-2] * g.shape[-1],))


def _my_columns(full, chip):
    n = full.shape[-1] // N_CHIPS
    return lax.dynamic_slice_in_dim(full, chip * n, n, axis=full.ndim - 1)


def _block_diag(w):
    eye = jnp.eye(LRU_BLOCKS, dtype=w.dtype)
    return jnp.einsum("nij,nm->nimj", w, eye).reshape(D_MODEL, D_MODEL)


def _step(x, c, target, wts, mom, var):
    ix, iy, ic = _place()
    chip = 2 * ix + iy
    dev = 2 * chip + ic
    n_cond = wts["w_cond"].shape[2]

    small_shapes = [(D_MODEL,)] + [wts[n].shape for n in _COL_SHARDED_SMALL]
    g1 = _allgather8(_pack_rows([c[0]] + [wts[n] for n in _COL_SHARDED_SMALL]), "gather_small").reshape(N_DEV, -1)
    c_all = g1[:, :D_MODEL]
    per_chip = [jnp.stack(col) for col in zip(*[_unpack(g1[2 * k], small_shapes) for k in range(N_CHIPS)])]
    small_full = {n: _join_chips(v) for n, v in zip(_COL_SHARDED_SMALL, per_chip[1:])}

    c_pad = jnp.pad(c_all, ((0, COND_ROWS - N_DEV), (0, 0)))
    b_shard = _my_columns(wts["b_cond"], chip)[:, None, :]
    mod_part = _cond_fwd(c_pad, wts["w_cond"], b_shard, "cond_fwd")
    g2 = _allgather8(mod_part[:, :N_DEV].transpose(1, 0, 2).reshape(N_DEV, DEPTH * n_cond), "gather_mod")
    g2 = g2.reshape(N_DEV, N_DEV, DEPTH, n_cond)[0::2]
    mod = _join_chips(lax.dynamic_index_in_dim(g2, dev, axis=1, keepdims=False)).reshape(DEPTH, N_SUB, 3, D_MODEL)

    mixer_names = [("fox_w_in", "fox_w_out"), ("sconv_w_in", "sconv_w_out"), ("lru_w_in", "lru_w_out")]
    in_flight, token = [], None
    for i in range(DEPTH):
        n_in, n_out = mixer_names[i % 3]
        shards = [wts["w_ffn_in"][i], wts["w_ffn_out"][i], wts[n_in][i // 3], wts[n_out][i // 3]]
        layouts = [_Gathered(s.shape, s.ndim - 2 if by_rows else 0) for s, by_rows in zip(shards, (False, True) * 2)]
        lands = [_own_block_placed(s.astype(BF16), lay, chip) for s, lay in zip(shards, layouts)]
        send_sems, recv_sems, lands, token = _gather_start(lands, layouts, token, f"gather_start_l{i}")
        in_flight.append((send_sems, recv_sems, lands, layouts))
    lru_ax = jnp.concatenate([_block_diag(wts["lru_w_a"][0]), _block_diag(wts["lru_w_x"][0])], axis=1).astype(BF16)

    def rows_joined(g):
        return g.reshape(g.shape[:-3] + (g.shape[-3] * g.shape[-2], g.shape[-1]))

    def layer_params(i, x_in):
        send_sems, recv_sems, lands, layouts = in_flight[i]
        lands = _gather_wait(send_sems, recv_sems, lands, layouts, x_in, f"gather_wait_l{i}")
        ffn_in, ffn_out, mix_in, mix_out = _gather_forward(lands, layouts, f"gather_forward_l{i}")
        ffn_out, mix_out = rows_joined(ffn_out), rows_joined(mix_out)
        j = i // 3
        if i % 3 == 0:
            w_in = jnp.pad(_join_chips(mix_in), ((0, 0), (0, FOX_PAD - 3 * D_MODEL - FOX_HEADS)))
            mixer = {"w_in": _W(w_in), "w_out": _W(mix_out), "b_f": wts["fox_b_f"][j][:, None]}
        elif i % 3 == 1:
            mixer = {"w_in": _W(mix_in, (), True), "w_out": _W(mix_out), "conv_w": small_full["sconv_conv_w"][j]}
        else:
            mixer = {"w_in": _W(mix_in, (), True), "w_out": _W(mix_out), "conv_w": small_full["lru_conv_w"][j],
                     "conv_b": small_full["lru_conv_b"], "w_ax": _W(lru_ax), "b_a": wts["lru_b_a"].reshape(1, D_MODEL),
                     "b_x": wts["lru_b_x"].reshape(1, D_MODEL), "lam": small_full["lru_lambda"]}
        return {"ffn_in": [_W(ffn_in, (k,), True) for k in range(2)], "ffn_out": [_W(ffn_out, (k,)) for k in range(2)],
                "norm_pre": small_full["norm_pre"][i], "norm_post": small_full["norm_post"][i], "mixer": mixer}

    place_idx = jnp.stack([ic, chip]).astype(jnp.int32)
    c_idx = place_idx[:1]
    big_index = {n: o for o, (n, _) in enumerate(_BIG)}
    exchanges = []

    def chip_blocks(g, by_rows, width):
        if by_rows:
            return g.reshape(N_CHIPS, g.shape[0] // N_CHIPS, g.shape[1])
        if g.ndim == 3:
            return g
        return g[:, :width * N_CHIPS].reshape(g.shape[0], N_CHIPS, width).transpose(1, 0, 2)

    def on_grads(i, g):
        n_in, n_out = mixer_names[i % 3]
        items = [("w_ffn_in", (i, k), g["ffn_in"][k]) for k in range(2)]
        items += [("w_ffn_out", (i, k), g["ffn_out"][k]) for k in range(2)]
        items += [(n_in, (i // 3,), g["mixer"]["w_in"]), (n_out, (i // 3,), g["mixer"]["w_out"])]
        tensors = [chip_blocks(t, dict(_BIG)[n], wts[n].shape[-1]) for n, _, t in items]
        homes = [(big_index[n], lead, wts[n].shape[-2]) for n, lead, _ in items]
        recv = _pair_exchange(tensors, f"grads_pair_exchange_l{i}")
        parts = [_pair_sum(t, r, c_idx, f"grads_pair_sum_l{i}_{k}") for k, (t, r) in enumerate(zip(tensors, recv))]
        send_sems, recv_sems, parts, lands, tok = _chip_send_start(parts, None, f"grads_chip_start_l{i}")
        exchanges.append((i, send_sems, recv_sems, parts, lands, homes))
        return tok

    loss_row, grad_x, dmod, lg = _local_step(x[0], target[0], mod, layer_params, on_grads, token)
    loss = lax.psum(loss_row[0, 0], ("x", "y", "c"))

    fox_layers = [i for i in range(DEPTH) if i % 3 == 0]
    sconv_g, lru_g = lg[1]["mixer"], lg[2]["mixer"]
    small_g = {
        "dmod": dmod, "norm_pre": jnp.stack([g["norm_pre"] for g in lg]), "norm_post": jnp.stack([g["norm_post"] for g in lg]),
        "fox_b_f": jnp.stack([lg[i]["mixer"]["b_f"][:, 0] for i in fox_layers]),
        "sconv_conv_w": sconv_g["conv_w"][None], "lru_conv_w": lru_g["conv_w"][None], "lru_conv_b": lru_g["conv_b"],
        "lru_w_a": lru_g["w_a"][None], "lru_b_a": lru_g["b_a"].reshape(1, LRU_BLOCKS, LRU_BLOCK_DIM),
        "lru_w_x": lru_g["w_x"][None], "lru_b_x": lru_g["b_x"].reshape(1, LRU_BLOCKS, LRU_BLOCK_DIM),
        "lru_lambda": lru_g["lam"]}
    g4 = _allgather8(_pack_rows(list(small_g.values())), "gather_small_grads").reshape(N_DEV, -1)
    summed = dict(zip(small_g, _unpack(_sum_devices(g4, "sum_small_grads")[0], [v.shape for v in small_g.values()])))
    grads = {n: (_my_columns(summed[n], chip) if n in _COL_SHARDED_SMALL else summed[n]) for n in _SMALL if n != "b_cond"}
    grads["b_cond"] = summed["dmod"].reshape(DEPTH, N_SUB * 3 * D_MODEL)

    dmod_all = g4[:, :dmod.size].reshape(N_DEV, DEPTH, N_SUB * 3 * D_MODEL)
    dmod_s = jnp.pad(_my_columns(dmod_all, chip).transpose(1, 0, 2), ((0, 0), (0, COND_PAD - N_DEV), (0, 0))).astype(BF16)
    c_t = jnp.pad(c_all.T, ((0, 0), (0, COND_PAD - N_DEV)))
    grads["w_cond"], d_cond, m_cond, v_cond = _cond_bwd_adamw(c_t, dmod_s, wts["w_cond"], mom["w_cond"],
                                                              var["w_cond"], "cond_bwd_adamw")

    bufs = [lax.empty(wts[n].shape, F32) for n, _ in _BIG]
    all_homes = []
    for i, send_sems, recv_sems, parts, lands, homes in exchanges:
        parts, lands = _chip_send_wait(send_sems, recv_sems, parts, lands, grad_x, f"grads_chip_wait_l{i}")
        for k, (part, land, (o, lead, _)) in enumerate(zip(parts, lands, homes)):
            bufs[o] = _chip_sum(part, land, bufs[o], lead, place_idx, f"grads_chip_sum_l{i}_{k}")
        all_homes += homes
    grads.update(zip([n for n, _ in _BIG], _pair_gather(bufs, all_homes, "grads_pair_gather")))

    delta, new_m, new_v = {"w_cond": d_cond}, {"w_cond": m_cond}, {"w_cond": v_cond}
    for n, _ in _BIG:
        two_d = lambda a: a.reshape(-1, a.shape[-1])
        d, nm, nv = _adamw(two_d(wts[n]), two_d(grads[n]), two_d(mom[n]), two_d(var[n]), "adamw_" + n)
        delta[n], new_m[n], new_v[n] = (a.reshape(wts[n].shape) for a in (d, nm, nv))
    shapes = [wts[n].shape for n in _SMALL]
    packed = [_pack_rows([src[n] for n in _SMALL]) for src in (wts, grads, mom, var)]
    for dst, out in zip((delta, new_m, new_v), _adamw(*packed, "adamw_small")):
        dst.update(zip(_SMALL, _unpack(out.reshape(-1), shapes)))

    return (loss, grad_x[None], *[grads[n] for n in _WEIGHTS], *[delta[n] for n in _WEIGHTS],
            *[new_m[n] for n in _WEIGHTS], *[new_v[n] for n in _WEIGHTS])


def kernel(x, c, w_cond, b_cond, norm_pre, norm_post, w_ffn_in, w_ffn_out, fox_w_in, fox_b_f, fox_w_out, sconv_w_in, sconv_conv_w, sconv_w_out, lru_w_in, lru_conv_w, lru_conv_b, lru_w_a, lru_b_a, lru_w_x, lru_b_x, lru_lambda, lru_w_out, loss_target, m_w_cond, m_b_cond, m_norm_pre, m_norm_post, m_w_ffn_in, m_w_ffn_out, m_fox_w_in, m_fox_b_f, m_fox_w_out, m_sconv_w_in, m_sconv_conv_w, m_sconv_w_out, m_lru_w_in, m_lru_conv_w, m_lru_conv_b, m_lru_w_a, m_lru_b_a, m_lru_w_x, m_lru_b_x, m_lru_lambda, m_lru_w_out, v_w_cond, v_b_cond, v_norm_pre, v_norm_post, v_w_ffn_in, v_w_ffn_out, v_fox_w_in, v_fox_b_f, v_fox_w_out, v_sconv_w_in, v_sconv_conv_w, v_sconv_w_out, v_lru_w_in, v_lru_conv_w, v_lru_conv_b, v_lru_w_a, v_lru_b_a, v_lru_w_x, v_lru_b_x, v_lru_lambda, v_lru_w_out):
    given = dict(locals())
    wts = {n: given[n] for n in _WEIGHTS}
    mom = {n: given["m_" + n] for n in _WEIGHTS}
    var = {n: given["v_" + n] for n in _WEIGHTS}
    return _step(x, c, loss_target, wts, mom, var)
```

```python
import functools
import math
from typing import NamedTuple

import jax
import jax.numpy as jnp
from jax import lax
from jax.experimental import pallas as pl
from jax.experimental.pallas import tpu as pltpu

F32 = jnp.float32
BF16 = jnp.bfloat16

D_MODEL = 1024
DEPTH = 4
N_SUB = 3
D_FF = 2816
RMS_EPS = 1e-6
FOX_HEADS = 16
FOX_HEAD_DIM = 64
FOX_PAD = 3200
LRU_BLOCKS = 16
LRU_BLOCK_DIM = 64
LRU_C = 8.0
N_CHIPS = 4
N_DEV = 8

ADAM_LR = 0.001
ADAM_B1 = 0.9
ADAM_B2 = 0.999
ADAM_EPS = 1e-08
ADAM_WD = 0.01
ADAM_STEP = 10

VMEM_LIMIT_V7X = 56 * 1024 * 1024
ROW_TILE = 256
COL_TILE = 256
ATT_TILE = 256
MM_ROWS = 256


def _cparams(sem=None):
    return pltpu.CompilerParams(vmem_limit_bytes=VMEM_LIMIT_V7X, dimension_semantics=sem)


def _sigmoid(z):
    return 1.0 / (1.0 + jnp.exp(-z))


def _softplus(z):
    return jnp.maximum(z, 0.0) + jnp.log(1.0 + jnp.exp(-jnp.abs(z)))


def _rows_sum(v):
    return jnp.sum(v, axis=0, keepdims=True)


class _W(NamedTuple):
    arr: jax.Array
    prefix: tuple = ()
    blocked: bool = False


def _w_spec(w, block2, pos):
    lead = (None,) * (len(w.prefix) + (1 if w.blocked else 0))
    if w.blocked:
        return pl.BlockSpec(lead + block2, lambda *g: (pos(*g)[0], *w.prefix, pos(*g)[1], pos(*g)[2]))
    return pl.BlockSpec(lead + block2, lambda *g: (*w.prefix, pos(*g)[1], pos(*g)[2]))


def _mm_nn(a, b, name, tn=None):
    m, k = a.shape
    if b.blocked:
        steps, bn = b.arr.shape[0], b.arr.shape[-1]
        b_spec = _w_spec(b, (k, bn), lambda n: (n, 0, 0))
    else:
        n_total = b.arr.shape[-1]
        bn = n_total if tn is None else tn
        steps = n_total // bn
        assert steps * bn == n_total
        b_spec = _w_spec(b, (k, bn), lambda n: (0, 0, n))
    tm = min(MM_ROWS, m)

    def body(a_ref, b_ref, o_ref):
        def step(i, carry):
            r = pl.ds(pl.multiple_of(i * tm, tm), tm)
            o_ref[r, :] = jnp.dot(a_ref[r, :], b_ref[...], preferred_element_type=F32)
            return carry
        lax.fori_loop(0, m // tm, step, 0)

    return pl.pallas_call(
        body, name=name, grid=(steps,),
        in_specs=[pl.BlockSpec((m, k), lambda n: (0, 0)), b_spec],
        out_specs=pl.BlockSpec((m, bn), lambda n: (0, n)),
        out_shape=jax.ShapeDtypeStruct((m, steps * bn), F32),
        compiler_params=_cparams(("arbitrary",)),
    )(a, b.arr)


def _mm_nt(dy, w, name, tk=None, tn=None):
    m, n_total = dy.shape
    k = w.arr.shape[-2]
    if w.blocked:
        bk, bn = k, w.arr.shape[-1]
        grid = (1, w.arr.shape[0])
        w_spec = _w_spec(w, (k, bn), lambda kt, n: (n, 0, 0))
    else:
        bk = k if tk is None else tk
        bn = n_total if tn is None else tn
        grid = (k // bk, n_total // bn)
        assert grid[0] * bk == k and grid[1] * bn == n_total
        w_spec = _w_spec(w, (bk, bn), lambda kt, n: (0, kt, n))
    tm = min(MM_ROWS, m)

    def body(dy_ref, w_ref, o_ref):
        def step(i, carry):
            r = pl.ds(pl.multiple_of(i * tm, tm), tm)
            o_ref[r, :] += lax.dot_general(dy_ref[r, :], w_ref[...], (((1,), (1,)), ((), ())),
                                           preferred_element_type=F32)
            return carry

        @pl.when(pl.program_id(1) == 0)
        def _():
            o_ref[...] = jnp.zeros_like(o_ref)
        lax.fori_loop(0, m // tm, step, 0)

    return pl.pallas_call(
        body, name=name, grid=grid,
        in_specs=[pl.BlockSpec((m, bn), lambda kt, n: (0, n)), w_spec],
        out_specs=pl.BlockSpec((m, bk), lambda kt, n: (0, kt)),
        out_shape=jax.ShapeDtypeStruct((m, k), F32),
        compiler_params=_cparams(("arbitrary", "arbitrary")),
    )(dy, w.arr)


def _mm_tn(x, dy, name, tk=None, tn=None, blocked_out=False):
    s, k = x.shape
    n_total = dy.shape[1]
    bk = k if tk is None else tk
    bn = n_total if tn is None else tn
    grid = (k // bk, n_total // bn)
    assert grid[0] * bk == k and grid[1] * bn == n_total
    ck = 256 if bk % 256 == 0 else 128

    def body(x_ref, dy_ref, o_ref):
        def step(i, carry):
            c = pl.ds(pl.multiple_of(i * ck, ck), ck)
            o_ref[c, :] = lax.dot_general(x_ref[:, c], dy_ref[...], (((0,), (0,)), ((), ())),
                                          preferred_element_type=F32)
            return carry
        lax.fori_loop(0, bk // ck, step, 0)

    if blocked_out:
        assert grid[0] == 1
        out_spec = pl.BlockSpec((None, bk, bn), lambda kt, n: (n, 0, 0))
        out_shape = jax.ShapeDtypeStruct((grid[1], k, bn), F32)
    else:
        out_spec = pl.BlockSpec((bk, bn), lambda kt, n: (kt, n))
        out_shape = jax.ShapeDtypeStruct((k, n_total), F32)
    return pl.pallas_call(
        body, name=name, grid=grid,
        in_specs=[pl.BlockSpec((s, bk), lambda kt, n: (0, kt)), pl.BlockSpec((s, bn), lambda kt, n: (0, n))],
        out_specs=out_spec, out_shape=out_shape,
        compiler_params=_cparams(("arbitrary", "arbitrary")),
    )(x, dy)


def _row_call(name, body, rows, fulls, row_outs, acc_outs, tr=ROW_TILE, after=None):
    s = rows[0].shape[0]
    tr = min(tr, s)
    in_specs = [pl.BlockSpec((tr, a.shape[1]), lambda i: (i, 0)) for a in rows]
    in_specs += [pl.BlockSpec(a.shape, lambda i: (0, 0)) for a in fulls]
    n_in = len(in_specs)
    order = [] if after is None else [after]
    in_specs += [pl.BlockSpec(memory_space=pl.ANY)] * len(order)
    out_specs = [pl.BlockSpec((tr, c), lambda i: (i, 0)) for c, _ in row_outs]
    out_specs += [pl.BlockSpec((1, c), lambda i: (0, 0)) for c, _ in acc_outs]
    out_shape = [jax.ShapeDtypeStruct((s, c), dt) for c, dt in row_outs]
    out_shape += [jax.ShapeDtypeStruct((1, c), dt) for c, dt in acc_outs]
    n_acc = len(acc_outs)

    def wrapped(*refs):
        refs = refs[:n_in] + refs[n_in + len(order):]
        if n_acc:
            @pl.when(pl.program_id(0) == 0)
            def _():
                for r in refs[len(refs) - n_acc:]:
                    r[...] = jnp.zeros_like(r)
        body(*refs)

    return pl.pallas_call(
        wrapped, name=name, grid=(s // tr,), in_specs=in_specs, out_specs=out_specs, out_shape=out_shape,
        compiler_params=_cparams(("arbitrary",)),
    )(*rows, *fulls, *order)


def _rms(v):
    return lax.rsqrt(jnp.mean(v * v, axis=-1, keepdims=True) + RMS_EPS)


def _pre_norm(x, g_pre, scale, shift, name, after=None):
    def body(x_ref, g_ref, sc_ref, sh_ref, h_ref):
        xv = x_ref[...]
        h = (xv * _rms(xv)) * g_ref[...] * (1.0 + sc_ref[...]) + sh_ref[...]
        h_ref[...] = h.astype(BF16)
    return _row_call(name, body, [x], [g_pre, scale, shift], [(D_MODEL, BF16)], [], after=after)[0]


def _post_norm(x, y, g_post, gate, coef, name):
    def body(x_ref, y_ref, g_ref, gate_ref, o_ref):
        yv = y_ref[...]
        o_ref[...] = x_ref[...] + (coef * gate_ref[...]) * ((yv * _rms(yv)) * g_ref[...])
    return _row_call(name, body, [x, y], [g_post, gate], [(D_MODEL, F32)], [])[0]


def _post_norm_bwd(dxo, y, g_post, gate, coef, name, after=None):
    def body(dxo_ref, y_ref, g_ref, gate_ref, dy_ref, dgate_ref, dg_ref):
        yv = y_ref[...]
        r2 = _rms(yv)
        yn = yv * r2
        dxo_v = dxo_ref[...]
        dgate_ref[...] += _rows_sum(dxo_v * (yn * g_ref[...])) * coef
        dz = dxo_v * (coef * gate_ref[...])
        dg_ref[...] += _rows_sum(dz * yn)
        dyn = dz * g_ref[...]
        dy = r2 * (dyn - yn * jnp.mean(dyn * yn, axis=-1, keepdims=True))
        dy_ref[...] = dy.astype(BF16)
    return _row_call(name, body, [dxo, y], [g_post, gate], [(D_MODEL, BF16)], [(D_MODEL, F32), (D_MODEL, F32)],
                     after=after)


def _pre_norm_bwd(dxo, dh, x, g_pre, scale, name):
    def body(dxo_ref, dh_ref, x_ref, g_ref, sc_ref, dx_ref, dshift_ref, dscale_ref, dg_ref):
        xv = x_ref[...]
        r = _rms(xv)
        xn = xv * r
        dh_v = dh_ref[...]
        one_sc = 1.0 + sc_ref[...]
        dshift_ref[...] += _rows_sum(dh_v)
        dscale_ref[...] += _rows_sum(dh_v * (xn * g_ref[...]))
        dg_ref[...] += _rows_sum(dh_v * xn * one_sc)
        dxn = dh_v * (g_ref[...] * one_sc)
        dx_ref[...] = dxo_ref[...] + r * (dxn - xn * jnp.mean(dxn * xn, axis=-1, keepdims=True))
    return _row_call(name, body, [dxo, dh, x], [g_pre, scale], [(D_MODEL, F32)],
                     [(D_MODEL, F32), (D_MODEL, F32), (D_MODEL, F32)])


def _swiglu_act(gu, name):
    def body(gu_ref, a_ref):
        g = gu_ref[:, :D_FF]
        a_ref[...] = (g * _sigmoid(g) * gu_ref[:, D_FF:]).astype(BF16)
    return _row_call(name, body, [gu], [], [(D_FF, BF16)], [])[0]


def _swiglu_act_bwd(da, gu, name):
    def body(da_ref, gu_ref, dgu_ref):
        g = gu_ref[:, :D_FF]
        sg = _sigmoid(g)
        da_v = da_ref[...]
        dgu_ref[:, :D_FF] = (da_v * gu_ref[:, D_FF:] * (sg * (1.0 + g * (1.0 - sg)))).astype(BF16)
        dgu_ref[:, D_FF:] = (da_v * (g * sg)).astype(BF16)
    return _row_call(name, body, [da, gu], [], [(2 * D_FF, BF16)], [])[0]


def _loss_head(y, target, name):
    def body(y_ref, t_ref, dy_ref, loss_ref):
        e = y_ref[...] - t_ref[...]
        dy_ref[...] = e * (1.0 / D_MODEL)
        part = jnp.sum(jnp.mean(e * e, axis=-1, keepdims=True), axis=0, keepdims=True) * 0.5
        loss_ref[...] += jnp.broadcast_to(part, loss_ref.shape)
    return _row_call(name, body, [y, target], [], [(D_MODEL, F32)], [(128, F32)])


def _lane_scan(v, reverse):
    s = v.shape[1]
    lane = lax.broadcasted_iota(jnp.int32, v.shape, 1)
    d = 1
    while d < s:
        if reverse:
            v = v + jnp.where(lane < s - d, pltpu.roll(v, s - d, 1), 0.0)
        else:
            v = v + jnp.where(lane >= d, pltpu.roll(v, d, 1), 0.0)
        d *= 2
    return v


def _fox_gate(flt, b_f, name):
    def body(f_ref, b_ref, cum_ref):
        z = f_ref[...] + b_ref[...]
        cum_ref[...] = _lane_scan(-_softplus(-z), reverse=False)
    return pl.pallas_call(body, name=name, out_shape=jax.ShapeDtypeStruct(flt.shape, F32),
                          compiler_params=_cparams())(flt, b_f)


def _fox_gate_bwd(dcum_q, dcum_k, flt, b_f, name):
    def body(dq_ref, dk_ref, f_ref, b_ref, df_ref, db_ref):
        z = f_ref[...] + b_ref[...]
        df = _lane_scan(dq_ref[...] + dk_ref[...], reverse=True) * _sigmoid(-z)
        df_ref[...] = df
        db_ref[...] = jnp.sum(df, axis=1, keepdims=True)
    h = flt.shape[0]
    return pl.pallas_call(body, name=name,
                          out_shape=(jax.ShapeDtypeStruct(flt.shape, F32), jax.ShapeDtypeStruct((h, 1), F32)),
                          compiler_params=_cparams())(dcum_q, dcum_k, flt, b_f)


def _pick_head(block, h):
    lane = lax.broadcasted_iota(jnp.int32, block.shape, 1)
    return jnp.sum(jnp.where(lane == h, block, 0.0), axis=1, keepdims=True)


def _put_head(ref, col, h):
    @pl.when(h == 0)
    def _():
        ref[...] = jnp.zeros_like(ref)
    lane = lax.broadcasted_iota(jnp.int32, ref.shape, 1)
    ref[...] = jnp.where(lane == h, col, ref[...])


_NT = (((1,), (1,)), ((), ()))
_FOX_SCALE = FOX_HEAD_DIM ** -0.5


def _causal(s_tile, t):
    row = lax.broadcasted_iota(jnp.int32, (t, t), 0)
    col = lax.broadcasted_iota(jnp.int32, (t, t), 1)
    return jnp.where(col <= row, s_tile, -jnp.inf)


HEAD_PAIRS = FOX_HEADS // 2
PAIR_W = 2 * FOX_HEAD_DIM


def _low_half(shape):
    return lax.broadcasted_iota(jnp.int32, shape, 1) < FOX_HEAD_DIM


def _fox_attn_fwd(qkv, cum, cum_t, name):
    s = qkv.shape[0]
    t = min(ATT_TILE, s)

    def body(q_ref, k_ref, v_ref, cum_ref, cumt_ref, o_ref, ob_ref, lse_ref):
        i = pl.program_id(0)
        hp = pl.program_id(1)
        lo = _low_half((t, PAIR_W))
        qv = q_ref[...]
        zero = jnp.zeros_like(qv)
        q2 = (jnp.where(lo, qv, zero), jnp.where(lo, zero, qv))
        cum_v = cum_ref[...]
        cq2 = (_pick_head(cum_v, 2 * hp), _pick_head(cum_v, 2 * hp + 1))

        def step(j, carry, masked):
            ks = pl.ds(pl.multiple_of(j * t, t), t)
            kj = k_ref[ks, :]
            vj = v_ref[ks, :]
            out = []
            for e in range(2):
                m, l, acc = carry[e]
                sc = lax.dot_general(q2[e], kj, _NT, preferred_element_type=F32) * _FOX_SCALE
                sc = sc + cq2[e] - cumt_ref[e:e + 1, ks]
                if masked:
                    sc = _causal(sc, t)
                m_new = jnp.maximum(m, jnp.max(sc, axis=1, keepdims=True))
                alpha = jnp.exp(m - m_new)
                p = jnp.exp(sc - m_new)
                l = alpha * l + jnp.sum(p, axis=1, keepdims=True)
                acc = alpha * acc + jnp.dot(p.astype(BF16), vj, preferred_element_type=F32)
                out.append((m_new, l, acc))
            return tuple(out)

        one = (jnp.full((t, 1), -jnp.inf, F32), jnp.zeros((t, 1), F32), jnp.zeros((t, PAIR_W), F32))
        carry = lax.fori_loop(0, i, lambda j, c: step(j, c, False), (one, one))
        (m0, l0, a0), (m1, l1, a1) = step(i, carry, True)
        o = jnp.where(lo, a0 / l0, a1 / l1)
        o_ref[...] = o
        ob_ref[...] = o.astype(BF16)
        _put_head(lse_ref, m0 + jnp.log(l0), 2 * hp)
        _put_head(lse_ref, m1 + jnp.log(l1), 2 * hp + 1)

    nat_tile = pl.BlockSpec((t, FOX_HEADS), lambda i, hp: (i, 0))
    out_tile = pl.BlockSpec((t, PAIR_W), lambda i, hp: (i, hp))
    return pl.pallas_call(
        body, name=name, grid=(s // t, HEAD_PAIRS),
        in_specs=[pl.BlockSpec((t, PAIR_W), lambda i, hp: (i, hp)),
                  pl.BlockSpec((s, PAIR_W), lambda i, hp: (0, HEAD_PAIRS + hp)),
                  pl.BlockSpec((s, PAIR_W), lambda i, hp: (0, 2 * HEAD_PAIRS + hp)),
                  nat_tile, pl.BlockSpec((None, 2, s), lambda i, hp: (hp, 0, 0))],
        out_specs=[out_tile, out_tile, nat_tile],
        out_shape=[jax.ShapeDtypeStruct((s, D_MODEL), F32), jax.ShapeDtypeStruct((s, D_MODEL), BF16),
                   jax.ShapeDtypeStruct((s, FOX_HEADS), F32)],
        compiler_params=_cparams(("arbitrary", "arbitrary")),
    )(qkv, qkv, qkv, cum, cum_t)


def _fox_delta(do, o, expand, name):
    def body(do_ref, o_ref, e_ref, d_ref):
        prod = do_ref[...] * o_ref[...]
        hi = prod.astype(BF16)
        lo = (prod - hi.astype(F32)).astype(BF16)
        tot = (jnp.dot(hi, e_ref[...], preferred_element_type=F32)
               + jnp.dot(lo, e_ref[...], preferred_element_type=F32))
        d_ref[...] = tot[:, :FOX_HEADS]
    return _row_call(name, body, [do, o], [expand], [(FOX_HEADS, F32)], [])[0]


def _fox_attn_bwd(qkv, do, cum, cum_t, lse_t, delta_t, name):
    s = qkv.shape[0]
    t = min(ATT_TILE, s)
    nq = s // t
    tn_dims = (((0,), (0,)), ((), ()))

    def body(q_ref, k_ref, v_ref, do_ref, cum_ref, cumt_ref, lset_ref, deltat_ref,
             dq_ref, dk_ref, dv_ref, dck_ref, dcq_ref):
        hp = pl.program_id(0)
        j = pl.program_id(1)

        @pl.when(j == 0)
        def _():
            dq_ref[...] = jnp.zeros_like(dq_ref)
            dcq_ref[...] = jnp.zeros_like(dcq_ref)
        dk_ref[...] = jnp.zeros_like(dk_ref)
        dv_ref[...] = jnp.zeros_like(dv_ref)

        lo = _low_half((t, PAIR_W))
        lane = lax.broadcasted_iota(jnp.int32, (t, PAIR_W), 1)
        kv = k_ref[...]
        vv = v_ref[...]
        zero = jnp.zeros_like(kv)
        k2 = (jnp.where(lo, kv, zero), jnp.where(lo, zero, kv))
        v2 = (jnp.where(lo, vv, zero), jnp.where(lo, zero, vv))
        cum_v = cum_ref[...]
        ck2 = (_pick_head(cum_v, 2 * hp), _pick_head(cum_v, 2 * hp + 1))

        def step(i, dck, masked):
            qs = pl.ds(pl.multiple_of(i * t, t), t)
            qi = q_ref[qs, :]
            do_i = do_ref[qs, :].astype(BF16)
            dv_p, dk_p, dq_p = [], [], []
            for e in range(2):
                st = lax.dot_general(k2[e], qi, _NT, preferred_element_type=F32) * _FOX_SCALE
                st = st + cumt_ref[e:e + 1, qs] - ck2[e]
                if masked:
                    row = lax.broadcasted_iota(jnp.int32, (t, t), 0)
                    col = lax.broadcasted_iota(jnp.int32, (t, t), 1)
                    st = jnp.where(row <= col, st, -jnp.inf)
                pt = jnp.exp(st - lset_ref[e:e + 1, qs])
                dv_p.append(jnp.dot(pt.astype(BF16), do_i, preferred_element_type=F32))
                dpt = lax.dot_general(v2[e], do_i, _NT, preferred_element_type=F32)
                dst = pt * (dpt - deltat_ref[e:e + 1, qs])
                dsb = dst.astype(BF16)
                dk_p.append(jnp.dot(dsb, qi, preferred_element_type=F32))
                dq_p.append(lax.dot_general(dsb, kv, tn_dims, preferred_element_type=F32))
                dck = dck - jnp.where(lane == e, jnp.sum(dst, axis=1, keepdims=True), 0.0)
                dcq_ref[e:e + 1, qs] += jnp.sum(dst, axis=0, keepdims=True)
            dv_ref[...] += jnp.where(lo, dv_p[0], dv_p[1])
            dk_ref[...] += jnp.where(lo, dk_p[0], dk_p[1])
            dq_ref[qs, :] += jnp.where(lo, dq_p[0], dq_p[1]) * _FOX_SCALE
            return dck

        dck = step(j, jnp.zeros((t, PAIR_W), F32), True)
        dck = lax.fori_loop(j + 1, nq, lambda i, c: step(i, c, False), dck)
        dk_ref[...] = dk_ref[...] * _FOX_SCALE
        dck_ref[...] = dck

    pair_full = lambda part: pl.BlockSpec((s, PAIR_W), lambda hp, j: (0, part * HEAD_PAIRS + hp))
    pair_tile = lambda part: pl.BlockSpec((t, PAIR_W), lambda hp, j: (j, part * HEAD_PAIRS + hp))
    rows = pl.BlockSpec((None, 2, s), lambda hp, j: (hp, 0, 0))
    return pl.pallas_call(
        body, name=name, grid=(HEAD_PAIRS, nq),
        in_specs=[pair_full(0), pair_tile(1), pair_tile(2), pair_full(0),
                  pl.BlockSpec((t, FOX_HEADS), lambda hp, j: (j, 0)), rows, rows, rows],
        out_specs=[pair_full(0), pair_tile(0), pair_tile(0),
                   pl.BlockSpec((None, t, PAIR_W), lambda hp, j: (hp, j, 0)), rows],
        out_shape=[jax.ShapeDtypeStruct((s, D_MODEL), F32)] * 3
        + [jax.ShapeDtypeStruct((HEAD_PAIRS, s, PAIR_W), F32), jax.ShapeDtypeStruct((HEAD_PAIRS, 2, s), F32)],
        compiler_params=_cparams(("arbitrary", "arbitrary")),
    )(qkv, qkv, qkv, do, cum, cum_t, lse_t, delta_t)


def _shift_down(v, d):
    row = lax.broadcasted_iota(jnp.int32, v.shape, 0)
    return jnp.where(row >= d, pltpu.roll(v, d, 0), 0.0)


def _shift_up(v, d):
    s = v.shape[0]
    row = lax.broadcasted_iota(jnp.int32, v.shape, 0)
    return jnp.where(row < s - d, pltpu.roll(v, s - d, 0), 0.0)


def _conv_taps(v, cw_ref, width):
    out = cw_ref[width - 1:width, :] * v
    for k in range(width - 1):
        out = out + cw_ref[k:k + 1, :] * _shift_down(v, width - 1 - k)
    return out


def _conv_taps_bwd(dout, v, cw_ref, dcw_ref, width):
    dv = cw_ref[width - 1:width, :] * dout
    dcw_ref[width - 1:width, :] = _rows_sum(dout * v)
    for k in range(width - 1):
        d = width - 1 - k
        dv = dv + cw_ref[k:k + 1, :] * _shift_up(dout, d)
        dcw_ref[k:k + 1, :] = _rows_sum(dout * _shift_down(v, d))
    return dv


def _col_spec(s, tc, part=0):
    off = part * (D_MODEL // tc)
    return pl.BlockSpec((s, tc), lambda c: (0, c + off))


def _small_spec(rows, tc):
    return pl.BlockSpec((rows, tc), lambda c: (0, c))


def _col_call(name, body, in_arrays, in_specs, out_rows, s, tc):
    return pl.pallas_call(
        body, name=name, grid=(D_MODEL // tc,), in_specs=in_specs,
        out_specs=[pl.BlockSpec((r, tc), lambda c: (0, c)) for r, _ in out_rows],
        out_shape=[jax.ShapeDtypeStruct((r, D_MODEL), dt) for r, dt in out_rows],
        compiler_params=_cparams(("arbitrary",)),
    )(*in_arrays)


def _sconv_fwd(proj, conv_w, name):
    s = proj.shape[0]
    tc = COL_TILE

    def body(b_ref, c_ref, x_ref, cw_ref, y_ref):
        y_ref[...] = (b_ref[...] * _conv_taps(c_ref[...] * x_ref[...], cw_ref, 3)).astype(BF16)

    return _col_call(name, body, [proj, proj, proj, conv_w],
                     [_col_spec(s, tc, 0), _col_spec(s, tc, 1), _col_spec(s, tc, 2), _small_spec(3, tc)],
                     [(s, BF16)], s, tc)[0]


def _sconv_bwd(dy, proj, conv_w, name):
    s = proj.shape[0]
    tc = COL_TILE

    def body(dy_ref, b_ref, c_ref, x_ref, cw_ref, db_ref, dc_ref, dx_ref, dcw_ref):
        w = c_ref[...] * x_ref[...]
        dy_v = dy_ref[...]
        db_ref[...] = (dy_v * _conv_taps(w, cw_ref, 3)).astype(BF16)
        dw = _conv_taps_bwd(dy_v * b_ref[...], w, cw_ref, dcw_ref, 3)
        dc_ref[...] = (dw * x_ref[...]).astype(BF16)
        dx_ref[...] = (dw * c_ref[...]).astype(BF16)

    return _col_call(name, body, [dy, proj, proj, proj, conv_w],
                     [_col_spec(s, tc), _col_spec(s, tc, 0), _col_spec(s, tc, 1), _col_spec(s, tc, 2),
                      _small_spec(3, tc)],
                     [(s, BF16), (s, BF16), (s, BF16), (3, F32)], s, tc)


def _lru_conv(proj, conv_w, conv_b, name):
    s = proj.shape[0]
    tc = COL_TILE

    def body(x_ref, cw_ref, cb_ref, xb_ref, xbb_ref):
        xb = _conv_taps(x_ref[...], cw_ref, 4) + cb_ref[...]
        xb_ref[...] = xb
        xbb_ref[...] = xb.astype(BF16)

    return _col_call(name, body, [proj, conv_w, conv_b],
                     [_col_spec(s, tc, 1), _small_spec(4, tc), _small_spec(1, tc)],
                     [(s, F32), (s, BF16)], s, tc)


def _lru_conv_bwd(dxb1, dxb2, proj, conv_w, name):
    s = proj.shape[0]
    tc = COL_TILE

    def body(d1_ref, d2_ref, x_ref, cw_ref, dx_ref, dcw_ref, dcb_ref):
        dxb = d1_ref[...] + d2_ref[...]
        dcb_ref[...] = _rows_sum(dxb)
        dx_ref[...] = _conv_taps_bwd(dxb, x_ref[...], cw_ref, dcw_ref, 4).astype(BF16)

    return _col_call(name, body, [dxb1, dxb2, proj, conv_w],
                     [_col_spec(s, tc), _col_spec(s, tc), _col_spec(s, tc, 1), _small_spec(4, tc)],
                     [(s, BF16), (4, F32), (1, F32)], s, tc)


_GELU_C = math.sqrt(2.0 / math.pi)


def _gelu_parts(g):
    inner = _GELU_C * (g + 0.044715 * g * g * g)
    th = jnp.tanh(inner)
    val = 0.5 * g * (1.0 + th)
    der = 0.5 * (1.0 + th) + 0.5 * g * (1.0 - th * th) * (_GELU_C * (1.0 + 3.0 * 0.044715 * g * g))
    return val, der


def _lru_gates(pa_ref, px_ref, ba_ref, bx_ref, lam_ref):
    r = _sigmoid(pa_ref[...] + ba_ref[...])
    ig = _sigmoid(px_ref[...] + bx_ref[...])
    sp = _softplus(-lam_ref[...])
    log_a = (-LRU_C) * r * sp
    a = jnp.exp(log_a)
    z = 2.0 * log_a
    one_m_a2 = jnp.where(z > -1e-3, -(z * (1.0 + z * (0.5 + z * (1.0 / 6.0)))), 1.0 - jnp.exp(z))
    return r, ig, sp, a, jnp.sqrt(one_m_a2)


def _lru_scan(pre, xb, proj, b_a, b_x, lam, name):
    s = xb.shape[0]
    tc = COL_TILE

    def body(pa_ref, px_ref, xb_ref, g_ref, ba_ref, bx_ref, lam_ref, y_ref, hs_ref):
        _, ig, _, a, mult = _lru_gates(pa_ref, px_ref, ba_ref, bx_ref, lam_ref)
        b = mult * (ig * xb_ref[...])
        d = 1
        while d < s:
            row = lax.broadcasted_iota(jnp.int32, a.shape, 0)
            keep = row >= d
            b = b + a * jnp.where(keep, pltpu.roll(b, d, 0), 0.0)
            a = a * jnp.where(keep, pltpu.roll(a, d, 0), 1.0)
            d *= 2
        hs_ref[...] = b
        y_ref[...] = (b * _gelu_parts(g_ref[...])[0]).astype(BF16)

    return _col_call(name, body, [pre, pre, xb, proj, b_a, b_x, lam],
                     [_col_spec(s, tc, 0), _col_spec(s, tc, 1), _col_spec(s, tc), _col_spec(s, tc, 0),
                      _small_spec(1, tc), _small_spec(1, tc), _small_spec(1, tc)],
                     [(s, BF16), (s, F32)], s, tc)


def _lru_scan_bwd(dy, pre, xb, proj, hs, b_a, b_x, lam, name):
    s = xb.shape[0]
    tc = COL_TILE

    def body(dy_ref, pa_ref, px_ref, xb_ref, g_ref, hs_ref, ba_ref, bx_ref, lam_ref,
             dg_ref, dpa_ref, dpx_ref, dxb_ref, dba_ref, dbx_ref, dlam_ref):
        r, ig, sp, a, mult = _lru_gates(pa_ref, px_ref, ba_ref, bx_ref, lam_ref)
        gl, gl_der = _gelu_parts(g_ref[...])
        dy_v = dy_ref[...]
        hs_v = hs_ref[...]
        dg_ref[...] = (dy_v * hs_v * gl_der).astype(BF16)
        lam_t = dy_v * gl
        coef = _shift_up(a, 1)
        d = 1
        while d < s:
            row = lax.broadcasted_iota(jnp.int32, coef.shape, 0)
            keep = row < s - d
            lam_t = lam_t + coef * jnp.where(keep, pltpu.roll(lam_t, s - d, 0), 0.0)
            coef = coef * jnp.where(keep, pltpu.roll(coef, s - d, 0), 1.0)
            d *= 2
        xb_v = xb_ref[...]
        da = lam_t * _shift_down(hs_v, 1)
        dmult = lam_t * (ig * xb_v)
        dig = lam_t * mult * xb_v
        dxb_ref[...] = lam_t * mult * ig
        dlog_a = da * a - dmult * (a * a) / mult
        dr = dlog_a * ((-LRU_C) * sp)
        dsp = _rows_sum(dlog_a * ((-LRU_C) * r))
        dlam_ref[...] = -dsp * _sigmoid(-lam_ref[...])
        dpa = dr * r * (1.0 - r)
        dpx = dig * ig * (1.0 - ig)
        dba_ref[...] = _rows_sum(dpa)
        dbx_ref[...] = _rows_sum(dpx)
        dpa_ref[...] = dpa.astype(BF16)
        dpx_ref[...] = dpx.astype(BF16)

    return _col_call(name, body, [dy, pre, pre, xb, proj, hs, b_a, b_x, lam],
                     [_col_spec(s, tc), _col_spec(s, tc, 0), _col_spec(s, tc, 1), _col_spec(s, tc),
                      _col_spec(s, tc, 0), _col_spec(s, tc),
                      _small_spec(1, tc), _small_spec(1, tc), _small_spec(1, tc)],
                     [(s, BF16), (s, BF16), (s, BF16), (s, F32), (1, F32), (1, F32), (1, F32)], s, tc)


def _ffn_fwd(x, w_in, w_out, g_pre, g_post, shift, scale, gate, tag, after=None):
    h = _pre_norm(x, g_pre, scale, shift, tag + "_pre", after=after)
    gu = _mm_nn(h, w_in, tag + "_in")
    a = _swiglu_act(gu, tag + "_act")
    y = _mm_nn(a, w_out, tag + "_out", tn=512)
    xo = _post_norm(x, y, g_post, gate, 0.5, tag + "_post")
    return xo, (x, h, gu, a, y)


def _ffn_bwd(dxo, saved, w_in, w_out, g_pre, g_post, scale, gate, tag, after=None):
    x, h, gu, a, y = saved
    dy, dgate, dg_post = _post_norm_bwd(dxo, y, g_post, gate, 0.5, tag + "_post_b", after=after)
    da = _mm_nt(dy, w_out, tag + "_out_bx", tk=D_FF // 2)
    dw_out = _mm_tn(a, dy, tag + "_out_bw", tk=D_FF // 2)
    dgu = _swiglu_act_bwd(da, gu, tag + "_act_b")
    dh = _mm_nt(dgu, w_in, tag + "_in_bx")
    dw_in = _mm_tn(h, dgu, tag + "_in_bw", tn=w_in.arr.shape[-1], blocked_out=True)
    dx, dshift, dscale, dg_pre = _pre_norm_bwd(dxo, dh, x, g_pre, scale, tag + "_pre_b")
    return dx, dw_in, dw_out, (dshift, dscale, dgate), dg_pre, dg_post


def _pair_rows(v):
    return v.T.reshape(HEAD_PAIRS, 2, v.shape[0])


def _fox_fwd(h, p, tag):
    s = h.shape[0]
    proj = _mm_nn(h, p["w_in"], tag + "_in", tn=640)
    qkv = proj[:, :3 * D_MODEL].astype(BF16)
    flt = proj[:, 3 * D_MODEL:3 * D_MODEL + FOX_HEADS].T
    cum_t = _fox_gate(flt, p["b_f"], tag + "_gate")
    cum = cum_t.T
    cum_t2 = cum_t.reshape(HEAD_PAIRS, 2, s)
    o, ob, lse = _fox_attn_fwd(qkv, cum, cum_t2, tag + "_attn")
    y = _mm_nn(ob, p["w_out"], tag + "_out")
    return y, (qkv, flt, cum, cum_t2, o, ob, lse)


def _fox_bwd(dy, h, saved, p, tag):
    qkv, flt, cum, cum_t2, o, ob, lse = saved
    s = h.shape[0]
    do = _mm_nt(dy, p["w_out"], tag + "_out_bx")
    dw_out = _mm_tn(ob, dy, tag + "_out_bw")
    expand = jnp.pad(jnp.repeat(jnp.eye(FOX_HEADS, dtype=BF16), FOX_HEAD_DIM, axis=0),
                     ((0, 0), (0, PAIR_W - FOX_HEADS)))
    delta = _fox_delta(do, o, expand, tag + "_attn_delta")
    dq, dk, dv, dck, dcq = _fox_attn_bwd(qkv, do, cum, cum_t2, _pair_rows(lse), _pair_rows(delta), tag + "_attn_b")
    dcum_k = dck[:, :, :2].transpose(0, 2, 1).reshape(FOX_HEADS, s)
    dflt, db_f = _fox_gate_bwd(dcq.reshape(FOX_HEADS, s), dcum_k, flt, p["b_f"], tag + "_gate_b")
    dproj = jnp.concatenate(
        [dq, dk, dv, dflt.T, jnp.zeros((s, FOX_PAD - 3 * D_MODEL - FOX_HEADS), F32)], axis=1).astype(BF16)
    dh = _mm_nt(dproj, p["w_in"], tag + "_in_bx", tn=640)
    dw_in = _mm_tn(h, dproj, tag + "_in_bw", tn=640)
    return dh, {"w_in": dw_in, "w_out": dw_out, "b_f": db_f}


def _sconv_mix_fwd(h, p, tag):
    proj = _mm_nn(h, p["w_in"], tag + "_in")
    yb = _sconv_fwd(proj, p["conv_w"], tag + "_conv")
    y = _mm_nn(yb, p["w_out"], tag + "_out")
    return y, (proj, yb)


def _sconv_mix_bwd(dy, h, saved, p, tag):
    proj, yb = saved
    dyb = _mm_nt(dy, p["w_out"], tag + "_out_bx")
    dw_out = _mm_tn(yb, dy, tag + "_out_bw")
    db, dc, dxv, dcw = _sconv_bwd(dyb, proj, p["conv_w"], tag + "_conv_b")
    dproj = jnp.concatenate([db, dc, dxv], axis=1)
    dh = _mm_nt(dproj, p["w_in"], tag + "_in_bx")
    dw_in = _mm_tn(h, dproj, tag + "_in_bw", tn=p["w_in"].arr.shape[-1], blocked_out=True)
    return dh, {"w_in": dw_in, "w_out": dw_out, "conv_w": dcw}


def _lru_mix_fwd(h, p, tag):
    proj = _mm_nn(h, p["w_in"], tag + "_in")
    xb, xbb = _lru_conv(proj, p["conv_w"], p["conv_b"], tag + "_conv")
    pre = _mm_nn(xbb, p["w_ax"], tag + "_gates", tn=D_MODEL)
    yb, hs = _lru_scan(pre, xb, proj, p["b_a"], p["b_x"], p["lam"], tag + "_scan")
    y = _mm_nn(yb, p["w_out"], tag + "_out")
    return y, (proj, xb, xbb, pre, yb, hs)


def _diag_blocks(m):
    return jnp.stack([m[LRU_BLOCK_DIM * n:LRU_BLOCK_DIM * (n + 1), LRU_BLOCK_DIM * n:LRU_BLOCK_DIM * (n + 1)]
                      for n in range(LRU_BLOCKS)])


def _lru_mix_bwd(dy, h, saved, p, tag):
    proj, xb, xbb, pre, yb, hs = saved
    dyb = _mm_nt(dy, p["w_out"], tag + "_out_bx")
    dw_out = _mm_tn(yb, dy, tag + "_out_bw")
    dg, dpa, dpx, dxb1, dba, dbx, dlam = _lru_scan_bwd(dyb, pre, xb, proj, hs, p["b_a"], p["b_x"], p["lam"],
                                                       tag + "_scan_b")
    dpre = jnp.concatenate([dpa, dpx], axis=1)
    dxb2 = _mm_nt(dpre, p["w_ax"], tag + "_gates_bx", tn=D_MODEL)
    dw_ax = _mm_tn(xbb, dpre, tag + "_gates_bw", tn=D_MODEL)
    dx0, dcw, dcb = _lru_conv_bwd(dxb1, dxb2, proj, p["conv_w"], tag + "_conv_b")
    dproj = jnp.concatenate([dg, dx0], axis=1)
    dh = _mm_nt(dproj, p["w_in"], tag + "_in_bx")
    dw_in = _mm_tn(h, dproj, tag + "_in_bw", tn=p["w_in"].arr.shape[-1], blocked_out=True)
    grads = {"w_in": dw_in, "w_out": dw_out, "conv_w": dcw, "conv_b": dcb,
             "w_a": _diag_blocks(dw_ax[:, :D_MODEL]), "w_x": _diag_blocks(dw_ax[:, D_MODEL:]),
             "b_a": dba, "b_x": dbx, "lam": dlam}
    return dh, grads


_MIXERS = ((_fox_fwd, _fox_bwd), (_sconv_mix_fwd, _sconv_mix_bwd), (_lru_mix_fwd, _lru_mix_bwd))


def _local_step(x, target, mod, layer_params, on_grads=None, first_after=None):
    layers = []
    tape = []
    for i in range(DEPTH):
        lp = dict(layer_params(i, 0, x))
        layers.append(lp)
        row = lambda v: v[None, :]
        m = lambda sub, what: mod[i, sub, what][None, :]
        x, sv0 = _ffn_fwd(x, lp["ffn_in"][0], lp["ffn_out"][0], row(lp["norm_pre"][0]), row(lp["norm_post"][0]),
                          m(0, 0), m(0, 1), m(0, 2), f"l{i}_ffn0", after=first_after if i == 0 else None)
        lp.update(layer_params(i, 1, x))
        h = _pre_norm(x, row(lp["norm_pre"][1]), m(1, 1), m(1, 0), f"l{i}_mix_pre")
        y, svm = _MIXERS[i % 3][0](h, lp["mixer"], f"l{i}_mix")
        x1 = _post_norm(x, y, row(lp["norm_post"][1]), m(1, 2), 1.0, f"l{i}_mix_post")
        second = layer_params(i, 2, x1)
        lp["ffn_in"] = lp["ffn_in"] + second["ffn_in"]
        lp["ffn_out"] = lp["ffn_out"] + second["ffn_out"]
        x2, sv2 = _ffn_fwd(x1, lp["ffn_in"][1], lp["ffn_out"][1], row(lp["norm_pre"][2]), row(lp["norm_post"][2]),
                           m(2, 0), m(2, 1), m(2, 2), f"l{i}_ffn1")
        tape.append((sv0, (x, h, y, svm), sv2))
        x = x2
    dx, loss_row = _loss_head(x, target, "loss_head")

    layer_grads = [None] * DEPTH
    dmod = [None] * DEPTH
    after = None
    for i in reversed(range(DEPTH)):
        lp = layers[i]
        row = lambda v: v[None, :]
        m = lambda sub, what: mod[i, sub, what][None, :]
        sv0, (xm, h, y, svm), sv2 = tape[i]
        dx, dw_in1, dw_out1, dm2, dgp2, dgq2 = _ffn_bwd(dx, sv2, lp["ffn_in"][1], lp["ffn_out"][1],
                                                        row(lp["norm_pre"][2]), row(lp["norm_post"][2]),
                                                        m(2, 1), m(2, 2), f"l{i}_ffn1", after=after)
        dy, dgate1, dgq1 = _post_norm_bwd(dx, y, row(lp["norm_post"][1]), m(1, 2), 1.0, f"l{i}_mix_post_b")
        dh, mg = _MIXERS[i % 3][1](dy, h, svm, lp["mixer"], f"l{i}_mix")
        dx, dshift1, dscale1, dgp1 = _pre_norm_bwd(dx, dh, xm, row(lp["norm_pre"][1]), m(1, 1), f"l{i}_mix_pre_b")
        dx, dw_in0, dw_out0, dm0, dgp0, dgq0 = _ffn_bwd(dx, sv0, lp["ffn_in"][0], lp["ffn_out"][0],
                                                        row(lp["norm_pre"][0]), row(lp["norm_post"][0]),
                                                        m(0, 1), m(0, 2), f"l{i}_ffn0")
        dmod[i] = jnp.concatenate([*dm0, dshift1, dscale1, dgate1, *dm2], axis=0).reshape(N_SUB, 3, D_MODEL)
        layer_grads[i] = {"ffn_in": (dw_in0, dw_in1), "ffn_out": (dw_out0, dw_out1),
                          "norm_pre": jnp.concatenate([dgp0, dgp1, dgp2], axis=0),
                          "norm_post": jnp.concatenate([dgq0, dgq1, dgq2], axis=0), "mixer": mg}
        if on_grads is not None:
            after = on_grads(i, layer_grads[i], dx)
    return loss_row, dx, jnp.stack(dmod), layer_grads


COND_ROWS = 16
COND_PAD = 128


def _cond_fwd(c_pad, w_cond, b_shard, name):
    nl, d, n = w_cond.shape
    tn = 768

    def body(c_ref, w_ref, b_ref, o_ref):
        cv = c_ref[...]
        act = (cv * _sigmoid(cv)).astype(BF16)
        o_ref[...] = jnp.dot(act, w_ref[...].astype(BF16), preferred_element_type=F32) + b_ref[...]

    return pl.pallas_call(
        body, name=name, grid=(nl, n // tn),
        in_specs=[pl.BlockSpec((COND_ROWS, d), lambda i, j: (0, 0)),
                  pl.BlockSpec((None, d, tn), lambda i, j: (i, 0, j)),
                  pl.BlockSpec((None, 1, tn), lambda i, j: (i, 0, j))],
        out_specs=pl.BlockSpec((None, COND_ROWS, tn), lambda i, j: (i, 0, j)),
        out_shape=jax.ShapeDtypeStruct((nl, COND_ROWS, n), F32),
        compiler_params=_cparams(("arbitrary", "arbitrary")),
    )(c_pad, w_cond, b_shard)


def _adam_math(w, g, m, v):
    nm = ADAM_B1 * m + (1.0 - ADAM_B1) * g
    nv = ADAM_B2 * v + (1.0 - ADAM_B2) * (g * g)
    m_hat = nm / (1.0 - ADAM_B1 ** ADAM_STEP)
    v_hat = nv / (1.0 - ADAM_B2 ** ADAM_STEP)
    delta = (-ADAM_LR) * (m_hat / (jnp.sqrt(v_hat) + ADAM_EPS) + ADAM_WD * w)
    return delta, nm, nv


def _cond_bwd_adamw(c_t, dmod_s, w, m, v, name):
    nl, d, n = w.shape
    tn = 384
    blk = pl.BlockSpec((None, d, tn), lambda i, j: (i, 0, j))

    def body(c_ref, dm_ref, w_ref, m_ref, v_ref, g_ref, d_ref, nm_ref, nv_ref):
        cv = c_ref[...]
        g = jnp.dot((cv * _sigmoid(cv)).astype(BF16), dm_ref[...], preferred_element_type=F32)
        g_ref[...] = g
        d_ref[...], nm_ref[...], nv_ref[...] = _adam_math(w_ref[...], g, m_ref[...], v_ref[...])

    return pl.pallas_call(
        body, name=name, grid=(nl, n // tn),
        in_specs=[pl.BlockSpec((d, COND_PAD), lambda i, j: (0, 0)),
                  pl.BlockSpec((None, COND_PAD, tn), lambda i, j: (i, 0, j)), blk, blk, blk],
        out_specs=[blk] * 4, out_shape=[jax.ShapeDtypeStruct(w.shape, F32)] * 4,
        compiler_params=_cparams(("arbitrary", "arbitrary")),
    )(c_t, dmod_s, w, m, v)


def _adamw(w, g, m, v, name):
    rows, cols = w.shape
    tr = next(t for t in (256, 176, 128, 64, 32, 16, 8) if rows % t == 0)
    blk = pl.BlockSpec((tr, cols), lambda i: (i, 0))

    def body(w_ref, g_ref, m_ref, v_ref, d_ref, nm_ref, nv_ref):
        d_ref[...], nm_ref[...], nv_ref[...] = _adam_math(w_ref[...], g_ref[...], m_ref[...], v_ref[...])

    return pl.pallas_call(
        body, name=name, grid=(rows // tr,), in_specs=[blk] * 4, out_specs=[blk] * 3,
        out_shape=[jax.ShapeDtypeStruct(w.shape, F32)] * 3, compiler_params=_cparams(("arbitrary",)),
    )(w, g, m, v)


_MESH = pl.DeviceIdType.MESH
_ANY = pl.BlockSpec(memory_space=pl.ANY)


def _place():
    return lax.axis_index("x"), lax.axis_index("y"), lax.axis_index("c")


def _other_chips(x, y):
    return [(1 - x, y), (x, 1 - y), (1 - x, 1 - y)]


def _allgather8(block, name):
    m_per, n = block.shape

    def body(x_ref, out_ref, send_sems, recv_sems, local_sem):
        x, y, c = _place()
        me, sibling = (x, y, c), (x, y, 1 - c)
        chips = _other_chips(x, y)

        def rows(px, py, pc):
            return out_ref.at[pl.ds((4 * px + 2 * py + pc) * m_per, m_per), :]

        def copy(k, blk, to, src=None):
            return pltpu.make_async_remote_copy(
                src_ref=rows(*blk) if src is None else src, dst_ref=rows(*blk),
                send_sem=send_sems.at[k], recv_sem=recv_sems.at[k], device_id=to, device_id_type=_MESH)

        mine = pltpu.make_async_copy(x_ref, rows(*me), local_sem)
        mine.start()
        first = [copy(0, me, sibling, src=x_ref)]
        first += [copy(1 + j, me, (*chip, c), src=x_ref) for j, chip in enumerate(chips)]
        for cp in first:
            cp.start()
        passed = [copy(4 + j, (*chip, c), sibling) for j, chip in enumerate(chips)]
        for j, chip in enumerate(chips):
            copy(1 + j, (*chip, c), me).wait_recv()
            passed[j].start()
        copy(0, sibling, me).wait_recv()
        for j, chip in enumerate(chips):
            copy(4 + j, (*chip, 1 - c), me).wait_recv()
        for cp in first + passed:
            cp.wait_send()
        mine.wait()

    return pl.pallas_call(
        body, name=name, out_shape=jax.ShapeDtypeStruct((N_DEV * m_per, n), block.dtype),
        in_specs=[pl.BlockSpec(memory_space=pltpu.VMEM)], out_specs=pl.BlockSpec(memory_space=pltpu.VMEM),
        scratch_shapes=[pltpu.SemaphoreType.DMA((7,)), pltpu.SemaphoreType.DMA((7,)), pltpu.SemaphoreType.DMA],
        compiler_params=_cparams(),
    )(block)


def _split_axis(shape):
    return next(a for a, n in enumerate(shape) if n > 1)


_HBM = pl.BlockSpec(memory_space=pltpu.HBM)
_SEM = pl.BlockSpec(memory_space=pltpu.SEMAPHORE)
_SPLIT_COPY = pltpu.CompilerParams(has_side_effects=pltpu.SideEffectType.DATAFLOW_SIDE_EFFECTING)
_TOKEN = jax.ShapeDtypeStruct((8, 128), F32)


def _in_hbm(arrays):
    return [pltpu.with_memory_space_constraint(a, pltpu.HBM) for a in arrays]


class _Gathered(NamedTuple):
    shard_shape: tuple
    chip_axis: int

    @property
    def shape(self):
        return self.shard_shape[:self.chip_axis] + (N_CHIPS,) + self.shard_shape[self.chip_axis:]

    def half(self, ref, chip, pc):
        cut = _split_axis(self.shard_shape)
        n = self.shard_shape[cut] // 2
        idx = [slice(None)] * len(self.shard_shape)
        idx[cut] = pl.ds(pc * n, n)
        idx.insert(self.chip_axis, chip)
        return ref.at[tuple(idx)]


def _own_block_placed(shard, layout, chip):
    return lax.dynamic_update_slice_in_dim(lax.empty(layout.shape, shard.dtype),
                                           jnp.expand_dims(shard, layout.chip_axis), chip, axis=layout.chip_axis)


def _gather_copies(lands, layouts, send_sems, recv_sems):
    x, y, c = _place()
    out = []
    for t, (land, lay) in enumerate(zip(lands, layouts)):
        for j, (px, py) in enumerate(_other_chips(x, y)):
            def copy(chip, t=t, j=j, px=px, py=py, land=land, lay=lay):
                return pltpu.make_async_remote_copy(
                    src_ref=lay.half(land, chip, c), dst_ref=lay.half(land, chip, c),
                    send_sem=send_sems.at[3 * t + j], recv_sem=recv_sems.at[3 * t + j],
                    device_id=(px, py, c), device_id_type=_MESH)
            out.append((copy(2 * x + y), copy(2 * px + py)))
    return out


def _gather_start(lands, layouts, after, name):
    nt = len(lands)
    order = [] if after is None else [after]

    def body(*refs):
        land_refs = refs[:nt]
        send_sems, recv_sems = refs[nt + len(order):nt + len(order) + 2]
        token = refs[-1]
        for send, _ in _gather_copies(land_refs, layouts, send_sems, recv_sems):
            send.start()
        token[...] = jnp.zeros_like(token)

    out = pl.pallas_call(
        body, name=name,
        out_shape=(pltpu.SemaphoreType.DMA((3 * nt,)), pltpu.SemaphoreType.DMA((3 * nt,)),
                   *[pltpu.HBM(a.shape, a.dtype) for a in lands], _TOKEN),
        in_specs=[_HBM] * nt + [_ANY] * len(order),
        out_specs=(_SEM, _SEM, *[_HBM] * nt, pl.BlockSpec(memory_space=pltpu.VMEM)),
        input_output_aliases={t: 2 + t for t in range(nt)}, compiler_params=_SPLIT_COPY,
    )(*_in_hbm(lands), *order)
    return out[0], out[1], list(out[2:2 + nt]), out[-1]


def _gather_wait(send_sems, recv_sems, lands, layouts, after, name):
    nt = len(lands)

    def body(*refs):
        land_refs = refs[:nt]
        sems = refs[nt:nt + 2]
        for send, arrival in _gather_copies(land_refs, layouts, *sems):
            send.wait_send()
            arrival.wait_recv()

    return list(pl.pallas_call(
        body, name=name, out_shape=tuple(pltpu.HBM(a.shape, a.dtype) for a in lands),
        in_specs=[_HBM] * nt + [_SEM, _SEM, _ANY], out_specs=tuple([_HBM] * nt),
        input_output_aliases={t: t for t in range(nt)}, compiler_params=_SPLIT_COPY,
    )(*lands, send_sems, recv_sems, after))


def _gather_forward(lands, layouts, name):
    nt = len(lands)

    def body(*refs):
        outs = refs[nt:2 * nt]
        send_sems, recv_sems = refs[2 * nt:]
        x, y, c = _place()
        sends, arrivals = [], []
        for t, lay in enumerate(layouts):
            for j, (px, py) in enumerate(_other_chips(x, y)):
                for pc, group in ((c, sends), (1 - c, arrivals)):
                    part = lay.half(outs[t], 2 * px + py, pc)
                    group.append(pltpu.make_async_remote_copy(
                        src_ref=part, dst_ref=part, send_sem=send_sems.at[3 * t + j], recv_sem=recv_sems.at[3 * t + j],
                        device_id=(x, y, 1 - c), device_id_type=_MESH))
        for cp in sends:
            cp.start()
        for cp in arrivals:
            cp.wait_recv()
        for cp in sends:
            cp.wait_send()

    return list(pl.pallas_call(
        body, name=name, out_shape=[jax.ShapeDtypeStruct(a.shape, a.dtype) for a in lands],
        in_specs=[_ANY] * nt, out_specs=[_ANY] * nt, input_output_aliases={t: t for t in range(nt)},
        scratch_shapes=[pltpu.SemaphoreType.DMA((3 * nt,)), pltpu.SemaphoreType.DMA((3 * nt,))],
        compiler_params=_cparams(),
    )(*lands))


def _pair_copies(grads, lands, send_sems, recv_sems):
    x, y, c = _place()
    out = []
    for t, (g, land) in enumerate(zip(grads, lands)):
        h = g.shape[1] // 2
        out.append(pltpu.make_async_remote_copy(
            src_ref=g.at[:, pl.ds((1 - c) * h, h), :], dst_ref=land, send_sem=send_sems.at[t],
            recv_sem=recv_sems.at[t], device_id=(x, y, 1 - c), device_id_type=_MESH))
    return out


def _pair_start(grads, after, name):
    nt = len(grads)
    lands = [lax.empty((N_CHIPS, g.shape[1] // 2, g.shape[2]), g.dtype) for g in grads]
    order = [] if after is None else [after]

    def body(*refs):
        send_sems, recv_sems = refs[2 * nt + len(order):2 * nt + len(order) + 2]
        token = refs[-1]
        for cp in _pair_copies(refs[:nt], refs[nt:2 * nt], send_sems, recv_sems):
            cp.start()
        token[...] = jnp.zeros_like(token)

    out = pl.pallas_call(
        body, name=name,
        out_shape=(pltpu.SemaphoreType.DMA((nt,)), pltpu.SemaphoreType.DMA((nt,)),
                   *[pltpu.HBM(a.shape, a.dtype) for a in grads + lands], _TOKEN),
        in_specs=[_HBM] * (2 * nt) + [_ANY] * len(order),
        out_specs=(_SEM, _SEM, *[_HBM] * (2 * nt), pl.BlockSpec(memory_space=pltpu.VMEM)),
        input_output_aliases={t: 2 + t for t in range(2 * nt)}, compiler_params=_SPLIT_COPY,
    )(*_in_hbm(grads + lands), *order)
    return out[0], out[1], list(out[2:2 + nt]), list(out[2 + nt:2 + 2 * nt]), out[-1]


def _pair_wait(send_sems, recv_sems, grads, lands, after, name):
    nt = len(grads)

    def body(*refs):
        for cp in _pair_copies(refs[:nt], refs[nt:2 * nt], *refs[2 * nt:2 * nt + 2]):
            cp.wait_send()
            cp.wait_recv()

    out = pl.pallas_call(
        body, name=name, out_shape=tuple(pltpu.HBM(a.shape, a.dtype) for a in grads + lands),
        in_specs=[_HBM] * (2 * nt) + [_SEM, _SEM, _ANY], out_specs=tuple([_HBM] * (2 * nt)),
        input_output_aliases={t: t for t in range(2 * nt)}, compiler_params=_SPLIT_COPY,
    )(*grads, *lands, send_sems, recv_sems, after)
    return list(out[:nt]), list(out[nt:])


def _pair_sum(own, recv, c_idx, name):
    _, h, cols = recv.shape

    def body(c_ref, own_ref, recv_ref, o_ref):
        o_ref[...] = (own_ref[...] + recv_ref[...]).astype(BF16)

    return pl.pallas_call(
        body, name=name,
        grid_spec=pltpu.PrefetchScalarGridSpec(
            num_scalar_prefetch=1, grid=(N_CHIPS,),
            in_specs=[pl.BlockSpec((None, h, cols), lambda k, c_ref: (k, c_ref[0], 0)),
                      pl.BlockSpec((None, h, cols), lambda k, c_ref: (k, 0, 0))],
            out_specs=pl.BlockSpec((None, h, cols), lambda k, c_ref: (k, 0, 0))),
        out_shape=jax.ShapeDtypeStruct(recv.shape, BF16), compiler_params=_cparams(("arbitrary",)),
    )(c_idx, own, recv)


def _chip_copies(parts, lands, send_sems, recv_sems):
    x, y, c = _place()
    out = []
    for t, (part, land) in enumerate(zip(parts, lands)):
        for j, (px, py) in enumerate(_other_chips(x, y)):
            out.append(pltpu.make_async_remote_copy(
                src_ref=part.at[2 * px + py], dst_ref=land.at[j], send_sem=send_sems.at[3 * t + j],
                recv_sem=recv_sems.at[3 * t + j], device_id=(px, py, c), device_id_type=_MESH))
    return out


def _chip_send_start(parts, after, name):
    nt = len(parts)
    lands = [lax.empty((N_CHIPS - 1,) + p.shape[1:], p.dtype) for p in parts]
    order = [] if after is None else [after]

    def body(*refs):
        send_sems, recv_sems = refs[2 * nt + len(order):2 * nt + len(order) + 2]
        token = refs[-1]
        for cp in _chip_copies(refs[:nt], refs[nt:2 * nt], send_sems, recv_sems):
            cp.start()
        token[...] = jnp.zeros_like(token)

    out = pl.pallas_call(
        body, name=name,
        out_shape=(pltpu.SemaphoreType.DMA((3 * nt,)), pltpu.SemaphoreType.DMA((3 * nt,)),
                   *[pltpu.HBM(a.shape, a.dtype) for a in parts + lands], _TOKEN),
        in_specs=[_HBM] * (2 * nt) + [_ANY] * len(order),
        out_specs=(_SEM, _SEM, *[_HBM] * (2 * nt), pl.BlockSpec(memory_space=pltpu.VMEM)),
        input_output_aliases={t: 2 + t for t in range(2 * nt)}, compiler_params=_SPLIT_COPY,
    )(*_in_hbm(parts + lands), *order)
    return out[0], out[1], list(out[2:2 + nt]), list(out[2 + nt:2 + 2 * nt]), out[-1]


def _chip_send_wait(send_sems, recv_sems, parts, lands, after, name):
    nt = len(parts)

    def body(*refs):
        for cp in _chip_copies(refs[:nt], refs[nt:2 * nt], *refs[2 * nt:2 * nt + 2]):
            cp.wait_send()
            cp.wait_recv()

    out = pl.pallas_call(
        body, name=name, out_shape=tuple(pltpu.HBM(a.shape, a.dtype) for a in parts + lands),
        in_specs=[_HBM] * (2 * nt) + [_SEM, _SEM, _ANY], out_specs=tuple([_HBM] * (2 * nt)),
        input_output_aliases={t: t for t in range(2 * nt)}, compiler_params=_SPLIT_COPY,
    )(*parts, *lands, send_sems, recv_sems, after)
    return list(out[:nt]), list(out[nt:])


def _chip_sum(part, arrived, into, lead, place_idx, name):
    _, h, cols = part.shape

    def body(idx_ref, own_ref, arr_ref, into_ref, o_ref):
        acc = own_ref[...].astype(F32)
        for k in range(N_CHIPS - 1):
            acc = acc + arr_ref[k].astype(F32)
        o_ref[...] = acc

    return pl.pallas_call(
        body, name=name,
        grid_spec=pltpu.PrefetchScalarGridSpec(
            num_scalar_prefetch=1, grid=(1,),
            in_specs=[pl.BlockSpec((None, h, cols), lambda g, idx: (idx[1], 0, 0)),
                      pl.BlockSpec((N_CHIPS - 1, h, cols), lambda g, idx: (0, 0, 0)), _ANY],
            out_specs=pl.BlockSpec((None,) * len(lead) + (h, cols), lambda g, idx: (*lead, idx[0], 0))),
        out_shape=jax.ShapeDtypeStruct(into.shape, F32), input_output_aliases={3: 0},
        compiler_params=_cparams(("arbitrary",)),
    )(place_idx, part, arrived, into)


def _pair_gather(bufs, homes, name):
    nt, nb = len(homes), len(bufs)

    def body(*refs):
        outs = refs[nb:2 * nb]
        send_sems, recv_sems = refs[2 * nb:]
        x, y, c = _place()

        def home(t, pc):
            o, lead, rows = homes[t]
            return outs[o].at[(*lead, pl.ds(pc * (rows // 2), rows // 2), slice(None))]

        def copy(t, pc):
            return pltpu.make_async_remote_copy(src_ref=home(t, pc), dst_ref=home(t, pc), send_sem=send_sems.at[t],
                                                recv_sem=recv_sems.at[t], device_id=(x, y, 1 - c), device_id_type=_MESH)

        sends = [copy(t, c) for t in range(nt)]
        for cp in sends:
            cp.start()
        for t in range(nt):
            copy(t, 1 - c).wait_recv()
        for cp in sends:
            cp.wait_send()

    return pl.pallas_call(
        body, name=name, out_shape=[jax.ShapeDtypeStruct(b.shape, b.dtype) for b in bufs],
        in_specs=[_ANY] * nb, out_specs=[_ANY] * nb, input_output_aliases={o: o for o in range(nb)},
        scratch_shapes=[pltpu.SemaphoreType.DMA((nt,)), pltpu.SemaphoreType.DMA((nt,))],
        compiler_params=_cparams(),
    )(*bufs)


def _sum_devices(g, name):
    def body(g_ref, o_ref):
        acc = g_ref[0:1, :]
        for d in range(1, N_DEV):
            acc = acc + g_ref[d:d + 1, :]
        o_ref[...] = acc
    return pl.pallas_call(body, name=name, out_shape=jax.ShapeDtypeStruct((1, g.shape[1]), F32),
                          compiler_params=_cparams())(g)


_WEIGHTS = ("w_cond", "b_cond", "norm_pre", "norm_post", "w_ffn_in", "w_ffn_out", "fox_w_in", "fox_b_f",
            "fox_w_out", "sconv_w_in", "sconv_conv_w", "sconv_w_out", "lru_w_in", "lru_conv_w", "lru_conv_b",
            "lru_w_a", "lru_b_a", "lru_w_x", "lru_b_x", "lru_lambda", "lru_w_out")
_BIG = (("w_ffn_in", False), ("w_ffn_out", True), ("fox_w_in", False), ("fox_w_out", True),
        ("sconv_w_in", False), ("sconv_w_out", True), ("lru_w_in", False), ("lru_w_out", True))
_SMALL = tuple(n for n in _WEIGHTS if n != "w_cond" and n not in dict(_BIG))
_COL_SHARDED_SMALL = ("norm_pre", "norm_post", "sconv_conv_w", "lru_conv_w", "lru_conv_b", "lru_lambda")


def _pack_rows(parts, rows=8):
    flat = jnp.concatenate([p.reshape(-1) for p in parts])
    width = -(-flat.size // (rows * 128)) * 128
    return jnp.pad(flat, (0, rows * width - flat.size)).reshape(rows, width)


def _unpack(flat, shapes):
    out, off = [], 0
    for shp in shapes:
        n = math.prod(shp)
        out.append(flat[off:off + n].reshape(shp))
        off += n
    return out


def _join_chips(g):
    g = jnp.moveaxis(g, 0, -2)
    return g.reshape(g.shape[:-2] + (g.shape[-2] * g.shape[-1],))


def _my_columns(full, chip):
    n = full.shape[-1] // N_CHIPS
    return lax.dynamic_slice_in_dim(full, chip * n, n, axis=full.ndim - 1)


def _block_diag(w):
    eye = jnp.eye(LRU_BLOCKS, dtype=w.dtype)
    return jnp.einsum("nij,nm->nimj", w, eye).reshape(D_MODEL, D_MODEL)


def _step(x, c, target, wts, mom, var):
    ix, iy, ic = _place()
    chip = 2 * ix + iy
    dev = 2 * chip + ic
    n_cond = wts["w_cond"].shape[2]

    mixer_names = [("fox_w_in", "fox_w_out"), ("sconv_w_in", "sconv_w_out"), ("lru_w_in", "lru_w_out")]

    def shards_of(i, sub):
        if sub == 1:
            return [wts[n][i // 3] for n in mixer_names[i % 3]]
        return [wts["w_ffn_in"][i, sub // 2], wts["w_ffn_out"][i, sub // 2]]

    chunks = [[(0, 0)], [(0, 1), (0, 2)]] + [[(i, sub) for sub in range(N_SUB)] for i in range(1, DEPTH)]
    in_flight, chunk_of, token = [], {}, None
    for k, members in enumerate(chunks):
        shards = [s for i, sub in members for s in shards_of(i, sub)]
        layouts = [_Gathered(s.shape, 0) for s in shards]
        lands = [_own_block_placed(s.astype(BF16), lay, chip) for s, lay in zip(shards, layouts)]
        send_sems, recv_sems, lands, token = _gather_start(lands, layouts, token, f"gather_start_{k}")
        in_flight.append([send_sems, recv_sems, lands, layouts, False])
        chunk_of.update({m: (k, 2 * pos) for pos, m in enumerate(members)})

    small_shapes = [(D_MODEL,)] + [wts[n].shape for n in _COL_SHARDED_SMALL]
    g1 = _allgather8(_pack_rows([c[0] + token[0, 0]] + [wts[n] for n in _COL_SHARDED_SMALL]), "gather_small")
    g1 = g1.reshape(N_DEV, -1)
    c_all = g1[:, :D_MODEL]
    per_chip = [jnp.stack(col) for col in zip(*[_unpack(g1[2 * k], small_shapes) for k in range(N_CHIPS)])]
    small_full = {n: _join_chips(v) for n, v in zip(_COL_SHARDED_SMALL, per_chip[1:])}

    c_pad = jnp.pad(c_all, ((0, COND_ROWS - N_DEV), (0, 0)))
    b_shard = _my_columns(wts["b_cond"], chip)[:, None, :]
    mod_part = _cond_fwd(c_pad, wts["w_cond"], b_shard, "cond_fwd")
    g2 = _allgather8(mod_part[:, :N_DEV].transpose(1, 0, 2).reshape(N_DEV, DEPTH * n_cond), "gather_mod")
    g2 = g2.reshape(N_DEV, N_DEV, DEPTH, n_cond)[0::2]
    mod = _join_chips(lax.dynamic_index_in_dim(g2, dev, axis=1, keepdims=False)).reshape(DEPTH, N_SUB, 3, D_MODEL)

    lru_ax = jnp.concatenate([_block_diag(wts["lru_w_a"][0]), _block_diag(wts["lru_w_x"][0])], axis=1).astype(BF16)

    def layer_params(i, sub, x_in):
        k, pos = chunk_of[(i, sub)]
        send_sems, recv_sems, lands, layouts, arrived = in_flight[k]
        if not arrived:
            lands = _gather_wait(send_sems, recv_sems, lands, layouts, x_in, f"gather_wait_{k}")
            in_flight[k][2:] = [_gather_forward(lands, layouts, f"gather_forward_{k}"), layouts, True]
        w_in, w_out = in_flight[k][2][pos:pos + 2]
        w_out = w_out.reshape(-1, w_out.shape[-1])
        if sub != 1:
            out = {"ffn_in": [_W(w_in, (), True)], "ffn_out": [_W(w_out)]}
            if sub == 0:
                out.update(norm_pre=small_full["norm_pre"][i], norm_post=small_full["norm_post"][i])
            return out
        j = i // 3
        if i % 3 == 0:
            w_in = jnp.pad(_join_chips(w_in), ((0, 0), (0, FOX_PAD - 3 * D_MODEL - FOX_HEADS)))
            return {"mixer": {"w_in": _W(w_in), "w_out": _W(w_out), "b_f": wts["fox_b_f"][j][:, None]}}
        if i % 3 == 1:
            return {"mixer": {"w_in": _W(w_in, (), True), "w_out": _W(w_out), "conv_w": small_full["sconv_conv_w"][j]}}
        return {"mixer": {"w_in": _W(w_in, (), True), "w_out": _W(w_out), "conv_w": small_full["lru_conv_w"][j],
                          "conv_b": small_full["lru_conv_b"], "w_ax": _W(lru_ax),
                          "b_a": wts["lru_b_a"].reshape(1, D_MODEL), "b_x": wts["lru_b_x"].reshape(1, D_MODEL),
                          "lam": small_full["lru_lambda"]}}

    place_idx = jnp.stack([ic, chip]).astype(jnp.int32)
    c_idx = place_idx[:1]
    big_index = {n: o for o, (n, _) in enumerate(_BIG)}
    exchanges, pending = [], []

    def to_chips(after):
        i, send_sems, recv_sems, tensors, lands, homes = pending.pop()
        tensors, recv = _pair_wait(send_sems, recv_sems, tensors, lands, after, f"grads_pair_wait_l{i}")
        parts = [_pair_sum(t, r, c_idx, f"grads_pair_sum_l{i}_{k}") for k, (t, r) in enumerate(zip(tensors, recv))]
        send_sems, recv_sems, parts, lands, tok = _chip_send_start(parts, None, f"grads_chip_start_l{i}")
        exchanges.append((i, send_sems, recv_sems, parts, lands, homes))
        return tok

    def chip_blocks(g, by_rows, width):
        if by_rows:
            return g.reshape(N_CHIPS, g.shape[0] // N_CHIPS, g.shape[1])
        if g.ndim == 3:
            return g
        return g[:, :width * N_CHIPS].reshape(g.shape[0], N_CHIPS, width).transpose(1, 0, 2)

    def on_grads(i, g, dx):
        tok = to_chips(dx) if pending else None
        n_in, n_out = mixer_names[i % 3]
        items = [("w_ffn_in", (i, k), g["ffn_in"][k]) for k in range(2)]
        items += [("w_ffn_out", (i, k), g["ffn_out"][k]) for k in range(2)]
        items += [(n_in, (i // 3,), g["mixer"]["w_in"]), (n_out, (i // 3,), g["mixer"]["w_out"])]
        tensors = [chip_blocks(t, dict(_BIG)[n], wts[n].shape[-1]) for n, _, t in items]
        homes = [(big_index[n], lead, wts[n].shape[-2]) for n, lead, _ in items]
        send_sems, recv_sems, tensors, lands, tok = _pair_start(tensors, tok, f"grads_pair_start_l{i}")
        pending.append((i, send_sems, recv_sems, tensors, lands, homes))
        return tok

    loss_row, grad_x, dmod, lg = _local_step(x[0], target[0], mod, layer_params, on_grads)
    loss = lax.psum(loss_row[0, 0], ("x", "y", "c"))
    last_start = to_chips(grad_x)

    fox_layers = [i for i in range(DEPTH) if i % 3 == 0]
    sconv_g, lru_g = lg[1]["mixer"], lg[2]["mixer"]
    small_g = {
        "dmod": dmod + last_start[0, 0], "norm_pre": jnp.stack([g["norm_pre"] for g in lg]), "norm_post": jnp.stack([g["norm_post"] for g in lg]),
        "fox_b_f": jnp.stack([lg[i]["mixer"]["b_f"][:, 0] for i in fox_layers]),
        "sconv_conv_w": sconv_g["conv_w"][None], "lru_conv_w": lru_g["conv_w"][None], "lru_conv_b": lru_g["conv_b"],
        "lru_w_a": lru_g["w_a"][None], "lru_b_a": lru_g["b_a"].reshape(1, LRU_BLOCKS, LRU_BLOCK_DIM),
        "lru_w_x": lru_g["w_x"][None], "lru_b_x": lru_g["b_x"].reshape(1, LRU_BLOCKS, LRU_BLOCK_DIM),
        "lru_lambda": lru_g["lam"]}
    g4 = _allgather8(_pack_rows(list(small_g.values())), "gather_small_grads").reshape(N_DEV, -1)
    summed = dict(zip(small_g, _unpack(_sum_devices(g4, "sum_small_grads")[0], [v.shape for v in small_g.values()])))
    grads = {n: (_my_columns(summed[n], chip) if n in _COL_SHARDED_SMALL else summed[n]) for n in _SMALL if n != "b_cond"}
    grads["b_cond"] = summed["dmod"].reshape(DEPTH, N_SUB * 3 * D_MODEL)

    dmod_all = g4[:, :dmod.size].reshape(N_DEV, DEPTH, N_SUB * 3 * D_MODEL)
    dmod_s = jnp.pad(_my_columns(dmod_all, chip).transpose(1, 0, 2), ((0, 0), (0, COND_PAD - N_DEV), (0, 0))).astype(BF16)
    c_t = jnp.pad(c_all.T, ((0, 0), (0, COND_PAD - N_DEV)))
    grads["w_cond"], d_cond, m_cond, v_cond = _cond_bwd_adamw(c_t, dmod_s, wts["w_cond"], mom["w_cond"],
                                                              var["w_cond"], "cond_bwd_adamw")

    bufs = [lax.empty(wts[n].shape, F32) for n, _ in _BIG]
    all_homes = []
    for i, send_sems, recv_sems, parts, lands, homes in exchanges:
        follows = d_cond if not all_homes else bufs[0]
        parts, lands = _chip_send_wait(send_sems, recv_sems, parts, lands, follows, f"grads_chip_wait_l{i}")
        for k, (part, land, (o, lead, _)) in enumerate(zip(parts, lands, homes)):
            bufs[o] = _chip_sum(part, land, bufs[o], lead, place_idx, f"grads_chip_sum_l{i}_{k}")
        all_homes += homes
    grads.update(zip([n for n, _ in _BIG], _pair_gather(bufs, all_homes, "grads_pair_gather")))

    delta, new_m, new_v = {"w_cond": d_cond}, {"w_cond": m_cond}, {"w_cond": v_cond}
    for n, _ in _BIG:
        two_d = lambda a: a.reshape(-1, a.shape[-1])
        d, nm, nv = _adamw(two_d(wts[n]), two_d(grads[n]), two_d(mom[n]), two_d(var[n]), "adamw_" + n)
        delta[n], new_m[n], new_v[n] = (a.reshape(wts[n].shape) for a in (d, nm, nv))
    shapes = [wts[n].shape for n in _SMALL]
    packed = [_pack_rows([src[n] for n in _SMALL]) for src in (wts, grads, mom, var)]
    for dst, out in zip((delta, new_m, new_v), _adamw(*packed, "adamw_small")):
        dst.update(zip(_SMALL, _unpack(out.reshape(-1), shapes)))

    return (loss, grad_x[None], *[grads[n] for n in _WEIGHTS], *[delta[n] for n in _WEIGHTS],
            *[new_m[n] for n in _WEIGHTS], *[new_v[n] for n in _WEIGHTS])


def kernel(x, c, w_cond, b_cond, norm_pre, norm_post, w_ffn_in, w_ffn_out, fox_w_in, fox_b_f, fox_w_out, sconv_w_in, sconv_conv_w, sconv_w_out, lru_w_in, lru_conv_w, lru_conv_b, lru_w_a, lru_b_a, lru_w_x, lru_b_x, lru_lambda, lru_w_out, loss_target, m_w_cond, m_b_cond, m_norm_pre, m_norm_post, m_w_ffn_in, m_w_ffn_out, m_fox_w_in, m_fox_b_f, m_fox_w_out, m_sconv_w_in, m_sconv_conv_w, m_sconv_w_out, m_lru_w_in, m_lru_conv_w, m_lru_conv_b, m_lru_w_a, m_lru_b_a, m_lru_w_x, m_lru_b_x, m_lru_lambda, m_lru_w_out, v_w_cond, v_b_cond, v_norm_pre, v_norm_post, v_w_ffn_in, v_w_ffn_out, v_fox_w_in, v_fox_b_f, v_fox_w_out, v_sconv_w_in, v_sconv_conv_w, v_sconv_w_out, v_lru_w_in, v_lru_conv_w, v_lru_conv_b, v_lru_w_a, v_lru_b_a, v_lru_w_x, v_lru_b_x, v_lru_lambda, v_lru_w_out):
    given = dict(locals())
    wts = {n: given[n] for n in _WEIGHTS}
    mom = {n: given["m_" + n] for n in _WEIGHTS}
    var = {n: given["v_" + n] for n in _WEIGHTS}
    return _step(x, c, loss_target, wts, mom, var)
```

```python
import functools
import math
from typing import NamedTuple

import jax
import jax.numpy as jnp
from jax import lax
from jax.experimental import pallas as pl
from jax.experimental.pallas import tpu as pltpu

F32 = jnp.float32
BF16 = jnp.bfloat16

D_MODEL = 1024
DEPTH = 4
N_SUB = 3
D_FF = 2816
RMS_EPS = 1e-6
FOX_HEADS = 16
FOX_HEAD_DIM = 64
FOX_PAD = 3200
LRU_BLOCKS = 16
LRU_BLOCK_DIM = 64
LRU_C = 8.0
N_CHIPS = 4
N_DEV = 8

ADAM_LR = 0.001
ADAM_B1 = 0.9
ADAM_B2 = 0.999
ADAM_EPS = 1e-08
ADAM_WD = 0.01
ADAM_STEP = 10

VMEM_LIMIT_V7X = 56 * 1024 * 1024
ROW_TILE = 256
COL_TILE = 256
ATT_TILE = 256
MM_ROWS = 256


def _cparams(sem=None):
    return pltpu.CompilerParams(vmem_limit_bytes=VMEM_LIMIT_V7X, dimension_semantics=sem)


def _sigmoid(z):
    return 1.0 / (1.0 + jnp.exp(-z))


def _softplus(z):
    return jnp.maximum(z, 0.0) + jnp.log(1.0 + jnp.exp(-jnp.abs(z)))


def _rows_sum(v):
    return jnp.sum(v, axis=0, keepdims=True)


class _W(NamedTuple):
    arr: jax.Array
    prefix: tuple = ()
    blocked: bool = False


def _w_spec(w, block2, pos):
    lead = (None,) * (len(w.prefix) + (1 if w.blocked else 0))
    if w.blocked:
        return pl.BlockSpec(lead + block2, lambda *g: (pos(*g)[0], *w.prefix, pos(*g)[1], pos(*g)[2]))
    return pl.BlockSpec(lead + block2, lambda *g: (*w.prefix, pos(*g)[1], pos(*g)[2]))


def _mm_nn(a, b, name, tn=None):
    m, k = a.shape
    if b.blocked:
        steps, bn = b.arr.shape[0], b.arr.shape[-1]
        b_spec = _w_spec(b, (k, bn), lambda n: (n, 0, 0))
    else:
        n_total = b.arr.shape[-1]
        bn = n_total if tn is None else tn
        steps = n_total // bn
        assert steps * bn == n_total
        b_spec = _w_spec(b, (k, bn), lambda n: (0, 0, n))
    tm = min(MM_ROWS, m)

    def body(a_ref, b_ref, o_ref):
        def step(i, carry):
            r = pl.ds(pl.multiple_of(i * tm, tm), tm)
            o_ref[r, :] = jnp.dot(a_ref[r, :], b_ref[...], preferred_element_type=F32)
            return carry
        lax.fori_loop(0, m // tm, step, 0)

    return pl.pallas_call(
        body, name=name, grid=(steps,),
        in_specs=[pl.BlockSpec((m, k), lambda n: (0, 0)), b_spec],
        out_specs=pl.BlockSpec((m, bn), lambda n: (0, n)),
        out_shape=jax.ShapeDtypeStruct((m, steps * bn), F32),
        compiler_params=_cparams(("arbitrary",)),
    )(a, b.arr)


def _mm_nt(dy, w, name, tk=None, tn=None):
    m, n_total = dy.shape
    k = w.arr.shape[-2]
    if w.blocked:
        bk, bn = k, w.arr.shape[-1]
        grid = (1, w.arr.shape[0])
        w_spec = _w_spec(w, (k, bn), lambda kt, n: (n, 0, 0))
    else:
        bk = k if tk is None else tk
        bn = n_total if tn is None else tn
        grid = (k // bk, n_total // bn)
        assert grid[0] * bk == k and grid[1] * bn == n_total
        w_spec = _w_spec(w, (bk, bn), lambda kt, n: (0, kt, n))
    tm = min(MM_ROWS, m)

    def body(dy_ref, w_ref, o_ref):
        def step(i, carry):
            r = pl.ds(pl.multiple_of(i * tm, tm), tm)
            o_ref[r, :] += lax.dot_general(dy_ref[r, :], w_ref[...], (((1,), (1,)), ((), ())),
                                           preferred_element_type=F32)
            return carry

        @pl.when(pl.program_id(1) == 0)
        def _():
            o_ref[...] = jnp.zeros_like(o_ref)
        lax.fori_loop(0, m // tm, step, 0)

    return pl.pallas_call(
        body, name=name, grid=grid,
        in_specs=[pl.BlockSpec((m, bn), lambda kt, n: (0, n)), w_spec],
        out_specs=pl.BlockSpec((m, bk), lambda kt, n: (0, kt)),
        out_shape=jax.ShapeDtypeStruct((m, k), F32),
        compiler_params=_cparams(("arbitrary", "arbitrary")),
    )(dy, w.arr)


def _mm_tn(x, dy, name, tk=None, tn=None, blocked_out=False):
    s, k = x.shape
    n_total = dy.shape[1]
    bk = k if tk is None else tk
    bn = n_total if tn is None else tn
    grid = (k // bk, n_total // bn)
    assert grid[0] * bk == k and grid[1] * bn == n_total
    ck = 256 if bk % 256 == 0 else 128

    def body(x_ref, dy_ref, o_ref):
        def step(i, carry):
            c = pl.ds(pl.multiple_of(i * ck, ck), ck)
            o_ref[c, :] = lax.dot_general(x_ref[:, c], dy_ref[...], (((0,), (0,)), ((), ())),
                                          preferred_element_type=F32)
            return carry
        lax.fori_loop(0, bk // ck, step, 0)

    if blocked_out:
        assert grid[0] == 1
        out_spec = pl.BlockSpec((None, bk, bn), lambda kt, n: (n, 0, 0))
        out_shape = jax.ShapeDtypeStruct((grid[1], k, bn), F32)
    else:
        out_spec = pl.BlockSpec((bk, bn), lambda kt, n: (kt, n))
        out_shape = jax.ShapeDtypeStruct((k, n_total), F32)
    return pl.pallas_call(
        body, name=name, grid=grid,
        in_specs=[pl.BlockSpec((s, bk), lambda kt, n: (0, kt)), pl.BlockSpec((s, bn), lambda kt, n: (0, n))],
        out_specs=out_spec, out_shape=out_shape,
        compiler_params=_cparams(("arbitrary", "arbitrary")),
    )(x, dy)


def _row_call(name, body, rows, fulls, row_outs, acc_outs, tr=ROW_TILE, after=None):
    s = rows[0].shape[0]
    tr = min(tr, s)
    in_specs = [pl.BlockSpec((tr, a.shape[1]), lambda i: (i, 0)) for a in rows]
    in_specs += [pl.BlockSpec(a.shape, lambda i: (0, 0)) for a in fulls]
    n_in = len(in_specs)
    order = [] if after is None else [after]
    in_specs += [pl.BlockSpec(memory_space=pl.ANY)] * len(order)
    out_specs = [pl.BlockSpec((tr, c), lambda i: (i, 0)) for c, _ in row_outs]
    out_specs += [pl.BlockSpec((1, c), lambda i: (0, 0)) for c, _ in acc_outs]
    out_shape = [jax.ShapeDtypeStruct((s, c), dt) for c, dt in row_outs]
    out_shape += [jax.ShapeDtypeStruct((1, c), dt) for c, dt in acc_outs]
    n_acc = len(acc_outs)

    def wrapped(*refs):
        refs = refs[:n_in] + refs[n_in + len(order):]
        if n_acc:
            @pl.when(pl.program_id(0) == 0)
            def _():
                for r in refs[len(refs) - n_acc:]:
                    r[...] = jnp.zeros_like(r)
        body(*refs)

    return pl.pallas_call(
        wrapped, name=name, grid=(s // tr,), in_specs=in_specs, out_specs=out_specs, out_shape=out_shape,
        compiler_params=_cparams(("arbitrary",)),
    )(*rows, *fulls, *order)


def _rms(v):
    return lax.rsqrt(jnp.mean(v * v, axis=-1, keepdims=True) + RMS_EPS)


def _pre_norm(x, g_pre, scale, shift, name, after=None):
    def body(x_ref, g_ref, sc_ref, sh_ref, h_ref):
        xv = x_ref[...]
        h = (xv * _rms(xv)) * g_ref[...] * (1.0 + sc_ref[...]) + sh_ref[...]
        h_ref[...] = h.astype(BF16)
    return _row_call(name, body, [x], [g_pre, scale, shift], [(D_MODEL, BF16)], [], after=after)[0]


def _post_norm(x, y, g_post, gate, coef, name):
    def body(x_ref, y_ref, g_ref, gate_ref, o_ref):
        yv = y_ref[...]
        o_ref[...] = x_ref[...] + (coef * gate_ref[...]) * ((yv * _rms(yv)) * g_ref[...])
    return _row_call(name, body, [x, y], [g_post, gate], [(D_MODEL, F32)], [])[0]


def _post_norm_bwd(dxo, y, g_post, gate, coef, name, after=None):
    def body(dxo_ref, y_ref, g_ref, gate_ref, dy_ref, dgate_ref, dg_ref):
        yv = y_ref[...]
        r2 = _rms(yv)
        yn = yv * r2
        dxo_v = dxo_ref[...]
        dgate_ref[...] += _rows_sum(dxo_v * (yn * g_ref[...])) * coef
        dz = dxo_v * (coef * gate_ref[...])
        dg_ref[...] += _rows_sum(dz * yn)
        dyn = dz * g_ref[...]
        dy = r2 * (dyn - yn * jnp.mean(dyn * yn, axis=-1, keepdims=True))
        dy_ref[...] = dy.astype(BF16)
    return _row_call(name, body, [dxo, y], [g_post, gate], [(D_MODEL, BF16)], [(D_MODEL, F32), (D_MODEL, F32)],
                     after=after)


def _pre_norm_bwd(dxo, dh, x, g_pre, scale, name):
    def body(dxo_ref, dh_ref, x_ref, g_ref, sc_ref, dx_ref, dshift_ref, dscale_ref, dg_ref):
        xv = x_ref[...]
        r = _rms(xv)
        xn = xv * r
        dh_v = dh_ref[...]
        one_sc = 1.0 + sc_ref[...]
        dshift_ref[...] += _rows_sum(dh_v)
        dscale_ref[...] += _rows_sum(dh_v * (xn * g_ref[...]))
        dg_ref[...] += _rows_sum(dh_v * xn * one_sc)
        dxn = dh_v * (g_ref[...] * one_sc)
        dx_ref[...] = dxo_ref[...] + r * (dxn - xn * jnp.mean(dxn * xn, axis=-1, keepdims=True))
    return _row_call(name, body, [dxo, dh, x], [g_pre, scale], [(D_MODEL, F32)],
                     [(D_MODEL, F32), (D_MODEL, F32), (D_MODEL, F32)])


def _swiglu_act(gu, name):
    def body(gu_ref, a_ref):
        g = gu_ref[:, :D_FF]
        a_ref[...] = (g * _sigmoid(g) * gu_ref[:, D_FF:]).astype(BF16)
    return _row_call(name, body, [gu], [], [(D_FF, BF16)], [])[0]


def _swiglu_act_bwd(da, gu, name):
    def body(da_ref, gu_ref, dgu_ref):
        g = gu_ref[:, :D_FF]
        sg = _sigmoid(g)
        da_v = da_ref[...]
        dgu_ref[:, :D_FF] = (da_v * gu_ref[:, D_FF:] * (sg * (1.0 + g * (1.0 - sg)))).astype(BF16)
        dgu_ref[:, D_FF:] = (da_v * (g * sg)).astype(BF16)
    return _row_call(name, body, [da, gu], [], [(2 * D_FF, BF16)], [])[0]


def _loss_head(y, target, name):
    def body(y_ref, t_ref, dy_ref, loss_ref):
        e = y_ref[...] - t_ref[...]
        dy_ref[...] = e * (1.0 / D_MODEL)
        part = jnp.sum(jnp.mean(e * e, axis=-1, keepdims=True), axis=0, keepdims=True) * 0.5
        loss_ref[...] += jnp.broadcast_to(part, loss_ref.shape)
    return _row_call(name, body, [y, target], [], [(D_MODEL, F32)], [(128, F32)])


def _lane_scan(v, reverse):
    s = v.shape[1]
    lane = lax.broadcasted_iota(jnp.int32, v.shape, 1)
    d = 1
    while d < s:
        if reverse:
            v = v + jnp.where(lane < s - d, pltpu.roll(v, s - d, 1), 0.0)
        else:
            v = v + jnp.where(lane >= d, pltpu.roll(v, d, 1), 0.0)
        d *= 2
    return v


def _fox_gate(flt, b_f, name):
    def body(f_ref, b_ref, cum_ref):
        z = f_ref[...] + b_ref[...]
        cum_ref[...] = _lane_scan(-_softplus(-z), reverse=False)
    return pl.pallas_call(body, name=name, out_shape=jax.ShapeDtypeStruct(flt.shape, F32),
                          compiler_params=_cparams())(flt, b_f)


def _fox_gate_bwd(dcum_q, dcum_k, flt, b_f, name):
    def body(dq_ref, dk_ref, f_ref, b_ref, df_ref, db_ref):
        z = f_ref[...] + b_ref[...]
        df = _lane_scan(dq_ref[...] + dk_ref[...], reverse=True) * _sigmoid(-z)
        df_ref[...] = df
        db_ref[...] = jnp.sum(df, axis=1, keepdims=True)
    h = flt.shape[0]
    return pl.pallas_call(body, name=name,
                          out_shape=(jax.ShapeDtypeStruct(flt.shape, F32), jax.ShapeDtypeStruct((h, 1), F32)),
                          compiler_params=_cparams())(dcum_q, dcum_k, flt, b_f)


def _pick_head(block, h):
    lane = lax.broadcasted_iota(jnp.int32, block.shape, 1)
    return jnp.sum(jnp.where(lane == h, block, 0.0), axis=1, keepdims=True)


def _put_head(ref, col, h):
    @pl.when(h == 0)
    def _():
        ref[...] = jnp.zeros_like(ref)
    lane = lax.broadcasted_iota(jnp.int32, ref.shape, 1)
    ref[...] = jnp.where(lane == h, col, ref[...])


_NT = (((1,), (1,)), ((), ()))
_FOX_SCALE = FOX_HEAD_DIM ** -0.5


def _causal(s_tile, t):
    row = lax.broadcasted_iota(jnp.int32, (t, t), 0)
    col = lax.broadcasted_iota(jnp.int32, (t, t), 1)
    return jnp.where(col <= row, s_tile, -jnp.inf)


HEAD_PAIRS = FOX_HEADS // 2
PAIR_W = 2 * FOX_HEAD_DIM


def _low_half(shape):
    return lax.broadcasted_iota(jnp.int32, shape, 1) < FOX_HEAD_DIM


def _fox_attn_fwd(qkv, cum, cum_t, name):
    s = qkv.shape[0]
    t = min(ATT_TILE, s)

    def body(q_ref, k_ref, v_ref, cum_ref, cumt_ref, o_ref, ob_ref, lse_ref):
        i = pl.program_id(0)
        hp = pl.program_id(1)
        lo = _low_half((t, PAIR_W))
        qv = q_ref[...]
        zero = jnp.zeros_like(qv)
        q2 = (jnp.where(lo, qv, zero), jnp.where(lo, zero, qv))
        cum_v = cum_ref[...]
        cq2 = (_pick_head(cum_v, 2 * hp), _pick_head(cum_v, 2 * hp + 1))

        def step(j, carry, masked):
            ks = pl.ds(pl.multiple_of(j * t, t), t)
            kj = k_ref[ks, :]
            vj = v_ref[ks, :]
            out = []
            for e in range(2):
                m, l, acc = carry[e]
                sc = lax.dot_general(q2[e], kj, _NT, preferred_element_type=F32) * _FOX_SCALE
                sc = sc + cq2[e] - cumt_ref[e:e + 1, ks]
                if masked:
                    sc = _causal(sc, t)
                m_new = jnp.maximum(m, jnp.max(sc, axis=1, keepdims=True))
                alpha = jnp.exp(m - m_new)
                p = jnp.exp(sc - m_new)
                l = alpha * l + jnp.sum(p, axis=1, keepdims=True)
                acc = alpha * acc + jnp.dot(p.astype(BF16), vj, preferred_element_type=F32)
                out.append((m_new, l, acc))
            return tuple(out)

        one = (jnp.full((t, 1), -jnp.inf, F32), jnp.zeros((t, 1), F32), jnp.zeros((t, PAIR_W), F32))
        carry = lax.fori_loop(0, i, lambda j, c: step(j, c, False), (one, one))
        (m0, l0, a0), (m1, l1, a1) = step(i, carry, True)
        o = jnp.where(lo, a0 / l0, a1 / l1)
        o_ref[...] = o
        ob_ref[...] = o.astype(BF16)
        _put_head(lse_ref, m0 + jnp.log(l0), 2 * hp)
        _put_head(lse_ref, m1 + jnp.log(l1), 2 * hp + 1)

    nat_tile = pl.BlockSpec((t, FOX_HEADS), lambda i, hp: (i, 0))
    out_tile = pl.BlockSpec((t, PAIR_W), lambda i, hp: (i, hp))
    return pl.pallas_call(
        body, name=name, grid=(s // t, HEAD_PAIRS),
        in_specs=[pl.BlockSpec((t, PAIR_W), lambda i, hp: (i, hp)),
                  pl.BlockSpec((s, PAIR_W), lambda i, hp: (0, HEAD_PAIRS + hp)),
                  pl.BlockSpec((s, PAIR_W), lambda i, hp: (0, 2 * HEAD_PAIRS + hp)),
                  nat_tile, pl.BlockSpec((None, 2, s), lambda i, hp: (hp, 0, 0))],
        out_specs=[out_tile, out_tile, nat_tile],
        out_shape=[jax.ShapeDtypeStruct((s, D_MODEL), F32), jax.ShapeDtypeStruct((s, D_MODEL), BF16),
                   jax.ShapeDtypeStruct((s, FOX_HEADS), F32)],
        compiler_params=_cparams(("arbitrary", "arbitrary")),
    )(qkv, qkv, qkv, cum, cum_t)


def _fox_delta(do, o, expand, name):
    def body(do_ref, o_ref, e_ref, d_ref):
        prod = do_ref[...] * o_ref[...]
        hi = prod.astype(BF16)
        lo = (prod - hi.astype(F32)).astype(BF16)
        tot = (jnp.dot(hi, e_ref[...], preferred_element_type=F32)
               + jnp.dot(lo, e_ref[...], preferred_element_type=F32))
        d_ref[...] = tot[:, :FOX_HEADS]
    return _row_call(name, body, [do, o], [expand], [(FOX_HEADS, F32)], [])[0]


def _fox_attn_bwd(qkv, do, cum, cum_t, lse_t, delta_t, name):
    s = qkv.shape[0]
    t = min(ATT_TILE, s)
    nq = s // t
    tn_dims = (((0,), (0,)), ((), ()))

    def body(q_ref, k_ref, v_ref, do_ref, cum_ref, cumt_ref, lset_ref, deltat_ref,
             dq_ref, dk_ref, dv_ref, dck_ref, dcq_ref):
        hp = pl.program_id(0)
        j = pl.program_id(1)

        @pl.when(j == 0)
        def _():
            dq_ref[...] = jnp.zeros_like(dq_ref)
            dcq_ref[...] = jnp.zeros_like(dcq_ref)
        dk_ref[...] = jnp.zeros_like(dk_ref)
        dv_ref[...] = jnp.zeros_like(dv_ref)

        lo = _low_half((t, PAIR_W))
        lane = lax.broadcasted_iota(jnp.int32, (t, PAIR_W), 1)
        kv = k_ref[...]
        vv = v_ref[...]
        zero = jnp.zeros_like(kv)
        k2 = (jnp.where(lo, kv, zero), jnp.where(lo, zero, kv))
        v2 = (jnp.where(lo, vv, zero), jnp.where(lo, zero, vv))
        cum_v = cum_ref[...]
        ck2 = (_pick_head(cum_v, 2 * hp), _pick_head(cum_v, 2 * hp + 1))

        def step(i, dck, masked):
            qs = pl.ds(pl.multiple_of(i * t, t), t)
            qi = q_ref[qs, :]
            do_i = do_ref[qs, :].astype(BF16)
            dv_p, dk_p, dq_p = [], [], []
            for e in range(2):
                st = lax.dot_general(k2[e], qi, _NT, preferred_element_type=F32) * _FOX_SCALE
                st = st + cumt_ref[e:e + 1, qs] - ck2[e]
                if masked:
                    row = lax.broadcasted_iota(jnp.int32, (t, t), 0)
                    col = lax.broadcasted_iota(jnp.int32, (t, t), 1)
                    st = jnp.where(row <= col, st, -jnp.inf)
                pt = jnp.exp(st - lset_ref[e:e + 1, qs])
                dv_p.append(jnp.dot(pt.astype(BF16), do_i, preferred_element_type=F32))
                dpt = lax.dot_general(v2[e], do_i, _NT, preferred_element_type=F32)
                dst = pt * (dpt - deltat_ref[e:e + 1, qs])
                dsb = dst.astype(BF16)
                dk_p.append(jnp.dot(dsb, qi, preferred_element_type=F32))
                dq_p.append(lax.dot_general(dsb, kv, tn_dims, preferred_element_type=F32))
                dck = dck - jnp.where(lane == e, jnp.sum(dst, axis=1, keepdims=True), 0.0)
                dcq_ref[e:e + 1, qs] += jnp.sum(dst, axis=0, keepdims=True)
            dv_ref[...] += jnp.where(lo, dv_p[0], dv_p[1])
            dk_ref[...] += jnp.where(lo, dk_p[0], dk_p[1])
            dq_ref[qs, :] += jnp.where(lo, dq_p[0], dq_p[1]) * _FOX_SCALE
            return dck

        dck = step(j, jnp.zeros((t, PAIR_W), F32), True)
        dck = lax.fori_loop(j + 1, nq, lambda i, c: step(i, c, False), dck)
        dk_ref[...] = dk_ref[...] * _FOX_SCALE
        dck_ref[...] = dck

    pair_full = lambda part: pl.BlockSpec((s, PAIR_W), lambda hp, j: (0, part * HEAD_PAIRS + hp))
    pair_tile = lambda part: pl.BlockSpec((t, PAIR_W), lambda hp, j: (j, part * HEAD_PAIRS + hp))
    rows = pl.BlockSpec((None, 2, s), lambda hp, j: (hp, 0, 0))
    return pl.pallas_call(
        body, name=name, grid=(HEAD_PAIRS, nq),
        in_specs=[pair_full(0), pair_tile(1), pair_tile(2), pair_full(0),
                  pl.BlockSpec((t, FOX_HEADS), lambda hp, j: (j, 0)), rows, rows, rows],
        out_specs=[pair_full(0), pair_tile(0), pair_tile(0),
                   pl.BlockSpec((None, t, PAIR_W), lambda hp, j: (hp, j, 0)), rows],
        out_shape=[jax.ShapeDtypeStruct((s, D_MODEL), F32)] * 3
        + [jax.ShapeDtypeStruct((HEAD_PAIRS, s, PAIR_W), F32), jax.ShapeDtypeStruct((HEAD_PAIRS, 2, s), F32)],
        compiler_params=_cparams(("arbitrary", "arbitrary")),
    )(qkv, qkv, qkv, do, cum, cum_t, lse_t, delta_t)


def _shift_down(v, d):
    row = lax.broadcasted_iota(jnp.int32, v.shape, 0)
    return jnp.where(row >= d, pltpu.roll(v, d, 0), 0.0)


def _shift_up(v, d):
    s = v.shape[0]
    row = lax.broadcasted_iota(jnp.int32, v.shape, 0)
    return jnp.where(row < s - d, pltpu.roll(v, s - d, 0), 0.0)


def _conv_taps(v, cw_ref, width):
    out = cw_ref[width - 1:width, :] * v
    for k in range(width - 1):
        out = out + cw_ref[k:k + 1, :] * _shift_down(v, width - 1 - k)
    return out


def _conv_taps_bwd(dout, v, cw_ref, dcw_ref, width):
    dv = cw_ref[width - 1:width, :] * dout
    dcw_ref[width - 1:width, :] = _rows_sum(dout * v)
    for k in range(width - 1):
        d = width - 1 - k
        dv = dv + cw_ref[k:k + 1, :] * _shift_up(dout, d)
        dcw_ref[k:k + 1, :] = _rows_sum(dout * _shift_down(v, d))
    return dv


def _col_spec(s, tc, part=0):
    off = part * (D_MODEL // tc)
    return pl.BlockSpec((s, tc), lambda c: (0, c + off))


def _small_spec(rows, tc):
    return pl.BlockSpec((rows, tc), lambda c: (0, c))


def _col_call(name, body, in_arrays, in_specs, out_rows, s, tc):
    return pl.pallas_call(
        body, name=name, grid=(D_MODEL // tc,), in_specs=in_specs,
        out_specs=[pl.BlockSpec((r, tc), lambda c: (0, c)) for r, _ in out_rows],
        out_shape=[jax.ShapeDtypeStruct((r, D_MODEL), dt) for r, dt in out_rows],
        compiler_params=_cparams(("arbitrary",)),
    )(*in_arrays)


def _sconv_fwd(proj, conv_w, name):
    s = proj.shape[0]
    tc = COL_TILE

    def body(b_ref, c_ref, x_ref, cw_ref, y_ref):
        y_ref[...] = (b_ref[...] * _conv_taps(c_ref[...] * x_ref[...], cw_ref, 3)).astype(BF16)

    return _col_call(name, body, [proj, proj, proj, conv_w],
                     [_col_spec(s, tc, 0), _col_spec(s, tc, 1), _col_spec(s, tc, 2), _small_spec(3, tc)],
                     [(s, BF16)], s, tc)[0]


def _sconv_bwd(dy, proj, conv_w, name):
    s = proj.shape[0]
    tc = COL_TILE

    def body(dy_ref, b_ref, c_ref, x_ref, cw_ref, db_ref, dc_ref, dx_ref, dcw_ref):
        w = c_ref[...] * x_ref[...]
        dy_v = dy_ref[...]
        db_ref[...] = (dy_v * _conv_taps(w, cw_ref, 3)).astype(BF16)
        dw = _conv_taps_bwd(dy_v * b_ref[...], w, cw_ref, dcw_ref, 3)
        dc_ref[...] = (dw * x_ref[...]).astype(BF16)
        dx_ref[...] = (dw * c_ref[...]).astype(BF16)

    return _col_call(name, body, [dy, proj, proj, proj, conv_w],
                     [_col_spec(s, tc), _col_spec(s, tc, 0), _col_spec(s, tc, 1), _col_spec(s, tc, 2),
                      _small_spec(3, tc)],
                     [(s, BF16), (s, BF16), (s, BF16), (3, F32)], s, tc)


def _lru_conv(proj, conv_w, conv_b, name):
    s = proj.shape[0]
    tc = COL_TILE

    def body(x_ref, cw_ref, cb_ref, xb_ref, xbb_ref):
        xb = _conv_taps(x_ref[...], cw_ref, 4) + cb_ref[...]
        xb_ref[...] = xb
        xbb_ref[...] = xb.astype(BF16)

    return _col_call(name, body, [proj, conv_w, conv_b],
                     [_col_spec(s, tc, 1), _small_spec(4, tc), _small_spec(1, tc)],
                     [(s, F32), (s, BF16)], s, tc)


def _lru_conv_bwd(dxb1, dxb2, proj, conv_w, name):
    s = proj.shape[0]
    tc = COL_TILE

    def body(d1_ref, d2_ref, x_ref, cw_ref, dx_ref, dcw_ref, dcb_ref):
        dxb = d1_ref[...] + d2_ref[...]
        dcb_ref[...] = _rows_sum(dxb)
        dx_ref[...] = _conv_taps_bwd(dxb, x_ref[...], cw_ref, dcw_ref, 4).astype(BF16)

    return _col_call(name, body, [dxb1, dxb2, proj, conv_w],
                     [_col_spec(s, tc), _col_spec(s, tc), _col_spec(s, tc, 1), _small_spec(4, tc)],
                     [(s, BF16), (4, F32), (1, F32)], s, tc)


_GELU_C = math.sqrt(2.0 / math.pi)


def _gelu_parts(g):
    inner = _GELU_C * (g + 0.044715 * g * g * g)
    th = jnp.tanh(inner)
    val = 0.5 * g * (1.0 + th)
    der = 0.5 * (1.0 + th) + 0.5 * g * (1.0 - th * th) * (_GELU_C * (1.0 + 3.0 * 0.044715 * g * g))
    return val, der


def _lru_gates(pa_ref, px_ref, ba_ref, bx_ref, lam_ref):
    r = _sigmoid(pa_ref[...] + ba_ref[...])
    ig = _sigmoid(px_ref[...] + bx_ref[...])
    sp = _softplus(-lam_ref[...])
    log_a = (-LRU_C) * r * sp
    a = jnp.exp(log_a)
    z = 2.0 * log_a
    one_m_a2 = jnp.where(z > -1e-3, -(z * (1.0 + z * (0.5 + z * (1.0 / 6.0)))), 1.0 - jnp.exp(z))
    return r, ig, sp, a, jnp.sqrt(one_m_a2)


def _lru_scan(pre, xb, proj, b_a, b_x, lam, name):
    s = xb.shape[0]
    tc = COL_TILE

    def body(pa_ref, px_ref, xb_ref, g_ref, ba_ref, bx_ref, lam_ref, y_ref, hs_ref):
        _, ig, _, a, mult = _lru_gates(pa_ref, px_ref, ba_ref, bx_ref, lam_ref)
        b = mult * (ig * xb_ref[...])
        d = 1
        while d < s:
            row = lax.broadcasted_iota(jnp.int32, a.shape, 0)
            keep = row >= d
            b = b + a * jnp.where(keep, pltpu.roll(b, d, 0), 0.0)
            a = a * jnp.where(keep, pltpu.roll(a, d, 0), 1.0)
            d *= 2
        hs_ref[...] = b
        y_ref[...] = (b * _gelu_parts(g_ref[...])[0]).astype(BF16)

    return _col_call(name, body, [pre, pre, xb, proj, b_a, b_x, lam],
                     [_col_spec(s, tc, 0), _col_spec(s, tc, 1), _col_spec(s, tc), _col_spec(s, tc, 0),
                      _small_spec(1, tc), _small_spec(1, tc), _small_spec(1, tc)],
                     [(s, BF16), (s, F32)], s, tc)


def _lru_scan_bwd(dy, pre, xb, proj, hs, b_a, b_x, lam, name):
    s = xb.shape[0]
    tc = COL_TILE

    def body(dy_ref, pa_ref, px_ref, xb_ref, g_ref, hs_ref, ba_ref, bx_ref, lam_ref,
             dg_ref, dpa_ref, dpx_ref, dxb_ref, dba_ref, dbx_ref, dlam_ref):
        r, ig, sp, a, mult = _lru_gates(pa_ref, px_ref, ba_ref, bx_ref, lam_ref)
        gl, gl_der = _gelu_parts(g_ref[...])
        dy_v = dy_ref[...]
        hs_v = hs_ref[...]
        dg_ref[...] = (dy_v * hs_v * gl_der).astype(BF16)
        lam_t = dy_v * gl
        coef = _shift_up(a, 1)
        d = 1
        while d < s:
            row = lax.broadcasted_iota(jnp.int32, coef.shape, 0)
            keep = row < s - d
            lam_t = lam_t + coef * jnp.where(keep, pltpu.roll(lam_t, s - d, 0), 0.0)
            coef = coef * jnp.where(keep, pltpu.roll(coef, s - d, 0), 1.0)
            d *= 2
        xb_v = xb_ref[...]
        da = lam_t * _shift_down(hs_v, 1)
        dmult = lam_t * (ig * xb_v)
        dig = lam_t * mult * xb_v
        dxb_ref[...] = lam_t * mult * ig
        dlog_a = da * a - dmult * (a * a) / mult
        dr = dlog_a * ((-LRU_C) * sp)
        dsp = _rows_sum(dlog_a * ((-LRU_C) * r))
        dlam_ref[...] = -dsp * _sigmoid(-lam_ref[...])
        dpa = dr * r * (1.0 - r)
        dpx = dig * ig * (1.0 - ig)
        dba_ref[...] = _rows_sum(dpa)
        dbx_ref[...] = _rows_sum(dpx)
        dpa_ref[...] = dpa.astype(BF16)
        dpx_ref[...] = dpx.astype(BF16)

    return _col_call(name, body, [dy, pre, pre, xb, proj, hs, b_a, b_x, lam],
                     [_col_spec(s, tc), _col_spec(s, tc, 0), _col_spec(s, tc, 1), _col_spec(s, tc),
                      _col_spec(s, tc, 0), _col_spec(s, tc),
                      _small_spec(1, tc), _small_spec(1, tc), _small_spec(1, tc)],
                     [(s, BF16), (s, BF16), (s, BF16), (s, F32), (1, F32), (1, F32), (1, F32)], s, tc)


def _ffn_fwd(x, w_in, w_out, g_pre, g_post, shift, scale, gate, tag, after=None):
    h = _pre_norm(x, g_pre, scale, shift, tag + "_pre", after=after)
    gu = _mm_nn(h, w_in, tag + "_in")
    a = _swiglu_act(gu, tag + "_act")
    y = _mm_nn(a, w_out, tag + "_out", tn=512)
    xo = _post_norm(x, y, g_post, gate, 0.5, tag + "_post")
    return xo, (x, h, gu, a, y)


def _ffn_bwd(dxo, saved, w_in, w_out, g_pre, g_post, scale, gate, tag, after=None):
    x, h, gu, a, y = saved
    dy, dgate, dg_post = _post_norm_bwd(dxo, y, g_post, gate, 0.5, tag + "_post_b", after=after)
    da = _mm_nt(dy, w_out, tag + "_out_bx", tk=D_FF // 2)
    dw_out = _mm_tn(a, dy, tag + "_out_bw", tk=D_FF // 2)
    dgu = _swiglu_act_bwd(da, gu, tag + "_act_b")
    dh = _mm_nt(dgu, w_in, tag + "_in_bx")
    dw_in = _mm_tn(h, dgu, tag + "_in_bw", tn=w_in.arr.shape[-1], blocked_out=True)
    dx, dshift, dscale, dg_pre = _pre_norm_bwd(dxo, dh, x, g_pre, scale, tag + "_pre_b")
    return dx, dw_in, dw_out, (dshift, dscale, dgate), dg_pre, dg_post


def _pair_rows(v):
    return v.T.reshape(HEAD_PAIRS, 2, v.shape[0])


def _fox_fwd(h, p, tag):
    s = h.shape[0]
    proj = _mm_nn(h, p["w_in"], tag + "_in", tn=640)
    qkv = proj[:, :3 * D_MODEL].astype(BF16)
    flt = proj[:, 3 * D_MODEL:3 * D_MODEL + FOX_HEADS].T
    cum_t = _fox_gate(flt, p["b_f"], tag + "_gate")
    cum = cum_t.T
    cum_t2 = cum_t.reshape(HEAD_PAIRS, 2, s)
    o, ob, lse = _fox_attn_fwd(qkv, cum, cum_t2, tag + "_attn")
    y = _mm_nn(ob, p["w_out"], tag + "_out")
    return y, (qkv, flt, cum, cum_t2, o, ob, lse)


def _fox_bwd(dy, h, saved, p, tag):
    qkv, flt, cum, cum_t2, o, ob, lse = saved
    s = h.shape[0]
    do = _mm_nt(dy, p["w_out"], tag + "_out_bx")
    dw_out = _mm_tn(ob, dy, tag + "_out_bw")
    expand = jnp.pad(jnp.repeat(jnp.eye(FOX_HEADS, dtype=BF16), FOX_HEAD_DIM, axis=0),
                     ((0, 0), (0, PAIR_W - FOX_HEADS)))
    delta = _fox_delta(do, o, expand, tag + "_attn_delta")
    dq, dk, dv, dck, dcq = _fox_attn_bwd(qkv, do, cum, cum_t2, _pair_rows(lse), _pair_rows(delta), tag + "_attn_b")
    dcum_k = dck[:, :, :2].transpose(0, 2, 1).reshape(FOX_HEADS, s)
    dflt, db_f = _fox_gate_bwd(dcq.reshape(FOX_HEADS, s), dcum_k, flt, p["b_f"], tag + "_gate_b")
    dproj = jnp.concatenate(
        [dq, dk, dv, dflt.T, jnp.zeros((s, FOX_PAD - 3 * D_MODEL - FOX_HEADS), F32)], axis=1).astype(BF16)
    dh = _mm_nt(dproj, p["w_in"], tag + "_in_bx", tn=640)
    dw_in = _mm_tn(h, dproj, tag + "_in_bw", tn=640)
    return dh, {"w_in": dw_in, "w_out": dw_out, "b_f": db_f}


def _sconv_mix_fwd(h, p, tag):
    proj = _mm_nn(h, p["w_in"], tag + "_in")
    yb = _sconv_fwd(proj, p["conv_w"], tag + "_conv")
    y = _mm_nn(yb, p["w_out"], tag + "_out")
    return y, (proj, yb)


def _sconv_mix_bwd(dy, h, saved, p, tag):
    proj, yb = saved
    dyb = _mm_nt(dy, p["w_out"], tag + "_out_bx")
    dw_out = _mm_tn(yb, dy, tag + "_out_bw")
    db, dc, dxv, dcw = _sconv_bwd(dyb, proj, p["conv_w"], tag + "_conv_b")
    dproj = jnp.concatenate([db, dc, dxv], axis=1)
    dh = _mm_nt(dproj, p["w_in"], tag + "_in_bx")
    dw_in = _mm_tn(h, dproj, tag + "_in_bw", tn=p["w_in"].arr.shape[-1], blocked_out=True)
    return dh, {"w_in": dw_in, "w_out": dw_out, "conv_w": dcw}


def _lru_mix_fwd(h, p, tag):
    proj = _mm_nn(h, p["w_in"], tag + "_in")
    xb, xbb = _lru_conv(proj, p["conv_w"], p["conv_b"], tag + "_conv")
    pre = _mm_nn(xbb, p["w_ax"], tag + "_gates", tn=D_MODEL)
    yb, hs = _lru_scan(pre, xb, proj, p["b_a"], p["b_x"], p["lam"], tag + "_scan")
    y = _mm_nn(yb, p["w_out"], tag + "_out")
    return y, (proj, xb, xbb, pre, yb, hs)


def _diag_blocks(m):
    return jnp.stack([m[LRU_BLOCK_DIM * n:LRU_BLOCK_DIM * (n + 1), LRU_BLOCK_DIM * n:LRU_BLOCK_DIM * (n + 1)]
                      for n in range(LRU_BLOCKS)])


def _lru_mix_bwd(dy, h, saved, p, tag):
    proj, xb, xbb, pre, yb, hs = saved
    dyb = _mm_nt(dy, p["w_out"], tag + "_out_bx")
    dw_out = _mm_tn(yb, dy, tag + "_out_bw")
    dg, dpa, dpx, dxb1, dba, dbx, dlam = _lru_scan_bwd(dyb, pre, xb, proj, hs, p["b_a"], p["b_x"], p["lam"],
                                                       tag + "_scan_b")
    dpre = jnp.concatenate([dpa, dpx], axis=1)
    dxb2 = _mm_nt(dpre, p["w_ax"], tag + "_gates_bx", tn=D_MODEL)
    dw_ax = _mm_tn(xbb, dpre, tag + "_gates_bw", tn=D_MODEL)
    dx0, dcw, dcb = _lru_conv_bwd(dxb1, dxb2, proj, p["conv_w"], tag + "_conv_b")
    dproj = jnp.concatenate([dg, dx0], axis=1)
    dh = _mm_nt(dproj, p["w_in"], tag + "_in_bx")
    dw_in = _mm_tn(h, dproj, tag + "_in_bw", tn=p["w_in"].arr.shape[-1], blocked_out=True)
    grads = {"w_in": dw_in, "w_out": dw_out, "conv_w": dcw, "conv_b": dcb,
             "w_a": _diag_blocks(dw_ax[:, :D_MODEL]), "w_x": _diag_blocks(dw_ax[:, D_MODEL:]),
             "b_a": dba, "b_x": dbx, "lam": dlam}
    return dh, grads


_MIXERS = ((_fox_fwd, _fox_bwd), (_sconv_mix_fwd, _sconv_mix_bwd), (_lru_mix_fwd, _lru_mix_bwd))


def _local_step(x, target, mod, layer_params, on_grads=None, on_mid=None, first_after=None):
    layers = []
    tape = []
    for i in range(DEPTH):
        lp = dict(layer_params(i, 0, x))
        layers.append(lp)
        row = lambda v: v[None, :]
        m = lambda sub, what: mod[i, sub, what][None, :]
        x, sv0 = _ffn_fwd(x, lp["ffn_in"][0], lp["ffn_out"][0], row(lp["norm_pre"][0]), row(lp["norm_post"][0]),
                          m(0, 0), m(0, 1), m(0, 2), f"l{i}_ffn0", after=first_after if i == 0 else None)
        lp.update(layer_params(i, 1, x))
        h = _pre_norm(x, row(lp["norm_pre"][1]), m(1, 1), m(1, 0), f"l{i}_mix_pre")
        y, svm = _MIXERS[i % 3][0](h, lp["mixer"], f"l{i}_mix")
        x1 = _post_norm(x, y, row(lp["norm_post"][1]), m(1, 2), 1.0, f"l{i}_mix_post")
        second = layer_params(i, 2, x1)
        lp["ffn_in"] = lp["ffn_in"] + second["ffn_in"]
        lp["ffn_out"] = lp["ffn_out"] + second["ffn_out"]
        x2, sv2 = _ffn_fwd(x1, lp["ffn_in"][1], lp["ffn_out"][1], row(lp["norm_pre"][2]), row(lp["norm_post"][2]),
                           m(2, 0), m(2, 1), m(2, 2), f"l{i}_ffn1")
        tape.append((sv0, (x, h, y, svm), sv2))
        x = x2
    dx, loss_row = _loss_head(x, target, "loss_head")

    layer_grads = [None] * DEPTH
    dmod = [None] * DEPTH
    after = None
    for i in reversed(range(DEPTH)):
        lp = layers[i]
        row = lambda v: v[None, :]
        m = lambda sub, what: mod[i, sub, what][None, :]
        sv0, (xm, h, y, svm), sv2 = tape[i]
        dx, dw_in1, dw_out1, dm2, dgp2, dgq2 = _ffn_bwd(dx, sv2, lp["ffn_in"][1], lp["ffn_out"][1],
                                                        row(lp["norm_pre"][2]), row(lp["norm_post"][2]),
                                                        m(2, 1), m(2, 2), f"l{i}_ffn1", after=after)
        after = on_mid(i, dx) if on_mid is not None else None
        dy, dgate1, dgq1 = _post_norm_bwd(dx, y, row(lp["norm_post"][1]), m(1, 2), 1.0, f"l{i}_mix_post_b", after=after)
        dh, mg = _MIXERS[i % 3][1](dy, h, svm, lp["mixer"], f"l{i}_mix")
        dx, dshift1, dscale1, dgp1 = _pre_norm_bwd(dx, dh, xm, row(lp["norm_pre"][1]), m(1, 1), f"l{i}_mix_pre_b")
        dx, dw_in0, dw_out0, dm0, dgp0, dgq0 = _ffn_bwd(dx, sv0, lp["ffn_in"][0], lp["ffn_out"][0],
                                                        row(lp["norm_pre"][0]), row(lp["norm_post"][0]),
                                                        m(0, 1), m(0, 2), f"l{i}_ffn0")
        dmod[i] = jnp.concatenate([*dm0, dshift1, dscale1, dgate1, *dm2], axis=0).reshape(N_SUB, 3, D_MODEL)
        layer_grads[i] = {"ffn_in": (dw_in0, dw_in1), "ffn_out": (dw_out0, dw_out1),
                          "norm_pre": jnp.concatenate([dgp0, dgp1, dgp2], axis=0),
                          "norm_post": jnp.concatenate([dgq0, dgq1, dgq2], axis=0), "mixer": mg}
        if on_grads is not None:
            after = on_grads(i, layer_grads[i], dx)
    return loss_row, dx, jnp.stack(dmod), layer_grads


COND_ROWS = 16
COND_PAD = 128


def _cond_fwd(c_pad, w_cond, b_shard, name):
    nl, d, n = w_cond.shape
    tn = 768

    def body(c_ref, w_ref, b_ref, o_ref):
        cv = c_ref[...]
        act = (cv * _sigmoid(cv)).astype(BF16)
        o_ref[...] = jnp.dot(act, w_ref[...].astype(BF16), preferred_element_type=F32) + b_ref[...]

    return pl.pallas_call(
        body, name=name, grid=(nl, n // tn),
        in_specs=[pl.BlockSpec((COND_ROWS, d), lambda i, j: (0, 0)),
                  pl.BlockSpec((None, d, tn), lambda i, j: (i, 0, j)),
                  pl.BlockSpec((None, 1, tn), lambda i, j: (i, 0, j))],
        out_specs=pl.BlockSpec((None, COND_ROWS, tn), lambda i, j: (i, 0, j)),
        out_shape=jax.ShapeDtypeStruct((nl, COND_ROWS, n), F32),
        compiler_params=_cparams(("arbitrary", "arbitrary")),
    )(c_pad, w_cond, b_shard)


def _adam_math(w, g, m, v):
    nm = ADAM_B1 * m + (1.0 - ADAM_B1) * g
    nv = ADAM_B2 * v + (1.0 - ADAM_B2) * (g * g)
    m_hat = nm / (1.0 - ADAM_B1 ** ADAM_STEP)
    v_hat = nv / (1.0 - ADAM_B2 ** ADAM_STEP)
    delta = (-ADAM_LR) * (m_hat / (jnp.sqrt(v_hat) + ADAM_EPS) + ADAM_WD * w)
    return delta, nm, nv


def _cond_bwd_adamw(c_t, dmod_s, w, m, v, name):
    nl, d, n = w.shape
    tn = 384
    blk = pl.BlockSpec((None, d, tn), lambda i, j: (i, 0, j))

    def body(c_ref, dm_ref, w_ref, m_ref, v_ref, g_ref, d_ref, nm_ref, nv_ref):
        cv = c_ref[...]
        g = jnp.dot((cv * _sigmoid(cv)).astype(BF16), dm_ref[...], preferred_element_type=F32)
        g_ref[...] = g
        d_ref[...], nm_ref[...], nv_ref[...] = _adam_math(w_ref[...], g, m_ref[...], v_ref[...])

    return pl.pallas_call(
        body, name=name, grid=(nl, n // tn),
        in_specs=[pl.BlockSpec((d, COND_PAD), lambda i, j: (0, 0)),
                  pl.BlockSpec((None, COND_PAD, tn), lambda i, j: (i, 0, j)), blk, blk, blk],
        out_specs=[blk] * 4, out_shape=[jax.ShapeDtypeStruct(w.shape, F32)] * 4,
        compiler_params=_cparams(("arbitrary", "arbitrary")),
    )(c_t, dmod_s, w, m, v)


def _adamw(w, g, m, v, name):
    rows, cols = w.shape
    tr = next(t for t in (256, 176, 128, 64, 32, 16, 8) if rows % t == 0)
    blk = pl.BlockSpec((tr, cols), lambda i: (i, 0))

    def body(w_ref, g_ref, m_ref, v_ref, d_ref, nm_ref, nv_ref):
        d_ref[...], nm_ref[...], nv_ref[...] = _adam_math(w_ref[...], g_ref[...], m_ref[...], v_ref[...])

    return pl.pallas_call(
        body, name=name, grid=(rows // tr,), in_specs=[blk] * 4, out_specs=[blk] * 3,
        out_shape=[jax.ShapeDtypeStruct(w.shape, F32)] * 3, compiler_params=_cparams(("arbitrary",)),
    )(w, g, m, v)


_MESH = pl.DeviceIdType.MESH
_ANY = pl.BlockSpec(memory_space=pl.ANY)


def _place():
    return lax.axis_index("x"), lax.axis_index("y"), lax.axis_index("c")


def _other_chips(x, y):
    return [(1 - x, y), (x, 1 - y), (1 - x, 1 - y)]


def _allgather8(block, name):
    m_per, n = block.shape

    def body(x_ref, out_ref, send_sems, recv_sems, local_sem):
        x, y, c = _place()
        me, sibling = (x, y, c), (x, y, 1 - c)
        chips = _other_chips(x, y)

        def rows(px, py, pc):
            return out_ref.at[pl.ds((4 * px + 2 * py + pc) * m_per, m_per), :]

        def copy(k, blk, to, src=None):
            return pltpu.make_async_remote_copy(
                src_ref=rows(*blk) if src is None else src, dst_ref=rows(*blk),
                send_sem=send_sems.at[k], recv_sem=recv_sems.at[k], device_id=to, device_id_type=_MESH)

        mine = pltpu.make_async_copy(x_ref, rows(*me), local_sem)
        mine.start()
        first = [copy(0, me, sibling, src=x_ref)]
        first += [copy(1 + j, me, (*chip, c), src=x_ref) for j, chip in enumerate(chips)]
        for cp in first:
            cp.start()
        passed = [copy(4 + j, (*chip, c), sibling) for j, chip in enumerate(chips)]
        for j, chip in enumerate(chips):
            copy(1 + j, (*chip, c), me).wait_recv()
            passed[j].start()
        copy(0, sibling, me).wait_recv()
        for j, chip in enumerate(chips):
            copy(4 + j, (*chip, 1 - c), me).wait_recv()
        for cp in first + passed:
            cp.wait_send()
        mine.wait()

    return pl.pallas_call(
        body, name=name, out_shape=jax.ShapeDtypeStruct((N_DEV * m_per, n), block.dtype),
        in_specs=[pl.BlockSpec(memory_space=pltpu.VMEM)], out_specs=pl.BlockSpec(memory_space=pltpu.VMEM),
        scratch_shapes=[pltpu.SemaphoreType.DMA((7,)), pltpu.SemaphoreType.DMA((7,)), pltpu.SemaphoreType.DMA],
        compiler_params=_cparams(),
    )(block)


def _split_axis(shape):
    return next(a for a, n in enumerate(shape) if n > 1)


_HBM = pl.BlockSpec(memory_space=pltpu.HBM)
_SEM = pl.BlockSpec(memory_space=pltpu.SEMAPHORE)
_SPLIT_COPY = pltpu.CompilerParams(has_side_effects=pltpu.SideEffectType.DATAFLOW_SIDE_EFFECTING)
_TOKEN = jax.ShapeDtypeStruct((8, 128), F32)


def _in_hbm(arrays):
    return [pltpu.with_memory_space_constraint(a, pltpu.HBM) for a in arrays]


class _Gathered(NamedTuple):
    shard_shape: tuple
    chip_axis: int

    @property
    def shape(self):
        return self.shard_shape[:self.chip_axis] + (N_CHIPS,) + self.shard_shape[self.chip_axis:]

    def half(self, ref, chip, pc):
        cut = _split_axis(self.shard_shape)
        n = self.shard_shape[cut] // 2
        idx = [slice(None)] * len(self.shard_shape)
        idx[cut] = pl.ds(pc * n, n)
        idx.insert(self.chip_axis, chip)
        return ref.at[tuple(idx)]


def _own_block_placed(shard, layout, chip):
    return lax.dynamic_update_slice_in_dim(lax.empty(layout.shape, shard.dtype),
                                           jnp.expand_dims(shard, layout.chip_axis), chip, axis=layout.chip_axis)


def _gather_copies(lands, layouts, send_sems, recv_sems):
    x, y, c = _place()
    out = []
    for t, (land, lay) in enumerate(zip(lands, layouts)):
        for j, (px, py) in enumerate(_other_chips(x, y)):
            def copy(chip, t=t, j=j, px=px, py=py, land=land, lay=lay):
                return pltpu.make_async_remote_copy(
                    src_ref=lay.half(land, chip, c), dst_ref=lay.half(land, chip, c),
                    send_sem=send_sems.at[3 * t + j], recv_sem=recv_sems.at[3 * t + j],
                    device_id=(px, py, c), device_id_type=_MESH)
            out.append((copy(2 * x + y), copy(2 * px + py)))
    return out


def _gather_start(lands, layouts, after, name):
    nt = len(lands)
    order = [] if after is None else [after]

    def body(*refs):
        land_refs = refs[:nt]
        send_sems, recv_sems = refs[nt + len(order):nt + len(order) + 2]
        token = refs[-1]
        for send, _ in _gather_copies(land_refs, layouts, send_sems, recv_sems):
            send.start()
        token[...] = jnp.zeros_like(token)

    out = pl.pallas_call(
        body, name=name,
        out_shape=(pltpu.SemaphoreType.DMA((3 * nt,)), pltpu.SemaphoreType.DMA((3 * nt,)),
                   *[pltpu.HBM(a.shape, a.dtype) for a in lands], _TOKEN),
        in_specs=[_HBM] * nt + [_ANY] * len(order),
        out_specs=(_SEM, _SEM, *[_HBM] * nt, pl.BlockSpec(memory_space=pltpu.VMEM)),
        input_output_aliases={t: 2 + t for t in range(nt)}, compiler_params=_SPLIT_COPY,
    )(*_in_hbm(lands), *order)
    return out[0], out[1], list(out[2:2 + nt]), out[-1]


def _gather_wait(send_sems, recv_sems, lands, layouts, after, name):
    nt = len(lands)

    def body(*refs):
        land_refs = refs[:nt]
        sems = refs[nt:nt + 2]
        for send, arrival in _gather_copies(land_refs, layouts, *sems):
            send.wait_send()
            arrival.wait_recv()

    return list(pl.pallas_call(
        body, name=name, out_shape=tuple(pltpu.HBM(a.shape, a.dtype) for a in lands),
        in_specs=[_HBM] * nt + [_SEM, _SEM, _ANY], out_specs=tuple([_HBM] * nt),
        input_output_aliases={t: t for t in range(nt)}, compiler_params=_SPLIT_COPY,
    )(*lands, send_sems, recv_sems, after))


def _gather_forward(lands, layouts, name):
    nt = len(lands)

    def body(*refs):
        outs = refs[nt:2 * nt]
        send_sems, recv_sems = refs[2 * nt:]
        x, y, c = _place()
        sends, arrivals = [], []
        for t, lay in enumerate(layouts):
            for j, (px, py) in enumerate(_other_chips(x, y)):
                for pc, group in ((c, sends), (1 - c, arrivals)):
                    part = lay.half(outs[t], 2 * px + py, pc)
                    group.append(pltpu.make_async_remote_copy(
                        src_ref=part, dst_ref=part, send_sem=send_sems.at[3 * t + j], recv_sem=recv_sems.at[3 * t + j],
                        device_id=(x, y, 1 - c), device_id_type=_MESH))
        for cp in sends:
            cp.start()
        for cp in arrivals:
            cp.wait_recv()
        for cp in sends:
            cp.wait_send()

    return list(pl.pallas_call(
        body, name=name, out_shape=[jax.ShapeDtypeStruct(a.shape, a.dtype) for a in lands],
        in_specs=[_ANY] * nt, out_specs=[_ANY] * nt, input_output_aliases={t: t for t in range(nt)},
        scratch_shapes=[pltpu.SemaphoreType.DMA((3 * nt,)), pltpu.SemaphoreType.DMA((3 * nt,))],
        compiler_params=_cparams(),
    )(*lands))


def _pair_copies(grads, lands, send_sems, recv_sems):
    x, y, c = _place()
    out = []
    for t, (g, land) in enumerate(zip(grads, lands)):
        h = g.shape[1] // 2
        out.append(pltpu.make_async_remote_copy(
            src_ref=g.at[:, pl.ds((1 - c) * h, h), :], dst_ref=land, send_sem=send_sems.at[t],
            recv_sem=recv_sems.at[t], device_id=(x, y, 1 - c), device_id_type=_MESH))
    return out


def _pair_start(grads, after, name):
    nt = len(grads)
    lands = [lax.empty((N_CHIPS, g.shape[1] // 2, g.shape[2]), g.dtype) for g in grads]
    order = [] if after is None else [after]

    def body(*refs):
        send_sems, recv_sems = refs[2 * nt + len(order):2 * nt + len(order) + 2]
        token = refs[-1]
        for cp in _pair_copies(refs[:nt], refs[nt:2 * nt], send_sems, recv_sems):
            cp.start()
        token[...] = jnp.zeros_like(token)

    out = pl.pallas_call(
        body, name=name,
        out_shape=(pltpu.SemaphoreType.DMA((nt,)), pltpu.SemaphoreType.DMA((nt,)),
                   *[pltpu.HBM(a.shape, a.dtype) for a in grads + lands], _TOKEN),
        in_specs=[_HBM] * (2 * nt) + [_ANY] * len(order),
        out_specs=(_SEM, _SEM, *[_HBM] * (2 * nt), pl.BlockSpec(memory_space=pltpu.VMEM)),
        input_output_aliases={t: 2 + t for t in range(2 * nt)}, compiler_params=_SPLIT_COPY,
    )(*_in_hbm(grads + lands), *order)
    return out[0], out[1], list(out[2:2 + nt]), list(out[2 + nt:2 + 2 * nt]), out[-1]


def _pair_wait(send_sems, recv_sems, grads, lands, after, name):
    nt = len(grads)

    def body(*refs):
        for cp in _pair_copies(refs[:nt], refs[nt:2 * nt], *refs[2 * nt:2 * nt + 2]):
            cp.wait_send()
            cp.wait_recv()

    out = pl.pallas_call(
        body, name=name, out_shape=tuple(pltpu.HBM(a.shape, a.dtype) for a in grads + lands),
        in_specs=[_HBM] * (2 * nt) + [_SEM, _SEM, _ANY], out_specs=tuple([_HBM] * (2 * nt)),
        input_output_aliases={t: t for t in range(2 * nt)}, compiler_params=_SPLIT_COPY,
    )(*grads, *lands, send_sems, recv_sems, after)
    return list(out[:nt]), list(out[nt:])


def _pair_sum(own, recv, c_idx, name):
    _, h, cols = recv.shape

    def body(c_ref, own_ref, recv_ref, o_ref):
        o_ref[...] = (own_ref[...] + recv_ref[...]).astype(BF16)

    return pl.pallas_call(
        body, name=name,
        grid_spec=pltpu.PrefetchScalarGridSpec(
            num_scalar_prefetch=1, grid=(N_CHIPS,),
            in_specs=[pl.BlockSpec((None, h, cols), lambda k, c_ref: (k, c_ref[0], 0)),
                      pl.BlockSpec((None, h, cols), lambda k, c_ref: (k, 0, 0))],
            out_specs=pl.BlockSpec((None, h, cols), lambda k, c_ref: (k, 0, 0))),
        out_shape=jax.ShapeDtypeStruct(recv.shape, BF16), compiler_params=_cparams(("arbitrary",)),
    )(c_idx, own, recv)


def _chip_copies(parts, lands, send_sems, recv_sems):
    x, y, c = _place()
    out = []
    for t, (part, land) in enumerate(zip(parts, lands)):
        for j, (px, py) in enumerate(_other_chips(x, y)):
            out.append(pltpu.make_async_remote_copy(
                src_ref=part.at[2 * px + py], dst_ref=land.at[j], send_sem=send_sems.at[3 * t + j],
                recv_sem=recv_sems.at[3 * t + j], device_id=(px, py, c), device_id_type=_MESH))
    return out


def _chip_send_start(parts, after, name):
    nt = len(parts)
    lands = [lax.empty((N_CHIPS - 1,) + p.shape[1:], p.dtype) for p in parts]
    order = [] if after is None else [after]

    def body(*refs):
        send_sems, recv_sems = refs[2 * nt + len(order):2 * nt + len(order) + 2]
        token = refs[-1]
        for cp in _chip_copies(refs[:nt], refs[nt:2 * nt], send_sems, recv_sems):
            cp.start()
        token[...] = jnp.zeros_like(token)

    out = pl.pallas_call(
        body, name=name,
        out_shape=(pltpu.SemaphoreType.DMA((3 * nt,)), pltpu.SemaphoreType.DMA((3 * nt,)),
                   *[pltpu.HBM(a.shape, a.dtype) for a in parts + lands], _TOKEN),
        in_specs=[_HBM] * (2 * nt) + [_ANY] * len(order),
        out_specs=(_SEM, _SEM, *[_HBM] * (2 * nt), pl.BlockSpec(memory_space=pltpu.VMEM)),
        input_output_aliases={t: 2 + t for t in range(2 * nt)}, compiler_params=_SPLIT_COPY,
    )(*_in_hbm(parts + lands), *order)
    return out[0], out[1], list(out[2:2 + nt]), list(out[2 + nt:2 + 2 * nt]), out[-1]


def _chip_send_wait(send_sems, recv_sems, parts, lands, after, name):
    nt = len(parts)

    def body(*refs):
        for cp in _chip_copies(refs[:nt], refs[nt:2 * nt], *refs[2 * nt:2 * nt + 2]):
            cp.wait_send()
            cp.wait_recv()

    out = pl.pallas_call(
        body, name=name, out_shape=tuple(pltpu.HBM(a.shape, a.dtype) for a in parts + lands),
        in_specs=[_HBM] * (2 * nt) + [_SEM, _SEM, _ANY], out_specs=tuple([_HBM] * (2 * nt)),
        input_output_aliases={t: t for t in range(2 * nt)}, compiler_params=_SPLIT_COPY,
    )(*parts, *lands, send_sems, recv_sems, after)
    return list(out[:nt]), list(out[nt:])


def _chip_sum(part, arrived, into, lead, place_idx, name):
    _, h, cols = part.shape

    def body(idx_ref, own_ref, arr_ref, into_ref, o_ref):
        acc = own_ref[...].astype(F32)
        for k in range(N_CHIPS - 1):
            acc = acc + arr_ref[k].astype(F32)
        o_ref[...] = acc

    return pl.pallas_call(
        body, name=name,
        grid_spec=pltpu.PrefetchScalarGridSpec(
            num_scalar_prefetch=1, grid=(1,),
            in_specs=[pl.BlockSpec((None, h, cols), lambda g, idx: (idx[1], 0, 0)),
                      pl.BlockSpec((N_CHIPS - 1, h, cols), lambda g, idx: (0, 0, 0)), _ANY],
            out_specs=pl.BlockSpec((None,) * len(lead) + (h, cols), lambda g, idx: (*lead, idx[0], 0))),
        out_shape=jax.ShapeDtypeStruct(into.shape, F32), input_output_aliases={3: 0},
        compiler_params=_cparams(("arbitrary",)),
    )(place_idx, part, arrived, into)


def _pair_gather(bufs, homes, name):
    nt, nb = len(homes), len(bufs)

    def body(*refs):
        outs = refs[nb:2 * nb]
        send_sems, recv_sems = refs[2 * nb:]
        x, y, c = _place()

        def home(t, pc):
            o, lead, rows = homes[t]
            return outs[o].at[(*lead, pl.ds(pc * (rows // 2), rows // 2), slice(None))]

        def copy(t, pc):
            return pltpu.make_async_remote_copy(src_ref=home(t, pc), dst_ref=home(t, pc), send_sem=send_sems.at[t],
                                                recv_sem=recv_sems.at[t], device_id=(x, y, 1 - c), device_id_type=_MESH)

        sends = [copy(t, c) for t in range(nt)]
        for cp in sends:
            cp.start()
        for t in range(nt):
            copy(t, 1 - c).wait_recv()
        for cp in sends:
            cp.wait_send()

    return pl.pallas_call(
        body, name=name, out_shape=[jax.ShapeDtypeStruct(b.shape, b.dtype) for b in bufs],
        in_specs=[_ANY] * nb, out_specs=[_ANY] * nb, input_output_aliases={o: o for o in range(nb)},
        scratch_shapes=[pltpu.SemaphoreType.DMA((nt,)), pltpu.SemaphoreType.DMA((nt,))],
        compiler_params=_cparams(),
    )(*bufs)


def _sum_devices(g, after, name):
    def body(g_ref, after_ref, o_ref):
        acc = g_ref[0:1, :]
        for d in range(1, N_DEV):
            acc = acc + g_ref[d:d + 1, :]
        o_ref[...] = acc
    vmem = pl.BlockSpec(memory_space=pltpu.VMEM)
    return pl.pallas_call(body, name=name, out_shape=jax.ShapeDtypeStruct((1, g.shape[1]), F32),
                          in_specs=[vmem, _ANY], out_specs=vmem, compiler_params=_cparams())(g, after)


_WEIGHTS = ("w_cond", "b_cond", "norm_pre", "norm_post", "w_ffn_in", "w_ffn_out", "fox_w_in", "fox_b_f",
            "fox_w_out", "sconv_w_in", "sconv_conv_w", "sconv_w_out", "lru_w_in", "lru_conv_w", "lru_conv_b",
            "lru_w_a", "lru_b_a", "lru_w_x", "lru_b_x", "lru_lambda", "lru_w_out")
_BIG = (("w_ffn_in", False), ("w_ffn_out", True), ("fox_w_in", False), ("fox_w_out", True),
        ("sconv_w_in", False), ("sconv_w_out", True), ("lru_w_in", False), ("lru_w_out", True))
_SMALL = tuple(n for n in _WEIGHTS if n != "w_cond" and n not in dict(_BIG))
_COL_SHARDED_SMALL = ("norm_pre", "norm_post", "sconv_conv_w", "lru_conv_w", "lru_conv_b", "lru_lambda")


def _pack_rows(parts, rows=8):
    flat = jnp.concatenate([p.reshape(-1) for p in parts])
    width = -(-flat.size // (rows * 128)) * 128
    return jnp.pad(flat, (0, rows * width - flat.size)).reshape(rows, width)


def _unpack(flat, shapes):
    out, off = [], 0
    for shp in shapes:
        n = math.prod(shp)
        out.append(flat[off:off + n].reshape(shp))
        off += n
    return out


def _join_chips(g):
    g = jnp.moveaxis(g, 0, -2)
    return g.reshape(g.shape[:-2] + (g.shape[-2] * g.shape[-1],))


def _my_columns(full, chip):
    n = full.shape[-1] // N_CHIPS
    return lax.dynamic_slice_in_dim(full, chip * n, n, axis=full.ndim - 1)


def _block_diag(w):
    eye = jnp.eye(LRU_BLOCKS, dtype=w.dtype)
    return jnp.einsum("nij,nm->nimj", w, eye).reshape(D_MODEL, D_MODEL)


def _step(x, c, target, wts, mom, var):
    ix, iy, ic = _place()
    chip = 2 * ix + iy
    dev = 2 * chip + ic
    n_cond = wts["w_cond"].shape[2]

    small_shapes = [(D_MODEL,)] + [wts[n].shape for n in _COL_SHARDED_SMALL]
    g1 = _allgather8(_pack_rows([c[0]] + [wts[n] for n in _COL_SHARDED_SMALL]), "gather_small").reshape(N_DEV, -1)
    c_all = g1[:, :D_MODEL]
    per_chip = [jnp.stack(col) for col in zip(*[_unpack(g1[2 * k], small_shapes) for k in range(N_CHIPS)])]
    small_full = {n: _join_chips(v) for n, v in zip(_COL_SHARDED_SMALL, per_chip[1:])}

    c_pad = jnp.pad(c_all, ((0, COND_ROWS - N_DEV), (0, 0)))
    b_shard = _my_columns(wts["b_cond"], chip)[:, None, :]
    mod_part = _cond_fwd(c_pad, wts["w_cond"], b_shard, "cond_fwd")
    g2 = _allgather8(mod_part[:, :N_DEV].transpose(1, 0, 2).reshape(N_DEV, DEPTH * n_cond), "gather_mod")
    g2 = g2.reshape(N_DEV, N_DEV, DEPTH, n_cond)[0::2]
    mod = _join_chips(lax.dynamic_index_in_dim(g2, dev, axis=1, keepdims=False)).reshape(DEPTH, N_SUB, 3, D_MODEL)

    mixer_names = [("fox_w_in", "fox_w_out"), ("sconv_w_in", "sconv_w_out"), ("lru_w_in", "lru_w_out")]

    def shards_of(i, sub):
        if sub == 1:
            return [wts[n][i // 3] for n in mixer_names[i % 3]]
        return [wts["w_ffn_in"][i, sub // 2], wts["w_ffn_out"][i, sub // 2]]

    chunks = [[(0, 0)], [(0, 1), (0, 2)]] + [[(i, sub) for sub in range(N_SUB)] for i in range(1, DEPTH)]
    in_flight, chunk_of, token = [], {}, mod
    for k, members in enumerate(chunks):
        shards = [s for i, sub in members for s in shards_of(i, sub)]
        layouts = [_Gathered(s.shape, 0) for s in shards]
        lands = [_own_block_placed(s.astype(BF16), lay, chip) for s, lay in zip(shards, layouts)]
        send_sems, recv_sems, lands, token = _gather_start(lands, layouts, token, f"gather_start_{k}")
        in_flight.append([send_sems, recv_sems, lands, layouts, False])
        chunk_of.update({m: (k, 2 * pos) for pos, m in enumerate(members)})
    lru_ax = jnp.concatenate([_block_diag(wts["lru_w_a"][0]), _block_diag(wts["lru_w_x"][0])], axis=1).astype(BF16)

    def layer_params(i, sub, x_in):
        k, pos = chunk_of[(i, sub)]
        send_sems, recv_sems, lands, layouts, arrived = in_flight[k]
        if not arrived:
            lands = _gather_wait(send_sems, recv_sems, lands, layouts, x_in, f"gather_wait_{k}")
            in_flight[k][2:] = [_gather_forward(lands, layouts, f"gather_forward_{k}"), layouts, True]
        w_in, w_out = in_flight[k][2][pos:pos + 2]
        w_out = w_out.reshape(-1, w_out.shape[-1])
        if sub != 1:
            out = {"ffn_in": [_W(w_in, (), True)], "ffn_out": [_W(w_out)]}
            if sub == 0:
                out.update(norm_pre=small_full["norm_pre"][i], norm_post=small_full["norm_post"][i])
            return out
        j = i // 3
        if i % 3 == 0:
            w_in = jnp.pad(_join_chips(w_in), ((0, 0), (0, FOX_PAD - 3 * D_MODEL - FOX_HEADS)))
            return {"mixer": {"w_in": _W(w_in), "w_out": _W(w_out), "b_f": wts["fox_b_f"][j][:, None]}}
        if i % 3 == 1:
            return {"mixer": {"w_in": _W(w_in, (), True), "w_out": _W(w_out), "conv_w": small_full["sconv_conv_w"][j]}}
        return {"mixer": {"w_in": _W(w_in, (), True), "w_out": _W(w_out), "conv_w": small_full["lru_conv_w"][j],
                          "conv_b": small_full["lru_conv_b"], "w_ax": _W(lru_ax),
                          "b_a": wts["lru_b_a"].reshape(1, D_MODEL), "b_x": wts["lru_b_x"].reshape(1, D_MODEL),
                          "lam": small_full["lru_lambda"]}}

    place_idx = jnp.stack([ic, chip]).astype(jnp.int32)
    c_idx = place_idx[:1]
    big_index = {n: o for o, (n, _) in enumerate(_BIG)}
    exchanges, pending = [], []

    def to_chips(after):
        i, send_sems, recv_sems, tensors, lands, homes = pending.pop()
        tensors, recv = _pair_wait(send_sems, recv_sems, tensors, lands, after, f"grads_pair_wait_l{i}")
        parts = [_pair_sum(t, r, c_idx, f"grads_pair_sum_l{i}_{k}") for k, (t, r) in enumerate(zip(tensors, recv))]
        send_sems, recv_sems, parts, lands, tok = _chip_send_start(parts, None, f"grads_chip_start_l{i}")
        exchanges.append((i, send_sems, recv_sems, parts, lands, homes))
        return tok

    def chip_blocks(g, by_rows, width):
        if by_rows:
            return g.reshape(N_CHIPS, g.shape[0] // N_CHIPS, g.shape[1])
        if g.ndim == 3:
            return g
        return g[:, :width * N_CHIPS].reshape(g.shape[0], N_CHIPS, width).transpose(1, 0, 2)

    def on_mid(i, dx):
        return to_chips(dx) if pending else None

    def on_grads(i, g, dx):
        n_in, n_out = mixer_names[i % 3]
        items = [("w_ffn_in", (i, k), g["ffn_in"][k]) for k in range(2)]
        items += [("w_ffn_out", (i, k), g["ffn_out"][k]) for k in range(2)]
        items += [(n_in, (i // 3,), g["mixer"]["w_in"]), (n_out, (i // 3,), g["mixer"]["w_out"])]
        tensors = [chip_blocks(t, dict(_BIG)[n], wts[n].shape[-1]) for n, _, t in items]
        homes = [(big_index[n], lead, wts[n].shape[-2]) for n, lead, _ in items]
        send_sems, recv_sems, tensors, lands, tok = _pair_start(tensors, None, f"grads_pair_start_l{i}")
        pending.append((i, send_sems, recv_sems, tensors, lands, homes))
        return tok

    loss_row, grad_x, dmod, lg = _local_step(x[0], target[0], mod, layer_params, on_grads, on_mid, token)
    loss = lax.psum(loss_row[0, 0], ("x", "y", "c"))

    fox_layers = [i for i in range(DEPTH) if i % 3 == 0]
    sconv_g, lru_g = lg[1]["mixer"], lg[2]["mixer"]
    small_g = {
        "dmod": dmod, "norm_pre": jnp.stack([g["norm_pre"] for g in lg]), "norm_post": jnp.stack([g["norm_post"] for g in lg]),
        "fox_b_f": jnp.stack([lg[i]["mixer"]["b_f"][:, 0] for i in fox_layers]),
        "sconv_conv_w": sconv_g["conv_w"][None], "lru_conv_w": lru_g["conv_w"][None], "lru_conv_b": lru_g["conv_b"],
        "lru_w_a": lru_g["w_a"][None], "lru_b_a": lru_g["b_a"].reshape(1, LRU_BLOCKS, LRU_BLOCK_DIM),
        "lru_w_x": lru_g["w_x"][None], "lru_b_x": lru_g["b_x"].reshape(1, LRU_BLOCKS, LRU_BLOCK_DIM),
        "lru_lambda": lru_g["lam"]}
    g4 = _allgather8(_pack_rows(list(small_g.values())), "gather_small_grads").reshape(N_DEV, -1)
    summed = _sum_devices(g4, to_chips(g4), "sum_small_grads")[0]
    summed = dict(zip(small_g, _unpack(summed, [v.shape for v in small_g.values()])))
    grads = {n: (_my_columns(summed[n], chip) if n in _COL_SHARDED_SMALL else summed[n]) for n in _SMALL if n != "b_cond"}
    grads["b_cond"] = summed["dmod"].reshape(DEPTH, N_SUB * 3 * D_MODEL)

    dmod_all = g4[:, :dmod.size].reshape(N_DEV, DEPTH, N_SUB * 3 * D_MODEL)
    dmod_s = jnp.pad(_my_columns(dmod_all, chip).transpose(1, 0, 2), ((0, 0), (0, COND_PAD - N_DEV), (0, 0))).astype(BF16)
    c_t = jnp.pad(c_all.T, ((0, 0), (0, COND_PAD - N_DEV)))
    grads["w_cond"], d_cond, m_cond, v_cond = _cond_bwd_adamw(c_t, dmod_s, wts["w_cond"], mom["w_cond"],
                                                              var["w_cond"], "cond_bwd_adamw")

    bufs = [lax.empty(wts[n].shape, F32) for n, _ in _BIG]
    all_homes = []
    for i, send_sems, recv_sems, parts, lands, homes in exchanges:
        follows = d_cond if not all_homes else bufs[0]
        parts, lands = _chip_send_wait(send_sems, recv_sems, parts, lands, follows, f"grads_chip_wait_l{i}")
        for k, (part, land, (o, lead, _)) in enumerate(zip(parts, lands, homes)):
            bufs[o] = _chip_sum(part, land, bufs[o], lead, place_idx, f"grads_chip_sum_l{i}_{k}")
        all_homes += homes
    grads.update(zip([n for n, _ in _BIG], _pair_gather(bufs, all_homes, "grads_pair_gather")))

    delta, new_m, new_v = {"w_cond": d_cond}, {"w_cond": m_cond}, {"w_cond": v_cond}
    for n, _ in _BIG:
        two_d = lambda a: a.reshape(-1, a.shape[-1])
        d, nm, nv = _adamw(two_d(wts[n]), two_d(grads[n]), two_d(mom[n]), two_d(var[n]), "adamw_" + n)
        delta[n], new_m[n], new_v[n] = (a.reshape(wts[n].shape) for a in (d, nm, nv))
    shapes = [wts[n].shape for n in _SMALL]
    packed = [_pack_rows([src[n] for n in _SMALL]) for src in (wts, grads, mom, var)]
    for dst, out in zip((delta, new_m, new_v), _adamw(*packed, "adamw_small")):
        dst.update(zip(_SMALL, _unpack(out.reshape(-1), shapes)))

    return (loss, grad_x[None], *[grads[n] for n in _WEIGHTS], *[delta[n] for n in _WEIGHTS],
            *[new_m[n] for n in _WEIGHTS], *[new_v[n] for n in _WEIGHTS])


def kernel(x, c, w_cond, b_cond, norm_pre, norm_post, w_ffn_in, w_ffn_out, fox_w_in, fox_b_f, fox_w_out, sconv_w_in, sconv_conv_w, sconv_w_out, lru_w_in, lru_conv_w, lru_conv_b, lru_w_a, lru_b_a, lru_w_x, lru_b_x, lru_lambda, lru_w_out, loss_target, m_w_cond, m_b_cond, m_norm_pre, m_norm_post, m_w_ffn_in, m_w_ffn_out, m_fox_w_in, m_fox_b_f, m_fox_w_out, m_sconv_w_in, m_sconv_conv_w, m_sconv_w_out, m_lru_w_in, m_lru_conv_w, m_lru_conv_b, m_lru_w_a, m_lru_b_a, m_lru_w_x, m_lru_b_x, m_lru_lambda, m_lru_w_out, v_w_cond, v_b_cond, v_norm_pre, v_norm_post, v_w_ffn_in, v_w_ffn_out, v_fox_w_in, v_fox_b_f, v_fox_w_out, v_sconv_w_in, v_sconv_conv_w, v_sconv_w_out, v_lru_w_in, v_lru_conv_w, v_lru_conv_b, v_lru_w_a, v_lru_b_a, v_lru_w_x, v_lru_b_x, v_lru_lambda, v_lru_w_out):
    given = dict(locals())
    wts = {n: given[n] for n in _WEIGHTS}
    mom = {n: given["m_" + n] for n in _WEIGHTS}
    var = {n: given["v_" + n] for n in _WEIGHTS}
    return _step(x, c, loss_target, wts, mom, var)
```

```python
import functools
import math
from typing import NamedTuple

import jax
import jax.numpy as jnp
from jax import lax
from jax.experimental import pallas as pl
from jax.experimental.pallas import tpu as pltpu

F32 = jnp.float32
BF16 = jnp.bfloat16

D_MODEL = 1024
DEPTH = 4
N_SUB = 3
D_FF = 2816
RMS_EPS = 1e-6
FOX_HEADS = 16
FOX_HEAD_DIM = 64
FOX_PAD = 3200
LRU_BLOCKS = 16
LRU_BLOCK_DIM = 64
LRU_C = 8.0
N_CHIPS = 4
N_DEV = 8

ADAM_LR = 0.001
ADAM_B1 = 0.9
ADAM_B2 = 0.999
ADAM_EPS = 1e-08
ADAM_WD = 0.01
ADAM_STEP = 10

VMEM_LIMIT_V7X = 56 * 1024 * 1024
ROW_TILE = 256
COL_TILE = 256
ATT_TILE = 256
MM_ROWS = 256


def _cparams(sem=None):
    return pltpu.CompilerParams(vmem_limit_bytes=VMEM_LIMIT_V7X, dimension_semantics=sem)


def _sigmoid(z):
    return 1.0 / (1.0 + jnp.exp(-z))


def _softplus(z):
    return jnp.maximum(z, 0.0) + jnp.log(1.0 + jnp.exp(-jnp.abs(z)))


def _rows_sum(v):
    return jnp.sum(v, axis=0, keepdims=True)


class _W(NamedTuple):
    arr: jax.Array
    prefix: tuple = ()
    blocked: bool = False


def _w_spec(w, block2, pos):
    lead = (None,) * (len(w.prefix) + (1 if w.blocked else 0))
    if w.blocked:
        return pl.BlockSpec(lead + block2, lambda *g: (pos(*g)[0], *w.prefix, pos(*g)[1], pos(*g)[2]))
    return pl.BlockSpec(lead + block2, lambda *g: (*w.prefix, pos(*g)[1], pos(*g)[2]))


def _mm_nn(a, b, name, tn=None):
    m, k = a.shape
    if b.blocked:
        steps, bn = b.arr.shape[0], b.arr.shape[-1]
        b_spec = _w_spec(b, (k, bn), lambda n: (n, 0, 0))
    else:
        n_total = b.arr.shape[-1]
        bn = n_total if tn is None else tn
        steps = n_total // bn
        assert steps * bn == n_total
        b_spec = _w_spec(b, (k, bn), lambda n: (0, 0, n))
    tm = min(MM_ROWS, m)

    def body(a_ref, b_ref, o_ref):
        def step(i, carry):
            r = pl.ds(pl.multiple_of(i * tm, tm), tm)
            o_ref[r, :] = jnp.dot(a_ref[r, :], b_ref[...], preferred_element_type=F32)
            return carry
        lax.fori_loop(0, m // tm, step, 0)

    return pl.pallas_call(
        body, name=name, grid=(steps,),
        in_specs=[pl.BlockSpec((m, k), lambda n: (0, 0)), b_spec],
        out_specs=pl.BlockSpec((m, bn), lambda n: (0, n)),
        out_shape=jax.ShapeDtypeStruct((m, steps * bn), F32),
        compiler_params=_cparams(("arbitrary",)),
    )(a, b.arr)


def _mm_nt(dy, w, name, tk=None, tn=None):
    m, n_total = dy.shape
    k = w.arr.shape[-2]
    if w.blocked:
        bk, bn = k, w.arr.shape[-1]
        grid = (1, w.arr.shape[0])
        w_spec = _w_spec(w, (k, bn), lambda kt, n: (n, 0, 0))
    else:
        bk = k if tk is None else tk
        bn = n_total if tn is None else tn
        grid = (k // bk, n_total // bn)
        assert grid[0] * bk == k and grid[1] * bn == n_total
        w_spec = _w_spec(w, (bk, bn), lambda kt, n: (0, kt, n))
    tm = min(MM_ROWS, m)

    def body(dy_ref, w_ref, o_ref):
        def step(i, carry):
            r = pl.ds(pl.multiple_of(i * tm, tm), tm)
            o_ref[r, :] += lax.dot_general(dy_ref[r, :], w_ref[...], (((1,), (1,)), ((), ())),
                                           preferred_element_type=F32)
            return carry

        @pl.when(pl.program_id(1) == 0)
        def _():
            o_ref[...] = jnp.zeros_like(o_ref)
        lax.fori_loop(0, m // tm, step, 0)

    return pl.pallas_call(
        body, name=name, grid=grid,
        in_specs=[pl.BlockSpec((m, bn), lambda kt, n: (0, n)), w_spec],
        out_specs=pl.BlockSpec((m, bk), lambda kt, n: (0, kt)),
        out_shape=jax.ShapeDtypeStruct((m, k), F32),
        compiler_params=_cparams(("arbitrary", "arbitrary")),
    )(dy, w.arr)


def _mm_tn(x, dy, name, tk=None, tn=None, blocked_out=False):
    s, k = x.shape
    n_total = dy.shape[1]
    bk = k if tk is None else tk
    bn = n_total if tn is None else tn
    grid = (k // bk, n_total // bn)
    assert grid[0] * bk == k and grid[1] * bn == n_total
    ck = 256 if bk % 256 == 0 else 128

    def body(x_ref, dy_ref, o_ref):
        def step(i, carry):
            c = pl.ds(pl.multiple_of(i * ck, ck), ck)
            o_ref[c, :] = lax.dot_general(x_ref[:, c], dy_ref[...], (((0,), (0,)), ((), ())),
                                          preferred_element_type=F32)
            return carry
        lax.fori_loop(0, bk // ck, step, 0)

    if blocked_out:
        assert grid[0] == 1
        out_spec = pl.BlockSpec((None, bk, bn), lambda kt, n: (n, 0, 0))
        out_shape = jax.ShapeDtypeStruct((grid[1], k, bn), F32)
    else:
        out_spec = pl.BlockSpec((bk, bn), lambda kt, n: (kt, n))
        out_shape = jax.ShapeDtypeStruct((k, n_total), F32)
    return pl.pallas_call(
        body, name=name, grid=grid,
        in_specs=[pl.BlockSpec((s, bk), lambda kt, n: (0, kt)), pl.BlockSpec((s, bn), lambda kt, n: (0, n))],
        out_specs=out_spec, out_shape=out_shape,
        compiler_params=_cparams(("arbitrary", "arbitrary")),
    )(x, dy)


def _row_call(name, body, rows, fulls, row_outs, acc_outs, tr=ROW_TILE, after=None):
    s = rows[0].shape[0]
    tr = min(tr, s)
    in_specs = [pl.BlockSpec((tr, a.shape[1]), lambda i: (i, 0)) for a in rows]
    in_specs += [pl.BlockSpec(a.shape, lambda i: (0, 0)) for a in fulls]
    n_in = len(in_specs)
    order = [] if after is None else [after]
    in_specs += [pl.BlockSpec(memory_space=pl.ANY)] * len(order)
    out_specs = [pl.BlockSpec((tr, c), lambda i: (i, 0)) for c, _ in row_outs]
    out_specs += [pl.BlockSpec((1, c), lambda i: (0, 0)) for c, _ in acc_outs]
    out_shape = [jax.ShapeDtypeStruct((s, c), dt) for c, dt in row_outs]
    out_shape += [jax.ShapeDtypeStruct((1, c), dt) for c, dt in acc_outs]
    n_acc = len(acc_outs)

    def wrapped(*refs):
        refs = refs[:n_in] + refs[n_in + len(order):]
        if n_acc:
            @pl.when(pl.program_id(0) == 0)
            def _():
                for r in refs[len(refs) - n_acc:]:
                    r[...] = jnp.zeros_like(r)
        body(*refs)

    return pl.pallas_call(
        wrapped, name=name, grid=(s // tr,), in_specs=in_specs, out_specs=out_specs, out_shape=out_shape,
        compiler_params=_cparams(("arbitrary",)),
    )(*rows, *fulls, *order)


def _rms(v):
    return lax.rsqrt(jnp.mean(v * v, axis=-1, keepdims=True) + RMS_EPS)


def _pre_norm(x, g_pre, scale, shift, name, after=None):
    def body(x_ref, g_ref, sc_ref, sh_ref, h_ref):
        xv = x_ref[...]
        h = (xv * _rms(xv)) * g_ref[...] * (1.0 + sc_ref[...]) + sh_ref[...]
        h_ref[...] = h.astype(BF16)
    return _row_call(name, body, [x], [g_pre, scale, shift], [(D_MODEL, BF16)], [], after=after)[0]


def _post_norm(x, y, g_post, gate, coef, name):
    def body(x_ref, y_ref, g_ref, gate_ref, o_ref):
        yv = y_ref[...]
        o_ref[...] = x_ref[...] + (coef * gate_ref[...]) * ((yv * _rms(yv)) * g_ref[...])
    return _row_call(name, body, [x, y], [g_post, gate], [(D_MODEL, F32)], [])[0]


def _post_norm_bwd(dxo, y, g_post, gate, coef, name, after=None):
    def body(dxo_ref, y_ref, g_ref, gate_ref, dy_ref, dgate_ref, dg_ref):
        yv = y_ref[...]
        r2 = _rms(yv)
        yn = yv * r2
        dxo_v = dxo_ref[...]
        dgate_ref[...] += _rows_sum(dxo_v * (yn * g_ref[...])) * coef
        dz = dxo_v * (coef * gate_ref[...])
        dg_ref[...] += _rows_sum(dz * yn)
        dyn = dz * g_ref[...]
        dy = r2 * (dyn - yn * jnp.mean(dyn * yn, axis=-1, keepdims=True))
        dy_ref[...] = dy.astype(BF16)
    return _row_call(name, body, [dxo, y], [g_post, gate], [(D_MODEL, BF16)], [(D_MODEL, F32), (D_MODEL, F32)],
                     after=after)


def _pre_norm_bwd(dxo, dh, x, g_pre, scale, name):
    def body(dxo_ref, dh_ref, x_ref, g_ref, sc_ref, dx_ref, dshift_ref, dscale_ref, dg_ref):
        xv = x_ref[...]
        r = _rms(xv)
        xn = xv * r
        dh_v = dh_ref[...]
        one_sc = 1.0 + sc_ref[...]
        dshift_ref[...] += _rows_sum(dh_v)
        dscale_ref[...] += _rows_sum(dh_v * (xn * g_ref[...]))
        dg_ref[...] += _rows_sum(dh_v * xn * one_sc)
        dxn = dh_v * (g_ref[...] * one_sc)
        dx_ref[...] = dxo_ref[...] + r * (dxn - xn * jnp.mean(dxn * xn, axis=-1, keepdims=True))
    return _row_call(name, body, [dxo, dh, x], [g_pre, scale], [(D_MODEL, F32)],
                     [(D_MODEL, F32), (D_MODEL, F32), (D_MODEL, F32)])


def _swiglu_act(gu, name):
    def body(gu_ref, a_ref):
        g = gu_ref[:, :D_FF]
        a_ref[...] = (g * _sigmoid(g) * gu_ref[:, D_FF:]).astype(BF16)
    return _row_call(name, body, [gu], [], [(D_FF, BF16)], [])[0]


def _swiglu_act_bwd(da, gu, name):
    def body(da_ref, gu_ref, dgu_ref):
        g = gu_ref[:, :D_FF]
        sg = _sigmoid(g)
        da_v = da_ref[...]
        dgu_ref[:, :D_FF] = (da_v * gu_ref[:, D_FF:] * (sg * (1.0 + g * (1.0 - sg)))).astype(BF16)
        dgu_ref[:, D_FF:] = (da_v * (g * sg)).astype(BF16)
    return _row_call(name, body, [da, gu], [], [(2 * D_FF, BF16)], [])[0]


def _loss_head(y, target, name):
    def body(y_ref, t_ref, dy_ref, loss_ref):
        e = y_ref[...] - t_ref[...]
        dy_ref[...] = e * (1.0 / D_MODEL)
        part = jnp.sum(jnp.mean(e * e, axis=-1, keepdims=True), axis=0, keepdims=True) * 0.5
        loss_ref[...] += jnp.broadcast_to(part, loss_ref.shape)
    return _row_call(name, body, [y, target], [], [(D_MODEL, F32)], [(128, F32)])


def _lane_scan(v, reverse):
    s = v.shape[1]
    lane = lax.broadcasted_iota(jnp.int32, v.shape, 1)
    d = 1
    while d < s:
        if reverse:
            v = v + jnp.where(lane < s - d, pltpu.roll(v, s - d, 1), 0.0)
        else:
            v = v + jnp.where(lane >= d, pltpu.roll(v, d, 1), 0.0)
        d *= 2
    return v


def _fox_gate(flt, b_f, name):
    def body(f_ref, b_ref, cum_ref):
        z = f_ref[...] + b_ref[...]
        cum_ref[...] = _lane_scan(-_softplus(-z), reverse=False)
    return pl.pallas_call(body, name=name, out_shape=jax.ShapeDtypeStruct(flt.shape, F32),
                          compiler_params=_cparams())(flt, b_f)


def _fox_gate_bwd(dcum_q, dcum_k, flt, b_f, name):
    def body(dq_ref, dk_ref, f_ref, b_ref, df_ref, db_ref):
        z = f_ref[...] + b_ref[...]
        df = _lane_scan(dq_ref[...] + dk_ref[...], reverse=True) * _sigmoid(-z)
        df_ref[...] = df
        db_ref[...] = jnp.sum(df, axis=1, keepdims=True)
    h = flt.shape[0]
    return pl.pallas_call(body, name=name,
                          out_shape=(jax.ShapeDtypeStruct(flt.shape, F32), jax.ShapeDtypeStruct((h, 1), F32)),
                          compiler_params=_cparams())(dcum_q, dcum_k, flt, b_f)


def _pick_head(block, h):
    lane = lax.broadcasted_iota(jnp.int32, block.shape, 1)
    return jnp.sum(jnp.where(lane == h, block, 0.0), axis=1, keepdims=True)


def _put_head(ref, col, h):
    @pl.when(h == 0)
    def _():
        ref[...] = jnp.zeros_like(ref)
    lane = lax.broadcasted_iota(jnp.int32, ref.shape, 1)
    ref[...] = jnp.where(lane == h, col, ref[...])


_NT = (((1,), (1,)), ((), ()))
_FOX_SCALE = FOX_HEAD_DIM ** -0.5


def _causal(s_tile, t):
    row = lax.broadcasted_iota(jnp.int32, (t, t), 0)
    col = lax.broadcasted_iota(jnp.int32, (t, t), 1)
    return jnp.where(col <= row, s_tile, -jnp.inf)


HEAD_PAIRS = FOX_HEADS // 2
PAIR_W = 2 * FOX_HEAD_DIM


def _low_half(shape):
    return lax.broadcasted_iota(jnp.int32, shape, 1) < FOX_HEAD_DIM


def _fox_attn_fwd(qkv, cum, cum_t, name):
    s = qkv.shape[0]
    t = min(ATT_TILE, s)

    def body(q_ref, k_ref, v_ref, cum_ref, cumt_ref, o_ref, ob_ref, lse_ref):
        i = pl.program_id(0)
        hp = pl.program_id(1)
        lo = _low_half((t, PAIR_W))
        qv = q_ref[...]
        zero = jnp.zeros_like(qv)
        q2 = (jnp.where(lo, qv, zero), jnp.where(lo, zero, qv))
        cum_v = cum_ref[...]
        cq2 = (_pick_head(cum_v, 2 * hp), _pick_head(cum_v, 2 * hp + 1))

        def step(j, carry, masked):
            ks = pl.ds(pl.multiple_of(j * t, t), t)
            kj = k_ref[ks, :]
            vj = v_ref[ks, :]
            out = []
            for e in range(2):
                m, l, acc = carry[e]
                sc = lax.dot_general(q2[e], kj, _NT, preferred_element_type=F32) * _FOX_SCALE
                sc = sc + cq2[e] - cumt_ref[e:e + 1, ks]
                if masked:
                    sc = _causal(sc, t)
                m_new = jnp.maximum(m, jnp.max(sc, axis=1, keepdims=True))
                alpha = jnp.exp(m - m_new)
                p = jnp.exp(sc - m_new)
                l = alpha * l + jnp.sum(p, axis=1, keepdims=True)
                acc = alpha * acc + jnp.dot(p.astype(BF16), vj, preferred_element_type=F32)
                out.append((m_new, l, acc))
            return tuple(out)

        one = (jnp.full((t, 1), -jnp.inf, F32), jnp.zeros((t, 1), F32), jnp.zeros((t, PAIR_W), F32))
        carry = lax.fori_loop(0, i, lambda j, c: step(j, c, False), (one, one))
        (m0, l0, a0), (m1, l1, a1) = step(i, carry, True)
        o = jnp.where(lo, a0 / l0, a1 / l1)
        o_ref[...] = o
        ob_ref[...] = o.astype(BF16)
        _put_head(lse_ref, m0 + jnp.log(l0), 2 * hp)
        _put_head(lse_ref, m1 + jnp.log(l1), 2 * hp + 1)

    nat_tile = pl.BlockSpec((t, FOX_HEADS), lambda i, hp: (i, 0))
    out_tile = pl.BlockSpec((t, PAIR_W), lambda i, hp: (i, hp))
    return pl.pallas_call(
        body, name=name, grid=(s // t, HEAD_PAIRS),
        in_specs=[pl.BlockSpec((t, PAIR_W), lambda i, hp: (i, hp)),
                  pl.BlockSpec((s, PAIR_W), lambda i, hp: (0, HEAD_PAIRS + hp)),
                  pl.BlockSpec((s, PAIR_W), lambda i, hp: (0, 2 * HEAD_PAIRS + hp)),
                  nat_tile, pl.BlockSpec((None, 2, s), lambda i, hp: (hp, 0, 0))],
        out_specs=[out_tile, out_tile, nat_tile],
        out_shape=[jax.ShapeDtypeStruct((s, D_MODEL), F32), jax.ShapeDtypeStruct((s, D_MODEL), BF16),
                   jax.ShapeDtypeStruct((s, FOX_HEADS), F32)],
        compiler_params=_cparams(("arbitrary", "arbitrary")),
    )(qkv, qkv, qkv, cum, cum_t)


def _fox_delta(do, o, expand, name):
    def body(do_ref, o_ref, e_ref, d_ref):
        prod = do_ref[...] * o_ref[...]
        hi = prod.astype(BF16)
        lo = (prod - hi.astype(F32)).astype(BF16)
        tot = (jnp.dot(hi, e_ref[...], preferred_element_type=F32)
               + jnp.dot(lo, e_ref[...], preferred_element_type=F32))
        d_ref[...] = tot[:, :FOX_HEADS]
    return _row_call(name, body, [do, o], [expand], [(FOX_HEADS, F32)], [])[0]


def _fox_attn_bwd(qkv, do, cum, cum_t, lse_t, delta_t, name):
    s = qkv.shape[0]
    t = min(ATT_TILE, s)
    nq = s // t
    tn_dims = (((0,), (0,)), ((), ()))

    def body(q_ref, k_ref, v_ref, do_ref, cum_ref, cumt_ref, lset_ref, deltat_ref,
             dq_ref, dk_ref, dv_ref, dck_ref, dcq_ref):
        hp = pl.program_id(0)
        j = pl.program_id(1)

        @pl.when(j == 0)
        def _():
            dq_ref[...] = jnp.zeros_like(dq_ref)
            dcq_ref[...] = jnp.zeros_like(dcq_ref)
        dk_ref[...] = jnp.zeros_like(dk_ref)
        dv_ref[...] = jnp.zeros_like(dv_ref)

        lo = _low_half((t, PAIR_W))
        lane = lax.broadcasted_iota(jnp.int32, (t, PAIR_W), 1)
        kv = k_ref[...]
        vv = v_ref[...]
        zero = jnp.zeros_like(kv)
        k2 = (jnp.where(lo, kv, zero), jnp.where(lo, zero, kv))
        v2 = (jnp.where(lo, vv, zero), jnp.where(lo, zero, vv))
        cum_v = cum_ref[...]
        ck2 = (_pick_head(cum_v, 2 * hp), _pick_head(cum_v, 2 * hp + 1))

        def step(i, dck, masked):
            qs = pl.ds(pl.multiple_of(i * t, t), t)
            qi = q_ref[qs, :]
            do_i = do_ref[qs, :].astype(BF16)
            dv_p, dk_p, dq_p = [], [], []
            for e in range(2):
                st = lax.dot_general(k2[e], qi, _NT, preferred_element_type=F32) * _FOX_SCALE
                st = st + cumt_ref[e:e + 1, qs] - ck2[e]
                if masked:
                    row = lax.broadcasted_iota(jnp.int32, (t, t), 0)
                    col = lax.broadcasted_iota(jnp.int32, (t, t), 1)
                    st = jnp.where(row <= col, st, -jnp.inf)
                pt = jnp.exp(st - lset_ref[e:e + 1, qs])
                dv_p.append(jnp.dot(pt.astype(BF16), do_i, preferred_element_type=F32))
                dpt = lax.dot_general(v2[e], do_i, _NT, preferred_element_type=F32)
                dst = pt * (dpt - deltat_ref[e:e + 1, qs])
                dsb = dst.astype(BF16)
                dk_p.append(jnp.dot(dsb, qi, preferred_element_type=F32))
                dq_p.append(lax.dot_general(dsb, kv, tn_dims, preferred_element_type=F32))
                dck = dck - jnp.where(lane == e, jnp.sum(dst, axis=1, keepdims=True), 0.0)
                dcq_ref[e:e + 1, qs] += jnp.sum(dst, axis=0, keepdims=True)
            dv_ref[...] += jnp.where(lo, dv_p[0], dv_p[1])
            dk_ref[...] += jnp.where(lo, dk_p[0], dk_p[1])
            dq_ref[qs, :] += jnp.where(lo, dq_p[0], dq_p[1]) * _FOX_SCALE
            return dck

        dck = step(j, jnp.zeros((t, PAIR_W), F32), True)
        dck = lax.fori_loop(j + 1, nq, lambda i, c: step(i, c, False), dck)
        dk_ref[...] = dk_ref[...] * _FOX_SCALE
        dck_ref[...] = dck

    pair_full = lambda part: pl.BlockSpec((s, PAIR_W), lambda hp, j: (0, part * HEAD_PAIRS + hp))
    pair_tile = lambda part: pl.BlockSpec((t, PAIR_W), lambda hp, j: (j, part * HEAD_PAIRS + hp))
    rows = pl.BlockSpec((None, 2, s), lambda hp, j: (hp, 0, 0))
    return pl.pallas_call(
        body, name=name, grid=(HEAD_PAIRS, nq),
        in_specs=[pair_full(0), pair_tile(1), pair_tile(2), pair_full(0),
                  pl.BlockSpec((t, FOX_HEADS), lambda hp, j: (j, 0)), rows, rows, rows],
        out_specs=[pair_full(0), pair_tile(0), pair_tile(0),
                   pl.BlockSpec((None, t, PAIR_W), lambda hp, j: (hp, j, 0)), rows],
        out_shape=[jax.ShapeDtypeStruct((s, D_MODEL), F32)] * 3
        + [jax.ShapeDtypeStruct((HEAD_PAIRS, s, PAIR_W), F32), jax.ShapeDtypeStruct((HEAD_PAIRS, 2, s), F32)],
        compiler_params=_cparams(("arbitrary", "arbitrary")),
    )(qkv, qkv, qkv, do, cum, cum_t, lse_t, delta_t)


def _shift_down(v, d):
    row = lax.broadcasted_iota(jnp.int32, v.shape, 0)
    return jnp.where(row >= d, pltpu.roll(v, d, 0), 0.0)


def _shift_up(v, d):
    s = v.shape[0]
    row = lax.broadcasted_iota(jnp.int32, v.shape, 0)
    return jnp.where(row < s - d, pltpu.roll(v, s - d, 0), 0.0)


def _conv_taps(v, cw_ref, width):
    out = cw_ref[width - 1:width, :] * v
    for k in range(width - 1):
        out = out + cw_ref[k:k + 1, :] * _shift_down(v, width - 1 - k)
    return out


def _conv_taps_bwd(dout, v, cw_ref, dcw_ref, width):
    dv = cw_ref[width - 1:width, :] * dout
    dcw_ref[width - 1:width, :] = _rows_sum(dout * v)
    for k in range(width - 1):
        d = width - 1 - k
        dv = dv + cw_ref[k:k + 1, :] * _shift_up(dout, d)
        dcw_ref[k:k + 1, :] = _rows_sum(dout * _shift_down(v, d))
    return dv


def _col_spec(s, tc, part=0):
    off = part * (D_MODEL // tc)
    return pl.BlockSpec((s, tc), lambda c: (0, c + off))


def _small_spec(rows, tc):
    return pl.BlockSpec((rows, tc), lambda c: (0, c))


def _col_call(name, body, in_arrays, in_specs, out_rows, s, tc):
    return pl.pallas_call(
        body, name=name, grid=(D_MODEL // tc,), in_specs=in_specs,
        out_specs=[pl.BlockSpec((r, tc), lambda c: (0, c)) for r, _ in out_rows],
        out_shape=[jax.ShapeDtypeStruct((r, D_MODEL), dt) for r, dt in out_rows],
        compiler_params=_cparams(("arbitrary",)),
    )(*in_arrays)


def _sconv_fwd(proj, conv_w, name):
    s = proj.shape[0]
    tc = COL_TILE

    def body(b_ref, c_ref, x_ref, cw_ref, y_ref):
        y_ref[...] = (b_ref[...] * _conv_taps(c_ref[...] * x_ref[...], cw_ref, 3)).astype(BF16)

    return _col_call(name, body, [proj, proj, proj, conv_w],
                     [_col_spec(s, tc, 0), _col_spec(s, tc, 1), _col_spec(s, tc, 2), _small_spec(3, tc)],
                     [(s, BF16)], s, tc)[0]


def _sconv_bwd(dy, proj, conv_w, name):
    s = proj.shape[0]
    tc = COL_TILE

    def body(dy_ref, b_ref, c_ref, x_ref, cw_ref, db_ref, dc_ref, dx_ref, dcw_ref):
        w = c_ref[...] * x_ref[...]
        dy_v = dy_ref[...]
        db_ref[...] = (dy_v * _conv_taps(w, cw_ref, 3)).astype(BF16)
        dw = _conv_taps_bwd(dy_v * b_ref[...], w, cw_ref, dcw_ref, 3)
        dc_ref[...] = (dw * x_ref[...]).astype(BF16)
        dx_ref[...] = (dw * c_ref[...]).astype(BF16)

    return _col_call(name, body, [dy, proj, proj, proj, conv_w],
                     [_col_spec(s, tc), _col_spec(s, tc, 0), _col_spec(s, tc, 1), _col_spec(s, tc, 2),
                      _small_spec(3, tc)],
                     [(s, BF16), (s, BF16), (s, BF16), (3, F32)], s, tc)


def _lru_conv(proj, conv_w, conv_b, name):
    s = proj.shape[0]
    tc = COL_TILE

    def body(x_ref, cw_ref, cb_ref, xb_ref, xbb_ref):
        xb = _conv_taps(x_ref[...], cw_ref, 4) + cb_ref[...]
        xb_ref[...] = xb
        xbb_ref[...] = xb.astype(BF16)

    return _col_call(name, body, [proj, conv_w, conv_b],
                     [_col_spec(s, tc, 1), _small_spec(4, tc), _small_spec(1, tc)],
                     [(s, F32), (s, BF16)], s, tc)


def _lru_conv_bwd(dxb1, dxb2, proj, conv_w, name):
    s = proj.shape[0]
    tc = COL_TILE

    def body(d1_ref, d2_ref, x_ref, cw_ref, dx_ref, dcw_ref, dcb_ref):
        dxb = d1_ref[...] + d2_ref[...]
        dcb_ref[...] = _rows_sum(dxb)
        dx_ref[...] = _conv_taps_bwd(dxb, x_ref[...], cw_ref, dcw_ref, 4).astype(BF16)

    return _col_call(name, body, [dxb1, dxb2, proj, conv_w],
                     [_col_spec(s, tc), _col_spec(s, tc), _col_spec(s, tc, 1), _small_spec(4, tc)],
                     [(s, BF16), (4, F32), (1, F32)], s, tc)


_GELU_C = math.sqrt(2.0 / math.pi)


def _gelu_parts(g):
    inner = _GELU_C * (g + 0.044715 * g * g * g)
    th = jnp.tanh(inner)
    val = 0.5 * g * (1.0 + th)
    der = 0.5 * (1.0 + th) + 0.5 * g * (1.0 - th * th) * (_GELU_C * (1.0 + 3.0 * 0.044715 * g * g))
    return val, der


def _lru_gates(pa_ref, px_ref, ba_ref, bx_ref, lam_ref):
    r = _sigmoid(pa_ref[...] + ba_ref[...])
    ig = _sigmoid(px_ref[...] + bx_ref[...])
    sp = _softplus(-lam_ref[...])
    log_a = (-LRU_C) * r * sp
    a = jnp.exp(log_a)
    z = 2.0 * log_a
    one_m_a2 = jnp.where(z > -1e-3, -(z * (1.0 + z * (0.5 + z * (1.0 / 6.0)))), 1.0 - jnp.exp(z))
    return r, ig, sp, a, jnp.sqrt(one_m_a2)


def _lru_scan(pre, xb, proj, b_a, b_x, lam, name):
    s = xb.shape[0]
    tc = COL_TILE

    def body(pa_ref, px_ref, xb_ref, g_ref, ba_ref, bx_ref, lam_ref, y_ref, hs_ref):
        _, ig, _, a, mult = _lru_gates(pa_ref, px_ref, ba_ref, bx_ref, lam_ref)
        b = mult * (ig * xb_ref[...])
        d = 1
        while d < s:
            row = lax.broadcasted_iota(jnp.int32, a.shape, 0)
            keep = row >= d
            b = b + a * jnp.where(keep, pltpu.roll(b, d, 0), 0.0)
            a = a * jnp.where(keep, pltpu.roll(a, d, 0), 1.0)
            d *= 2
        hs_ref[...] = b
        y_ref[...] = (b * _gelu_parts(g_ref[...])[0]).astype(BF16)

    return _col_call(name, body, [pre, pre, xb, proj, b_a, b_x, lam],
                     [_col_spec(s, tc, 0), _col_spec(s, tc, 1), _col_spec(s, tc), _col_spec(s, tc, 0),
                      _small_spec(1, tc), _small_spec(1, tc), _small_spec(1, tc)],
                     [(s, BF16), (s, F32)], s, tc)


def _lru_scan_bwd(dy, pre, xb, proj, hs, b_a, b_x, lam, name):
    s = xb.shape[0]
    tc = COL_TILE

    def body(dy_ref, pa_ref, px_ref, xb_ref, g_ref, hs_ref, ba_ref, bx_ref, lam_ref,
             dg_ref, dpa_ref, dpx_ref, dxb_ref, dba_ref, dbx_ref, dlam_ref):
        r, ig, sp, a, mult = _lru_gates(pa_ref, px_ref, ba_ref, bx_ref, lam_ref)
        gl, gl_der = _gelu_parts(g_ref[...])
        dy_v = dy_ref[...]
        hs_v = hs_ref[...]
        dg_ref[...] = (dy_v * hs_v * gl_der).astype(BF16)
        lam_t = dy_v * gl
        coef = _shift_up(a, 1)
        d = 1
        while d < s:
            row = lax.broadcasted_iota(jnp.int32, coef.shape, 0)
            keep = row < s - d
            lam_t = lam_t + coef * jnp.where(keep, pltpu.roll(lam_t, s - d, 0), 0.0)
            coef = coef * jnp.where(keep, pltpu.roll(coef, s - d, 0), 1.0)
            d *= 2
        xb_v = xb_ref[...]
        da = lam_t * _shift_down(hs_v, 1)
        dmult = lam_t * (ig * xb_v)
        dig = lam_t * mult * xb_v
        dxb_ref[...] = lam_t * mult * ig
        dlog_a = da * a - dmult * (a * a) / mult
        dr = dlog_a * ((-LRU_C) * sp)
        dsp = _rows_sum(dlog_a * ((-LRU_C) * r))
        dlam_ref[...] = -dsp * _sigmoid(-lam_ref[...])
        dpa = dr * r * (1.0 - r)
        dpx = dig * ig * (1.0 - ig)
        dba_ref[...] = _rows_sum(dpa)
        dbx_ref[...] = _rows_sum(dpx)
        dpa_ref[...] = dpa.astype(BF16)
        dpx_ref[...] = dpx.astype(BF16)

    return _col_call(name, body, [dy, pre, pre, xb, proj, hs, b_a, b_x, lam],
                     [_col_spec(s, tc), _col_spec(s, tc, 0), _col_spec(s, tc, 1), _col_spec(s, tc),
                      _col_spec(s, tc, 0), _col_spec(s, tc),
                      _small_spec(1, tc), _small_spec(1, tc), _small_spec(1, tc)],
                     [(s, BF16), (s, BF16), (s, BF16), (s, F32), (1, F32), (1, F32), (1, F32)], s, tc)


def _ffn_fwd(x, w_in, w_out, g_pre, g_post, shift, scale, gate, tag, after=None):
    h = _pre_norm(x, g_pre, scale, shift, tag + "_pre", after=after)
    gu = _mm_nn(h, w_in, tag + "_in")
    a = _swiglu_act(gu, tag + "_act")
    y = _mm_nn(a, w_out, tag + "_out", tn=512)
    xo = _post_norm(x, y, g_post, gate, 0.5, tag + "_post")
    return xo, (x, h, gu, a, y)


def _ffn_bwd(dxo, saved, w_in, w_out, g_pre, g_post, scale, gate, tag, after=None):
    x, h, gu, a, y = saved
    dy, dgate, dg_post = _post_norm_bwd(dxo, y, g_post, gate, 0.5, tag + "_post_b", after=after)
    da = _mm_nt(dy, w_out, tag + "_out_bx", tk=D_FF // 2)
    dw_out = _mm_tn(a, dy, tag + "_out_bw", tk=D_FF // 2)
    dgu = _swiglu_act_bwd(da, gu, tag + "_act_b")
    dh = _mm_nt(dgu, w_in, tag + "_in_bx")
    dw_in = _mm_tn(h, dgu, tag + "_in_bw", tn=w_in.arr.shape[-1], blocked_out=True)
    dx, dshift, dscale, dg_pre = _pre_norm_bwd(dxo, dh, x, g_pre, scale, tag + "_pre_b")
    return dx, dw_in, dw_out, (dshift, dscale, dgate), dg_pre, dg_post


def _pair_rows(v):
    return v.T.reshape(HEAD_PAIRS, 2, v.shape[0])


def _fox_fwd(h, p, tag):
    s = h.shape[0]
    proj = _mm_nn(h, p["w_in"], tag + "_in", tn=640)
    qkv = proj[:, :3 * D_MODEL].astype(BF16)
    flt = proj[:, 3 * D_MODEL:3 * D_MODEL + FOX_HEADS].T
    cum_t = _fox_gate(flt, p["b_f"], tag + "_gate")
    cum = cum_t.T
    cum_t2 = cum_t.reshape(HEAD_PAIRS, 2, s)
    o, ob, lse = _fox_attn_fwd(qkv, cum, cum_t2, tag + "_attn")
    y = _mm_nn(ob, p["w_out"], tag + "_out")
    return y, (qkv, flt, cum, cum_t2, o, ob, lse)


def _fox_bwd(dy, h, saved, p, tag):
    qkv, flt, cum, cum_t2, o, ob, lse = saved
    s = h.shape[0]
    do = _mm_nt(dy, p["w_out"], tag + "_out_bx")
    dw_out = _mm_tn(ob, dy, tag + "_out_bw")
    expand = jnp.pad(jnp.repeat(jnp.eye(FOX_HEADS, dtype=BF16), FOX_HEAD_DIM, axis=0),
                     ((0, 0), (0, PAIR_W - FOX_HEADS)))
    delta = _fox_delta(do, o, expand, tag + "_attn_delta")
    dq, dk, dv, dck, dcq = _fox_attn_bwd(qkv, do, cum, cum_t2, _pair_rows(lse), _pair_rows(delta), tag + "_attn_b")
    dcum_k = dck[:, :, :2].transpose(0, 2, 1).reshape(FOX_HEADS, s)
    dflt, db_f = _fox_gate_bwd(dcq.reshape(FOX_HEADS, s), dcum_k, flt, p["b_f"], tag + "_gate_b")
    dproj = jnp.concatenate(
        [dq, dk, dv, dflt.T, jnp.zeros((s, FOX_PAD - 3 * D_MODEL - FOX_HEADS), F32)], axis=1).astype(BF16)
    dh = _mm_nt(dproj, p["w_in"], tag + "_in_bx", tn=640)
    dw_in = _mm_tn(h, dproj, tag + "_in_bw", tn=640)
    return dh, {"w_in": dw_in, "w_out": dw_out, "b_f": db_f}


def _sconv_mix_fwd(h, p, tag):
    proj = _mm_nn(h, p["w_in"], tag + "_in")
    yb = _sconv_fwd(proj, p["conv_w"], tag + "_conv")
    y = _mm_nn(yb, p["w_out"], tag + "_out")
    return y, (proj, yb)


def _sconv_mix_bwd(dy, h, saved, p, tag):
    proj, yb = saved
    dyb = _mm_nt(dy, p["w_out"], tag + "_out_bx")
    dw_out = _mm_tn(yb, dy, tag + "_out_bw")
    db, dc, dxv, dcw = _sconv_bwd(dyb, proj, p["conv_w"], tag + "_conv_b")
    dproj = jnp.concatenate([db, dc, dxv], axis=1)
    dh = _mm_nt(dproj, p["w_in"], tag + "_in_bx")
    dw_in = _mm_tn(h, dproj, tag + "_in_bw", tn=p["w_in"].arr.shape[-1], blocked_out=True)
    return dh, {"w_in": dw_in, "w_out": dw_out, "conv_w": dcw}


def _lru_mix_fwd(h, p, tag):
    proj = _mm_nn(h, p["w_in"], tag + "_in")
    xb, xbb = _lru_conv(proj, p["conv_w"], p["conv_b"], tag + "_conv")
    pre = _mm_nn(xbb, p["w_ax"], tag + "_gates", tn=D_MODEL)
    yb, hs = _lru_scan(pre, xb, proj, p["b_a"], p["b_x"], p["lam"], tag + "_scan")
    y = _mm_nn(yb, p["w_out"], tag + "_out")
    return y, (proj, xb, xbb, pre, yb, hs)


def _diag_blocks(m):
    return jnp.stack([m[LRU_BLOCK_DIM * n:LRU_BLOCK_DIM * (n + 1), LRU_BLOCK_DIM * n:LRU_BLOCK_DIM * (n + 1)]
                      for n in range(LRU_BLOCKS)])


def _lru_mix_bwd(dy, h, saved, p, tag):
    proj, xb, xbb, pre, yb, hs = saved
    dyb = _mm_nt(dy, p["w_out"], tag + "_out_bx")
    dw_out = _mm_tn(yb, dy, tag + "_out_bw")
    dg, dpa, dpx, dxb1, dba, dbx, dlam = _lru_scan_bwd(dyb, pre, xb, proj, hs, p["b_a"], p["b_x"], p["lam"],
                                                       tag + "_scan_b")
    dpre = jnp.concatenate([dpa, dpx], axis=1)
    dxb2 = _mm_nt(dpre, p["w_ax"], tag + "_gates_bx", tn=D_MODEL)
    dw_ax = _mm_tn(xbb, dpre, tag + "_gates_bw", tn=D_MODEL)
    dx0, dcw, dcb = _lru_conv_bwd(dxb1, dxb2, proj, p["conv_w"], tag + "_conv_b")
    dproj = jnp.concatenate([dg, dx0], axis=1)
    dh = _mm_nt(dproj, p["w_in"], tag + "_in_bx")
    dw_in = _mm_tn(h, dproj, tag + "_in_bw", tn=p["w_in"].arr.shape[-1], blocked_out=True)
    grads = {"w_in": dw_in, "w_out": dw_out, "conv_w": dcw, "conv_b": dcb,
             "w_a": _diag_blocks(dw_ax[:, :D_MODEL]), "w_x": _diag_blocks(dw_ax[:, D_MODEL:]),
             "b_a": dba, "b_x": dbx, "lam": dlam}
    return dh, grads


_MIXERS = ((_fox_fwd, _fox_bwd), (_sconv_mix_fwd, _sconv_mix_bwd), (_lru_mix_fwd, _lru_mix_bwd))


def _local_step(x, target, mod, layer_params, on_grads=None, on_mid=None, first_after=None):
    layers = []
    tape = []
    for i in range(DEPTH):
        lp = dict(layer_params(i, 0, x))
        layers.append(lp)
        row = lambda v: v[None, :]
        m = lambda sub, what: mod[i, sub, what][None, :]
        x, sv0 = _ffn_fwd(x, lp["ffn_in"][0], lp["ffn_out"][0], row(lp["norm_pre"][0]), row(lp["norm_post"][0]),
                          m(0, 0), m(0, 1), m(0, 2), f"l{i}_ffn0", after=first_after if i == 0 else None)
        lp.update(layer_params(i, 1, x))
        h = _pre_norm(x, row(lp["norm_pre"][1]), m(1, 1), m(1, 0), f"l{i}_mix_pre")
        y, svm = _MIXERS[i % 3][0](h, lp["mixer"], f"l{i}_mix")
        x1 = _post_norm(x, y, row(lp["norm_post"][1]), m(1, 2), 1.0, f"l{i}_mix_post")
        second = layer_params(i, 2, x1)
        lp["ffn_in"] = lp["ffn_in"] + second["ffn_in"]
        lp["ffn_out"] = lp["ffn_out"] + second["ffn_out"]
        x2, sv2 = _ffn_fwd(x1, lp["ffn_in"][1], lp["ffn_out"][1], row(lp["norm_pre"][2]), row(lp["norm_post"][2]),
                           m(2, 0), m(2, 1), m(2, 2), f"l{i}_ffn1")
        tape.append((sv0, (x, h, y, svm), sv2))
        x = x2
    dx, loss_row = _loss_head(x, target, "loss_head")

    layer_grads = [None] * DEPTH
    dmod = [None] * DEPTH
    after = None
    for i in reversed(range(DEPTH)):
        lp = layers[i]
        row = lambda v: v[None, :]
        m = lambda sub, what: mod[i, sub, what][None, :]
        sv0, (xm, h, y, svm), sv2 = tape[i]
        dx, dw_in1, dw_out1, dm2, dgp2, dgq2 = _ffn_bwd(dx, sv2, lp["ffn_in"][1], lp["ffn_out"][1],
                                                        row(lp["norm_pre"][2]), row(lp["norm_post"][2]),
                                                        m(2, 1), m(2, 2), f"l{i}_ffn1", after=after)
        after = on_mid(i, dx) if on_mid is not None else None
        dy, dgate1, dgq1 = _post_norm_bwd(dx, y, row(lp["norm_post"][1]), m(1, 2), 1.0, f"l{i}_mix_post_b", after=after)
        dh, mg = _MIXERS[i % 3][1](dy, h, svm, lp["mixer"], f"l{i}_mix")
        dx, dshift1, dscale1, dgp1 = _pre_norm_bwd(dx, dh, xm, row(lp["norm_pre"][1]), m(1, 1), f"l{i}_mix_pre_b")
        dx, dw_in0, dw_out0, dm0, dgp0, dgq0 = _ffn_bwd(dx, sv0, lp["ffn_in"][0], lp["ffn_out"][0],
                                                        row(lp["norm_pre"][0]), row(lp["norm_post"][0]),
                                                        m(0, 1), m(0, 2), f"l{i}_ffn0")
        dmod[i] = jnp.concatenate([*dm0, dshift1, dscale1, dgate1, *dm2], axis=0).reshape(N_SUB, 3, D_MODEL)
        layer_grads[i] = {"ffn_in": (dw_in0, dw_in1), "ffn_out": (dw_out0, dw_out1),
                          "norm_pre": jnp.concatenate([dgp0, dgp1, dgp2], axis=0),
                          "norm_post": jnp.concatenate([dgq0, dgq1, dgq2], axis=0), "mixer": mg}
        if on_grads is not None:
            after = on_grads(i, layer_grads[i], dx)
    return loss_row, dx, jnp.stack(dmod), layer_grads


COND_ROWS = 16
COND_PAD = 128


def _cond_fwd(c_pad, w_cond, b_shard, name):
    nl, d, n = w_cond.shape
    tn = 768

    def body(c_ref, w_ref, b_ref, o_ref):
        cv = c_ref[...]
        act = (cv * _sigmoid(cv)).astype(BF16)
        o_ref[...] = jnp.dot(act, w_ref[...].astype(BF16), preferred_element_type=F32) + b_ref[...]

    return pl.pallas_call(
        body, name=name, grid=(nl, n // tn),
        in_specs=[pl.BlockSpec((COND_ROWS, d), lambda i, j: (0, 0)),
                  pl.BlockSpec((None, d, tn), lambda i, j: (i, 0, j)),
                  pl.BlockSpec((None, 1, tn), lambda i, j: (i, 0, j))],
        out_specs=pl.BlockSpec((None, COND_ROWS, tn), lambda i, j: (i, 0, j)),
        out_shape=jax.ShapeDtypeStruct((nl, COND_ROWS, n), F32),
        compiler_params=_cparams(("arbitrary", "arbitrary")),
    )(c_pad, w_cond, b_shard)


def _adam_math(w, g, m, v):
    nm = ADAM_B1 * m + (1.0 - ADAM_B1) * g
    nv = ADAM_B2 * v + (1.0 - ADAM_B2) * (g * g)
    m_hat = nm / (1.0 - ADAM_B1 ** ADAM_STEP)
    v_hat = nv / (1.0 - ADAM_B2 ** ADAM_STEP)
    delta = (-ADAM_LR) * (m_hat / (jnp.sqrt(v_hat) + ADAM_EPS) + ADAM_WD * w)
    return delta, nm, nv


def _cond_bwd_adamw(c_t, dmod_s, w, m, v, name):
    nl, d, n = w.shape
    tn = 384
    blk = pl.BlockSpec((None, d, tn), lambda i, j: (i, 0, j))

    def body(c_ref, dm_ref, w_ref, m_ref, v_ref, g_ref, d_ref, nm_ref, nv_ref):
        cv = c_ref[...]
        g = jnp.dot((cv * _sigmoid(cv)).astype(BF16), dm_ref[...], preferred_element_type=F32)
        g_ref[...] = g
        d_ref[...], nm_ref[...], nv_ref[...] = _adam_math(w_ref[...], g, m_ref[...], v_ref[...])

    return pl.pallas_call(
        body, name=name, grid=(nl, n // tn),
        in_specs=[pl.BlockSpec((d, COND_PAD), lambda i, j: (0, 0)),
                  pl.BlockSpec((None, COND_PAD, tn), lambda i, j: (i, 0, j)), blk, blk, blk],
        out_specs=[blk] * 4, out_shape=[jax.ShapeDtypeStruct(w.shape, F32)] * 4,
        compiler_params=_cparams(("arbitrary", "arbitrary")),
    )(c_t, dmod_s, w, m, v)


def _adamw(w, g, m, v, name):
    rows, cols = w.shape
    tr = next(t for t in (256, 176, 128, 64, 32, 16, 8) if rows % t == 0)
    blk = pl.BlockSpec((tr, cols), lambda i: (i, 0))

    def body(w_ref, g_ref, m_ref, v_ref, d_ref, nm_ref, nv_ref):
        d_ref[...], nm_ref[...], nv_ref[...] = _adam_math(w_ref[...], g_ref[...], m_ref[...], v_ref[...])

    return pl.pallas_call(
        body, name=name, grid=(rows // tr,), in_specs=[blk] * 4, out_specs=[blk] * 3,
        out_shape=[jax.ShapeDtypeStruct(w.shape, F32)] * 3, compiler_params=_cparams(("arbitrary",)),
    )(w, g, m, v)


_MESH = pl.DeviceIdType.MESH
_ANY = pl.BlockSpec(memory_space=pl.ANY)


def _place():
    return lax.axis_index("x"), lax.axis_index("y"), lax.axis_index("c")


def _other_chips(x, y):
    return [(1 - x, y), (x, 1 - y), (1 - x, 1 - y)]


def _allgather8(block, name):
    m_per, n = block.shape

    def body(x_ref, out_ref, send_sems, recv_sems, local_sem):
        x, y, c = _place()
        me, sibling = (x, y, c), (x, y, 1 - c)
        chips = _other_chips(x, y)

        def rows(px, py, pc):
            return out_ref.at[pl.ds((4 * px + 2 * py + pc) * m_per, m_per), :]

        def copy(k, blk, to, src=None):
            return pltpu.make_async_remote_copy(
                src_ref=rows(*blk) if src is None else src, dst_ref=rows(*blk),
                send_sem=send_sems.at[k], recv_sem=recv_sems.at[k], device_id=to, device_id_type=_MESH)

        mine = pltpu.make_async_copy(x_ref, rows(*me), local_sem)
        mine.start()
        first = [copy(0, me, sibling, src=x_ref)]
        first += [copy(1 + j, me, (*chip, c), src=x_ref) for j, chip in enumerate(chips)]
        for cp in first:
            cp.start()
        passed = [copy(4 + j, (*chip, c), sibling) for j, chip in enumerate(chips)]
        for j, chip in enumerate(chips):
            copy(1 + j, (*chip, c), me).wait_recv()
            passed[j].start()
        copy(0, sibling, me).wait_recv()
        for j, chip in enumerate(chips):
            copy(4 + j, (*chip, 1 - c), me).wait_recv()
        for cp in first + passed:
            cp.wait_send()
        mine.wait()

    return pl.pallas_call(
        body, name=name, out_shape=jax.ShapeDtypeStruct((N_DEV * m_per, n), block.dtype),
        in_specs=[pl.BlockSpec(memory_space=pltpu.VMEM)], out_specs=pl.BlockSpec(memory_space=pltpu.VMEM),
        scratch_shapes=[pltpu.SemaphoreType.DMA((7,)), pltpu.SemaphoreType.DMA((7,)), pltpu.SemaphoreType.DMA],
        compiler_params=_cparams(),
    )(block)


def _split_axis(shape):
    return next(a for a, n in enumerate(shape) if n > 1)


_HBM = pl.BlockSpec(memory_space=pltpu.HBM)
_SEM = pl.BlockSpec(memory_space=pltpu.SEMAPHORE)
_SPLIT_COPY = pltpu.CompilerParams(has_side_effects=pltpu.SideEffectType.DATAFLOW_SIDE_EFFECTING)
_TOKEN = jax.ShapeDtypeStruct((8, 128), F32)


def _in_hbm(arrays):
    return [pltpu.with_memory_space_constraint(a, pltpu.HBM) for a in arrays]


class _Gathered(NamedTuple):
    shard_shape: tuple
    chip_axis: int

    @property
    def shape(self):
        return self.shard_shape[:self.chip_axis] + (N_CHIPS,) + self.shard_shape[self.chip_axis:]

    def half(self, ref, chip, pc):
        cut = _split_axis(self.shard_shape)
        n = self.shard_shape[cut] // 2
        idx = [slice(None)] * len(self.shard_shape)
        idx[cut] = pl.ds(pc * n, n)
        idx.insert(self.chip_axis, chip)
        return ref.at[tuple(idx)]


def _own_block_placed(shard, layout, chip):
    return lax.dynamic_update_slice_in_dim(lax.empty(layout.shape, shard.dtype),
                                           jnp.expand_dims(shard, layout.chip_axis), chip, axis=layout.chip_axis)


def _gather_copies(lands, layouts, send_sems, recv_sems):
    x, y, c = _place()
    out = []
    for t, (land, lay) in enumerate(zip(lands, layouts)):
        for j, (px, py) in enumerate(_other_chips(x, y)):
            def copy(chip, t=t, j=j, px=px, py=py, land=land, lay=lay):
                return pltpu.make_async_remote_copy(
                    src_ref=lay.half(land, chip, c), dst_ref=lay.half(land, chip, c),
                    send_sem=send_sems.at[3 * t + j], recv_sem=recv_sems.at[3 * t + j],
                    device_id=(px, py, c), device_id_type=_MESH)
            out.append((copy(2 * x + y), copy(2 * px + py)))
    return out


def _gather_start(lands, layouts, after, name):
    nt = len(lands)
    order = [] if after is None else [after]

    def body(*refs):
        land_refs = refs[:nt]
        send_sems, recv_sems = refs[nt + len(order):nt + len(order) + 2]
        token = refs[-1]
        for send, _ in _gather_copies(land_refs, layouts, send_sems, recv_sems):
            send.start()
        token[...] = jnp.zeros_like(token)

    out = pl.pallas_call(
        body, name=name,
        out_shape=(pltpu.SemaphoreType.DMA((3 * nt,)), pltpu.SemaphoreType.DMA((3 * nt,)),
                   *[pltpu.HBM(a.shape, a.dtype) for a in lands], _TOKEN),
        in_specs=[_HBM] * nt + [_ANY] * len(order),
        out_specs=(_SEM, _SEM, *[_HBM] * nt, pl.BlockSpec(memory_space=pltpu.VMEM)),
        input_output_aliases={t: 2 + t for t in range(nt)}, compiler_params=_SPLIT_COPY,
    )(*_in_hbm(lands), *order)
    return out[0], out[1], list(out[2:2 + nt]), out[-1]


def _gather_wait(send_sems, recv_sems, lands, layouts, after, name):
    nt = len(lands)

    def body(*refs):
        land_refs = refs[:nt]
        sems = refs[nt:nt + 2]
        for send, arrival in _gather_copies(land_refs, layouts, *sems):
            send.wait_send()
            arrival.wait_recv()

    return list(pl.pallas_call(
        body, name=name, out_shape=tuple(pltpu.HBM(a.shape, a.dtype) for a in lands),
        in_specs=[_HBM] * nt + [_SEM, _SEM, _ANY], out_specs=tuple([_HBM] * nt),
        input_output_aliases={t: t for t in range(nt)}, compiler_params=_SPLIT_COPY,
    )(*lands, send_sems, recv_sems, after))


def _gather_forward(lands, layouts, name):
    nt = len(lands)

    def body(*refs):
        outs = refs[nt:2 * nt]
        send_sems, recv_sems = refs[2 * nt:]
        x, y, c = _place()
        sends, arrivals = [], []
        for t, lay in enumerate(layouts):
            for j, (px, py) in enumerate(_other_chips(x, y)):
                for pc, group in ((c, sends), (1 - c, arrivals)):
                    part = lay.half(outs[t], 2 * px + py, pc)
                    group.append(pltpu.make_async_remote_copy(
                        src_ref=part, dst_ref=part, send_sem=send_sems.at[3 * t + j], recv_sem=recv_sems.at[3 * t + j],
                        device_id=(x, y, 1 - c), device_id_type=_MESH))
        for cp in sends:
            cp.start()
        for cp in arrivals:
            cp.wait_recv()
        for cp in sends:
            cp.wait_send()

    return list(pl.pallas_call(
        body, name=name, out_shape=[jax.ShapeDtypeStruct(a.shape, a.dtype) for a in lands],
        in_specs=[_ANY] * nt, out_specs=[_ANY] * nt, input_output_aliases={t: t for t in range(nt)},
        scratch_shapes=[pltpu.SemaphoreType.DMA((3 * nt,)), pltpu.SemaphoreType.DMA((3 * nt,))],
        compiler_params=_cparams(),
    )(*lands))


def _pair_copies(grads, lands, send_sems, recv_sems):
    x, y, c = _place()
    out = []
    for t, (g, land) in enumerate(zip(grads, lands)):
        h = g.shape[1] // 2
        out.append(pltpu.make_async_remote_copy(
            src_ref=g.at[:, pl.ds((1 - c) * h, h), :], dst_ref=land, send_sem=send_sems.at[t],
            recv_sem=recv_sems.at[t], device_id=(x, y, 1 - c), device_id_type=_MESH))
    return out


def _pair_start(grads, after, name):
    nt = len(grads)
    lands = [lax.empty((N_CHIPS, g.shape[1] // 2, g.shape[2]), g.dtype) for g in grads]
    order = [] if after is None else [after]

    def body(*refs):
        send_sems, recv_sems = refs[2 * nt + len(order):2 * nt + len(order) + 2]
        token = refs[-1]
        for cp in _pair_copies(refs[:nt], refs[nt:2 * nt], send_sems, recv_sems):
            cp.start()
        token[...] = jnp.zeros_like(token)

    out = pl.pallas_call(
        body, name=name,
        out_shape=(pltpu.SemaphoreType.DMA((nt,)), pltpu.SemaphoreType.DMA((nt,)),
                   *[pltpu.HBM(a.shape, a.dtype) for a in grads + lands], _TOKEN),
        in_specs=[_HBM] * (2 * nt) + [_ANY] * len(order),
        out_specs=(_SEM, _SEM, *[_HBM] * (2 * nt), pl.BlockSpec(memory_space=pltpu.VMEM)),
        input_output_aliases={t: 2 + t for t in range(2 * nt)}, compiler_params=_SPLIT_COPY,
    )(*_in_hbm(grads + lands), *order)
    return out[0], out[1], list(out[2:2 + nt]), list(out[2 + nt:2 + 2 * nt]), out[-1]


def _pair_wait(send_sems, recv_sems, grads, lands, after, name):
    nt = len(grads)

    def body(*refs):
        for cp in _pair_copies(refs[:nt], refs[nt:2 * nt], *refs[2 * nt:2 * nt + 2]):
            cp.wait_send()
            cp.wait_recv()

    out = pl.pallas_call(
        body, name=name, out_shape=tuple(pltpu.HBM(a.shape, a.dtype) for a in grads + lands),
        in_specs=[_HBM] * (2 * nt) + [_SEM, _SEM, _ANY], out_specs=tuple([_HBM] * (2 * nt)),
        input_output_aliases={t: t for t in range(2 * nt)}, compiler_params=_SPLIT_COPY,
    )(*grads, *lands, send_sems, recv_sems, after)
    return list(out[:nt]), list(out[nt:])


def _pair_sum(own, recv, c_idx, name):
    _, h, cols = recv.shape

    def body(c_ref, own_ref, recv_ref, o_ref):
        o_ref[...] = (own_ref[...] + recv_ref[...]).astype(BF16)

    return pl.pallas_call(
        body, name=name,
        grid_spec=pltpu.PrefetchScalarGridSpec(
            num_scalar_prefetch=1, grid=(N_CHIPS,),
            in_specs=[pl.BlockSpec((None, h, cols), lambda k, c_ref: (k, c_ref[0], 0)),
                      pl.BlockSpec((None, h, cols), lambda k, c_ref: (k, 0, 0))],
            out_specs=pl.BlockSpec((None, h, cols), lambda k, c_ref: (k, 0, 0))),
        out_shape=jax.ShapeDtypeStruct(recv.shape, BF16), compiler_params=_cparams(("arbitrary",)),
    )(c_idx, own, recv)


def _chip_copies(parts, lands, send_sems, recv_sems):
    x, y, c = _place()
    out = []
    for t, (part, land) in enumerate(zip(parts, lands)):
        for j, (px, py) in enumerate(_other_chips(x, y)):
            out.append(pltpu.make_async_remote_copy(
                src_ref=part.at[2 * px + py], dst_ref=land.at[j], send_sem=send_sems.at[3 * t + j],
                recv_sem=recv_sems.at[3 * t + j], device_id=(px, py, c), device_id_type=_MESH))
    return out


def _chip_send_start(parts, after, name):
    nt = len(parts)
    lands = [lax.empty((N_CHIPS - 1,) + p.shape[1:], p.dtype) for p in parts]
    order = [] if after is None else [after]

    def body(*refs):
        send_sems, recv_sems = refs[2 * nt + len(order):2 * nt + len(order) + 2]
        token = refs[-1]
        for cp in _chip_copies(refs[:nt], refs[nt:2 * nt], send_sems, recv_sems):
            cp.start()
        token[...] = jnp.zeros_like(token)

    out = pl.pallas_call(
        body, name=name,
        out_shape=(pltpu.SemaphoreType.DMA((3 * nt,)), pltpu.SemaphoreType.DMA((3 * nt,)),
                   *[pltpu.HBM(a.shape, a.dtype) for a in parts + lands], _TOKEN),
        in_specs=[_HBM] * (2 * nt) + [_ANY] * len(order),
        out_specs=(_SEM, _SEM, *[_HBM] * (2 * nt), pl.BlockSpec(memory_space=pltpu.VMEM)),
        input_output_aliases={t: 2 + t for t in range(2 * nt)}, compiler_params=_SPLIT_COPY,
    )(*_in_hbm(parts + lands), *order)
    return out[0], out[1], list(out[2:2 + nt]), list(out[2 + nt:2 + 2 * nt]), out[-1]


def _chip_send_wait(send_sems, recv_sems, parts, lands, after, name):
    nt = len(parts)

    def body(*refs):
        for cp in _chip_copies(refs[:nt], refs[nt:2 * nt], *refs[2 * nt:2 * nt + 2]):
            cp.wait_send()
            cp.wait_recv()

    out = pl.pallas_call(
        body, name=name, out_shape=tuple(pltpu.HBM(a.shape, a.dtype) for a in parts + lands),
        in_specs=[_HBM] * (2 * nt) + [_SEM, _SEM, _ANY], out_specs=tuple([_HBM] * (2 * nt)),
        input_output_aliases={t: t for t in range(2 * nt)}, compiler_params=_SPLIT_COPY,
    )(*parts, *lands, send_sems, recv_sems, after)
    return list(out[:nt]), list(out[nt:])


def _chip_sum(part, arrived, into, lead, place_idx, name):
    _, h, cols = part.shape

    def body(idx_ref, own_ref, arr_ref, into_ref, o_ref):
        acc = own_ref[...].astype(F32)
        for k in range(N_CHIPS - 1):
            acc = acc + arr_ref[k].astype(F32)
        o_ref[...] = acc

    return pl.pallas_call(
        body, name=name,
        grid_spec=pltpu.PrefetchScalarGridSpec(
            num_scalar_prefetch=1, grid=(1,),
            in_specs=[pl.BlockSpec((None, h, cols), lambda g, idx: (idx[1], 0, 0)),
                      pl.BlockSpec((N_CHIPS - 1, h, cols), lambda g, idx: (0, 0, 0)), _ANY],
            out_specs=pl.BlockSpec((None,) * len(lead) + (h, cols), lambda g, idx: (*lead, idx[0], 0))),
        out_shape=jax.ShapeDtypeStruct(into.shape, F32), input_output_aliases={3: 0},
        compiler_params=_cparams(("arbitrary",)),
    )(place_idx, part, arrived, into)


def _pair_gather(bufs, homes, name):
    nt, nb = len(homes), len(bufs)

    def body(*refs):
        outs = refs[nb:2 * nb]
        send_sems, recv_sems = refs[2 * nb:]
        x, y, c = _place()

        def home(t, pc):
            o, lead, rows = homes[t]
            return outs[o].at[(*lead, pl.ds(pc * (rows // 2), rows // 2), slice(None))]

        def copy(t, pc):
            return pltpu.make_async_remote_copy(src_ref=home(t, pc), dst_ref=home(t, pc), send_sem=send_sems.at[t],
                                                recv_sem=recv_sems.at[t], device_id=(x, y, 1 - c), device_id_type=_MESH)

        sends = [copy(t, c) for t in range(nt)]
        for cp in sends:
            cp.start()
        for t in range(nt):
            copy(t, 1 - c).wait_recv()
        for cp in sends:
            cp.wait_send()

    return pl.pallas_call(
        body, name=name, out_shape=[jax.ShapeDtypeStruct(b.shape, b.dtype) for b in bufs],
        in_specs=[_ANY] * nb, out_specs=[_ANY] * nb, input_output_aliases={o: o for o in range(nb)},
        scratch_shapes=[pltpu.SemaphoreType.DMA((nt,)), pltpu.SemaphoreType.DMA((nt,))],
        compiler_params=_cparams(),
    )(*bufs)


def _sum_devices(g, after, name):
    def body(g_ref, after_ref, o_ref):
        acc = g_ref[0:1, :]
        for d in range(1, N_DEV):
            acc = acc + g_ref[d:d + 1, :]
        o_ref[...] = acc
    vmem = pl.BlockSpec(memory_space=pltpu.VMEM)
    return pl.pallas_call(body, name=name, out_shape=jax.ShapeDtypeStruct((1, g.shape[1]), F32),
                          in_specs=[vmem, _ANY], out_specs=vmem, compiler_params=_cparams())(g, after)


_WEIGHTS = ("w_cond", "b_cond", "norm_pre", "norm_post", "w_ffn_in", "w_ffn_out", "fox_w_in", "fox_b_f",
            "fox_w_out", "sconv_w_in", "sconv_conv_w", "sconv_w_out", "lru_w_in", "lru_conv_w", "lru_conv_b",
            "lru_w_a", "lru_b_a", "lru_w_x", "lru_b_x", "lru_lambda", "lru_w_out")
_BIG = (("w_ffn_in", False), ("w_ffn_out", True), ("fox_w_in", False), ("fox_w_out", True),
        ("sconv_w_in", False), ("sconv_w_out", True), ("lru_w_in", False), ("lru_w_out", True))
_SMALL = tuple(n for n in _WEIGHTS if n != "w_cond" and n not in dict(_BIG))
_COL_SHARDED_SMALL = ("norm_pre", "norm_post", "sconv_conv_w", "lru_conv_w", "lru_conv_b", "lru_lambda")


def _pack_rows(parts, rows=8):
    flat = jnp.concatenate([p.reshape(-1) for p in parts])
    width = -(-flat.size // (rows * 128)) * 128
    return jnp.pad(flat, (0, rows * width - flat.size)).reshape(rows, width)


def _unpack(flat, shapes):
    out, off = [], 0
    for shp in shapes:
        n = math.prod(shp)
        out.append(flat[off:off + n].reshape(shp))
        off += n
    return out


def _join_chips(g):
    g = jnp.moveaxis(g, 0, -2)
    return g.reshape(g.shape[:-2] + (g.shape[-2] * g.shape[-1],))


def _my_columns(full, chip):
    n = full.shape[-1] // N_CHIPS
    return lax.dynamic_slice_in_dim(full, chip * n, n, axis=full.ndim - 1)


def _block_diag(w):
    eye = jnp.eye(LRU_BLOCKS, dtype=w.dtype)
    return jnp.einsum("nij,nm->nimj", w, eye).reshape(D_MODEL, D_MODEL)


def _step(x, c, target, wts, mom, var):
    ix, iy, ic = _place()
    chip = 2 * ix + iy
    dev = 2 * chip + ic
    n_cond = wts["w_cond"].shape[2]

    small_shapes = [(D_MODEL,)] + [wts[n].shape for n in _COL_SHARDED_SMALL]
    g1 = _allgather8(_pack_rows([c[0]] + [wts[n] for n in _COL_SHARDED_SMALL]), "gather_small").reshape(N_DEV, -1)
    c_all = g1[:, :D_MODEL]
    per_chip = [jnp.stack(col) for col in zip(*[_unpack(g1[2 * k], small_shapes) for k in range(N_CHIPS)])]
    small_full = {n: _join_chips(v) for n, v in zip(_COL_SHARDED_SMALL, per_chip[1:])}

    c_pad = jnp.pad(c_all, ((0, COND_ROWS - N_DEV), (0, 0)))
    b_shard = _my_columns(wts["b_cond"], chip)[:, None, :]
    mod_part = _cond_fwd(c_pad, wts["w_cond"], b_shard, "cond_fwd")
    g2 = _allgather8(mod_part[:, :N_DEV].transpose(1, 0, 2).reshape(N_DEV, DEPTH * n_cond), "gather_mod")
    g2 = g2.reshape(N_DEV, N_DEV, DEPTH, n_cond)[0::2]
    mod = _join_chips(lax.dynamic_index_in_dim(g2, dev, axis=1, keepdims=False)).reshape(DEPTH, N_SUB, 3, D_MODEL)

    mixer_names = [("fox_w_in", "fox_w_out"), ("sconv_w_in", "sconv_w_out"), ("lru_w_in", "lru_w_out")]

    def shards_of(i, sub):
        if sub == 1:
            return [wts[n][i // 3] for n in mixer_names[i % 3]]
        return [wts["w_ffn_in"][i, sub // 2], wts["w_ffn_out"][i, sub // 2]]

    chunks = [[(0, 0)], [(0, 1), (0, 2)]] + [[(i, sub) for sub in range(N_SUB)] for i in range(1, DEPTH)]
    in_flight, chunk_of, token = [], {}, mod
    for k, members in enumerate(chunks):
        shards = [s for i, sub in members for s in shards_of(i, sub)]
        layouts = [_Gathered(s.shape, 0) for s in shards]
        if k:
            shards = [s + token[0, 0] for s in shards]
        lands = [_own_block_placed(s.astype(BF16), lay, chip) for s, lay in zip(shards, layouts)]
        send_sems, recv_sems, lands, token = _gather_start(lands, layouts, token, f"gather_start_{k}")
        in_flight.append([send_sems, recv_sems, lands, layouts, False])
        chunk_of.update({m: (k, 2 * pos) for pos, m in enumerate(members)})
    lru_ax = jnp.concatenate([_block_diag(wts["lru_w_a"][0]), _block_diag(wts["lru_w_x"][0])], axis=1).astype(BF16)

    def layer_params(i, sub, x_in):
        k, pos = chunk_of[(i, sub)]
        send_sems, recv_sems, lands, layouts, arrived = in_flight[k]
        if not arrived:
            lands = _gather_wait(send_sems, recv_sems, lands, layouts, x_in, f"gather_wait_{k}")
            in_flight[k][2:] = [_gather_forward(lands, layouts, f"gather_forward_{k}"), layouts, True]
        w_in, w_out = in_flight[k][2][pos:pos + 2]
        w_out = w_out.reshape(-1, w_out.shape[-1])
        if sub != 1:
            out = {"ffn_in": [_W(w_in, (), True)], "ffn_out": [_W(w_out)]}
            if sub == 0:
                out.update(norm_pre=small_full["norm_pre"][i], norm_post=small_full["norm_post"][i])
            return out
        j = i // 3
        if i % 3 == 0:
            w_in = jnp.pad(_join_chips(w_in), ((0, 0), (0, FOX_PAD - 3 * D_MODEL - FOX_HEADS)))
            return {"mixer": {"w_in": _W(w_in), "w_out": _W(w_out), "b_f": wts["fox_b_f"][j][:, None]}}
        if i % 3 == 1:
            return {"mixer": {"w_in": _W(w_in, (), True), "w_out": _W(w_out), "conv_w": small_full["sconv_conv_w"][j]}}
        return {"mixer": {"w_in": _W(w_in, (), True), "w_out": _W(w_out), "conv_w": small_full["lru_conv_w"][j],
                          "conv_b": small_full["lru_conv_b"], "w_ax": _W(lru_ax),
                          "b_a": wts["lru_b_a"].reshape(1, D_MODEL), "b_x": wts["lru_b_x"].reshape(1, D_MODEL),
                          "lam": small_full["lru_lambda"]}}

    place_idx = jnp.stack([ic, chip]).astype(jnp.int32)
    c_idx = place_idx[:1]
    big_index = {n: o for o, (n, _) in enumerate(_BIG)}
    exchanges, pending = [], []

    def to_chips(after):
        i, send_sems, recv_sems, tensors, lands, homes = pending.pop()
        tensors, recv = _pair_wait(send_sems, recv_sems, tensors, lands, after, f"grads_pair_wait_l{i}")
        parts = [_pair_sum(t, r, c_idx, f"grads_pair_sum_l{i}_{k}") for k, (t, r) in enumerate(zip(tensors, recv))]
        send_sems, recv_sems, parts, lands, tok = _chip_send_start(parts, None, f"grads_chip_start_l{i}")
        exchanges.append((i, send_sems, recv_sems, parts, lands, homes))
        return tok

    def chip_blocks(g, by_rows, width):
        if by_rows:
            return g.reshape(N_CHIPS, g.shape[0] // N_CHIPS, g.shape[1])
        if g.ndim == 3:
            return g
        return g[:, :width * N_CHIPS].reshape(g.shape[0], N_CHIPS, width).transpose(1, 0, 2)

    def on_mid(i, dx):
        return to_chips(dx) if pending else None

    def on_grads(i, g, dx):
        n_in, n_out = mixer_names[i % 3]
        items = [("w_ffn_in", (i, k), g["ffn_in"][k]) for k in range(2)]
        items += [("w_ffn_out", (i, k), g["ffn_out"][k]) for k in range(2)]
        items += [(n_in, (i // 3,), g["mixer"]["w_in"]), (n_out, (i // 3,), g["mixer"]["w_out"])]
        tensors = [chip_blocks(t, dict(_BIG)[n], wts[n].shape[-1]) for n, _, t in items]
        homes = [(big_index[n], lead, wts[n].shape[-2]) for n, lead, _ in items]
        send_sems, recv_sems, tensors, lands, tok = _pair_start(tensors, None, f"grads_pair_start_l{i}")
        pending.append((i, send_sems, recv_sems, tensors, lands, homes))
        pair_tokens.append(tok)
        return tok

    pair_tokens = []
    loss_row, grad_x, dmod, lg = _local_step(x[0], target[0], mod, layer_params, on_grads, on_mid, token)
    loss = lax.psum(loss_row[0, 0], ("x", "y", "c"))
    dmod = dmod + pair_tokens[-1][0, 0]

    fox_layers = [i for i in range(DEPTH) if i % 3 == 0]
    sconv_g, lru_g = lg[1]["mixer"], lg[2]["mixer"]
    small_g = {
        "dmod": dmod, "norm_pre": jnp.stack([g["norm_pre"] for g in lg]), "norm_post": jnp.stack([g["norm_post"] for g in lg]),
        "fox_b_f": jnp.stack([lg[i]["mixer"]["b_f"][:, 0] for i in fox_layers]),
        "sconv_conv_w": sconv_g["conv_w"][None], "lru_conv_w": lru_g["conv_w"][None], "lru_conv_b": lru_g["conv_b"],
        "lru_w_a": lru_g["w_a"][None], "lru_b_a": lru_g["b_a"].reshape(1, LRU_BLOCKS, LRU_BLOCK_DIM),
        "lru_w_x": lru_g["w_x"][None], "lru_b_x": lru_g["b_x"].reshape(1, LRU_BLOCKS, LRU_BLOCK_DIM),
        "lru_lambda": lru_g["lam"]}
    g4 = _allgather8(_pack_rows(list(small_g.values())), "gather_small_grads").reshape(N_DEV, -1)
    last_start = to_chips(g4)
    summed = _sum_devices(g4, last_start, "sum_small_grads")[0]
    summed = dict(zip(small_g, _unpack(summed, [v.shape for v in small_g.values()])))
    grads = {n: (_my_columns(summed[n], chip) if n in _COL_SHARDED_SMALL else summed[n]) for n in _SMALL if n != "b_cond"}
    grads["b_cond"] = summed["dmod"].reshape(DEPTH, N_SUB * 3 * D_MODEL)

    dmod_all = (g4[:, :dmod.size] + last_start[0, 0]).reshape(N_DEV, DEPTH, N_SUB * 3 * D_MODEL)
    dmod_s = jnp.pad(_my_columns(dmod_all, chip).transpose(1, 0, 2), ((0, 0), (0, COND_PAD - N_DEV), (0, 0))).astype(BF16)
    c_t = jnp.pad(c_all.T, ((0, 0), (0, COND_PAD - N_DEV)))
    grads["w_cond"], d_cond, m_cond, v_cond = _cond_bwd_adamw(c_t, dmod_s, wts["w_cond"], mom["w_cond"],
                                                              var["w_cond"], "cond_bwd_adamw")

    bufs = [lax.empty(wts[n].shape, F32) for n, _ in _BIG]
    all_homes = []
    for i, send_sems, recv_sems, parts, lands, homes in exchanges:
        follows = d_cond if not all_homes else bufs[0]
        parts, lands = _chip_send_wait(send_sems, recv_sems, parts, lands, follows, f"grads_chip_wait_l{i}")
        for k, (part, land, (o, lead, _)) in enumerate(zip(parts, lands, homes)):
            bufs[o] = _chip_sum(part, land, bufs[o], lead, place_idx, f"grads_chip_sum_l{i}_{k}")
        all_homes += homes
    grads.update(zip([n for n, _ in _BIG], _pair_gather(bufs, all_homes, "grads_pair_gather")))

    delta, new_m, new_v = {"w_cond": d_cond}, {"w_cond": m_cond}, {"w_cond": v_cond}
    for n, _ in _BIG:
        two_d = lambda a: a.reshape(-1, a.shape[-1])
        d, nm, nv = _adamw(two_d(wts[n]), two_d(grads[n]), two_d(mom[n]), two_d(var[n]), "adamw_" + n)
        delta[n], new_m[n], new_v[n] = (a.reshape(wts[n].shape) for a in (d, nm, nv))
    shapes = [wts[n].shape for n in _SMALL]
    packed = [_pack_rows([src[n] for n in _SMALL]) for src in (wts, grads, mom, var)]
    for dst, out in zip((delta, new_m, new_v), _adamw(*packed, "adamw_small")):
        dst.update(zip(_SMALL, _unpack(out.reshape(-1), shapes)))

    return (loss, grad_x[None], *[grads[n] for n in _WEIGHTS], *[delta[n] for n in _WEIGHTS],
            *[new_m[n] for n in _WEIGHTS], *[new_v[n] for n in _WEIGHTS])


def kernel(x, c, w_cond, b_cond, norm_pre, norm_post, w_ffn_in, w_ffn_out, fox_w_in, fox_b_f, fox_w_out, sconv_w_in, sconv_conv_w, sconv_w_out, lru_w_in, lru_conv_w, lru_conv_b, lru_w_a, lru_b_a, lru_w_x, lru_b_x, lru_lambda, lru_w_out, loss_target, m_w_cond, m_b_cond, m_norm_pre, m_norm_post, m_w_ffn_in, m_w_ffn_out, m_fox_w_in, m_fox_b_f, m_fox_w_out, m_sconv_w_in, m_sconv_conv_w, m_sconv_w_out, m_lru_w_in, m_lru_conv_w, m_lru_conv_b, m_lru_w_a, m_lru_b_a, m_lru_w_x, m_lru_b_x, m_lru_lambda, m_lru_w_out, v_w_cond, v_b_cond, v_norm_pre, v_norm_post, v_w_ffn_in, v_w_ffn_out, v_fox_w_in, v_fox_b_f, v_fox_w_out, v_sconv_w_in, v_sconv_conv_w, v_sconv_w_out, v_lru_w_in, v_lru_conv_w, v_lru_conv_b, v_lru_w_a, v_lru_b_a, v_lru_w_x, v_lru_b_x, v_lru_lambda, v_lru_w_out):
    given = dict(locals())
    wts = {n: given[n] for n in _WEIGHTS}
    mom = {n: given["m_" + n] for n in _WEIGHTS}
    var = {n: given["v_" + n] for n in _WEIGHTS}
    return _step(x, c, loss_target, wts, mom, var)
```

```python
import functools
import math
from typing import NamedTuple

import jax
import jax.numpy as jnp
from jax import lax
from jax.experimental import pallas as pl
from jax.experimental.pallas import tpu as pltpu

F32 = jnp.float32
BF16 = jnp.bfloat16

D_MODEL = 1024
DEPTH = 4
N_SUB = 3
D_FF = 2816
RMS_EPS = 1e-6
FOX_HEADS = 16
FOX_HEAD_DIM = 64
FOX_PAD = 3200
LRU_BLOCKS = 16
LRU_BLOCK_DIM = 64
LRU_C = 8.0
N_CHIPS = 4
N_DEV = 8

ADAM_LR = 0.001
ADAM_B1 = 0.9
ADAM_B2 = 0.999
ADAM_EPS = 1e-08
ADAM_WD = 0.01
ADAM_STEP = 10

VMEM_LIMIT_V7X = 56 * 1024 * 1024
ROW_TILE = 256
COL_TILE = 256
ATT_TILE = 256
MM_ROWS = 1024


def _cparams(sem=None):
    return pltpu.CompilerParams(vmem_limit_bytes=VMEM_LIMIT_V7X, dimension_semantics=sem)


def _sigmoid(z):
    return 1.0 / (1.0 + jnp.exp(-z))


def _softplus(z):
    return jnp.maximum(z, 0.0) + jnp.log(1.0 + jnp.exp(-jnp.abs(z)))


def _rows_sum(v):
    return jnp.sum(v, axis=0, keepdims=True)


class _W(NamedTuple):
    arr: jax.Array
    prefix: tuple = ()
    blocked: bool = False


def _w_spec(w, block2, pos):
    lead = (None,) * (len(w.prefix) + (1 if w.blocked else 0))
    if w.blocked:
        return pl.BlockSpec(lead + block2, lambda *g: (pos(*g)[0], *w.prefix, pos(*g)[1], pos(*g)[2]))
    return pl.BlockSpec(lead + block2, lambda *g: (*w.prefix, pos(*g)[1], pos(*g)[2]))


def _mm_nn(a, b, name, tn=None):
    m, k = a.shape
    if b.blocked:
        steps, bn = b.arr.shape[0], b.arr.shape[-1]
        b_spec = _w_spec(b, (k, bn), lambda n: (n, 0, 0))
    else:
        n_total = b.arr.shape[-1]
        bn = n_total if tn is None else tn
        steps = n_total // bn
        assert steps * bn == n_total
        b_spec = _w_spec(b, (k, bn), lambda n: (0, 0, n))
    tm = min(MM_ROWS, m)

    def body(a_ref, b_ref, o_ref):
        def step(i, carry):
            r = pl.ds(pl.multiple_of(i * tm, tm), tm)
            o_ref[r, :] = jnp.dot(a_ref[r, :], b_ref[...], preferred_element_type=F32)
            return carry
        lax.fori_loop(0, m // tm, step, 0)

    return pl.pallas_call(
        body, name=name, grid=(steps,),
        in_specs=[pl.BlockSpec((m, k), lambda n: (0, 0)), b_spec],
        out_specs=pl.BlockSpec((m, bn), lambda n: (0, n)),
        out_shape=jax.ShapeDtypeStruct((m, steps * bn), F32),
        compiler_params=_cparams(("arbitrary",)),
    )(a, b.arr)


def _mm_nt(dy, w, name, tk=None, tn=None):
    m, n_total = dy.shape
    k = w.arr.shape[-2]
    if w.blocked:
        bk, bn = k, w.arr.shape[-1]
        grid = (1, w.arr.shape[0])
        w_spec = _w_spec(w, (k, bn), lambda kt, n: (n, 0, 0))
    else:
        bk = k if tk is None else tk
        bn = n_total if tn is None else tn
        grid = (k // bk, n_total // bn)
        assert grid[0] * bk == k and grid[1] * bn == n_total
        w_spec = _w_spec(w, (bk, bn), lambda kt, n: (0, kt, n))
    tm = min(MM_ROWS, m)

    reduce_steps = grid[1]

    def body(dy_ref, w_ref, o_ref):
        def step(i, carry):
            r = pl.ds(pl.multiple_of(i * tm, tm), tm)
            part = lax.dot_general(dy_ref[r, :], w_ref[...], (((1,), (1,)), ((), ())), preferred_element_type=F32)
            if reduce_steps == 1:
                o_ref[r, :] = part
            else:
                o_ref[r, :] += part
            return carry

        if reduce_steps > 1:
            @pl.when(pl.program_id(1) == 0)
            def _():
                o_ref[...] = jnp.zeros_like(o_ref)
        lax.fori_loop(0, m // tm, step, 0)

    return pl.pallas_call(
        body, name=name, grid=grid,
        in_specs=[pl.BlockSpec((m, bn), lambda kt, n: (0, n)), w_spec],
        out_specs=pl.BlockSpec((m, bk), lambda kt, n: (0, kt)),
        out_shape=jax.ShapeDtypeStruct((m, k), F32),
        compiler_params=_cparams(("arbitrary", "arbitrary")),
    )(dy, w.arr)


def _mm_tn(x, dy, name, tk=None, tn=None, blocked_out=False):
    s, k = x.shape
    n_total = dy.shape[1]
    bk = k if tk is None else tk
    bn = n_total if tn is None else tn
    grid = (k // bk, n_total // bn)
    assert grid[0] * bk == k and grid[1] * bn == n_total
    ck = 256 if bk % 256 == 0 else 128

    def body(x_ref, dy_ref, o_ref):
        def step(i, carry):
            c = pl.ds(pl.multiple_of(i * ck, ck), ck)
            o_ref[c, :] = lax.dot_general(x_ref[:, c], dy_ref[...], (((0,), (0,)), ((), ())),
                                          preferred_element_type=F32)
            return carry
        lax.fori_loop(0, bk // ck, step, 0)

    if blocked_out:
        assert grid[0] == 1
        out_spec = pl.BlockSpec((None, bk, bn), lambda kt, n: (n, 0, 0))
        out_shape = jax.ShapeDtypeStruct((grid[1], k, bn), F32)
    else:
        out_spec = pl.BlockSpec((bk, bn), lambda kt, n: (kt, n))
        out_shape = jax.ShapeDtypeStruct((k, n_total), F32)
    return pl.pallas_call(
        body, name=name, grid=grid,
        in_specs=[pl.BlockSpec((s, bk), lambda kt, n: (0, kt)), pl.BlockSpec((s, bn), lambda kt, n: (0, n))],
        out_specs=out_spec, out_shape=out_shape,
        compiler_params=_cparams(("arbitrary", "arbitrary")),
    )(x, dy)


def _row_call(name, body, rows, fulls, row_outs, acc_outs, tr=ROW_TILE, after=None):
    s = rows[0].shape[0]
    tr = min(tr, s)
    in_specs = [pl.BlockSpec((tr, a.shape[1]), lambda i: (i, 0)) for a in rows]
    in_specs += [pl.BlockSpec(a.shape, lambda i: (0, 0)) for a in fulls]
    n_in = len(in_specs)
    order = [] if after is None else [after]
    in_specs += [pl.BlockSpec(memory_space=pl.ANY)] * len(order)
    out_specs = [pl.BlockSpec((tr, c), lambda i: (i, 0)) for c, _ in row_outs]
    out_specs += [pl.BlockSpec((1, c), lambda i: (0, 0)) for c, _ in acc_outs]
    out_shape = [jax.ShapeDtypeStruct((s, c), dt) for c, dt in row_outs]
    out_shape += [jax.ShapeDtypeStruct((1, c), dt) for c, dt in acc_outs]
    n_acc = len(acc_outs)

    def wrapped(*refs):
        refs = refs[:n_in] + refs[n_in + len(order):]
        if n_acc:
            @pl.when(pl.program_id(0) == 0)
            def _():
                for r in refs[len(refs) - n_acc:]:
                    r[...] = jnp.zeros_like(r)
        body(*refs)

    return pl.pallas_call(
        wrapped, name=name, grid=(s // tr,), in_specs=in_specs, out_specs=out_specs, out_shape=out_shape,
        compiler_params=_cparams(("arbitrary",)),
    )(*rows, *fulls, *order)


def _rms(v):
    return lax.rsqrt(jnp.mean(v * v, axis=-1, keepdims=True) + RMS_EPS)


def _pre_norm(x, g_pre, scale, shift, name, after=None):
    def body(x_ref, g_ref, sc_ref, sh_ref, h_ref):
        xv = x_ref[...]
        h = (xv * _rms(xv)) * g_ref[...] * (1.0 + sc_ref[...]) + sh_ref[...]
        h_ref[...] = h.astype(BF16)
    return _row_call(name, body, [x], [g_pre, scale, shift], [(D_MODEL, BF16)], [], after=after)[0]


def _post_norm(x, y, g_post, gate, coef, name):
    def body(x_ref, y_ref, g_ref, gate_ref, o_ref):
        yv = y_ref[...]
        o_ref[...] = x_ref[...] + (coef * gate_ref[...]) * ((yv * _rms(yv)) * g_ref[...])
    return _row_call(name, body, [x, y], [g_post, gate], [(D_MODEL, F32)], [])[0]


def _post_norm_bwd(dxo, y, g_post, gate, coef, name, after=None):
    def body(dxo_ref, y_ref, g_ref, gate_ref, dy_ref, dgate_ref, dg_ref):
        yv = y_ref[...]
        r2 = _rms(yv)
        yn = yv * r2
        dxo_v = dxo_ref[...]
        dgate_ref[...] += _rows_sum(dxo_v * (yn * g_ref[...])) * coef
        dz = dxo_v * (coef * gate_ref[...])
        dg_ref[...] += _rows_sum(dz * yn)
        dyn = dz * g_ref[...]
        dy = r2 * (dyn - yn * jnp.mean(dyn * yn, axis=-1, keepdims=True))
        dy_ref[...] = dy.astype(BF16)
    return _row_call(name, body, [dxo, y], [g_post, gate], [(D_MODEL, BF16)], [(D_MODEL, F32), (D_MODEL, F32)],
                     after=after)


def _pre_norm_bwd(dxo, dh, x, g_pre, scale, name):
    def body(dxo_ref, dh_ref, x_ref, g_ref, sc_ref, dx_ref, dshift_ref, dscale_ref, dg_ref):
        xv = x_ref[...]
        r = _rms(xv)
        xn = xv * r
        dh_v = dh_ref[...]
        one_sc = 1.0 + sc_ref[...]
        dshift_ref[...] += _rows_sum(dh_v)
        dscale_ref[...] += _rows_sum(dh_v * (xn * g_ref[...]))
        dg_ref[...] += _rows_sum(dh_v * xn * one_sc)
        dxn = dh_v * (g_ref[...] * one_sc)
        dx_ref[...] = dxo_ref[...] + r * (dxn - xn * jnp.mean(dxn * xn, axis=-1, keepdims=True))
    return _row_call(name, body, [dxo, dh, x], [g_pre, scale], [(D_MODEL, F32)],
                     [(D_MODEL, F32), (D_MODEL, F32), (D_MODEL, F32)])


FFN_COLS = 1408


def _ffn_in_act(h, w_in, name):
    m, k = h.shape
    half, bn = w_in.arr.shape[0] // 2, w_in.arr.shape[-1]
    assert bn == FFN_COLS and half * bn == D_FF
    tm = min(MM_ROWS, m)

    def body(h_ref, wg_ref, wu_ref, g_ref, u_ref, a_ref):
        g = jnp.dot(h_ref[...], wg_ref[...], preferred_element_type=F32)
        g_ref[...] = g
        u = jnp.dot(h_ref[...], wu_ref[...], preferred_element_type=F32)
        u_ref[...] = u
        a_ref[...] = (g * _sigmoid(g) * u).astype(BF16)

    tile = pl.BlockSpec((tm, bn), lambda t, i: (i, t))
    return pl.pallas_call(
        body, name=name, grid=(half, m // tm),
        in_specs=[pl.BlockSpec((tm, k), lambda t, i: (i, 0)),
                  _w_spec(w_in, (k, bn), lambda t, i: (t, 0, 0)),
                  _w_spec(w_in, (k, bn), lambda t, i: (half + t, 0, 0))],
        out_specs=[tile, tile, tile],
        out_shape=[jax.ShapeDtypeStruct((m, D_FF), F32)] * 2 + [jax.ShapeDtypeStruct((m, D_FF), BF16)],
        compiler_params=_cparams(("arbitrary", "arbitrary")),
    )(h, w_in.arr, w_in.arr)


def _ffn_out_bx_act(dy, w_out, g, u, name):
    m = dy.shape[0]
    tr = min(512, m)

    def body(dy_ref, w_ref, g_ref, u_ref, dgu_ref):
        dy_v = dy_ref[...]
        for c in range(D_FF // FFN_COLS):
            cols = slice(c * FFN_COLS, (c + 1) * FFN_COLS)
            da = lax.dot_general(dy_v, w_ref[cols, :], _NT, preferred_element_type=F32)
            gv = g_ref[:, cols]
            sg = _sigmoid(gv)
            dgu_ref[:, cols] = (da * u_ref[:, cols] * (sg * (1.0 + gv * (1.0 - sg)))).astype(BF16)
            dgu_ref[:, D_FF + c * FFN_COLS:D_FF + (c + 1) * FFN_COLS] = (da * (gv * sg)).astype(BF16)

    rows = lambda width: pl.BlockSpec((tr, width), lambda i: (i, 0))
    return pl.pallas_call(
        body, name=name, grid=(m // tr,),
        in_specs=[rows(D_MODEL), _w_spec(w_out, (D_FF, D_MODEL), lambda i: (0, 0, 0)), rows(D_FF), rows(D_FF)],
        out_specs=rows(2 * D_FF), out_shape=jax.ShapeDtypeStruct((m, 2 * D_FF), BF16),
        compiler_params=_cparams(("arbitrary",)),
    )(dy, w_out.arr, g, u)


def _loss_head(y, target, name):
    def body(y_ref, t_ref, dy_ref, loss_ref):
        e = y_ref[...] - t_ref[...]
        dy_ref[...] = e * (1.0 / D_MODEL)
        part = jnp.sum(jnp.mean(e * e, axis=-1, keepdims=True), axis=0, keepdims=True) * 0.5
        loss_ref[...] += jnp.broadcast_to(part, loss_ref.shape)
    return _row_call(name, body, [y, target], [], [(D_MODEL, F32)], [(128, F32)])


def _lane_scan(v, reverse):
    s = v.shape[1]
    lane = lax.broadcasted_iota(jnp.int32, v.shape, 1)
    d = 1
    while d < s:
        if reverse:
            v = v + jnp.where(lane < s - d, pltpu.roll(v, s - d, 1), 0.0)
        else:
            v = v + jnp.where(lane >= d, pltpu.roll(v, d, 1), 0.0)
        d *= 2
    return v


def _fox_gate(flt, b_f, name):
    def body(f_ref, b_ref, cum_ref):
        z = f_ref[...] + b_ref[...]
        cum_ref[...] = _lane_scan(-_softplus(-z), reverse=False)
    return pl.pallas_call(body, name=name, out_shape=jax.ShapeDtypeStruct(flt.shape, F32),
                          compiler_params=_cparams())(flt, b_f)


def _fox_gate_bwd(dcum_q, dcum_k, flt, b_f, name):
    def body(dq_ref, dk_ref, f_ref, b_ref, df_ref, db_ref):
        z = f_ref[...] + b_ref[...]
        df = _lane_scan(dq_ref[...] + dk_ref[...], reverse=True) * _sigmoid(-z)
        df_ref[...] = df
        db_ref[...] = jnp.sum(df, axis=1, keepdims=True)
    h = flt.shape[0]
    return pl.pallas_call(body, name=name,
                          out_shape=(jax.ShapeDtypeStruct(flt.shape, F32), jax.ShapeDtypeStruct((h, 1), F32)),
                          compiler_params=_cparams())(dcum_q, dcum_k, flt, b_f)


def _pick_head(block, h):
    lane = lax.broadcasted_iota(jnp.int32, block.shape, 1)
    return jnp.sum(jnp.where(lane == h, block, 0.0), axis=1, keepdims=True)


def _put_head(ref, col, h):
    @pl.when(h == 0)
    def _():
        ref[...] = jnp.zeros_like(ref)
    lane = lax.broadcasted_iota(jnp.int32, ref.shape, 1)
    ref[...] = jnp.where(lane == h, col, ref[...])


_NT = (((1,), (1,)), ((), ()))
_FOX_SCALE = FOX_HEAD_DIM ** -0.5


def _causal(s_tile, t):
    row = lax.broadcasted_iota(jnp.int32, (t, t), 0)
    col = lax.broadcasted_iota(jnp.int32, (t, t), 1)
    return jnp.where(col <= row, s_tile, -jnp.inf)


HEAD_PAIRS = FOX_HEADS // 2
PAIR_W = 2 * FOX_HEAD_DIM


def _low_half(shape):
    return lax.broadcasted_iota(jnp.int32, shape, 1) < FOX_HEAD_DIM


def _fox_attn_fwd(qkv, cum, cum_t, name):
    s = qkv.shape[0]
    t = min(ATT_TILE, s)

    def body(q_ref, k_ref, v_ref, cum_ref, cumt_ref, o_ref, ob_ref, lse_ref):
        i = pl.program_id(0)
        hp = pl.program_id(1)
        lo = _low_half((t, PAIR_W))
        qv = q_ref[...]
        zero = jnp.zeros_like(qv)
        q2 = (jnp.where(lo, qv, zero), jnp.where(lo, zero, qv))
        cum_v = cum_ref[...]
        cq2 = (_pick_head(cum_v, 2 * hp), _pick_head(cum_v, 2 * hp + 1))

        def step(j, carry, masked):
            ks = pl.ds(pl.multiple_of(j * t, t), t)
            kj = k_ref[ks, :]
            vj = v_ref[ks, :]
            out = []
            for e in range(2):
                m, l, acc = carry[e]
                sc = lax.dot_general(q2[e], kj, _NT, preferred_element_type=F32) * _FOX_SCALE
                sc = sc + cq2[e] - cumt_ref[e:e + 1, ks]
                if masked:
                    sc = _causal(sc, t)
                m_new = jnp.maximum(m, jnp.max(sc, axis=1, keepdims=True))
                alpha = jnp.exp(m - m_new)
                p = jnp.exp(sc - m_new)
                l = alpha * l + jnp.sum(p, axis=1, keepdims=True)
                acc = alpha * acc + jnp.dot(p.astype(BF16), vj, preferred_element_type=F32)
                out.append((m_new, l, acc))
            return tuple(out)

        one = (jnp.full((t, 1), -jnp.inf, F32), jnp.zeros((t, 1), F32), jnp.zeros((t, PAIR_W), F32))
        carry = lax.fori_loop(0, i, lambda j, c: step(j, c, False), (one, one))
        (m0, l0, a0), (m1, l1, a1) = step(i, carry, True)
        o = jnp.where(lo, a0 / l0, a1 / l1)
        o_ref[...] = o
        ob_ref[...] = o.astype(BF16)
        _put_head(lse_ref, m0 + jnp.log(l0), 2 * hp)
        _put_head(lse_ref, m1 + jnp.log(l1), 2 * hp + 1)

    nat_tile = pl.BlockSpec((t, FOX_HEADS), lambda i, hp: (i, 0))
    out_tile = pl.BlockSpec((t, PAIR_W), lambda i, hp: (i, hp))
    return pl.pallas_call(
        body, name=name, grid=(s // t, HEAD_PAIRS),
        in_specs=[pl.BlockSpec((t, PAIR_W), lambda i, hp: (i, hp)),
                  pl.BlockSpec((s, PAIR_W), lambda i, hp: (0, HEAD_PAIRS + hp)),
                  pl.BlockSpec((s, PAIR_W), lambda i, hp: (0, 2 * HEAD_PAIRS + hp)),
                  nat_tile, pl.BlockSpec((None, 2, s), lambda i, hp: (hp, 0, 0))],
        out_specs=[out_tile, out_tile, nat_tile],
        out_shape=[jax.ShapeDtypeStruct((s, D_MODEL), F32), jax.ShapeDtypeStruct((s, D_MODEL), BF16),
                   jax.ShapeDtypeStruct((s, FOX_HEADS), F32)],
        compiler_params=_cparams(("arbitrary", "arbitrary")),
    )(qkv, qkv, qkv, cum, cum_t)


def _fox_delta(do, o, expand, name):
    def body(do_ref, o_ref, e_ref, d_ref):
        prod = do_ref[...] * o_ref[...]
        hi = prod.astype(BF16)
        lo = (prod - hi.astype(F32)).astype(BF16)
        tot = (jnp.dot(hi, e_ref[...], preferred_element_type=F32)
               + jnp.dot(lo, e_ref[...], preferred_element_type=F32))
        d_ref[...] = tot[:, :FOX_HEADS]
    return _row_call(name, body, [do, o], [expand], [(FOX_HEADS, F32)], [])[0]


def _fox_attn_bwd(qkv, do, cum, cum_t, lse_t, delta_t, name):
    s = qkv.shape[0]
    t = min(ATT_TILE, s)
    nq = s // t
    tn_dims = (((0,), (0,)), ((), ()))

    def body(q_ref, k_ref, v_ref, do_ref, cum_ref, cumt_ref, lset_ref, deltat_ref,
             dq_ref, dk_ref, dv_ref, dck_ref, dcq_ref):
        hp = pl.program_id(0)
        j = pl.program_id(1)

        @pl.when(j == 0)
        def _():
            dq_ref[...] = jnp.zeros_like(dq_ref)
            dcq_ref[...] = jnp.zeros_like(dcq_ref)
        dk_ref[...] = jnp.zeros_like(dk_ref)
        dv_ref[...] = jnp.zeros_like(dv_ref)

        lo = _low_half((t, PAIR_W))
        lane = lax.broadcasted_iota(jnp.int32, (t, PAIR_W), 1)
        kv = k_ref[...]
        vv = v_ref[...]
        zero = jnp.zeros_like(kv)
        k2 = (jnp.where(lo, kv, zero), jnp.where(lo, zero, kv))
        v2 = (jnp.where(lo, vv, zero), jnp.where(lo, zero, vv))
        cum_v = cum_ref[...]
        ck2 = (_pick_head(cum_v, 2 * hp), _pick_head(cum_v, 2 * hp + 1))

        def step(i, dck, masked):
            qs = pl.ds(pl.multiple_of(i * t, t), t)
            qi = q_ref[qs, :]
            do_i = do_ref[qs, :].astype(BF16)
            dv_p, dk_p, dq_p = [], [], []
            for e in range(2):
                st = lax.dot_general(k2[e], qi, _NT, preferred_element_type=F32) * _FOX_SCALE
                st = st + cumt_ref[e:e + 1, qs] - ck2[e]
                if masked:
                    row = lax.broadcasted_iota(jnp.int32, (t, t), 0)
                    col = lax.broadcasted_iota(jnp.int32, (t, t), 1)
                    st = jnp.where(row <= col, st, -jnp.inf)
                pt = jnp.exp(st - lset_ref[e:e + 1, qs])
                dv_p.append(jnp.dot(pt.astype(BF16), do_i, preferred_element_type=F32))
                dpt = lax.dot_general(v2[e], do_i, _NT, preferred_element_type=F32)
                dst = pt * (dpt - deltat_ref[e:e + 1, qs])
                dsb = dst.astype(BF16)
                dk_p.append(jnp.dot(dsb, qi, preferred_element_type=F32))
                dq_p.append(lax.dot_general(dsb, kv, tn_dims, preferred_element_type=F32))
                dck = dck - jnp.where(lane == e, jnp.sum(dst, axis=1, keepdims=True), 0.0)
                dcq_ref[e:e + 1, qs] += jnp.sum(dst, axis=0, keepdims=True)
            dv_ref[...] += jnp.where(lo, dv_p[0], dv_p[1])
            dk_ref[...] += jnp.where(lo, dk_p[0], dk_p[1])
            dq_ref[qs, :] += jnp.where(lo, dq_p[0], dq_p[1]) * _FOX_SCALE
            return dck

        dck = step(j, jnp.zeros((t, PAIR_W), F32), True)
        dck = lax.fori_loop(j + 1, nq, lambda i, c: step(i, c, False), dck)
        dk_ref[...] = dk_ref[...] * _FOX_SCALE
        dck_ref[...] = dck

    pair_full = lambda part: pl.BlockSpec((s, PAIR_W), lambda hp, j: (0, part * HEAD_PAIRS + hp))
    pair_tile = lambda part: pl.BlockSpec((t, PAIR_W), lambda hp, j: (j, part * HEAD_PAIRS + hp))
    rows = pl.BlockSpec((None, 2, s), lambda hp, j: (hp, 0, 0))
    return pl.pallas_call(
        body, name=name, grid=(HEAD_PAIRS, nq),
        in_specs=[pair_full(0), pair_tile(1), pair_tile(2), pair_full(0),
                  pl.BlockSpec((t, FOX_HEADS), lambda hp, j: (j, 0)), rows, rows, rows],
        out_specs=[pair_full(0), pair_tile(0), pair_tile(0),
                   pl.BlockSpec((None, t, PAIR_W), lambda hp, j: (hp, j, 0)), rows],
        out_shape=[jax.ShapeDtypeStruct((s, D_MODEL), F32)] * 3
        + [jax.ShapeDtypeStruct((HEAD_PAIRS, s, PAIR_W), F32), jax.ShapeDtypeStruct((HEAD_PAIRS, 2, s), F32)],
        compiler_params=_cparams(("arbitrary", "arbitrary")),
    )(qkv, qkv, qkv, do, cum, cum_t, lse_t, delta_t)


def _shift_down(v, d):
    row = lax.broadcasted_iota(jnp.int32, v.shape, 0)
    return jnp.where(row >= d, pltpu.roll(v, d, 0), 0.0)


def _shift_up(v, d):
    s = v.shape[0]
    row = lax.broadcasted_iota(jnp.int32, v.shape, 0)
    return jnp.where(row < s - d, pltpu.roll(v, s - d, 0), 0.0)


def _conv_taps(v, cw_ref, width):
    out = cw_ref[width - 1:width, :] * v
    for k in range(width - 1):
        out = out + cw_ref[k:k + 1, :] * _shift_down(v, width - 1 - k)
    return out


def _conv_taps_bwd(dout, v, cw_ref, dcw_ref, width):
    dv = cw_ref[width - 1:width, :] * dout
    dcw_ref[width - 1:width, :] = _rows_sum(dout * v)
    for k in range(width - 1):
        d = width - 1 - k
        dv = dv + cw_ref[k:k + 1, :] * _shift_up(dout, d)
        dcw_ref[k:k + 1, :] = _rows_sum(dout * _shift_down(v, d))
    return dv


def _col_spec(s, tc, part=0):
    off = part * (D_MODEL // tc)
    return pl.BlockSpec((s, tc), lambda c: (0, c + off))


def _small_spec(rows, tc):
    return pl.BlockSpec((rows, tc), lambda c: (0, c))


def _col_call(name, body, in_arrays, in_specs, out_rows, s, tc):
    return pl.pallas_call(
        body, name=name, grid=(D_MODEL // tc,), in_specs=in_specs,
        out_specs=[pl.BlockSpec((r, tc), lambda c: (0, c)) for r, _ in out_rows],
        out_shape=[jax.ShapeDtypeStruct((r, D_MODEL), dt) for r, dt in out_rows],
        compiler_params=_cparams(("arbitrary",)),
    )(*in_arrays)


def _sconv_fwd(proj, conv_w, name):
    s = proj.shape[0]
    tc = COL_TILE

    def body(b_ref, c_ref, x_ref, cw_ref, y_ref):
        y_ref[...] = (b_ref[...] * _conv_taps(c_ref[...] * x_ref[...], cw_ref, 3)).astype(BF16)

    return _col_call(name, body, [proj, proj, proj, conv_w],
                     [_col_spec(s, tc, 0), _col_spec(s, tc, 1), _col_spec(s, tc, 2), _small_spec(3, tc)],
                     [(s, BF16)], s, tc)[0]


def _sconv_bwd(dy, proj, conv_w, name):
    s = proj.shape[0]
    tc = COL_TILE

    def body(dy_ref, b_ref, c_ref, x_ref, cw_ref, db_ref, dc_ref, dx_ref, dcw_ref):
        w = c_ref[...] * x_ref[...]
        dy_v = dy_ref[...]
        db_ref[...] = (dy_v * _conv_taps(w, cw_ref, 3)).astype(BF16)
        dw = _conv_taps_bwd(dy_v * b_ref[...], w, cw_ref, dcw_ref, 3)
        dc_ref[...] = (dw * x_ref[...]).astype(BF16)
        dx_ref[...] = (dw * c_ref[...]).astype(BF16)

    return _col_call(name, body, [dy, proj, proj, proj, conv_w],
                     [_col_spec(s, tc), _col_spec(s, tc, 0), _col_spec(s, tc, 1), _col_spec(s, tc, 2),
                      _small_spec(3, tc)],
                     [(s, BF16), (s, BF16), (s, BF16), (3, F32)], s, tc)


def _lru_conv(proj, conv_w, conv_b, name):
    s = proj.shape[0]
    tc = COL_TILE

    def body(x_ref, cw_ref, cb_ref, xb_ref, xbb_ref):
        xb = _conv_taps(x_ref[...], cw_ref, 4) + cb_ref[...]
        xb_ref[...] = xb
        xbb_ref[...] = xb.astype(BF16)

    return _col_call(name, body, [proj, conv_w, conv_b],
                     [_col_spec(s, tc, 1), _small_spec(4, tc), _small_spec(1, tc)],
                     [(s, F32), (s, BF16)], s, tc)


def _lru_conv_bwd(dxb1, dxb2, proj, conv_w, name):
    s = proj.shape[0]
    tc = COL_TILE

    def body(d1_ref, d2_ref, x_ref, cw_ref, dx_ref, dcw_ref, dcb_ref):
        dxb = d1_ref[...] + d2_ref[...]
        dcb_ref[...] = _rows_sum(dxb)
        dx_ref[...] = _conv_taps_bwd(dxb, x_ref[...], cw_ref, dcw_ref, 4).astype(BF16)

    return _col_call(name, body, [dxb1, dxb2, proj, conv_w],
                     [_col_spec(s, tc), _col_spec(s, tc), _col_spec(s, tc, 1), _small_spec(4, tc)],
                     [(s, BF16), (4, F32), (1, F32)], s, tc)


_GELU_C = math.sqrt(2.0 / math.pi)


def _gelu_parts(g):
    inner = _GELU_C * (g + 0.044715 * g * g * g)
    th = jnp.tanh(inner)
    val = 0.5 * g * (1.0 + th)
    der = 0.5 * (1.0 + th) + 0.5 * g * (1.0 - th * th) * (_GELU_C * (1.0 + 3.0 * 0.044715 * g * g))
    return val, der


def _lru_gates(pa_ref, px_ref, ba_ref, bx_ref, lam_ref):
    r = _sigmoid(pa_ref[...] + ba_ref[...])
    ig = _sigmoid(px_ref[...] + bx_ref[...])
    sp = _softplus(-lam_ref[...])
    log_a = (-LRU_C) * r * sp
    a = jnp.exp(log_a)
    z = 2.0 * log_a
    one_m_a2 = jnp.where(z > -1e-3, -(z * (1.0 + z * (0.5 + z * (1.0 / 6.0)))), 1.0 - jnp.exp(z))
    return r, ig, sp, a, jnp.sqrt(one_m_a2)


def _lru_scan(pre, xb, proj, b_a, b_x, lam, name):
    s = xb.shape[0]
    tc = COL_TILE

    def body(pa_ref, px_ref, xb_ref, g_ref, ba_ref, bx_ref, lam_ref, y_ref, hs_ref):
        _, ig, _, a, mult = _lru_gates(pa_ref, px_ref, ba_ref, bx_ref, lam_ref)
        b = mult * (ig * xb_ref[...])
        d = 1
        while d < s:
            row = lax.broadcasted_iota(jnp.int32, a.shape, 0)
            keep = row >= d
            b = b + a * jnp.where(keep, pltpu.roll(b, d, 0), 0.0)
            a = a * jnp.where(keep, pltpu.roll(a, d, 0), 1.0)
            d *= 2
        hs_ref[...] = b
        y_ref[...] = (b * _gelu_parts(g_ref[...])[0]).astype(BF16)

    return _col_call(name, body, [pre, pre, xb, proj, b_a, b_x, lam],
                     [_col_spec(s, tc, 0), _col_spec(s, tc, 1), _col_spec(s, tc), _col_spec(s, tc, 0),
                      _small_spec(1, tc), _small_spec(1, tc), _small_spec(1, tc)],
                     [(s, BF16), (s, F32)], s, tc)


def _lru_scan_bwd(dy, pre, xb, proj, hs, b_a, b_x, lam, name):
    s = xb.shape[0]
    tc = COL_TILE

    def body(dy_ref, pa_ref, px_ref, xb_ref, g_ref, hs_ref, ba_ref, bx_ref, lam_ref,
             dg_ref, dpa_ref, dpx_ref, dxb_ref, dba_ref, dbx_ref, dlam_ref):
        r, ig, sp, a, mult = _lru_gates(pa_ref, px_ref, ba_ref, bx_ref, lam_ref)
        gl, gl_der = _gelu_parts(g_ref[...])
        dy_v = dy_ref[...]
        hs_v = hs_ref[...]
        dg_ref[...] = (dy_v * hs_v * gl_der).astype(BF16)
        lam_t = dy_v * gl
        coef = _shift_up(a, 1)
        d = 1
        while d < s:
            row = lax.broadcasted_iota(jnp.int32, coef.shape, 0)
            keep = row < s - d
            lam_t = lam_t + coef * jnp.where(keep, pltpu.roll(lam_t, s - d, 0), 0.0)
            coef = coef * jnp.where(keep, pltpu.roll(coef, s - d, 0), 1.0)
            d *= 2
        xb_v = xb_ref[...]
        da = lam_t * _shift_down(hs_v, 1)
        dmult = lam_t * (ig * xb_v)
        dig = lam_t * mult * xb_v
        dxb_ref[...] = lam_t * mult * ig
        dlog_a = da * a - dmult * (a * a) / mult
        dr = dlog_a * ((-LRU_C) * sp)
        dsp = _rows_sum(dlog_a * ((-LRU_C) * r))
        dlam_ref[...] = -dsp * _sigmoid(-lam_ref[...])
        dpa = dr * r * (1.0 - r)
        dpx = dig * ig * (1.0 - ig)
        dba_ref[...] = _rows_sum(dpa)
        dbx_ref[...] = _rows_sum(dpx)
        dpa_ref[...] = dpa.astype(BF16)
        dpx_ref[...] = dpx.astype(BF16)

    return _col_call(name, body, [dy, pre, pre, xb, proj, hs, b_a, b_x, lam],
                     [_col_spec(s, tc), _col_spec(s, tc, 0), _col_spec(s, tc, 1), _col_spec(s, tc),
                      _col_spec(s, tc, 0), _col_spec(s, tc),
                      _small_spec(1, tc), _small_spec(1, tc), _small_spec(1, tc)],
                     [(s, BF16), (s, BF16), (s, BF16), (s, F32), (1, F32), (1, F32), (1, F32)], s, tc)


def _ffn_fwd(x, w_in, w_out, g_pre, g_post, shift, scale, gate, tag, after=None):
    h = _pre_norm(x, g_pre, scale, shift, tag + "_pre", after=after)
    g, u, a = _ffn_in_act(h, w_in, tag + "_in")
    y = _mm_nn(a, w_out, tag + "_out", tn=512)
    xo = _post_norm(x, y, g_post, gate, 0.5, tag + "_post")
    return xo, (x, h, g, u, a, y)


def _ffn_bwd(dxo, saved, w_in, w_out, g_pre, g_post, scale, gate, tag, after=None):
    x, h, g, u, a, y = saved
    dy, dgate, dg_post = _post_norm_bwd(dxo, y, g_post, gate, 0.5, tag + "_post_b", after=after)
    dw_out = _mm_tn(a, dy, tag + "_out_bw", tk=D_FF // 2)
    dgu = _ffn_out_bx_act(dy, w_out, g, u, tag + "_out_bx")
    dh = _mm_nt(dgu, w_in, tag + "_in_bx")
    dw_in = _mm_tn(h, dgu, tag + "_in_bw", tn=w_in.arr.shape[-1], blocked_out=True)
    dx, dshift, dscale, dg_pre = _pre_norm_bwd(dxo, dh, x, g_pre, scale, tag + "_pre_b")
    return dx, dw_in, dw_out, (dshift, dscale, dgate), dg_pre, dg_post


def _pair_rows(v):
    return v.T.reshape(HEAD_PAIRS, 2, v.shape[0])


def _fox_fwd(h, p, tag):
    s = h.shape[0]
    proj = _mm_nn(h, p["w_in"], tag + "_in", tn=640)
    qkv = proj[:, :3 * D_MODEL].astype(BF16)
    flt = proj[:, 3 * D_MODEL:3 * D_MODEL + FOX_HEADS].T
    cum_t = _fox_gate(flt, p["b_f"], tag + "_gate")
    cum = cum_t.T
    cum_t2 = cum_t.reshape(HEAD_PAIRS, 2, s)
    o, ob, lse = _fox_attn_fwd(qkv, cum, cum_t2, tag + "_attn")
    y = _mm_nn(ob, p["w_out"], tag + "_out")
    return y, (qkv, flt, cum, cum_t2, o, ob, lse)


def _fox_bwd(dy, h, saved, p, tag):
    qkv, flt, cum, cum_t2, o, ob, lse = saved
    s = h.shape[0]
    do = _mm_nt(dy, p["w_out"], tag + "_out_bx")
    dw_out = _mm_tn(ob, dy, tag + "_out_bw")
    expand = jnp.pad(jnp.repeat(jnp.eye(FOX_HEADS, dtype=BF16), FOX_HEAD_DIM, axis=0),
                     ((0, 0), (0, PAIR_W - FOX_HEADS)))
    delta = _fox_delta(do, o, expand, tag + "_attn_delta")
    dq, dk, dv, dck, dcq = _fox_attn_bwd(qkv, do, cum, cum_t2, _pair_rows(lse), _pair_rows(delta), tag + "_attn_b")
    dcum_k = dck[:, :, :2].transpose(0, 2, 1).reshape(FOX_HEADS, s)
    dflt, db_f = _fox_gate_bwd(dcq.reshape(FOX_HEADS, s), dcum_k, flt, p["b_f"], tag + "_gate_b")
    dproj = jnp.concatenate(
        [dq, dk, dv, dflt.T, jnp.zeros((s, FOX_PAD - 3 * D_MODEL - FOX_HEADS), F32)], axis=1).astype(BF16)
    dh = _mm_nt(dproj, p["w_in"], tag + "_in_bx", tn=640)
    dw_in = _mm_tn(h, dproj, tag + "_in_bw", tn=640)
    return dh, {"w_in": dw_in, "w_out": dw_out, "b_f": db_f}


def _sconv_mix_fwd(h, p, tag):
    proj = _mm_nn(h, p["w_in"], tag + "_in")
    yb = _sconv_fwd(proj, p["conv_w"], tag + "_conv")
    y = _mm_nn(yb, p["w_out"], tag + "_out")
    return y, (proj, yb)


def _sconv_mix_bwd(dy, h, saved, p, tag):
    proj, yb = saved
    dyb = _mm_nt(dy, p["w_out"], tag + "_out_bx")
    dw_out = _mm_tn(yb, dy, tag + "_out_bw")
    db, dc, dxv, dcw = _sconv_bwd(dyb, proj, p["conv_w"], tag + "_conv_b")
    dproj = jnp.concatenate([db, dc, dxv], axis=1)
    dh = _mm_nt(dproj, p["w_in"], tag + "_in_bx")
    dw_in = _mm_tn(h, dproj, tag + "_in_bw", tn=p["w_in"].arr.shape[-1], blocked_out=True)
    return dh, {"w_in": dw_in, "w_out": dw_out, "conv_w": dcw}


def _lru_mix_fwd(h, p, tag):
    proj = _mm_nn(h, p["w_in"], tag + "_in")
    xb, xbb = _lru_conv(proj, p["conv_w"], p["conv_b"], tag + "_conv")
    pre = _mm_nn(xbb, p["w_ax"], tag + "_gates", tn=D_MODEL)
    yb, hs = _lru_scan(pre, xb, proj, p["b_a"], p["b_x"], p["lam"], tag + "_scan")
    y = _mm_nn(yb, p["w_out"], tag + "_out")
    return y, (proj, xb, xbb, pre, yb, hs)


def _diag_blocks(m):
    return jnp.stack([m[LRU_BLOCK_DIM * n:LRU_BLOCK_DIM * (n + 1), LRU_BLOCK_DIM * n:LRU_BLOCK_DIM * (n + 1)]
                      for n in range(LRU_BLOCKS)])


def _lru_mix_bwd(dy, h, saved, p, tag):
    proj, xb, xbb, pre, yb, hs = saved
    dyb = _mm_nt(dy, p["w_out"], tag + "_out_bx")
    dw_out = _mm_tn(yb, dy, tag + "_out_bw")
    dg, dpa, dpx, dxb1, dba, dbx, dlam = _lru_scan_bwd(dyb, pre, xb, proj, hs, p["b_a"], p["b_x"], p["lam"],
                                                       tag + "_scan_b")
    dpre = jnp.concatenate([dpa, dpx], axis=1)
    dxb2 = _mm_nt(dpre, p["w_ax"], tag + "_gates_bx", tn=D_MODEL)
    dw_ax = _mm_tn(xbb, dpre, tag + "_gates_bw", tn=D_MODEL)
    dx0, dcw, dcb = _lru_conv_bwd(dxb1, dxb2, proj, p["conv_w"], tag + "_conv_b")
    dproj = jnp.concatenate([dg, dx0], axis=1)
    dh = _mm_nt(dproj, p["w_in"], tag + "_in_bx")
    dw_in = _mm_tn(h, dproj, tag + "_in_bw", tn=p["w_in"].arr.shape[-1], blocked_out=True)
    grads = {"w_in": dw_in, "w_out": dw_out, "conv_w": dcw, "conv_b": dcb,
             "w_a": _diag_blocks(dw_ax[:, :D_MODEL]), "w_x": _diag_blocks(dw_ax[:, D_MODEL:]),
             "b_a": dba, "b_x": dbx, "lam": dlam}
    return dh, grads


_MIXERS = ((_fox_fwd, _fox_bwd), (_sconv_mix_fwd, _sconv_mix_bwd), (_lru_mix_fwd, _lru_mix_bwd))


def _local_step(x, target, mod, layer_params, on_grads=None, on_mid=None, first_after=None):
    layers = []
    tape = []
    for i in range(DEPTH):
        lp = dict(layer_params(i, 0, x))
        layers.append(lp)
        row = lambda v: v[None, :]
        m = lambda sub, what: mod[i, sub, what][None, :]
        x, sv0 = _ffn_fwd(x, lp["ffn_in"][0], lp["ffn_out"][0], row(lp["norm_pre"][0]), row(lp["norm_post"][0]),
                          m(0, 0), m(0, 1), m(0, 2), f"l{i}_ffn0", after=first_after if i == 0 else None)
        lp.update(layer_params(i, 1, x))
        h = _pre_norm(x, row(lp["norm_pre"][1]), m(1, 1), m(1, 0), f"l{i}_mix_pre")
        y, svm = _MIXERS[i % 3][0](h, lp["mixer"], f"l{i}_mix")
        x1 = _post_norm(x, y, row(lp["norm_post"][1]), m(1, 2), 1.0, f"l{i}_mix_post")
        second = layer_params(i, 2, x1)
        lp["ffn_in"] = lp["ffn_in"] + second["ffn_in"]
        lp["ffn_out"] = lp["ffn_out"] + second["ffn_out"]
        x2, sv2 = _ffn_fwd(x1, lp["ffn_in"][1], lp["ffn_out"][1], row(lp["norm_pre"][2]), row(lp["norm_post"][2]),
                           m(2, 0), m(2, 1), m(2, 2), f"l{i}_ffn1")
        tape.append((sv0, (x, h, y, svm), sv2))
        x = x2
    dx, loss_row = _loss_head(x, target, "loss_head")

    layer_grads = [None] * DEPTH
    dmod = [None] * DEPTH
    after = None
    for i in reversed(range(DEPTH)):
        lp = layers[i]
        row = lambda v: v[None, :]
        m = lambda sub, what: mod[i, sub, what][None, :]
        sv0, (xm, h, y, svm), sv2 = tape[i]
        dx, dw_in1, dw_out1, dm2, dgp2, dgq2 = _ffn_bwd(dx, sv2, lp["ffn_in"][1], lp["ffn_out"][1],
                                                        row(lp["norm_pre"][2]), row(lp["norm_post"][2]),
                                                        m(2, 1), m(2, 2), f"l{i}_ffn1", after=after)
        after = on_mid(i, dx) if on_mid is not None else None
        dy, dgate1, dgq1 = _post_norm_bwd(dx, y, row(lp["norm_post"][1]), m(1, 2), 1.0, f"l{i}_mix_post_b", after=after)
        dh, mg = _MIXERS[i % 3][1](dy, h, svm, lp["mixer"], f"l{i}_mix")
        dx, dshift1, dscale1, dgp1 = _pre_norm_bwd(dx, dh, xm, row(lp["norm_pre"][1]), m(1, 1), f"l{i}_mix_pre_b")
        dx, dw_in0, dw_out0, dm0, dgp0, dgq0 = _ffn_bwd(dx, sv0, lp["ffn_in"][0], lp["ffn_out"][0],
                                                        row(lp["norm_pre"][0]), row(lp["norm_post"][0]),
                                                        m(0, 1), m(0, 2), f"l{i}_ffn0")
        dmod[i] = jnp.concatenate([*dm0, dshift1, dscale1, dgate1, *dm2], axis=0).reshape(N_SUB, 3, D_MODEL)
        layer_grads[i] = {"ffn_in": (dw_in0, dw_in1), "ffn_out": (dw_out0, dw_out1),
                          "norm_pre": jnp.concatenate([dgp0, dgp1, dgp2], axis=0),
                          "norm_post": jnp.concatenate([dgq0, dgq1, dgq2], axis=0), "mixer": mg}
        if on_grads is not None:
            after = on_grads(i, layer_grads[i], dx)
    return loss_row, dx, jnp.stack(dmod), layer_grads


COND_ROWS = 16
COND_PAD = 128


def _cond_fwd(c_pad, w_cond, b_shard, name):
    nl, d, n = w_cond.shape
    tn = 768

    def body(c_ref, w_ref, b_ref, o_ref):
        cv = c_ref[...]
        act = (cv * _sigmoid(cv)).astype(BF16)
        o_ref[...] = jnp.dot(act, w_ref[...].astype(BF16), preferred_element_type=F32) + b_ref[...]

    return pl.pallas_call(
        body, name=name, grid=(nl, n // tn),
        in_specs=[pl.BlockSpec((COND_ROWS, d), lambda i, j: (0, 0)),
                  pl.BlockSpec((None, d, tn), lambda i, j: (i, 0, j)),
                  pl.BlockSpec((None, 1, tn), lambda i, j: (i, 0, j))],
        out_specs=pl.BlockSpec((None, COND_ROWS, tn), lambda i, j: (i, 0, j)),
        out_shape=jax.ShapeDtypeStruct((nl, COND_ROWS, n), F32),
        compiler_params=_cparams(("arbitrary", "arbitrary")),
    )(c_pad, w_cond, b_shard)


def _adam_math(w, g, m, v):
    nm = ADAM_B1 * m + (1.0 - ADAM_B1) * g
    nv = ADAM_B2 * v + (1.0 - ADAM_B2) * (g * g)
    m_hat = nm / (1.0 - ADAM_B1 ** ADAM_STEP)
    v_hat = nv / (1.0 - ADAM_B2 ** ADAM_STEP)
    delta = (-ADAM_LR) * (m_hat / (jnp.sqrt(v_hat) + ADAM_EPS) + ADAM_WD * w)
    return delta, nm, nv


def _cond_bwd_adamw(c_t, dmod_s, w, m, v, name):
    nl, d, n = w.shape
    tn = 384
    blk = pl.BlockSpec((None, d, tn), lambda i, j: (i, 0, j))

    def body(c_ref, dm_ref, w_ref, m_ref, v_ref, g_ref, d_ref, nm_ref, nv_ref):
        cv = c_ref[...]
        g = jnp.dot((cv * _sigmoid(cv)).astype(BF16), dm_ref[...], preferred_element_type=F32)
        g_ref[...] = g
        d_ref[...], nm_ref[...], nv_ref[...] = _adam_math(w_ref[...], g, m_ref[...], v_ref[...])

    return pl.pallas_call(
        body, name=name, grid=(nl, n // tn),
        in_specs=[pl.BlockSpec((d, COND_PAD), lambda i, j: (0, 0)),
                  pl.BlockSpec((None, COND_PAD, tn), lambda i, j: (i, 0, j)), blk, blk, blk],
        out_specs=[blk] * 4, out_shape=[jax.ShapeDtypeStruct(w.shape, F32)] * 4,
        compiler_params=_cparams(("arbitrary", "arbitrary")),
    )(c_t, dmod_s, w, m, v)


def _adamw(w, g, m, v, name):
    rows, cols = w.shape
    tr = next(t for t in (256, 176, 128, 64, 32, 16, 8) if rows % t == 0)
    blk = pl.BlockSpec((tr, cols), lambda i: (i, 0))

    def body(w_ref, g_ref, m_ref, v_ref, d_ref, nm_ref, nv_ref):
        d_ref[...], nm_ref[...], nv_ref[...] = _adam_math(w_ref[...], g_ref[...], m_ref[...], v_ref[...])

    return pl.pallas_call(
        body, name=name, grid=(rows // tr,), in_specs=[blk] * 4, out_specs=[blk] * 3,
        out_shape=[jax.ShapeDtypeStruct(w.shape, F32)] * 3, compiler_params=_cparams(("arbitrary",)),
    )(w, g, m, v)


_MESH = pl.DeviceIdType.MESH
_ANY = pl.BlockSpec(memory_space=pl.ANY)


def _place():
    return lax.axis_index("x"), lax.axis_index("y"), lax.axis_index("c")


def _other_chips(x, y):
    return [(1 - x, y), (x, 1 - y), (1 - x, 1 - y)]


def _allgather8(block, name):
    m_per, n = block.shape

    def body(x_ref, out_ref, send_sems, recv_sems, local_sem):
        x, y, c = _place()
        me, sibling = (x, y, c), (x, y, 1 - c)
        chips = _other_chips(x, y)

        def rows(px, py, pc):
            return out_ref.at[pl.ds((4 * px + 2 * py + pc) * m_per, m_per), :]

        def copy(k, blk, to, src=None):
            return pltpu.make_async_remote_copy(
                src_ref=rows(*blk) if src is None else src, dst_ref=rows(*blk),
                send_sem=send_sems.at[k], recv_sem=recv_sems.at[k], device_id=to, device_id_type=_MESH)

        mine = pltpu.make_async_copy(x_ref, rows(*me), local_sem)
        mine.start()
        first = [copy(0, me, sibling, src=x_ref)]
        first += [copy(1 + j, me, (*chip, c), src=x_ref) for j, chip in enumerate(chips)]
        for cp in first:
            cp.start()
        passed = [copy(4 + j, (*chip, c), sibling) for j, chip in enumerate(chips)]
        for j, chip in enumerate(chips):
            copy(1 + j, (*chip, c), me).wait_recv()
            passed[j].start()
        copy(0, sibling, me).wait_recv()
        for j, chip in enumerate(chips):
            copy(4 + j, (*chip, 1 - c), me).wait_recv()
        for cp in first + passed:
            cp.wait_send()
        mine.wait()

    return pl.pallas_call(
        body, name=name, out_shape=jax.ShapeDtypeStruct((N_DEV * m_per, n), block.dtype),
        in_specs=[pl.BlockSpec(memory_space=pltpu.VMEM)], out_specs=pl.BlockSpec(memory_space=pltpu.VMEM),
        scratch_shapes=[pltpu.SemaphoreType.DMA((7,)), pltpu.SemaphoreType.DMA((7,)), pltpu.SemaphoreType.DMA],
        compiler_params=_cparams(),
    )(block)


def _split_axis(shape):
    return next(a for a, n in enumerate(shape) if n > 1)


_HBM = pl.BlockSpec(memory_space=pltpu.HBM)
_SEM = pl.BlockSpec(memory_space=pltpu.SEMAPHORE)
_SPLIT_COPY = pltpu.CompilerParams(has_side_effects=pltpu.SideEffectType.DATAFLOW_SIDE_EFFECTING)
_TOKEN = jax.ShapeDtypeStruct((8, 128), F32)


def _in_hbm(arrays):
    return [pltpu.with_memory_space_constraint(a, pltpu.HBM) for a in arrays]


class _Gathered(NamedTuple):
    shard_shape: tuple
    chip_axis: int

    @property
    def shape(self):
        return self.shard_shape[:self.chip_axis] + (N_CHIPS,) + self.shard_shape[self.chip_axis:]

    def half(self, ref, chip, pc):
        cut = _split_axis(self.shard_shape)
        n = self.shard_shape[cut] // 2
        idx = [slice(None)] * len(self.shard_shape)
        idx[cut] = pl.ds(pc * n, n)
        idx.insert(self.chip_axis, chip)
        return ref.at[tuple(idx)]


def _own_block_placed(shard, layout, chip):
    return lax.dynamic_update_slice_in_dim(lax.empty(layout.shape, shard.dtype),
                                           jnp.expand_dims(shard, layout.chip_axis), chip, axis=layout.chip_axis)


def _gather_copies(lands, layouts, send_sems, recv_sems):
    x, y, c = _place()
    out = []
    for t, (land, lay) in enumerate(zip(lands, layouts)):
        for j, (px, py) in enumerate(_other_chips(x, y)):
            def copy(chip, t=t, j=j, px=px, py=py, land=land, lay=lay):
                return pltpu.make_async_remote_copy(
                    src_ref=lay.half(land, chip, c), dst_ref=lay.half(land, chip, c),
                    send_sem=send_sems.at[3 * t + j], recv_sem=recv_sems.at[3 * t + j],
                    device_id=(px, py, c), device_id_type=_MESH)
            out.append((copy(2 * x + y), copy(2 * px + py)))
    return out


def _gather_start(lands, layouts, after, name):
    nt = len(lands)
    order = [] if after is None else [after]

    def body(*refs):
        land_refs = refs[:nt]
        send_sems, recv_sems = refs[nt + len(order):nt + len(order) + 2]
        token = refs[-1]
        for send, _ in _gather_copies(land_refs, layouts, send_sems, recv_sems):
            send.start()
        token[...] = jnp.zeros_like(token)

    out = pl.pallas_call(
        body, name=name,
        out_shape=(pltpu.SemaphoreType.DMA((3 * nt,)), pltpu.SemaphoreType.DMA((3 * nt,)),
                   *[pltpu.HBM(a.shape, a.dtype) for a in lands], _TOKEN),
        in_specs=[_HBM] * nt + [_ANY] * len(order),
        out_specs=(_SEM, _SEM, *[_HBM] * nt, pl.BlockSpec(memory_space=pltpu.VMEM)),
        input_output_aliases={t: 2 + t for t in range(nt)}, compiler_params=_SPLIT_COPY,
    )(*_in_hbm(lands), *order)
    return out[0], out[1], list(out[2:2 + nt]), out[-1]


def _gather_wait(send_sems, recv_sems, lands, layouts, after, name):
    nt = len(lands)

    def body(*refs):
        land_refs = refs[:nt]
        sems = refs[nt:nt + 2]
        for send, arrival in _gather_copies(land_refs, layouts, *sems):
            send.wait_send()
            arrival.wait_recv()

    return list(pl.pallas_call(
        body, name=name, out_shape=tuple(pltpu.HBM(a.shape, a.dtype) for a in lands),
        in_specs=[_HBM] * nt + [_SEM, _SEM, _ANY], out_specs=tuple([_HBM] * nt),
        input_output_aliases={t: t for t in range(nt)}, compiler_params=_SPLIT_COPY,
    )(*lands, send_sems, recv_sems, after))


def _gather_forward(lands, layouts, name):
    nt = len(lands)

    def body(*refs):
        outs = refs[nt:2 * nt]
        send_sems, recv_sems = refs[2 * nt:]
        x, y, c = _place()
        sends, arrivals = [], []
        for t, lay in enumerate(layouts):
            for j, (px, py) in enumerate(_other_chips(x, y)):
                for pc, group in ((c, sends), (1 - c, arrivals)):
                    part = lay.half(outs[t], 2 * px + py, pc)
                    group.append(pltpu.make_async_remote_copy(
                        src_ref=part, dst_ref=part, send_sem=send_sems.at[3 * t + j], recv_sem=recv_sems.at[3 * t + j],
                        device_id=(x, y, 1 - c), device_id_type=_MESH))
        for cp in sends:
            cp.start()
        for cp in arrivals:
            cp.wait_recv()
        for cp in sends:
            cp.wait_send()

    return list(pl.pallas_call(
        body, name=name, out_shape=[jax.ShapeDtypeStruct(a.shape, a.dtype) for a in lands],
        in_specs=[_ANY] * nt, out_specs=[_ANY] * nt, input_output_aliases={t: t for t in range(nt)},
        scratch_shapes=[pltpu.SemaphoreType.DMA((3 * nt,)), pltpu.SemaphoreType.DMA((3 * nt,))],
        compiler_params=_cparams(),
    )(*lands))


def _pair_copies(grads, lands, send_sems, recv_sems):
    x, y, c = _place()
    out = []
    for t, (g, land) in enumerate(zip(grads, lands)):
        h = g.shape[1] // 2
        out.append(pltpu.make_async_remote_copy(
            src_ref=g.at[:, pl.ds((1 - c) * h, h), :], dst_ref=land, send_sem=send_sems.at[t],
            recv_sem=recv_sems.at[t], device_id=(x, y, 1 - c), device_id_type=_MESH))
    return out


def _pair_start(grads, after, name):
    nt = len(grads)
    lands = [lax.empty((N_CHIPS, g.shape[1] // 2, g.shape[2]), g.dtype) for g in grads]
    order = [] if after is None else [after]

    def body(*refs):
        send_sems, recv_sems = refs[2 * nt + len(order):2 * nt + len(order) + 2]
        token = refs[-1]
        for cp in _pair_copies(refs[:nt], refs[nt:2 * nt], send_sems, recv_sems):
            cp.start()
        token[...] = jnp.zeros_like(token)

    out = pl.pallas_call(
        body, name=name,
        out_shape=(pltpu.SemaphoreType.DMA((nt,)), pltpu.SemaphoreType.DMA((nt,)),
                   *[pltpu.HBM(a.shape, a.dtype) for a in grads + lands], _TOKEN),
        in_specs=[_HBM] * (2 * nt) + [_ANY] * len(order),
        out_specs=(_SEM, _SEM, *[_HBM] * (2 * nt), pl.BlockSpec(memory_space=pltpu.VMEM)),
        input_output_aliases={t: 2 + t for t in range(2 * nt)}, compiler_params=_SPLIT_COPY,
    )(*_in_hbm(grads + lands), *order)
    return out[0], out[1], list(out[2:2 + nt]), list(out[2 + nt:2 + 2 * nt]), out[-1]


def _pair_wait(send_sems, recv_sems, grads, lands, after, name):
    nt = len(grads)

    def body(*refs):
        for cp in _pair_copies(refs[:nt], refs[nt:2 * nt], *refs[2 * nt:2 * nt + 2]):
            cp.wait_send()
            cp.wait_recv()

    out = pl.pallas_call(
        body, name=name, out_shape=tuple(pltpu.HBM(a.shape, a.dtype) for a in grads + lands),
        in_specs=[_HBM] * (2 * nt) + [_SEM, _SEM, _ANY], out_specs=tuple([_HBM] * (2 * nt)),
        input_output_aliases={t: t for t in range(2 * nt)}, compiler_params=_SPLIT_COPY,
    )(*grads, *lands, send_sems, recv_sems, after)
    return list(out[:nt]), list(out[nt:])


def _pair_sum(own, recv, c_idx, name):
    _, h, cols = recv.shape

    def body(c_ref, own_ref, recv_ref, o_ref):
        o_ref[...] = (own_ref[...] + recv_ref[...]).astype(BF16)

    return pl.pallas_call(
        body, name=name,
        grid_spec=pltpu.PrefetchScalarGridSpec(
            num_scalar_prefetch=1, grid=(N_CHIPS,),
            in_specs=[pl.BlockSpec((None, h, cols), lambda k, c_ref: (k, c_ref[0], 0)),
                      pl.BlockSpec((None, h, cols), lambda k, c_ref: (k, 0, 0))],
            out_specs=pl.BlockSpec((None, h, cols), lambda k, c_ref: (k, 0, 0))),
        out_shape=jax.ShapeDtypeStruct(recv.shape, BF16), compiler_params=_cparams(("arbitrary",)),
    )(c_idx, own, recv)


def _chip_copies(parts, lands, send_sems, recv_sems):
    x, y, c = _place()
    out = []
    for t, (part, land) in enumerate(zip(parts, lands)):
        for j, (px, py) in enumerate(_other_chips(x, y)):
            out.append(pltpu.make_async_remote_copy(
                src_ref=part.at[2 * px + py], dst_ref=land.at[j], send_sem=send_sems.at[3 * t + j],
                recv_sem=recv_sems.at[3 * t + j], device_id=(px, py, c), device_id_type=_MESH))
    return out


def _chip_send_start(parts, after, name):
    nt = len(parts)
    lands = [lax.empty((N_CHIPS - 1,) + p.shape[1:], p.dtype) for p in parts]
    order = [] if after is None else [after]

    def body(*refs):
        send_sems, recv_sems = refs[2 * nt + len(order):2 * nt + len(order) + 2]
        token = refs[-1]
        for cp in _chip_copies(refs[:nt], refs[nt:2 * nt], send_sems, recv_sems):
            cp.start()
        token[...] = jnp.zeros_like(token)

    out = pl.pallas_call(
        body, name=name,
        out_shape=(pltpu.SemaphoreType.DMA((3 * nt,)), pltpu.SemaphoreType.DMA((3 * nt,)),
                   *[pltpu.HBM(a.shape, a.dtype) for a in parts + lands], _TOKEN),
        in_specs=[_HBM] * (2 * nt) + [_ANY] * len(order),
        out_specs=(_SEM, _SEM, *[_HBM] * (2 * nt), pl.BlockSpec(memory_space=pltpu.VMEM)),
        input_output_aliases={t: 2 + t for t in range(2 * nt)}, compiler_params=_SPLIT_COPY,
    )(*_in_hbm(parts + lands), *order)
    return out[0], out[1], list(out[2:2 + nt]), list(out[2 + nt:2 + 2 * nt]), out[-1]


def _chip_send_wait(send_sems, recv_sems, parts, lands, after, name):
    nt = len(parts)

    def body(*refs):
        for cp in _chip_copies(refs[:nt], refs[nt:2 * nt], *refs[2 * nt:2 * nt + 2]):
            cp.wait_send()
            cp.wait_recv()

    out = pl.pallas_call(
        body, name=name, out_shape=tuple(pltpu.HBM(a.shape, a.dtype) for a in parts + lands),
        in_specs=[_HBM] * (2 * nt) + [_SEM, _SEM, _ANY], out_specs=tuple([_HBM] * (2 * nt)),
        input_output_aliases={t: t for t in range(2 * nt)}, compiler_params=_SPLIT_COPY,
    )(*parts, *lands, send_sems, recv_sems, after)
    return list(out[:nt]), list(out[nt:])


def _chip_sum(part, arrived, into, lead, place_idx, name):
    _, h, cols = part.shape

    def body(idx_ref, own_ref, arr_ref, into_ref, o_ref):
        acc = own_ref[...].astype(F32)
        for k in range(N_CHIPS - 1):
            acc = acc + arr_ref[k].astype(F32)
        o_ref[...] = acc

    return pl.pallas_call(
        body, name=name,
        grid_spec=pltpu.PrefetchScalarGridSpec(
            num_scalar_prefetch=1, grid=(1,),
            in_specs=[pl.BlockSpec((None, h, cols), lambda g, idx: (idx[1], 0, 0)),
                      pl.BlockSpec((N_CHIPS - 1, h, cols), lambda g, idx: (0, 0, 0)), _ANY],
            out_specs=pl.BlockSpec((None,) * len(lead) + (h, cols), lambda g, idx: (*lead, idx[0], 0))),
        out_shape=jax.ShapeDtypeStruct(into.shape, F32), input_output_aliases={3: 0},
        compiler_params=_cparams(("arbitrary",)),
    )(place_idx, part, arrived, into)


def _pair_gather(bufs, homes, name):
    nt, nb = len(homes), len(bufs)

    def body(*refs):
        outs = refs[nb:2 * nb]
        send_sems, recv_sems = refs[2 * nb:]
        x, y, c = _place()

        def home(t, pc):
            o, lead, rows = homes[t]
            return outs[o].at[(*lead, pl.ds(pc * (rows // 2), rows // 2), slice(None))]

        def copy(t, pc):
            return pltpu.make_async_remote_copy(src_ref=home(t, pc), dst_ref=home(t, pc), send_sem=send_sems.at[t],
                                                recv_sem=recv_sems.at[t], device_id=(x, y, 1 - c), device_id_type=_MESH)

        sends = [copy(t, c) for t in range(nt)]
        for cp in sends:
            cp.start()
        for t in range(nt):
            copy(t, 1 - c).wait_recv()
        for cp in sends:
            cp.wait_send()

    return pl.pallas_call(
        body, name=name, out_shape=[jax.ShapeDtypeStruct(b.shape, b.dtype) for b in bufs],
        in_specs=[_ANY] * nb, out_specs=[_ANY] * nb, input_output_aliases={o: o for o in range(nb)},
        scratch_shapes=[pltpu.SemaphoreType.DMA((nt,)), pltpu.SemaphoreType.DMA((nt,))],
        compiler_params=_cparams(),
    )(*bufs)


def _sum_devices(g, after, name):
    def body(g_ref, after_ref, o_ref):
        acc = g_ref[0:1, :]
        for d in range(1, N_DEV):
            acc = acc + g_ref[d:d + 1, :]
        o_ref[...] = acc
    vmem = pl.BlockSpec(memory_space=pltpu.VMEM)
    return pl.pallas_call(body, name=name, out_shape=jax.ShapeDtypeStruct((1, g.shape[1]), F32),
                          in_specs=[vmem, _ANY], out_specs=vmem, compiler_params=_cparams())(g, after)


_WEIGHTS = ("w_cond", "b_cond", "norm_pre", "norm_post", "w_ffn_in", "w_ffn_out", "fox_w_in", "fox_b_f",
            "fox_w_out", "sconv_w_in", "sconv_conv_w", "sconv_w_out", "lru_w_in", "lru_conv_w", "lru_conv_b",
            "lru_w_a", "lru_b_a", "lru_w_x", "lru_b_x", "lru_lambda", "lru_w_out")
_BIG = (("w_ffn_in", False), ("w_ffn_out", True), ("fox_w_in", False), ("fox_w_out", True),
        ("sconv_w_in", False), ("sconv_w_out", True), ("lru_w_in", False), ("lru_w_out", True))
_SMALL = tuple(n for n in _WEIGHTS if n != "w_cond" and n not in dict(_BIG))
_COL_SHARDED_SMALL = ("norm_pre", "norm_post", "sconv_conv_w", "lru_conv_w", "lru_conv_b", "lru_lambda")


def _pack_rows(parts, rows=8):
    flat = jnp.concatenate([p.reshape(-1) for p in parts])
    width = -(-flat.size // (rows * 128)) * 128
    return jnp.pad(flat, (0, rows * width - flat.size)).reshape(rows, width)


def _unpack(flat, shapes):
    out, off = [], 0
    for shp in shapes:
        n = math.prod(shp)
        out.append(flat[off:off + n].reshape(shp))
        off += n
    return out


def _join_chips(g):
    g = jnp.moveaxis(g, 0, -2)
    return g.reshape(g.shape[:-2] + (g.shape[-2] * g.shape[-1],))


def _my_columns(full, chip):
    n = full.shape[-1] // N_CHIPS
    return lax.dynamic_slice_in_dim(full, chip * n, n, axis=full.ndim - 1)


def _block_diag(w):
    eye = jnp.eye(LRU_BLOCKS, dtype=w.dtype)
    return jnp.einsum("nij,nm->nimj", w, eye).reshape(D_MODEL, D_MODEL)


def _step(x, c, target, wts, mom, var):
    ix, iy, ic = _place()
    chip = 2 * ix + iy
    dev = 2 * chip + ic
    n_cond = wts["w_cond"].shape[2]

    small_shapes = [(D_MODEL,)] + [wts[n].shape for n in _COL_SHARDED_SMALL]
    g1 = _allgather8(_pack_rows([c[0]] + [wts[n] for n in _COL_SHARDED_SMALL]), "gather_small").reshape(N_DEV, -1)
    c_all = g1[:, :D_MODEL]
    per_chip = [jnp.stack(col) for col in zip(*[_unpack(g1[2 * k], small_shapes) for k in range(N_CHIPS)])]
    small_full = {n: _join_chips(v) for n, v in zip(_COL_SHARDED_SMALL, per_chip[1:])}

    c_pad = jnp.pad(c_all, ((0, COND_ROWS - N_DEV), (0, 0)))
    b_shard = _my_columns(wts["b_cond"], chip)[:, None, :]
    mod_part = _cond_fwd(c_pad, wts["w_cond"], b_shard, "cond_fwd")
    g2 = _allgather8(mod_part[:, :N_DEV].transpose(1, 0, 2).reshape(N_DEV, DEPTH * n_cond), "gather_mod")
    g2 = g2.reshape(N_DEV, N_DEV, DEPTH, n_cond)[0::2]
    mod = _join_chips(lax.dynamic_index_in_dim(g2, dev, axis=1, keepdims=False)).reshape(DEPTH, N_SUB, 3, D_MODEL)

    mixer_names = [("fox_w_in", "fox_w_out"), ("sconv_w_in", "sconv_w_out"), ("lru_w_in", "lru_w_out")]

    def shards_of(i, sub):
        if sub == 1:
            return [wts[n][i // 3] for n in mixer_names[i % 3]]
        return [wts["w_ffn_in"][i, sub // 2], wts["w_ffn_out"][i, sub // 2]]

    chunks = [[(0, 0)], [(0, 1), (0, 2)]] + [[(i, sub) for sub in range(N_SUB)] for i in range(1, DEPTH)]
    in_flight, chunk_of, token = [], {}, mod
    for k, members in enumerate(chunks):
        shards = [s for i, sub in members for s in shards_of(i, sub)]
        layouts = [_Gathered(s.shape, 0) for s in shards]
        if k:
            shards = [s + token[0, 0] for s in shards]
        lands = [_own_block_placed(s.astype(BF16), lay, chip) for s, lay in zip(shards, layouts)]
        send_sems, recv_sems, lands, token = _gather_start(lands, layouts, token, f"gather_start_{k}")
        in_flight.append([send_sems, recv_sems, lands, layouts, False])
        chunk_of.update({m: (k, 2 * pos) for pos, m in enumerate(members)})
    lru_ax = jnp.concatenate([_block_diag(wts["lru_w_a"][0]), _block_diag(wts["lru_w_x"][0])], axis=1).astype(BF16)

    def layer_params(i, sub, x_in):
        k, pos = chunk_of[(i, sub)]
        send_sems, recv_sems, lands, layouts, arrived = in_flight[k]
        if not arrived:
            lands = _gather_wait(send_sems, recv_sems, lands, layouts, x_in, f"gather_wait_{k}")
            in_flight[k][2:] = [_gather_forward(lands, layouts, f"gather_forward_{k}"), layouts, True]
        w_in, w_out = in_flight[k][2][pos:pos + 2]
        w_out = w_out.reshape(-1, w_out.shape[-1])
        if sub != 1:
            out = {"ffn_in": [_W(w_in, (), True)], "ffn_out": [_W(w_out)]}
            if sub == 0:
                out.update(norm_pre=small_full["norm_pre"][i], norm_post=small_full["norm_post"][i])
            return out
        j = i // 3
        if i % 3 == 0:
            w_in = jnp.pad(_join_chips(w_in), ((0, 0), (0, FOX_PAD - 3 * D_MODEL - FOX_HEADS)))
            return {"mixer": {"w_in": _W(w_in), "w_out": _W(w_out), "b_f": wts["fox_b_f"][j][:, None]}}
        if i % 3 == 1:
            return {"mixer": {"w_in": _W(w_in, (), True), "w_out": _W(w_out), "conv_w": small_full["sconv_conv_w"][j]}}
        return {"mixer": {"w_in": _W(w_in, (), True), "w_out": _W(w_out), "conv_w": small_full["lru_conv_w"][j],
                          "conv_b": small_full["lru_conv_b"], "w_ax": _W(lru_ax),
                          "b_a": wts["lru_b_a"].reshape(1, D_MODEL), "b_x": wts["lru_b_x"].reshape(1, D_MODEL),
                          "lam": small_full["lru_lambda"]}}

    place_idx = jnp.stack([ic, chip]).astype(jnp.int32)
    c_idx = place_idx[:1]
    big_index = {n: o for o, (n, _) in enumerate(_BIG)}
    exchanges, pending = [], []

    def to_chips(after):
        i, send_sems, recv_sems, tensors, lands, homes = pending.pop()
        tensors, recv = _pair_wait(send_sems, recv_sems, tensors, lands, after, f"grads_pair_wait_l{i}")
        parts = [_pair_sum(t, r, c_idx, f"grads_pair_sum_l{i}_{k}") for k, (t, r) in enumerate(zip(tensors, recv))]
        send_sems, recv_sems, parts, lands, tok = _chip_send_start(parts, None, f"grads_chip_start_l{i}")
        exchanges.append((i, send_sems, recv_sems, parts, lands, homes))
        return tok

    def chip_blocks(g, by_rows, width):
        if by_rows:
            return g.reshape(N_CHIPS, g.shape[0] // N_CHIPS, g.shape[1])
        if g.ndim == 3:
            return g
        return g[:, :width * N_CHIPS].reshape(g.shape[0], N_CHIPS, width).transpose(1, 0, 2)

    def on_mid(i, dx):
        return to_chips(dx) if pending else None

    def on_grads(i, g, dx):
        n_in, n_out = mixer_names[i % 3]
        items = [("w_ffn_in", (i, k), g["ffn_in"][k]) for k in range(2)]
        items += [("w_ffn_out", (i, k), g["ffn_out"][k]) for k in range(2)]
        items += [(n_in, (i // 3,), g["mixer"]["w_in"]), (n_out, (i // 3,), g["mixer"]["w_out"])]
        tensors = [chip_blocks(t, dict(_BIG)[n], wts[n].shape[-1]) for n, _, t in items]
        homes = [(big_index[n], lead, wts[n].shape[-2]) for n, lead, _ in items]
        send_sems, recv_sems, tensors, lands, tok = _pair_start(tensors, None, f"grads_pair_start_l{i}")
        pending.append((i, send_sems, recv_sems, tensors, lands, homes))
        pair_tokens.append(tok)
        return tok

    pair_tokens = []
    loss_row, grad_x, dmod, lg = _local_step(x[0], target[0], mod, layer_params, on_grads, on_mid, token)
    loss = lax.psum(loss_row[0, 0], ("x", "y", "c"))
    dmod = dmod + pair_tokens[-1][0, 0]

    fox_layers = [i for i in range(DEPTH) if i % 3 == 0]
    sconv_g, lru_g = lg[1]["mixer"], lg[2]["mixer"]
    small_g = {
        "dmod": dmod, "norm_pre": jnp.stack([g["norm_pre"] for g in lg]), "norm_post": jnp.stack([g["norm_post"] for g in lg]),
        "fox_b_f": jnp.stack([lg[i]["mixer"]["b_f"][:, 0] for i in fox_layers]),
        "sconv_conv_w": sconv_g["conv_w"][None], "lru_conv_w": lru_g["conv_w"][None], "lru_conv_b": lru_g["conv_b"],
        "lru_w_a": lru_g["w_a"][None], "lru_b_a": lru_g["b_a"].reshape(1, LRU_BLOCKS, LRU_BLOCK_DIM),
        "lru_w_x": lru_g["w_x"][None], "lru_b_x": lru_g["b_x"].reshape(1, LRU_BLOCKS, LRU_BLOCK_DIM),
        "lru_lambda": lru_g["lam"]}
    g4 = _allgather8(_pack_rows(list(small_g.values())), "gather_small_grads").reshape(N_DEV, -1)
    last_start = to_chips(g4)
    summed = _sum_devices(g4, last_start, "sum_small_grads")[0]
    summed = dict(zip(small_g, _unpack(summed, [v.shape for v in small_g.values()])))
    grads = {n: (_my_columns(summed[n], chip) if n in _COL_SHARDED_SMALL else summed[n]) for n in _SMALL if n != "b_cond"}
    grads["b_cond"] = summed["dmod"].reshape(DEPTH, N_SUB * 3 * D_MODEL)

    dmod_all = (g4[:, :dmod.size] + last_start[0, 0]).reshape(N_DEV, DEPTH, N_SUB * 3 * D_MODEL)
    dmod_s = jnp.pad(_my_columns(dmod_all, chip).transpose(1, 0, 2), ((0, 0), (0, COND_PAD - N_DEV), (0, 0))).astype(BF16)
    c_t = jnp.pad(c_all.T, ((0, 0), (0, COND_PAD - N_DEV)))
    grads["w_cond"], d_cond, m_cond, v_cond = _cond_bwd_adamw(c_t, dmod_s, wts["w_cond"], mom["w_cond"],
                                                              var["w_cond"], "cond_bwd_adamw")

    bufs = [lax.empty(wts[n].shape, F32) for n, _ in _BIG]
    all_homes = []
    for i, send_sems, recv_sems, parts, lands, homes in exchanges:
        follows = d_cond if not all_homes else bufs[0]
        parts, lands = _chip_send_wait(send_sems, recv_sems, parts, lands, follows, f"grads_chip_wait_l{i}")
        for k, (part, land, (o, lead, _)) in enumerate(zip(parts, lands, homes)):
            bufs[o] = _chip_sum(part, land, bufs[o], lead, place_idx, f"grads_chip_sum_l{i}_{k}")
        all_homes += homes
    grads.update(zip([n for n, _ in _BIG], _pair_gather(bufs, all_homes, "grads_pair_gather")))

    delta, new_m, new_v = {"w_cond": d_cond}, {"w_cond": m_cond}, {"w_cond": v_cond}
    for n, _ in _BIG:
        two_d = lambda a: a.reshape(-1, a.shape[-1])
        d, nm, nv = _adamw(two_d(wts[n]), two_d(grads[n]), two_d(mom[n]), two_d(var[n]), "adamw_" + n)
        delta[n], new_m[n], new_v[n] = (a.reshape(wts[n].shape) for a in (d, nm, nv))
    shapes = [wts[n].shape for n in _SMALL]
    packed = [_pack_rows([src[n] for n in _SMALL]) for src in (wts, grads, mom, var)]
    for dst, out in zip((delta, new_m, new_v), _adamw(*packed, "adamw_small")):
        dst.update(zip(_SMALL, _unpack(out.reshape(-1), shapes)))

    return (loss, grad_x[None], *[grads[n] for n in _WEIGHTS], *[delta[n] for n in _WEIGHTS],
            *[new_m[n] for n in _WEIGHTS], *[new_v[n] for n in _WEIGHTS])


def kernel(x, c, w_cond, b_cond, norm_pre, norm_post, w_ffn_in, w_ffn_out, fox_w_in, fox_b_f, fox_w_out, sconv_w_in, sconv_conv_w, sconv_w_out, lru_w_in, lru_conv_w, lru_conv_b, lru_w_a, lru_b_a, lru_w_x, lru_b_x, lru_lambda, lru_w_out, loss_target, m_w_cond, m_b_cond, m_norm_pre, m_norm_post, m_w_ffn_in, m_w_ffn_out, m_fox_w_in, m_fox_b_f, m_fox_w_out, m_sconv_w_in, m_sconv_conv_w, m_sconv_w_out, m_lru_w_in, m_lru_conv_w, m_lru_conv_b, m_lru_w_a, m_lru_b_a, m_lru_w_x, m_lru_b_x, m_lru_lambda, m_lru_w_out, v_w_cond, v_b_cond, v_norm_pre, v_norm_post, v_w_ffn_in, v_w_ffn_out, v_fox_w_in, v_fox_b_f, v_fox_w_out, v_sconv_w_in, v_sconv_conv_w, v_sconv_w_out, v_lru_w_in, v_lru_conv_w, v_lru_conv_b, v_lru_w_a, v_lru_b_a, v_lru_w_x, v_lru_b_x, v_lru_lambda, v_lru_w_out):
    given = dict(locals())
    wts = {n: given[n] for n in _WEIGHTS}
    mom = {n: given["m_" + n] for n in _WEIGHTS}
    var = {n: given["v_" + n] for n in _WEIGHTS}
    return _step(x, c, loss_target, wts, mom, var)
```

```python
import functools
import math
from typing import NamedTuple

import jax
import jax.numpy as jnp
from jax import lax
from jax.experimental import pallas as pl
from jax.experimental.pallas import tpu as pltpu

F32 = jnp.float32
BF16 = jnp.bfloat16

D_MODEL = 1024
DEPTH = 4
N_SUB = 3
D_FF = 2816
RMS_EPS = 1e-6
FOX_HEADS = 16
FOX_HEAD_DIM = 64
FOX_PAD = 3200
LRU_BLOCKS = 16
LRU_BLOCK_DIM = 64
LRU_C = 8.0
N_CHIPS = 4
N_DEV = 8

ADAM_LR = 0.001
ADAM_B1 = 0.9
ADAM_B2 = 0.999
ADAM_EPS = 1e-08
ADAM_WD = 0.01
ADAM_STEP = 10

VMEM_LIMIT_V7X = 56 * 1024 * 1024
ROW_TILE = 256
COL_TILE = 256
ATT_TILE = 256
ATT_WIDE = 512
MM_ROWS = 1024


def _cparams(sem=None):
    return pltpu.CompilerParams(vmem_limit_bytes=VMEM_LIMIT_V7X, dimension_semantics=sem)


def _sigmoid(z):
    return 1.0 / (1.0 + jnp.exp(-z))


def _softplus(z):
    return jnp.maximum(z, 0.0) + jnp.log(1.0 + jnp.exp(-jnp.abs(z)))


def _rows_sum(v):
    return jnp.sum(v, axis=0, keepdims=True)


class _W(NamedTuple):
    arr: jax.Array
    prefix: tuple = ()
    blocked: bool = False


def _w_spec(w, block2, pos):
    lead = (None,) * (len(w.prefix) + (1 if w.blocked else 0))
    if w.blocked:
        return pl.BlockSpec(lead + block2, lambda *g: (pos(*g)[0], *w.prefix, pos(*g)[1], pos(*g)[2]))
    return pl.BlockSpec(lead + block2, lambda *g: (*w.prefix, pos(*g)[1], pos(*g)[2]))


def _mm_nn(a, b, name, tn=None):
    m, k = a.shape
    if b.blocked:
        steps, bn = b.arr.shape[0], b.arr.shape[-1]
        b_spec = _w_spec(b, (k, bn), lambda n: (n, 0, 0))
    else:
        n_total = b.arr.shape[-1]
        bn = n_total if tn is None else tn
        steps = n_total // bn
        assert steps * bn == n_total
        b_spec = _w_spec(b, (k, bn), lambda n: (0, 0, n))
    tm = min(MM_ROWS, m)

    def body(a_ref, b_ref, o_ref):
        def step(i, carry):
            r = pl.ds(pl.multiple_of(i * tm, tm), tm)
            o_ref[r, :] = jnp.dot(a_ref[r, :], b_ref[...], preferred_element_type=F32)
            return carry
        lax.fori_loop(0, m // tm, step, 0)

    return pl.pallas_call(
        body, name=name, grid=(steps,),
        in_specs=[pl.BlockSpec((m, k), lambda n: (0, 0)), b_spec],
        out_specs=pl.BlockSpec((m, bn), lambda n: (0, n)),
        out_shape=jax.ShapeDtypeStruct((m, steps * bn), F32),
        compiler_params=_cparams(("arbitrary",)),
    )(a, b.arr)


def _mm_nt(dy, w, name, tk=None, tn=None):
    m, n_total = dy.shape
    k = w.arr.shape[-2]
    if w.blocked:
        bk, bn = k, w.arr.shape[-1]
        grid = (1, w.arr.shape[0])
        w_spec = _w_spec(w, (k, bn), lambda kt, n: (n, 0, 0))
    else:
        bk = k if tk is None else tk
        bn = n_total if tn is None else tn
        grid = (k // bk, n_total // bn)
        assert grid[0] * bk == k and grid[1] * bn == n_total
        w_spec = _w_spec(w, (bk, bn), lambda kt, n: (0, kt, n))
    tm = min(MM_ROWS, m)

    reduce_steps = grid[1]

    def body(dy_ref, w_ref, o_ref):
        def step(i, carry):
            r = pl.ds(pl.multiple_of(i * tm, tm), tm)
            part = lax.dot_general(dy_ref[r, :], w_ref[...], (((1,), (1,)), ((), ())), preferred_element_type=F32)
            if reduce_steps == 1:
                o_ref[r, :] = part
            else:
                o_ref[r, :] += part
            return carry

        if reduce_steps > 1:
            @pl.when(pl.program_id(1) == 0)
            def _():
                o_ref[...] = jnp.zeros_like(o_ref)
        lax.fori_loop(0, m // tm, step, 0)

    return pl.pallas_call(
        body, name=name, grid=grid,
        in_specs=[pl.BlockSpec((m, bn), lambda kt, n: (0, n)), w_spec],
        out_specs=pl.BlockSpec((m, bk), lambda kt, n: (0, kt)),
        out_shape=jax.ShapeDtypeStruct((m, k), F32),
        compiler_params=_cparams(("arbitrary", "arbitrary")),
    )(dy, w.arr)


def _mm_tn(x, dy, name, tk=None, tn=None, blocked_out=False):
    s, k = x.shape
    n_total = dy.shape[1]
    bk = k if tk is None else tk
    bn = n_total if tn is None else tn
    grid = (k // bk, n_total // bn)
    assert grid[0] * bk == k and grid[1] * bn == n_total
    ck = 256 if bk % 256 == 0 else 128

    def body(x_ref, dy_ref, o_ref):
        def step(i, carry):
            c = pl.ds(pl.multiple_of(i * ck, ck), ck)
            o_ref[c, :] = lax.dot_general(x_ref[:, c], dy_ref[...], (((0,), (0,)), ((), ())),
                                          preferred_element_type=F32)
            return carry
        lax.fori_loop(0, bk // ck, step, 0)

    if blocked_out:
        assert grid[0] == 1
        out_spec = pl.BlockSpec((None, bk, bn), lambda kt, n: (n, 0, 0))
        out_shape = jax.ShapeDtypeStruct((grid[1], k, bn), F32)
    else:
        out_spec = pl.BlockSpec((bk, bn), lambda kt, n: (kt, n))
        out_shape = jax.ShapeDtypeStruct((k, n_total), F32)
    return pl.pallas_call(
        body, name=name, grid=grid,
        in_specs=[pl.BlockSpec((s, bk), lambda kt, n: (0, kt)), pl.BlockSpec((s, bn), lambda kt, n: (0, n))],
        out_specs=out_spec, out_shape=out_shape,
        compiler_params=_cparams(("arbitrary", "arbitrary")),
    )(x, dy)


def _row_call(name, body, rows, fulls, row_outs, acc_outs, tr=ROW_TILE, after=None):
    s = rows[0].shape[0]
    tr = min(tr, s)
    in_specs = [pl.BlockSpec((tr, a.shape[1]), lambda i: (i, 0)) for a in rows]
    in_specs += [pl.BlockSpec(a.shape, lambda i: (0, 0)) for a in fulls]
    n_in = len(in_specs)
    order = [] if after is None else [after]
    in_specs += [pl.BlockSpec(memory_space=pl.ANY)] * len(order)
    out_specs = [pl.BlockSpec((tr, c), lambda i: (i, 0)) for c, _ in row_outs]
    out_specs += [pl.BlockSpec((1, c), lambda i: (0, 0)) for c, _ in acc_outs]
    out_shape = [jax.ShapeDtypeStruct((s, c), dt) for c, dt in row_outs]
    out_shape += [jax.ShapeDtypeStruct((1, c), dt) for c, dt in acc_outs]
    n_acc = len(acc_outs)

    def wrapped(*refs):
        refs = refs[:n_in] + refs[n_in + len(order):]
        if n_acc:
            @pl.when(pl.program_id(0) == 0)
            def _():
                for r in refs[len(refs) - n_acc:]:
                    r[...] = jnp.zeros_like(r)
        body(*refs)

    return pl.pallas_call(
        wrapped, name=name, grid=(s // tr,), in_specs=in_specs, out_specs=out_specs, out_shape=out_shape,
        compiler_params=_cparams(("arbitrary",)),
    )(*rows, *fulls, *order)


def _rms(v):
    return lax.rsqrt(jnp.mean(v * v, axis=-1, keepdims=True) + RMS_EPS)


def _pre_norm(x, g_pre, scale, shift, name, after=None):
    def body(x_ref, g_ref, sc_ref, sh_ref, h_ref):
        xv = x_ref[...]
        h = (xv * _rms(xv)) * g_ref[...] * (1.0 + sc_ref[...]) + sh_ref[...]
        h_ref[...] = h.astype(BF16)
    return _row_call(name, body, [x], [g_pre, scale, shift], [(D_MODEL, BF16)], [], after=after)[0]


def _post_norm(x, y, g_post, gate, coef, name):
    def body(x_ref, y_ref, g_ref, gate_ref, o_ref):
        yv = y_ref[...]
        o_ref[...] = x_ref[...] + (coef * gate_ref[...]) * ((yv * _rms(yv)) * g_ref[...])
    return _row_call(name, body, [x, y], [g_post, gate], [(D_MODEL, F32)], [])[0]


def _post_norm_bwd(dxo, y, g_post, gate, coef, name, after=None):
    def body(dxo_ref, y_ref, g_ref, gate_ref, dy_ref, dgate_ref, dg_ref):
        yv = y_ref[...]
        r2 = _rms(yv)
        yn = yv * r2
        dxo_v = dxo_ref[...]
        dgate_ref[...] += _rows_sum(dxo_v * (yn * g_ref[...])) * coef
        dz = dxo_v * (coef * gate_ref[...])
        dg_ref[...] += _rows_sum(dz * yn)
        dyn = dz * g_ref[...]
        dy = r2 * (dyn - yn * jnp.mean(dyn * yn, axis=-1, keepdims=True))
        dy_ref[...] = dy.astype(BF16)
    return _row_call(name, body, [dxo, y], [g_post, gate], [(D_MODEL, BF16)], [(D_MODEL, F32), (D_MODEL, F32)],
                     after=after)


def _pre_norm_bwd(dxo, dh, x, g_pre, scale, name):
    def body(dxo_ref, dh_ref, x_ref, g_ref, sc_ref, dx_ref, dshift_ref, dscale_ref, dg_ref):
        xv = x_ref[...]
        r = _rms(xv)
        xn = xv * r
        dh_v = dh_ref[...]
        one_sc = 1.0 + sc_ref[...]
        dshift_ref[...] += _rows_sum(dh_v)
        dscale_ref[...] += _rows_sum(dh_v * (xn * g_ref[...]))
        dg_ref[...] += _rows_sum(dh_v * xn * one_sc)
        dxn = dh_v * (g_ref[...] * one_sc)
        dx_ref[...] = dxo_ref[...] + r * (dxn - xn * jnp.mean(dxn * xn, axis=-1, keepdims=True))
    return _row_call(name, body, [dxo, dh, x], [g_pre, scale], [(D_MODEL, F32)],
                     [(D_MODEL, F32), (D_MODEL, F32), (D_MODEL, F32)])


FFN_COLS = 1408


def _ffn_in_act(h, w_in, name):
    m, k = h.shape
    half, bn = w_in.arr.shape[0] // 2, w_in.arr.shape[-1]
    assert bn == FFN_COLS and half * bn == D_FF
    tm = min(MM_ROWS, m)

    def body(h_ref, wg_ref, wu_ref, g_ref, u_ref, a_ref):
        g = jnp.dot(h_ref[...], wg_ref[...], preferred_element_type=F32)
        g_ref[...] = g
        u = jnp.dot(h_ref[...], wu_ref[...], preferred_element_type=F32)
        u_ref[...] = u
        a_ref[...] = (g * _sigmoid(g) * u).astype(BF16)

    tile = pl.BlockSpec((tm, bn), lambda t, i: (i, t))
    return pl.pallas_call(
        body, name=name, grid=(half, m // tm),
        in_specs=[pl.BlockSpec((tm, k), lambda t, i: (i, 0)),
                  _w_spec(w_in, (k, bn), lambda t, i: (t, 0, 0)),
                  _w_spec(w_in, (k, bn), lambda t, i: (half + t, 0, 0))],
        out_specs=[tile, tile, tile],
        out_shape=[jax.ShapeDtypeStruct((m, D_FF), F32)] * 2 + [jax.ShapeDtypeStruct((m, D_FF), BF16)],
        compiler_params=_cparams(("arbitrary", "arbitrary")),
    )(h, w_in.arr, w_in.arr)


def _ffn_out_bx_act(dy, w_out, g, u, name):
    m = dy.shape[0]
    tr = min(512, m)

    def body(dy_ref, w_ref, g_ref, u_ref, dgu_ref):
        dy_v = dy_ref[...]
        for c in range(D_FF // FFN_COLS):
            cols = slice(c * FFN_COLS, (c + 1) * FFN_COLS)
            da = lax.dot_general(dy_v, w_ref[cols, :], _NT, preferred_element_type=F32)
            gv = g_ref[:, cols]
            sg = _sigmoid(gv)
            dgu_ref[:, cols] = (da * u_ref[:, cols] * (sg * (1.0 + gv * (1.0 - sg)))).astype(BF16)
            dgu_ref[:, D_FF + c * FFN_COLS:D_FF + (c + 1) * FFN_COLS] = (da * (gv * sg)).astype(BF16)

    rows = lambda width: pl.BlockSpec((tr, width), lambda i: (i, 0))
    return pl.pallas_call(
        body, name=name, grid=(m // tr,),
        in_specs=[rows(D_MODEL), _w_spec(w_out, (D_FF, D_MODEL), lambda i: (0, 0, 0)), rows(D_FF), rows(D_FF)],
        out_specs=rows(2 * D_FF), out_shape=jax.ShapeDtypeStruct((m, 2 * D_FF), BF16),
        compiler_params=_cparams(("arbitrary",)),
    )(dy, w_out.arr, g, u)


def _loss_head(y, target, name):
    def body(y_ref, t_ref, dy_ref, loss_ref):
        e = y_ref[...] - t_ref[...]
        dy_ref[...] = e * (1.0 / D_MODEL)
        part = jnp.sum(jnp.mean(e * e, axis=-1, keepdims=True), axis=0, keepdims=True) * 0.5
        loss_ref[...] += jnp.broadcast_to(part, loss_ref.shape)
    return _row_call(name, body, [y, target], [], [(D_MODEL, F32)], [(128, F32)])


def _lane_scan(v, reverse):
    s = v.shape[1]
    lane = lax.broadcasted_iota(jnp.int32, v.shape, 1)
    d = 1
    while d < s:
        if reverse:
            v = v + jnp.where(lane < s - d, pltpu.roll(v, s - d, 1), 0.0)
        else:
            v = v + jnp.where(lane >= d, pltpu.roll(v, d, 1), 0.0)
        d *= 2
    return v


def _fox_gate(flt, b_f, name):
    def body(f_ref, b_ref, cum_ref):
        z = f_ref[...] + b_ref[...]
        cum_ref[...] = _lane_scan(-_softplus(-z), reverse=False)
    return pl.pallas_call(body, name=name, out_shape=jax.ShapeDtypeStruct(flt.shape, F32),
                          compiler_params=_cparams())(flt, b_f)


def _fox_gate_bwd(dcum_q, dcum_k, flt, b_f, name):
    def body(dq_ref, dk_ref, f_ref, b_ref, df_ref, db_ref):
        z = f_ref[...] + b_ref[...]
        df = _lane_scan(dq_ref[...] + dk_ref[...], reverse=True) * _sigmoid(-z)
        df_ref[...] = df
        db_ref[...] = jnp.sum(df, axis=1, keepdims=True)
    h = flt.shape[0]
    return pl.pallas_call(body, name=name,
                          out_shape=(jax.ShapeDtypeStruct(flt.shape, F32), jax.ShapeDtypeStruct((h, 1), F32)),
                          compiler_params=_cparams())(dcum_q, dcum_k, flt, b_f)


def _pick_head(block, h):
    lane = lax.broadcasted_iota(jnp.int32, block.shape, 1)
    return jnp.sum(jnp.where(lane == h, block, 0.0), axis=1, keepdims=True)


def _put_head(ref, col, h):
    @pl.when(h == 0)
    def _():
        ref[...] = jnp.zeros_like(ref)
    lane = lax.broadcasted_iota(jnp.int32, ref.shape, 1)
    ref[...] = jnp.where(lane == h, col, ref[...])


_NT = (((1,), (1,)), ((), ()))
_FOX_SCALE = FOX_HEAD_DIM ** -0.5


HEAD_PAIRS = FOX_HEADS // 2
PAIR_W = 2 * FOX_HEAD_DIM


def _low_half(shape):
    return lax.broadcasted_iota(jnp.int32, shape, 1) < FOX_HEAD_DIM


def _fox_attn_fwd(qkv, cum, cum_t, name):
    s = qkv.shape[0]
    t = min(ATT_TILE, s)
    wide = min(ATT_WIDE, s)

    def body(q_ref, k_ref, v_ref, cum_ref, cumt_ref, o_ref, ob_ref, lse_ref):
        i = pl.program_id(0)
        hp = pl.program_id(1)
        lo = _low_half((t, PAIR_W))
        qv = q_ref[...]
        zero = jnp.zeros_like(qv)
        q2 = (jnp.where(lo, qv, zero), jnp.where(lo, zero, qv))
        cum_v = cum_ref[...]
        cq2 = (_pick_head(cum_v, 2 * hp), _pick_head(cum_v, 2 * hp + 1))

        def step(j, carry, masked):
            ks = pl.ds(pl.multiple_of(j * wide, wide), wide)
            kj = k_ref[ks, :]
            vj = v_ref[ks, :]
            out = []
            for e in range(2):
                m, l, acc = carry[e]
                sc = lax.dot_general(q2[e], kj, _NT, preferred_element_type=F32) * _FOX_SCALE
                sc = sc + cq2[e] - cumt_ref[e:e + 1, ks]
                if masked:
                    q_pos = i * t + lax.broadcasted_iota(jnp.int32, (t, wide), 0)
                    k_pos = j * wide + lax.broadcasted_iota(jnp.int32, (t, wide), 1)
                    sc = jnp.where(k_pos <= q_pos, sc, -jnp.inf)
                m_new = jnp.maximum(m, jnp.max(sc, axis=1, keepdims=True))
                alpha = jnp.exp(m - m_new)
                p = jnp.exp(sc - m_new)
                l = alpha * l + jnp.sum(p, axis=1, keepdims=True)
                acc = alpha * acc + jnp.dot(p.astype(BF16), vj, preferred_element_type=F32)
                out.append((m_new, l, acc))
            return tuple(out)

        one = (jnp.full((t, 1), -jnp.inf, F32), jnp.zeros((t, 1), F32), jnp.zeros((t, PAIR_W), F32))
        whole = (i * t) // wide
        carry = lax.fori_loop(0, whole, lambda j, c: step(j, c, False), (one, one))
        (m0, l0, a0), (m1, l1, a1) = step(whole, carry, True)
        o = jnp.where(lo, a0 / l0, a1 / l1)
        o_ref[...] = o
        ob_ref[...] = o.astype(BF16)
        _put_head(lse_ref, m0 + jnp.log(l0), 2 * hp)
        _put_head(lse_ref, m1 + jnp.log(l1), 2 * hp + 1)

    nat_tile = pl.BlockSpec((t, FOX_HEADS), lambda i, hp: (i, 0))
    out_tile = pl.BlockSpec((t, PAIR_W), lambda i, hp: (i, hp))
    return pl.pallas_call(
        body, name=name, grid=(s // t, HEAD_PAIRS),
        in_specs=[pl.BlockSpec((t, PAIR_W), lambda i, hp: (i, hp)),
                  pl.BlockSpec((s, PAIR_W), lambda i, hp: (0, HEAD_PAIRS + hp)),
                  pl.BlockSpec((s, PAIR_W), lambda i, hp: (0, 2 * HEAD_PAIRS + hp)),
                  nat_tile, pl.BlockSpec((None, 2, s), lambda i, hp: (hp, 0, 0))],
        out_specs=[out_tile, out_tile, nat_tile],
        out_shape=[jax.ShapeDtypeStruct((s, D_MODEL), F32), jax.ShapeDtypeStruct((s, D_MODEL), BF16),
                   jax.ShapeDtypeStruct((s, FOX_HEADS), F32)],
        compiler_params=_cparams(("arbitrary", "arbitrary")),
    )(qkv, qkv, qkv, cum, cum_t)


def _fox_delta(do, o, expand, name):
    def body(do_ref, o_ref, e_ref, d_ref):
        prod = do_ref[...] * o_ref[...]
        hi = prod.astype(BF16)
        lo = (prod - hi.astype(F32)).astype(BF16)
        tot = (jnp.dot(hi, e_ref[...], preferred_element_type=F32)
               + jnp.dot(lo, e_ref[...], preferred_element_type=F32))
        d_ref[...] = tot[:, :FOX_HEADS]
    return _row_call(name, body, [do, o], [expand], [(FOX_HEADS, F32)], [])[0]


def _fox_attn_bwd(qkv, do, cum, cum_t, lse_t, delta_t, name):
    s = qkv.shape[0]
    t = min(ATT_TILE, s)
    wide = min(ATT_WIDE, s)
    nq = s // t
    tn_dims = (((0,), (0,)), ((), ()))

    def body(q_ref, k_ref, v_ref, do_ref, cum_ref, cumt_ref, lset_ref, deltat_ref,
             dq_ref, dk_ref, dv_ref, dck_ref, dcq_ref):
        hp = pl.program_id(0)
        j = pl.program_id(1)

        @pl.when(j == 0)
        def _():
            dq_ref[...] = jnp.zeros_like(dq_ref)
            dcq_ref[...] = jnp.zeros_like(dcq_ref)
        dk_ref[...] = jnp.zeros_like(dk_ref)
        dv_ref[...] = jnp.zeros_like(dv_ref)

        lo = _low_half((t, PAIR_W))
        lane = lax.broadcasted_iota(jnp.int32, (t, PAIR_W), 1)
        kv = k_ref[...]
        vv = v_ref[...]
        zero = jnp.zeros_like(kv)
        k2 = (jnp.where(lo, kv, zero), jnp.where(lo, zero, kv))
        v2 = (jnp.where(lo, vv, zero), jnp.where(lo, zero, vv))
        cum_v = cum_ref[...]
        ck2 = (_pick_head(cum_v, 2 * hp), _pick_head(cum_v, 2 * hp + 1))

        def step(i, dck, masked):
            qs = pl.ds(pl.multiple_of(i * wide, wide), wide)
            qi = q_ref[qs, :]
            do_i = do_ref[qs, :].astype(BF16)
            dv_p, dk_p, dq_p = [], [], []
            for e in range(2):
                st = lax.dot_general(k2[e], qi, _NT, preferred_element_type=F32) * _FOX_SCALE
                st = st + cumt_ref[e:e + 1, qs] - ck2[e]
                if masked:
                    k_pos = j * t + lax.broadcasted_iota(jnp.int32, (t, wide), 0)
                    q_pos = i * wide + lax.broadcasted_iota(jnp.int32, (t, wide), 1)
                    st = jnp.where(k_pos <= q_pos, st, -jnp.inf)
                pt = jnp.exp(st - lset_ref[e:e + 1, qs])
                dv_p.append(jnp.dot(pt.astype(BF16), do_i, preferred_element_type=F32))
                dpt = lax.dot_general(v2[e], do_i, _NT, preferred_element_type=F32)
                dst = pt * (dpt - deltat_ref[e:e + 1, qs])
                dsb = dst.astype(BF16)
                dk_p.append(jnp.dot(dsb, qi, preferred_element_type=F32))
                dq_p.append(lax.dot_general(dsb, kv, tn_dims, preferred_element_type=F32))
                dck = dck - jnp.where(lane == e, jnp.sum(dst, axis=1, keepdims=True), 0.0)
                dcq_ref[e:e + 1, qs] += jnp.sum(dst, axis=0, keepdims=True)
            dv_ref[...] += jnp.where(lo, dv_p[0], dv_p[1])
            dk_ref[...] += jnp.where(lo, dk_p[0], dk_p[1])
            dq_ref[qs, :] += jnp.where(_low_half((wide, PAIR_W)), dq_p[0], dq_p[1]) * _FOX_SCALE
            return dck

        first = (j * t) // wide
        dck = step(first, jnp.zeros((t, PAIR_W), F32), True)
        dck = lax.fori_loop(first + 1, s // wide, lambda i, c: step(i, c, False), dck)
        dk_ref[...] = dk_ref[...] * _FOX_SCALE
        dck_ref[...] = dck

    pair_full = lambda part: pl.BlockSpec((s, PAIR_W), lambda hp, j: (0, part * HEAD_PAIRS + hp))
    pair_tile = lambda part: pl.BlockSpec((t, PAIR_W), lambda hp, j: (j, part * HEAD_PAIRS + hp))
    rows = pl.BlockSpec((None, 2, s), lambda hp, j: (hp, 0, 0))
    return pl.pallas_call(
        body, name=name, grid=(HEAD_PAIRS, nq),
        in_specs=[pair_full(0), pair_tile(1), pair_tile(2), pair_full(0),
                  pl.BlockSpec((t, FOX_HEADS), lambda hp, j: (j, 0)), rows, rows, rows],
        out_specs=[pair_full(0), pair_tile(0), pair_tile(0),
                   pl.BlockSpec((None, t, PAIR_W), lambda hp, j: (hp, j, 0)), rows],
        out_shape=[jax.ShapeDtypeStruct((s, D_MODEL), F32)] * 3
        + [jax.ShapeDtypeStruct((HEAD_PAIRS, s, PAIR_W), F32), jax.ShapeDtypeStruct((HEAD_PAIRS, 2, s), F32)],
        compiler_params=_cparams(("arbitrary", "arbitrary")),
    )(qkv, qkv, qkv, do, cum, cum_t, lse_t, delta_t)


def _shift_down(v, d):
    row = lax.broadcasted_iota(jnp.int32, v.shape, 0)
    return jnp.where(row >= d, pltpu.roll(v, d, 0), 0.0)


def _shift_up(v, d):
    s = v.shape[0]
    row = lax.broadcasted_iota(jnp.int32, v.shape, 0)
    return jnp.where(row < s - d, pltpu.roll(v, s - d, 0), 0.0)


def _conv_taps(v, cw_ref, width):
    out = cw_ref[width - 1:width, :] * v
    for k in range(width - 1):
        out = out + cw_ref[k:k + 1, :] * _shift_down(v, width - 1 - k)
    return out


def _conv_taps_bwd(dout, v, cw_ref, dcw_ref, width):
    dv = cw_ref[width - 1:width, :] * dout
    dcw_ref[width - 1:width, :] = _rows_sum(dout * v)
    for k in range(width - 1):
        d = width - 1 - k
        dv = dv + cw_ref[k:k + 1, :] * _shift_up(dout, d)
        dcw_ref[k:k + 1, :] = _rows_sum(dout * _shift_down(v, d))
    return dv


def _col_spec(s, tc, part=0):
    off = part * (D_MODEL // tc)
    return pl.BlockSpec((s, tc), lambda c: (0, c + off))


def _small_spec(rows, tc):
    return pl.BlockSpec((rows, tc), lambda c: (0, c))


def _col_call(name, body, in_arrays, in_specs, out_rows, s, tc):
    return pl.pallas_call(
        body, name=name, grid=(D_MODEL // tc,), in_specs=in_specs,
        out_specs=[pl.BlockSpec((r, tc), lambda c: (0, c)) for r, _ in out_rows],
        out_shape=[jax.ShapeDtypeStruct((r, D_MODEL), dt) for r, dt in out_rows],
        compiler_params=_cparams(("arbitrary",)),
    )(*in_arrays)


def _sconv_fwd(proj, conv_w, name):
    s = proj.shape[0]
    tc = COL_TILE

    def body(b_ref, c_ref, x_ref, cw_ref, y_ref):
        y_ref[...] = (b_ref[...] * _conv_taps(c_ref[...] * x_ref[...], cw_ref, 3)).astype(BF16)

    return _col_call(name, body, [proj, proj, proj, conv_w],
                     [_col_spec(s, tc, 0), _col_spec(s, tc, 1), _col_spec(s, tc, 2), _small_spec(3, tc)],
                     [(s, BF16)], s, tc)[0]


def _sconv_bwd(dy, proj, conv_w, name):
    s = proj.shape[0]
    tc = COL_TILE

    def body(dy_ref, b_ref, c_ref, x_ref, cw_ref, db_ref, dc_ref, dx_ref, dcw_ref):
        w = c_ref[...] * x_ref[...]
        dy_v = dy_ref[...]
        db_ref[...] = (dy_v * _conv_taps(w, cw_ref, 3)).astype(BF16)
        dw = _conv_taps_bwd(dy_v * b_ref[...], w, cw_ref, dcw_ref, 3)
        dc_ref[...] = (dw * x_ref[...]).astype(BF16)
        dx_ref[...] = (dw * c_ref[...]).astype(BF16)

    return _col_call(name, body, [dy, proj, proj, proj, conv_w],
                     [_col_spec(s, tc), _col_spec(s, tc, 0), _col_spec(s, tc, 1), _col_spec(s, tc, 2),
                      _small_spec(3, tc)],
                     [(s, BF16), (s, BF16), (s, BF16), (3, F32)], s, tc)


def _lru_conv(proj, conv_w, conv_b, name):
    s = proj.shape[0]
    tc = COL_TILE

    def body(x_ref, cw_ref, cb_ref, xb_ref, xbb_ref):
        xb = _conv_taps(x_ref[...], cw_ref, 4) + cb_ref[...]
        xb_ref[...] = xb
        xbb_ref[...] = xb.astype(BF16)

    return _col_call(name, body, [proj, conv_w, conv_b],
                     [_col_spec(s, tc, 1), _small_spec(4, tc), _small_spec(1, tc)],
                     [(s, F32), (s, BF16)], s, tc)


def _lru_conv_bwd(dxb1, dxb2, proj, conv_w, name):
    s = proj.shape[0]
    tc = COL_TILE

    def body(d1_ref, d2_ref, x_ref, cw_ref, dx_ref, dcw_ref, dcb_ref):
        dxb = d1_ref[...] + d2_ref[...]
        dcb_ref[...] = _rows_sum(dxb)
        dx_ref[...] = _conv_taps_bwd(dxb, x_ref[...], cw_ref, dcw_ref, 4).astype(BF16)

    return _col_call(name, body, [dxb1, dxb2, proj, conv_w],
                     [_col_spec(s, tc), _col_spec(s, tc), _col_spec(s, tc, 1), _small_spec(4, tc)],
                     [(s, BF16), (4, F32), (1, F32)], s, tc)


_GELU_C = math.sqrt(2.0 / math.pi)


def _gelu_parts(g):
    inner = _GELU_C * (g + 0.044715 * g * g * g)
    th = jnp.tanh(inner)
    val = 0.5 * g * (1.0 + th)
    der = 0.5 * (1.0 + th) + 0.5 * g * (1.0 - th * th) * (_GELU_C * (1.0 + 3.0 * 0.044715 * g * g))
    return val, der


def _lru_gates(pa_ref, px_ref, ba_ref, bx_ref, lam_ref):
    r = _sigmoid(pa_ref[...] + ba_ref[...])
    ig = _sigmoid(px_ref[...] + bx_ref[...])
    sp = _softplus(-lam_ref[...])
    log_a = (-LRU_C) * r * sp
    a = jnp.exp(log_a)
    z = 2.0 * log_a
    one_m_a2 = jnp.where(z > -1e-3, -(z * (1.0 + z * (0.5 + z * (1.0 / 6.0)))), 1.0 - jnp.exp(z))
    return r, ig, sp, a, jnp.sqrt(one_m_a2)


def _lru_scan(pre, xb, proj, b_a, b_x, lam, name):
    s = xb.shape[0]
    tc = COL_TILE

    def body(pa_ref, px_ref, xb_ref, g_ref, ba_ref, bx_ref, lam_ref, y_ref, hs_ref):
        _, ig, _, a, mult = _lru_gates(pa_ref, px_ref, ba_ref, bx_ref, lam_ref)
        b = mult * (ig * xb_ref[...])
        d = 1
        while d < s:
            row = lax.broadcasted_iota(jnp.int32, a.shape, 0)
            keep = row >= d
            b = b + a * jnp.where(keep, pltpu.roll(b, d, 0), 0.0)
            a = a * jnp.where(keep, pltpu.roll(a, d, 0), 1.0)
            d *= 2
        hs_ref[...] = b
        y_ref[...] = (b * _gelu_parts(g_ref[...])[0]).astype(BF16)

    return _col_call(name, body, [pre, pre, xb, proj, b_a, b_x, lam],
                     [_col_spec(s, tc, 0), _col_spec(s, tc, 1), _col_spec(s, tc), _col_spec(s, tc, 0),
                      _small_spec(1, tc), _small_spec(1, tc), _small_spec(1, tc)],
                     [(s, BF16), (s, F32)], s, tc)


def _lru_scan_bwd(dy, pre, xb, proj, hs, b_a, b_x, lam, name):
    s = xb.shape[0]
    tc = COL_TILE

    def body(dy_ref, pa_ref, px_ref, xb_ref, g_ref, hs_ref, ba_ref, bx_ref, lam_ref,
             dg_ref, dpa_ref, dpx_ref, dxb_ref, dba_ref, dbx_ref, dlam_ref):
        r, ig, sp, a, mult = _lru_gates(pa_ref, px_ref, ba_ref, bx_ref, lam_ref)
        gl, gl_der = _gelu_parts(g_ref[...])
        dy_v = dy_ref[...]
        hs_v = hs_ref[...]
        dg_ref[...] = (dy_v * hs_v * gl_der).astype(BF16)
        lam_t = dy_v * gl
        coef = _shift_up(a, 1)
        d = 1
        while d < s:
            row = lax.broadcasted_iota(jnp.int32, coef.shape, 0)
            keep = row < s - d
            lam_t = lam_t + coef * jnp.where(keep, pltpu.roll(lam_t, s - d, 0), 0.0)
            coef = coef * jnp.where(keep, pltpu.roll(coef, s - d, 0), 1.0)
            d *= 2
        xb_v = xb_ref[...]
        da = lam_t * _shift_down(hs_v, 1)
        dmult = lam_t * (ig * xb_v)
        dig = lam_t * mult * xb_v
        dxb_ref[...] = lam_t * mult * ig
        dlog_a = da * a - dmult * (a * a) / mult
        dr = dlog_a * ((-LRU_C) * sp)
        dsp = _rows_sum(dlog_a * ((-LRU_C) * r))
        dlam_ref[...] = -dsp * _sigmoid(-lam_ref[...])
        dpa = dr * r * (1.0 - r)
        dpx = dig * ig * (1.0 - ig)
        dba_ref[...] = _rows_sum(dpa)
        dbx_ref[...] = _rows_sum(dpx)
        dpa_ref[...] = dpa.astype(BF16)
        dpx_ref[...] = dpx.astype(BF16)

    return _col_call(name, body, [dy, pre, pre, xb, proj, hs, b_a, b_x, lam],
                     [_col_spec(s, tc), _col_spec(s, tc, 0), _col_spec(s, tc, 1), _col_spec(s, tc),
                      _col_spec(s, tc, 0), _col_spec(s, tc),
                      _small_spec(1, tc), _small_spec(1, tc), _small_spec(1, tc)],
                     [(s, BF16), (s, BF16), (s, BF16), (s, F32), (1, F32), (1, F32), (1, F32)], s, tc)


def _ffn_fwd(x, w_in, w_out, g_pre, g_post, shift, scale, gate, tag, after=None):
    h = _pre_norm(x, g_pre, scale, shift, tag + "_pre", after=after)
    g, u, a = _ffn_in_act(h, w_in, tag + "_in")
    y = _mm_nn(a, w_out, tag + "_out", tn=512)
    xo = _post_norm(x, y, g_post, gate, 0.5, tag + "_post")
    return xo, (x, h, g, u, a, y)


def _ffn_bwd(dxo, saved, w_in, w_out, g_pre, g_post, scale, gate, tag, after=None):
    x, h, g, u, a, y = saved
    dy, dgate, dg_post = _post_norm_bwd(dxo, y, g_post, gate, 0.5, tag + "_post_b", after=after)
    dw_out = _mm_tn(a, dy, tag + "_out_bw", tk=D_FF // 2)
    dgu = _ffn_out_bx_act(dy, w_out, g, u, tag + "_out_bx")
    dh = _mm_nt(dgu, w_in, tag + "_in_bx")
    dw_in = _mm_tn(h, dgu, tag + "_in_bw", tn=w_in.arr.shape[-1], blocked_out=True)
    dx, dshift, dscale, dg_pre = _pre_norm_bwd(dxo, dh, x, g_pre, scale, tag + "_pre_b")
    return dx, dw_in, dw_out, (dshift, dscale, dgate), dg_pre, dg_post


def _pair_rows(v):
    return v.T.reshape(HEAD_PAIRS, 2, v.shape[0])


def _fox_fwd(h, p, tag):
    s = h.shape[0]
    proj = _mm_nn(h, p["w_in"], tag + "_in", tn=640)
    qkv = proj[:, :3 * D_MODEL].astype(BF16)
    flt = proj[:, 3 * D_MODEL:3 * D_MODEL + FOX_HEADS].T
    cum_t = _fox_gate(flt, p["b_f"], tag + "_gate")
    cum = cum_t.T
    cum_t2 = cum_t.reshape(HEAD_PAIRS, 2, s)
    o, ob, lse = _fox_attn_fwd(qkv, cum, cum_t2, tag + "_attn")
    y = _mm_nn(ob, p["w_out"], tag + "_out")
    return y, (qkv, flt, cum, cum_t2, o, ob, lse)


def _fox_bwd(dy, h, saved, p, tag):
    qkv, flt, cum, cum_t2, o, ob, lse = saved
    s = h.shape[0]
    do = _mm_nt(dy, p["w_out"], tag + "_out_bx")
    dw_out = _mm_tn(ob, dy, tag + "_out_bw")
    expand = jnp.pad(jnp.repeat(jnp.eye(FOX_HEADS, dtype=BF16), FOX_HEAD_DIM, axis=0),
                     ((0, 0), (0, PAIR_W - FOX_HEADS)))
    delta = _fox_delta(do, o, expand, tag + "_attn_delta")
    dq, dk, dv, dck, dcq = _fox_attn_bwd(qkv, do, cum, cum_t2, _pair_rows(lse), _pair_rows(delta), tag + "_attn_b")
    dcum_k = dck[:, :, :2].transpose(0, 2, 1).reshape(FOX_HEADS, s)
    dflt, db_f = _fox_gate_bwd(dcq.reshape(FOX_HEADS, s), dcum_k, flt, p["b_f"], tag + "_gate_b")
    dproj = jnp.concatenate(
        [dq, dk, dv, dflt.T, jnp.zeros((s, FOX_PAD - 3 * D_MODEL - FOX_HEADS), F32)], axis=1).astype(BF16)
    dh = _mm_nt(dproj, p["w_in"], tag + "_in_bx", tn=640)
    dw_in = _mm_tn(h, dproj, tag + "_in_bw", tn=640)
    return dh, {"w_in": dw_in, "w_out": dw_out, "b_f": db_f}


def _sconv_mix_fwd(h, p, tag):
    proj = _mm_nn(h, p["w_in"], tag + "_in")
    yb = _sconv_fwd(proj, p["conv_w"], tag + "_conv")
    y = _mm_nn(yb, p["w_out"], tag + "_out")
    return y, (proj, yb)


def _sconv_mix_bwd(dy, h, saved, p, tag):
    proj, yb = saved
    dyb = _mm_nt(dy, p["w_out"], tag + "_out_bx")
    dw_out = _mm_tn(yb, dy, tag + "_out_bw")
    db, dc, dxv, dcw = _sconv_bwd(dyb, proj, p["conv_w"], tag + "_conv_b")
    dproj = jnp.concatenate([db, dc, dxv], axis=1)
    dh = _mm_nt(dproj, p["w_in"], tag + "_in_bx")
    dw_in = _mm_tn(h, dproj, tag + "_in_bw", tn=p["w_in"].arr.shape[-1], blocked_out=True)
    return dh, {"w_in": dw_in, "w_out": dw_out, "conv_w": dcw}


def _lru_mix_fwd(h, p, tag):
    proj = _mm_nn(h, p["w_in"], tag + "_in")
    xb, xbb = _lru_conv(proj, p["conv_w"], p["conv_b"], tag + "_conv")
    pre = _mm_nn(xbb, p["w_ax"], tag + "_gates", tn=D_MODEL)
    yb, hs = _lru_scan(pre, xb, proj, p["b_a"], p["b_x"], p["lam"], tag + "_scan")
    y = _mm_nn(yb, p["w_out"], tag + "_out")
    return y, (proj, xb, xbb, pre, yb, hs)


def _diag_blocks(m):
    return jnp.stack([m[LRU_BLOCK_DIM * n:LRU_BLOCK_DIM * (n + 1), LRU_BLOCK_DIM * n:LRU_BLOCK_DIM * (n + 1)]
                      for n in range(LRU_BLOCKS)])


def _lru_mix_bwd(dy, h, saved, p, tag):
    proj, xb, xbb, pre, yb, hs = saved
    dyb = _mm_nt(dy, p["w_out"], tag + "_out_bx")
    dw_out = _mm_tn(yb, dy, tag + "_out_bw")
    dg, dpa, dpx, dxb1, dba, dbx, dlam = _lru_scan_bwd(dyb, pre, xb, proj, hs, p["b_a"], p["b_x"], p["lam"],
                                                       tag + "_scan_b")
    dpre = jnp.concatenate([dpa, dpx], axis=1)
    dxb2 = _mm_nt(dpre, p["w_ax"], tag + "_gates_bx", tn=D_MODEL)
    dw_ax = _mm_tn(xbb, dpre, tag + "_gates_bw", tn=D_MODEL)
    dx0, dcw, dcb = _lru_conv_bwd(dxb1, dxb2, proj, p["conv_w"], tag + "_conv_b")
    dproj = jnp.concatenate([dg, dx0], axis=1)
    dh = _mm_nt(dproj, p["w_in"], tag + "_in_bx")
    dw_in = _mm_tn(h, dproj, tag + "_in_bw", tn=p["w_in"].arr.shape[-1], blocked_out=True)
    grads = {"w_in": dw_in, "w_out": dw_out, "conv_w": dcw, "conv_b": dcb,
             "w_a": _diag_blocks(dw_ax[:, :D_MODEL]), "w_x": _diag_blocks(dw_ax[:, D_MODEL:]),
             "b_a": dba, "b_x": dbx, "lam": dlam}
    return dh, grads


_MIXERS = ((_fox_fwd, _fox_bwd), (_sconv_mix_fwd, _sconv_mix_bwd), (_lru_mix_fwd, _lru_mix_bwd))


def _local_step(x, target, mod, layer_params, on_grads=None, on_mid=None, first_after=None):
    layers = []
    tape = []
    for i in range(DEPTH):
        lp = dict(layer_params(i, 0, x))
        layers.append(lp)
        row = lambda v: v[None, :]
        m = lambda sub, what: mod[i, sub, what][None, :]
        x, sv0 = _ffn_fwd(x, lp["ffn_in"][0], lp["ffn_out"][0], row(lp["norm_pre"][0]), row(lp["norm_post"][0]),
                          m(0, 0), m(0, 1), m(0, 2), f"l{i}_ffn0", after=first_after if i == 0 else None)
        lp.update(layer_params(i, 1, x))
        h = _pre_norm(x, row(lp["norm_pre"][1]), m(1, 1), m(1, 0), f"l{i}_mix_pre")
        y, svm = _MIXERS[i % 3][0](h, lp["mixer"], f"l{i}_mix")
        x1 = _post_norm(x, y, row(lp["norm_post"][1]), m(1, 2), 1.0, f"l{i}_mix_post")
        second = layer_params(i, 2, x1)
        lp["ffn_in"] = lp["ffn_in"] + second["ffn_in"]
        lp["ffn_out"] = lp["ffn_out"] + second["ffn_out"]
        x2, sv2 = _ffn_fwd(x1, lp["ffn_in"][1], lp["ffn_out"][1], row(lp["norm_pre"][2]), row(lp["norm_post"][2]),
                           m(2, 0), m(2, 1), m(2, 2), f"l{i}_ffn1")
        tape.append((sv0, (x, h, y, svm), sv2))
        x = x2
    dx, loss_row = _loss_head(x, target, "loss_head")

    layer_grads = [None] * DEPTH
    dmod = [None] * DEPTH
    after = None
    for i in reversed(range(DEPTH)):
        lp = layers[i]
        row = lambda v: v[None, :]
        m = lambda sub, what: mod[i, sub, what][None, :]
        sv0, (xm, h, y, svm), sv2 = tape[i]
        dx, dw_in1, dw_out1, dm2, dgp2, dgq2 = _ffn_bwd(dx, sv2, lp["ffn_in"][1], lp["ffn_out"][1],
                                                        row(lp["norm_pre"][2]), row(lp["norm_post"][2]),
                                                        m(2, 1), m(2, 2), f"l{i}_ffn1", after=after)
        after = on_mid(i, dx) if on_mid is not None else None
        dy, dgate1, dgq1 = _post_norm_bwd(dx, y, row(lp["norm_post"][1]), m(1, 2), 1.0, f"l{i}_mix_post_b", after=after)
        dh, mg = _MIXERS[i % 3][1](dy, h, svm, lp["mixer"], f"l{i}_mix")
        dx, dshift1, dscale1, dgp1 = _pre_norm_bwd(dx, dh, xm, row(lp["norm_pre"][1]), m(1, 1), f"l{i}_mix_pre_b")
        dx, dw_in0, dw_out0, dm0, dgp0, dgq0 = _ffn_bwd(dx, sv0, lp["ffn_in"][0], lp["ffn_out"][0],
                                                        row(lp["norm_pre"][0]), row(lp["norm_post"][0]),
                                                        m(0, 1), m(0, 2), f"l{i}_ffn0")
        dmod[i] = jnp.concatenate([*dm0, dshift1, dscale1, dgate1, *dm2], axis=0).reshape(N_SUB, 3, D_MODEL)
        layer_grads[i] = {"ffn_in": (dw_in0, dw_in1), "ffn_out": (dw_out0, dw_out1),
                          "norm_pre": jnp.concatenate([dgp0, dgp1, dgp2], axis=0),
                          "norm_post": jnp.concatenate([dgq0, dgq1, dgq2], axis=0), "mixer": mg}
        if on_grads is not None:
            after = on_grads(i, layer_grads[i], dx)
    return loss_row, dx, jnp.stack(dmod), layer_grads


COND_ROWS = 16
COND_PAD = 128


def _cond_fwd(c_pad, w_cond, b_shard, name):
    nl, d, n = w_cond.shape
    tn = 768

    def body(c_ref, w_ref, b_ref, o_ref):
        cv = c_ref[...]
        act = (cv * _sigmoid(cv)).astype(BF16)
        o_ref[...] = jnp.dot(act, w_ref[...].astype(BF16), preferred_element_type=F32) + b_ref[...]

    return pl.pallas_call(
        body, name=name, grid=(nl, n // tn),
        in_specs=[pl.BlockSpec((COND_ROWS, d), lambda i, j: (0, 0)),
                  pl.BlockSpec((None, d, tn), lambda i, j: (i, 0, j)),
                  pl.BlockSpec((None, 1, tn), lambda i, j: (i, 0, j))],
        out_specs=pl.BlockSpec((None, COND_ROWS, tn), lambda i, j: (i, 0, j)),
        out_shape=jax.ShapeDtypeStruct((nl, COND_ROWS, n), F32),
        compiler_params=_cparams(("arbitrary", "arbitrary")),
    )(c_pad, w_cond, b_shard)


def _adam_math(w, g, m, v):
    nm = ADAM_B1 * m + (1.0 - ADAM_B1) * g
    nv = ADAM_B2 * v + (1.0 - ADAM_B2) * (g * g)
    m_hat = nm / (1.0 - ADAM_B1 ** ADAM_STEP)
    v_hat = nv / (1.0 - ADAM_B2 ** ADAM_STEP)
    delta = (-ADAM_LR) * (m_hat / (jnp.sqrt(v_hat) + ADAM_EPS) + ADAM_WD * w)
    return delta, nm, nv


def _cond_bwd_adamw(c_t, dmod_s, w, m, v, name):
    nl, d, n = w.shape
    tn = 384
    blk = pl.BlockSpec((None, d, tn), lambda i, j: (i, 0, j))

    def body(c_ref, dm_ref, w_ref, m_ref, v_ref, g_ref, d_ref, nm_ref, nv_ref):
        cv = c_ref[...]
        g = jnp.dot((cv * _sigmoid(cv)).astype(BF16), dm_ref[...], preferred_element_type=F32)
        g_ref[...] = g
        d_ref[...], nm_ref[...], nv_ref[...] = _adam_math(w_ref[...], g, m_ref[...], v_ref[...])

    return pl.pallas_call(
        body, name=name, grid=(nl, n // tn),
        in_specs=[pl.BlockSpec((d, COND_PAD), lambda i, j: (0, 0)),
                  pl.BlockSpec((None, COND_PAD, tn), lambda i, j: (i, 0, j)), blk, blk, blk],
        out_specs=[blk] * 4, out_shape=[jax.ShapeDtypeStruct(w.shape, F32)] * 4,
        compiler_params=_cparams(("arbitrary", "arbitrary")),
    )(c_t, dmod_s, w, m, v)


def _adamw(w, g, m, v, name):
    rows, cols = w.shape
    tr = next(t for t in (256, 176, 128, 64, 32, 16, 8) if rows % t == 0)
    blk = pl.BlockSpec((tr, cols), lambda i: (i, 0))

    def body(w_ref, g_ref, m_ref, v_ref, d_ref, nm_ref, nv_ref):
        d_ref[...], nm_ref[...], nv_ref[...] = _adam_math(w_ref[...], g_ref[...], m_ref[...], v_ref[...])

    return pl.pallas_call(
        body, name=name, grid=(rows // tr,), in_specs=[blk] * 4, out_specs=[blk] * 3,
        out_shape=[jax.ShapeDtypeStruct(w.shape, F32)] * 3, compiler_params=_cparams(("arbitrary",)),
    )(w, g, m, v)


_MESH = pl.DeviceIdType.MESH
_ANY = pl.BlockSpec(memory_space=pl.ANY)


def _place():
    return lax.axis_index("x"), lax.axis_index("y"), lax.axis_index("c")


def _other_chips(x, y):
    return [(1 - x, y), (x, 1 - y), (1 - x, 1 - y)]


def _allgather8(block, name):
    m_per, n = block.shape

    def body(x_ref, out_ref, send_sems, recv_sems, local_sem):
        x, y, c = _place()
        me, sibling = (x, y, c), (x, y, 1 - c)
        chips = _other_chips(x, y)

        def rows(px, py, pc):
            return out_ref.at[pl.ds((4 * px + 2 * py + pc) * m_per, m_per), :]

        def copy(k, blk, to, src=None):
            return pltpu.make_async_remote_copy(
                src_ref=rows(*blk) if src is None else src, dst_ref=rows(*blk),
                send_sem=send_sems.at[k], recv_sem=recv_sems.at[k], device_id=to, device_id_type=_MESH)

        mine = pltpu.make_async_copy(x_ref, rows(*me), local_sem)
        mine.start()
        first = [copy(0, me, sibling, src=x_ref)]
        first += [copy(1 + j, me, (*chip, c), src=x_ref) for j, chip in enumerate(chips)]
        for cp in first:
            cp.start()
        passed = [copy(4 + j, (*chip, c), sibling) for j, chip in enumerate(chips)]
        for j, chip in enumerate(chips):
            copy(1 + j, (*chip, c), me).wait_recv()
            passed[j].start()
        copy(0, sibling, me).wait_recv()
        for j, chip in enumerate(chips):
            copy(4 + j, (*chip, 1 - c), me).wait_recv()
        for cp in first + passed:
            cp.wait_send()
        mine.wait()

    return pl.pallas_call(
        body, name=name, out_shape=jax.ShapeDtypeStruct((N_DEV * m_per, n), block.dtype),
        in_specs=[pl.BlockSpec(memory_space=pltpu.VMEM)], out_specs=pl.BlockSpec(memory_space=pltpu.VMEM),
        scratch_shapes=[pltpu.SemaphoreType.DMA((7,)), pltpu.SemaphoreType.DMA((7,)), pltpu.SemaphoreType.DMA],
        compiler_params=_cparams(),
    )(block)


def _split_axis(shape):
    return next(a for a, n in enumerate(shape) if n > 1)


_HBM = pl.BlockSpec(memory_space=pltpu.HBM)
_SEM = pl.BlockSpec(memory_space=pltpu.SEMAPHORE)
_SPLIT_COPY = pltpu.CompilerParams(has_side_effects=pltpu.SideEffectType.DATAFLOW_SIDE_EFFECTING)
_TOKEN = jax.ShapeDtypeStruct((8, 128), F32)


def _in_hbm(arrays):
    return [pltpu.with_memory_space_constraint(a, pltpu.HBM) for a in arrays]


class _Gathered(NamedTuple):
    shard_shape: tuple
    chip_axis: int

    @property
    def shape(self):
        return self.shard_shape[:self.chip_axis] + (N_CHIPS,) + self.shard_shape[self.chip_axis:]

    def half(self, ref, chip, pc):
        cut = _split_axis(self.shard_shape)
        n = self.shard_shape[cut] // 2
        idx = [slice(None)] * len(self.shard_shape)
        idx[cut] = pl.ds(pc * n, n)
        idx.insert(self.chip_axis, chip)
        return ref.at[tuple(idx)]


def _own_block_placed(shard, layout, chip):
    return lax.dynamic_update_slice_in_dim(lax.empty(layout.shape, shard.dtype),
                                           jnp.expand_dims(shard, layout.chip_axis), chip, axis=layout.chip_axis)


def _gather_copies(lands, layouts, send_sems, recv_sems):
    x, y, c = _place()
    out = []
    for t, (land, lay) in enumerate(zip(lands, layouts)):
        for j, (px, py) in enumerate(_other_chips(x, y)):
            def copy(chip, t=t, j=j, px=px, py=py, land=land, lay=lay):
                return pltpu.make_async_remote_copy(
                    src_ref=lay.half(land, chip, c), dst_ref=lay.half(land, chip, c),
                    send_sem=send_sems.at[3 * t + j], recv_sem=recv_sems.at[3 * t + j],
                    device_id=(px, py, c), device_id_type=_MESH)
            out.append((copy(2 * x + y), copy(2 * px + py)))
    return out


def _gather_start(lands, layouts, after, name):
    nt = len(lands)
    order = [] if after is None else [after]

    def body(*refs):
        land_refs = refs[:nt]
        send_sems, recv_sems = refs[nt + len(order):nt + len(order) + 2]
        token = refs[-1]
        for send, _ in _gather_copies(land_refs, layouts, send_sems, recv_sems):
            send.start()
        token[...] = jnp.zeros_like(token)

    out = pl.pallas_call(
        body, name=name,
        out_shape=(pltpu.SemaphoreType.DMA((3 * nt,)), pltpu.SemaphoreType.DMA((3 * nt,)),
                   *[pltpu.HBM(a.shape, a.dtype) for a in lands], _TOKEN),
        in_specs=[_HBM] * nt + [_ANY] * len(order),
        out_specs=(_SEM, _SEM, *[_HBM] * nt, pl.BlockSpec(memory_space=pltpu.VMEM)),
        input_output_aliases={t: 2 + t for t in range(nt)}, compiler_params=_SPLIT_COPY,
    )(*_in_hbm(lands), *order)
    return out[0], out[1], list(out[2:2 + nt]), out[-1]


def _gather_wait(send_sems, recv_sems, lands, layouts, after, name):
    nt = len(lands)

    def body(*refs):
        land_refs = refs[:nt]
        sems = refs[nt:nt + 2]
        for send, arrival in _gather_copies(land_refs, layouts, *sems):
            send.wait_send()
            arrival.wait_recv()

    return list(pl.pallas_call(
        body, name=name, out_shape=tuple(pltpu.HBM(a.shape, a.dtype) for a in lands),
        in_specs=[_HBM] * nt + [_SEM, _SEM, _ANY], out_specs=tuple([_HBM] * nt),
        input_output_aliases={t: t for t in range(nt)}, compiler_params=_SPLIT_COPY,
    )(*lands, send_sems, recv_sems, after))


def _gather_forward(lands, layouts, name):
    nt = len(lands)

    def body(*refs):
        outs = refs[nt:2 * nt]
        send_sems, recv_sems = refs[2 * nt:]
        x, y, c = _place()
        sends, arrivals = [], []
        for t, lay in enumerate(layouts):
            for j, (px, py) in enumerate(_other_chips(x, y)):
                for pc, group in ((c, sends), (1 - c, arrivals)):
                    part = lay.half(outs[t], 2 * px + py, pc)
                    group.append(pltpu.make_async_remote_copy(
                        src_ref=part, dst_ref=part, send_sem=send_sems.at[3 * t + j], recv_sem=recv_sems.at[3 * t + j],
                        device_id=(x, y, 1 - c), device_id_type=_MESH))
        for cp in sends:
            cp.start()
        for cp in arrivals:
            cp.wait_recv()
        for cp in sends:
            cp.wait_send()

    return list(pl.pallas_call(
        body, name=name, out_shape=[jax.ShapeDtypeStruct(a.shape, a.dtype) for a in lands],
        in_specs=[_ANY] * nt, out_specs=[_ANY] * nt, input_output_aliases={t: t for t in range(nt)},
        scratch_shapes=[pltpu.SemaphoreType.DMA((3 * nt,)), pltpu.SemaphoreType.DMA((3 * nt,))],
        compiler_params=_cparams(),
    )(*lands))


def _pair_copies(grads, lands, send_sems, recv_sems):
    x, y, c = _place()
    out = []
    for t, (g, land) in enumerate(zip(grads, lands)):
        h = g.shape[1] // 2
        out.append(pltpu.make_async_remote_copy(
            src_ref=g.at[:, pl.ds((1 - c) * h, h), :], dst_ref=land, send_sem=send_sems.at[t],
            recv_sem=recv_sems.at[t], device_id=(x, y, 1 - c), device_id_type=_MESH))
    return out


def _pair_start(grads, after, name):
    nt = len(grads)
    lands = [lax.empty((N_CHIPS, g.shape[1] // 2, g.shape[2]), g.dtype) for g in grads]
    order = [] if after is None else [after]

    def body(*refs):
        send_sems, recv_sems = refs[2 * nt + len(order):2 * nt + len(order) + 2]
        token = refs[-1]
        for cp in _pair_copies(refs[:nt], refs[nt:2 * nt], send_sems, recv_sems):
            cp.start()
        token[...] = jnp.zeros_like(token)

    out = pl.pallas_call(
        body, name=name,
        out_shape=(pltpu.SemaphoreType.DMA((nt,)), pltpu.SemaphoreType.DMA((nt,)),
                   *[pltpu.HBM(a.shape, a.dtype) for a in grads + lands], _TOKEN),
        in_specs=[_HBM] * (2 * nt) + [_ANY] * len(order),
        out_specs=(_SEM, _SEM, *[_HBM] * (2 * nt), pl.BlockSpec(memory_space=pltpu.VMEM)),
        input_output_aliases={t: 2 + t for t in range(2 * nt)}, compiler_params=_SPLIT_COPY,
    )(*_in_hbm(grads + lands), *order)
    return out[0], out[1], list(out[2:2 + nt]), list(out[2 + nt:2 + 2 * nt]), out[-1]


def _pair_wait(send_sems, recv_sems, grads, lands, after, name):
    nt = len(grads)

    def body(*refs):
        for cp in _pair_copies(refs[:nt], refs[nt:2 * nt], *refs[2 * nt:2 * nt + 2]):
            cp.wait_send()
            cp.wait_recv()

    out = pl.pallas_call(
        body, name=name, out_shape=tuple(pltpu.HBM(a.shape, a.dtype) for a in grads + lands),
        in_specs=[_HBM] * (2 * nt) + [_SEM, _SEM, _ANY], out_specs=tuple([_HBM] * (2 * nt)),
        input_output_aliases={t: t for t in range(2 * nt)}, compiler_params=_SPLIT_COPY,
    )(*grads, *lands, send_sems, recv_sems, after)
    return list(out[:nt]), list(out[nt:])


def _pair_sum(own, recv, c_idx, name):
    _, h, cols = recv.shape

    def body(c_ref, own_ref, recv_ref, o_ref):
        o_ref[...] = (own_ref[...] + recv_ref[...]).astype(BF16)

    return pl.pallas_call(
        body, name=name,
        grid_spec=pltpu.PrefetchScalarGridSpec(
            num_scalar_prefetch=1, grid=(N_CHIPS,),
            in_specs=[pl.BlockSpec((None, h, cols), lambda k, c_ref: (k, c_ref[0], 0)),
                      pl.BlockSpec((None, h, cols), lambda k, c_ref: (k, 0, 0))],
            out_specs=pl.BlockSpec((None, h, cols), lambda k, c_ref: (k, 0, 0))),
        out_shape=jax.ShapeDtypeStruct(recv.shape, BF16), compiler_params=_cparams(("arbitrary",)),
    )(c_idx, own, recv)


def _chip_copies(parts, lands, send_sems, recv_sems):
    x, y, c = _place()
    out = []
    for t, (part, land) in enumerate(zip(parts, lands)):
        for j, (px, py) in enumerate(_other_chips(x, y)):
            out.append(pltpu.make_async_remote_copy(
                src_ref=part.at[2 * px + py], dst_ref=land.at[j], send_sem=send_sems.at[3 * t + j],
                recv_sem=recv_sems.at[3 * t + j], device_id=(px, py, c), device_id_type=_MESH))
    return out


def _chip_send_start(parts, after, name):
    nt = len(parts)
    lands = [lax.empty((N_CHIPS - 1,) + p.shape[1:], p.dtype) for p in parts]
    order = [] if after is None else [after]

    def body(*refs):
        send_sems, recv_sems = refs[2 * nt + len(order):2 * nt + len(order) + 2]
        token = refs[-1]
        for cp in _chip_copies(refs[:nt], refs[nt:2 * nt], send_sems, recv_sems):
            cp.start()
        token[...] = jnp.zeros_like(token)

    out = pl.pallas_call(
        body, name=name,
        out_shape=(pltpu.SemaphoreType.DMA((3 * nt,)), pltpu.SemaphoreType.DMA((3 * nt,)),
                   *[pltpu.HBM(a.shape, a.dtype) for a in parts + lands], _TOKEN),
        in_specs=[_HBM] * (2 * nt) + [_ANY] * len(order),
        out_specs=(_SEM, _SEM, *[_HBM] * (2 * nt), pl.BlockSpec(memory_space=pltpu.VMEM)),
        input_output_aliases={t: 2 + t for t in range(2 * nt)}, compiler_params=_SPLIT_COPY,
    )(*_in_hbm(parts + lands), *order)
    return out[0], out[1], list(out[2:2 + nt]), list(out[2 + nt:2 + 2 * nt]), out[-1]


def _chip_send_wait(send_sems, recv_sems, parts, lands, after, name):
    nt = len(parts)

    def body(*refs):
        for cp in _chip_copies(refs[:nt], refs[nt:2 * nt], *refs[2 * nt:2 * nt + 2]):
            cp.wait_send()
            cp.wait_recv()

    out = pl.pallas_call(
        body, name=name, out_shape=tuple(pltpu.HBM(a.shape, a.dtype) for a in parts + lands),
        in_specs=[_HBM] * (2 * nt) + [_SEM, _SEM, _ANY], out_specs=tuple([_HBM] * (2 * nt)),
        input_output_aliases={t: t for t in range(2 * nt)}, compiler_params=_SPLIT_COPY,
    )(*parts, *lands, send_sems, recv_sems, after)
    return list(out[:nt]), list(out[nt:])


def _chip_sum(part, arrived, into, lead, place_idx, name):
    _, h, cols = part.shape

    def body(idx_ref, own_ref, arr_ref, into_ref, o_ref):
        acc = own_ref[...].astype(F32)
        for k in range(N_CHIPS - 1):
            acc = acc + arr_ref[k].astype(F32)
        o_ref[...] = acc

    return pl.pallas_call(
        body, name=name,
        grid_spec=pltpu.PrefetchScalarGridSpec(
            num_scalar_prefetch=1, grid=(1,),
            in_specs=[pl.BlockSpec((None, h, cols), lambda g, idx: (idx[1], 0, 0)),
                      pl.BlockSpec((N_CHIPS - 1, h, cols), lambda g, idx: (0, 0, 0)), _ANY],
            out_specs=pl.BlockSpec((None,) * len(lead) + (h, cols), lambda g, idx: (*lead, idx[0], 0))),
        out_shape=jax.ShapeDtypeStruct(into.shape, F32), input_output_aliases={3: 0},
        compiler_params=_cparams(("arbitrary",)),
    )(place_idx, part, arrived, into)


def _pair_gather(bufs, homes, name):
    nt, nb = len(homes), len(bufs)

    def body(*refs):
        outs = refs[nb:2 * nb]
        send_sems, recv_sems = refs[2 * nb:]
        x, y, c = _place()

        def home(t, pc):
            o, lead, rows = homes[t]
            return outs[o].at[(*lead, pl.ds(pc * (rows // 2), rows // 2), slice(None))]

        def copy(t, pc):
            return pltpu.make_async_remote_copy(src_ref=home(t, pc), dst_ref=home(t, pc), send_sem=send_sems.at[t],
                                                recv_sem=recv_sems.at[t], device_id=(x, y, 1 - c), device_id_type=_MESH)

        sends = [copy(t, c) for t in range(nt)]
        for cp in sends:
            cp.start()
        for t in range(nt):
            copy(t, 1 - c).wait_recv()
        for cp in sends:
            cp.wait_send()

    return pl.pallas_call(
        body, name=name, out_shape=[jax.ShapeDtypeStruct(b.shape, b.dtype) for b in bufs],
        in_specs=[_ANY] * nb, out_specs=[_ANY] * nb, input_output_aliases={o: o for o in range(nb)},
        scratch_shapes=[pltpu.SemaphoreType.DMA((nt,)), pltpu.SemaphoreType.DMA((nt,))],
        compiler_params=_cparams(),
    )(*bufs)


def _sum_devices(g, after, name):
    def body(g_ref, after_ref, o_ref):
        acc = g_ref[0:1, :]
        for d in range(1, N_DEV):
            acc = acc + g_ref[d:d + 1, :]
        o_ref[...] = acc
    vmem = pl.BlockSpec(memory_space=pltpu.VMEM)
    return pl.pallas_call(body, name=name, out_shape=jax.ShapeDtypeStruct((1, g.shape[1]), F32),
                          in_specs=[vmem, _ANY], out_specs=vmem, compiler_params=_cparams())(g, after)


_WEIGHTS = ("w_cond", "b_cond", "norm_pre", "norm_post", "w_ffn_in", "w_ffn_out", "fox_w_in", "fox_b_f",
            "fox_w_out", "sconv_w_in", "sconv_conv_w", "sconv_w_out", "lru_w_in", "lru_conv_w", "lru_conv_b",
            "lru_w_a", "lru_b_a", "lru_w_x", "lru_b_x", "lru_lambda", "lru_w_out")
_BIG = (("w_ffn_in", False), ("w_ffn_out", True), ("fox_w_in", False), ("fox_w_out", True),
        ("sconv_w_in", False), ("sconv_w_out", True), ("lru_w_in", False), ("lru_w_out", True))
_SMALL = tuple(n for n in _WEIGHTS if n != "w_cond" and n not in dict(_BIG))
_COL_SHARDED_SMALL = ("norm_pre", "norm_post", "sconv_conv_w", "lru_conv_w", "lru_conv_b", "lru_lambda")


def _pack_rows(parts, rows=8):
    flat = jnp.concatenate([p.reshape(-1) for p in parts])
    width = -(-flat.size // (rows * 128)) * 128
    return jnp.pad(flat, (0, rows * width - flat.size)).reshape(rows, width)


def _unpack(flat, shapes):
    out, off = [], 0
    for shp in shapes:
        n = math.prod(shp)
        out.append(flat[off:off + n].reshape(shp))
        off += n
    return out


def _join_chips(g):
    g = jnp.moveaxis(g, 0, -2)
    return g.reshape(g.shape[:-2] + (g.shape[-2] * g.shape[-1],))


def _my_columns(full, chip):
    n = full.shape[-1] // N_CHIPS
    return lax.dynamic_slice_in_dim(full, chip * n, n, axis=full.ndim - 1)


def _block_diag(w):
    eye = jnp.eye(LRU_BLOCKS, dtype=w.dtype)
    return jnp.einsum("nij,nm->nimj", w, eye).reshape(D_MODEL, D_MODEL)


def _step(x, c, target, wts, mom, var):
    ix, iy, ic = _place()
    chip = 2 * ix + iy
    dev = 2 * chip + ic
    n_cond = wts["w_cond"].shape[2]

    small_shapes = [(D_MODEL,)] + [wts[n].shape for n in _COL_SHARDED_SMALL]
    g1 = _allgather8(_pack_rows([c[0]] + [wts[n] for n in _COL_SHARDED_SMALL]), "gather_small").reshape(N_DEV, -1)
    c_all = g1[:, :D_MODEL]
    per_chip = [jnp.stack(col) for col in zip(*[_unpack(g1[2 * k], small_shapes) for k in range(N_CHIPS)])]
    small_full = {n: _join_chips(v) for n, v in zip(_COL_SHARDED_SMALL, per_chip[1:])}

    c_pad = jnp.pad(c_all, ((0, COND_ROWS - N_DEV), (0, 0)))
    b_shard = _my_columns(wts["b_cond"], chip)[:, None, :]
    mod_part = _cond_fwd(c_pad, wts["w_cond"], b_shard, "cond_fwd")
    g2 = _allgather8(mod_part[:, :N_DEV].transpose(1, 0, 2).reshape(N_DEV, DEPTH * n_cond), "gather_mod")
    g2 = g2.reshape(N_DEV, N_DEV, DEPTH, n_cond)[0::2]
    mod = _join_chips(lax.dynamic_index_in_dim(g2, dev, axis=1, keepdims=False)).reshape(DEPTH, N_SUB, 3, D_MODEL)

    mixer_names = [("fox_w_in", "fox_w_out"), ("sconv_w_in", "sconv_w_out"), ("lru_w_in", "lru_w_out")]

    def shards_of(i, sub):
        if sub == 1:
            return [wts[n][i // 3] for n in mixer_names[i % 3]]
        return [wts["w_ffn_in"][i, sub // 2], wts["w_ffn_out"][i, sub // 2]]

    chunks = [[(0, 0)], [(0, 1), (0, 2)]] + [[(i, sub) for sub in range(N_SUB)] for i in range(1, DEPTH)]
    in_flight, chunk_of, token = [], {}, mod
    for k, members in enumerate(chunks):
        shards = [s for i, sub in members for s in shards_of(i, sub)]
        layouts = [_Gathered(s.shape, 0) for s in shards]
        if k:
            shards = [s + token[0, 0] for s in shards]
        lands = [_own_block_placed(s.astype(BF16), lay, chip) for s, lay in zip(shards, layouts)]
        send_sems, recv_sems, lands, token = _gather_start(lands, layouts, token, f"gather_start_{k}")
        in_flight.append([send_sems, recv_sems, lands, layouts, False])
        chunk_of.update({m: (k, 2 * pos) for pos, m in enumerate(members)})
    lru_ax = jnp.concatenate([_block_diag(wts["lru_w_a"][0]), _block_diag(wts["lru_w_x"][0])], axis=1).astype(BF16)

    def layer_params(i, sub, x_in):
        k, pos = chunk_of[(i, sub)]
        send_sems, recv_sems, lands, layouts, arrived = in_flight[k]
        if not arrived:
            lands = _gather_wait(send_sems, recv_sems, lands, layouts, x_in, f"gather_wait_{k}")
            in_flight[k][2:] = [_gather_forward(lands, layouts, f"gather_forward_{k}"), layouts, True]
        w_in, w_out = in_flight[k][2][pos:pos + 2]
        w_out = w_out.reshape(-1, w_out.shape[-1])
        if sub != 1:
            out = {"ffn_in": [_W(w_in, (), True)], "ffn_out": [_W(w_out)]}
            if sub == 0:
                out.update(norm_pre=small_full["norm_pre"][i], norm_post=small_full["norm_post"][i])
            return out
        j = i // 3
        if i % 3 == 0:
            w_in = jnp.pad(_join_chips(w_in), ((0, 0), (0, FOX_PAD - 3 * D_MODEL - FOX_HEADS)))
            return {"mixer": {"w_in": _W(w_in), "w_out": _W(w_out), "b_f": wts["fox_b_f"][j][:, None]}}
        if i % 3 == 1:
            return {"mixer": {"w_in": _W(w_in, (), True), "w_out": _W(w_out), "conv_w": small_full["sconv_conv_w"][j]}}
        return {"mixer": {"w_in": _W(w_in, (), True), "w_out": _W(w_out), "conv_w": small_full["lru_conv_w"][j],
                          "conv_b": small_full["lru_conv_b"], "w_ax": _W(lru_ax),
                          "b_a": wts["lru_b_a"].reshape(1, D_MODEL), "b_x": wts["lru_b_x"].reshape(1, D_MODEL),
                          "lam": small_full["lru_lambda"]}}

    place_idx = jnp.stack([ic, chip]).astype(jnp.int32)
    c_idx = place_idx[:1]
    big_index = {n: o for o, (n, _) in enumerate(_BIG)}
    exchanges, pending = [], []

    def to_chips(after):
        i, send_sems, recv_sems, tensors, lands, homes = pending.pop()
        tensors, recv = _pair_wait(send_sems, recv_sems, tensors, lands, after, f"grads_pair_wait_l{i}")
        parts = [_pair_sum(t, r, c_idx, f"grads_pair_sum_l{i}_{k}") for k, (t, r) in enumerate(zip(tensors, recv))]
        send_sems, recv_sems, parts, lands, tok = _chip_send_start(parts, None, f"grads_chip_start_l{i}")
        exchanges.append((i, send_sems, recv_sems, parts, lands, homes))
        return tok

    def chip_blocks(g, by_rows, width):
        if by_rows:
            return g.reshape(N_CHIPS, g.shape[0] // N_CHIPS, g.shape[1])
        if g.ndim == 3:
            return g
        return g[:, :width * N_CHIPS].reshape(g.shape[0], N_CHIPS, width).transpose(1, 0, 2)

    def on_mid(i, dx):
        return to_chips(dx) if pending else None

    def on_grads(i, g, dx):
        n_in, n_out = mixer_names[i % 3]
        items = [("w_ffn_in", (i, k), g["ffn_in"][k]) for k in range(2)]
        items += [("w_ffn_out", (i, k), g["ffn_out"][k]) for k in range(2)]
        items += [(n_in, (i // 3,), g["mixer"]["w_in"]), (n_out, (i // 3,), g["mixer"]["w_out"])]
        tensors = [chip_blocks(t, dict(_BIG)[n], wts[n].shape[-1]) for n, _, t in items]
        homes = [(big_index[n], lead, wts[n].shape[-2]) for n, lead, _ in items]
        send_sems, recv_sems, tensors, lands, tok = _pair_start(tensors, None, f"grads_pair_start_l{i}")
        pending.append((i, send_sems, recv_sems, tensors, lands, homes))
        pair_tokens.append(tok)
        return tok

    pair_tokens = []
    loss_row, grad_x, dmod, lg = _local_step(x[0], target[0], mod, layer_params, on_grads, on_mid, token)
    loss = lax.psum(loss_row[0, 0], ("x", "y", "c"))
    dmod = dmod + pair_tokens[-1][0, 0]

    fox_layers = [i for i in range(DEPTH) if i % 3 == 0]
    sconv_g, lru_g = lg[1]["mixer"], lg[2]["mixer"]
    small_g = {
        "dmod": dmod, "norm_pre": jnp.stack([g["norm_pre"] for g in lg]), "norm_post": jnp.stack([g["norm_post"] for g in lg]),
        "fox_b_f": jnp.stack([lg[i]["mixer"]["b_f"][:, 0] for i in fox_layers]),
        "sconv_conv_w": sconv_g["conv_w"][None], "lru_conv_w": lru_g["conv_w"][None], "lru_conv_b": lru_g["conv_b"],
        "lru_w_a": lru_g["w_a"][None], "lru_b_a": lru_g["b_a"].reshape(1, LRU_BLOCKS, LRU_BLOCK_DIM),
        "lru_w_x": lru_g["w_x"][None], "lru_b_x": lru_g["b_x"].reshape(1, LRU_BLOCKS, LRU_BLOCK_DIM),
        "lru_lambda": lru_g["lam"]}
    g4 = _allgather8(_pack_rows(list(small_g.values())), "gather_small_grads").reshape(N_DEV, -1)
    last_start = to_chips(g4)
    summed = _sum_devices(g4, last_start, "sum_small_grads")[0]
    summed = dict(zip(small_g, _unpack(summed, [v.shape for v in small_g.values()])))
    grads = {n: (_my_columns(summed[n], chip) if n in _COL_SHARDED_SMALL else summed[n]) for n in _SMALL if n != "b_cond"}
    grads["b_cond"] = summed["dmod"].reshape(DEPTH, N_SUB * 3 * D_MODEL)

    dmod_all = (g4[:, :dmod.size] + last_start[0, 0]).reshape(N_DEV, DEPTH, N_SUB * 3 * D_MODEL)
    dmod_s = jnp.pad(_my_columns(dmod_all, chip).transpose(1, 0, 2), ((0, 0), (0, COND_PAD - N_DEV), (0, 0))).astype(BF16)
    c_t = jnp.pad(c_all.T, ((0, 0), (0, COND_PAD - N_DEV)))
    grads["w_cond"], d_cond, m_cond, v_cond = _cond_bwd_adamw(c_t, dmod_s, wts["w_cond"], mom["w_cond"],
                                                              var["w_cond"], "cond_bwd_adamw")

    bufs = [lax.empty(wts[n].shape, F32) for n, _ in _BIG]
    all_homes = []
    for i, send_sems, recv_sems, parts, lands, homes in exchanges:
        follows = d_cond if not all_homes else bufs[0]
        parts, lands = _chip_send_wait(send_sems, recv_sems, parts, lands, follows, f"grads_chip_wait_l{i}")
        for k, (part, land, (o, lead, _)) in enumerate(zip(parts, lands, homes)):
            bufs[o] = _chip_sum(part, land, bufs[o], lead, place_idx, f"grads_chip_sum_l{i}_{k}")
        all_homes += homes
    grads.update(zip([n for n, _ in _BIG], _pair_gather(bufs, all_homes, "grads_pair_gather")))

    delta, new_m, new_v = {"w_cond": d_cond}, {"w_cond": m_cond}, {"w_cond": v_cond}
    for n, _ in _BIG:
        two_d = lambda a: a.reshape(-1, a.shape[-1])
        d, nm, nv = _adamw(two_d(wts[n]), two_d(grads[n]), two_d(mom[n]), two_d(var[n]), "adamw_" + n)
        delta[n], new_m[n], new_v[n] = (a.reshape(wts[n].shape) for a in (d, nm, nv))
    shapes = [wts[n].shape for n in _SMALL]
    packed = [_pack_rows([src[n] for n in _SMALL]) for src in (wts, grads, mom, var)]
    for dst, out in zip((delta, new_m, new_v), _adamw(*packed, "adamw_small")):
        dst.update(zip(_SMALL, _unpack(out.reshape(-1), shapes)))

    return (loss, grad_x[None], *[grads[n] for n in _WEIGHTS], *[delta[n] for n in _WEIGHTS],
            *[new_m[n] for n in _WEIGHTS], *[new_v[n] for n in _WEIGHTS])


def kernel(x, c, w_cond, b_cond, norm_pre, norm_post, w_ffn_in, w_ffn_out, fox_w_in, fox_b_f, fox_w_out, sconv_w_in, sconv_conv_w, sconv_w_out, lru_w_in, lru_conv_w, lru_conv_b, lru_w_a, lru_b_a, lru_w_x, lru_b_x, lru_lambda, lru_w_out, loss_target, m_w_cond, m_b_cond, m_norm_pre, m_norm_post, m_w_ffn_in, m_w_ffn_out, m_fox_w_in, m_fox_b_f, m_fox_w_out, m_sconv_w_in, m_sconv_conv_w, m_sconv_w_out, m_lru_w_in, m_lru_conv_w, m_lru_conv_b, m_lru_w_a, m_lru_b_a, m_lru_w_x, m_lru_b_x, m_lru_lambda, m_lru_w_out, v_w_cond, v_b_cond, v_norm_pre, v_norm_post, v_w_ffn_in, v_w_ffn_out, v_fox_w_in, v_fox_b_f, v_fox_w_out, v_sconv_w_in, v_sconv_conv_w, v_sconv_w_out, v_lru_w_in, v_lru_conv_w, v_lru_conv_b, v_lru_w_a, v_lru_b_a, v_lru_w_x, v_lru_b_x, v_lru_lambda, v_lru_w_out):
    given = dict(locals())
    wts = {n: given[n] for n in _WEIGHTS}
    mom = {n: given["m_" + n] for n in _WEIGHTS}
    var = {n: given["v_" + n] for n in _WEIGHTS}
    return _step(x, c, loss_target, wts, mom, var)
```

```python
import functools
import math
from typing import NamedTuple

import jax
import jax.numpy as jnp
from jax import lax
from jax.experimental import pallas as pl
from jax.experimental.pallas import tpu as pltpu

F32 = jnp.float32
BF16 = jnp.bfloat16

D_MODEL = 1024
DEPTH = 4
N_SUB = 3
D_FF = 2816
RMS_EPS = 1e-6
FOX_HEADS = 16
FOX_HEAD_DIM = 64
FOX_PAD = 3200
LRU_BLOCKS = 16
LRU_BLOCK_DIM = 64
LRU_C = 8.0
N_CHIPS = 4
N_DEV = 8

ADAM_LR = 0.001
ADAM_B1 = 0.9
ADAM_B2 = 0.999
ADAM_EPS = 1e-08
ADAM_WD = 0.01
ADAM_STEP = 10

VMEM_LIMIT_V7X = 56 * 1024 * 1024
ROW_TILE = 512
COL_TILE = 256
ATT_TILE = 256
ATT_WIDE = 512
MM_ROWS = 1024


def _cparams(sem=None):
    return pltpu.CompilerParams(vmem_limit_bytes=VMEM_LIMIT_V7X, dimension_semantics=sem)


def _sigmoid(z):
    return 1.0 / (1.0 + jnp.exp(-z))


def _softplus(z):
    return jnp.maximum(z, 0.0) + jnp.log(1.0 + jnp.exp(-jnp.abs(z)))


def _rows_sum(v):
    return jnp.sum(v, axis=0, keepdims=True)


class _W(NamedTuple):
    arr: jax.Array
    prefix: tuple = ()
    blocked: bool = False


def _w_spec(w, block2, pos):
    lead = (None,) * (len(w.prefix) + (1 if w.blocked else 0))
    if w.blocked:
        return pl.BlockSpec(lead + block2, lambda *g: (pos(*g)[0], *w.prefix, pos(*g)[1], pos(*g)[2]))
    return pl.BlockSpec(lead + block2, lambda *g: (*w.prefix, pos(*g)[1], pos(*g)[2]))


def _mm_nn(a, b, name, tn=None):
    m, k = a.shape
    if b.blocked:
        steps, bn = b.arr.shape[0], b.arr.shape[-1]
        b_spec = _w_spec(b, (k, bn), lambda n: (n, 0, 0))
    else:
        n_total = b.arr.shape[-1]
        bn = n_total if tn is None else tn
        steps = n_total // bn
        assert steps * bn == n_total
        b_spec = _w_spec(b, (k, bn), lambda n: (0, 0, n))
    tm = min(MM_ROWS, m)

    def body(a_ref, b_ref, o_ref):
        def step(i, carry):
            r = pl.ds(pl.multiple_of(i * tm, tm), tm)
            o_ref[r, :] = jnp.dot(a_ref[r, :], b_ref[...], preferred_element_type=F32)
            return carry
        lax.fori_loop(0, m // tm, step, 0)

    return pl.pallas_call(
        body, name=name, grid=(steps,),
        in_specs=[pl.BlockSpec((m, k), lambda n: (0, 0)), b_spec],
        out_specs=pl.BlockSpec((m, bn), lambda n: (0, n)),
        out_shape=jax.ShapeDtypeStruct((m, steps * bn), F32),
        compiler_params=_cparams(("arbitrary",)),
    )(a, b.arr)


def _cols_shape(dy):
    return (dy.shape[0], dy.shape[1]) if dy.ndim == 2 else (dy.shape[1], 2 * dy.shape[2])


def _cols_spec(dy, bn):
    if dy.ndim == 2:
        return pl.BlockSpec((dy.shape[0], bn), lambda kt, n: (0, n))
    per = dy.shape[2] // bn
    assert per * bn == dy.shape[2]
    return pl.BlockSpec((None, dy.shape[1], bn), lambda kt, n: (n // per, 0, n % per))


def _mm_nt(dy, w, name, tk=None, tn=None):
    m, n_total = _cols_shape(dy)
    k = w.arr.shape[-2]
    if w.blocked:
        bk, bn = k, w.arr.shape[-1]
        grid = (1, w.arr.shape[0])
        w_spec = _w_spec(w, (k, bn), lambda kt, n: (n, 0, 0))
    else:
        bk = k if tk is None else tk
        bn = n_total if tn is None else tn
        grid = (k // bk, n_total // bn)
        assert grid[0] * bk == k and grid[1] * bn == n_total
        w_spec = _w_spec(w, (bk, bn), lambda kt, n: (0, kt, n))
    tm = min(MM_ROWS, m)

    reduce_steps = grid[1]

    def body(dy_ref, w_ref, o_ref):
        def step(i, carry):
            r = pl.ds(pl.multiple_of(i * tm, tm), tm)
            part = lax.dot_general(dy_ref[r, :], w_ref[...], (((1,), (1,)), ((), ())), preferred_element_type=F32)
            if reduce_steps == 1:
                o_ref[r, :] = part
            else:
                o_ref[r, :] += part
            return carry

        if reduce_steps > 1:
            @pl.when(pl.program_id(1) == 0)
            def _():
                o_ref[...] = jnp.zeros_like(o_ref)
        lax.fori_loop(0, m // tm, step, 0)

    return pl.pallas_call(
        body, name=name, grid=grid,
        in_specs=[_cols_spec(dy, bn), w_spec],
        out_specs=pl.BlockSpec((m, bk), lambda kt, n: (0, kt)),
        out_shape=jax.ShapeDtypeStruct((m, k), F32),
        compiler_params=_cparams(("arbitrary", "arbitrary")),
    )(dy, w.arr)


def _mm_tn(x, dy, name, tk=None, tn=None, blocked_out=False):
    s, k = x.shape
    n_total = _cols_shape(dy)[1]
    bk = k if tk is None else tk
    bn = n_total if tn is None else tn
    grid = (k // bk, n_total // bn)
    assert grid[0] * bk == k and grid[1] * bn == n_total
    ck = next(c for c in (512, 256, 128) if bk % c == 0)

    def body(x_ref, dy_ref, o_ref):
        def step(i, carry):
            c = pl.ds(pl.multiple_of(i * ck, ck), ck)
            o_ref[c, :] = lax.dot_general(x_ref[:, c], dy_ref[...], (((0,), (0,)), ((), ())),
                                          preferred_element_type=F32)
            return carry
        lax.fori_loop(0, bk // ck, step, 0)

    if blocked_out:
        assert grid[0] == 1
        out_spec = pl.BlockSpec((None, bk, bn), lambda kt, n: (n, 0, 0))
        out_shape = jax.ShapeDtypeStruct((grid[1], k, bn), F32)
    else:
        out_spec = pl.BlockSpec((bk, bn), lambda kt, n: (kt, n))
        out_shape = jax.ShapeDtypeStruct((k, n_total), F32)
    return pl.pallas_call(
        body, name=name, grid=grid,
        in_specs=[pl.BlockSpec((s, bk), lambda kt, n: (0, kt)), _cols_spec(dy, bn)],
        out_specs=out_spec, out_shape=out_shape,
        compiler_params=_cparams(("arbitrary", "arbitrary")),
    )(x, dy)


def _row_call(name, body, rows, fulls, row_outs, acc_outs, tr=ROW_TILE, after=None):
    s = rows[0].shape[0]
    tr = min(tr, s)
    in_specs = [pl.BlockSpec((tr, a.shape[1]), lambda i: (i, 0)) for a in rows]
    in_specs += [pl.BlockSpec(a.shape, lambda i: (0, 0)) for a in fulls]
    n_in = len(in_specs)
    order = [] if after is None else [after]
    in_specs += [pl.BlockSpec(memory_space=pl.ANY)] * len(order)
    out_specs = [pl.BlockSpec((tr, c), lambda i: (i, 0)) for c, _ in row_outs]
    out_specs += [pl.BlockSpec((1, c), lambda i: (0, 0)) for c, _ in acc_outs]
    out_shape = [jax.ShapeDtypeStruct((s, c), dt) for c, dt in row_outs]
    out_shape += [jax.ShapeDtypeStruct((1, c), dt) for c, dt in acc_outs]
    n_acc = len(acc_outs)

    def wrapped(*refs):
        refs = refs[:n_in] + refs[n_in + len(order):]
        if n_acc:
            @pl.when(pl.program_id(0) == 0)
            def _():
                for r in refs[len(refs) - n_acc:]:
                    r[...] = jnp.zeros_like(r)
        body(*refs)

    return pl.pallas_call(
        wrapped, name=name, grid=(s // tr,), in_specs=in_specs, out_specs=out_specs, out_shape=out_shape,
        compiler_params=_cparams(("arbitrary",)),
    )(*rows, *fulls, *order)


def _rms(v):
    return lax.rsqrt(jnp.mean(v * v, axis=-1, keepdims=True) + RMS_EPS)


def _pre_norm(x, g_pre, scale, shift, name, after=None):
    def body(x_ref, g_ref, sc_ref, sh_ref, h_ref):
        xv = x_ref[...]
        h = (xv * _rms(xv)) * g_ref[...] * (1.0 + sc_ref[...]) + sh_ref[...]
        h_ref[...] = h.astype(BF16)
    return _row_call(name, body, [x], [g_pre, scale, shift], [(D_MODEL, BF16)], [], after=after)[0]


def _post_norm(x, y, g_post, gate, coef, name):
    def body(x_ref, y_ref, g_ref, gate_ref, o_ref):
        yv = y_ref[...]
        o_ref[...] = x_ref[...] + (coef * gate_ref[...]) * ((yv * _rms(yv)) * g_ref[...])
    return _row_call(name, body, [x, y], [g_post, gate], [(D_MODEL, F32)], [])[0]


def _post_norm_bwd(dxo, y, g_post, gate, coef, name, after=None):
    def body(dxo_ref, y_ref, g_ref, gate_ref, dy_ref, dgate_ref, dg_ref):
        yv = y_ref[...]
        r2 = _rms(yv)
        yn = yv * r2
        dxo_v = dxo_ref[...]
        dgate_ref[...] += _rows_sum(dxo_v * (yn * g_ref[...])) * coef
        dz = dxo_v * (coef * gate_ref[...])
        dg_ref[...] += _rows_sum(dz * yn)
        dyn = dz * g_ref[...]
        dy = r2 * (dyn - yn * jnp.mean(dyn * yn, axis=-1, keepdims=True))
        dy_ref[...] = dy.astype(BF16)
    return _row_call(name, body, [dxo, y], [g_post, gate], [(D_MODEL, BF16)], [(D_MODEL, F32), (D_MODEL, F32)],
                     after=after)


def _pre_norm_bwd(dxo, dh, x, g_pre, scale, name):
    def body(dxo_ref, dh_ref, x_ref, g_ref, sc_ref, dx_ref, dshift_ref, dscale_ref, dg_ref):
        xv = x_ref[...]
        r = _rms(xv)
        xn = xv * r
        dh_v = dh_ref[...]
        one_sc = 1.0 + sc_ref[...]
        dshift_ref[...] += _rows_sum(dh_v)
        dscale_ref[...] += _rows_sum(dh_v * (xn * g_ref[...]))
        dg_ref[...] += _rows_sum(dh_v * xn * one_sc)
        dxn = dh_v * (g_ref[...] * one_sc)
        dx_ref[...] = dxo_ref[...] + r * (dxn - xn * jnp.mean(dxn * xn, axis=-1, keepdims=True))
    return _row_call(name, body, [dxo, dh, x], [g_pre, scale], [(D_MODEL, F32)],
                     [(D_MODEL, F32), (D_MODEL, F32), (D_MODEL, F32)])


FFN_COLS = 1408


def _ffn_in_act(h, w_in, name):
    m, k = h.shape
    half, bn = w_in.arr.shape[0] // 2, w_in.arr.shape[-1]
    assert bn == FFN_COLS and half * bn == D_FF
    tm = min(MM_ROWS, m)

    def body(h_ref, wg_ref, wu_ref, g_ref, u_ref, a_ref):
        g = jnp.dot(h_ref[...], wg_ref[...], preferred_element_type=F32)
        g_ref[...] = g
        u = jnp.dot(h_ref[...], wu_ref[...], preferred_element_type=F32)
        u_ref[...] = u
        a_ref[...] = (g * _sigmoid(g) * u).astype(BF16)

    tile = pl.BlockSpec((tm, bn), lambda t, i: (i, t))
    return pl.pallas_call(
        body, name=name, grid=(half, m // tm),
        in_specs=[pl.BlockSpec((tm, k), lambda t, i: (i, 0)),
                  _w_spec(w_in, (k, bn), lambda t, i: (t, 0, 0)),
                  _w_spec(w_in, (k, bn), lambda t, i: (half + t, 0, 0))],
        out_specs=[tile, tile, tile],
        out_shape=[jax.ShapeDtypeStruct((m, D_FF), F32)] * 2 + [jax.ShapeDtypeStruct((m, D_FF), BF16)],
        compiler_params=_cparams(("arbitrary", "arbitrary")),
    )(h, w_in.arr, w_in.arr)


def _ffn_out_bx_act(dy, w_out, g, u, name):
    m = dy.shape[0]
    tr = min(MM_ROWS, m)

    def body(dy_ref, w_ref, g_ref, u_ref, dgu_ref):
        da = lax.dot_general(dy_ref[...], w_ref[...], _NT, preferred_element_type=F32)
        gv = g_ref[...]
        sg = _sigmoid(gv)
        dgu_ref[0] = (da * u_ref[...] * (sg * (1.0 + gv * (1.0 - sg)))).astype(BF16)
        dgu_ref[1] = (da * (gv * sg)).astype(BF16)

    tile = pl.BlockSpec((tr, FFN_COLS), lambda i, c: (i, c))
    return pl.pallas_call(
        body, name=name, grid=(m // tr, D_FF // FFN_COLS),
        in_specs=[pl.BlockSpec((tr, D_MODEL), lambda i, c: (i, 0)),
                  _w_spec(w_out, (FFN_COLS, D_MODEL), lambda i, c: (0, c, 0)), tile, tile],
        out_specs=pl.BlockSpec((2, tr, FFN_COLS), lambda i, c: (0, i, c)),
        out_shape=jax.ShapeDtypeStruct((2, m, D_FF), BF16),
        compiler_params=_cparams(("arbitrary", "arbitrary")),
    )(dy, w_out.arr, g, u)


def _ffn_in_bwd(dgu, h, w_in, name):
    m, k = h.shape
    nb, bn = w_in.arr.shape[0], w_in.arr.shape[-1]
    per = dgu.shape[2] // bn
    tm = min(MM_ROWS, m)
    ck = next(c for c in (512, 256, 128) if k % c == 0)
    once = pl.Buffered(1)

    def body(dgu_ref, h_ref, w_ref, dh_ref, dw_ref):
        @pl.when(pl.program_id(0) == 0)
        def _():
            dh_ref[...] = jnp.zeros_like(dh_ref)

        def rows(i, carry):
            r = pl.ds(pl.multiple_of(i * tm, tm), tm)
            dh_ref[r, :] += lax.dot_general(dgu_ref[r, :], w_ref[...], _NT, preferred_element_type=F32)
            return carry
        lax.fori_loop(0, m // tm, rows, 0)

        def cols(i, carry):
            c = pl.ds(pl.multiple_of(i * ck, ck), ck)
            dw_ref[c, :] = lax.dot_general(h_ref[:, c], dgu_ref[...], (((0,), (0,)), ((), ())),
                                           preferred_element_type=F32)
            return carry
        lax.fori_loop(0, k // ck, cols, 0)

    return pl.pallas_call(
        body, name=name, grid=(nb,),
        in_specs=[pl.BlockSpec((None, m, bn), lambda n: (n // per, 0, n % per)),
                  pl.BlockSpec((m, k), lambda n: (0, 0), pipeline_mode=once),
                  _w_spec(w_in, (k, bn), lambda n: (n, 0, 0))],
        out_specs=[pl.BlockSpec((m, k), lambda n: (0, 0), pipeline_mode=once),
                   pl.BlockSpec((None, k, bn), lambda n: (n, 0, 0))],
        out_shape=[jax.ShapeDtypeStruct((m, k), F32), jax.ShapeDtypeStruct((nb, k, bn), F32)],
        compiler_params=_cparams(("arbitrary",)),
    )(dgu, h, w_in.arr)


def _loss_head(y, target, name):
    def body(y_ref, t_ref, dy_ref, loss_ref):
        e = y_ref[...] - t_ref[...]
        dy_ref[...] = e * (1.0 / D_MODEL)
        part = jnp.sum(jnp.mean(e * e, axis=-1, keepdims=True), axis=0, keepdims=True) * 0.5
        loss_ref[...] += jnp.broadcast_to(part, loss_ref.shape)
    return _row_call(name, body, [y, target], [], [(D_MODEL, F32)], [(128, F32)])


def _lane_scan(v, reverse):
    s = v.shape[1]
    lane = lax.broadcasted_iota(jnp.int32, v.shape, 1)
    d = 1
    while d < s:
        if reverse:
            v = v + jnp.where(lane < s - d, pltpu.roll(v, s - d, 1), 0.0)
        else:
            v = v + jnp.where(lane >= d, pltpu.roll(v, d, 1), 0.0)
        d *= 2
    return v


def _fox_gate(flt, b_f, name):
    def body(f_ref, b_ref, cum_ref):
        z = f_ref[...] + b_ref[...]
        cum_ref[...] = _lane_scan(-_softplus(-z), reverse=False)
    return pl.pallas_call(body, name=name, out_shape=jax.ShapeDtypeStruct(flt.shape, F32),
                          compiler_params=_cparams())(flt, b_f)


def _fox_gate_bwd(dcum_q, dcum_k, flt, b_f, name):
    def body(dq_ref, dk_ref, f_ref, b_ref, df_ref, db_ref):
        z = f_ref[...] + b_ref[...]
        df = _lane_scan(dq_ref[...] + dk_ref[...], reverse=True) * _sigmoid(-z)
        df_ref[...] = df
        db_ref[...] = jnp.sum(df, axis=1, keepdims=True)
    h = flt.shape[0]
    return pl.pallas_call(body, name=name,
                          out_shape=(jax.ShapeDtypeStruct(flt.shape, F32), jax.ShapeDtypeStruct((h, 1), F32)),
                          compiler_params=_cparams())(dcum_q, dcum_k, flt, b_f)


def _pick_head(block, h):
    lane = lax.broadcasted_iota(jnp.int32, block.shape, 1)
    return jnp.sum(jnp.where(lane == h, block, 0.0), axis=1, keepdims=True)


def _put_head(ref, col, h):
    @pl.when(h == 0)
    def _():
        ref[...] = jnp.zeros_like(ref)
    lane = lax.broadcasted_iota(jnp.int32, ref.shape, 1)
    ref[...] = jnp.where(lane == h, col, ref[...])


_NT = (((1,), (1,)), ((), ()))
_FOX_SCALE = FOX_HEAD_DIM ** -0.5


HEAD_PAIRS = FOX_HEADS // 2
PAIR_W = 2 * FOX_HEAD_DIM


def _low_half(shape):
    return lax.broadcasted_iota(jnp.int32, shape, 1) < FOX_HEAD_DIM


def _fox_attn_fwd(qkv, cum, cum_t, name):
    s = qkv.shape[0]
    t = min(ATT_TILE, s)
    wide = min(ATT_WIDE, s)

    def body(q_ref, k_ref, v_ref, cum_ref, cumt_ref, o_ref, ob_ref, lse_ref):
        i = pl.program_id(0)
        hp = pl.program_id(1)
        lo = _low_half((t, PAIR_W))
        qv = q_ref[...]
        zero = jnp.zeros_like(qv)
        q2 = (jnp.where(lo, qv, zero), jnp.where(lo, zero, qv))
        cum_v = cum_ref[...]
        cq2 = (_pick_head(cum_v, 2 * hp), _pick_head(cum_v, 2 * hp + 1))

        def step(j, carry, masked):
            ks = pl.ds(pl.multiple_of(j * wide, wide), wide)
            kj = k_ref[ks, :]
            vj = v_ref[ks, :]
            out = []
            for e in range(2):
                m, l, acc = carry[e]
                sc = lax.dot_general(q2[e], kj, _NT, preferred_element_type=F32) * _FOX_SCALE
                sc = sc + cq2[e] - cumt_ref[e:e + 1, ks]
                if masked:
                    q_pos = i * t + lax.broadcasted_iota(jnp.int32, (t, wide), 0)
                    k_pos = j * wide + lax.broadcasted_iota(jnp.int32, (t, wide), 1)
                    sc = jnp.where(k_pos <= q_pos, sc, -jnp.inf)
                m_new = jnp.maximum(m, jnp.max(sc, axis=1, keepdims=True))
                alpha = jnp.exp(m - m_new)
                p = jnp.exp(sc - m_new)
                l = alpha * l + jnp.sum(p, axis=1, keepdims=True)
                acc = alpha * acc + jnp.dot(p.astype(BF16), vj, preferred_element_type=F32)
                out.append((m_new, l, acc))
            return tuple(out)

        one = (jnp.full((t, 1), -jnp.inf, F32), jnp.zeros((t, 1), F32), jnp.zeros((t, PAIR_W), F32))
        whole = (i * t) // wide
        carry = lax.fori_loop(0, whole, lambda j, c: step(j, c, False), (one, one))
        (m0, l0, a0), (m1, l1, a1) = step(whole, carry, True)
        o = jnp.where(lo, a0 / l0, a1 / l1)
        o_ref[...] = o
        ob_ref[...] = o.astype(BF16)
        _put_head(lse_ref, m0 + jnp.log(l0), 2 * hp)
        _put_head(lse_ref, m1 + jnp.log(l1), 2 * hp + 1)

    nat_tile = pl.BlockSpec((t, FOX_HEADS), lambda i, hp: (i, 0))
    out_tile = pl.BlockSpec((t, PAIR_W), lambda i, hp: (i, hp))
    return pl.pallas_call(
        body, name=name, grid=(s // t, HEAD_PAIRS),
        in_specs=[pl.BlockSpec((t, PAIR_W), lambda i, hp: (i, hp)),
                  pl.BlockSpec((s, PAIR_W), lambda i, hp: (0, HEAD_PAIRS + hp)),
                  pl.BlockSpec((s, PAIR_W), lambda i, hp: (0, 2 * HEAD_PAIRS + hp)),
                  nat_tile, pl.BlockSpec((None, 2, s), lambda i, hp: (hp, 0, 0))],
        out_specs=[out_tile, out_tile, nat_tile],
        out_shape=[jax.ShapeDtypeStruct((s, D_MODEL), F32), jax.ShapeDtypeStruct((s, D_MODEL), BF16),
                   jax.ShapeDtypeStruct((s, FOX_HEADS), F32)],
        compiler_params=_cparams(("arbitrary", "arbitrary")),
    )(qkv, qkv, qkv, cum, cum_t)


def _fox_delta(do, o, expand, name):
    def body(do_ref, o_ref, e_ref, d_ref):
        prod = do_ref[...] * o_ref[...]
        hi = prod.astype(BF16)
        lo = (prod - hi.astype(F32)).astype(BF16)
        tot = (jnp.dot(hi, e_ref[...], preferred_element_type=F32)
               + jnp.dot(lo, e_ref[...], preferred_element_type=F32))
        d_ref[...] = tot[:, :FOX_HEADS]
    return _row_call(name, body, [do, o], [expand], [(FOX_HEADS, F32)], [])[0]


def _fox_attn_bwd(qkv, do, cum, cum_t, lse_t, delta_t, name):
    s = qkv.shape[0]
    t = min(ATT_TILE, s)
    wide = min(ATT_WIDE, s)
    nq = s // t
    tn_dims = (((0,), (0,)), ((), ()))

    def body(q_ref, k_ref, v_ref, do_ref, cum_ref, cumt_ref, lset_ref, deltat_ref,
             dq_ref, dk_ref, dv_ref, dck_ref, dcq_ref):
        hp = pl.program_id(0)
        j = pl.program_id(1)

        @pl.when(j == 0)
        def _():
            dq_ref[...] = jnp.zeros_like(dq_ref)
            dcq_ref[...] = jnp.zeros_like(dcq_ref)
        dk_ref[...] = jnp.zeros_like(dk_ref)
        dv_ref[...] = jnp.zeros_like(dv_ref)

        lo = _low_half((t, PAIR_W))
        lane = lax.broadcasted_iota(jnp.int32, (t, PAIR_W), 1)
        kv = k_ref[...]
        vv = v_ref[...]
        zero = jnp.zeros_like(kv)
        k2 = (jnp.where(lo, kv, zero), jnp.where(lo, zero, kv))
        v2 = (jnp.where(lo, vv, zero), jnp.where(lo, zero, vv))
        cum_v = cum_ref[...]
        ck2 = (_pick_head(cum_v, 2 * hp), _pick_head(cum_v, 2 * hp + 1))

        def step(i, dck, masked):
            qs = pl.ds(pl.multiple_of(i * wide, wide), wide)
            qi = q_ref[qs, :]
            do_i = do_ref[qs, :].astype(BF16)
            dv_p, dk_p, dq_p = [], [], []
            for e in range(2):
                st = lax.dot_general(k2[e], qi, _NT, preferred_element_type=F32) * _FOX_SCALE
                st = st + cumt_ref[e:e + 1, qs] - ck2[e]
                if masked:
                    k_pos = j * t + lax.broadcasted_iota(jnp.int32, (t, wide), 0)
                    q_pos = i * wide + lax.broadcasted_iota(jnp.int32, (t, wide), 1)
                    st = jnp.where(k_pos <= q_pos, st, -jnp.inf)
                pt = jnp.exp(st - lset_ref[e:e + 1, qs])
                dv_p.append(jnp.dot(pt.astype(BF16), do_i, preferred_element_type=F32))
                dpt = lax.dot_general(v2[e], do_i, _NT, preferred_element_type=F32)
                dst = pt * (dpt - deltat_ref[e:e + 1, qs])
                dsb = dst.astype(BF16)
                dk_p.append(jnp.dot(dsb, qi, preferred_element_type=F32))
                dq_p.append(lax.dot_general(dsb, kv, tn_dims, preferred_element_type=F32))
                dck = dck - jnp.where(lane == e, jnp.sum(dst, axis=1, keepdims=True), 0.0)
                dcq_ref[e:e + 1, qs] += jnp.sum(dst, axis=0, keepdims=True)
            dv_ref[...] += jnp.where(lo, dv_p[0], dv_p[1])
            dk_ref[...] += jnp.where(lo, dk_p[0], dk_p[1])
            dq_ref[qs, :] += jnp.where(_low_half((wide, PAIR_W)), dq_p[0], dq_p[1]) * _FOX_SCALE
            return dck

        first = (j * t) // wide
        dck = step(first, jnp.zeros((t, PAIR_W), F32), True)
        dck = lax.fori_loop(first + 1, s // wide, lambda i, c: step(i, c, False), dck)
        dk_ref[...] = dk_ref[...] * _FOX_SCALE
        dck_ref[...] = dck

    pair_full = lambda part: pl.BlockSpec((s, PAIR_W), lambda hp, j: (0, part * HEAD_PAIRS + hp))
    pair_tile = lambda part: pl.BlockSpec((t, PAIR_W), lambda hp, j: (j, part * HEAD_PAIRS + hp))
    rows = pl.BlockSpec((None, 2, s), lambda hp, j: (hp, 0, 0))
    return pl.pallas_call(
        body, name=name, grid=(HEAD_PAIRS, nq),
        in_specs=[pair_full(0), pair_tile(1), pair_tile(2), pair_full(0),
                  pl.BlockSpec((t, FOX_HEADS), lambda hp, j: (j, 0)), rows, rows, rows],
        out_specs=[pair_full(0), pair_tile(0), pair_tile(0),
                   pl.BlockSpec((None, t, PAIR_W), lambda hp, j: (hp, j, 0)), rows],
        out_shape=[jax.ShapeDtypeStruct((s, D_MODEL), F32)] * 3
        + [jax.ShapeDtypeStruct((HEAD_PAIRS, s, PAIR_W), F32), jax.ShapeDtypeStruct((HEAD_PAIRS, 2, s), F32)],
        compiler_params=_cparams(("arbitrary", "arbitrary")),
    )(qkv, qkv, qkv, do, cum, cum_t, lse_t, delta_t)


def _shift_down(v, d):
    row = lax.broadcasted_iota(jnp.int32, v.shape, 0)
    return jnp.where(row >= d, pltpu.roll(v, d, 0), 0.0)


def _shift_up(v, d):
    s = v.shape[0]
    row = lax.broadcasted_iota(jnp.int32, v.shape, 0)
    return jnp.where(row < s - d, pltpu.roll(v, s - d, 0), 0.0)


def _conv_taps(v, cw_ref, width):
    out = cw_ref[width - 1:width, :] * v
    for k in range(width - 1):
        out = out + cw_ref[k:k + 1, :] * _shift_down(v, width - 1 - k)
    return out


def _conv_taps_bwd(dout, v, cw_ref, dcw_ref, width):
    dv = cw_ref[width - 1:width, :] * dout
    dcw_ref[width - 1:width, :] = _rows_sum(dout * v)
    for k in range(width - 1):
        d = width - 1 - k
        dv = dv + cw_ref[k:k + 1, :] * _shift_up(dout, d)
        dcw_ref[k:k + 1, :] = _rows_sum(dout * _shift_down(v, d))
    return dv


def _col_spec(s, tc, part=0):
    off = part * (D_MODEL // tc)
    return pl.BlockSpec((s, tc), lambda c: (0, c + off))


def _small_spec(rows, tc):
    return pl.BlockSpec((rows, tc), lambda c: (0, c))


def _col_call(name, body, in_arrays, in_specs, out_rows, s, tc):
    return pl.pallas_call(
        body, name=name, grid=(D_MODEL // tc,), in_specs=in_specs,
        out_specs=[pl.BlockSpec((r, tc), lambda c: (0, c)) for r, _ in out_rows],
        out_shape=[jax.ShapeDtypeStruct((r, D_MODEL), dt) for r, dt in out_rows],
        compiler_params=_cparams(("arbitrary",)),
    )(*in_arrays)


def _sconv_fwd(proj, conv_w, name):
    s = proj.shape[0]
    tc = COL_TILE

    def body(b_ref, c_ref, x_ref, cw_ref, y_ref):
        y_ref[...] = (b_ref[...] * _conv_taps(c_ref[...] * x_ref[...], cw_ref, 3)).astype(BF16)

    return _col_call(name, body, [proj, proj, proj, conv_w],
                     [_col_spec(s, tc, 0), _col_spec(s, tc, 1), _col_spec(s, tc, 2), _small_spec(3, tc)],
                     [(s, BF16)], s, tc)[0]


def _sconv_bwd(dy, proj, conv_w, name):
    s = proj.shape[0]
    tc = COL_TILE

    def body(dy_ref, b_ref, c_ref, x_ref, cw_ref, db_ref, dc_ref, dx_ref, dcw_ref):
        w = c_ref[...] * x_ref[...]
        dy_v = dy_ref[...]
        db_ref[...] = (dy_v * _conv_taps(w, cw_ref, 3)).astype(BF16)
        dw = _conv_taps_bwd(dy_v * b_ref[...], w, cw_ref, dcw_ref, 3)
        dc_ref[...] = (dw * x_ref[...]).astype(BF16)
        dx_ref[...] = (dw * c_ref[...]).astype(BF16)

    return _col_call(name, body, [dy, proj, proj, proj, conv_w],
                     [_col_spec(s, tc), _col_spec(s, tc, 0), _col_spec(s, tc, 1), _col_spec(s, tc, 2),
                      _small_spec(3, tc)],
                     [(s, BF16), (s, BF16), (s, BF16), (3, F32)], s, tc)


def _lru_conv(proj, conv_w, conv_b, name):
    s = proj.shape[0]
    tc = COL_TILE

    def body(x_ref, cw_ref, cb_ref, xb_ref, xbb_ref):
        xb = _conv_taps(x_ref[...], cw_ref, 4) + cb_ref[...]
        xb_ref[...] = xb
        xbb_ref[...] = xb.astype(BF16)

    return _col_call(name, body, [proj, conv_w, conv_b],
                     [_col_spec(s, tc, 1), _small_spec(4, tc), _small_spec(1, tc)],
                     [(s, F32), (s, BF16)], s, tc)


def _lru_conv_bwd(dxb1, dxb2, proj, conv_w, name):
    s = proj.shape[0]
    tc = COL_TILE

    def body(d1_ref, d2_ref, x_ref, cw_ref, dx_ref, dcw_ref, dcb_ref):
        dxb = d1_ref[...] + d2_ref[...]
        dcb_ref[...] = _rows_sum(dxb)
        dx_ref[...] = _conv_taps_bwd(dxb, x_ref[...], cw_ref, dcw_ref, 4).astype(BF16)

    return _col_call(name, body, [dxb1, dxb2, proj, conv_w],
                     [_col_spec(s, tc), _col_spec(s, tc), _col_spec(s, tc, 1), _small_spec(4, tc)],
                     [(s, BF16), (4, F32), (1, F32)], s, tc)


_GELU_C = math.sqrt(2.0 / math.pi)


def _gelu_parts(g):
    inner = _GELU_C * (g + 0.044715 * g * g * g)
    th = jnp.tanh(inner)
    val = 0.5 * g * (1.0 + th)
    der = 0.5 * (1.0 + th) + 0.5 * g * (1.0 - th * th) * (_GELU_C * (1.0 + 3.0 * 0.044715 * g * g))
    return val, der


def _lru_gates(pa_ref, px_ref, ba_ref, bx_ref, lam_ref):
    r = _sigmoid(pa_ref[...] + ba_ref[...])
    ig = _sigmoid(px_ref[...] + bx_ref[...])
    sp = _softplus(-lam_ref[...])
    log_a = (-LRU_C) * r * sp
    a = jnp.exp(log_a)
    z = 2.0 * log_a
    one_m_a2 = jnp.where(z > -1e-3, -(z * (1.0 + z * (0.5 + z * (1.0 / 6.0)))), 1.0 - jnp.exp(z))
    return r, ig, sp, a, jnp.sqrt(one_m_a2)


def _lru_scan(pre, xb, proj, b_a, b_x, lam, name):
    s = xb.shape[0]
    tc = COL_TILE

    def body(pa_ref, px_ref, xb_ref, g_ref, ba_ref, bx_ref, lam_ref, y_ref, hs_ref):
        _, ig, _, a, mult = _lru_gates(pa_ref, px_ref, ba_ref, bx_ref, lam_ref)
        b = mult * (ig * xb_ref[...])
        d = 1
        while d < s:
            row = lax.broadcasted_iota(jnp.int32, a.shape, 0)
            keep = row >= d
            b = b + a * jnp.where(keep, pltpu.roll(b, d, 0), 0.0)
            a = a * jnp.where(keep, pltpu.roll(a, d, 0), 1.0)
            d *= 2
        hs_ref[...] = b
        y_ref[...] = (b * _gelu_parts(g_ref[...])[0]).astype(BF16)

    return _col_call(name, body, [pre, pre, xb, proj, b_a, b_x, lam],
                     [_col_spec(s, tc, 0), _col_spec(s, tc, 1), _col_spec(s, tc), _col_spec(s, tc, 0),
                      _small_spec(1, tc), _small_spec(1, tc), _small_spec(1, tc)],
                     [(s, BF16), (s, F32)], s, tc)


def _lru_scan_bwd(dy, pre, xb, proj, hs, b_a, b_x, lam, name):
    s = xb.shape[0]
    tc = COL_TILE

    def body(dy_ref, pa_ref, px_ref, xb_ref, g_ref, hs_ref, ba_ref, bx_ref, lam_ref,
             dg_ref, dpa_ref, dpx_ref, dxb_ref, dba_ref, dbx_ref, dlam_ref):
        r, ig, sp, a, mult = _lru_gates(pa_ref, px_ref, ba_ref, bx_ref, lam_ref)
        gl, gl_der = _gelu_parts(g_ref[...])
        dy_v = dy_ref[...]
        hs_v = hs_ref[...]
        dg_ref[...] = (dy_v * hs_v * gl_der).astype(BF16)
        lam_t = dy_v * gl
        coef = _shift_up(a, 1)
        d = 1
        while d < s:
            row = lax.broadcasted_iota(jnp.int32, coef.shape, 0)
            keep = row < s - d
            lam_t = lam_t + coef * jnp.where(keep, pltpu.roll(lam_t, s - d, 0), 0.0)
            coef = coef * jnp.where(keep, pltpu.roll(coef, s - d, 0), 1.0)
            d *= 2
        xb_v = xb_ref[...]
        da = lam_t * _shift_down(hs_v, 1)
        dmult = lam_t * (ig * xb_v)
        dig = lam_t * mult * xb_v
        dxb_ref[...] = lam_t * mult * ig
        dlog_a = da * a - dmult * (a * a) / mult
        dr = dlog_a * ((-LRU_C) * sp)
        dsp = _rows_sum(dlog_a * ((-LRU_C) * r))
        dlam_ref[...] = -dsp * _sigmoid(-lam_ref[...])
        dpa = dr * r * (1.0 - r)
        dpx = dig * ig * (1.0 - ig)
        dba_ref[...] = _rows_sum(dpa)
        dbx_ref[...] = _rows_sum(dpx)
        dpa_ref[...] = dpa.astype(BF16)
        dpx_ref[...] = dpx.astype(BF16)

    return _col_call(name, body, [dy, pre, pre, xb, proj, hs, b_a, b_x, lam],
                     [_col_spec(s, tc), _col_spec(s, tc, 0), _col_spec(s, tc, 1), _col_spec(s, tc),
                      _col_spec(s, tc, 0), _col_spec(s, tc),
                      _small_spec(1, tc), _small_spec(1, tc), _small_spec(1, tc)],
                     [(s, BF16), (s, BF16), (s, BF16), (s, F32), (1, F32), (1, F32), (1, F32)], s, tc)


def _ffn_fwd(x, w_in, w_out, g_pre, g_post, shift, scale, gate, tag, after=None):
    h = _pre_norm(x, g_pre, scale, shift, tag + "_pre", after=after)
    g, u, a = _ffn_in_act(h, w_in, tag + "_in")
    y = _mm_nn(a, w_out, tag + "_out", tn=512)
    xo = _post_norm(x, y, g_post, gate, 0.5, tag + "_post")
    return xo, (x, h, g, u, a, y)


def _ffn_bwd(dxo, saved, w_in, w_out, g_pre, g_post, scale, gate, tag, after=None):
    x, h, g, u, a, y = saved
    dy, dgate, dg_post = _post_norm_bwd(dxo, y, g_post, gate, 0.5, tag + "_post_b", after=after)
    dw_out = _mm_tn(a, dy, tag + "_out_bw", tn=512)
    dgu = _ffn_out_bx_act(dy, w_out, g, u, tag + "_out_bx")
    dh, dw_in = _ffn_in_bwd(dgu, h, w_in, tag + "_in_b")
    dx, dshift, dscale, dg_pre = _pre_norm_bwd(dxo, dh, x, g_pre, scale, tag + "_pre_b")
    return dx, dw_in, dw_out, (dshift, dscale, dgate), dg_pre, dg_post


def _pair_rows(v):
    return v.T.reshape(HEAD_PAIRS, 2, v.shape[0])


def _fox_fwd(h, p, tag):
    s = h.shape[0]
    proj = _mm_nn(h, p["w_in"], tag + "_in", tn=640)
    qkv = proj[:, :3 * D_MODEL].astype(BF16)
    flt = proj[:, 3 * D_MODEL:3 * D_MODEL + FOX_HEADS].T
    cum_t = _fox_gate(flt, p["b_f"], tag + "_gate")
    cum = cum_t.T
    cum_t2 = cum_t.reshape(HEAD_PAIRS, 2, s)
    o, ob, lse = _fox_attn_fwd(qkv, cum, cum_t2, tag + "_attn")
    y = _mm_nn(ob, p["w_out"], tag + "_out")
    return y, (qkv, flt, cum, cum_t2, o, ob, lse)


def _fox_bwd(dy, h, saved, p, tag):
    qkv, flt, cum, cum_t2, o, ob, lse = saved
    s = h.shape[0]
    do = _mm_nt(dy, p["w_out"], tag + "_out_bx")
    dw_out = _mm_tn(ob, dy, tag + "_out_bw")
    expand = jnp.pad(jnp.repeat(jnp.eye(FOX_HEADS, dtype=BF16), FOX_HEAD_DIM, axis=0),
                     ((0, 0), (0, PAIR_W - FOX_HEADS)))
    delta = _fox_delta(do, o, expand, tag + "_attn_delta")
    dq, dk, dv, dck, dcq = _fox_attn_bwd(qkv, do, cum, cum_t2, _pair_rows(lse), _pair_rows(delta), tag + "_attn_b")
    dcum_k = dck[:, :, :2].transpose(0, 2, 1).reshape(FOX_HEADS, s)
    dflt, db_f = _fox_gate_bwd(dcq.reshape(FOX_HEADS, s), dcum_k, flt, p["b_f"], tag + "_gate_b")
    dproj = jnp.concatenate(
        [dq, dk, dv, dflt.T, jnp.zeros((s, FOX_PAD - 3 * D_MODEL - FOX_HEADS), F32)], axis=1).astype(BF16)
    dh = _mm_nt(dproj, p["w_in"], tag + "_in_bx", tn=640)
    dw_in = _mm_tn(h, dproj, tag + "_in_bw", tn=640)
    return dh, {"w_in": dw_in, "w_out": dw_out, "b_f": db_f}


def _sconv_mix_fwd(h, p, tag):
    proj = _mm_nn(h, p["w_in"], tag + "_in")
    yb = _sconv_fwd(proj, p["conv_w"], tag + "_conv")
    y = _mm_nn(yb, p["w_out"], tag + "_out")
    return y, (proj, yb)


def _sconv_mix_bwd(dy, h, saved, p, tag):
    proj, yb = saved
    dyb = _mm_nt(dy, p["w_out"], tag + "_out_bx")
    dw_out = _mm_tn(yb, dy, tag + "_out_bw")
    db, dc, dxv, dcw = _sconv_bwd(dyb, proj, p["conv_w"], tag + "_conv_b")
    dproj = jnp.concatenate([db, dc, dxv], axis=1)
    dh = _mm_nt(dproj, p["w_in"], tag + "_in_bx")
    dw_in = _mm_tn(h, dproj, tag + "_in_bw", tn=p["w_in"].arr.shape[-1], blocked_out=True)
    return dh, {"w_in": dw_in, "w_out": dw_out, "conv_w": dcw}


def _lru_mix_fwd(h, p, tag):
    proj = _mm_nn(h, p["w_in"], tag + "_in")
    xb, xbb = _lru_conv(proj, p["conv_w"], p["conv_b"], tag + "_conv")
    pre = _mm_nn(xbb, p["w_ax"], tag + "_gates", tn=D_MODEL)
    yb, hs = _lru_scan(pre, xb, proj, p["b_a"], p["b_x"], p["lam"], tag + "_scan")
    y = _mm_nn(yb, p["w_out"], tag + "_out")
    return y, (proj, xb, xbb, pre, yb, hs)


def _diag_blocks(m):
    return jnp.stack([m[LRU_BLOCK_DIM * n:LRU_BLOCK_DIM * (n + 1), LRU_BLOCK_DIM * n:LRU_BLOCK_DIM * (n + 1)]
                      for n in range(LRU_BLOCKS)])


def _lru_mix_bwd(dy, h, saved, p, tag):
    proj, xb, xbb, pre, yb, hs = saved
    dyb = _mm_nt(dy, p["w_out"], tag + "_out_bx")
    dw_out = _mm_tn(yb, dy, tag + "_out_bw")
    dg, dpa, dpx, dxb1, dba, dbx, dlam = _lru_scan_bwd(dyb, pre, xb, proj, hs, p["b_a"], p["b_x"], p["lam"],
                                                       tag + "_scan_b")
    dpre = jnp.concatenate([dpa, dpx], axis=1)
    dxb2 = _mm_nt(dpre, p["w_ax"], tag + "_gates_bx", tn=D_MODEL)
    dw_ax = _mm_tn(xbb, dpre, tag + "_gates_bw", tn=D_MODEL)
    dx0, dcw, dcb = _lru_conv_bwd(dxb1, dxb2, proj, p["conv_w"], tag + "_conv_b")
    dproj = jnp.concatenate([dg, dx0], axis=1)
    dh = _mm_nt(dproj, p["w_in"], tag + "_in_bx")
    dw_in = _mm_tn(h, dproj, tag + "_in_bw", tn=p["w_in"].arr.shape[-1], blocked_out=True)
    grads = {"w_in": dw_in, "w_out": dw_out, "conv_w": dcw, "conv_b": dcb,
             "w_a": _diag_blocks(dw_ax[:, :D_MODEL]), "w_x": _diag_blocks(dw_ax[:, D_MODEL:]),
             "b_a": dba, "b_x": dbx, "lam": dlam}
    return dh, grads


_MIXERS = ((_fox_fwd, _fox_bwd), (_sconv_mix_fwd, _sconv_mix_bwd), (_lru_mix_fwd, _lru_mix_bwd))


def _local_step(x, target, mod, layer_params, on_grads=None, on_mid=None, first_after=None):
    layers = []
    tape = []
    for i in range(DEPTH):
        lp = dict(layer_params(i, 0, x))
        layers.append(lp)
        row = lambda v: v[None, :]
        m = lambda sub, what: mod[i, sub, what][None, :]
        x, sv0 = _ffn_fwd(x, lp["ffn_in"][0], lp["ffn_out"][0], row(lp["norm_pre"][0]), row(lp["norm_post"][0]),
                          m(0, 0), m(0, 1), m(0, 2), f"l{i}_ffn0", after=first_after if i == 0 else None)
        lp.update(layer_params(i, 1, x))
        h = _pre_norm(x, row(lp["norm_pre"][1]), m(1, 1), m(1, 0), f"l{i}_mix_pre")
        y, svm = _MIXERS[i % 3][0](h, lp["mixer"], f"l{i}_mix")
        x1 = _post_norm(x, y, row(lp["norm_post"][1]), m(1, 2), 1.0, f"l{i}_mix_post")
        second = layer_params(i, 2, x1)
        lp["ffn_in"] = lp["ffn_in"] + second["ffn_in"]
        lp["ffn_out"] = lp["ffn_out"] + second["ffn_out"]
        x2, sv2 = _ffn_fwd(x1, lp["ffn_in"][1], lp["ffn_out"][1], row(lp["norm_pre"][2]), row(lp["norm_post"][2]),
                           m(2, 0), m(2, 1), m(2, 2), f"l{i}_ffn1")
        tape.append((sv0, (x, h, y, svm), sv2))
        x = x2
    dx, loss_row = _loss_head(x, target, "loss_head")

    layer_grads = [None] * DEPTH
    dmod = [None] * DEPTH
    after = None
    for i in reversed(range(DEPTH)):
        lp = layers[i]
        row = lambda v: v[None, :]
        m = lambda sub, what: mod[i, sub, what][None, :]
        sv0, (xm, h, y, svm), sv2 = tape[i]
        dx, dw_in1, dw_out1, dm2, dgp2, dgq2 = _ffn_bwd(dx, sv2, lp["ffn_in"][1], lp["ffn_out"][1],
                                                        row(lp["norm_pre"][2]), row(lp["norm_post"][2]),
                                                        m(2, 1), m(2, 2), f"l{i}_ffn1", after=after)
        after = on_mid(i, dx) if on_mid is not None else None
        dy, dgate1, dgq1 = _post_norm_bwd(dx, y, row(lp["norm_post"][1]), m(1, 2), 1.0, f"l{i}_mix_post_b", after=after)
        dh, mg = _MIXERS[i % 3][1](dy, h, svm, lp["mixer"], f"l{i}_mix")
        dx, dshift1, dscale1, dgp1 = _pre_norm_bwd(dx, dh, xm, row(lp["norm_pre"][1]), m(1, 1), f"l{i}_mix_pre_b")
        dx, dw_in0, dw_out0, dm0, dgp0, dgq0 = _ffn_bwd(dx, sv0, lp["ffn_in"][0], lp["ffn_out"][0],
                                                        row(lp["norm_pre"][0]), row(lp["norm_post"][0]),
                                                        m(0, 1), m(0, 2), f"l{i}_ffn0")
        dmod[i] = jnp.concatenate([*dm0, dshift1, dscale1, dgate1, *dm2], axis=0).reshape(N_SUB, 3, D_MODEL)
        layer_grads[i] = {"ffn_in": (dw_in0, dw_in1), "ffn_out": (dw_out0, dw_out1),
                          "norm_pre": jnp.concatenate([dgp0, dgp1, dgp2], axis=0),
                          "norm_post": jnp.concatenate([dgq0, dgq1, dgq2], axis=0), "mixer": mg}
        if on_grads is not None:
            after = on_grads(i, layer_grads[i], dx)
    return loss_row, dx, jnp.stack(dmod), layer_grads


COND_ROWS = 16
COND_PAD = 128


def _cond_fwd(c_pad, w_cond, b_shard, name):
    nl, d, n = w_cond.shape
    tn = 768

    def body(c_ref, w_ref, b_ref, o_ref):
        cv = c_ref[...]
        act = (cv * _sigmoid(cv)).astype(BF16)
        o_ref[...] = jnp.dot(act, w_ref[...].astype(BF16), preferred_element_type=F32) + b_ref[...]

    return pl.pallas_call(
        body, name=name, grid=(nl, n // tn),
        in_specs=[pl.BlockSpec((COND_ROWS, d), lambda i, j: (0, 0)),
                  pl.BlockSpec((None, d, tn), lambda i, j: (i, 0, j)),
                  pl.BlockSpec((None, 1, tn), lambda i, j: (i, 0, j))],
        out_specs=pl.BlockSpec((None, COND_ROWS, tn), lambda i, j: (i, 0, j)),
        out_shape=jax.ShapeDtypeStruct((nl, COND_ROWS, n), F32),
        compiler_params=_cparams(("arbitrary", "arbitrary")),
    )(c_pad, w_cond, b_shard)


def _adam_math(w, g, m, v):
    nm = ADAM_B1 * m + (1.0 - ADAM_B1) * g
    nv = ADAM_B2 * v + (1.0 - ADAM_B2) * (g * g)
    m_hat = nm / (1.0 - ADAM_B1 ** ADAM_STEP)
    v_hat = nv / (1.0 - ADAM_B2 ** ADAM_STEP)
    delta = (-ADAM_LR) * (m_hat / (jnp.sqrt(v_hat) + ADAM_EPS) + ADAM_WD * w)
    return delta, nm, nv


def _cond_bwd_adamw(c_t, dmod_s, w, m, v, name):
    nl, d, n = w.shape
    tn = 384
    blk = pl.BlockSpec((None, d, tn), lambda i, j: (i, 0, j))

    def body(c_ref, dm_ref, w_ref, m_ref, v_ref, g_ref, d_ref, nm_ref, nv_ref):
        cv = c_ref[...]
        g = jnp.dot((cv * _sigmoid(cv)).astype(BF16), dm_ref[...], preferred_element_type=F32)
        g_ref[...] = g
        d_ref[...], nm_ref[...], nv_ref[...] = _adam_math(w_ref[...], g, m_ref[...], v_ref[...])

    return pl.pallas_call(
        body, name=name, grid=(nl, n // tn),
        in_specs=[pl.BlockSpec((d, COND_PAD), lambda i, j: (0, 0)),
                  pl.BlockSpec((None, COND_PAD, tn), lambda i, j: (i, 0, j)), blk, blk, blk],
        out_specs=[blk] * 4, out_shape=[jax.ShapeDtypeStruct(w.shape, F32)] * 4,
        compiler_params=_cparams(("arbitrary", "arbitrary")),
    )(c_t, dmod_s, w, m, v)


def _adamw(w, g, m, v, name):
    rows, cols = w.shape
    tr = next(t for t in (256, 176, 128, 64, 32, 16, 8) if rows % t == 0)
    blk = pl.BlockSpec((tr, cols), lambda i: (i, 0))

    def body(w_ref, g_ref, m_ref, v_ref, d_ref, nm_ref, nv_ref):
        d_ref[...], nm_ref[...], nv_ref[...] = _adam_math(w_ref[...], g_ref[...], m_ref[...], v_ref[...])

    return pl.pallas_call(
        body, name=name, grid=(rows // tr,), in_specs=[blk] * 4, out_specs=[blk] * 3,
        out_shape=[jax.ShapeDtypeStruct(w.shape, F32)] * 3, compiler_params=_cparams(("arbitrary",)),
    )(w, g, m, v)


_MESH = pl.DeviceIdType.MESH
_ANY = pl.BlockSpec(memory_space=pl.ANY)


def _place():
    return lax.axis_index("x"), lax.axis_index("y"), lax.axis_index("c")


def _other_chips(x, y):
    return [(1 - x, y), (x, 1 - y), (1 - x, 1 - y)]


def _allgather8(block, name):
    m_per, n = block.shape

    def body(x_ref, out_ref, send_sems, recv_sems, local_sem):
        x, y, c = _place()
        me, sibling = (x, y, c), (x, y, 1 - c)
        chips = _other_chips(x, y)

        def rows(px, py, pc):
            return out_ref.at[pl.ds((4 * px + 2 * py + pc) * m_per, m_per), :]

        def copy(k, blk, to, src=None):
            return pltpu.make_async_remote_copy(
                src_ref=rows(*blk) if src is None else src, dst_ref=rows(*blk),
                send_sem=send_sems.at[k], recv_sem=recv_sems.at[k], device_id=to, device_id_type=_MESH)

        mine = pltpu.make_async_copy(x_ref, rows(*me), local_sem)
        mine.start()
        first = [copy(0, me, sibling, src=x_ref)]
        first += [copy(1 + j, me, (*chip, c), src=x_ref) for j, chip in enumerate(chips)]
        for cp in first:
            cp.start()
        passed = [copy(4 + j, (*chip, c), sibling) for j, chip in enumerate(chips)]
        for j, chip in enumerate(chips):
            copy(1 + j, (*chip, c), me).wait_recv()
            passed[j].start()
        copy(0, sibling, me).wait_recv()
        for j, chip in enumerate(chips):
            copy(4 + j, (*chip, 1 - c), me).wait_recv()
        for cp in first + passed:
            cp.wait_send()
        mine.wait()

    return pl.pallas_call(
        body, name=name, out_shape=jax.ShapeDtypeStruct((N_DEV * m_per, n), block.dtype),
        in_specs=[pl.BlockSpec(memory_space=pltpu.VMEM)], out_specs=pl.BlockSpec(memory_space=pltpu.VMEM),
        scratch_shapes=[pltpu.SemaphoreType.DMA((7,)), pltpu.SemaphoreType.DMA((7,)), pltpu.SemaphoreType.DMA],
        compiler_params=_cparams(),
    )(block)


def _split_axis(shape):
    return next(a for a, n in enumerate(shape) if n > 1)


_HBM = pl.BlockSpec(memory_space=pltpu.HBM)
_SEM = pl.BlockSpec(memory_space=pltpu.SEMAPHORE)
_SPLIT_COPY = pltpu.CompilerParams(has_side_effects=pltpu.SideEffectType.DATAFLOW_SIDE_EFFECTING)
_TOKEN = jax.ShapeDtypeStruct((8, 128), F32)


def _in_hbm(arrays):
    return [pltpu.with_memory_space_constraint(a, pltpu.HBM) for a in arrays]


class _Gathered(NamedTuple):
    shard_shape: tuple
    chip_axis: int

    @property
    def shape(self):
        return self.shard_shape[:self.chip_axis] + (N_CHIPS,) + self.shard_shape[self.chip_axis:]

    def half(self, ref, chip, pc):
        cut = _split_axis(self.shard_shape)
        n = self.shard_shape[cut] // 2
        idx = [slice(None)] * len(self.shard_shape)
        idx[cut] = pl.ds(pc * n, n)
        idx.insert(self.chip_axis, chip)
        return ref.at[tuple(idx)]


def _own_block_placed(shard, layout, chip):
    return lax.dynamic_update_slice_in_dim(lax.empty(layout.shape, shard.dtype),
                                           jnp.expand_dims(shard, layout.chip_axis), chip, axis=layout.chip_axis)


def _gather_copies(lands, layouts, send_sems, recv_sems):
    x, y, c = _place()
    out = []
    for t, (land, lay) in enumerate(zip(lands, layouts)):
        for j, (px, py) in enumerate(_other_chips(x, y)):
            def copy(chip, t=t, j=j, px=px, py=py, land=land, lay=lay):
                return pltpu.make_async_remote_copy(
                    src_ref=lay.half(land, chip, c), dst_ref=lay.half(land, chip, c),
                    send_sem=send_sems.at[3 * t + j], recv_sem=recv_sems.at[3 * t + j],
                    device_id=(px, py, c), device_id_type=_MESH)
            out.append((copy(2 * x + y), copy(2 * px + py)))
    return out


def _gather_start(lands, layouts, after, name):
    nt = len(lands)
    order = [] if after is None else [after]

    def body(*refs):
        land_refs = refs[:nt]
        send_sems, recv_sems = refs[nt + len(order):nt + len(order) + 2]
        token = refs[-1]
        for send, _ in _gather_copies(land_refs, layouts, send_sems, recv_sems):
            send.start()
        token[...] = jnp.zeros_like(token)

    out = pl.pallas_call(
        body, name=name,
        out_shape=(pltpu.SemaphoreType.DMA((3 * nt,)), pltpu.SemaphoreType.DMA((3 * nt,)),
                   *[pltpu.HBM(a.shape, a.dtype) for a in lands], _TOKEN),
        in_specs=[_HBM] * nt + [_ANY] * len(order),
        out_specs=(_SEM, _SEM, *[_HBM] * nt, pl.BlockSpec(memory_space=pltpu.VMEM)),
        input_output_aliases={t: 2 + t for t in range(nt)}, compiler_params=_SPLIT_COPY,
    )(*_in_hbm(lands), *order)
    return out[0], out[1], list(out[2:2 + nt]), out[-1]


def _gather_wait(send_sems, recv_sems, lands, layouts, after, name):
    nt = len(lands)

    def body(*refs):
        land_refs = refs[:nt]
        sems = refs[nt:nt + 2]
        for send, arrival in _gather_copies(land_refs, layouts, *sems):
            send.wait_send()
            arrival.wait_recv()

    return list(pl.pallas_call(
        body, name=name, out_shape=tuple(pltpu.HBM(a.shape, a.dtype) for a in lands),
        in_specs=[_HBM] * nt + [_SEM, _SEM, _ANY], out_specs=tuple([_HBM] * nt),
        input_output_aliases={t: t for t in range(nt)}, compiler_params=_SPLIT_COPY,
    )(*lands, send_sems, recv_sems, after))


def _gather_forward(lands, layouts, name):
    nt = len(lands)

    def body(*refs):
        outs = refs[nt:2 * nt]
        send_sems, recv_sems = refs[2 * nt:]
        x, y, c = _place()
        sends, arrivals = [], []
        for t, lay in enumerate(layouts):
            for j, (px, py) in enumerate(_other_chips(x, y)):
                for pc, group in ((c, sends), (1 - c, arrivals)):
                    part = lay.half(outs[t], 2 * px + py, pc)
                    group.append(pltpu.make_async_remote_copy(
                        src_ref=part, dst_ref=part, send_sem=send_sems.at[3 * t + j], recv_sem=recv_sems.at[3 * t + j],
                        device_id=(x, y, 1 - c), device_id_type=_MESH))
        for cp in sends:
            cp.start()
        for cp in arrivals:
            cp.wait_recv()
        for cp in sends:
            cp.wait_send()

    return list(pl.pallas_call(
        body, name=name, out_shape=[jax.ShapeDtypeStruct(a.shape, a.dtype) for a in lands],
        in_specs=[_ANY] * nt, out_specs=[_ANY] * nt, input_output_aliases={t: t for t in range(nt)},
        scratch_shapes=[pltpu.SemaphoreType.DMA((3 * nt,)), pltpu.SemaphoreType.DMA((3 * nt,))],
        compiler_params=_cparams(),
    )(*lands))


def _pair_copies(grads, lands, send_sems, recv_sems):
    x, y, c = _place()
    out = []
    for t, (g, land) in enumerate(zip(grads, lands)):
        h = g.shape[1] // 2
        out.append(pltpu.make_async_remote_copy(
            src_ref=g.at[:, pl.ds((1 - c) * h, h), :], dst_ref=land, send_sem=send_sems.at[t],
            recv_sem=recv_sems.at[t], device_id=(x, y, 1 - c), device_id_type=_MESH))
    return out


def _pair_start(grads, after, name):
    nt = len(grads)
    lands = [lax.empty((N_CHIPS, g.shape[1] // 2, g.shape[2]), g.dtype) for g in grads]
    order = [] if after is None else [after]

    def body(*refs):
        send_sems, recv_sems = refs[2 * nt + len(order):2 * nt + len(order) + 2]
        token = refs[-1]
        for cp in _pair_copies(refs[:nt], refs[nt:2 * nt], send_sems, recv_sems):
            cp.start()
        token[...] = jnp.zeros_like(token)

    out = pl.pallas_call(
        body, name=name,
        out_shape=(pltpu.SemaphoreType.DMA((nt,)), pltpu.SemaphoreType.DMA((nt,)),
                   *[pltpu.HBM(a.shape, a.dtype) for a in grads + lands], _TOKEN),
        in_specs=[_HBM] * (2 * nt) + [_ANY] * len(order),
        out_specs=(_SEM, _SEM, *[_HBM] * (2 * nt), pl.BlockSpec(memory_space=pltpu.VMEM)),
        input_output_aliases={t: 2 + t for t in range(2 * nt)}, compiler_params=_SPLIT_COPY,
    )(*_in_hbm(grads + lands), *order)
    return out[0], out[1], list(out[2:2 + nt]), list(out[2 + nt:2 + 2 * nt]), out[-1]


def _pair_wait(send_sems, recv_sems, grads, lands, after, name):
    nt = len(grads)

    def body(*refs):
        for cp in _pair_copies(refs[:nt], refs[nt:2 * nt], *refs[2 * nt:2 * nt + 2]):
            cp.wait_send()
            cp.wait_recv()

    out = pl.pallas_call(
        body, name=name, out_shape=tuple(pltpu.HBM(a.shape, a.dtype) for a in grads + lands),
        in_specs=[_HBM] * (2 * nt) + [_SEM, _SEM, _ANY], out_specs=tuple([_HBM] * (2 * nt)),
        input_output_aliases={t: t for t in range(2 * nt)}, compiler_params=_SPLIT_COPY,
    )(*grads, *lands, send_sems, recv_sems, after)
    return list(out[:nt]), list(out[nt:])


def _pair_sum(own, recv, c_idx, name):
    _, h, cols = recv.shape

    def body(c_ref, own_ref, recv_ref, o_ref):
        o_ref[...] = (own_ref[...] + recv_ref[...]).astype(BF16)

    return pl.pallas_call(
        body, name=name,
        grid_spec=pltpu.PrefetchScalarGridSpec(
            num_scalar_prefetch=1, grid=(N_CHIPS,),
            in_specs=[pl.BlockSpec((None, h, cols), lambda k, c_ref: (k, c_ref[0], 0)),
                      pl.BlockSpec((None, h, cols), lambda k, c_ref: (k, 0, 0))],
            out_specs=pl.BlockSpec((None, h, cols), lambda k, c_ref: (k, 0, 0))),
        out_shape=jax.ShapeDtypeStruct(recv.shape, BF16), compiler_params=_cparams(("arbitrary",)),
    )(c_idx, own, recv)


def _chip_copies(parts, lands, send_sems, recv_sems):
    x, y, c = _place()
    out = []
    for t, (part, land) in enumerate(zip(parts, lands)):
        for j, (px, py) in enumerate(_other_chips(x, y)):
            out.append(pltpu.make_async_remote_copy(
                src_ref=part.at[2 * px + py], dst_ref=land.at[j], send_sem=send_sems.at[3 * t + j],
                recv_sem=recv_sems.at[3 * t + j], device_id=(px, py, c), device_id_type=_MESH))
    return out


def _chip_send_start(parts, after, name):
    nt = len(parts)
    lands = [lax.empty((N_CHIPS - 1,) + p.shape[1:], p.dtype) for p in parts]
    order = [] if after is None else [after]

    def body(*refs):
        send_sems, recv_sems = refs[2 * nt + len(order):2 * nt + len(order) + 2]
        token = refs[-1]
        for cp in _chip_copies(refs[:nt], refs[nt:2 * nt], send_sems, recv_sems):
            cp.start()
        token[...] = jnp.zeros_like(token)

    out = pl.pallas_call(
        body, name=name,
        out_shape=(pltpu.SemaphoreType.DMA((3 * nt,)), pltpu.SemaphoreType.DMA((3 * nt,)),
                   *[pltpu.HBM(a.shape, a.dtype) for a in parts + lands], _TOKEN),
        in_specs=[_HBM] * (2 * nt) + [_ANY] * len(order),
        out_specs=(_SEM, _SEM, *[_HBM] * (2 * nt), pl.BlockSpec(memory_space=pltpu.VMEM)),
        input_output_aliases={t: 2 + t for t in range(2 * nt)}, compiler_params=_SPLIT_COPY,
    )(*_in_hbm(parts + lands), *order)
    return out[0], out[1], list(out[2:2 + nt]), list(out[2 + nt:2 + 2 * nt]), out[-1]


def _chip_send_wait(send_sems, recv_sems, parts, lands, after, name):
    nt = len(parts)

    def body(*refs):
        for cp in _chip_copies(refs[:nt], refs[nt:2 * nt], *refs[2 * nt:2 * nt + 2]):
            cp.wait_send()
            cp.wait_recv()

    out = pl.pallas_call(
        body, name=name, out_shape=tuple(pltpu.HBM(a.shape, a.dtype) for a in parts + lands),
        in_specs=[_HBM] * (2 * nt) + [_SEM, _SEM, _ANY], out_specs=tuple([_HBM] * (2 * nt)),
        input_output_aliases={t: t for t in range(2 * nt)}, compiler_params=_SPLIT_COPY,
    )(*parts, *lands, send_sems, recv_sems, after)
    return list(out[:nt]), list(out[nt:])


def _chip_sum(part, arrived, into, lead, place_idx, name):
    _, h, cols = part.shape

    def body(idx_ref, own_ref, arr_ref, into_ref, o_ref):
        acc = own_ref[...].astype(F32)
        for k in range(N_CHIPS - 1):
            acc = acc + arr_ref[k].astype(F32)
        o_ref[...] = acc

    return pl.pallas_call(
        body, name=name,
        grid_spec=pltpu.PrefetchScalarGridSpec(
            num_scalar_prefetch=1, grid=(1,),
            in_specs=[pl.BlockSpec((None, h, cols), lambda g, idx: (idx[1], 0, 0)),
                      pl.BlockSpec((N_CHIPS - 1, h, cols), lambda g, idx: (0, 0, 0)), _ANY],
            out_specs=pl.BlockSpec((None,) * len(lead) + (h, cols), lambda g, idx: (*lead, idx[0], 0))),
        out_shape=jax.ShapeDtypeStruct(into.shape, F32), input_output_aliases={3: 0},
        compiler_params=_cparams(("arbitrary",)),
    )(place_idx, part, arrived, into)


def _pair_gather(bufs, homes, name):
    nt, nb = len(homes), len(bufs)

    def body(*refs):
        outs = refs[nb:2 * nb]
        send_sems, recv_sems = refs[2 * nb:]
        x, y, c = _place()

        def home(t, pc):
            o, lead, rows = homes[t]
            return outs[o].at[(*lead, pl.ds(pc * (rows // 2), rows // 2), slice(None))]

        def copy(t, pc):
            return pltpu.make_async_remote_copy(src_ref=home(t, pc), dst_ref=home(t, pc), send_sem=send_sems.at[t],
                                                recv_sem=recv_sems.at[t], device_id=(x, y, 1 - c), device_id_type=_MESH)

        sends = [copy(t, c) for t in range(nt)]
        for cp in sends:
            cp.start()
        for t in range(nt):
            copy(t, 1 - c).wait_recv()
        for cp in sends:
            cp.wait_send()

    return pl.pallas_call(
        body, name=name, out_shape=[jax.ShapeDtypeStruct(b.shape, b.dtype) for b in bufs],
        in_specs=[_ANY] * nb, out_specs=[_ANY] * nb, input_output_aliases={o: o for o in range(nb)},
        scratch_shapes=[pltpu.SemaphoreType.DMA((nt,)), pltpu.SemaphoreType.DMA((nt,))],
        compiler_params=_cparams(),
    )(*bufs)


def _sum_devices(g, after, name):
    def body(g_ref, after_ref, o_ref):
        acc = g_ref[0:1, :]
        for d in range(1, N_DEV):
            acc = acc + g_ref[d:d + 1, :]
        o_ref[...] = acc
    vmem = pl.BlockSpec(memory_space=pltpu.VMEM)
    return pl.pallas_call(body, name=name, out_shape=jax.ShapeDtypeStruct((1, g.shape[1]), F32),
                          in_specs=[vmem, _ANY], out_specs=vmem, compiler_params=_cparams())(g, after)


_WEIGHTS = ("w_cond", "b_cond", "norm_pre", "norm_post", "w_ffn_in", "w_ffn_out", "fox_w_in", "fox_b_f",
            "fox_w_out", "sconv_w_in", "sconv_conv_w", "sconv_w_out", "lru_w_in", "lru_conv_w", "lru_conv_b",
            "lru_w_a", "lru_b_a", "lru_w_x", "lru_b_x", "lru_lambda", "lru_w_out")
_BIG = (("w_ffn_in", False), ("w_ffn_out", True), ("fox_w_in", False), ("fox_w_out", True),
        ("sconv_w_in", False), ("sconv_w_out", True), ("lru_w_in", False), ("lru_w_out", True))
_SMALL = tuple(n for n in _WEIGHTS if n != "w_cond" and n not in dict(_BIG))
_COL_SHARDED_SMALL = ("norm_pre", "norm_post", "sconv_conv_w", "lru_conv_w", "lru_conv_b", "lru_lambda")


def _pack_rows(parts, rows=8):
    flat = jnp.concatenate([p.reshape(-1) for p in parts])
    width = -(-flat.size // (rows * 128)) * 128
    return jnp.pad(flat, (0, rows * width - flat.size)).reshape(rows, width)


def _unpack(flat, shapes):
    out, off = [], 0
    for shp in shapes:
        n = math.prod(shp)
        out.append(flat[off:off + n].reshape(shp))
        off += n
    return out


def _join_chips(g):
    g = jnp.moveaxis(g, 0, -2)
    return g.reshape(g.shape[:-2] + (g.shape[-2] * g.shape[-1],))


def _my_columns(full, chip):
    n = full.shape[-1] // N_CHIPS
    return lax.dynamic_slice_in_dim(full, chip * n, n, axis=full.ndim - 1)


def _block_diag(w):
    eye = jnp.eye(LRU_BLOCKS, dtype=w.dtype)
    return jnp.einsum("nij,nm->nimj", w, eye).reshape(D_MODEL, D_MODEL)


def _step(x, c, target, wts, mom, var):
    ix, iy, ic = _place()
    chip = 2 * ix + iy
    dev = 2 * chip + ic
    n_cond = wts["w_cond"].shape[2]

    small_shapes = [(D_MODEL,)] + [wts[n].shape for n in _COL_SHARDED_SMALL]
    g1 = _allgather8(_pack_rows([c[0]] + [wts[n] for n in _COL_SHARDED_SMALL]), "gather_small").reshape(N_DEV, -1)
    c_all = g1[:, :D_MODEL]
    per_chip = [jnp.stack(col) for col in zip(*[_unpack(g1[2 * k], small_shapes) for k in range(N_CHIPS)])]
    small_full = {n: _join_chips(v) for n, v in zip(_COL_SHARDED_SMALL, per_chip[1:])}

    c_pad = jnp.pad(c_all, ((0, COND_ROWS - N_DEV), (0, 0)))
    b_shard = _my_columns(wts["b_cond"], chip)[:, None, :]
    mod_part = _cond_fwd(c_pad, wts["w_cond"], b_shard, "cond_fwd")
    g2 = _allgather8(mod_part[:, :N_DEV].transpose(1, 0, 2).reshape(N_DEV, DEPTH * n_cond), "gather_mod")
    g2 = g2.reshape(N_DEV, N_DEV, DEPTH, n_cond)[0::2]
    mod = _join_chips(lax.dynamic_index_in_dim(g2, dev, axis=1, keepdims=False)).reshape(DEPTH, N_SUB, 3, D_MODEL)

    mixer_names = [("fox_w_in", "fox_w_out"), ("sconv_w_in", "sconv_w_out"), ("lru_w_in", "lru_w_out")]

    def shards_of(i, sub):
        if sub == 1:
            return [wts[n][i // 3] for n in mixer_names[i % 3]]
        return [wts["w_ffn_in"][i, sub // 2], wts["w_ffn_out"][i, sub // 2]]

    chunks = [[(0, 0)], [(0, 1), (0, 2)]] + [[(i, sub) for sub in range(N_SUB)] for i in range(1, DEPTH)]
    in_flight, chunk_of, token = [], {}, mod
    for k, members in enumerate(chunks):
        shards = [s for i, sub in members for s in shards_of(i, sub)]
        layouts = [_Gathered(s.shape, 0) for s in shards]
        if k:
            shards = [s + token[0, 0] for s in shards]
        lands = [_own_block_placed(s.astype(BF16), lay, chip) for s, lay in zip(shards, layouts)]
        send_sems, recv_sems, lands, token = _gather_start(lands, layouts, token, f"gather_start_{k}")
        in_flight.append([send_sems, recv_sems, lands, layouts, False])
        chunk_of.update({m: (k, 2 * pos) for pos, m in enumerate(members)})
    lru_ax = jnp.concatenate([_block_diag(wts["lru_w_a"][0]), _block_diag(wts["lru_w_x"][0])], axis=1).astype(BF16)

    def layer_params(i, sub, x_in):
        k, pos = chunk_of[(i, sub)]
        send_sems, recv_sems, lands, layouts, arrived = in_flight[k]
        if not arrived:
            lands = _gather_wait(send_sems, recv_sems, lands, layouts, x_in, f"gather_wait_{k}")
            in_flight[k][2:] = [_gather_forward(lands, layouts, f"gather_forward_{k}"), layouts, True]
        w_in, w_out = in_flight[k][2][pos:pos + 2]
        w_out = w_out.reshape(-1, w_out.shape[-1])
        if sub != 1:
            out = {"ffn_in": [_W(w_in, (), True)], "ffn_out": [_W(w_out)]}
            if sub == 0:
                out.update(norm_pre=small_full["norm_pre"][i], norm_post=small_full["norm_post"][i])
            return out
        j = i // 3
        if i % 3 == 0:
            w_in = jnp.pad(_join_chips(w_in), ((0, 0), (0, FOX_PAD - 3 * D_MODEL - FOX_HEADS)))
            return {"mixer": {"w_in": _W(w_in), "w_out": _W(w_out), "b_f": wts["fox_b_f"][j][:, None]}}
        if i % 3 == 1:
            return {"mixer": {"w_in": _W(w_in, (), True), "w_out": _W(w_out), "conv_w": small_full["sconv_conv_w"][j]}}
        return {"mixer": {"w_in": _W(w_in, (), True), "w_out": _W(w_out), "conv_w": small_full["lru_conv_w"][j],
                          "conv_b": small_full["lru_conv_b"], "w_ax": _W(lru_ax),
                          "b_a": wts["lru_b_a"].reshape(1, D_MODEL), "b_x": wts["lru_b_x"].reshape(1, D_MODEL),
                          "lam": small_full["lru_lambda"]}}

    place_idx = jnp.stack([ic, chip]).astype(jnp.int32)
    c_idx = place_idx[:1]
    big_index = {n: o for o, (n, _) in enumerate(_BIG)}
    exchanges, pending = [], []

    def to_chips(after):
        i, send_sems, recv_sems, tensors, lands, homes = pending.pop()
        tensors, recv = _pair_wait(send_sems, recv_sems, tensors, lands, after, f"grads_pair_wait_l{i}")
        parts = [_pair_sum(t, r, c_idx, f"grads_pair_sum_l{i}_{k}") for k, (t, r) in enumerate(zip(tensors, recv))]
        send_sems, recv_sems, parts, lands, tok = _chip_send_start(parts, None, f"grads_chip_start_l{i}")
        exchanges.append((i, send_sems, recv_sems, parts, lands, homes))
        return tok

    def chip_blocks(g, by_rows, width):
        if by_rows:
            return g.reshape(N_CHIPS, g.shape[0] // N_CHIPS, g.shape[1])
        if g.ndim == 3:
            return g
        return g[:, :width * N_CHIPS].reshape(g.shape[0], N_CHIPS, width).transpose(1, 0, 2)

    def on_mid(i, dx):
        return to_chips(dx) if pending else None

    def on_grads(i, g, dx):
        n_in, n_out = mixer_names[i % 3]
        items = [("w_ffn_in", (i, k), g["ffn_in"][k]) for k in range(2)]
        items += [("w_ffn_out", (i, k), g["ffn_out"][k]) for k in range(2)]
        items += [(n_in, (i // 3,), g["mixer"]["w_in"]), (n_out, (i // 3,), g["mixer"]["w_out"])]
        tensors = [chip_blocks(t, dict(_BIG)[n], wts[n].shape[-1]) for n, _, t in items]
        homes = [(big_index[n], lead, wts[n].shape[-2]) for n, lead, _ in items]
        send_sems, recv_sems, tensors, lands, tok = _pair_start(tensors, None, f"grads_pair_start_l{i}")
        pending.append((i, send_sems, recv_sems, tensors, lands, homes))
        pair_tokens.append(tok)
        return tok

    pair_tokens = []
    loss_row, grad_x, dmod, lg = _local_step(x[0], target[0], mod, layer_params, on_grads, on_mid, token)
    loss = lax.psum(loss_row[0, 0], ("x", "y", "c"))
    dmod = dmod + pair_tokens[-1][0, 0]

    fox_layers = [i for i in range(DEPTH) if i % 3 == 0]
    sconv_g, lru_g = lg[1]["mixer"], lg[2]["mixer"]
    small_g = {
        "dmod": dmod, "norm_pre": jnp.stack([g["norm_pre"] for g in lg]), "norm_post": jnp.stack([g["norm_post"] for g in lg]),
        "fox_b_f": jnp.stack([lg[i]["mixer"]["b_f"][:, 0] for i in fox_layers]),
        "sconv_conv_w": sconv_g["conv_w"][None], "lru_conv_w": lru_g["conv_w"][None], "lru_conv_b": lru_g["conv_b"],
        "lru_w_a": lru_g["w_a"][None], "lru_b_a": lru_g["b_a"].reshape(1, LRU_BLOCKS, LRU_BLOCK_DIM),
        "lru_w_x": lru_g["w_x"][None], "lru_b_x": lru_g["b_x"].reshape(1, LRU_BLOCKS, LRU_BLOCK_DIM),
        "lru_lambda": lru_g["lam"]}
    g4 = _allgather8(_pack_rows(list(small_g.values())), "gather_small_grads").reshape(N_DEV, -1)
    last_start = to_chips(g4)
    summed = _sum_devices(g4, last_start, "sum_small_grads")[0]
    summed = dict(zip(small_g, _unpack(summed, [v.shape for v in small_g.values()])))
    grads = {n: (_my_columns(summed[n], chip) if n in _COL_SHARDED_SMALL else summed[n]) for n in _SMALL if n != "b_cond"}
    grads["b_cond"] = summed["dmod"].reshape(DEPTH, N_SUB * 3 * D_MODEL)

    dmod_all = (g4[:, :dmod.size] + last_start[0, 0]).reshape(N_DEV, DEPTH, N_SUB * 3 * D_MODEL)
    dmod_s = jnp.pad(_my_columns(dmod_all, chip).transpose(1, 0, 2), ((0, 0), (0, COND_PAD - N_DEV), (0, 0))).astype(BF16)
    c_t = jnp.pad(c_all.T, ((0, 0), (0, COND_PAD - N_DEV)))
    grads["w_cond"], d_cond, m_cond, v_cond = _cond_bwd_adamw(c_t, dmod_s, wts["w_cond"], mom["w_cond"],
                                                              var["w_cond"], "cond_bwd_adamw")

    bufs = [lax.empty(wts[n].shape, F32) for n, _ in _BIG]
    all_homes = []
    for i, send_sems, recv_sems, parts, lands, homes in exchanges:
        follows = d_cond if not all_homes else bufs[0]
        parts, lands = _chip_send_wait(send_sems, recv_sems, parts, lands, follows, f"grads_chip_wait_l{i}")
        for k, (part, land, (o, lead, _)) in enumerate(zip(parts, lands, homes)):
            bufs[o] = _chip_sum(part, land, bufs[o], lead, place_idx, f"grads_chip_sum_l{i}_{k}")
        all_homes += homes
    grads.update(zip([n for n, _ in _BIG], _pair_gather(bufs, all_homes, "grads_pair_gather")))

    delta, new_m, new_v = {"w_cond": d_cond}, {"w_cond": m_cond}, {"w_cond": v_cond}
    for n, _ in _BIG:
        two_d = lambda a: a.reshape(-1, a.shape[-1])
        d, nm, nv = _adamw(two_d(wts[n]), two_d(grads[n]), two_d(mom[n]), two_d(var[n]), "adamw_" + n)
        delta[n], new_m[n], new_v[n] = (a.reshape(wts[n].shape) for a in (d, nm, nv))
    shapes = [wts[n].shape for n in _SMALL]
    packed = [_pack_rows([src[n] for n in _SMALL]) for src in (wts, grads, mom, var)]
    for dst, out in zip((delta, new_m, new_v), _adamw(*packed, "adamw_small")):
        dst.update(zip(_SMALL, _unpack(out.reshape(-1), shapes)))

    return (loss, grad_x[None], *[grads[n] for n in _WEIGHTS], *[delta[n] for n in _WEIGHTS],
            *[new_m[n] for n in _WEIGHTS], *[new_v[n] for n in _WEIGHTS])


def kernel(x, c, w_cond, b_cond, norm_pre, norm_post, w_ffn_in, w_ffn_out, fox_w_in, fox_b_f, fox_w_out, sconv_w_in, sconv_conv_w, sconv_w_out, lru_w_in, lru_conv_w, lru_conv_b, lru_w_a, lru_b_a, lru_w_x, lru_b_x, lru_lambda, lru_w_out, loss_target, m_w_cond, m_b_cond, m_norm_pre, m_norm_post, m_w_ffn_in, m_w_ffn_out, m_fox_w_in, m_fox_b_f, m_fox_w_out, m_sconv_w_in, m_sconv_conv_w, m_sconv_w_out, m_lru_w_in, m_lru_conv_w, m_lru_conv_b, m_lru_w_a, m_lru_b_a, m_lru_w_x, m_lru_b_x, m_lru_lambda, m_lru_w_out, v_w_cond, v_b_cond, v_norm_pre, v_norm_post, v_w_ffn_in, v_w_ffn_out, v_fox_w_in, v_fox_b_f, v_fox_w_out, v_sconv_w_in, v_sconv_conv_w, v_sconv_w_out, v_lru_w_in, v_lru_conv_w, v_lru_conv_b, v_lru_w_a, v_lru_b_a, v_lru_w_x, v_lru_b_x, v_lru_lambda, v_lru_w_out):
    given = dict(locals())
    wts = {n: given[n] for n in _WEIGHTS}
    mom = {n: given["m_" + n] for n in _WEIGHTS}
    var = {n: given["v_" + n] for n in _WEIGHTS}
    return _step(x, c, loss_target, wts, mom, var)
```

```python
import functools
import math
from typing import NamedTuple

import jax
import jax.numpy as jnp
from jax import lax
from jax.experimental import pallas as pl
from jax.experimental.pallas import tpu as pltpu

F32 = jnp.float32
BF16 = jnp.bfloat16

D_MODEL = 1024
DEPTH = 4
N_SUB = 3
D_FF = 2816
RMS_EPS = 1e-6
FOX_HEADS = 16
FOX_HEAD_DIM = 64
FOX_PAD = 3200
LRU_BLOCKS = 16
LRU_BLOCK_DIM = 64
LRU_C = 8.0
N_CHIPS = 4
N_DEV = 8

ADAM_LR = 0.001
ADAM_B1 = 0.9
ADAM_B2 = 0.999
ADAM_EPS = 1e-08
ADAM_WD = 0.01
ADAM_STEP = 10

VMEM_LIMIT_V7X = 56 * 1024 * 1024
ROW_TILE = 512
COL_TILE = 256
ATT_TILE = 256
ATT_WIDE = 512
MM_ROWS = 1024


def _cparams(sem=None):
    return pltpu.CompilerParams(vmem_limit_bytes=VMEM_LIMIT_V7X, dimension_semantics=sem)


def _sigmoid(z):
    return 1.0 / (1.0 + jnp.exp(-z))


def _softplus(z):
    return jnp.maximum(z, 0.0) + jnp.log(1.0 + jnp.exp(-jnp.abs(z)))


def _rows_sum(v):
    return jnp.sum(v, axis=0, keepdims=True)


class _W(NamedTuple):
    arr: jax.Array
    prefix: tuple = ()
    blocked: bool = False


def _w_spec(w, block2, pos):
    lead = (None,) * (len(w.prefix) + (1 if w.blocked else 0))
    if w.blocked:
        return pl.BlockSpec(lead + block2, lambda *g: (pos(*g)[0], *w.prefix, pos(*g)[1], pos(*g)[2]))
    return pl.BlockSpec(lead + block2, lambda *g: (*w.prefix, pos(*g)[1], pos(*g)[2]))


def _mm_nn(a, b, name, tn=None):
    m, k = a.shape
    if b.blocked:
        steps, bn = b.arr.shape[0], b.arr.shape[-1]
        b_spec = _w_spec(b, (k, bn), lambda n: (n, 0, 0))
    else:
        n_total = b.arr.shape[-1]
        bn = n_total if tn is None else tn
        steps = n_total // bn
        assert steps * bn == n_total
        b_spec = _w_spec(b, (k, bn), lambda n: (0, 0, n))
    tm = min(MM_ROWS, m)

    def body(a_ref, b_ref, o_ref):
        def step(i, carry):
            r = pl.ds(pl.multiple_of(i * tm, tm), tm)
            o_ref[r, :] = jnp.dot(a_ref[r, :], b_ref[...], preferred_element_type=F32)
            return carry
        lax.fori_loop(0, m // tm, step, 0)

    return pl.pallas_call(
        body, name=name, grid=(steps,),
        in_specs=[pl.BlockSpec((m, k), lambda n: (0, 0)), b_spec],
        out_specs=pl.BlockSpec((m, bn), lambda n: (0, n)),
        out_shape=jax.ShapeDtypeStruct((m, steps * bn), F32),
        compiler_params=_cparams(("arbitrary",)),
    )(a, b.arr)


def _cols_shape(dy):
    return (dy.shape[0], dy.shape[1]) if dy.ndim == 2 else (dy.shape[1], 2 * dy.shape[2])


def _cols_spec(dy, bn):
    if dy.ndim == 2:
        return pl.BlockSpec((dy.shape[0], bn), lambda kt, n: (0, n))
    per = dy.shape[2] // bn
    assert per * bn == dy.shape[2]
    return pl.BlockSpec((None, dy.shape[1], bn), lambda kt, n: (n // per, 0, n % per))


def _mm_nt(dy, w, name, tk=None, tn=None):
    m, n_total = _cols_shape(dy)
    k = w.arr.shape[-2]
    if w.blocked:
        bk, bn = k, w.arr.shape[-1]
        grid = (1, w.arr.shape[0])
        w_spec = _w_spec(w, (k, bn), lambda kt, n: (n, 0, 0))
    else:
        bk = k if tk is None else tk
        bn = n_total if tn is None else tn
        grid = (k // bk, n_total // bn)
        assert grid[0] * bk == k and grid[1] * bn == n_total
        w_spec = _w_spec(w, (bk, bn), lambda kt, n: (0, kt, n))
    tm = min(MM_ROWS, m)

    reduce_steps = grid[1]

    def body(dy_ref, w_ref, o_ref):
        def step(i, carry):
            r = pl.ds(pl.multiple_of(i * tm, tm), tm)
            part = lax.dot_general(dy_ref[r, :], w_ref[...], (((1,), (1,)), ((), ())), preferred_element_type=F32)
            if reduce_steps == 1:
                o_ref[r, :] = part
            else:
                o_ref[r, :] += part
            return carry

        if reduce_steps > 1:
            @pl.when(pl.program_id(1) == 0)
            def _():
                o_ref[...] = jnp.zeros_like(o_ref)
        lax.fori_loop(0, m // tm, step, 0)

    return pl.pallas_call(
        body, name=name, grid=grid,
        in_specs=[_cols_spec(dy, bn), w_spec],
        out_specs=pl.BlockSpec((m, bk), lambda kt, n: (0, kt)),
        out_shape=jax.ShapeDtypeStruct((m, k), F32),
        compiler_params=_cparams(("arbitrary", "arbitrary")),
    )(dy, w.arr)


def _mm_tn(x, dy, name, tk=None, tn=None, blocked_out=False):
    s, k = x.shape
    n_total = _cols_shape(dy)[1]
    bk = k if tk is None else tk
    bn = n_total if tn is None else tn
    grid = (k // bk, n_total // bn)
    assert grid[0] * bk == k and grid[1] * bn == n_total
    ck = next(c for c in (512, 256, 128) if bk % c == 0)

    def body(x_ref, dy_ref, o_ref):
        def step(i, carry):
            c = pl.ds(pl.multiple_of(i * ck, ck), ck)
            o_ref[c, :] = lax.dot_general(x_ref[:, c], dy_ref[...], (((0,), (0,)), ((), ())),
                                          preferred_element_type=F32)
            return carry
        lax.fori_loop(0, bk // ck, step, 0)

    if blocked_out:
        assert grid[0] == 1
        out_spec = pl.BlockSpec((None, bk, bn), lambda kt, n: (n, 0, 0))
        out_shape = jax.ShapeDtypeStruct((grid[1], k, bn), F32)
    else:
        out_spec = pl.BlockSpec((bk, bn), lambda kt, n: (kt, n))
        out_shape = jax.ShapeDtypeStruct((k, n_total), F32)
    return pl.pallas_call(
        body, name=name, grid=grid,
        in_specs=[pl.BlockSpec((s, bk), lambda kt, n: (0, kt)), _cols_spec(dy, bn)],
        out_specs=out_spec, out_shape=out_shape,
        compiler_params=_cparams(("arbitrary", "arbitrary")),
    )(x, dy)


def _row_call(name, body, rows, fulls, row_outs, acc_outs, tr=ROW_TILE, after=None):
    s = rows[0].shape[0]
    tr = min(tr, s)
    in_specs = [pl.BlockSpec((tr, a.shape[1]), lambda i: (i, 0)) for a in rows]
    in_specs += [pl.BlockSpec(a.shape, lambda i: (0, 0)) for a in fulls]
    n_in = len(in_specs)
    order = [] if after is None else [after]
    in_specs += [pl.BlockSpec(memory_space=pl.ANY)] * len(order)
    out_specs = [pl.BlockSpec((tr, c), lambda i: (i, 0)) for c, _ in row_outs]
    out_specs += [pl.BlockSpec((1, c), lambda i: (0, 0)) for c, _ in acc_outs]
    out_shape = [jax.ShapeDtypeStruct((s, c), dt) for c, dt in row_outs]
    out_shape += [jax.ShapeDtypeStruct((1, c), dt) for c, dt in acc_outs]
    n_acc = len(acc_outs)

    def wrapped(*refs):
        refs = refs[:n_in] + refs[n_in + len(order):]
        if n_acc:
            @pl.when(pl.program_id(0) == 0)
            def _():
                for r in refs[len(refs) - n_acc:]:
                    r[...] = jnp.zeros_like(r)
        body(*refs)

    return pl.pallas_call(
        wrapped, name=name, grid=(s // tr,), in_specs=in_specs, out_specs=out_specs, out_shape=out_shape,
        compiler_params=_cparams(("arbitrary",)),
    )(*rows, *fulls, *order)


def _rms(v):
    return lax.rsqrt(jnp.mean(v * v, axis=-1, keepdims=True) + RMS_EPS)


def _pre_norm(x, g_pre, scale, shift, name, after=None):
    def body(x_ref, g_ref, sc_ref, sh_ref, h_ref):
        xv = x_ref[...]
        h = (xv * _rms(xv)) * g_ref[...] * (1.0 + sc_ref[...]) + sh_ref[...]
        h_ref[...] = h.astype(BF16)
    return _row_call(name, body, [x], [g_pre, scale, shift], [(D_MODEL, BF16)], [], after=after)[0]


def _post_norm(x, y, g_post, gate, coef, name):
    def body(x_ref, y_ref, g_ref, gate_ref, o_ref):
        yv = y_ref[...]
        o_ref[...] = x_ref[...] + (coef * gate_ref[...]) * ((yv * _rms(yv)) * g_ref[...])
    return _row_call(name, body, [x, y], [g_post, gate], [(D_MODEL, F32)], [])[0]


def _post_norm_bwd(dxo, y, g_post, gate, coef, name, after=None):
    def body(dxo_ref, y_ref, g_ref, gate_ref, dy_ref, dgate_ref, dg_ref):
        yv = y_ref[...]
        r2 = _rms(yv)
        yn = yv * r2
        dxo_v = dxo_ref[...]
        dgate_ref[...] += _rows_sum(dxo_v * (yn * g_ref[...])) * coef
        dz = dxo_v * (coef * gate_ref[...])
        dg_ref[...] += _rows_sum(dz * yn)
        dyn = dz * g_ref[...]
        dy = r2 * (dyn - yn * jnp.mean(dyn * yn, axis=-1, keepdims=True))
        dy_ref[...] = dy.astype(BF16)
    return _row_call(name, body, [dxo, y], [g_post, gate], [(D_MODEL, BF16)], [(D_MODEL, F32), (D_MODEL, F32)],
                     after=after)


def _pre_norm_bwd(dxo, dh, x, g_pre, scale, name):
    def body(dxo_ref, dh_ref, x_ref, g_ref, sc_ref, dx_ref, dshift_ref, dscale_ref, dg_ref):
        xv = x_ref[...]
        r = _rms(xv)
        xn = xv * r
        dh_v = dh_ref[...]
        one_sc = 1.0 + sc_ref[...]
        dshift_ref[...] += _rows_sum(dh_v)
        dscale_ref[...] += _rows_sum(dh_v * (xn * g_ref[...]))
        dg_ref[...] += _rows_sum(dh_v * xn * one_sc)
        dxn = dh_v * (g_ref[...] * one_sc)
        dx_ref[...] = dxo_ref[...] + r * (dxn - xn * jnp.mean(dxn * xn, axis=-1, keepdims=True))
    return _row_call(name, body, [dxo, dh, x], [g_pre, scale], [(D_MODEL, F32)],
                     [(D_MODEL, F32), (D_MODEL, F32), (D_MODEL, F32)])


FFN_COLS = 1408


def _ffn_in_act(h, w_in, name):
    m, k = h.shape
    half, bn = w_in.arr.shape[0] // 2, w_in.arr.shape[-1]
    assert bn == FFN_COLS and half * bn == D_FF
    tm = min(MM_ROWS, m)

    def body(h_ref, wg_ref, wu_ref, g_ref, u_ref, a_ref):
        g = jnp.dot(h_ref[...], wg_ref[...], preferred_element_type=F32)
        g_ref[...] = g
        u = jnp.dot(h_ref[...], wu_ref[...], preferred_element_type=F32)
        u_ref[...] = u
        a_ref[...] = (g * _sigmoid(g) * u).astype(BF16)

    tile = pl.BlockSpec((tm, bn), lambda t, i: (i, t))
    return pl.pallas_call(
        body, name=name, grid=(half, m // tm),
        in_specs=[pl.BlockSpec((tm, k), lambda t, i: (i, 0)),
                  _w_spec(w_in, (k, bn), lambda t, i: (t, 0, 0)),
                  _w_spec(w_in, (k, bn), lambda t, i: (half + t, 0, 0))],
        out_specs=[tile, tile, tile],
        out_shape=[jax.ShapeDtypeStruct((m, D_FF), F32)] * 2 + [jax.ShapeDtypeStruct((m, D_FF), BF16)],
        compiler_params=_cparams(("arbitrary", "arbitrary")),
    )(h, w_in.arr, w_in.arr)


def _ffn_out_bx_act(dy, w_out, g, u, name):
    m = dy.shape[0]
    tr = min(MM_ROWS, m)

    def body(dy_ref, w_ref, g_ref, u_ref, dgu_ref):
        da = lax.dot_general(dy_ref[...], w_ref[...], _NT, preferred_element_type=F32)
        gv = g_ref[...]
        sg = _sigmoid(gv)
        dgu_ref[0] = (da * u_ref[...] * (sg * (1.0 + gv * (1.0 - sg)))).astype(BF16)
        dgu_ref[1] = (da * (gv * sg)).astype(BF16)

    tile = pl.BlockSpec((tr, FFN_COLS), lambda i, c: (i, c))
    return pl.pallas_call(
        body, name=name, grid=(m // tr, D_FF // FFN_COLS),
        in_specs=[pl.BlockSpec((tr, D_MODEL), lambda i, c: (i, 0)),
                  _w_spec(w_out, (FFN_COLS, D_MODEL), lambda i, c: (0, c, 0)), tile, tile],
        out_specs=pl.BlockSpec((2, tr, FFN_COLS), lambda i, c: (0, i, c)),
        out_shape=jax.ShapeDtypeStruct((2, m, D_FF), BF16),
        compiler_params=_cparams(("arbitrary", "arbitrary")),
    )(dy, w_out.arr, g, u)


def _ffn_in_bwd(dgu, h, w_in, name):
    m, k = h.shape
    nb, bn = w_in.arr.shape[0], w_in.arr.shape[-1]
    per = dgu.shape[2] // bn
    tm = min(MM_ROWS, m)
    ck = next(c for c in (512, 256, 128) if k % c == 0)
    once = pl.Buffered(1)

    def body(dgu_ref, h_ref, w_ref, dh_ref, dw_ref):
        @pl.when(pl.program_id(0) == 0)
        def _():
            dh_ref[...] = jnp.zeros_like(dh_ref)

        def rows(i, carry):
            r = pl.ds(pl.multiple_of(i * tm, tm), tm)
            dh_ref[r, :] += lax.dot_general(dgu_ref[r, :], w_ref[...], _NT, preferred_element_type=F32)
            return carry
        lax.fori_loop(0, m // tm, rows, 0)

        def cols(i, carry):
            c = pl.ds(pl.multiple_of(i * ck, ck), ck)
            dw_ref[c, :] = lax.dot_general(h_ref[:, c], dgu_ref[...], (((0,), (0,)), ((), ())),
                                           preferred_element_type=F32)
            return carry
        lax.fori_loop(0, k // ck, cols, 0)

    return pl.pallas_call(
        body, name=name, grid=(nb,),
        in_specs=[pl.BlockSpec((None, m, bn), lambda n: (n // per, 0, n % per)),
                  pl.BlockSpec((m, k), lambda n: (0, 0), pipeline_mode=once),
                  _w_spec(w_in, (k, bn), lambda n: (n, 0, 0))],
        out_specs=[pl.BlockSpec((m, k), lambda n: (0, 0), pipeline_mode=once),
                   pl.BlockSpec((None, k, bn), lambda n: (n, 0, 0))],
        out_shape=[jax.ShapeDtypeStruct((m, k), F32), jax.ShapeDtypeStruct((nb, k, bn), F32)],
        compiler_params=_cparams(("arbitrary",)),
    )(dgu, h, w_in.arr)


def _loss_head(y, target, name):
    def body(y_ref, t_ref, dy_ref, loss_ref):
        e = y_ref[...] - t_ref[...]
        dy_ref[...] = e * (1.0 / D_MODEL)
        part = jnp.sum(jnp.mean(e * e, axis=-1, keepdims=True), axis=0, keepdims=True) * 0.5
        loss_ref[...] += jnp.broadcast_to(part, loss_ref.shape)
    return _row_call(name, body, [y, target], [], [(D_MODEL, F32)], [(128, F32)])


def _lane_scan(v, reverse):
    s = v.shape[1]
    lane = lax.broadcasted_iota(jnp.int32, v.shape, 1)
    d = 1
    while d < s:
        if reverse:
            v = v + jnp.where(lane < s - d, pltpu.roll(v, s - d, 1), 0.0)
        else:
            v = v + jnp.where(lane >= d, pltpu.roll(v, d, 1), 0.0)
        d *= 2
    return v


def _fox_gate(flt, b_f, name):
    def body(f_ref, b_ref, cum_ref):
        z = f_ref[...] + b_ref[...]
        cum_ref[...] = _lane_scan(-_softplus(-z), reverse=False)
    return pl.pallas_call(body, name=name, out_shape=jax.ShapeDtypeStruct(flt.shape, F32),
                          compiler_params=_cparams())(flt, b_f)


def _fox_gate_bwd(dcum_q, dcum_k, flt, b_f, name):
    def body(dq_ref, dk_ref, f_ref, b_ref, df_ref, db_ref):
        z = f_ref[...] + b_ref[...]
        df = _lane_scan(dq_ref[...] + dk_ref[...], reverse=True) * _sigmoid(-z)
        df_ref[...] = df
        db_ref[...] = jnp.sum(df, axis=1, keepdims=True)
    h = flt.shape[0]
    return pl.pallas_call(body, name=name,
                          out_shape=(jax.ShapeDtypeStruct(flt.shape, F32), jax.ShapeDtypeStruct((h, 1), F32)),
                          compiler_params=_cparams())(dcum_q, dcum_k, flt, b_f)


def _pick_head(block, h):
    lane = lax.broadcasted_iota(jnp.int32, block.shape, 1)
    return jnp.sum(jnp.where(lane == h, block, 0.0), axis=1, keepdims=True)


def _put_head(ref, col, h):
    @pl.when(h == 0)
    def _():
        ref[...] = jnp.zeros_like(ref)
    lane = lax.broadcasted_iota(jnp.int32, ref.shape, 1)
    ref[...] = jnp.where(lane == h, col, ref[...])


_NT = (((1,), (1,)), ((), ()))
_FOX_SCALE = FOX_HEAD_DIM ** -0.5


HEAD_PAIRS = FOX_HEADS // 2
PAIR_W = 2 * FOX_HEAD_DIM


def _low_half(shape):
    return lax.broadcasted_iota(jnp.int32, shape, 1) < FOX_HEAD_DIM


def _fox_attn_fwd(qkv, cum, cum_t, name):
    s = qkv.shape[0]
    t = min(ATT_TILE, s)
    wide = min(ATT_WIDE, s)

    def body(q_ref, k_ref, v_ref, cum_ref, cumt_ref, o_ref, ob_ref, lse_ref):
        i = pl.program_id(0)
        hp = pl.program_id(1)
        lo = _low_half((t, PAIR_W))
        qv = q_ref[...]
        zero = jnp.zeros_like(qv)
        q2 = (jnp.where(lo, qv, zero), jnp.where(lo, zero, qv))
        cum_v = cum_ref[...]
        cq2 = (_pick_head(cum_v, 2 * hp), _pick_head(cum_v, 2 * hp + 1))

        def step(j, carry, masked):
            ks = pl.ds(pl.multiple_of(j * wide, wide), wide)
            kj = k_ref[ks, :]
            vj = v_ref[ks, :]
            out = []
            for e in range(2):
                m, l, acc = carry[e]
                sc = lax.dot_general(q2[e], kj, _NT, preferred_element_type=F32) * _FOX_SCALE
                sc = sc + cq2[e] - cumt_ref[e:e + 1, ks]
                if masked:
                    q_pos = i * t + lax.broadcasted_iota(jnp.int32, (t, wide), 0)
                    k_pos = j * wide + lax.broadcasted_iota(jnp.int32, (t, wide), 1)
                    sc = jnp.where(k_pos <= q_pos, sc, -jnp.inf)
                m_new = jnp.maximum(m, jnp.max(sc, axis=1, keepdims=True))
                alpha = jnp.exp(m - m_new)
                p = jnp.exp(sc - m_new)
                l = alpha * l + jnp.sum(p, axis=1, keepdims=True)
                acc = alpha * acc + jnp.dot(p.astype(BF16), vj, preferred_element_type=F32)
                out.append((m_new, l, acc))
            return tuple(out)

        one = (jnp.full((t, 1), -jnp.inf, F32), jnp.zeros((t, 1), F32), jnp.zeros((t, PAIR_W), F32))
        whole = (i * t) // wide
        carry = lax.fori_loop(0, whole, lambda j, c: step(j, c, False), (one, one))
        (m0, l0, a0), (m1, l1, a1) = step(whole, carry, True)
        o = jnp.where(lo, a0 / l0, a1 / l1)
        o_ref[...] = o
        ob_ref[...] = o.astype(BF16)
        _put_head(lse_ref, m0 + jnp.log(l0), 2 * hp)
        _put_head(lse_ref, m1 + jnp.log(l1), 2 * hp + 1)

    nat_tile = pl.BlockSpec((t, FOX_HEADS), lambda i, hp: (i, 0))
    out_tile = pl.BlockSpec((t, PAIR_W), lambda i, hp: (i, hp))
    return pl.pallas_call(
        body, name=name, grid=(s // t, HEAD_PAIRS),
        in_specs=[pl.BlockSpec((t, PAIR_W), lambda i, hp: (i, hp)),
                  pl.BlockSpec((s, PAIR_W), lambda i, hp: (0, HEAD_PAIRS + hp)),
                  pl.BlockSpec((s, PAIR_W), lambda i, hp: (0, 2 * HEAD_PAIRS + hp)),
                  nat_tile, pl.BlockSpec((None, 2, s), lambda i, hp: (hp, 0, 0))],
        out_specs=[out_tile, out_tile, nat_tile],
        out_shape=[jax.ShapeDtypeStruct((s, D_MODEL), F32), jax.ShapeDtypeStruct((s, D_MODEL), BF16),
                   jax.ShapeDtypeStruct((s, FOX_HEADS), F32)],
        compiler_params=_cparams(("arbitrary", "arbitrary")),
    )(qkv, qkv, qkv, cum, cum_t)


def _fox_delta(do, o, expand, name):
    def body(do_ref, o_ref, e_ref, d_ref):
        prod = do_ref[...] * o_ref[...]
        hi = prod.astype(BF16)
        lo = (prod - hi.astype(F32)).astype(BF16)
        tot = (jnp.dot(hi, e_ref[...], preferred_element_type=F32)
               + jnp.dot(lo, e_ref[...], preferred_element_type=F32))
        d_ref[...] = tot[:, :FOX_HEADS]
    return _row_call(name, body, [do, o], [expand], [(FOX_HEADS, F32)], [])[0]


def _fox_attn_bwd(qkv, do, cum, cum_t, lse_t, delta_t, name):
    s = qkv.shape[0]
    t = min(ATT_TILE, s)
    wide = min(ATT_WIDE, s)
    nq = s // t
    tn_dims = (((0,), (0,)), ((), ()))

    def body(q_ref, k_ref, v_ref, do_ref, cum_ref, cumt_ref, lset_ref, deltat_ref,
             dq_ref, dk_ref, dv_ref, dck_ref, dcq_ref):
        hp = pl.program_id(0)
        j = pl.program_id(1)

        @pl.when(j == 0)
        def _():
            dq_ref[...] = jnp.zeros_like(dq_ref)
            dcq_ref[...] = jnp.zeros_like(dcq_ref)
        dk_ref[...] = jnp.zeros_like(dk_ref)
        dv_ref[...] = jnp.zeros_like(dv_ref)

        lo = _low_half((t, PAIR_W))
        lane = lax.broadcasted_iota(jnp.int32, (t, PAIR_W), 1)
        kv = k_ref[...]
        vv = v_ref[...]
        zero = jnp.zeros_like(kv)
        k2 = (jnp.where(lo, kv, zero), jnp.where(lo, zero, kv))
        v2 = (jnp.where(lo, vv, zero), jnp.where(lo, zero, vv))
        cum_v = cum_ref[...]
        ck2 = (_pick_head(cum_v, 2 * hp), _pick_head(cum_v, 2 * hp + 1))

        def step(i, dck, masked):
            qs = pl.ds(pl.multiple_of(i * wide, wide), wide)
            qi = q_ref[qs, :]
            do_i = do_ref[qs, :].astype(BF16)
            dv_p, dk_p, dq_p = [], [], []
            for e in range(2):
                st = lax.dot_general(k2[e], qi, _NT, preferred_element_type=F32) * _FOX_SCALE
                st = st + cumt_ref[e:e + 1, qs] - ck2[e]
                if masked:
                    k_pos = j * t + lax.broadcasted_iota(jnp.int32, (t, wide), 0)
                    q_pos = i * wide + lax.broadcasted_iota(jnp.int32, (t, wide), 1)
                    st = jnp.where(k_pos <= q_pos, st, -jnp.inf)
                pt = jnp.exp(st - lset_ref[e:e + 1, qs])
                dv_p.append(jnp.dot(pt.astype(BF16), do_i, preferred_element_type=F32))
                dpt = lax.dot_general(v2[e], do_i, _NT, preferred_element_type=F32)
                dst = pt * (dpt - deltat_ref[e:e + 1, qs])
                dsb = dst.astype(BF16)
                dk_p.append(jnp.dot(dsb, qi, preferred_element_type=F32))
                dq_p.append(lax.dot_general(dsb, kv, tn_dims, preferred_element_type=F32))
                dck = dck - jnp.where(lane == e, jnp.sum(dst, axis=1, keepdims=True), 0.0)
                dcq_ref[e:e + 1, qs] += jnp.sum(dst, axis=0, keepdims=True)
            dv_ref[...] += jnp.where(lo, dv_p[0], dv_p[1])
            dk_ref[...] += jnp.where(lo, dk_p[0], dk_p[1])
            dq_ref[qs, :] += jnp.where(_low_half((wide, PAIR_W)), dq_p[0], dq_p[1]) * _FOX_SCALE
            return dck

        first = (j * t) // wide
        dck = step(first, jnp.zeros((t, PAIR_W), F32), True)
        dck = lax.fori_loop(first + 1, s // wide, lambda i, c: step(i, c, False), dck)
        dk_ref[...] = dk_ref[...] * _FOX_SCALE
        dck_ref[...] = dck

    pair_full = lambda part: pl.BlockSpec((s, PAIR_W), lambda hp, j: (0, part * HEAD_PAIRS + hp))
    pair_tile = lambda part: pl.BlockSpec((t, PAIR_W), lambda hp, j: (j, part * HEAD_PAIRS + hp))
    rows = pl.BlockSpec((None, 2, s), lambda hp, j: (hp, 0, 0))
    return pl.pallas_call(
        body, name=name, grid=(HEAD_PAIRS, nq),
        in_specs=[pair_full(0), pair_tile(1), pair_tile(2), pair_full(0),
                  pl.BlockSpec((t, FOX_HEADS), lambda hp, j: (j, 0)), rows, rows, rows],
        out_specs=[pair_full(0), pair_tile(0), pair_tile(0),
                   pl.BlockSpec((None, t, PAIR_W), lambda hp, j: (hp, j, 0)), rows],
        out_shape=[jax.ShapeDtypeStruct((s, D_MODEL), F32)] * 3
        + [jax.ShapeDtypeStruct((HEAD_PAIRS, s, PAIR_W), F32), jax.ShapeDtypeStruct((HEAD_PAIRS, 2, s), F32)],
        compiler_params=_cparams(("arbitrary", "arbitrary")),
    )(qkv, qkv, qkv, do, cum, cum_t, lse_t, delta_t)


def _shift_down(v, d):
    row = lax.broadcasted_iota(jnp.int32, v.shape, 0)
    return jnp.where(row >= d, pltpu.roll(v, d, 0), 0.0)


def _shift_up(v, d):
    s = v.shape[0]
    row = lax.broadcasted_iota(jnp.int32, v.shape, 0)
    return jnp.where(row < s - d, pltpu.roll(v, s - d, 0), 0.0)


def _conv_taps(v, cw_ref, width):
    out = cw_ref[width - 1:width, :] * v
    for k in range(width - 1):
        out = out + cw_ref[k:k + 1, :] * _shift_down(v, width - 1 - k)
    return out


def _conv_taps_bwd(dout, v, cw_ref, dcw_ref, width):
    dv = cw_ref[width - 1:width, :] * dout
    dcw_ref[width - 1:width, :] = _rows_sum(dout * v)
    for k in range(width - 1):
        d = width - 1 - k
        dv = dv + cw_ref[k:k + 1, :] * _shift_up(dout, d)
        dcw_ref[k:k + 1, :] = _rows_sum(dout * _shift_down(v, d))
    return dv


def _col_spec(s, tc, part=0):
    off = part * (D_MODEL // tc)
    return pl.BlockSpec((s, tc), lambda c: (0, c + off))


def _small_spec(rows, tc):
    return pl.BlockSpec((rows, tc), lambda c: (0, c))


def _col_call(name, body, in_arrays, in_specs, out_rows, s, tc):
    return pl.pallas_call(
        body, name=name, grid=(D_MODEL // tc,), in_specs=in_specs,
        out_specs=[pl.BlockSpec((r, tc), lambda c: (0, c)) for r, _ in out_rows],
        out_shape=[jax.ShapeDtypeStruct((r, D_MODEL), dt) for r, dt in out_rows],
        compiler_params=_cparams(("arbitrary",)),
    )(*in_arrays)


def _sconv_fwd(proj, conv_w, name):
    s = proj.shape[0]
    tc = COL_TILE

    def body(b_ref, c_ref, x_ref, cw_ref, y_ref):
        y_ref[...] = (b_ref[...] * _conv_taps(c_ref[...] * x_ref[...], cw_ref, 3)).astype(BF16)

    return _col_call(name, body, [proj, proj, proj, conv_w],
                     [_col_spec(s, tc, 0), _col_spec(s, tc, 1), _col_spec(s, tc, 2), _small_spec(3, tc)],
                     [(s, BF16)], s, tc)[0]


def _sconv_bwd(dy, proj, conv_w, name):
    s = proj.shape[0]
    tc = COL_TILE

    def body(dy_ref, b_ref, c_ref, x_ref, cw_ref, db_ref, dc_ref, dx_ref, dcw_ref):
        w = c_ref[...] * x_ref[...]
        dy_v = dy_ref[...]
        db_ref[...] = (dy_v * _conv_taps(w, cw_ref, 3)).astype(BF16)
        dw = _conv_taps_bwd(dy_v * b_ref[...], w, cw_ref, dcw_ref, 3)
        dc_ref[...] = (dw * x_ref[...]).astype(BF16)
        dx_ref[...] = (dw * c_ref[...]).astype(BF16)

    return _col_call(name, body, [dy, proj, proj, proj, conv_w],
                     [_col_spec(s, tc), _col_spec(s, tc, 0), _col_spec(s, tc, 1), _col_spec(s, tc, 2),
                      _small_spec(3, tc)],
                     [(s, BF16), (s, BF16), (s, BF16), (3, F32)], s, tc)


def _lru_conv(proj, conv_w, conv_b, name):
    s = proj.shape[0]
    tc = COL_TILE

    def body(x_ref, cw_ref, cb_ref, xb_ref, xbb_ref):
        xb = _conv_taps(x_ref[...], cw_ref, 4) + cb_ref[...]
        xb_ref[...] = xb
        xbb_ref[...] = xb.astype(BF16)

    return _col_call(name, body, [proj, conv_w, conv_b],
                     [_col_spec(s, tc, 1), _small_spec(4, tc), _small_spec(1, tc)],
                     [(s, F32), (s, BF16)], s, tc)


def _lru_conv_bwd(dxb1, dxb2, proj, conv_w, name):
    s = proj.shape[0]
    tc = COL_TILE

    def body(d1_ref, d2_ref, x_ref, cw_ref, dx_ref, dcw_ref, dcb_ref):
        dxb = d1_ref[...] + d2_ref[...]
        dcb_ref[...] = _rows_sum(dxb)
        dx_ref[...] = _conv_taps_bwd(dxb, x_ref[...], cw_ref, dcw_ref, 4).astype(BF16)

    return _col_call(name, body, [dxb1, dxb2, proj, conv_w],
                     [_col_spec(s, tc), _col_spec(s, tc), _col_spec(s, tc, 1), _small_spec(4, tc)],
                     [(s, BF16), (4, F32), (1, F32)], s, tc)


_GELU_C = math.sqrt(2.0 / math.pi)


def _gelu_parts(g):
    inner = _GELU_C * (g + 0.044715 * g * g * g)
    th = jnp.tanh(inner)
    val = 0.5 * g * (1.0 + th)
    der = 0.5 * (1.0 + th) + 0.5 * g * (1.0 - th * th) * (_GELU_C * (1.0 + 3.0 * 0.044715 * g * g))
    return val, der


def _lru_gates(pa_ref, px_ref, ba_ref, bx_ref, lam_ref):
    r = _sigmoid(pa_ref[...] + ba_ref[...])
    ig = _sigmoid(px_ref[...] + bx_ref[...])
    sp = _softplus(-lam_ref[...])
    log_a = (-LRU_C) * r * sp
    a = jnp.exp(log_a)
    z = 2.0 * log_a
    one_m_a2 = jnp.where(z > -1e-3, -(z * (1.0 + z * (0.5 + z * (1.0 / 6.0)))), 1.0 - jnp.exp(z))
    return r, ig, sp, a, jnp.sqrt(one_m_a2)


def _lru_scan(pre, xb, proj, b_a, b_x, lam, name):
    s = xb.shape[0]
    tc = COL_TILE

    def body(pa_ref, px_ref, xb_ref, g_ref, ba_ref, bx_ref, lam_ref, y_ref, hs_ref):
        _, ig, _, a, mult = _lru_gates(pa_ref, px_ref, ba_ref, bx_ref, lam_ref)
        b = mult * (ig * xb_ref[...])
        d = 1
        while d < s:
            row = lax.broadcasted_iota(jnp.int32, a.shape, 0)
            keep = row >= d
            b = b + a * jnp.where(keep, pltpu.roll(b, d, 0), 0.0)
            a = a * jnp.where(keep, pltpu.roll(a, d, 0), 1.0)
            d *= 2
        hs_ref[...] = b
        y_ref[...] = (b * _gelu_parts(g_ref[...])[0]).astype(BF16)

    return _col_call(name, body, [pre, pre, xb, proj, b_a, b_x, lam],
                     [_col_spec(s, tc, 0), _col_spec(s, tc, 1), _col_spec(s, tc), _col_spec(s, tc, 0),
                      _small_spec(1, tc), _small_spec(1, tc), _small_spec(1, tc)],
                     [(s, BF16), (s, F32)], s, tc)


def _lru_scan_bwd(dy, pre, xb, proj, hs, b_a, b_x, lam, name):
    s = xb.shape[0]
    tc = COL_TILE

    def body(dy_ref, pa_ref, px_ref, xb_ref, g_ref, hs_ref, ba_ref, bx_ref, lam_ref,
             dg_ref, dpa_ref, dpx_ref, dxb_ref, dba_ref, dbx_ref, dlam_ref):
        r, ig, sp, a, mult = _lru_gates(pa_ref, px_ref, ba_ref, bx_ref, lam_ref)
        gl, gl_der = _gelu_parts(g_ref[...])
        dy_v = dy_ref[...]
        hs_v = hs_ref[...]
        dg_ref[...] = (dy_v * hs_v * gl_der).astype(BF16)
        lam_t = dy_v * gl
        coef = _shift_up(a, 1)
        d = 1
        while d < s:
            row = lax.broadcasted_iota(jnp.int32, coef.shape, 0)
            keep = row < s - d
            lam_t = lam_t + coef * jnp.where(keep, pltpu.roll(lam_t, s - d, 0), 0.0)
            coef = coef * jnp.where(keep, pltpu.roll(coef, s - d, 0), 1.0)
            d *= 2
        xb_v = xb_ref[...]
        da = lam_t * _shift_down(hs_v, 1)
        dmult = lam_t * (ig * xb_v)
        dig = lam_t * mult * xb_v
        dxb_ref[...] = lam_t * mult * ig
        dlog_a = da * a - dmult * (a * a) / mult
        dr = dlog_a * ((-LRU_C) * sp)
        dsp = _rows_sum(dlog_a * ((-LRU_C) * r))
        dlam_ref[...] = -dsp * _sigmoid(-lam_ref[...])
        dpa = dr * r * (1.0 - r)
        dpx = dig * ig * (1.0 - ig)
        dba_ref[...] = _rows_sum(dpa)
        dbx_ref[...] = _rows_sum(dpx)
        dpa_ref[...] = dpa.astype(BF16)
        dpx_ref[...] = dpx.astype(BF16)

    return _col_call(name, body, [dy, pre, pre, xb, proj, hs, b_a, b_x, lam],
                     [_col_spec(s, tc), _col_spec(s, tc, 0), _col_spec(s, tc, 1), _col_spec(s, tc),
                      _col_spec(s, tc, 0), _col_spec(s, tc),
                      _small_spec(1, tc), _small_spec(1, tc), _small_spec(1, tc)],
                     [(s, BF16), (s, BF16), (s, BF16), (s, F32), (1, F32), (1, F32), (1, F32)], s, tc)


def _ffn_fwd(x, w_in, w_out, g_pre, g_post, shift, scale, gate, tag, after=None):
    h = _pre_norm(x, g_pre, scale, shift, tag + "_pre", after=after)
    g, u, a = _ffn_in_act(h, w_in, tag + "_in")
    y = _mm_nn(a, w_out, tag + "_out", tn=512)
    xo = _post_norm(x, y, g_post, gate, 0.5, tag + "_post")
    return xo, (x, h, g, u, a, y)


def _ffn_bwd(dxo, saved, w_in, w_out, g_pre, g_post, scale, gate, tag, after=None):
    x, h, g, u, a, y = saved
    dy, dgate, dg_post = _post_norm_bwd(dxo, y, g_post, gate, 0.5, tag + "_post_b", after=after)
    dw_out = _mm_tn(a, dy, tag + "_out_bw", tn=512)
    dgu = _ffn_out_bx_act(dy, w_out, g, u, tag + "_out_bx")
    dh, dw_in = _ffn_in_bwd(dgu, h, w_in, tag + "_in_b")
    dx, dshift, dscale, dg_pre = _pre_norm_bwd(dxo, dh, x, g_pre, scale, tag + "_pre_b")
    return dx, dw_in, dw_out, (dshift, dscale, dgate), dg_pre, dg_post


def _pair_rows(v):
    return v.T.reshape(HEAD_PAIRS, 2, v.shape[0])


def _fox_fwd(h, p, tag):
    s = h.shape[0]
    proj = _mm_nn(h, p["w_in"], tag + "_in", tn=640)
    qkv = proj[:, :3 * D_MODEL].astype(BF16)
    flt = proj[:, 3 * D_MODEL:3 * D_MODEL + FOX_HEADS].T
    cum_t = _fox_gate(flt, p["b_f"], tag + "_gate")
    cum = cum_t.T
    cum_t2 = cum_t.reshape(HEAD_PAIRS, 2, s)
    o, ob, lse = _fox_attn_fwd(qkv, cum, cum_t2, tag + "_attn")
    y = _mm_nn(ob, p["w_out"], tag + "_out")
    return y, (qkv, flt, cum, cum_t2, o, ob, lse)


def _fox_bwd(dy, h, saved, p, tag):
    qkv, flt, cum, cum_t2, o, ob, lse = saved
    s = h.shape[0]
    do = _mm_nt(dy, p["w_out"], tag + "_out_bx")
    dw_out = _mm_tn(ob, dy, tag + "_out_bw")
    expand = jnp.pad(jnp.repeat(jnp.eye(FOX_HEADS, dtype=BF16), FOX_HEAD_DIM, axis=0),
                     ((0, 0), (0, PAIR_W - FOX_HEADS)))
    delta = _fox_delta(do, o, expand, tag + "_attn_delta")
    dq, dk, dv, dck, dcq = _fox_attn_bwd(qkv, do, cum, cum_t2, _pair_rows(lse), _pair_rows(delta), tag + "_attn_b")
    dcum_k = dck[:, :, :2].transpose(0, 2, 1).reshape(FOX_HEADS, s)
    dflt, db_f = _fox_gate_bwd(dcq.reshape(FOX_HEADS, s), dcum_k, flt, p["b_f"], tag + "_gate_b")
    dproj = jnp.concatenate(
        [dq, dk, dv, dflt.T, jnp.zeros((s, FOX_PAD - 3 * D_MODEL - FOX_HEADS), F32)], axis=1).astype(BF16)
    dh = _mm_nt(dproj, p["w_in"], tag + "_in_bx", tn=640)
    dw_in = _mm_tn(h, dproj, tag + "_in_bw", tn=640)
    return dh, {"w_in": dw_in, "w_out": dw_out, "b_f": db_f}


def _sconv_mix_fwd(h, p, tag):
    proj = _mm_nn(h, p["w_in"], tag + "_in")
    yb = _sconv_fwd(proj, p["conv_w"], tag + "_conv")
    y = _mm_nn(yb, p["w_out"], tag + "_out")
    return y, (proj, yb)


def _sconv_mix_bwd(dy, h, saved, p, tag):
    proj, yb = saved
    dyb = _mm_nt(dy, p["w_out"], tag + "_out_bx")
    dw_out = _mm_tn(yb, dy, tag + "_out_bw")
    db, dc, dxv, dcw = _sconv_bwd(dyb, proj, p["conv_w"], tag + "_conv_b")
    dproj = jnp.concatenate([db, dc, dxv], axis=1)
    dh = _mm_nt(dproj, p["w_in"], tag + "_in_bx")
    dw_in = _mm_tn(h, dproj, tag + "_in_bw", tn=p["w_in"].arr.shape[-1], blocked_out=True)
    return dh, {"w_in": dw_in, "w_out": dw_out, "conv_w": dcw}


def _lru_mix_fwd(h, p, tag):
    proj = _mm_nn(h, p["w_in"], tag + "_in")
    xb, xbb = _lru_conv(proj, p["conv_w"], p["conv_b"], tag + "_conv")
    pre = _mm_nn(xbb, p["w_ax"], tag + "_gates", tn=D_MODEL)
    yb, hs = _lru_scan(pre, xb, proj, p["b_a"], p["b_x"], p["lam"], tag + "_scan")
    y = _mm_nn(yb, p["w_out"], tag + "_out")
    return y, (proj, xb, xbb, pre, yb, hs)


def _diag_blocks(m):
    return jnp.stack([m[LRU_BLOCK_DIM * n:LRU_BLOCK_DIM * (n + 1), LRU_BLOCK_DIM * n:LRU_BLOCK_DIM * (n + 1)]
                      for n in range(LRU_BLOCKS)])


def _lru_mix_bwd(dy, h, saved, p, tag):
    proj, xb, xbb, pre, yb, hs = saved
    dyb = _mm_nt(dy, p["w_out"], tag + "_out_bx")
    dw_out = _mm_tn(yb, dy, tag + "_out_bw")
    dg, dpa, dpx, dxb1, dba, dbx, dlam = _lru_scan_bwd(dyb, pre, xb, proj, hs, p["b_a"], p["b_x"], p["lam"],
                                                       tag + "_scan_b")
    dpre = jnp.concatenate([dpa, dpx], axis=1)
    dxb2 = _mm_nt(dpre, p["w_ax"], tag + "_gates_bx", tn=D_MODEL)
    dw_ax = _mm_tn(xbb, dpre, tag + "_gates_bw", tn=D_MODEL)
    dx0, dcw, dcb = _lru_conv_bwd(dxb1, dxb2, proj, p["conv_w"], tag + "_conv_b")
    dproj = jnp.concatenate([dg, dx0], axis=1)
    dh = _mm_nt(dproj, p["w_in"], tag + "_in_bx")
    dw_in = _mm_tn(h, dproj, tag + "_in_bw", tn=p["w_in"].arr.shape[-1], blocked_out=True)
    grads = {"w_in": dw_in, "w_out": dw_out, "conv_w": dcw, "conv_b": dcb,
             "w_a": _diag_blocks(dw_ax[:, :D_MODEL]), "w_x": _diag_blocks(dw_ax[:, D_MODEL:]),
             "b_a": dba, "b_x": dbx, "lam": dlam}
    return dh, grads


_MIXERS = ((_fox_fwd, _fox_bwd), (_sconv_mix_fwd, _sconv_mix_bwd), (_lru_mix_fwd, _lru_mix_bwd))


def _local_step(x, target, mod, layer_params, on_grads=None, on_mid=None, first_after=None):
    layers = []
    tape = []
    for i in range(DEPTH):
        lp = dict(layer_params(i, 0, x))
        layers.append(lp)
        row = lambda v: v[None, :]
        m = lambda sub, what: mod[i, sub, what][None, :]
        x, sv0 = _ffn_fwd(x, lp["ffn_in"][0], lp["ffn_out"][0], row(lp["norm_pre"][0]), row(lp["norm_post"][0]),
                          m(0, 0), m(0, 1), m(0, 2), f"l{i}_ffn0", after=first_after if i == 0 else None)
        lp.update(layer_params(i, 1, x))
        h = _pre_norm(x, row(lp["norm_pre"][1]), m(1, 1), m(1, 0), f"l{i}_mix_pre")
        y, svm = _MIXERS[i % 3][0](h, lp["mixer"], f"l{i}_mix")
        x1 = _post_norm(x, y, row(lp["norm_post"][1]), m(1, 2), 1.0, f"l{i}_mix_post")
        second = layer_params(i, 2, x1)
        lp["ffn_in"] = lp["ffn_in"] + second["ffn_in"]
        lp["ffn_out"] = lp["ffn_out"] + second["ffn_out"]
        x2, sv2 = _ffn_fwd(x1, lp["ffn_in"][1], lp["ffn_out"][1], row(lp["norm_pre"][2]), row(lp["norm_post"][2]),
                           m(2, 0), m(2, 1), m(2, 2), f"l{i}_ffn1")
        tape.append((sv0, (x, h, y, svm), sv2))
        x = x2
    dx, loss_row = _loss_head(x, target, "loss_head")

    layer_grads = [None] * DEPTH
    dmod = [None] * DEPTH
    after = None
    for i in reversed(range(DEPTH)):
        lp = layers[i]
        row = lambda v: v[None, :]
        m = lambda sub, what: mod[i, sub, what][None, :]
        sv0, (xm, h, y, svm), sv2 = tape[i]
        dx, dw_in1, dw_out1, dm2, dgp2, dgq2 = _ffn_bwd(dx, sv2, lp["ffn_in"][1], lp["ffn_out"][1],
                                                        row(lp["norm_pre"][2]), row(lp["norm_post"][2]),
                                                        m(2, 1), m(2, 2), f"l{i}_ffn1", after=after)
        after = on_mid(i, dx) if on_mid is not None else None
        dy, dgate1, dgq1 = _post_norm_bwd(dx, y, row(lp["norm_post"][1]), m(1, 2), 1.0, f"l{i}_mix_post_b", after=after)
        dh, mg = _MIXERS[i % 3][1](dy, h, svm, lp["mixer"], f"l{i}_mix")
        dx, dshift1, dscale1, dgp1 = _pre_norm_bwd(dx, dh, xm, row(lp["norm_pre"][1]), m(1, 1), f"l{i}_mix_pre_b")
        dx, dw_in0, dw_out0, dm0, dgp0, dgq0 = _ffn_bwd(dx, sv0, lp["ffn_in"][0], lp["ffn_out"][0],
                                                        row(lp["norm_pre"][0]), row(lp["norm_post"][0]),
                                                        m(0, 1), m(0, 2), f"l{i}_ffn0")
        dmod[i] = jnp.concatenate([*dm0, dshift1, dscale1, dgate1, *dm2], axis=0).reshape(N_SUB, 3, D_MODEL)
        layer_grads[i] = {"ffn_in": (dw_in0, dw_in1), "ffn_out": (dw_out0, dw_out1),
                          "norm_pre": jnp.concatenate([dgp0, dgp1, dgp2], axis=0),
                          "norm_post": jnp.concatenate([dgq0, dgq1, dgq2], axis=0), "mixer": mg}
        if on_grads is not None:
            after = on_grads(i, layer_grads[i], dx)
    return loss_row, dx, jnp.stack(dmod), layer_grads


COND_ROWS = 16
COND_PAD = 128


def _cond_fwd(c_pad, w_cond, b_shard, name):
    nl, d, n = w_cond.shape
    tn = 768

    def body(c_ref, w_ref, b_ref, o_ref):
        cv = c_ref[...]
        act = (cv * _sigmoid(cv)).astype(BF16)
        o_ref[...] = jnp.dot(act, w_ref[...].astype(BF16), preferred_element_type=F32) + b_ref[...]

    return pl.pallas_call(
        body, name=name, grid=(nl, n // tn),
        in_specs=[pl.BlockSpec((COND_ROWS, d), lambda i, j: (0, 0)),
                  pl.BlockSpec((None, d, tn), lambda i, j: (i, 0, j)),
                  pl.BlockSpec((None, 1, tn), lambda i, j: (i, 0, j))],
        out_specs=pl.BlockSpec((None, COND_ROWS, tn), lambda i, j: (i, 0, j)),
        out_shape=jax.ShapeDtypeStruct((nl, COND_ROWS, n), F32),
        compiler_params=_cparams(("arbitrary", "arbitrary")),
    )(c_pad, w_cond, b_shard)


def _adam_math(w, g, m, v):
    nm = ADAM_B1 * m + (1.0 - ADAM_B1) * g
    nv = ADAM_B2 * v + (1.0 - ADAM_B2) * (g * g)
    m_hat = nm / (1.0 - ADAM_B1 ** ADAM_STEP)
    v_hat = nv / (1.0 - ADAM_B2 ** ADAM_STEP)
    delta = (-ADAM_LR) * (m_hat / (jnp.sqrt(v_hat) + ADAM_EPS) + ADAM_WD * w)
    return delta, nm, nv


def _cond_bwd_adamw(c_t, dmod_s, w, m, v, name):
    nl, d, n = w.shape
    tn = 384
    blk = pl.BlockSpec((None, d, tn), lambda i, j: (i, 0, j))

    def body(c_ref, dm_ref, w_ref, m_ref, v_ref, g_ref, d_ref, nm_ref, nv_ref):
        cv = c_ref[...]
        g = jnp.dot((cv * _sigmoid(cv)).astype(BF16), dm_ref[...], preferred_element_type=F32)
        g_ref[...] = g
        d_ref[...], nm_ref[...], nv_ref[...] = _adam_math(w_ref[...], g, m_ref[...], v_ref[...])

    return pl.pallas_call(
        body, name=name, grid=(nl, n // tn),
        in_specs=[pl.BlockSpec((d, COND_PAD), lambda i, j: (0, 0)),
                  pl.BlockSpec((None, COND_PAD, tn), lambda i, j: (i, 0, j)), blk, blk, blk],
        out_specs=[blk] * 4, out_shape=[jax.ShapeDtypeStruct(w.shape, F32)] * 4,
        compiler_params=_cparams(("arbitrary", "arbitrary")),
    )(c_t, dmod_s, w, m, v)


def _adamw(w, g, m, v, name):
    rows, cols = w.shape
    tr = next(t for t in (256, 176, 128, 64, 32, 16, 8) if rows % t == 0)
    blk = pl.BlockSpec((tr, cols), lambda i: (i, 0))

    def body(w_ref, g_ref, m_ref, v_ref, d_ref, nm_ref, nv_ref):
        d_ref[...], nm_ref[...], nv_ref[...] = _adam_math(w_ref[...], g_ref[...], m_ref[...], v_ref[...])

    return pl.pallas_call(
        body, name=name, grid=(rows // tr,), in_specs=[blk] * 4, out_specs=[blk] * 3,
        out_shape=[jax.ShapeDtypeStruct(w.shape, F32)] * 3, compiler_params=_cparams(("arbitrary",)),
    )(w, g, m, v)


def _adamw_rows(w, g, m, v, outs, row0, nrows, name):
    cols = w.shape[1]
    tr = next(t for t in (512, 256, 128, 64, 32, 16, 8) if nrows % t == 0 and row0 % t == 0)
    blk = pl.BlockSpec((tr, cols), lambda i: (i + row0 // tr, 0))
    anywhere = pl.BlockSpec(memory_space=pl.ANY)

    def body(w_ref, g_ref, m_ref, v_ref, d_in, nm_in, nv_in, d_ref, nm_ref, nv_ref, g_out):
        d_ref[...], nm_ref[...], nv_ref[...] = _adam_math(w_ref[...], g_ref[...], m_ref[...], v_ref[...])

    return pl.pallas_call(
        body, name=name, grid=(nrows // tr,), in_specs=[blk] * 4 + [anywhere] * 3,
        out_specs=[blk] * 3 + [anywhere], out_shape=[jax.ShapeDtypeStruct(w.shape, F32)] * 4,
        input_output_aliases={4: 0, 5: 1, 6: 2, 1: 3}, compiler_params=_cparams(("arbitrary",)),
    )(w, g, m, v, *outs)


_MESH = pl.DeviceIdType.MESH
_ANY = pl.BlockSpec(memory_space=pl.ANY)


def _place():
    return lax.axis_index("x"), lax.axis_index("y"), lax.axis_index("c")


def _other_chips(x, y):
    return [(1 - x, y), (x, 1 - y), (1 - x, 1 - y)]


def _allgather8(block, name):
    m_per, n = block.shape

    def body(x_ref, out_ref, send_sems, recv_sems, local_sem):
        x, y, c = _place()
        me, sibling = (x, y, c), (x, y, 1 - c)
        chips = _other_chips(x, y)

        def rows(px, py, pc):
            return out_ref.at[pl.ds((4 * px + 2 * py + pc) * m_per, m_per), :]

        def copy(k, blk, to, src=None):
            return pltpu.make_async_remote_copy(
                src_ref=rows(*blk) if src is None else src, dst_ref=rows(*blk),
                send_sem=send_sems.at[k], recv_sem=recv_sems.at[k], device_id=to, device_id_type=_MESH)

        mine = pltpu.make_async_copy(x_ref, rows(*me), local_sem)
        mine.start()
        first = [copy(0, me, sibling, src=x_ref)]
        first += [copy(1 + j, me, (*chip, c), src=x_ref) for j, chip in enumerate(chips)]
        for cp in first:
            cp.start()
        passed = [copy(4 + j, (*chip, c), sibling) for j, chip in enumerate(chips)]
        for j, chip in enumerate(chips):
            copy(1 + j, (*chip, c), me).wait_recv()
            passed[j].start()
        copy(0, sibling, me).wait_recv()
        for j, chip in enumerate(chips):
            copy(4 + j, (*chip, 1 - c), me).wait_recv()
        for cp in first + passed:
            cp.wait_send()
        mine.wait()

    return pl.pallas_call(
        body, name=name, out_shape=jax.ShapeDtypeStruct((N_DEV * m_per, n), block.dtype),
        in_specs=[pl.BlockSpec(memory_space=pltpu.VMEM)], out_specs=pl.BlockSpec(memory_space=pltpu.VMEM),
        scratch_shapes=[pltpu.SemaphoreType.DMA((7,)), pltpu.SemaphoreType.DMA((7,)), pltpu.SemaphoreType.DMA],
        compiler_params=_cparams(),
    )(block)


def _split_axis(shape):
    return next(a for a, n in enumerate(shape) if n > 1)


_HBM = pl.BlockSpec(memory_space=pltpu.HBM)
_SEM = pl.BlockSpec(memory_space=pltpu.SEMAPHORE)
_SPLIT_COPY = pltpu.CompilerParams(has_side_effects=pltpu.SideEffectType.DATAFLOW_SIDE_EFFECTING)
_TOKEN = jax.ShapeDtypeStruct((8, 128), F32)


def _in_hbm(arrays):
    return [pltpu.with_memory_space_constraint(a, pltpu.HBM) for a in arrays]


class _Gathered(NamedTuple):
    shard_shape: tuple
    chip_axis: int

    @property
    def shape(self):
        return self.shard_shape[:self.chip_axis] + (N_CHIPS,) + self.shard_shape[self.chip_axis:]

    def half(self, ref, chip, pc):
        cut = _split_axis(self.shard_shape)
        n = self.shard_shape[cut] // 2
        idx = [slice(None)] * len(self.shard_shape)
        idx[cut] = pl.ds(pc * n, n)
        idx.insert(self.chip_axis, chip)
        return ref.at[tuple(idx)]


def _own_block_placed(shard, layout, chip):
    return lax.dynamic_update_slice_in_dim(lax.empty(layout.shape, shard.dtype),
                                           jnp.expand_dims(shard, layout.chip_axis), chip, axis=layout.chip_axis)


def _gather_copies(lands, layouts, send_sems, recv_sems):
    x, y, c = _place()
    out = []
    for t, (land, lay) in enumerate(zip(lands, layouts)):
        for j, (px, py) in enumerate(_other_chips(x, y)):
            def copy(chip, t=t, j=j, px=px, py=py, land=land, lay=lay):
                return pltpu.make_async_remote_copy(
                    src_ref=lay.half(land, chip, c), dst_ref=lay.half(land, chip, c),
                    send_sem=send_sems.at[3 * t + j], recv_sem=recv_sems.at[3 * t + j],
                    device_id=(px, py, c), device_id_type=_MESH)
            out.append((copy(2 * x + y), copy(2 * px + py)))
    return out


def _gather_start(lands, layouts, after, name):
    nt = len(lands)
    order = [] if after is None else [after]

    def body(*refs):
        land_refs = refs[:nt]
        send_sems, recv_sems = refs[nt + len(order):nt + len(order) + 2]
        token = refs[-1]
        for send, _ in _gather_copies(land_refs, layouts, send_sems, recv_sems):
            send.start()
        token[...] = jnp.zeros_like(token)

    out = pl.pallas_call(
        body, name=name,
        out_shape=(pltpu.SemaphoreType.DMA((3 * nt,)), pltpu.SemaphoreType.DMA((3 * nt,)),
                   *[pltpu.HBM(a.shape, a.dtype) for a in lands], _TOKEN),
        in_specs=[_HBM] * nt + [_ANY] * len(order),
        out_specs=(_SEM, _SEM, *[_HBM] * nt, pl.BlockSpec(memory_space=pltpu.VMEM)),
        input_output_aliases={t: 2 + t for t in range(nt)}, compiler_params=_SPLIT_COPY,
    )(*_in_hbm(lands), *order)
    return out[0], out[1], list(out[2:2 + nt]), out[-1]


def _gather_wait(send_sems, recv_sems, lands, layouts, after, name):
    nt = len(lands)

    def body(*refs):
        land_refs = refs[:nt]
        sems = refs[nt:nt + 2]
        for send, arrival in _gather_copies(land_refs, layouts, *sems):
            send.wait_send()
            arrival.wait_recv()

    return list(pl.pallas_call(
        body, name=name, out_shape=tuple(pltpu.HBM(a.shape, a.dtype) for a in lands),
        in_specs=[_HBM] * nt + [_SEM, _SEM, _ANY], out_specs=tuple([_HBM] * nt),
        input_output_aliases={t: t for t in range(nt)}, compiler_params=_SPLIT_COPY,
    )(*lands, send_sems, recv_sems, after))


def _gather_forward(lands, layouts, name):
    nt = len(lands)

    def body(*refs):
        outs = refs[nt:2 * nt]
        send_sems, recv_sems = refs[2 * nt:]
        x, y, c = _place()
        sends, arrivals = [], []
        for t, lay in enumerate(layouts):
            for j, (px, py) in enumerate(_other_chips(x, y)):
                for pc, group in ((c, sends), (1 - c, arrivals)):
                    part = lay.half(outs[t], 2 * px + py, pc)
                    group.append(pltpu.make_async_remote_copy(
                        src_ref=part, dst_ref=part, send_sem=send_sems.at[3 * t + j], recv_sem=recv_sems.at[3 * t + j],
                        device_id=(x, y, 1 - c), device_id_type=_MESH))
        for cp in sends:
            cp.start()
        for cp in arrivals:
            cp.wait_recv()
        for cp in sends:
            cp.wait_send()

    return list(pl.pallas_call(
        body, name=name, out_shape=[jax.ShapeDtypeStruct(a.shape, a.dtype) for a in lands],
        in_specs=[_ANY] * nt, out_specs=[_ANY] * nt, input_output_aliases={t: t for t in range(nt)},
        scratch_shapes=[pltpu.SemaphoreType.DMA((3 * nt,)), pltpu.SemaphoreType.DMA((3 * nt,))],
        compiler_params=_cparams(),
    )(*lands))


def _pair_copies(grads, lands, send_sems, recv_sems):
    x, y, c = _place()
    out = []
    for t, (g, land) in enumerate(zip(grads, lands)):
        h = g.shape[1] // 2
        out.append(pltpu.make_async_remote_copy(
            src_ref=g.at[:, pl.ds((1 - c) * h, h), :], dst_ref=land, send_sem=send_sems.at[t],
            recv_sem=recv_sems.at[t], device_id=(x, y, 1 - c), device_id_type=_MESH))
    return out


def _pair_start(grads, after, name):
    nt = len(grads)
    lands = [lax.empty((N_CHIPS, g.shape[1] // 2, g.shape[2]), g.dtype) for g in grads]
    order = [] if after is None else [after]

    def body(*refs):
        send_sems, recv_sems = refs[2 * nt + len(order):2 * nt + len(order) + 2]
        token = refs[-1]
        for cp in _pair_copies(refs[:nt], refs[nt:2 * nt], send_sems, recv_sems):
            cp.start()
        token[...] = jnp.zeros_like(token)

    out = pl.pallas_call(
        body, name=name,
        out_shape=(pltpu.SemaphoreType.DMA((nt,)), pltpu.SemaphoreType.DMA((nt,)),
                   *[pltpu.HBM(a.shape, a.dtype) for a in grads + lands], _TOKEN),
        in_specs=[_HBM] * (2 * nt) + [_ANY] * len(order),
        out_specs=(_SEM, _SEM, *[_HBM] * (2 * nt), pl.BlockSpec(memory_space=pltpu.VMEM)),
        input_output_aliases={t: 2 + t for t in range(2 * nt)}, compiler_params=_SPLIT_COPY,
    )(*_in_hbm(grads + lands), *order)
    return out[0], out[1], list(out[2:2 + nt]), list(out[2 + nt:2 + 2 * nt]), out[-1]


def _pair_wait(send_sems, recv_sems, grads, lands, after, name):
    nt = len(grads)

    def body(*refs):
        for cp in _pair_copies(refs[:nt], refs[nt:2 * nt], *refs[2 * nt:2 * nt + 2]):
            cp.wait_send()
            cp.wait_recv()

    out = pl.pallas_call(
        body, name=name, out_shape=tuple(pltpu.HBM(a.shape, a.dtype) for a in grads + lands),
        in_specs=[_HBM] * (2 * nt) + [_SEM, _SEM, _ANY], out_specs=tuple([_HBM] * (2 * nt)),
        input_output_aliases={t: t for t in range(2 * nt)}, compiler_params=_SPLIT_COPY,
    )(*grads, *lands, send_sems, recv_sems, after)
    return list(out[:nt]), list(out[nt:])


def _pair_sum(own, recv, c_idx, name):
    _, h, cols = recv.shape

    def body(c_ref, own_ref, recv_ref, o_ref):
        o_ref[...] = (own_ref[...] + recv_ref[...]).astype(BF16)

    return pl.pallas_call(
        body, name=name,
        grid_spec=pltpu.PrefetchScalarGridSpec(
            num_scalar_prefetch=1, grid=(N_CHIPS,),
            in_specs=[pl.BlockSpec((None, h, cols), lambda k, c_ref: (k, c_ref[0], 0)),
                      pl.BlockSpec((None, h, cols), lambda k, c_ref: (k, 0, 0))],
            out_specs=pl.BlockSpec((None, h, cols), lambda k, c_ref: (k, 0, 0))),
        out_shape=jax.ShapeDtypeStruct(recv.shape, BF16), compiler_params=_cparams(("arbitrary",)),
    )(c_idx, own, recv)


def _chip_copies(parts, lands, send_sems, recv_sems):
    x, y, c = _place()
    out = []
    for t, (part, land) in enumerate(zip(parts, lands)):
        for j, (px, py) in enumerate(_other_chips(x, y)):
            out.append(pltpu.make_async_remote_copy(
                src_ref=part.at[2 * px + py], dst_ref=land.at[j], send_sem=send_sems.at[3 * t + j],
                recv_sem=recv_sems.at[3 * t + j], device_id=(px, py, c), device_id_type=_MESH))
    return out


def _chip_send_start(parts, after, name):
    nt = len(parts)
    lands = [lax.empty((N_CHIPS - 1,) + p.shape[1:], p.dtype) for p in parts]
    order = [] if after is None else [after]

    def body(*refs):
        send_sems, recv_sems = refs[2 * nt + len(order):2 * nt + len(order) + 2]
        token = refs[-1]
        for cp in _chip_copies(refs[:nt], refs[nt:2 * nt], send_sems, recv_sems):
            cp.start()
        token[...] = jnp.zeros_like(token)

    out = pl.pallas_call(
        body, name=name,
        out_shape=(pltpu.SemaphoreType.DMA((3 * nt,)), pltpu.SemaphoreType.DMA((3 * nt,)),
                   *[pltpu.HBM(a.shape, a.dtype) for a in parts + lands], _TOKEN),
        in_specs=[_HBM] * (2 * nt) + [_ANY] * len(order),
        out_specs=(_SEM, _SEM, *[_HBM] * (2 * nt), pl.BlockSpec(memory_space=pltpu.VMEM)),
        input_output_aliases={t: 2 + t for t in range(2 * nt)}, compiler_params=_SPLIT_COPY,
    )(*_in_hbm(parts + lands), *order)
    return out[0], out[1], list(out[2:2 + nt]), list(out[2 + nt:2 + 2 * nt]), out[-1]


def _chip_send_wait(send_sems, recv_sems, parts, lands, after, name):
    nt = len(parts)

    def body(*refs):
        for cp in _chip_copies(refs[:nt], refs[nt:2 * nt], *refs[2 * nt:2 * nt + 2]):
            cp.wait_send()
            cp.wait_recv()

    out = pl.pallas_call(
        body, name=name, out_shape=tuple(pltpu.HBM(a.shape, a.dtype) for a in parts + lands),
        in_specs=[_HBM] * (2 * nt) + [_SEM, _SEM, _ANY], out_specs=tuple([_HBM] * (2 * nt)),
        input_output_aliases={t: t for t in range(2 * nt)}, compiler_params=_SPLIT_COPY,
    )(*parts, *lands, send_sems, recv_sems, after)
    return list(out[:nt]), list(out[nt:])


def _chip_sum(part, arrived, into, lead, place_idx, name):
    _, h, cols = part.shape

    def body(idx_ref, own_ref, arr_ref, into_ref, o_ref):
        acc = own_ref[...].astype(F32)
        for k in range(N_CHIPS - 1):
            acc = acc + arr_ref[k].astype(F32)
        o_ref[...] = acc

    return pl.pallas_call(
        body, name=name,
        grid_spec=pltpu.PrefetchScalarGridSpec(
            num_scalar_prefetch=1, grid=(1,),
            in_specs=[pl.BlockSpec((None, h, cols), lambda g, idx: (idx[1], 0, 0)),
                      pl.BlockSpec((N_CHIPS - 1, h, cols), lambda g, idx: (0, 0, 0)), _ANY],
            out_specs=pl.BlockSpec((None,) * len(lead) + (h, cols), lambda g, idx: (*lead, idx[0], 0))),
        out_shape=jax.ShapeDtypeStruct(into.shape, F32), input_output_aliases={3: 0},
        compiler_params=_cparams(("arbitrary",)),
    )(place_idx, part, arrived, into)


def _pair_gather(bufs, homes, name):
    nt, nb = len(homes), len(bufs)

    def body(*refs):
        outs = refs[nb:2 * nb]
        send_sems, recv_sems = refs[2 * nb:]
        x, y, c = _place()

        def home(t, pc):
            o, lead, rows = homes[t]
            return outs[o].at[(*lead, pl.ds(pc * (rows // 2), rows // 2), slice(None))]

        def copy(t, pc):
            return pltpu.make_async_remote_copy(src_ref=home(t, pc), dst_ref=home(t, pc), send_sem=send_sems.at[t],
                                                recv_sem=recv_sems.at[t], device_id=(x, y, 1 - c), device_id_type=_MESH)

        sends = [copy(t, c) for t in range(nt)]
        for cp in sends:
            cp.start()
        for t in range(nt):
            copy(t, 1 - c).wait_recv()
        for cp in sends:
            cp.wait_send()

    return pl.pallas_call(
        body, name=name, out_shape=[jax.ShapeDtypeStruct(b.shape, b.dtype) for b in bufs],
        in_specs=[_ANY] * nb, out_specs=[_ANY] * nb, input_output_aliases={o: o for o in range(nb)},
        scratch_shapes=[pltpu.SemaphoreType.DMA((nt,)), pltpu.SemaphoreType.DMA((nt,))],
        compiler_params=_cparams(),
    )(*bufs)


def _sum_devices(g, after, name):
    def body(g_ref, after_ref, o_ref):
        acc = g_ref[0:1, :]
        for d in range(1, N_DEV):
            acc = acc + g_ref[d:d + 1, :]
        o_ref[...] = acc
    vmem = pl.BlockSpec(memory_space=pltpu.VMEM)
    return pl.pallas_call(body, name=name, out_shape=jax.ShapeDtypeStruct((1, g.shape[1]), F32),
                          in_specs=[vmem, _ANY], out_specs=vmem, compiler_params=_cparams())(g, after)


_WEIGHTS = ("w_cond", "b_cond", "norm_pre", "norm_post", "w_ffn_in", "w_ffn_out", "fox_w_in", "fox_b_f",
            "fox_w_out", "sconv_w_in", "sconv_conv_w", "sconv_w_out", "lru_w_in", "lru_conv_w", "lru_conv_b",
            "lru_w_a", "lru_b_a", "lru_w_x", "lru_b_x", "lru_lambda", "lru_w_out")
_BIG = (("w_ffn_in", False), ("w_ffn_out", True), ("fox_w_in", False), ("fox_w_out", True),
        ("sconv_w_in", False), ("sconv_w_out", True), ("lru_w_in", False), ("lru_w_out", True))
_SMALL = tuple(n for n in _WEIGHTS if n != "w_cond" and n not in dict(_BIG))
_COL_SHARDED_SMALL = ("norm_pre", "norm_post", "sconv_conv_w", "lru_conv_w", "lru_conv_b", "lru_lambda")


def _pack_rows(parts, rows=8):
    flat = jnp.concatenate([p.reshape(-1) for p in parts])
    width = -(-flat.size // (rows * 128)) * 128
    return jnp.pad(flat, (0, rows * width - flat.size)).reshape(rows, width)


def _unpack(flat, shapes):
    out, off = [], 0
    for shp in shapes:
        n = math.prod(shp)
        out.append(flat[off:off + n].reshape(shp))
        off += n
    return out


def _join_chips(g):
    g = jnp.moveaxis(g, 0, -2)
    return g.reshape(g.shape[:-2] + (g.shape[-2] * g.shape[-1],))


def _my_columns(full, chip):
    n = full.shape[-1] // N_CHIPS
    return lax.dynamic_slice_in_dim(full, chip * n, n, axis=full.ndim - 1)


def _block_diag(w):
    eye = jnp.eye(LRU_BLOCKS, dtype=w.dtype)
    return jnp.einsum("nij,nm->nimj", w, eye).reshape(D_MODEL, D_MODEL)


def _step(x, c, target, wts, mom, var):
    ix, iy, ic = _place()
    chip = 2 * ix + iy
    dev = 2 * chip + ic
    n_cond = wts["w_cond"].shape[2]

    small_shapes = [(D_MODEL,)] + [wts[n].shape for n in _COL_SHARDED_SMALL]
    g1 = _allgather8(_pack_rows([c[0]] + [wts[n] for n in _COL_SHARDED_SMALL]), "gather_small").reshape(N_DEV, -1)
    c_all = g1[:, :D_MODEL]
    per_chip = [jnp.stack(col) for col in zip(*[_unpack(g1[2 * k], small_shapes) for k in range(N_CHIPS)])]
    small_full = {n: _join_chips(v) for n, v in zip(_COL_SHARDED_SMALL, per_chip[1:])}

    c_pad = jnp.pad(c_all, ((0, COND_ROWS - N_DEV), (0, 0)))
    b_shard = _my_columns(wts["b_cond"], chip)[:, None, :]
    mod_part = _cond_fwd(c_pad, wts["w_cond"], b_shard, "cond_fwd")
    g2 = _allgather8(mod_part[:, :N_DEV].transpose(1, 0, 2).reshape(N_DEV, DEPTH * n_cond), "gather_mod")
    g2 = g2.reshape(N_DEV, N_DEV, DEPTH, n_cond)[0::2]
    mod = _join_chips(lax.dynamic_index_in_dim(g2, dev, axis=1, keepdims=False)).reshape(DEPTH, N_SUB, 3, D_MODEL)

    mixer_names = [("fox_w_in", "fox_w_out"), ("sconv_w_in", "sconv_w_out"), ("lru_w_in", "lru_w_out")]

    def shards_of(i, sub):
        if sub == 1:
            return [wts[n][i // 3] for n in mixer_names[i % 3]]
        return [wts["w_ffn_in"][i, sub // 2], wts["w_ffn_out"][i, sub // 2]]

    chunks = [[(0, sub)] for sub in range(N_SUB)] + [[(i, sub) for sub in range(N_SUB)] for i in range(1, DEPTH)]
    in_flight, chunk_of, token = [], {}, mod
    for k, members in enumerate(chunks):
        shards = [s for i, sub in members for s in shards_of(i, sub)]
        layouts = [_Gathered(s.shape, 0) for s in shards]
        if k:
            shards = [s + token[0, 0] for s in shards]
        lands = [_own_block_placed(s.astype(BF16), lay, chip) for s, lay in zip(shards, layouts)]
        send_sems, recv_sems, lands, token = _gather_start(lands, layouts, token, f"gather_start_{k}")
        in_flight.append([send_sems, recv_sems, lands, layouts, False])
        chunk_of.update({m: (k, 2 * pos) for pos, m in enumerate(members)})
    lru_ax = jnp.concatenate([_block_diag(wts["lru_w_a"][0]), _block_diag(wts["lru_w_x"][0])], axis=1).astype(BF16)

    def layer_params(i, sub, x_in):
        k, pos = chunk_of[(i, sub)]
        send_sems, recv_sems, lands, layouts, arrived = in_flight[k]
        if not arrived:
            lands = _gather_wait(send_sems, recv_sems, lands, layouts, x_in, f"gather_wait_{k}")
            in_flight[k][2:] = [_gather_forward(lands, layouts, f"gather_forward_{k}"), layouts, True]
        w_in, w_out = in_flight[k][2][pos:pos + 2]
        w_out = w_out.reshape(-1, w_out.shape[-1])
        if sub != 1:
            out = {"ffn_in": [_W(w_in, (), True)], "ffn_out": [_W(w_out)]}
            if sub == 0:
                out.update(norm_pre=small_full["norm_pre"][i], norm_post=small_full["norm_post"][i])
            return out
        j = i // 3
        if i % 3 == 0:
            w_in = jnp.pad(_join_chips(w_in), ((0, 0), (0, FOX_PAD - 3 * D_MODEL - FOX_HEADS)))
            return {"mixer": {"w_in": _W(w_in), "w_out": _W(w_out), "b_f": wts["fox_b_f"][j][:, None]}}
        if i % 3 == 1:
            return {"mixer": {"w_in": _W(w_in, (), True), "w_out": _W(w_out), "conv_w": small_full["sconv_conv_w"][j]}}
        return {"mixer": {"w_in": _W(w_in, (), True), "w_out": _W(w_out), "conv_w": small_full["lru_conv_w"][j],
                          "conv_b": small_full["lru_conv_b"], "w_ax": _W(lru_ax),
                          "b_a": wts["lru_b_a"].reshape(1, D_MODEL), "b_x": wts["lru_b_x"].reshape(1, D_MODEL),
                          "lam": small_full["lru_lambda"]}}

    place_idx = jnp.stack([ic, chip]).astype(jnp.int32)
    c_idx = place_idx[:1]
    big_index = {n: o for o, (n, _) in enumerate(_BIG)}
    exchanges, pending = [], []

    def to_chips(after):
        i, send_sems, recv_sems, tensors, lands, homes = pending.pop()
        tensors, recv = _pair_wait(send_sems, recv_sems, tensors, lands, after, f"grads_pair_wait_l{i}")
        parts = [_pair_sum(t, r, c_idx, f"grads_pair_sum_l{i}_{k}") for k, (t, r) in enumerate(zip(tensors, recv))]
        send_sems, recv_sems, parts, lands, tok = _chip_send_start(parts, None, f"grads_chip_start_l{i}")
        exchanges.append((i, send_sems, recv_sems, parts, lands, homes))
        return tok

    def chip_blocks(g, by_rows, width):
        if by_rows:
            return g.reshape(N_CHIPS, g.shape[0] // N_CHIPS, g.shape[1])
        if g.ndim == 3:
            return g
        return g[:, :width * N_CHIPS].reshape(g.shape[0], N_CHIPS, width).transpose(1, 0, 2)

    def on_mid(i, dx):
        return to_chips(dx) if pending else None

    def on_grads(i, g, dx):
        n_in, n_out = mixer_names[i % 3]
        items = [("w_ffn_in", (i, k), g["ffn_in"][k]) for k in range(2)]
        items += [("w_ffn_out", (i, k), g["ffn_out"][k]) for k in range(2)]
        items += [(n_in, (i // 3,), g["mixer"]["w_in"]), (n_out, (i // 3,), g["mixer"]["w_out"])]
        tensors = [chip_blocks(t, dict(_BIG)[n], wts[n].shape[-1]) for n, _, t in items]
        homes = [(big_index[n], lead, wts[n].shape[-2]) for n, lead, _ in items]
        send_sems, recv_sems, tensors, lands, tok = _pair_start(tensors, None, f"grads_pair_start_l{i}")
        pending.append((i, send_sems, recv_sems, tensors, lands, homes))
        pair_tokens.append(tok)
        return tok

    pair_tokens = []
    loss_row, grad_x, dmod, lg = _local_step(x[0], target[0], mod, layer_params, on_grads, on_mid, token)
    loss = lax.psum(loss_row[0, 0], ("x", "y", "c"))
    dmod = dmod + pair_tokens[-1][0, 0]

    fox_layers = [i for i in range(DEPTH) if i % 3 == 0]
    sconv_g, lru_g = lg[1]["mixer"], lg[2]["mixer"]
    small_g = {
        "dmod": dmod, "norm_pre": jnp.stack([g["norm_pre"] for g in lg]), "norm_post": jnp.stack([g["norm_post"] for g in lg]),
        "fox_b_f": jnp.stack([lg[i]["mixer"]["b_f"][:, 0] for i in fox_layers]),
        "sconv_conv_w": sconv_g["conv_w"][None], "lru_conv_w": lru_g["conv_w"][None], "lru_conv_b": lru_g["conv_b"],
        "lru_w_a": lru_g["w_a"][None], "lru_b_a": lru_g["b_a"].reshape(1, LRU_BLOCKS, LRU_BLOCK_DIM),
        "lru_w_x": lru_g["w_x"][None], "lru_b_x": lru_g["b_x"].reshape(1, LRU_BLOCKS, LRU_BLOCK_DIM),
        "lru_lambda": lru_g["lam"]}
    g4 = _allgather8(_pack_rows(list(small_g.values())), "gather_small_grads").reshape(N_DEV, -1)
    last_start = to_chips(g4)
    summed = _sum_devices(g4, last_start, "sum_small_grads")[0]
    summed = dict(zip(small_g, _unpack(summed, [v.shape for v in small_g.values()])))
    grads = {n: (_my_columns(summed[n], chip) if n in _COL_SHARDED_SMALL else summed[n]) for n in _SMALL if n != "b_cond"}
    grads["b_cond"] = summed["dmod"].reshape(DEPTH, N_SUB * 3 * D_MODEL)

    dmod_all = (g4[:, :dmod.size] + last_start[0, 0]).reshape(N_DEV, DEPTH, N_SUB * 3 * D_MODEL)
    dmod_s = jnp.pad(_my_columns(dmod_all, chip).transpose(1, 0, 2), ((0, 0), (0, COND_PAD - N_DEV), (0, 0))).astype(BF16)
    c_t = jnp.pad(c_all.T, ((0, 0), (0, COND_PAD - N_DEV)))
    grads["w_cond"], d_cond, m_cond, v_cond = _cond_bwd_adamw(c_t, dmod_s, wts["w_cond"], mom["w_cond"],
                                                              var["w_cond"], "cond_bwd_adamw")

    big = [n for n, _ in _BIG]
    two_d = lambda a: a.reshape(-1, a.shape[-1])
    bufs = [lax.empty(wts[n].shape, F32) for n in big]
    updates = [[lax.empty(two_d(wts[n]).shape, F32) for _ in range(3)] for n in big]
    follows = d_cond
    for i, send_sems, recv_sems, parts, lands, homes in exchanges:
        parts, lands = _chip_send_wait(send_sems, recv_sems, parts, lands, follows, f"grads_chip_wait_l{i}")
        for k, (part, land, (o, lead, _)) in enumerate(zip(parts, lands, homes)):
            bufs[o] = _chip_sum(part, land, bufs[o], lead, place_idx, f"grads_chip_sum_l{i}_{k}")
        bufs = list(_pair_gather(bufs, homes, f"grads_pair_gather_l{i}"))
        for o in sorted({o for o, _, _ in homes}):
            n = big[o]
            starts = [sum(a * math.prod(wts[n].shape[d + 1:-1]) for d, a in enumerate(lead))
                      for oo, lead, _ in homes if oo == o]
            rows = wts[n].shape[-2]
            *updates[o], g_out = _adamw_rows(two_d(wts[n]), two_d(bufs[o]), two_d(mom[n]), two_d(var[n]), updates[o],
                                             min(starts), max(starts) + rows - min(starts), f"adamw_{n}_l{i}")
            bufs[o] = g_out.reshape(wts[n].shape)
        follows = updates[0][0]
    grads.update(zip(big, bufs))

    delta, new_m, new_v = {"w_cond": d_cond}, {"w_cond": m_cond}, {"w_cond": v_cond}
    for n, (d, nm, nv) in zip(big, updates):
        delta[n], new_m[n], new_v[n] = (a.reshape(wts[n].shape) for a in (d, nm, nv))
    shapes = [wts[n].shape for n in _SMALL]
    packed = [_pack_rows([src[n] for n in _SMALL]) for src in (wts, grads, mom, var)]
    for dst, out in zip((delta, new_m, new_v), _adamw(*packed, "adamw_small")):
        dst.update(zip(_SMALL, _unpack(out.reshape(-1), shapes)))

    return (loss, grad_x[None], *[grads[n] for n in _WEIGHTS], *[delta[n] for n in _WEIGHTS],
            *[new_m[n] for n in _WEIGHTS], *[new_v[n] for n in _WEIGHTS])


def kernel(x, c, w_cond, b_cond, norm_pre, norm_post, w_ffn_in, w_ffn_out, fox_w_in, fox_b_f, fox_w_out, sconv_w_in, sconv_conv_w, sconv_w_out, lru_w_in, lru_conv_w, lru_conv_b, lru_w_a, lru_b_a, lru_w_x, lru_b_x, lru_lambda, lru_w_out, loss_target, m_w_cond, m_b_cond, m_norm_pre, m_norm_post, m_w_ffn_in, m_w_ffn_out, m_fox_w_in, m_fox_b_f, m_fox_w_out, m_sconv_w_in, m_sconv_conv_w, m_sconv_w_out, m_lru_w_in, m_lru_conv_w, m_lru_conv_b, m_lru_w_a, m_lru_b_a, m_lru_w_x, m_lru_b_x, m_lru_lambda, m_lru_w_out, v_w_cond, v_b_cond, v_norm_pre, v_norm_post, v_w_ffn_in, v_w_ffn_out, v_fox_w_in, v_fox_b_f, v_fox_w_out, v_sconv_w_in, v_sconv_conv_w, v_sconv_w_out, v_lru_w_in, v_lru_conv_w, v_lru_conv_b, v_lru_w_a, v_lru_b_a, v_lru_w_x, v_lru_b_x, v_lru_lambda, v_lru_w_out):
    given = dict(locals())
    wts = {n: given[n] for n in _WEIGHTS}
    mom = {n: given["m_" + n] for n in _WEIGHTS}
    var = {n: given["v_" + n] for n in _WEIGHTS}
    return _step(x, c, loss_target, wts, mom, var)
```

```python
import functools
import math
from typing import NamedTuple

import jax
import jax.numpy as jnp
from jax import lax
from jax.experimental import pallas as pl
from jax.experimental.pallas import tpu as pltpu

F32 = jnp.float32
BF16 = jnp.bfloat16

D_MODEL = 1024
DEPTH = 4
N_SUB = 3
D_FF = 2816
RMS_EPS = 1e-6
FOX_HEADS = 16
FOX_HEAD_DIM = 64
FOX_PAD = 3200
LRU_BLOCKS = 16
LRU_BLOCK_DIM = 64
LRU_C = 8.0
N_CHIPS = 4
N_DEV = 8

ADAM_LR = 0.001
ADAM_B1 = 0.9
ADAM_B2 = 0.999
ADAM_EPS = 1e-08
ADAM_WD = 0.01
ADAM_STEP = 10

VMEM_LIMIT_V7X = 56 * 1024 * 1024
ROW_TILE = 512
COL_TILE = 256
ATT_TILE = 256
ATT_WIDE = 512
MM_ROWS = 1024


def _cparams(sem=None):
    return pltpu.CompilerParams(vmem_limit_bytes=VMEM_LIMIT_V7X, dimension_semantics=sem)


def _sigmoid(z):
    return 1.0 / (1.0 + jnp.exp(-z))


def _softplus(z):
    return jnp.maximum(z, 0.0) + jnp.log(1.0 + jnp.exp(-jnp.abs(z)))


def _rows_sum(v):
    return jnp.sum(v, axis=0, keepdims=True)


class _W(NamedTuple):
    arr: jax.Array
    prefix: tuple = ()
    blocked: bool = False


def _w_spec(w, block2, pos):
    lead = (None,) * (len(w.prefix) + (1 if w.blocked else 0))
    if w.blocked:
        return pl.BlockSpec(lead + block2, lambda *g: (pos(*g)[0], *w.prefix, pos(*g)[1], pos(*g)[2]))
    return pl.BlockSpec(lead + block2, lambda *g: (*w.prefix, pos(*g)[1], pos(*g)[2]))


def _mm_nn(a, b, name, tn=None):
    m, k = a.shape
    if b.blocked:
        steps, bn = b.arr.shape[0], b.arr.shape[-1]
        b_spec = _w_spec(b, (k, bn), lambda n: (n, 0, 0))
    else:
        n_total = b.arr.shape[-1]
        bn = n_total if tn is None else tn
        steps = n_total // bn
        assert steps * bn == n_total
        b_spec = _w_spec(b, (k, bn), lambda n: (0, 0, n))
    tm = min(MM_ROWS, m)

    def body(a_ref, b_ref, o_ref):
        def step(i, carry):
            r = pl.ds(pl.multiple_of(i * tm, tm), tm)
            o_ref[r, :] = jnp.dot(a_ref[r, :], b_ref[...], preferred_element_type=F32)
            return carry
        lax.fori_loop(0, m // tm, step, 0)

    return pl.pallas_call(
        body, name=name, grid=(steps,),
        in_specs=[pl.BlockSpec((m, k), lambda n: (0, 0)), b_spec],
        out_specs=pl.BlockSpec((m, bn), lambda n: (0, n)),
        out_shape=jax.ShapeDtypeStruct((m, steps * bn), F32),
        compiler_params=_cparams(("arbitrary",)),
    )(a, b.arr)


def _cols_shape(dy):
    return (dy.shape[0], dy.shape[1]) if dy.ndim == 2 else (dy.shape[1], 2 * dy.shape[2])


def _cols_spec(dy, bn):
    if dy.ndim == 2:
        return pl.BlockSpec((dy.shape[0], bn), lambda kt, n: (0, n))
    per = dy.shape[2] // bn
    assert per * bn == dy.shape[2]
    return pl.BlockSpec((None, dy.shape[1], bn), lambda kt, n: (n // per, 0, n % per))


def _mm_nt(dy, w, name, tk=None, tn=None):
    m, n_total = _cols_shape(dy)
    k = w.arr.shape[-2]
    if w.blocked:
        bk, bn = k, w.arr.shape[-1]
        grid = (1, w.arr.shape[0])
        w_spec = _w_spec(w, (k, bn), lambda kt, n: (n, 0, 0))
    else:
        bk = k if tk is None else tk
        bn = n_total if tn is None else tn
        grid = (k // bk, n_total // bn)
        assert grid[0] * bk == k and grid[1] * bn == n_total
        w_spec = _w_spec(w, (bk, bn), lambda kt, n: (0, kt, n))
    tm = min(MM_ROWS, m)

    reduce_steps = grid[1]

    def body(dy_ref, w_ref, o_ref):
        def step(i, carry):
            r = pl.ds(pl.multiple_of(i * tm, tm), tm)
            part = lax.dot_general(dy_ref[r, :], w_ref[...], (((1,), (1,)), ((), ())), preferred_element_type=F32)
            if reduce_steps == 1:
                o_ref[r, :] = part
            else:
                o_ref[r, :] += part
            return carry

        if reduce_steps > 1:
            @pl.when(pl.program_id(1) == 0)
            def _():
                o_ref[...] = jnp.zeros_like(o_ref)
        lax.fori_loop(0, m // tm, step, 0)

    return pl.pallas_call(
        body, name=name, grid=grid,
        in_specs=[_cols_spec(dy, bn), w_spec],
        out_specs=pl.BlockSpec((m, bk), lambda kt, n: (0, kt)),
        out_shape=jax.ShapeDtypeStruct((m, k), F32),
        compiler_params=_cparams(("arbitrary", "arbitrary")),
    )(dy, w.arr)


def _mm_tn(x, dy, name, tk=None, tn=None, blocked_out=False):
    s, k = x.shape
    n_total = _cols_shape(dy)[1]
    bk = k if tk is None else tk
    bn = n_total if tn is None else tn
    grid = (k // bk, n_total // bn)
    assert grid[0] * bk == k and grid[1] * bn == n_total
    ck = next(c for c in (512, 256, 128) if bk % c == 0)

    def body(x_ref, dy_ref, o_ref):
        def step(i, carry):
            c = pl.ds(pl.multiple_of(i * ck, ck), ck)
            o_ref[c, :] = lax.dot_general(x_ref[:, c], dy_ref[...], (((0,), (0,)), ((), ())),
                                          preferred_element_type=F32)
            return carry
        lax.fori_loop(0, bk // ck, step, 0)

    if blocked_out:
        assert grid[0] == 1
        out_spec = pl.BlockSpec((None, bk, bn), lambda kt, n: (n, 0, 0))
        out_shape = jax.ShapeDtypeStruct((grid[1], k, bn), F32)
    else:
        out_spec = pl.BlockSpec((bk, bn), lambda kt, n: (kt, n))
        out_shape = jax.ShapeDtypeStruct((k, n_total), F32)
    return pl.pallas_call(
        body, name=name, grid=grid,
        in_specs=[pl.BlockSpec((s, bk), lambda kt, n: (0, kt)), _cols_spec(dy, bn)],
        out_specs=out_spec, out_shape=out_shape,
        compiler_params=_cparams(("arbitrary", "arbitrary")),
    )(x, dy)


def _row_call(name, body, rows, fulls, row_outs, acc_outs, tr=ROW_TILE, after=None):
    s = rows[0].shape[0]
    tr = min(tr, s)
    in_specs = [pl.BlockSpec((tr, a.shape[1]), lambda i: (i, 0)) for a in rows]
    in_specs += [pl.BlockSpec(a.shape, lambda i: (0, 0)) for a in fulls]
    n_in = len(in_specs)
    order = [] if after is None else [after]
    in_specs += [pl.BlockSpec(memory_space=pl.ANY)] * len(order)
    out_specs = [pl.BlockSpec((tr, c), lambda i: (i, 0)) for c, _ in row_outs]
    out_specs += [pl.BlockSpec((1, c), lambda i: (0, 0)) for c, _ in acc_outs]
    out_shape = [jax.ShapeDtypeStruct((s, c), dt) for c, dt in row_outs]
    out_shape += [jax.ShapeDtypeStruct((1, c), dt) for c, dt in acc_outs]
    n_acc = len(acc_outs)

    def wrapped(*refs):
        refs = refs[:n_in] + refs[n_in + len(order):]
        if n_acc:
            @pl.when(pl.program_id(0) == 0)
            def _():
                for r in refs[len(refs) - n_acc:]:
                    r[...] = jnp.zeros_like(r)
        body(*refs)

    return pl.pallas_call(
        wrapped, name=name, grid=(s // tr,), in_specs=in_specs, out_specs=out_specs, out_shape=out_shape,
        compiler_params=_cparams(("arbitrary",)),
    )(*rows, *fulls, *order)


def _rms(v):
    return lax.rsqrt(jnp.mean(v * v, axis=-1, keepdims=True) + RMS_EPS)


def _pre_norm(x, g_pre, scale, shift, name, after=None):
    def body(x_ref, g_ref, sc_ref, sh_ref, h_ref):
        xv = x_ref[...]
        h = (xv * _rms(xv)) * g_ref[...] * (1.0 + sc_ref[...]) + sh_ref[...]
        h_ref[...] = h.astype(BF16)
    return _row_call(name, body, [x], [g_pre, scale, shift], [(D_MODEL, BF16)], [], after=after)[0]


def _post_norm(x, y, g_post, gate, coef, name):
    def body(x_ref, y_ref, g_ref, gate_ref, o_ref):
        yv = y_ref[...]
        o_ref[...] = x_ref[...] + (coef * gate_ref[...]) * ((yv * _rms(yv)) * g_ref[...])
    return _row_call(name, body, [x, y], [g_post, gate], [(D_MODEL, F32)], [])[0]


def _post_norm_bwd(dxo, y, g_post, gate, coef, name, after=None):
    def body(dxo_ref, y_ref, g_ref, gate_ref, dy_ref, dgate_ref, dg_ref):
        yv = y_ref[...]
        r2 = _rms(yv)
        yn = yv * r2
        dxo_v = dxo_ref[...]
        dgate_ref[...] += _rows_sum(dxo_v * (yn * g_ref[...])) * coef
        dz = dxo_v * (coef * gate_ref[...])
        dg_ref[...] += _rows_sum(dz * yn)
        dyn = dz * g_ref[...]
        dy = r2 * (dyn - yn * jnp.mean(dyn * yn, axis=-1, keepdims=True))
        dy_ref[...] = dy.astype(BF16)
    return _row_call(name, body, [dxo, y], [g_post, gate], [(D_MODEL, BF16)], [(D_MODEL, F32), (D_MODEL, F32)],
                     after=after)


def _pre_norm_bwd(dxo, dh, x, g_pre, scale, name):
    def body(dxo_ref, dh_ref, x_ref, g_ref, sc_ref, dx_ref, dshift_ref, dscale_ref, dg_ref):
        xv = x_ref[...]
        r = _rms(xv)
        xn = xv * r
        dh_v = dh_ref[...]
        one_sc = 1.0 + sc_ref[...]
        dshift_ref[...] += _rows_sum(dh_v)
        dscale_ref[...] += _rows_sum(dh_v * (xn * g_ref[...]))
        dg_ref[...] += _rows_sum(dh_v * xn * one_sc)
        dxn = dh_v * (g_ref[...] * one_sc)
        dx_ref[...] = dxo_ref[...] + r * (dxn - xn * jnp.mean(dxn * xn, axis=-1, keepdims=True))
    return _row_call(name, body, [dxo, dh, x], [g_pre, scale], [(D_MODEL, F32)],
                     [(D_MODEL, F32), (D_MODEL, F32), (D_MODEL, F32)])


FFN_COLS = 1408


def _ffn_in_act(h, w_in, name):
    m, k = h.shape
    half, bn = w_in.arr.shape[0] // 2, w_in.arr.shape[-1]
    assert bn == FFN_COLS and half * bn == D_FF
    tm = min(MM_ROWS, m)

    def body(h_ref, wg_ref, wu_ref, g_ref, u_ref, a_ref):
        g = jnp.dot(h_ref[...], wg_ref[...], preferred_element_type=F32)
        g_ref[...] = g
        u = jnp.dot(h_ref[...], wu_ref[...], preferred_element_type=F32)
        u_ref[...] = u
        a_ref[...] = (g * _sigmoid(g) * u).astype(BF16)

    tile = pl.BlockSpec((tm, bn), lambda t, i: (i, t))
    return pl.pallas_call(
        body, name=name, grid=(half, m // tm),
        in_specs=[pl.BlockSpec((tm, k), lambda t, i: (i, 0)),
                  _w_spec(w_in, (k, bn), lambda t, i: (t, 0, 0)),
                  _w_spec(w_in, (k, bn), lambda t, i: (half + t, 0, 0))],
        out_specs=[tile, tile, tile],
        out_shape=[jax.ShapeDtypeStruct((m, D_FF), F32)] * 2 + [jax.ShapeDtypeStruct((m, D_FF), BF16)],
        compiler_params=_cparams(("arbitrary", "arbitrary")),
    )(h, w_in.arr, w_in.arr)


def _ffn_out_bx_act(dy, w_out, g, u, name):
    m = dy.shape[0]
    tr = min(MM_ROWS, m)

    def body(dy_ref, w_ref, g_ref, u_ref, dgu_ref):
        da = lax.dot_general(dy_ref[...], w_ref[...], _NT, preferred_element_type=F32)
        gv = g_ref[...]
        sg = _sigmoid(gv)
        dgu_ref[0] = (da * u_ref[...] * (sg * (1.0 + gv * (1.0 - sg)))).astype(BF16)
        dgu_ref[1] = (da * (gv * sg)).astype(BF16)

    tile = pl.BlockSpec((tr, FFN_COLS), lambda i, c: (i, c))
    return pl.pallas_call(
        body, name=name, grid=(m // tr, D_FF // FFN_COLS),
        in_specs=[pl.BlockSpec((tr, D_MODEL), lambda i, c: (i, 0)),
                  _w_spec(w_out, (FFN_COLS, D_MODEL), lambda i, c: (0, c, 0)), tile, tile],
        out_specs=pl.BlockSpec((2, tr, FFN_COLS), lambda i, c: (0, i, c)),
        out_shape=jax.ShapeDtypeStruct((2, m, D_FF), BF16),
        compiler_params=_cparams(("arbitrary", "arbitrary")),
    )(dy, w_out.arr, g, u)


def _ffn_in_bwd(dgu, h, w_in, name):
    m, k = h.shape
    nb, bn = w_in.arr.shape[0], w_in.arr.shape[-1]
    per = dgu.shape[2] // bn
    tm = min(MM_ROWS, m)
    ck = next(c for c in (512, 256, 128) if k % c == 0)
    once = pl.Buffered(1)

    def body(dgu_ref, h_ref, w_ref, dh_ref, dw_ref):
        @pl.when(pl.program_id(0) == 0)
        def _():
            dh_ref[...] = jnp.zeros_like(dh_ref)

        def rows(i, carry):
            r = pl.ds(pl.multiple_of(i * tm, tm), tm)
            dh_ref[r, :] += lax.dot_general(dgu_ref[r, :], w_ref[...], _NT, preferred_element_type=F32)
            return carry
        lax.fori_loop(0, m // tm, rows, 0)

        def cols(i, carry):
            c = pl.ds(pl.multiple_of(i * ck, ck), ck)
            dw_ref[c, :] = lax.dot_general(h_ref[:, c], dgu_ref[...], (((0,), (0,)), ((), ())),
                                           preferred_element_type=F32)
            return carry
        lax.fori_loop(0, k // ck, cols, 0)

    return pl.pallas_call(
        body, name=name, grid=(nb,),
        in_specs=[pl.BlockSpec((None, m, bn), lambda n: (n // per, 0, n % per)),
                  pl.BlockSpec((m, k), lambda n: (0, 0), pipeline_mode=once),
                  _w_spec(w_in, (k, bn), lambda n: (n, 0, 0))],
        out_specs=[pl.BlockSpec((m, k), lambda n: (0, 0), pipeline_mode=once),
                   pl.BlockSpec((None, k, bn), lambda n: (n, 0, 0))],
        out_shape=[jax.ShapeDtypeStruct((m, k), F32), jax.ShapeDtypeStruct((nb, k, bn), F32)],
        compiler_params=_cparams(("arbitrary",)),
    )(dgu, h, w_in.arr)


def _loss_head(y, target, name):
    def body(y_ref, t_ref, dy_ref, loss_ref):
        e = y_ref[...] - t_ref[...]
        dy_ref[...] = e * (1.0 / D_MODEL)
        part = jnp.sum(jnp.mean(e * e, axis=-1, keepdims=True), axis=0, keepdims=True) * 0.5
        loss_ref[...] += jnp.broadcast_to(part, loss_ref.shape)
    return _row_call(name, body, [y, target], [], [(D_MODEL, F32)], [(128, F32)])


def _lane_scan(v, reverse):
    s = v.shape[1]
    lane = lax.broadcasted_iota(jnp.int32, v.shape, 1)
    d = 1
    while d < s:
        if reverse:
            v = v + jnp.where(lane < s - d, pltpu.roll(v, s - d, 1), 0.0)
        else:
            v = v + jnp.where(lane >= d, pltpu.roll(v, d, 1), 0.0)
        d *= 2
    return v


def _fox_gate(flt, b_f, name):
    def body(f_ref, b_ref, cum_ref):
        z = f_ref[...] + b_ref[...]
        cum_ref[...] = _lane_scan(-_softplus(-z), reverse=False)
    return pl.pallas_call(body, name=name, out_shape=jax.ShapeDtypeStruct(flt.shape, F32),
                          compiler_params=_cparams())(flt, b_f)


def _fox_gate_bwd(dcum_q, dcum_k, flt, b_f, name):
    def body(dq_ref, dk_ref, f_ref, b_ref, df_ref, db_ref):
        z = f_ref[...] + b_ref[...]
        df = _lane_scan(dq_ref[...] + dk_ref[...], reverse=True) * _sigmoid(-z)
        df_ref[...] = df
        db_ref[...] = jnp.sum(df, axis=1, keepdims=True)
    h = flt.shape[0]
    return pl.pallas_call(body, name=name,
                          out_shape=(jax.ShapeDtypeStruct(flt.shape, F32), jax.ShapeDtypeStruct((h, 1), F32)),
                          compiler_params=_cparams())(dcum_q, dcum_k, flt, b_f)


def _pick_head(block, h):
    lane = lax.broadcasted_iota(jnp.int32, block.shape, 1)
    return jnp.sum(jnp.where(lane == h, block, 0.0), axis=1, keepdims=True)


def _put_head(ref, col, h):
    @pl.when(h == 0)
    def _():
        ref[...] = jnp.zeros_like(ref)
    lane = lax.broadcasted_iota(jnp.int32, ref.shape, 1)
    ref[...] = jnp.where(lane == h, col, ref[...])


_NT = (((1,), (1,)), ((), ()))
_FOX_SCALE = FOX_HEAD_DIM ** -0.5


HEAD_PAIRS = FOX_HEADS // 2
PAIR_W = 2 * FOX_HEAD_DIM


def _low_half(shape):
    return lax.broadcasted_iota(jnp.int32, shape, 1) < FOX_HEAD_DIM


def _fox_attn_fwd(qkv, cum, cum_t, name):
    s = qkv.shape[0]
    t = min(ATT_TILE, s)
    wide = min(ATT_WIDE, s)

    def body(q_ref, k_ref, v_ref, cum_ref, cumt_ref, o_ref, ob_ref, lse_ref):
        i = pl.program_id(0)
        hp = pl.program_id(1)
        lo = _low_half((t, PAIR_W))
        qv = q_ref[...]
        zero = jnp.zeros_like(qv)
        q2 = (jnp.where(lo, qv, zero), jnp.where(lo, zero, qv))
        cum_v = cum_ref[...]
        cq2 = (_pick_head(cum_v, 2 * hp), _pick_head(cum_v, 2 * hp + 1))

        def step(j, carry, masked):
            ks = pl.ds(pl.multiple_of(j * wide, wide), wide)
            kj = k_ref[ks, :]
            vj = v_ref[ks, :]
            out = []
            for e in range(2):
                m, l, acc = carry[e]
                sc = lax.dot_general(q2[e], kj, _NT, preferred_element_type=F32) * _FOX_SCALE
                sc = sc + cq2[e] - cumt_ref[e:e + 1, ks]
                if masked:
                    q_pos = i * t + lax.broadcasted_iota(jnp.int32, (t, wide), 0)
                    k_pos = j * wide + lax.broadcasted_iota(jnp.int32, (t, wide), 1)
                    sc = jnp.where(k_pos <= q_pos, sc, -jnp.inf)
                m_new = jnp.maximum(m, jnp.max(sc, axis=1, keepdims=True))
                alpha = jnp.exp(m - m_new)
                p = jnp.exp(sc - m_new)
                l = alpha * l + jnp.sum(p, axis=1, keepdims=True)
                acc = alpha * acc + jnp.dot(p.astype(BF16), vj, preferred_element_type=F32)
                out.append((m_new, l, acc))
            return tuple(out)

        one = (jnp.full((t, 1), -jnp.inf, F32), jnp.zeros((t, 1), F32), jnp.zeros((t, PAIR_W), F32))
        whole = (i * t) // wide
        carry = lax.fori_loop(0, whole, lambda j, c: step(j, c, False), (one, one))
        (m0, l0, a0), (m1, l1, a1) = step(whole, carry, True)
        o = jnp.where(lo, a0 / l0, a1 / l1)
        o_ref[...] = o
        ob_ref[...] = o.astype(BF16)
        _put_head(lse_ref, m0 + jnp.log(l0), 2 * hp)
        _put_head(lse_ref, m1 + jnp.log(l1), 2 * hp + 1)

    nat_tile = pl.BlockSpec((t, FOX_HEADS), lambda i, hp: (i, 0))
    out_tile = pl.BlockSpec((t, PAIR_W), lambda i, hp: (i, hp))
    return pl.pallas_call(
        body, name=name, grid=(s // t, HEAD_PAIRS),
        in_specs=[pl.BlockSpec((t, PAIR_W), lambda i, hp: (i, hp)),
                  pl.BlockSpec((s, PAIR_W), lambda i, hp: (0, HEAD_PAIRS + hp)),
                  pl.BlockSpec((s, PAIR_W), lambda i, hp: (0, 2 * HEAD_PAIRS + hp)),
                  nat_tile, pl.BlockSpec((None, 2, s), lambda i, hp: (hp, 0, 0))],
        out_specs=[out_tile, out_tile, nat_tile],
        out_shape=[jax.ShapeDtypeStruct((s, D_MODEL), F32), jax.ShapeDtypeStruct((s, D_MODEL), BF16),
                   jax.ShapeDtypeStruct((s, FOX_HEADS), F32)],
        compiler_params=_cparams(("arbitrary", "arbitrary")),
    )(qkv, qkv, qkv, cum, cum_t)


def _fox_delta(do, o, expand, name):
    def body(do_ref, o_ref, e_ref, d_ref):
        prod = do_ref[...] * o_ref[...]
        hi = prod.astype(BF16)
        lo = (prod - hi.astype(F32)).astype(BF16)
        tot = (jnp.dot(hi, e_ref[...], preferred_element_type=F32)
               + jnp.dot(lo, e_ref[...], preferred_element_type=F32))
        d_ref[...] = tot[:, :FOX_HEADS]
    return _row_call(name, body, [do, o], [expand], [(FOX_HEADS, F32)], [])[0]


def _fox_attn_bwd(qkv, do, cum, cum_t, lse_t, delta_t, name):
    s = qkv.shape[0]
    t = min(ATT_TILE, s)
    wide = min(ATT_WIDE, s)
    nq = s // t
    tn_dims = (((0,), (0,)), ((), ()))

    def body(q_ref, k_ref, v_ref, do_ref, cum_ref, cumt_ref, lset_ref, deltat_ref,
             dq_ref, dk_ref, dv_ref, dck_ref, dcq_ref):
        hp = pl.program_id(0)
        j = pl.program_id(1)

        @pl.when(j == 0)
        def _():
            dq_ref[...] = jnp.zeros_like(dq_ref)
            dcq_ref[...] = jnp.zeros_like(dcq_ref)
        dk_ref[...] = jnp.zeros_like(dk_ref)
        dv_ref[...] = jnp.zeros_like(dv_ref)

        lo = _low_half((t, PAIR_W))
        lane = lax.broadcasted_iota(jnp.int32, (t, PAIR_W), 1)
        kv = k_ref[...]
        vv = v_ref[...]
        zero = jnp.zeros_like(kv)
        k2 = (jnp.where(lo, kv, zero), jnp.where(lo, zero, kv))
        v2 = (jnp.where(lo, vv, zero), jnp.where(lo, zero, vv))
        cum_v = cum_ref[...]
        ck2 = (_pick_head(cum_v, 2 * hp), _pick_head(cum_v, 2 * hp + 1))

        def step(i, dck, masked):
            qs = pl.ds(pl.multiple_of(i * wide, wide), wide)
            qi = q_ref[qs, :]
            do_i = do_ref[qs, :].astype(BF16)
            dv_p, dk_p, dq_p = [], [], []
            for e in range(2):
                st = lax.dot_general(k2[e], qi, _NT, preferred_element_type=F32) * _FOX_SCALE
                st = st + cumt_ref[e:e + 1, qs] - ck2[e]
                if masked:
                    k_pos = j * t + lax.broadcasted_iota(jnp.int32, (t, wide), 0)
                    q_pos = i * wide + lax.broadcasted_iota(jnp.int32, (t, wide), 1)
                    st = jnp.where(k_pos <= q_pos, st, -jnp.inf)
                pt = jnp.exp(st - lset_ref[e:e + 1, qs])
                dv_p.append(jnp.dot(pt.astype(BF16), do_i, preferred_element_type=F32))
                dpt = lax.dot_general(v2[e], do_i, _NT, preferred_element_type=F32)
                dst = pt * (dpt - deltat_ref[e:e + 1, qs])
                dsb = dst.astype(BF16)
                dk_p.append(jnp.dot(dsb, qi, preferred_element_type=F32))
                dq_p.append(lax.dot_general(dsb, kv, tn_dims, preferred_element_type=F32))
                dck = dck - jnp.where(lane == e, jnp.sum(dst, axis=1, keepdims=True), 0.0)
                dcq_ref[e:e + 1, qs] += jnp.sum(dst, axis=0, keepdims=True)
            dv_ref[...] += jnp.where(lo, dv_p[0], dv_p[1])
            dk_ref[...] += jnp.where(lo, dk_p[0], dk_p[1])
            dq_ref[qs, :] += jnp.where(_low_half((wide, PAIR_W)), dq_p[0], dq_p[1]) * _FOX_SCALE
            return dck

        first = (j * t) // wide
        dck = step(first, jnp.zeros((t, PAIR_W), F32), True)
        dck = lax.fori_loop(first + 1, s // wide, lambda i, c: step(i, c, False), dck)
        dk_ref[...] = dk_ref[...] * _FOX_SCALE
        dck_ref[...] = dck

    pair_full = lambda part: pl.BlockSpec((s, PAIR_W), lambda hp, j: (0, part * HEAD_PAIRS + hp))
    pair_tile = lambda part: pl.BlockSpec((t, PAIR_W), lambda hp, j: (j, part * HEAD_PAIRS + hp))
    rows = pl.BlockSpec((None, 2, s), lambda hp, j: (hp, 0, 0))
    return pl.pallas_call(
        body, name=name, grid=(HEAD_PAIRS, nq),
        in_specs=[pair_full(0), pair_tile(1), pair_tile(2), pair_full(0),
                  pl.BlockSpec((t, FOX_HEADS), lambda hp, j: (j, 0)), rows, rows, rows],
        out_specs=[pair_full(0), pair_tile(0), pair_tile(0),
                   pl.BlockSpec((None, t, PAIR_W), lambda hp, j: (hp, j, 0)), rows],
        out_shape=[jax.ShapeDtypeStruct((s, D_MODEL), F32)] * 3
        + [jax.ShapeDtypeStruct((HEAD_PAIRS, s, PAIR_W), F32), jax.ShapeDtypeStruct((HEAD_PAIRS, 2, s), F32)],
        compiler_params=_cparams(("arbitrary", "arbitrary")),
    )(qkv, qkv, qkv, do, cum, cum_t, lse_t, delta_t)


def _shift_down(v, d):
    row = lax.broadcasted_iota(jnp.int32, v.shape, 0)
    return jnp.where(row >= d, pltpu.roll(v, d, 0), 0.0)


def _shift_up(v, d):
    s = v.shape[0]
    row = lax.broadcasted_iota(jnp.int32, v.shape, 0)
    return jnp.where(row < s - d, pltpu.roll(v, s - d, 0), 0.0)


def _conv_taps(v, cw_ref, width):
    out = cw_ref[width - 1:width, :] * v
    for k in range(width - 1):
        out = out + cw_ref[k:k + 1, :] * _shift_down(v, width - 1 - k)
    return out


def _conv_taps_bwd(dout, v, cw_ref, dcw_ref, width):
    dv = cw_ref[width - 1:width, :] * dout
    dcw_ref[width - 1:width, :] = _rows_sum(dout * v)
    for k in range(width - 1):
        d = width - 1 - k
        dv = dv + cw_ref[k:k + 1, :] * _shift_up(dout, d)
        dcw_ref[k:k + 1, :] = _rows_sum(dout * _shift_down(v, d))
    return dv


def _col_spec(s, tc, part=0):
    off = part * (D_MODEL // tc)
    return pl.BlockSpec((s, tc), lambda c: (0, c + off))


def _small_spec(rows, tc):
    return pl.BlockSpec((rows, tc), lambda c: (0, c))


def _col_call(name, body, in_arrays, in_specs, out_rows, s, tc):
    return pl.pallas_call(
        body, name=name, grid=(D_MODEL // tc,), in_specs=in_specs,
        out_specs=[pl.BlockSpec((r, tc), lambda c: (0, c)) for r, _ in out_rows],
        out_shape=[jax.ShapeDtypeStruct((r, D_MODEL), dt) for r, dt in out_rows],
        compiler_params=_cparams(("arbitrary",)),
    )(*in_arrays)


def _sconv_fwd(proj, conv_w, name):
    s = proj.shape[0]
    tc = COL_TILE

    def body(b_ref, c_ref, x_ref, cw_ref, y_ref):
        y_ref[...] = (b_ref[...] * _conv_taps(c_ref[...] * x_ref[...], cw_ref, 3)).astype(BF16)

    return _col_call(name, body, [proj, proj, proj, conv_w],
                     [_col_spec(s, tc, 0), _col_spec(s, tc, 1), _col_spec(s, tc, 2), _small_spec(3, tc)],
                     [(s, BF16)], s, tc)[0]


def _sconv_bwd(dy, proj, conv_w, name):
    s = proj.shape[0]
    tc = COL_TILE

    def body(dy_ref, b_ref, c_ref, x_ref, cw_ref, db_ref, dc_ref, dx_ref, dcw_ref):
        w = c_ref[...] * x_ref[...]
        dy_v = dy_ref[...]
        db_ref[...] = (dy_v * _conv_taps(w, cw_ref, 3)).astype(BF16)
        dw = _conv_taps_bwd(dy_v * b_ref[...], w, cw_ref, dcw_ref, 3)
        dc_ref[...] = (dw * x_ref[...]).astype(BF16)
        dx_ref[...] = (dw * c_ref[...]).astype(BF16)

    return _col_call(name, body, [dy, proj, proj, proj, conv_w],
                     [_col_spec(s, tc), _col_spec(s, tc, 0), _col_spec(s, tc, 1), _col_spec(s, tc, 2),
                      _small_spec(3, tc)],
                     [(s, BF16), (s, BF16), (s, BF16), (3, F32)], s, tc)


def _lru_conv(proj, conv_w, conv_b, name):
    s = proj.shape[0]
    tc = COL_TILE

    def body(x_ref, cw_ref, cb_ref, xb_ref, xbb_ref):
        xb = _conv_taps(x_ref[...], cw_ref, 4) + cb_ref[...]
        xb_ref[...] = xb
        xbb_ref[...] = xb.astype(BF16)

    return _col_call(name, body, [proj, conv_w, conv_b],
                     [_col_spec(s, tc, 1), _small_spec(4, tc), _small_spec(1, tc)],
                     [(s, F32), (s, BF16)], s, tc)


def _lru_conv_bwd(dxb1, dxb2, proj, conv_w, name):
    s = proj.shape[0]
    tc = COL_TILE

    def body(d1_ref, d2_ref, x_ref, cw_ref, dx_ref, dcw_ref, dcb_ref):
        dxb = d1_ref[...] + d2_ref[...]
        dcb_ref[...] = _rows_sum(dxb)
        dx_ref[...] = _conv_taps_bwd(dxb, x_ref[...], cw_ref, dcw_ref, 4).astype(BF16)

    return _col_call(name, body, [dxb1, dxb2, proj, conv_w],
                     [_col_spec(s, tc), _col_spec(s, tc), _col_spec(s, tc, 1), _small_spec(4, tc)],
                     [(s, BF16), (4, F32), (1, F32)], s, tc)


_GELU_C = math.sqrt(2.0 / math.pi)


def _gelu_parts(g):
    inner = _GELU_C * (g + 0.044715 * g * g * g)
    th = jnp.tanh(inner)
    val = 0.5 * g * (1.0 + th)
    der = 0.5 * (1.0 + th) + 0.5 * g * (1.0 - th * th) * (_GELU_C * (1.0 + 3.0 * 0.044715 * g * g))
    return val, der


def _lru_gates(pa_ref, px_ref, ba_ref, bx_ref, lam_ref):
    r = _sigmoid(pa_ref[...] + ba_ref[...])
    ig = _sigmoid(px_ref[...] + bx_ref[...])
    sp = _softplus(-lam_ref[...])
    log_a = (-LRU_C) * r * sp
    a = jnp.exp(log_a)
    z = 2.0 * log_a
    one_m_a2 = jnp.where(z > -1e-3, -(z * (1.0 + z * (0.5 + z * (1.0 / 6.0)))), 1.0 - jnp.exp(z))
    return r, ig, sp, a, jnp.sqrt(one_m_a2)


def _lru_scan(pre, xb, proj, b_a, b_x, lam, name):
    s = xb.shape[0]
    tc = COL_TILE

    def body(pa_ref, px_ref, xb_ref, g_ref, ba_ref, bx_ref, lam_ref, y_ref, hs_ref):
        _, ig, _, a, mult = _lru_gates(pa_ref, px_ref, ba_ref, bx_ref, lam_ref)
        b = mult * (ig * xb_ref[...])
        d = 1
        while d < s:
            row = lax.broadcasted_iota(jnp.int32, a.shape, 0)
            keep = row >= d
            b = b + a * jnp.where(keep, pltpu.roll(b, d, 0), 0.0)
            a = a * jnp.where(keep, pltpu.roll(a, d, 0), 1.0)
            d *= 2
        hs_ref[...] = b
        y_ref[...] = (b * _gelu_parts(g_ref[...])[0]).astype(BF16)

    return _col_call(name, body, [pre, pre, xb, proj, b_a, b_x, lam],
                     [_col_spec(s, tc, 0), _col_spec(s, tc, 1), _col_spec(s, tc), _col_spec(s, tc, 0),
                      _small_spec(1, tc), _small_spec(1, tc), _small_spec(1, tc)],
                     [(s, BF16), (s, F32)], s, tc)


def _lru_scan_bwd(dy, pre, xb, proj, hs, b_a, b_x, lam, name):
    s = xb.shape[0]
    tc = COL_TILE

    def body(dy_ref, pa_ref, px_ref, xb_ref, g_ref, hs_ref, ba_ref, bx_ref, lam_ref,
             dg_ref, dpa_ref, dpx_ref, dxb_ref, dba_ref, dbx_ref, dlam_ref):
        r, ig, sp, a, mult = _lru_gates(pa_ref, px_ref, ba_ref, bx_ref, lam_ref)
        gl, gl_der = _gelu_parts(g_ref[...])
        dy_v = dy_ref[...]
        hs_v = hs_ref[...]
        dg_ref[...] = (dy_v * hs_v * gl_der).astype(BF16)
        lam_t = dy_v * gl
        coef = _shift_up(a, 1)
        d = 1
        while d < s:
            row = lax.broadcasted_iota(jnp.int32, coef.shape, 0)
            keep = row < s - d
            lam_t = lam_t + coef * jnp.where(keep, pltpu.roll(lam_t, s - d, 0), 0.0)
            coef = coef * jnp.where(keep, pltpu.roll(coef, s - d, 0), 1.0)
            d *= 2
        xb_v = xb_ref[...]
        da = lam_t * _shift_down(hs_v, 1)
        dmult = lam_t * (ig * xb_v)
        dig = lam_t * mult * xb_v
        dxb_ref[...] = lam_t * mult * ig
        dlog_a = da * a - dmult * (a * a) / mult
        dr = dlog_a * ((-LRU_C) * sp)
        dsp = _rows_sum(dlog_a * ((-LRU_C) * r))
        dlam_ref[...] = -dsp * _sigmoid(-lam_ref[...])
        dpa = dr * r * (1.0 - r)
        dpx = dig * ig * (1.0 - ig)
        dba_ref[...] = _rows_sum(dpa)
        dbx_ref[...] = _rows_sum(dpx)
        dpa_ref[...] = dpa.astype(BF16)
        dpx_ref[...] = dpx.astype(BF16)

    return _col_call(name, body, [dy, pre, pre, xb, proj, hs, b_a, b_x, lam],
                     [_col_spec(s, tc), _col_spec(s, tc, 0), _col_spec(s, tc, 1), _col_spec(s, tc),
                      _col_spec(s, tc, 0), _col_spec(s, tc),
                      _small_spec(1, tc), _small_spec(1, tc), _small_spec(1, tc)],
                     [(s, BF16), (s, BF16), (s, BF16), (s, F32), (1, F32), (1, F32), (1, F32)], s, tc)


def _ffn_fwd(x, w_in, w_out, g_pre, g_post, shift, scale, gate, tag, after=None):
    h = _pre_norm(x, g_pre, scale, shift, tag + "_pre", after=after)
    g, u, a = _ffn_in_act(h, w_in, tag + "_in")
    y = _mm_nn(a, w_out, tag + "_out", tn=512)
    xo = _post_norm(x, y, g_post, gate, 0.5, tag + "_post")
    return xo, (x, h, g, u, a, y)


def _ffn_bwd(dxo, saved, w_in, w_out, g_pre, g_post, scale, gate, tag, after=None):
    x, h, g, u, a, y = saved
    dy, dgate, dg_post = _post_norm_bwd(dxo, y, g_post, gate, 0.5, tag + "_post_b", after=after)
    dw_out = _mm_tn(a, dy, tag + "_out_bw", tn=512)
    dgu = _ffn_out_bx_act(dy, w_out, g, u, tag + "_out_bx")
    dh, dw_in = _ffn_in_bwd(dgu, h, w_in, tag + "_in_b")
    dx, dshift, dscale, dg_pre = _pre_norm_bwd(dxo, dh, x, g_pre, scale, tag + "_pre_b")
    return dx, dw_in, dw_out, (dshift, dscale, dgate), dg_pre, dg_post


def _pair_rows(v):
    return v.T.reshape(HEAD_PAIRS, 2, v.shape[0])


def _fox_fwd(h, p, tag):
    s = h.shape[0]
    proj = _mm_nn(h, p["w_in"], tag + "_in", tn=640)
    qkv = proj[:, :3 * D_MODEL].astype(BF16)
    flt = proj[:, 3 * D_MODEL:3 * D_MODEL + FOX_HEADS].T
    cum_t = _fox_gate(flt, p["b_f"], tag + "_gate")
    cum = cum_t.T
    cum_t2 = cum_t.reshape(HEAD_PAIRS, 2, s)
    o, ob, lse = _fox_attn_fwd(qkv, cum, cum_t2, tag + "_attn")
    y = _mm_nn(ob, p["w_out"], tag + "_out")
    return y, (qkv, flt, cum, cum_t2, o, ob, lse)


def _fox_bwd(dy, h, saved, p, tag):
    qkv, flt, cum, cum_t2, o, ob, lse = saved
    s = h.shape[0]
    do = _mm_nt(dy, p["w_out"], tag + "_out_bx")
    dw_out = _mm_tn(ob, dy, tag + "_out_bw")
    expand = jnp.pad(jnp.repeat(jnp.eye(FOX_HEADS, dtype=BF16), FOX_HEAD_DIM, axis=0),
                     ((0, 0), (0, PAIR_W - FOX_HEADS)))
    delta = _fox_delta(do, o, expand, tag + "_attn_delta")
    dq, dk, dv, dck, dcq = _fox_attn_bwd(qkv, do, cum, cum_t2, _pair_rows(lse), _pair_rows(delta), tag + "_attn_b")
    dcum_k = dck[:, :, :2].transpose(0, 2, 1).reshape(FOX_HEADS, s)
    dflt, db_f = _fox_gate_bwd(dcq.reshape(FOX_HEADS, s), dcum_k, flt, p["b_f"], tag + "_gate_b")
    dproj = jnp.concatenate(
        [dq, dk, dv, dflt.T, jnp.zeros((s, FOX_PAD - 3 * D_MODEL - FOX_HEADS), F32)], axis=1).astype(BF16)
    dh = _mm_nt(dproj, p["w_in"], tag + "_in_bx", tn=640)
    dw_in = _mm_tn(h, dproj, tag + "_in_bw", tn=640)
    return dh, {"w_in": dw_in, "w_out": dw_out, "b_f": db_f}


def _sconv_mix_fwd(h, p, tag):
    proj = _mm_nn(h, p["w_in"], tag + "_in")
    yb = _sconv_fwd(proj, p["conv_w"], tag + "_conv")
    y = _mm_nn(yb, p["w_out"], tag + "_out")
    return y, (proj, yb)


def _sconv_mix_bwd(dy, h, saved, p, tag):
    proj, yb = saved
    dyb = _mm_nt(dy, p["w_out"], tag + "_out_bx")
    dw_out = _mm_tn(yb, dy, tag + "_out_bw")
    db, dc, dxv, dcw = _sconv_bwd(dyb, proj, p["conv_w"], tag + "_conv_b")
    dproj = jnp.concatenate([db, dc, dxv], axis=1)
    dh = _mm_nt(dproj, p["w_in"], tag + "_in_bx")
    dw_in = _mm_tn(h, dproj, tag + "_in_bw", tn=p["w_in"].arr.shape[-1], blocked_out=True)
    return dh, {"w_in": dw_in, "w_out": dw_out, "conv_w": dcw}


def _lru_mix_fwd(h, p, tag):
    proj = _mm_nn(h, p["w_in"], tag + "_in")
    xb, xbb = _lru_conv(proj, p["conv_w"], p["conv_b"], tag + "_conv")
    pre = _mm_nn(xbb, p["w_ax"], tag + "_gates", tn=D_MODEL)
    yb, hs = _lru_scan(pre, xb, proj, p["b_a"], p["b_x"], p["lam"], tag + "_scan")
    y = _mm_nn(yb, p["w_out"], tag + "_out")
    return y, (proj, xb, xbb, pre, yb, hs)


def _diag_blocks(m):
    return jnp.stack([m[LRU_BLOCK_DIM * n:LRU_BLOCK_DIM * (n + 1), LRU_BLOCK_DIM * n:LRU_BLOCK_DIM * (n + 1)]
                      for n in range(LRU_BLOCKS)])


def _lru_mix_bwd(dy, h, saved, p, tag):
    proj, xb, xbb, pre, yb, hs = saved
    dyb = _mm_nt(dy, p["w_out"], tag + "_out_bx")
    dw_out = _mm_tn(yb, dy, tag + "_out_bw")
    dg, dpa, dpx, dxb1, dba, dbx, dlam = _lru_scan_bwd(dyb, pre, xb, proj, hs, p["b_a"], p["b_x"], p["lam"],
                                                       tag + "_scan_b")
    dpre = jnp.concatenate([dpa, dpx], axis=1)
    dxb2 = _mm_nt(dpre, p["w_ax"], tag + "_gates_bx", tn=D_MODEL)
    dw_ax = _mm_tn(xbb, dpre, tag + "_gates_bw", tn=D_MODEL)
    dx0, dcw, dcb = _lru_conv_bwd(dxb1, dxb2, proj, p["conv_w"], tag + "_conv_b")
    dproj = jnp.concatenate([dg, dx0], axis=1)
    dh = _mm_nt(dproj, p["w_in"], tag + "_in_bx")
    dw_in = _mm_tn(h, dproj, tag + "_in_bw", tn=p["w_in"].arr.shape[-1], blocked_out=True)
    grads = {"w_in": dw_in, "w_out": dw_out, "conv_w": dcw, "conv_b": dcb,
             "w_a": _diag_blocks(dw_ax[:, :D_MODEL]), "w_x": _diag_blocks(dw_ax[:, D_MODEL:]),
             "b_a": dba, "b_x": dbx, "lam": dlam}
    return dh, grads


_MIXERS = ((_fox_fwd, _fox_bwd), (_sconv_mix_fwd, _sconv_mix_bwd), (_lru_mix_fwd, _lru_mix_bwd))


def _local_step(x, target, mod, layer_params, on_grads=None, on_mid=None, first_after=None):
    layers = []
    tape = []
    for i in range(DEPTH):
        lp = dict(layer_params(i, 0, x))
        layers.append(lp)
        row = lambda v: v[None, :]
        m = lambda sub, what: mod[i, sub, what][None, :]
        x, sv0 = _ffn_fwd(x, lp["ffn_in"][0], lp["ffn_out"][0], row(lp["norm_pre"][0]), row(lp["norm_post"][0]),
                          m(0, 0), m(0, 1), m(0, 2), f"l{i}_ffn0", after=first_after if i == 0 else None)
        lp.update(layer_params(i, 1, x))
        h = _pre_norm(x, row(lp["norm_pre"][1]), m(1, 1), m(1, 0), f"l{i}_mix_pre")
        y, svm = _MIXERS[i % 3][0](h, lp["mixer"], f"l{i}_mix")
        x1 = _post_norm(x, y, row(lp["norm_post"][1]), m(1, 2), 1.0, f"l{i}_mix_post")
        second = layer_params(i, 2, x1)
        lp["ffn_in"] = lp["ffn_in"] + second["ffn_in"]
        lp["ffn_out"] = lp["ffn_out"] + second["ffn_out"]
        x2, sv2 = _ffn_fwd(x1, lp["ffn_in"][1], lp["ffn_out"][1], row(lp["norm_pre"][2]), row(lp["norm_post"][2]),
                           m(2, 0), m(2, 1), m(2, 2), f"l{i}_ffn1", after=second.get("after"))
        tape.append((sv0, (x, h, y, svm), sv2))
        x = x2
    dx, loss_row = _loss_head(x, target, "loss_head")

    layer_grads = [None] * DEPTH
    dmod = [None] * DEPTH
    after = None
    for i in reversed(range(DEPTH)):
        lp = layers[i]
        row = lambda v: v[None, :]
        m = lambda sub, what: mod[i, sub, what][None, :]
        sv0, (xm, h, y, svm), sv2 = tape[i]
        dx, dw_in1, dw_out1, dm2, dgp2, dgq2 = _ffn_bwd(dx, sv2, lp["ffn_in"][1], lp["ffn_out"][1],
                                                        row(lp["norm_pre"][2]), row(lp["norm_post"][2]),
                                                        m(2, 1), m(2, 2), f"l{i}_ffn1", after=after)
        after = on_mid(i, dx) if on_mid is not None else None
        dy, dgate1, dgq1 = _post_norm_bwd(dx, y, row(lp["norm_post"][1]), m(1, 2), 1.0, f"l{i}_mix_post_b", after=after)
        dh, mg = _MIXERS[i % 3][1](dy, h, svm, lp["mixer"], f"l{i}_mix")
        dx, dshift1, dscale1, dgp1 = _pre_norm_bwd(dx, dh, xm, row(lp["norm_pre"][1]), m(1, 1), f"l{i}_mix_pre_b")
        dx, dw_in0, dw_out0, dm0, dgp0, dgq0 = _ffn_bwd(dx, sv0, lp["ffn_in"][0], lp["ffn_out"][0],
                                                        row(lp["norm_pre"][0]), row(lp["norm_post"][0]),
                                                        m(0, 1), m(0, 2), f"l{i}_ffn0")
        dmod[i] = jnp.concatenate([*dm0, dshift1, dscale1, dgate1, *dm2], axis=0).reshape(N_SUB, 3, D_MODEL)
        layer_grads[i] = {"ffn_in": (dw_in0, dw_in1), "ffn_out": (dw_out0, dw_out1),
                          "norm_pre": jnp.concatenate([dgp0, dgp1, dgp2], axis=0),
                          "norm_post": jnp.concatenate([dgq0, dgq1, dgq2], axis=0), "mixer": mg}
        if on_grads is not None:
            after = on_grads(i, layer_grads[i], dx)
    return loss_row, dx, jnp.stack(dmod), layer_grads


COND_ROWS = 16
COND_PAD = 128


def _cond_fwd(c_pad, w_cond, b_shard, name):
    nl, d, n = w_cond.shape
    tn = 768

    def body(c_ref, w_ref, b_ref, o_ref):
        cv = c_ref[...]
        act = (cv * _sigmoid(cv)).astype(BF16)
        o_ref[...] = jnp.dot(act, w_ref[...].astype(BF16), preferred_element_type=F32) + b_ref[...]

    return pl.pallas_call(
        body, name=name, grid=(nl, n // tn),
        in_specs=[pl.BlockSpec((COND_ROWS, d), lambda i, j: (0, 0)),
                  pl.BlockSpec((None, d, tn), lambda i, j: (i, 0, j)),
                  pl.BlockSpec((None, 1, tn), lambda i, j: (i, 0, j))],
        out_specs=pl.BlockSpec((None, COND_ROWS, tn), lambda i, j: (i, 0, j)),
        out_shape=jax.ShapeDtypeStruct((nl, COND_ROWS, n), F32),
        compiler_params=_cparams(("arbitrary", "arbitrary")),
    )(c_pad, w_cond, b_shard)


def _adam_math(w, g, m, v):
    nm = ADAM_B1 * m + (1.0 - ADAM_B1) * g
    nv = ADAM_B2 * v + (1.0 - ADAM_B2) * (g * g)
    m_hat = nm / (1.0 - ADAM_B1 ** ADAM_STEP)
    v_hat = nv / (1.0 - ADAM_B2 ** ADAM_STEP)
    delta = (-ADAM_LR) * (m_hat / (jnp.sqrt(v_hat) + ADAM_EPS) + ADAM_WD * w)
    return delta, nm, nv


def _cond_bwd_adamw(c_t, dmod_s, w, m, v, name):
    nl, d, n = w.shape
    tn = 384
    blk = pl.BlockSpec((None, d, tn), lambda i, j: (i, 0, j))

    def body(c_ref, dm_ref, w_ref, m_ref, v_ref, g_ref, d_ref, nm_ref, nv_ref):
        cv = c_ref[...]
        g = jnp.dot((cv * _sigmoid(cv)).astype(BF16), dm_ref[...], preferred_element_type=F32)
        g_ref[...] = g
        d_ref[...], nm_ref[...], nv_ref[...] = _adam_math(w_ref[...], g, m_ref[...], v_ref[...])

    return pl.pallas_call(
        body, name=name, grid=(nl, n // tn),
        in_specs=[pl.BlockSpec((d, COND_PAD), lambda i, j: (0, 0)),
                  pl.BlockSpec((None, COND_PAD, tn), lambda i, j: (i, 0, j)), blk, blk, blk],
        out_specs=[blk] * 4, out_shape=[jax.ShapeDtypeStruct(w.shape, F32)] * 4,
        compiler_params=_cparams(("arbitrary", "arbitrary")),
    )(c_t, dmod_s, w, m, v)


def _adamw(w, g, m, v, name):
    rows, cols = w.shape
    tr = next(t for t in (256, 176, 128, 64, 32, 16, 8) if rows % t == 0)
    blk = pl.BlockSpec((tr, cols), lambda i: (i, 0))

    def body(w_ref, g_ref, m_ref, v_ref, d_ref, nm_ref, nv_ref):
        d_ref[...], nm_ref[...], nv_ref[...] = _adam_math(w_ref[...], g_ref[...], m_ref[...], v_ref[...])

    return pl.pallas_call(
        body, name=name, grid=(rows // tr,), in_specs=[blk] * 4, out_specs=[blk] * 3,
        out_shape=[jax.ShapeDtypeStruct(w.shape, F32)] * 3, compiler_params=_cparams(("arbitrary",)),
    )(w, g, m, v)


def _adamw_rows(w, g, m, v, outs, row0, nrows, name):
    cols = w.shape[1]
    tr = next(t for t in (512, 256, 128, 64, 32, 16, 8) if nrows % t == 0 and row0 % t == 0)
    blk = pl.BlockSpec((tr, cols), lambda i: (i + row0 // tr, 0))
    anywhere = pl.BlockSpec(memory_space=pl.ANY)

    def body(w_ref, g_ref, m_ref, v_ref, d_in, nm_in, nv_in, d_ref, nm_ref, nv_ref, g_out):
        d_ref[...], nm_ref[...], nv_ref[...] = _adam_math(w_ref[...], g_ref[...], m_ref[...], v_ref[...])

    return pl.pallas_call(
        body, name=name, grid=(nrows // tr,), in_specs=[blk] * 4 + [anywhere] * 3,
        out_specs=[blk] * 3 + [anywhere], out_shape=[jax.ShapeDtypeStruct(w.shape, F32)] * 4,
        input_output_aliases={4: 0, 5: 1, 6: 2, 1: 3}, compiler_params=_cparams(("arbitrary",)),
    )(w, g, m, v, *outs)


_MESH = pl.DeviceIdType.MESH
_ANY = pl.BlockSpec(memory_space=pl.ANY)


def _place():
    return lax.axis_index("x"), lax.axis_index("y"), lax.axis_index("c")


def _other_chips(x, y):
    return [(1 - x, y), (x, 1 - y), (1 - x, 1 - y)]


def _allgather8(block, name):
    m_per, n = block.shape

    def body(x_ref, out_ref, send_sems, recv_sems, local_sem):
        x, y, c = _place()
        me, sibling = (x, y, c), (x, y, 1 - c)
        chips = _other_chips(x, y)

        def rows(px, py, pc):
            return out_ref.at[pl.ds((4 * px + 2 * py + pc) * m_per, m_per), :]

        def copy(k, blk, to, src=None):
            return pltpu.make_async_remote_copy(
                src_ref=rows(*blk) if src is None else src, dst_ref=rows(*blk),
                send_sem=send_sems.at[k], recv_sem=recv_sems.at[k], device_id=to, device_id_type=_MESH)

        mine = pltpu.make_async_copy(x_ref, rows(*me), local_sem)
        mine.start()
        first = [copy(0, me, sibling, src=x_ref)]
        first += [copy(1 + j, me, (*chip, c), src=x_ref) for j, chip in enumerate(chips)]
        for cp in first:
            cp.start()
        passed = [copy(4 + j, (*chip, c), sibling) for j, chip in enumerate(chips)]
        for j, chip in enumerate(chips):
            copy(1 + j, (*chip, c), me).wait_recv()
            passed[j].start()
        copy(0, sibling, me).wait_recv()
        for j, chip in enumerate(chips):
            copy(4 + j, (*chip, 1 - c), me).wait_recv()
        for cp in first + passed:
            cp.wait_send()
        mine.wait()

    return pl.pallas_call(
        body, name=name, out_shape=jax.ShapeDtypeStruct((N_DEV * m_per, n), block.dtype),
        in_specs=[pl.BlockSpec(memory_space=pltpu.VMEM)], out_specs=pl.BlockSpec(memory_space=pltpu.VMEM),
        scratch_shapes=[pltpu.SemaphoreType.DMA((7,)), pltpu.SemaphoreType.DMA((7,)), pltpu.SemaphoreType.DMA],
        compiler_params=_cparams(),
    )(block)


def _split_axis(shape):
    return next(a for a, n in enumerate(shape) if n > 1)


_HBM = pl.BlockSpec(memory_space=pltpu.HBM)
_SEM = pl.BlockSpec(memory_space=pltpu.SEMAPHORE)
_SPLIT_COPY = pltpu.CompilerParams(has_side_effects=pltpu.SideEffectType.DATAFLOW_SIDE_EFFECTING)
_TOKEN = jax.ShapeDtypeStruct((8, 128), F32)


def _in_hbm(arrays):
    return [pltpu.with_memory_space_constraint(a, pltpu.HBM) for a in arrays]


class _Gathered(NamedTuple):
    shard_shape: tuple
    chip_axis: int

    @property
    def shape(self):
        return self.shard_shape[:self.chip_axis] + (N_CHIPS,) + self.shard_shape[self.chip_axis:]

    def half(self, ref, chip, pc):
        cut = _split_axis(self.shard_shape)
        n = self.shard_shape[cut] // 2
        idx = [slice(None)] * len(self.shard_shape)
        idx[cut] = pl.ds(pc * n, n)
        idx.insert(self.chip_axis, chip)
        return ref.at[tuple(idx)]


def _own_block_placed(shard, layout, chip):
    return lax.dynamic_update_slice_in_dim(lax.empty(layout.shape, shard.dtype),
                                           jnp.expand_dims(shard, layout.chip_axis), chip, axis=layout.chip_axis)


def _gather_copies(lands, layouts, send_sems, recv_sems):
    x, y, c = _place()
    out = []
    for t, (land, lay) in enumerate(zip(lands, layouts)):
        for j, (px, py) in enumerate(_other_chips(x, y)):
            def copy(chip, t=t, j=j, px=px, py=py, land=land, lay=lay):
                return pltpu.make_async_remote_copy(
                    src_ref=lay.half(land, chip, c), dst_ref=lay.half(land, chip, c),
                    send_sem=send_sems.at[3 * t + j], recv_sem=recv_sems.at[3 * t + j],
                    device_id=(px, py, c), device_id_type=_MESH)
            out.append((copy(2 * x + y), copy(2 * px + py)))
    return out


def _gather_start(lands, layouts, after, name):
    nt = len(lands)
    order = [] if after is None else [after]

    def body(*refs):
        land_refs = refs[:nt]
        send_sems, recv_sems = refs[nt + len(order):nt + len(order) + 2]
        token = refs[-1]
        for send, _ in _gather_copies(land_refs, layouts, send_sems, recv_sems):
            send.start()
        token[...] = jnp.zeros_like(token)

    out = pl.pallas_call(
        body, name=name,
        out_shape=(pltpu.SemaphoreType.DMA((3 * nt,)), pltpu.SemaphoreType.DMA((3 * nt,)),
                   *[pltpu.HBM(a.shape, a.dtype) for a in lands], _TOKEN),
        in_specs=[_HBM] * nt + [_ANY] * len(order),
        out_specs=(_SEM, _SEM, *[_HBM] * nt, pl.BlockSpec(memory_space=pltpu.VMEM)),
        input_output_aliases={t: 2 + t for t in range(nt)}, compiler_params=_SPLIT_COPY,
    )(*_in_hbm(lands), *order)
    return out[0], out[1], list(out[2:2 + nt]), out[-1]


def _gather_wait(send_sems, recv_sems, lands, layouts, after, name):
    nt = len(lands)

    def body(*refs):
        land_refs = refs[:nt]
        sems = refs[nt:nt + 2]
        for send, arrival in _gather_copies(land_refs, layouts, *sems):
            send.wait_send()
            arrival.wait_recv()

    return list(pl.pallas_call(
        body, name=name, out_shape=tuple(pltpu.HBM(a.shape, a.dtype) for a in lands),
        in_specs=[_HBM] * nt + [_SEM, _SEM, _ANY], out_specs=tuple([_HBM] * nt),
        input_output_aliases={t: t for t in range(nt)}, compiler_params=_SPLIT_COPY,
    )(*lands, send_sems, recv_sems, after))


def _gather_forward(lands, layouts, name):
    nt = len(lands)

    def body(*refs):
        outs = refs[nt:2 * nt]
        send_sems, recv_sems = refs[2 * nt:]
        x, y, c = _place()
        sends, arrivals = [], []
        for t, lay in enumerate(layouts):
            for j, (px, py) in enumerate(_other_chips(x, y)):
                for pc, group in ((c, sends), (1 - c, arrivals)):
                    part = lay.half(outs[t], 2 * px + py, pc)
                    group.append(pltpu.make_async_remote_copy(
                        src_ref=part, dst_ref=part, send_sem=send_sems.at[3 * t + j], recv_sem=recv_sems.at[3 * t + j],
                        device_id=(x, y, 1 - c), device_id_type=_MESH))
        for cp in sends:
            cp.start()
        for cp in arrivals:
            cp.wait_recv()
        for cp in sends:
            cp.wait_send()

    return list(pl.pallas_call(
        body, name=name, out_shape=[jax.ShapeDtypeStruct(a.shape, a.dtype) for a in lands],
        in_specs=[_ANY] * nt, out_specs=[_ANY] * nt, input_output_aliases={t: t for t in range(nt)},
        scratch_shapes=[pltpu.SemaphoreType.DMA((3 * nt,)), pltpu.SemaphoreType.DMA((3 * nt,))],
        compiler_params=_cparams(),
    )(*lands))


def _forward_copies(lands, layouts, send_sems, recv_sems):
    x, y, c = _place()
    out = []
    for t, (land, lay) in enumerate(zip(lands, layouts)):
        for j, (px, py) in enumerate(_other_chips(x, y)):
            def copy(pc, t=t, j=j, px=px, py=py, land=land, lay=lay):
                part = lay.half(land, 2 * px + py, pc)
                return pltpu.make_async_remote_copy(
                    src_ref=part, dst_ref=part, send_sem=send_sems.at[3 * t + j], recv_sem=recv_sems.at[3 * t + j],
                    device_id=(x, y, 1 - c), device_id_type=_MESH)
            out.append((copy(c), copy(1 - c)))
    return out


def _gather_forward_start(lands, layouts, name):
    nt = len(lands)

    def body(*refs):
        for send, _ in _forward_copies(refs[:nt], layouts, refs[nt], refs[nt + 1]):
            send.start()
        refs[-1][...] = jnp.zeros_like(refs[-1])

    out = pl.pallas_call(
        body, name=name,
        out_shape=(pltpu.SemaphoreType.DMA((3 * nt,)), pltpu.SemaphoreType.DMA((3 * nt,)),
                   *[pltpu.HBM(a.shape, a.dtype) for a in lands], _TOKEN),
        in_specs=[_HBM] * nt, out_specs=(_SEM, _SEM, *[_HBM] * nt, pl.BlockSpec(memory_space=pltpu.VMEM)),
        input_output_aliases={t: 2 + t for t in range(nt)}, compiler_params=_SPLIT_COPY,
    )(*_in_hbm(lands))
    return out[0], out[1], list(out[2:2 + nt]), out[-1]


def _gather_forward_wait(send_sems, recv_sems, lands, layouts, after, name):
    nt = len(lands)

    def body(*refs):
        for send, arrival in _forward_copies(refs[:nt], layouts, refs[nt], refs[nt + 1]):
            send.wait_send()
            arrival.wait_recv()

    return list(pl.pallas_call(
        body, name=name, out_shape=tuple(pltpu.HBM(a.shape, a.dtype) for a in lands),
        in_specs=[_HBM] * nt + [_SEM, _SEM, _ANY], out_specs=tuple([_HBM] * nt),
        input_output_aliases={t: t for t in range(nt)}, compiler_params=_SPLIT_COPY,
    )(*lands, send_sems, recv_sems, after))


def _pair_copies(grads, lands, send_sems, recv_sems):
    x, y, c = _place()
    out = []
    for t, (g, land) in enumerate(zip(grads, lands)):
        h = g.shape[1] // 2
        out.append(pltpu.make_async_remote_copy(
            src_ref=g.at[:, pl.ds((1 - c) * h, h), :], dst_ref=land, send_sem=send_sems.at[t],
            recv_sem=recv_sems.at[t], device_id=(x, y, 1 - c), device_id_type=_MESH))
    return out


def _pair_start(grads, after, name):
    nt = len(grads)
    lands = [lax.empty((N_CHIPS, g.shape[1] // 2, g.shape[2]), g.dtype) for g in grads]
    order = [] if after is None else [after]

    def body(*refs):
        send_sems, recv_sems = refs[2 * nt + len(order):2 * nt + len(order) + 2]
        token = refs[-1]
        for cp in _pair_copies(refs[:nt], refs[nt:2 * nt], send_sems, recv_sems):
            cp.start()
        token[...] = jnp.zeros_like(token)

    out = pl.pallas_call(
        body, name=name,
        out_shape=(pltpu.SemaphoreType.DMA((nt,)), pltpu.SemaphoreType.DMA((nt,)),
                   *[pltpu.HBM(a.shape, a.dtype) for a in grads + lands], _TOKEN),
        in_specs=[_HBM] * (2 * nt) + [_ANY] * len(order),
        out_specs=(_SEM, _SEM, *[_HBM] * (2 * nt), pl.BlockSpec(memory_space=pltpu.VMEM)),
        input_output_aliases={t: 2 + t for t in range(2 * nt)}, compiler_params=_SPLIT_COPY,
    )(*_in_hbm(grads + lands), *order)
    return out[0], out[1], list(out[2:2 + nt]), list(out[2 + nt:2 + 2 * nt]), out[-1]


def _pair_wait(send_sems, recv_sems, grads, lands, after, name):
    nt = len(grads)

    def body(*refs):
        for cp in _pair_copies(refs[:nt], refs[nt:2 * nt], *refs[2 * nt:2 * nt + 2]):
            cp.wait_send()
            cp.wait_recv()

    out = pl.pallas_call(
        body, name=name, out_shape=tuple(pltpu.HBM(a.shape, a.dtype) for a in grads + lands),
        in_specs=[_HBM] * (2 * nt) + [_SEM, _SEM, _ANY], out_specs=tuple([_HBM] * (2 * nt)),
        input_output_aliases={t: t for t in range(2 * nt)}, compiler_params=_SPLIT_COPY,
    )(*grads, *lands, send_sems, recv_sems, after)
    return list(out[:nt]), list(out[nt:])


def _pair_sum(own, recv, c_idx, name):
    _, h, cols = recv.shape

    def body(c_ref, own_ref, recv_ref, o_ref):
        o_ref[...] = (own_ref[...] + recv_ref[...]).astype(BF16)

    return pl.pallas_call(
        body, name=name,
        grid_spec=pltpu.PrefetchScalarGridSpec(
            num_scalar_prefetch=1, grid=(N_CHIPS,),
            in_specs=[pl.BlockSpec((None, h, cols), lambda k, c_ref: (k, c_ref[0], 0)),
                      pl.BlockSpec((None, h, cols), lambda k, c_ref: (k, 0, 0))],
            out_specs=pl.BlockSpec((None, h, cols), lambda k, c_ref: (k, 0, 0))),
        out_shape=jax.ShapeDtypeStruct(recv.shape, BF16), compiler_params=_cparams(("arbitrary",)),
    )(c_idx, own, recv)


def _chip_copies(parts, lands, send_sems, recv_sems):
    x, y, c = _place()
    out = []
    for t, (part, land) in enumerate(zip(parts, lands)):
        for j, (px, py) in enumerate(_other_chips(x, y)):
            out.append(pltpu.make_async_remote_copy(
                src_ref=part.at[2 * px + py], dst_ref=land.at[j], send_sem=send_sems.at[3 * t + j],
                recv_sem=recv_sems.at[3 * t + j], device_id=(px, py, c), device_id_type=_MESH))
    return out


def _chip_send_start(parts, after, name):
    nt = len(parts)
    lands = [lax.empty((N_CHIPS - 1,) + p.shape[1:], p.dtype) for p in parts]
    order = [] if after is None else [after]

    def body(*refs):
        send_sems, recv_sems = refs[2 * nt + len(order):2 * nt + len(order) + 2]
        token = refs[-1]
        for cp in _chip_copies(refs[:nt], refs[nt:2 * nt], send_sems, recv_sems):
            cp.start()
        token[...] = jnp.zeros_like(token)

    out = pl.pallas_call(
        body, name=name,
        out_shape=(pltpu.SemaphoreType.DMA((3 * nt,)), pltpu.SemaphoreType.DMA((3 * nt,)),
                   *[pltpu.HBM(a.shape, a.dtype) for a in parts + lands], _TOKEN),
        in_specs=[_HBM] * (2 * nt) + [_ANY] * len(order),
        out_specs=(_SEM, _SEM, *[_HBM] * (2 * nt), pl.BlockSpec(memory_space=pltpu.VMEM)),
        input_output_aliases={t: 2 + t for t in range(2 * nt)}, compiler_params=_SPLIT_COPY,
    )(*_in_hbm(parts + lands), *order)
    return out[0], out[1], list(out[2:2 + nt]), list(out[2 + nt:2 + 2 * nt]), out[-1]


def _chip_send_wait(send_sems, recv_sems, parts, lands, after, name):
    nt = len(parts)

    def body(*refs):
        for cp in _chip_copies(refs[:nt], refs[nt:2 * nt], *refs[2 * nt:2 * nt + 2]):
            cp.wait_send()
            cp.wait_recv()

    out = pl.pallas_call(
        body, name=name, out_shape=tuple(pltpu.HBM(a.shape, a.dtype) for a in parts + lands),
        in_specs=[_HBM] * (2 * nt) + [_SEM, _SEM, _ANY], out_specs=tuple([_HBM] * (2 * nt)),
        input_output_aliases={t: t for t in range(2 * nt)}, compiler_params=_SPLIT_COPY,
    )(*parts, *lands, send_sems, recv_sems, after)
    return list(out[:nt]), list(out[nt:])


def _chip_sum(part, arrived, into, lead, place_idx, name):
    _, h, cols = part.shape

    def body(idx_ref, own_ref, arr_ref, into_ref, o_ref):
        acc = own_ref[...].astype(F32)
        for k in range(N_CHIPS - 1):
            acc = acc + arr_ref[k].astype(F32)
        o_ref[...] = acc

    return pl.pallas_call(
        body, name=name,
        grid_spec=pltpu.PrefetchScalarGridSpec(
            num_scalar_prefetch=1, grid=(1,),
            in_specs=[pl.BlockSpec((None, h, cols), lambda g, idx: (idx[1], 0, 0)),
                      pl.BlockSpec((N_CHIPS - 1, h, cols), lambda g, idx: (0, 0, 0)), _ANY],
            out_specs=pl.BlockSpec((None,) * len(lead) + (h, cols), lambda g, idx: (*lead, idx[0], 0))),
        out_shape=jax.ShapeDtypeStruct(into.shape, F32), input_output_aliases={3: 0},
        compiler_params=_cparams(("arbitrary",)),
    )(place_idx, part, arrived, into)


def _pair_gather(bufs, homes, name):
    nt, nb = len(homes), len(bufs)

    def body(*refs):
        outs = refs[nb:2 * nb]
        send_sems, recv_sems = refs[2 * nb:]
        x, y, c = _place()

        def home(t, pc):
            o, lead, rows = homes[t]
            return outs[o].at[(*lead, pl.ds(pc * (rows // 2), rows // 2), slice(None))]

        def copy(t, pc):
            return pltpu.make_async_remote_copy(src_ref=home(t, pc), dst_ref=home(t, pc), send_sem=send_sems.at[t],
                                                recv_sem=recv_sems.at[t], device_id=(x, y, 1 - c), device_id_type=_MESH)

        sends = [copy(t, c) for t in range(nt)]
        for cp in sends:
            cp.start()
        for t in range(nt):
            copy(t, 1 - c).wait_recv()
        for cp in sends:
            cp.wait_send()

    return pl.pallas_call(
        body, name=name, out_shape=[jax.ShapeDtypeStruct(b.shape, b.dtype) for b in bufs],
        in_specs=[_ANY] * nb, out_specs=[_ANY] * nb, input_output_aliases={o: o for o in range(nb)},
        scratch_shapes=[pltpu.SemaphoreType.DMA((nt,)), pltpu.SemaphoreType.DMA((nt,))],
        compiler_params=_cparams(),
    )(*bufs)


def _sum_devices(g, after, name):
    def body(g_ref, after_ref, o_ref):
        acc = g_ref[0:1, :]
        for d in range(1, N_DEV):
            acc = acc + g_ref[d:d + 1, :]
        o_ref[...] = acc
    vmem = pl.BlockSpec(memory_space=pltpu.VMEM)
    return pl.pallas_call(body, name=name, out_shape=jax.ShapeDtypeStruct((1, g.shape[1]), F32),
                          in_specs=[vmem, _ANY], out_specs=vmem, compiler_params=_cparams())(g, after)


_WEIGHTS = ("w_cond", "b_cond", "norm_pre", "norm_post", "w_ffn_in", "w_ffn_out", "fox_w_in", "fox_b_f",
            "fox_w_out", "sconv_w_in", "sconv_conv_w", "sconv_w_out", "lru_w_in", "lru_conv_w", "lru_conv_b",
            "lru_w_a", "lru_b_a", "lru_w_x", "lru_b_x", "lru_lambda", "lru_w_out")
_BIG = (("w_ffn_in", False), ("w_ffn_out", True), ("fox_w_in", False), ("fox_w_out", True),
        ("sconv_w_in", False), ("sconv_w_out", True), ("lru_w_in", False), ("lru_w_out", True))
_SMALL = tuple(n for n in _WEIGHTS if n != "w_cond" and n not in dict(_BIG))
_COL_SHARDED_SMALL = ("norm_pre", "norm_post", "sconv_conv_w", "lru_conv_w", "lru_conv_b", "lru_lambda")


def _pack_rows(parts, rows=8):
    flat = jnp.concatenate([p.reshape(-1) for p in parts])
    width = -(-flat.size // (rows * 128)) * 128
    return jnp.pad(flat, (0, rows * width - flat.size)).reshape(rows, width)


def _unpack(flat, shapes):
    out, off = [], 0
    for shp in shapes:
        n = math.prod(shp)
        out.append(flat[off:off + n].reshape(shp))
        off += n
    return out


def _join_chips(g):
    g = jnp.moveaxis(g, 0, -2)
    return g.reshape(g.shape[:-2] + (g.shape[-2] * g.shape[-1],))


def _my_columns(full, chip):
    n = full.shape[-1] // N_CHIPS
    return lax.dynamic_slice_in_dim(full, chip * n, n, axis=full.ndim - 1)


def _block_diag(w):
    eye = jnp.eye(LRU_BLOCKS, dtype=w.dtype)
    return jnp.einsum("nij,nm->nimj", w, eye).reshape(D_MODEL, D_MODEL)


def _step(x, c, target, wts, mom, var):
    ix, iy, ic = _place()
    chip = 2 * ix + iy
    dev = 2 * chip + ic
    n_cond = wts["w_cond"].shape[2]

    small_shapes = [(D_MODEL,)] + [wts[n].shape for n in _COL_SHARDED_SMALL]
    g1 = _allgather8(_pack_rows([c[0]] + [wts[n] for n in _COL_SHARDED_SMALL]), "gather_small").reshape(N_DEV, -1)
    c_all = g1[:, :D_MODEL]
    per_chip = [jnp.stack(col) for col in zip(*[_unpack(g1[2 * k], small_shapes) for k in range(N_CHIPS)])]
    small_full = {n: _join_chips(v) for n, v in zip(_COL_SHARDED_SMALL, per_chip[1:])}

    c_pad = jnp.pad(c_all, ((0, COND_ROWS - N_DEV), (0, 0)))
    b_shard = _my_columns(wts["b_cond"], chip)[:, None, :]
    mod_part = _cond_fwd(c_pad, wts["w_cond"], b_shard, "cond_fwd")
    g2 = _allgather8(mod_part[:, :N_DEV].transpose(1, 0, 2).reshape(N_DEV, DEPTH * n_cond), "gather_mod")
    g2 = g2.reshape(N_DEV, N_DEV, DEPTH, n_cond)[0::2]
    mod = _join_chips(lax.dynamic_index_in_dim(g2, dev, axis=1, keepdims=False)).reshape(DEPTH, N_SUB, 3, D_MODEL)

    mixer_names = [("fox_w_in", "fox_w_out"), ("sconv_w_in", "sconv_w_out"), ("lru_w_in", "lru_w_out")]

    def shards_of(i, sub):
        if sub == 1:
            return [wts[n][i // 3] for n in mixer_names[i % 3]]
        return [wts["w_ffn_in"][i, sub // 2], wts["w_ffn_out"][i, sub // 2]]

    chunks = [[(0, sub)] for sub in range(N_SUB)] + [[(i, sub) for sub in range(N_SUB)] for i in range(1, DEPTH)]
    in_flight, chunk_of, token = [], {}, mod
    for k, members in enumerate(chunks):
        shards = [s for i, sub in members for s in shards_of(i, sub)]
        layouts = [_Gathered(s.shape, 0) for s in shards]
        if k:
            shards = [s + token[0, 0] for s in shards]
        lands = [_own_block_placed(s.astype(BF16), lay, chip) for s, lay in zip(shards, layouts)]
        send_sems, recv_sems, lands, token = _gather_start(lands, layouts, token, f"gather_start_{k}")
        in_flight.append([send_sems, recv_sems, lands, layouts, False])
        chunk_of.update({m: (k, 2 * pos) for pos, m in enumerate(members)})
    lru_ax = jnp.concatenate([_block_diag(wts["lru_w_a"][0]), _block_diag(wts["lru_w_x"][0])], axis=1).astype(BF16)

    prefetch_at = {(i, N_SUB - 1): i + N_SUB for i in range(DEPTH - 1)}

    def layer_params(i, sub, x_in):
        k, pos = chunk_of[(i, sub)]
        send_sems, recv_sems, lands, layouts, state = in_flight[k]
        if state == "passing":
            in_flight[k][2:] = [_gather_forward_wait(send_sems, recv_sems, lands, layouts, x_in, f"gather_pass_wait_{k}"),
                                layouts, "here"]
        elif state != "here":
            lands = _gather_wait(send_sems, recv_sems, lands, layouts, x_in, f"gather_wait_{k}")
            in_flight[k][2:] = [_gather_forward(lands, layouts, f"gather_forward_{k}"), layouts, "here"]
        nxt = prefetch_at.get((i, sub))
        started = None
        if nxt is not None:
            send_sems, recv_sems, lands, layouts, _ = in_flight[nxt]
            lands = _gather_wait(send_sems, recv_sems, lands, layouts, x_in, f"gather_wait_{nxt}")
            send_sems, recv_sems, lands, started = _gather_forward_start(lands, layouts, f"gather_pass_start_{nxt}")
            in_flight[nxt] = [send_sems, recv_sems, lands, layouts, "passing"]
        w_in, w_out = in_flight[k][2][pos:pos + 2]
        w_out = w_out.reshape(-1, w_out.shape[-1])
        if sub != 1:
            out = {"ffn_in": [_W(w_in, (), True)], "ffn_out": [_W(w_out)], "after": started}
            if sub == 0:
                out.update(norm_pre=small_full["norm_pre"][i], norm_post=small_full["norm_post"][i])
            return out
        j = i // 3
        if i % 3 == 0:
            w_in = jnp.pad(_join_chips(w_in), ((0, 0), (0, FOX_PAD - 3 * D_MODEL - FOX_HEADS)))
            return {"mixer": {"w_in": _W(w_in), "w_out": _W(w_out), "b_f": wts["fox_b_f"][j][:, None]}}
        if i % 3 == 1:
            return {"mixer": {"w_in": _W(w_in, (), True), "w_out": _W(w_out), "conv_w": small_full["sconv_conv_w"][j]}}
        return {"mixer": {"w_in": _W(w_in, (), True), "w_out": _W(w_out), "conv_w": small_full["lru_conv_w"][j],
                          "conv_b": small_full["lru_conv_b"], "w_ax": _W(lru_ax),
                          "b_a": wts["lru_b_a"].reshape(1, D_MODEL), "b_x": wts["lru_b_x"].reshape(1, D_MODEL),
                          "lam": small_full["lru_lambda"]}}

    place_idx = jnp.stack([ic, chip]).astype(jnp.int32)
    c_idx = place_idx[:1]
    big_index = {n: o for o, (n, _) in enumerate(_BIG)}
    exchanges, pending = [], []

    def to_chips(after):
        i, send_sems, recv_sems, tensors, lands, homes = pending.pop()
        tensors, recv = _pair_wait(send_sems, recv_sems, tensors, lands, after, f"grads_pair_wait_l{i}")
        parts = [_pair_sum(t, r, c_idx, f"grads_pair_sum_l{i}_{k}") for k, (t, r) in enumerate(zip(tensors, recv))]
        send_sems, recv_sems, parts, lands, tok = _chip_send_start(parts, None, f"grads_chip_start_l{i}")
        exchanges.append((i, send_sems, recv_sems, parts, lands, homes))
        return tok

    def chip_blocks(g, by_rows, width):
        if by_rows:
            return g.reshape(N_CHIPS, g.shape[0] // N_CHIPS, g.shape[1])
        if g.ndim == 3:
            return g
        return g[:, :width * N_CHIPS].reshape(g.shape[0], N_CHIPS, width).transpose(1, 0, 2)

    def on_mid(i, dx):
        return to_chips(dx) if pending else None

    def on_grads(i, g, dx):
        n_in, n_out = mixer_names[i % 3]
        items = [("w_ffn_in", (i, k), g["ffn_in"][k]) for k in range(2)]
        items += [("w_ffn_out", (i, k), g["ffn_out"][k]) for k in range(2)]
        items += [(n_in, (i // 3,), g["mixer"]["w_in"]), (n_out, (i // 3,), g["mixer"]["w_out"])]
        tensors = [chip_blocks(t, dict(_BIG)[n], wts[n].shape[-1]) for n, _, t in items]
        homes = [(big_index[n], lead, wts[n].shape[-2]) for n, lead, _ in items]
        send_sems, recv_sems, tensors, lands, tok = _pair_start(tensors, None, f"grads_pair_start_l{i}")
        pending.append((i, send_sems, recv_sems, tensors, lands, homes))
        pair_tokens.append(tok)
        return tok

    pair_tokens = []
    loss_row, grad_x, dmod, lg = _local_step(x[0], target[0], mod, layer_params, on_grads, on_mid, token)
    loss = lax.psum(loss_row[0, 0], ("x", "y", "c"))
    dmod = dmod + pair_tokens[-1][0, 0]

    fox_layers = [i for i in range(DEPTH) if i % 3 == 0]
    sconv_g, lru_g = lg[1]["mixer"], lg[2]["mixer"]
    small_g = {
        "dmod": dmod, "norm_pre": jnp.stack([g["norm_pre"] for g in lg]), "norm_post": jnp.stack([g["norm_post"] for g in lg]),
        "fox_b_f": jnp.stack([lg[i]["mixer"]["b_f"][:, 0] for i in fox_layers]),
        "sconv_conv_w": sconv_g["conv_w"][None], "lru_conv_w": lru_g["conv_w"][None], "lru_conv_b": lru_g["conv_b"],
        "lru_w_a": lru_g["w_a"][None], "lru_b_a": lru_g["b_a"].reshape(1, LRU_BLOCKS, LRU_BLOCK_DIM),
        "lru_w_x": lru_g["w_x"][None], "lru_b_x": lru_g["b_x"].reshape(1, LRU_BLOCKS, LRU_BLOCK_DIM),
        "lru_lambda": lru_g["lam"]}
    g4 = _allgather8(_pack_rows(list(small_g.values())), "gather_small_grads").reshape(N_DEV, -1)
    last_start = to_chips(g4)
    summed = _sum_devices(g4, last_start, "sum_small_grads")[0]
    summed = dict(zip(small_g, _unpack(summed, [v.shape for v in small_g.values()])))
    grads = {n: (_my_columns(summed[n], chip) if n in _COL_SHARDED_SMALL else summed[n]) for n in _SMALL if n != "b_cond"}
    grads["b_cond"] = summed["dmod"].reshape(DEPTH, N_SUB * 3 * D_MODEL)

    dmod_all = (g4[:, :dmod.size] + last_start[0, 0]).reshape(N_DEV, DEPTH, N_SUB * 3 * D_MODEL)
    dmod_s = jnp.pad(_my_columns(dmod_all, chip).transpose(1, 0, 2), ((0, 0), (0, COND_PAD - N_DEV), (0, 0))).astype(BF16)
    c_t = jnp.pad(c_all.T, ((0, 0), (0, COND_PAD - N_DEV)))
    grads["w_cond"], d_cond, m_cond, v_cond = _cond_bwd_adamw(c_t, dmod_s, wts["w_cond"], mom["w_cond"],
                                                              var["w_cond"], "cond_bwd_adamw")

    big = [n for n, _ in _BIG]
    two_d = lambda a: a.reshape(-1, a.shape[-1])
    bufs = [lax.empty(wts[n].shape, F32) for n in big]
    updates = [[lax.empty(two_d(wts[n]).shape, F32) for _ in range(3)] for n in big]
    follows = d_cond
    for i, send_sems, recv_sems, parts, lands, homes in exchanges:
        parts, lands = _chip_send_wait(send_sems, recv_sems, parts, lands, follows, f"grads_chip_wait_l{i}")
        for k, (part, land, (o, lead, _)) in enumerate(zip(parts, lands, homes)):
            bufs[o] = _chip_sum(part, land, bufs[o], lead, place_idx, f"grads_chip_sum_l{i}_{k}")
        bufs = list(_pair_gather(bufs, homes, f"grads_pair_gather_l{i}"))
        for o in sorted({o for o, _, _ in homes}):
            n = big[o]
            starts = [sum(a * math.prod(wts[n].shape[d + 1:-1]) for d, a in enumerate(lead))
                      for oo, lead, _ in homes if oo == o]
            rows = wts[n].shape[-2]
            *updates[o], g_out = _adamw_rows(two_d(wts[n]), two_d(bufs[o]), two_d(mom[n]), two_d(var[n]), updates[o],
                                             min(starts), max(starts) + rows - min(starts), f"adamw_{n}_l{i}")
            bufs[o] = g_out.reshape(wts[n].shape)
        follows = updates[0][0]
    grads.update(zip(big, bufs))

    delta, new_m, new_v = {"w_cond": d_cond}, {"w_cond": m_cond}, {"w_cond": v_cond}
    for n, (d, nm, nv) in zip(big, updates):
        delta[n], new_m[n], new_v[n] = (a.reshape(wts[n].shape) for a in (d, nm, nv))
    shapes = [wts[n].shape for n in _SMALL]
    packed = [_pack_rows([src[n] for n in _SMALL]) for src in (wts, grads, mom, var)]
    for dst, out in zip((delta, new_m, new_v), _adamw(*packed, "adamw_small")):
        dst.update(zip(_SMALL, _unpack(out.reshape(-1), shapes)))

    return (loss, grad_x[None], *[grads[n] for n in _WEIGHTS], *[delta[n] for n in _WEIGHTS],
            *[new_m[n] for n in _WEIGHTS], *[new_v[n] for n in _WEIGHTS])


def kernel(x, c, w_cond, b_cond, norm_pre, norm_post, w_ffn_in, w_ffn_out, fox_w_in, fox_b_f, fox_w_out, sconv_w_in, sconv_conv_w, sconv_w_out, lru_w_in, lru_conv_w, lru_conv_b, lru_w_a, lru_b_a, lru_w_x, lru_b_x, lru_lambda, lru_w_out, loss_target, m_w_cond, m_b_cond, m_norm_pre, m_norm_post, m_w_ffn_in, m_w_ffn_out, m_fox_w_in, m_fox_b_f, m_fox_w_out, m_sconv_w_in, m_sconv_conv_w, m_sconv_w_out, m_lru_w_in, m_lru_conv_w, m_lru_conv_b, m_lru_w_a, m_lru_b_a, m_lru_w_x, m_lru_b_x, m_lru_lambda, m_lru_w_out, v_w_cond, v_b_cond, v_norm_pre, v_norm_post, v_w_ffn_in, v_w_ffn_out, v_fox_w_in, v_fox_b_f, v_fox_w_out, v_sconv_w_in, v_sconv_conv_w, v_sconv_w_out, v_lru_w_in, v_lru_conv_w, v_lru_conv_b, v_lru_w_a, v_lru_b_a, v_lru_w_x, v_lru_b_x, v_lru_lambda, v_lru_w_out):
    given = dict(locals())
    wts = {n: given[n] for n in _WEIGHTS}
    mom = {n: given["m_" + n] for n in _WEIGHTS}
    var = {n: given["v_" + n] for n in _WEIGHTS}
    return _step(x, c, loss_target, wts, mom, var)
```

```python
import functools
import math
from typing import NamedTuple

import jax
import jax.numpy as jnp
from jax import lax
from jax.experimental import pallas as pl
from jax.experimental.pallas import tpu as pltpu

F32 = jnp.float32
BF16 = jnp.bfloat16

D_MODEL = 1024
DEPTH = 4
N_SUB = 3
D_FF = 2816
RMS_EPS = 1e-6
FOX_HEADS = 16
FOX_HEAD_DIM = 64
FOX_PAD = 3200
LRU_BLOCKS = 16
LRU_BLOCK_DIM = 64
LRU_C = 8.0
N_CHIPS = 4
N_DEV = 8

ADAM_LR = 0.001
ADAM_B1 = 0.9
ADAM_B2 = 0.999
ADAM_EPS = 1e-08
ADAM_WD = 0.01
ADAM_STEP = 10

VMEM_LIMIT_V7X = 56 * 1024 * 1024
ROW_TILE = 512
COL_TILE = 256
ATT_TILE = 256
ATT_WIDE = 512
MM_ROWS = 1024


def _cparams(sem=None):
    return pltpu.CompilerParams(vmem_limit_bytes=VMEM_LIMIT_V7X, dimension_semantics=sem)


def _sigmoid(z):
    return 1.0 / (1.0 + jnp.exp(-z))


def _softplus(z):
    return jnp.maximum(z, 0.0) + jnp.log(1.0 + jnp.exp(-jnp.abs(z)))


def _rows_sum(v):
    return jnp.sum(v, axis=0, keepdims=True)


class _W(NamedTuple):
    arr: jax.Array
    prefix: tuple = ()
    blocked: bool = False


def _w_spec(w, block2, pos):
    lead = (None,) * (len(w.prefix) + (1 if w.blocked else 0))
    if w.blocked:
        return pl.BlockSpec(lead + block2, lambda *g: (pos(*g)[0], *w.prefix, pos(*g)[1], pos(*g)[2]))
    return pl.BlockSpec(lead + block2, lambda *g: (*w.prefix, pos(*g)[1], pos(*g)[2]))


def _mm_nn(a, b, name, tn=None):
    m, k = a.shape
    if b.blocked:
        steps, bn = b.arr.shape[0], b.arr.shape[-1]
        b_spec = _w_spec(b, (k, bn), lambda n: (n, 0, 0))
    else:
        n_total = b.arr.shape[-1]
        bn = n_total if tn is None else tn
        steps = n_total // bn
        assert steps * bn == n_total
        b_spec = _w_spec(b, (k, bn), lambda n: (0, 0, n))
    tm = min(MM_ROWS, m)

    def body(a_ref, b_ref, o_ref):
        def step(i, carry):
            r = pl.ds(pl.multiple_of(i * tm, tm), tm)
            o_ref[r, :] = jnp.dot(a_ref[r, :], b_ref[...], preferred_element_type=F32)
            return carry
        lax.fori_loop(0, m // tm, step, 0)

    return pl.pallas_call(
        body, name=name, grid=(steps,),
        in_specs=[pl.BlockSpec((m, k), lambda n: (0, 0)), b_spec],
        out_specs=pl.BlockSpec((m, bn), lambda n: (0, n)),
        out_shape=jax.ShapeDtypeStruct((m, steps * bn), F32),
        compiler_params=_cparams(("arbitrary",)),
    )(a, b.arr)


def _cols_shape(dy):
    return (dy.shape[0], dy.shape[1]) if dy.ndim == 2 else (dy.shape[1], 2 * dy.shape[2])


def _cols_spec(dy, bn):
    if dy.ndim == 2:
        return pl.BlockSpec((dy.shape[0], bn), lambda kt, n: (0, n))
    per = dy.shape[2] // bn
    assert per * bn == dy.shape[2]
    return pl.BlockSpec((None, dy.shape[1], bn), lambda kt, n: (n // per, 0, n % per))


def _mm_nt(dy, w, name, tk=None, tn=None):
    m, n_total = _cols_shape(dy)
    k = w.arr.shape[-2]
    if w.blocked:
        bk, bn = k, w.arr.shape[-1]
        grid = (1, w.arr.shape[0])
        w_spec = _w_spec(w, (k, bn), lambda kt, n: (n, 0, 0))
    else:
        bk = k if tk is None else tk
        bn = n_total if tn is None else tn
        grid = (k // bk, n_total // bn)
        assert grid[0] * bk == k and grid[1] * bn == n_total
        w_spec = _w_spec(w, (bk, bn), lambda kt, n: (0, kt, n))
    tm = min(MM_ROWS, m)

    reduce_steps = grid[1]

    def body(dy_ref, w_ref, o_ref):
        def step(i, carry):
            r = pl.ds(pl.multiple_of(i * tm, tm), tm)
            part = lax.dot_general(dy_ref[r, :], w_ref[...], (((1,), (1,)), ((), ())), preferred_element_type=F32)
            if reduce_steps == 1:
                o_ref[r, :] = part
            else:
                o_ref[r, :] += part
            return carry

        if reduce_steps > 1:
            @pl.when(pl.program_id(1) == 0)
            def _():
                o_ref[...] = jnp.zeros_like(o_ref)
        lax.fori_loop(0, m // tm, step, 0)

    return pl.pallas_call(
        body, name=name, grid=grid,
        in_specs=[_cols_spec(dy, bn), w_spec],
        out_specs=pl.BlockSpec((m, bk), lambda kt, n: (0, kt)),
        out_shape=jax.ShapeDtypeStruct((m, k), F32),
        compiler_params=_cparams(("arbitrary", "arbitrary")),
    )(dy, w.arr)


def _mm_tn(x, dy, name, tk=None, tn=None, blocked_out=False):
    s, k = x.shape
    n_total = _cols_shape(dy)[1]
    bk = k if tk is None else tk
    bn = n_total if tn is None else tn
    grid = (k // bk, n_total // bn)
    assert grid[0] * bk == k and grid[1] * bn == n_total
    ck = next(c for c in (512, 256, 128) if bk % c == 0)

    def body(x_ref, dy_ref, o_ref):
        def step(i, carry):
            c = pl.ds(pl.multiple_of(i * ck, ck), ck)
            o_ref[c, :] = lax.dot_general(x_ref[:, c], dy_ref[...], (((0,), (0,)), ((), ())),
                                          preferred_element_type=F32)
            return carry
        lax.fori_loop(0, bk // ck, step, 0)

    if blocked_out:
        assert grid[0] == 1
        out_spec = pl.BlockSpec((None, bk, bn), lambda kt, n: (n, 0, 0))
        out_shape = jax.ShapeDtypeStruct((grid[1], k, bn), F32)
    else:
        out_spec = pl.BlockSpec((bk, bn), lambda kt, n: (kt, n))
        out_shape = jax.ShapeDtypeStruct((k, n_total), F32)
    return pl.pallas_call(
        body, name=name, grid=grid,
        in_specs=[pl.BlockSpec((s, bk), lambda kt, n: (0, kt)), _cols_spec(dy, bn)],
        out_specs=out_spec, out_shape=out_shape,
        compiler_params=_cparams(("arbitrary", "arbitrary")),
    )(x, dy)


def _row_call(name, body, rows, fulls, row_outs, acc_outs, tr=ROW_TILE, after=None):
    s = rows[0].shape[0]
    tr = min(tr, s)
    in_specs = [pl.BlockSpec((tr, a.shape[1]), lambda i: (i, 0)) for a in rows]
    in_specs += [pl.BlockSpec(a.shape, lambda i: (0, 0)) for a in fulls]
    n_in = len(in_specs)
    order = [] if after is None else [after]
    in_specs += [pl.BlockSpec(memory_space=pl.ANY)] * len(order)
    out_specs = [pl.BlockSpec((tr, c), lambda i: (i, 0)) for c, _ in row_outs]
    out_specs += [pl.BlockSpec((1, c), lambda i: (0, 0)) for c, _ in acc_outs]
    out_shape = [jax.ShapeDtypeStruct((s, c), dt) for c, dt in row_outs]
    out_shape += [jax.ShapeDtypeStruct((1, c), dt) for c, dt in acc_outs]
    n_acc = len(acc_outs)

    def wrapped(*refs):
        refs = refs[:n_in] + refs[n_in + len(order):]
        if n_acc:
            @pl.when(pl.program_id(0) == 0)
            def _():
                for r in refs[len(refs) - n_acc:]:
                    r[...] = jnp.zeros_like(r)
        body(*refs)

    return pl.pallas_call(
        wrapped, name=name, grid=(s // tr,), in_specs=in_specs, out_specs=out_specs, out_shape=out_shape,
        compiler_params=_cparams(("arbitrary",)),
    )(*rows, *fulls, *order)


def _rms(v):
    return lax.rsqrt(jnp.mean(v * v, axis=-1, keepdims=True) + RMS_EPS)


def _pre_norm(x, g_pre, scale, shift, name, after=None):
    def body(x_ref, g_ref, sc_ref, sh_ref, h_ref):
        xv = x_ref[...]
        h = (xv * _rms(xv)) * g_ref[...] * (1.0 + sc_ref[...]) + sh_ref[...]
        h_ref[...] = h.astype(BF16)
    return _row_call(name, body, [x], [g_pre, scale, shift], [(D_MODEL, BF16)], [], after=after)[0]


def _post_norm(x, y, g_post, gate, coef, name):
    def body(x_ref, y_ref, g_ref, gate_ref, o_ref):
        yv = y_ref[...]
        o_ref[...] = x_ref[...] + (coef * gate_ref[...]) * ((yv * _rms(yv)) * g_ref[...])
    return _row_call(name, body, [x, y], [g_post, gate], [(D_MODEL, F32)], [])[0]


def _post_norm_bwd(dxo, y, g_post, gate, coef, name, after=None):
    def body(dxo_ref, y_ref, g_ref, gate_ref, dy_ref, dgate_ref, dg_ref):
        yv = y_ref[...]
        r2 = _rms(yv)
        yn = yv * r2
        dxo_v = dxo_ref[...]
        dgate_ref[...] += _rows_sum(dxo_v * (yn * g_ref[...])) * coef
        dz = dxo_v * (coef * gate_ref[...])
        dg_ref[...] += _rows_sum(dz * yn)
        dyn = dz * g_ref[...]
        dy = r2 * (dyn - yn * jnp.mean(dyn * yn, axis=-1, keepdims=True))
        dy_ref[...] = dy.astype(BF16)
    return _row_call(name, body, [dxo, y], [g_post, gate], [(D_MODEL, BF16)], [(D_MODEL, F32), (D_MODEL, F32)],
                     after=after)


def _pre_norm_bwd(dxo, dh, x, g_pre, scale, name):
    def body(dxo_ref, dh_ref, x_ref, g_ref, sc_ref, dx_ref, dshift_ref, dscale_ref, dg_ref):
        xv = x_ref[...]
        r = _rms(xv)
        xn = xv * r
        dh_v = dh_ref[...]
        one_sc = 1.0 + sc_ref[...]
        dshift_ref[...] += _rows_sum(dh_v)
        dscale_ref[...] += _rows_sum(dh_v * (xn * g_ref[...]))
        dg_ref[...] += _rows_sum(dh_v * xn * one_sc)
        dxn = dh_v * (g_ref[...] * one_sc)
        dx_ref[...] = dxo_ref[...] + r * (dxn - xn * jnp.mean(dxn * xn, axis=-1, keepdims=True))
    return _row_call(name, body, [dxo, dh, x], [g_pre, scale], [(D_MODEL, F32)],
                     [(D_MODEL, F32), (D_MODEL, F32), (D_MODEL, F32)])


FFN_COLS = 1408


def _ffn_in_act(h, w_in, name, after=None):
    m, k = h.shape
    half, bn = w_in.arr.shape[0] // 2, w_in.arr.shape[-1]
    assert bn == FFN_COLS and half * bn == D_FF
    tm = min(MM_ROWS, m)
    order = [] if after is None else [after]

    def body(h_ref, wg_ref, wu_ref, *rest):
        g_ref, u_ref, a_ref = rest[len(order):]
        g = jnp.dot(h_ref[...], wg_ref[...], preferred_element_type=F32)
        g_ref[...] = g
        u = jnp.dot(h_ref[...], wu_ref[...], preferred_element_type=F32)
        u_ref[...] = u
        a_ref[...] = (g * _sigmoid(g) * u).astype(BF16)

    tile = pl.BlockSpec((tm, bn), lambda t, i: (i, t))
    return pl.pallas_call(
        body, name=name, grid=(half, m // tm),
        in_specs=[pl.BlockSpec((tm, k), lambda t, i: (i, 0)),
                  _w_spec(w_in, (k, bn), lambda t, i: (t, 0, 0)),
                  _w_spec(w_in, (k, bn), lambda t, i: (half + t, 0, 0))] + [pl.BlockSpec(memory_space=pl.ANY)] * len(order),
        out_specs=[tile, tile, tile],
        out_shape=[jax.ShapeDtypeStruct((m, D_FF), F32)] * 2 + [jax.ShapeDtypeStruct((m, D_FF), BF16)],
        compiler_params=_cparams(("arbitrary", "arbitrary")),
    )(h, w_in.arr, w_in.arr, *order)


def _ffn_out_bx_act(dy, w_out, g, u, name):
    m = dy.shape[0]
    tr = min(MM_ROWS, m)

    def body(dy_ref, w_ref, g_ref, u_ref, dgu_ref):
        da = lax.dot_general(dy_ref[...], w_ref[...], _NT, preferred_element_type=F32)
        gv = g_ref[...]
        sg = _sigmoid(gv)
        dgu_ref[0] = (da * u_ref[...] * (sg * (1.0 + gv * (1.0 - sg)))).astype(BF16)
        dgu_ref[1] = (da * (gv * sg)).astype(BF16)

    tile = pl.BlockSpec((tr, FFN_COLS), lambda i, c: (i, c))
    return pl.pallas_call(
        body, name=name, grid=(m // tr, D_FF // FFN_COLS),
        in_specs=[pl.BlockSpec((tr, D_MODEL), lambda i, c: (i, 0)),
                  _w_spec(w_out, (FFN_COLS, D_MODEL), lambda i, c: (0, c, 0)), tile, tile],
        out_specs=pl.BlockSpec((2, tr, FFN_COLS), lambda i, c: (0, i, c)),
        out_shape=jax.ShapeDtypeStruct((2, m, D_FF), BF16),
        compiler_params=_cparams(("arbitrary", "arbitrary")),
    )(dy, w_out.arr, g, u)


def _ffn_in_bwd(dgu, h, w_in, name):
    m, k = h.shape
    nb, bn = w_in.arr.shape[0], w_in.arr.shape[-1]
    per = dgu.shape[2] // bn
    tm = min(MM_ROWS, m)
    ck = next(c for c in (512, 256, 128) if k % c == 0)
    once = pl.Buffered(1)

    def body(dgu_ref, h_ref, w_ref, dh_ref, dw_ref):
        @pl.when(pl.program_id(0) == 0)
        def _():
            dh_ref[...] = jnp.zeros_like(dh_ref)

        def rows(i, carry):
            r = pl.ds(pl.multiple_of(i * tm, tm), tm)
            dh_ref[r, :] += lax.dot_general(dgu_ref[r, :], w_ref[...], _NT, preferred_element_type=F32)
            return carry
        lax.fori_loop(0, m // tm, rows, 0)

        def cols(i, carry):
            c = pl.ds(pl.multiple_of(i * ck, ck), ck)
            dw_ref[c, :] = lax.dot_general(h_ref[:, c], dgu_ref[...], (((0,), (0,)), ((), ())),
                                           preferred_element_type=F32)
            return carry
        lax.fori_loop(0, k // ck, cols, 0)

    return pl.pallas_call(
        body, name=name, grid=(nb,),
        in_specs=[pl.BlockSpec((None, m, bn), lambda n: (n // per, 0, n % per)),
                  pl.BlockSpec((m, k), lambda n: (0, 0), pipeline_mode=once),
                  _w_spec(w_in, (k, bn), lambda n: (n, 0, 0))],
        out_specs=[pl.BlockSpec((m, k), lambda n: (0, 0), pipeline_mode=once),
                   pl.BlockSpec((None, k, bn), lambda n: (n, 0, 0))],
        out_shape=[jax.ShapeDtypeStruct((m, k), F32), jax.ShapeDtypeStruct((nb, k, bn), F32)],
        compiler_params=_cparams(("arbitrary",)),
    )(dgu, h, w_in.arr)


def _loss_head(y, target, name):
    def body(y_ref, t_ref, dy_ref, loss_ref):
        e = y_ref[...] - t_ref[...]
        dy_ref[...] = e * (1.0 / D_MODEL)
        part = jnp.sum(jnp.mean(e * e, axis=-1, keepdims=True), axis=0, keepdims=True) * 0.5
        loss_ref[...] += jnp.broadcast_to(part, loss_ref.shape)
    return _row_call(name, body, [y, target], [], [(D_MODEL, F32)], [(128, F32)])


def _lane_scan(v, reverse):
    s = v.shape[1]
    lane = lax.broadcasted_iota(jnp.int32, v.shape, 1)
    d = 1
    while d < s:
        if reverse:
            v = v + jnp.where(lane < s - d, pltpu.roll(v, s - d, 1), 0.0)
        else:
            v = v + jnp.where(lane >= d, pltpu.roll(v, d, 1), 0.0)
        d *= 2
    return v


def _fox_gate(flt, b_f, name):
    def body(f_ref, b_ref, cum_ref):
        z = f_ref[...] + b_ref[...]
        cum_ref[...] = _lane_scan(-_softplus(-z), reverse=False)
    return pl.pallas_call(body, name=name, out_shape=jax.ShapeDtypeStruct(flt.shape, F32),
                          compiler_params=_cparams())(flt, b_f)


def _fox_gate_bwd(dcum_q, dcum_k, flt, b_f, name):
    def body(dq_ref, dk_ref, f_ref, b_ref, df_ref, db_ref):
        z = f_ref[...] + b_ref[...]
        df = _lane_scan(dq_ref[...] + dk_ref[...], reverse=True) * _sigmoid(-z)
        df_ref[...] = df
        db_ref[...] = jnp.sum(df, axis=1, keepdims=True)
    h = flt.shape[0]
    return pl.pallas_call(body, name=name,
                          out_shape=(jax.ShapeDtypeStruct(flt.shape, F32), jax.ShapeDtypeStruct((h, 1), F32)),
                          compiler_params=_cparams())(dcum_q, dcum_k, flt, b_f)


def _pick_head(block, h):
    lane = lax.broadcasted_iota(jnp.int32, block.shape, 1)
    return jnp.sum(jnp.where(lane == h, block, 0.0), axis=1, keepdims=True)


def _put_head(ref, col, h):
    @pl.when(h == 0)
    def _():
        ref[...] = jnp.zeros_like(ref)
    lane = lax.broadcasted_iota(jnp.int32, ref.shape, 1)
    ref[...] = jnp.where(lane == h, col, ref[...])


_NT = (((1,), (1,)), ((), ()))
_FOX_SCALE = FOX_HEAD_DIM ** -0.5


HEAD_PAIRS = FOX_HEADS // 2
PAIR_W = 2 * FOX_HEAD_DIM


def _low_half(shape):
    return lax.broadcasted_iota(jnp.int32, shape, 1) < FOX_HEAD_DIM


def _fox_attn_fwd(qkv, cum, cum_t, name):
    s = qkv.shape[0]
    t = min(ATT_TILE, s)
    wide = min(ATT_WIDE, s)

    def body(q_ref, k_ref, v_ref, cum_ref, cumt_ref, o_ref, ob_ref, lse_ref):
        i = pl.program_id(0)
        hp = pl.program_id(1)
        lo = _low_half((t, PAIR_W))
        qv = q_ref[...]
        zero = jnp.zeros_like(qv)
        q2 = (jnp.where(lo, qv, zero), jnp.where(lo, zero, qv))
        cum_v = cum_ref[...]
        cq2 = (_pick_head(cum_v, 2 * hp), _pick_head(cum_v, 2 * hp + 1))

        def step(j, carry, masked):
            ks = pl.ds(pl.multiple_of(j * wide, wide), wide)
            kj = k_ref[ks, :]
            vj = v_ref[ks, :]
            out = []
            for e in range(2):
                m, l, acc = carry[e]
                sc = lax.dot_general(q2[e], kj, _NT, preferred_element_type=F32) * _FOX_SCALE
                sc = sc + cq2[e] - cumt_ref[e:e + 1, ks]
                if masked:
                    q_pos = i * t + lax.broadcasted_iota(jnp.int32, (t, wide), 0)
                    k_pos = j * wide + lax.broadcasted_iota(jnp.int32, (t, wide), 1)
                    sc = jnp.where(k_pos <= q_pos, sc, -jnp.inf)
                m_new = jnp.maximum(m, jnp.max(sc, axis=1, keepdims=True))
                alpha = jnp.exp(m - m_new)
                p = jnp.exp(sc - m_new)
                l = alpha * l + jnp.sum(p, axis=1, keepdims=True)
                acc = alpha * acc + jnp.dot(p.astype(BF16), vj, preferred_element_type=F32)
                out.append((m_new, l, acc))
            return tuple(out)

        one = (jnp.full((t, 1), -jnp.inf, F32), jnp.zeros((t, 1), F32), jnp.zeros((t, PAIR_W), F32))
        whole = (i * t) // wide
        carry = lax.fori_loop(0, whole, lambda j, c: step(j, c, False), (one, one))
        (m0, l0, a0), (m1, l1, a1) = step(whole, carry, True)
        o = jnp.where(lo, a0 / l0, a1 / l1)
        o_ref[...] = o
        ob_ref[...] = o.astype(BF16)
        _put_head(lse_ref, m0 + jnp.log(l0), 2 * hp)
        _put_head(lse_ref, m1 + jnp.log(l1), 2 * hp + 1)

    nat_tile = pl.BlockSpec((t, FOX_HEADS), lambda i, hp: (i, 0))
    out_tile = pl.BlockSpec((t, PAIR_W), lambda i, hp: (i, hp))
    return pl.pallas_call(
        body, name=name, grid=(s // t, HEAD_PAIRS),
        in_specs=[pl.BlockSpec((t, PAIR_W), lambda i, hp: (i, hp)),
                  pl.BlockSpec((s, PAIR_W), lambda i, hp: (0, HEAD_PAIRS + hp)),
                  pl.BlockSpec((s, PAIR_W), lambda i, hp: (0, 2 * HEAD_PAIRS + hp)),
                  nat_tile, pl.BlockSpec((None, 2, s), lambda i, hp: (hp, 0, 0))],
        out_specs=[out_tile, out_tile, nat_tile],
        out_shape=[jax.ShapeDtypeStruct((s, D_MODEL), F32), jax.ShapeDtypeStruct((s, D_MODEL), BF16),
                   jax.ShapeDtypeStruct((s, FOX_HEADS), F32)],
        compiler_params=_cparams(("arbitrary", "arbitrary")),
    )(qkv, qkv, qkv, cum, cum_t)


def _fox_delta(do, o, expand, name):
    def body(do_ref, o_ref, e_ref, d_ref):
        prod = do_ref[...] * o_ref[...]
        hi = prod.astype(BF16)
        lo = (prod - hi.astype(F32)).astype(BF16)
        tot = (jnp.dot(hi, e_ref[...], preferred_element_type=F32)
               + jnp.dot(lo, e_ref[...], preferred_element_type=F32))
        d_ref[...] = tot[:, :FOX_HEADS]
    return _row_call(name, body, [do, o], [expand], [(FOX_HEADS, F32)], [])[0]


def _fox_attn_bwd(qkv, do, cum, cum_t, lse_t, delta_t, name):
    s = qkv.shape[0]
    t = min(ATT_TILE, s)
    wide = min(ATT_WIDE, s)
    nq = s // t
    tn_dims = (((0,), (0,)), ((), ()))

    def body(q_ref, k_ref, v_ref, do_ref, cum_ref, cumt_ref, lset_ref, deltat_ref,
             dq_ref, dk_ref, dv_ref, dck_ref, dcq_ref):
        hp = pl.program_id(0)
        j = pl.program_id(1)

        @pl.when(j == 0)
        def _():
            dq_ref[...] = jnp.zeros_like(dq_ref)
            dcq_ref[...] = jnp.zeros_like(dcq_ref)
        dk_ref[...] = jnp.zeros_like(dk_ref)
        dv_ref[...] = jnp.zeros_like(dv_ref)

        lo = _low_half((t, PAIR_W))
        lane = lax.broadcasted_iota(jnp.int32, (t, PAIR_W), 1)
        kv = k_ref[...]
        vv = v_ref[...]
        zero = jnp.zeros_like(kv)
        k2 = (jnp.where(lo, kv, zero), jnp.where(lo, zero, kv))
        v2 = (jnp.where(lo, vv, zero), jnp.where(lo, zero, vv))
        cum_v = cum_ref[...]
        ck2 = (_pick_head(cum_v, 2 * hp), _pick_head(cum_v, 2 * hp + 1))

        def step(i, dck, masked):
            qs = pl.ds(pl.multiple_of(i * wide, wide), wide)
            qi = q_ref[qs, :]
            do_i = do_ref[qs, :].astype(BF16)
            dv_p, dk_p, dq_p = [], [], []
            for e in range(2):
                st = lax.dot_general(k2[e], qi, _NT, preferred_element_type=F32) * _FOX_SCALE
                st = st + cumt_ref[e:e + 1, qs] - ck2[e]
                if masked:
                    k_pos = j * t + lax.broadcasted_iota(jnp.int32, (t, wide), 0)
                    q_pos = i * wide + lax.broadcasted_iota(jnp.int32, (t, wide), 1)
                    st = jnp.where(k_pos <= q_pos, st, -jnp.inf)
                pt = jnp.exp(st - lset_ref[e:e + 1, qs])
                dv_p.append(jnp.dot(pt.astype(BF16), do_i, preferred_element_type=F32))
                dpt = lax.dot_general(v2[e], do_i, _NT, preferred_element_type=F32)
                dst = pt * (dpt - deltat_ref[e:e + 1, qs])
                dsb = dst.astype(BF16)
                dk_p.append(jnp.dot(dsb, qi, preferred_element_type=F32))
                dq_p.append(lax.dot_general(dsb, kv, tn_dims, preferred_element_type=F32))
                dck = dck - jnp.where(lane == e, jnp.sum(dst, axis=1, keepdims=True), 0.0)
                dcq_ref[e:e + 1, qs] += jnp.sum(dst, axis=0, keepdims=True)
            dv_ref[...] += jnp.where(lo, dv_p[0], dv_p[1])
            dk_ref[...] += jnp.where(lo, dk_p[0], dk_p[1])
            dq_ref[qs, :] += jnp.where(_low_half((wide, PAIR_W)), dq_p[0], dq_p[1]) * _FOX_SCALE
            return dck

        first = (j * t) // wide
        dck = step(first, jnp.zeros((t, PAIR_W), F32), True)
        dck = lax.fori_loop(first + 1, s // wide, lambda i, c: step(i, c, False), dck)
        dk_ref[...] = dk_ref[...] * _FOX_SCALE
        dck_ref[...] = dck

    pair_full = lambda part: pl.BlockSpec((s, PAIR_W), lambda hp, j: (0, part * HEAD_PAIRS + hp))
    pair_tile = lambda part: pl.BlockSpec((t, PAIR_W), lambda hp, j: (j, part * HEAD_PAIRS + hp))
    rows = pl.BlockSpec((None, 2, s), lambda hp, j: (hp, 0, 0))
    return pl.pallas_call(
        body, name=name, grid=(HEAD_PAIRS, nq),
        in_specs=[pair_full(0), pair_tile(1), pair_tile(2), pair_full(0),
                  pl.BlockSpec((t, FOX_HEADS), lambda hp, j: (j, 0)), rows, rows, rows],
        out_specs=[pair_full(0), pair_tile(0), pair_tile(0),
                   pl.BlockSpec((None, t, PAIR_W), lambda hp, j: (hp, j, 0)), rows],
        out_shape=[jax.ShapeDtypeStruct((s, D_MODEL), F32)] * 3
        + [jax.ShapeDtypeStruct((HEAD_PAIRS, s, PAIR_W), F32), jax.ShapeDtypeStruct((HEAD_PAIRS, 2, s), F32)],
        compiler_params=_cparams(("arbitrary", "arbitrary")),
    )(qkv, qkv, qkv, do, cum, cum_t, lse_t, delta_t)


def _shift_down(v, d):
    row = lax.broadcasted_iota(jnp.int32, v.shape, 0)
    return jnp.where(row >= d, pltpu.roll(v, d, 0), 0.0)


def _shift_up(v, d):
    s = v.shape[0]
    row = lax.broadcasted_iota(jnp.int32, v.shape, 0)
    return jnp.where(row < s - d, pltpu.roll(v, s - d, 0), 0.0)


def _conv_taps(v, cw_ref, width):
    out = cw_ref[width - 1:width, :] * v
    for k in range(width - 1):
        out = out + cw_ref[k:k + 1, :] * _shift_down(v, width - 1 - k)
    return out


def _conv_taps_bwd(dout, v, cw_ref, dcw_ref, width):
    dv = cw_ref[width - 1:width, :] * dout
    dcw_ref[width - 1:width, :] = _rows_sum(dout * v)
    for k in range(width - 1):
        d = width - 1 - k
        dv = dv + cw_ref[k:k + 1, :] * _shift_up(dout, d)
        dcw_ref[k:k + 1, :] = _rows_sum(dout * _shift_down(v, d))
    return dv


def _col_spec(s, tc, part=0):
    off = part * (D_MODEL // tc)
    return pl.BlockSpec((s, tc), lambda c: (0, c + off))


def _small_spec(rows, tc):
    return pl.BlockSpec((rows, tc), lambda c: (0, c))


def _col_call(name, body, in_arrays, in_specs, out_rows, s, tc):
    return pl.pallas_call(
        body, name=name, grid=(D_MODEL // tc,), in_specs=in_specs,
        out_specs=[pl.BlockSpec((r, tc), lambda c: (0, c)) for r, _ in out_rows],
        out_shape=[jax.ShapeDtypeStruct((r, D_MODEL), dt) for r, dt in out_rows],
        compiler_params=_cparams(("arbitrary",)),
    )(*in_arrays)


def _sconv_fwd(proj, conv_w, name):
    s = proj.shape[0]
    tc = COL_TILE

    def body(b_ref, c_ref, x_ref, cw_ref, y_ref):
        y_ref[...] = (b_ref[...] * _conv_taps(c_ref[...] * x_ref[...], cw_ref, 3)).astype(BF16)

    return _col_call(name, body, [proj, proj, proj, conv_w],
                     [_col_spec(s, tc, 0), _col_spec(s, tc, 1), _col_spec(s, tc, 2), _small_spec(3, tc)],
                     [(s, BF16)], s, tc)[0]


def _sconv_bwd(dy, proj, conv_w, name):
    s = proj.shape[0]
    tc = COL_TILE

    def body(dy_ref, b_ref, c_ref, x_ref, cw_ref, db_ref, dc_ref, dx_ref, dcw_ref):
        w = c_ref[...] * x_ref[...]
        dy_v = dy_ref[...]
        db_ref[...] = (dy_v * _conv_taps(w, cw_ref, 3)).astype(BF16)
        dw = _conv_taps_bwd(dy_v * b_ref[...], w, cw_ref, dcw_ref, 3)
        dc_ref[...] = (dw * x_ref[...]).astype(BF16)
        dx_ref[...] = (dw * c_ref[...]).astype(BF16)

    return _col_call(name, body, [dy, proj, proj, proj, conv_w],
                     [_col_spec(s, tc), _col_spec(s, tc, 0), _col_spec(s, tc, 1), _col_spec(s, tc, 2),
                      _small_spec(3, tc)],
                     [(s, BF16), (s, BF16), (s, BF16), (3, F32)], s, tc)


def _lru_conv(proj, conv_w, conv_b, name):
    s = proj.shape[0]
    tc = COL_TILE

    def body(x_ref, cw_ref, cb_ref, xb_ref, xbb_ref):
        xb = _conv_taps(x_ref[...], cw_ref, 4) + cb_ref[...]
        xb_ref[...] = xb
        xbb_ref[...] = xb.astype(BF16)

    return _col_call(name, body, [proj, conv_w, conv_b],
                     [_col_spec(s, tc, 1), _small_spec(4, tc), _small_spec(1, tc)],
                     [(s, F32), (s, BF16)], s, tc)


def _lru_conv_bwd(dxb1, dxb2, proj, conv_w, name):
    s = proj.shape[0]
    tc = COL_TILE

    def body(d1_ref, d2_ref, x_ref, cw_ref, dx_ref, dcw_ref, dcb_ref):
        dxb = d1_ref[...] + d2_ref[...]
        dcb_ref[...] = _rows_sum(dxb)
        dx_ref[...] = _conv_taps_bwd(dxb, x_ref[...], cw_ref, dcw_ref, 4).astype(BF16)

    return _col_call(name, body, [dxb1, dxb2, proj, conv_w],
                     [_col_spec(s, tc), _col_spec(s, tc), _col_spec(s, tc, 1), _small_spec(4, tc)],
                     [(s, BF16), (4, F32), (1, F32)], s, tc)


_GELU_C = math.sqrt(2.0 / math.pi)


def _gelu_parts(g):
    inner = _GELU_C * (g + 0.044715 * g * g * g)
    th = jnp.tanh(inner)
    val = 0.5 * g * (1.0 + th)
    der = 0.5 * (1.0 + th) + 0.5 * g * (1.0 - th * th) * (_GELU_C * (1.0 + 3.0 * 0.044715 * g * g))
    return val, der


def _lru_gates(pa_ref, px_ref, ba_ref, bx_ref, lam_ref):
    r = _sigmoid(pa_ref[...] + ba_ref[...])
    ig = _sigmoid(px_ref[...] + bx_ref[...])
    sp = _softplus(-lam_ref[...])
    log_a = (-LRU_C) * r * sp
    a = jnp.exp(log_a)
    z = 2.0 * log_a
    one_m_a2 = jnp.where(z > -1e-3, -(z * (1.0 + z * (0.5 + z * (1.0 / 6.0)))), 1.0 - jnp.exp(z))
    return r, ig, sp, a, jnp.sqrt(one_m_a2)


def _lru_scan(pre, xb, proj, b_a, b_x, lam, name):
    s = xb.shape[0]
    tc = COL_TILE

    def body(pa_ref, px_ref, xb_ref, g_ref, ba_ref, bx_ref, lam_ref, y_ref, hs_ref):
        _, ig, _, a, mult = _lru_gates(pa_ref, px_ref, ba_ref, bx_ref, lam_ref)
        b = mult * (ig * xb_ref[...])
        d = 1
        while d < s:
            row = lax.broadcasted_iota(jnp.int32, a.shape, 0)
            keep = row >= d
            b = b + a * jnp.where(keep, pltpu.roll(b, d, 0), 0.0)
            a = a * jnp.where(keep, pltpu.roll(a, d, 0), 1.0)
            d *= 2
        hs_ref[...] = b
        y_ref[...] = (b * _gelu_parts(g_ref[...])[0]).astype(BF16)

    return _col_call(name, body, [pre, pre, xb, proj, b_a, b_x, lam],
                     [_col_spec(s, tc, 0), _col_spec(s, tc, 1), _col_spec(s, tc), _col_spec(s, tc, 0),
                      _small_spec(1, tc), _small_spec(1, tc), _small_spec(1, tc)],
                     [(s, BF16), (s, F32)], s, tc)


def _lru_scan_bwd(dy, pre, xb, proj, hs, b_a, b_x, lam, name):
    s = xb.shape[0]
    tc = COL_TILE

    def body(dy_ref, pa_ref, px_ref, xb_ref, g_ref, hs_ref, ba_ref, bx_ref, lam_ref,
             dg_ref, dpa_ref, dpx_ref, dxb_ref, dba_ref, dbx_ref, dlam_ref):
        r, ig, sp, a, mult = _lru_gates(pa_ref, px_ref, ba_ref, bx_ref, lam_ref)
        gl, gl_der = _gelu_parts(g_ref[...])
        dy_v = dy_ref[...]
        hs_v = hs_ref[...]
        dg_ref[...] = (dy_v * hs_v * gl_der).astype(BF16)
        lam_t = dy_v * gl
        coef = _shift_up(a, 1)
        d = 1
        while d < s:
            row = lax.broadcasted_iota(jnp.int32, coef.shape, 0)
            keep = row < s - d
            lam_t = lam_t + coef * jnp.where(keep, pltpu.roll(lam_t, s - d, 0), 0.0)
            coef = coef * jnp.where(keep, pltpu.roll(coef, s - d, 0), 1.0)
            d *= 2
        xb_v = xb_ref[...]
        da = lam_t * _shift_down(hs_v, 1)
        dmult = lam_t * (ig * xb_v)
        dig = lam_t * mult * xb_v
        dxb_ref[...] = lam_t * mult * ig
        dlog_a = da * a - dmult * (a * a) / mult
        dr = dlog_a * ((-LRU_C) * sp)
        dsp = _rows_sum(dlog_a * ((-LRU_C) * r))
        dlam_ref[...] = -dsp * _sigmoid(-lam_ref[...])
        dpa = dr * r * (1.0 - r)
        dpx = dig * ig * (1.0 - ig)
        dba_ref[...] = _rows_sum(dpa)
        dbx_ref[...] = _rows_sum(dpx)
        dpa_ref[...] = dpa.astype(BF16)
        dpx_ref[...] = dpx.astype(BF16)

    return _col_call(name, body, [dy, pre, pre, xb, proj, hs, b_a, b_x, lam],
                     [_col_spec(s, tc), _col_spec(s, tc, 0), _col_spec(s, tc, 1), _col_spec(s, tc),
                      _col_spec(s, tc, 0), _col_spec(s, tc),
                      _small_spec(1, tc), _small_spec(1, tc), _small_spec(1, tc)],
                     [(s, BF16), (s, BF16), (s, BF16), (s, F32), (1, F32), (1, F32), (1, F32)], s, tc)


def _post_pre(x, y, g_post, gate, coef, g_pre, scale, shift, name):
    def body(x_ref, y_ref, gq_ref, gate_ref, gp_ref, sc_ref, sh_ref, xo_ref, h_ref):
        yv = y_ref[...]
        xo = x_ref[...] + (coef * gate_ref[...]) * ((yv * _rms(yv)) * gq_ref[...])
        xo_ref[...] = xo
        h_ref[...] = ((xo * _rms(xo)) * gp_ref[...] * (1.0 + sc_ref[...]) + sh_ref[...]).astype(BF16)
    return _row_call(name, body, [x, y], [g_post, gate, g_pre, scale, shift], [(D_MODEL, F32), (D_MODEL, BF16)], [])


def _pre_post_bwd(dxo, dh, x, g_pre, scale, y, g_post, gate, coef, name, after=None):
    def body(dxo_ref, dh_ref, x_ref, y_ref, gp_ref, sc_ref, gq_ref, gate_ref,
             dx_ref, dy_ref, dshift_ref, dscale_ref, dgp_ref, dgate_ref, dgq_ref):
        xv = x_ref[...]
        r = _rms(xv)
        xn = xv * r
        dh_v = dh_ref[...]
        one_sc = 1.0 + sc_ref[...]
        dshift_ref[...] += _rows_sum(dh_v)
        dscale_ref[...] += _rows_sum(dh_v * (xn * gp_ref[...]))
        dgp_ref[...] += _rows_sum(dh_v * xn * one_sc)
        dxn = dh_v * (gp_ref[...] * one_sc)
        dx = dxo_ref[...] + r * (dxn - xn * jnp.mean(dxn * xn, axis=-1, keepdims=True))
        dx_ref[...] = dx
        yv = y_ref[...]
        r2 = _rms(yv)
        yn = yv * r2
        dgate_ref[...] += _rows_sum(dx * (yn * gq_ref[...])) * coef
        dz = dx * (coef * gate_ref[...])
        dgq_ref[...] += _rows_sum(dz * yn)
        dyn = dz * gq_ref[...]
        dy_ref[...] = (r2 * (dyn - yn * jnp.mean(dyn * yn, axis=-1, keepdims=True))).astype(BF16)
    return _row_call(name, body, [dxo, dh, x, y], [g_pre, scale, g_post, gate], [(D_MODEL, F32), (D_MODEL, BF16)],
                     [(D_MODEL, F32)] * 5, after=after)


def _ffn_core(h, w_in, w_out, tag, after=None):
    g, u, a = _ffn_in_act(h, w_in, tag + "_in", after=after)
    y = _mm_nn(a, w_out, tag + "_out", tn=512)
    return y, (h, g, u, a)


def _ffn_core_bwd(dy, saved, w_in, w_out, tag):
    h, g, u, a = saved
    dw_out = _mm_tn(a, dy, tag + "_out_bw", tn=512)
    dgu = _ffn_out_bx_act(dy, w_out, g, u, tag + "_out_bx")
    dh, dw_in = _ffn_in_bwd(dgu, h, w_in, tag + "_in_b")
    return dh, dw_in, dw_out


def _pair_rows(v):
    return v.T.reshape(HEAD_PAIRS, 2, v.shape[0])


def _fox_fwd(h, p, tag):
    s = h.shape[0]
    proj = _mm_nn(h, p["w_in"], tag + "_in", tn=640)
    qkv = proj[:, :3 * D_MODEL].astype(BF16)
    flt = proj[:, 3 * D_MODEL:3 * D_MODEL + FOX_HEADS].T
    cum_t = _fox_gate(flt, p["b_f"], tag + "_gate")
    cum = cum_t.T
    cum_t2 = cum_t.reshape(HEAD_PAIRS, 2, s)
    o, ob, lse = _fox_attn_fwd(qkv, cum, cum_t2, tag + "_attn")
    y = _mm_nn(ob, p["w_out"], tag + "_out")
    return y, (qkv, flt, cum, cum_t2, o, ob, lse)


def _fox_bwd(dy, h, saved, p, tag):
    qkv, flt, cum, cum_t2, o, ob, lse = saved
    s = h.shape[0]
    do = _mm_nt(dy, p["w_out"], tag + "_out_bx")
    dw_out = _mm_tn(ob, dy, tag + "_out_bw")
    expand = jnp.pad(jnp.repeat(jnp.eye(FOX_HEADS, dtype=BF16), FOX_HEAD_DIM, axis=0),
                     ((0, 0), (0, PAIR_W - FOX_HEADS)))
    delta = _fox_delta(do, o, expand, tag + "_attn_delta")
    dq, dk, dv, dck, dcq = _fox_attn_bwd(qkv, do, cum, cum_t2, _pair_rows(lse), _pair_rows(delta), tag + "_attn_b")
    dcum_k = dck[:, :, :2].transpose(0, 2, 1).reshape(FOX_HEADS, s)
    dflt, db_f = _fox_gate_bwd(dcq.reshape(FOX_HEADS, s), dcum_k, flt, p["b_f"], tag + "_gate_b")
    dproj = jnp.concatenate(
        [dq, dk, dv, dflt.T, jnp.zeros((s, FOX_PAD - 3 * D_MODEL - FOX_HEADS), F32)], axis=1).astype(BF16)
    dh = _mm_nt(dproj, p["w_in"], tag + "_in_bx", tn=640)
    dw_in = _mm_tn(h, dproj, tag + "_in_bw", tn=640)
    return dh, {"w_in": dw_in, "w_out": dw_out, "b_f": db_f}


def _sconv_mix_fwd(h, p, tag):
    proj = _mm_nn(h, p["w_in"], tag + "_in")
    yb = _sconv_fwd(proj, p["conv_w"], tag + "_conv")
    y = _mm_nn(yb, p["w_out"], tag + "_out")
    return y, (proj, yb)


def _sconv_mix_bwd(dy, h, saved, p, tag):
    proj, yb = saved
    dyb = _mm_nt(dy, p["w_out"], tag + "_out_bx")
    dw_out = _mm_tn(yb, dy, tag + "_out_bw")
    db, dc, dxv, dcw = _sconv_bwd(dyb, proj, p["conv_w"], tag + "_conv_b")
    dproj = jnp.concatenate([db, dc, dxv], axis=1)
    dh = _mm_nt(dproj, p["w_in"], tag + "_in_bx")
    dw_in = _mm_tn(h, dproj, tag + "_in_bw", tn=p["w_in"].arr.shape[-1], blocked_out=True)
    return dh, {"w_in": dw_in, "w_out": dw_out, "conv_w": dcw}


def _lru_mix_fwd(h, p, tag):
    proj = _mm_nn(h, p["w_in"], tag + "_in")
    xb, xbb = _lru_conv(proj, p["conv_w"], p["conv_b"], tag + "_conv")
    pre = _mm_nn(xbb, p["w_ax"], tag + "_gates", tn=D_MODEL)
    yb, hs = _lru_scan(pre, xb, proj, p["b_a"], p["b_x"], p["lam"], tag + "_scan")
    y = _mm_nn(yb, p["w_out"], tag + "_out")
    return y, (proj, xb, xbb, pre, yb, hs)


def _diag_blocks(m):
    return jnp.stack([m[LRU_BLOCK_DIM * n:LRU_BLOCK_DIM * (n + 1), LRU_BLOCK_DIM * n:LRU_BLOCK_DIM * (n + 1)]
                      for n in range(LRU_BLOCKS)])


def _lru_mix_bwd(dy, h, saved, p, tag):
    proj, xb, xbb, pre, yb, hs = saved
    dyb = _mm_nt(dy, p["w_out"], tag + "_out_bx")
    dw_out = _mm_tn(yb, dy, tag + "_out_bw")
    dg, dpa, dpx, dxb1, dba, dbx, dlam = _lru_scan_bwd(dyb, pre, xb, proj, hs, p["b_a"], p["b_x"], p["lam"],
                                                       tag + "_scan_b")
    dpre = jnp.concatenate([dpa, dpx], axis=1)
    dxb2 = _mm_nt(dpre, p["w_ax"], tag + "_gates_bx", tn=D_MODEL)
    dw_ax = _mm_tn(xbb, dpre, tag + "_gates_bw", tn=D_MODEL)
    dx0, dcw, dcb = _lru_conv_bwd(dxb1, dxb2, proj, p["conv_w"], tag + "_conv_b")
    dproj = jnp.concatenate([dg, dx0], axis=1)
    dh = _mm_nt(dproj, p["w_in"], tag + "_in_bx")
    dw_in = _mm_tn(h, dproj, tag + "_in_bw", tn=p["w_in"].arr.shape[-1], blocked_out=True)
    grads = {"w_in": dw_in, "w_out": dw_out, "conv_w": dcw, "conv_b": dcb,
             "w_a": _diag_blocks(dw_ax[:, :D_MODEL]), "w_x": _diag_blocks(dw_ax[:, D_MODEL:]),
             "b_a": dba, "b_x": dbx, "lam": dlam}
    return dh, grads


_MIXERS = ((_fox_fwd, _fox_bwd), (_sconv_mix_fwd, _sconv_mix_bwd), (_lru_mix_fwd, _lru_mix_bwd))


def _local_step(x, target, mod, layer_params, on_grads=None, on_mid=None, first_after=None):
    layers = []
    tape = []
    for i in range(DEPTH):
        lp = dict(layer_params(i, 0, x))
        layers.append(lp)
        row = lambda v: v[None, :]
        m = lambda sub, what: mod[i, sub, what][None, :]
        gp, gq = lp["norm_pre"], lp["norm_post"]
        h0 = _pre_norm(x, row(gp[0]), m(0, 1), m(0, 0), f"l{i}_ffn0_pre", after=first_after if i == 0 else None)
        y0, sv0 = _ffn_core(h0, lp["ffn_in"][0], lp["ffn_out"][0], f"l{i}_ffn0")
        x1, h1 = _post_pre(x, y0, row(gq[0]), m(0, 2), 0.5, row(gp[1]), m(1, 1), m(1, 0), f"l{i}_ffn0_post")
        lp.update(layer_params(i, 1, x1))
        y1, svm = _MIXERS[i % 3][0](h1, lp["mixer"], f"l{i}_mix")
        x2, h2 = _post_pre(x1, y1, row(gq[1]), m(1, 2), 1.0, row(gp[2]), m(2, 1), m(2, 0), f"l{i}_mix_post")
        second = layer_params(i, 2, x2)
        lp["ffn_in"] = lp["ffn_in"] + second["ffn_in"]
        lp["ffn_out"] = lp["ffn_out"] + second["ffn_out"]
        y2, sv2 = _ffn_core(h2, lp["ffn_in"][1], lp["ffn_out"][1], f"l{i}_ffn1", after=second.get("after"))
        x3 = _post_norm(x2, y2, row(gq[2]), m(2, 2), 0.5, f"l{i}_ffn1_post")
        tape.append((x, y0, sv0, x1, h1, y1, svm, x2, y2, sv2))
        x = x3
    dx, loss_row = _loss_head(x, target, "loss_head")

    layer_grads = [None] * DEPTH
    dmod = [None] * DEPTH
    after = None
    for i in reversed(range(DEPTH)):
        lp = layers[i]
        row = lambda v: v[None, :]
        m = lambda sub, what: mod[i, sub, what][None, :]
        gp, gq = lp["norm_pre"], lp["norm_post"]
        x0, y0, sv0, x1, h1, y1, svm, x2, y2, sv2 = tape[i]
        dy2, dgate2, dgq2 = _post_norm_bwd(dx, y2, row(gq[2]), m(2, 2), 0.5, f"l{i}_ffn1_post_b", after=after)
        dh2, dw_in1, dw_out1 = _ffn_core_bwd(dy2, sv2, lp["ffn_in"][1], lp["ffn_out"][1], f"l{i}_ffn1")
        after = on_mid(i, dh2) if on_mid is not None else None
        dx, dy1, dshift2, dscale2, dgp2, dgate1, dgq1 = _pre_post_bwd(
            dx, dh2, x2, row(gp[2]), m(2, 1), y1, row(gq[1]), m(1, 2), 1.0, f"l{i}_mix_post_b", after=after)
        dh1, mg = _MIXERS[i % 3][1](dy1, h1, svm, lp["mixer"], f"l{i}_mix")
        dx, dy0, dshift1, dscale1, dgp1, dgate0, dgq0 = _pre_post_bwd(
            dx, dh1, x1, row(gp[1]), m(1, 1), y0, row(gq[0]), m(0, 2), 0.5, f"l{i}_ffn0_post_b")
        dh0, dw_in0, dw_out0 = _ffn_core_bwd(dy0, sv0, lp["ffn_in"][0], lp["ffn_out"][0], f"l{i}_ffn0")
        dx, dshift0, dscale0, dgp0 = _pre_norm_bwd(dx, dh0, x0, row(gp[0]), m(0, 1), f"l{i}_ffn0_pre_b")
        dmod[i] = jnp.concatenate([dshift0, dscale0, dgate0, dshift1, dscale1, dgate1, dshift2, dscale2, dgate2],
                                  axis=0).reshape(N_SUB, 3, D_MODEL)
        layer_grads[i] = {"ffn_in": (dw_in0, dw_in1), "ffn_out": (dw_out0, dw_out1),
                          "norm_pre": jnp.concatenate([dgp0, dgp1, dgp2], axis=0),
                          "norm_post": jnp.concatenate([dgq0, dgq1, dgq2], axis=0), "mixer": mg}
        if on_grads is not None:
            after = on_grads(i, layer_grads[i], dx)
    return loss_row, dx, jnp.stack(dmod), layer_grads


COND_ROWS = 16
COND_PAD = 128


def _cond_fwd(c_pad, w_cond, b_shard, name):
    nl, d, n = w_cond.shape
    tn = 768

    def body(c_ref, w_ref, b_ref, o_ref):
        cv = c_ref[...]
        act = (cv * _sigmoid(cv)).astype(BF16)
        o_ref[...] = jnp.dot(act, w_ref[...].astype(BF16), preferred_element_type=F32) + b_ref[...]

    return pl.pallas_call(
        body, name=name, grid=(nl, n // tn),
        in_specs=[pl.BlockSpec((COND_ROWS, d), lambda i, j: (0, 0)),
                  pl.BlockSpec((None, d, tn), lambda i, j: (i, 0, j)),
                  pl.BlockSpec((None, 1, tn), lambda i, j: (i, 0, j))],
        out_specs=pl.BlockSpec((None, COND_ROWS, tn), lambda i, j: (i, 0, j)),
        out_shape=jax.ShapeDtypeStruct((nl, COND_ROWS, n), F32),
        compiler_params=_cparams(("arbitrary", "arbitrary")),
    )(c_pad, w_cond, b_shard)


def _adam_math(w, g, m, v):
    nm = ADAM_B1 * m + (1.0 - ADAM_B1) * g
    nv = ADAM_B2 * v + (1.0 - ADAM_B2) * (g * g)
    m_hat = nm / (1.0 - ADAM_B1 ** ADAM_STEP)
    v_hat = nv / (1.0 - ADAM_B2 ** ADAM_STEP)
    delta = (-ADAM_LR) * (m_hat / (jnp.sqrt(v_hat) + ADAM_EPS) + ADAM_WD * w)
    return delta, nm, nv


def _cond_bwd_adamw(c_t, dmod_s, w, m, v, name):
    nl, d, n = w.shape
    tn = 384
    blk = pl.BlockSpec((None, d, tn), lambda i, j: (i, 0, j))

    def body(c_ref, dm_ref, w_ref, m_ref, v_ref, g_ref, d_ref, nm_ref, nv_ref):
        cv = c_ref[...]
        g = jnp.dot((cv * _sigmoid(cv)).astype(BF16), dm_ref[...], preferred_element_type=F32)
        g_ref[...] = g
        d_ref[...], nm_ref[...], nv_ref[...] = _adam_math(w_ref[...], g, m_ref[...], v_ref[...])

    return pl.pallas_call(
        body, name=name, grid=(nl, n // tn),
        in_specs=[pl.BlockSpec((d, COND_PAD), lambda i, j: (0, 0)),
                  pl.BlockSpec((None, COND_PAD, tn), lambda i, j: (i, 0, j)), blk, blk, blk],
        out_specs=[blk] * 4, out_shape=[jax.ShapeDtypeStruct(w.shape, F32)] * 4,
        compiler_params=_cparams(("arbitrary", "arbitrary")),
    )(c_t, dmod_s, w, m, v)


def _adamw(w, g, m, v, name):
    rows, cols = w.shape
    tr = next(t for t in (256, 176, 128, 64, 32, 16, 8) if rows % t == 0)
    blk = pl.BlockSpec((tr, cols), lambda i: (i, 0))

    def body(w_ref, g_ref, m_ref, v_ref, d_ref, nm_ref, nv_ref):
        d_ref[...], nm_ref[...], nv_ref[...] = _adam_math(w_ref[...], g_ref[...], m_ref[...], v_ref[...])

    return pl.pallas_call(
        body, name=name, grid=(rows // tr,), in_specs=[blk] * 4, out_specs=[blk] * 3,
        out_shape=[jax.ShapeDtypeStruct(w.shape, F32)] * 3, compiler_params=_cparams(("arbitrary",)),
    )(w, g, m, v)


def _adamw_rows(w, g, m, v, outs, row0, nrows, name):
    cols = w.shape[1]
    tr = next(t for t in (512, 256, 128, 64, 32, 16, 8) if nrows % t == 0 and row0 % t == 0)
    blk = pl.BlockSpec((tr, cols), lambda i: (i + row0 // tr, 0))
    anywhere = pl.BlockSpec(memory_space=pl.ANY)

    def body(w_ref, g_ref, m_ref, v_ref, d_in, nm_in, nv_in, d_ref, nm_ref, nv_ref, g_out):
        d_ref[...], nm_ref[...], nv_ref[...] = _adam_math(w_ref[...], g_ref[...], m_ref[...], v_ref[...])

    return pl.pallas_call(
        body, name=name, grid=(nrows // tr,), in_specs=[blk] * 4 + [anywhere] * 3,
        out_specs=[blk] * 3 + [anywhere], out_shape=[jax.ShapeDtypeStruct(w.shape, F32)] * 4,
        input_output_aliases={4: 0, 5: 1, 6: 2, 1: 3}, compiler_params=_cparams(("arbitrary",)),
    )(w, g, m, v, *outs)


_MESH = pl.DeviceIdType.MESH
_ANY = pl.BlockSpec(memory_space=pl.ANY)


def _place():
    return lax.axis_index("x"), lax.axis_index("y"), lax.axis_index("c")


def _other_chips(x, y):
    return [(1 - x, y), (x, 1 - y), (1 - x, 1 - y)]


def _allgather8(block, name):
    m_per, n = block.shape

    def body(x_ref, out_ref, send_sems, recv_sems, local_sem):
        x, y, c = _place()
        me, sibling = (x, y, c), (x, y, 1 - c)
        chips = _other_chips(x, y)

        def rows(px, py, pc):
            return out_ref.at[pl.ds((4 * px + 2 * py + pc) * m_per, m_per), :]

        def copy(k, blk, to, src=None):
            return pltpu.make_async_remote_copy(
                src_ref=rows(*blk) if src is None else src, dst_ref=rows(*blk),
                send_sem=send_sems.at[k], recv_sem=recv_sems.at[k], device_id=to, device_id_type=_MESH)

        mine = pltpu.make_async_copy(x_ref, rows(*me), local_sem)
        mine.start()
        first = [copy(0, me, sibling, src=x_ref)]
        first += [copy(1 + j, me, (*chip, c), src=x_ref) for j, chip in enumerate(chips)]
        for cp in first:
            cp.start()
        passed = [copy(4 + j, (*chip, c), sibling) for j, chip in enumerate(chips)]
        for j, chip in enumerate(chips):
            copy(1 + j, (*chip, c), me).wait_recv()
            passed[j].start()
        copy(0, sibling, me).wait_recv()
        for j, chip in enumerate(chips):
            copy(4 + j, (*chip, 1 - c), me).wait_recv()
        for cp in first + passed:
            cp.wait_send()
        mine.wait()

    return pl.pallas_call(
        body, name=name, out_shape=jax.ShapeDtypeStruct((N_DEV * m_per, n), block.dtype),
        in_specs=[pl.BlockSpec(memory_space=pltpu.VMEM)], out_specs=pl.BlockSpec(memory_space=pltpu.VMEM),
        scratch_shapes=[pltpu.SemaphoreType.DMA((7,)), pltpu.SemaphoreType.DMA((7,)), pltpu.SemaphoreType.DMA],
        compiler_params=_cparams(),
    )(block)


def _split_axis(shape):
    return next(a for a, n in enumerate(shape) if n > 1)


_HBM = pl.BlockSpec(memory_space=pltpu.HBM)
_SEM = pl.BlockSpec(memory_space=pltpu.SEMAPHORE)
_SPLIT_COPY = pltpu.CompilerParams(has_side_effects=pltpu.SideEffectType.DATAFLOW_SIDE_EFFECTING)
_TOKEN = jax.ShapeDtypeStruct((8, 128), F32)


def _in_hbm(arrays):
    return [pltpu.with_memory_space_constraint(a, pltpu.HBM) for a in arrays]


class _Gathered(NamedTuple):
    shard_shape: tuple
    chip_axis: int

    @property
    def shape(self):
        return self.shard_shape[:self.chip_axis] + (N_CHIPS,) + self.shard_shape[self.chip_axis:]

    def half(self, ref, chip, pc):
        cut = _split_axis(self.shard_shape)
        n = self.shard_shape[cut] // 2
        idx = [slice(None)] * len(self.shard_shape)
        idx[cut] = pl.ds(pc * n, n)
        idx.insert(self.chip_axis, chip)
        return ref.at[tuple(idx)]


def _own_block_placed(shard, layout, chip):
    return lax.dynamic_update_slice_in_dim(lax.empty(layout.shape, shard.dtype),
                                           jnp.expand_dims(shard, layout.chip_axis), chip, axis=layout.chip_axis)


def _gather_copies(lands, layouts, send_sems, recv_sems):
    x, y, c = _place()
    out = []
    for t, (land, lay) in enumerate(zip(lands, layouts)):
        for j, (px, py) in enumerate(_other_chips(x, y)):
            def copy(chip, t=t, j=j, px=px, py=py, land=land, lay=lay):
                return pltpu.make_async_remote_copy(
                    src_ref=lay.half(land, chip, c), dst_ref=lay.half(land, chip, c),
                    send_sem=send_sems.at[3 * t + j], recv_sem=recv_sems.at[3 * t + j],
                    device_id=(px, py, c), device_id_type=_MESH)
            out.append((copy(2 * x + y), copy(2 * px + py)))
    return out


def _gather_start(lands, layouts, after, name):
    nt = len(lands)
    order = [] if after is None else [after]

    def body(*refs):
        land_refs = refs[:nt]
        send_sems, recv_sems = refs[nt + len(order):nt + len(order) + 2]
        token = refs[-1]
        for send, _ in _gather_copies(land_refs, layouts, send_sems, recv_sems):
            send.start()
        token[...] = jnp.zeros_like(token)

    out = pl.pallas_call(
        body, name=name,
        out_shape=(pltpu.SemaphoreType.DMA((3 * nt,)), pltpu.SemaphoreType.DMA((3 * nt,)),
                   *[pltpu.HBM(a.shape, a.dtype) for a in lands], _TOKEN),
        in_specs=[_HBM] * nt + [_ANY] * len(order),
        out_specs=(_SEM, _SEM, *[_HBM] * nt, pl.BlockSpec(memory_space=pltpu.VMEM)),
        input_output_aliases={t: 2 + t for t in range(nt)}, compiler_params=_SPLIT_COPY,
    )(*_in_hbm(lands), *order)
    return out[0], out[1], list(out[2:2 + nt]), out[-1]


def _gather_wait(send_sems, recv_sems, lands, layouts, after, name):
    nt = len(lands)

    def body(*refs):
        land_refs = refs[:nt]
        sems = refs[nt:nt + 2]
        for send, arrival in _gather_copies(land_refs, layouts, *sems):
            send.wait_send()
            arrival.wait_recv()

    return list(pl.pallas_call(
        body, name=name, out_shape=tuple(pltpu.HBM(a.shape, a.dtype) for a in lands),
        in_specs=[_HBM] * nt + [_SEM, _SEM, _ANY], out_specs=tuple([_HBM] * nt),
        input_output_aliases={t: t for t in range(nt)}, compiler_params=_SPLIT_COPY,
    )(*lands, send_sems, recv_sems, after))


def _gather_forward(lands, layouts, name):
    nt = len(lands)

    def body(*refs):
        outs = refs[nt:2 * nt]
        send_sems, recv_sems = refs[2 * nt:]
        x, y, c = _place()
        sends, arrivals = [], []
        for t, lay in enumerate(layouts):
            for j, (px, py) in enumerate(_other_chips(x, y)):
                for pc, group in ((c, sends), (1 - c, arrivals)):
                    part = lay.half(outs[t], 2 * px + py, pc)
                    group.append(pltpu.make_async_remote_copy(
                        src_ref=part, dst_ref=part, send_sem=send_sems.at[3 * t + j], recv_sem=recv_sems.at[3 * t + j],
                        device_id=(x, y, 1 - c), device_id_type=_MESH))
        for cp in sends:
            cp.start()
        for cp in arrivals:
            cp.wait_recv()
        for cp in sends:
            cp.wait_send()

    return list(pl.pallas_call(
        body, name=name, out_shape=[jax.ShapeDtypeStruct(a.shape, a.dtype) for a in lands],
        in_specs=[_ANY] * nt, out_specs=[_ANY] * nt, input_output_aliases={t: t for t in range(nt)},
        scratch_shapes=[pltpu.SemaphoreType.DMA((3 * nt,)), pltpu.SemaphoreType.DMA((3 * nt,))],
        compiler_params=_cparams(),
    )(*lands))


def _forward_copies(lands, layouts, send_sems, recv_sems):
    x, y, c = _place()
    out = []
    for t, (land, lay) in enumerate(zip(lands, layouts)):
        for j, (px, py) in enumerate(_other_chips(x, y)):
            def copy(pc, t=t, j=j, px=px, py=py, land=land, lay=lay):
                part = lay.half(land, 2 * px + py, pc)
                return pltpu.make_async_remote_copy(
                    src_ref=part, dst_ref=part, send_sem=send_sems.at[3 * t + j], recv_sem=recv_sems.at[3 * t + j],
                    device_id=(x, y, 1 - c), device_id_type=_MESH)
            out.append((copy(c), copy(1 - c)))
    return out


def _gather_forward_start(lands, layouts, name):
    nt = len(lands)

    def body(*refs):
        for send, _ in _forward_copies(refs[:nt], layouts, refs[nt], refs[nt + 1]):
            send.start()
        refs[-1][...] = jnp.zeros_like(refs[-1])

    out = pl.pallas_call(
        body, name=name,
        out_shape=(pltpu.SemaphoreType.DMA((3 * nt,)), pltpu.SemaphoreType.DMA((3 * nt,)),
                   *[pltpu.HBM(a.shape, a.dtype) for a in lands], _TOKEN),
        in_specs=[_HBM] * nt, out_specs=(_SEM, _SEM, *[_HBM] * nt, pl.BlockSpec(memory_space=pltpu.VMEM)),
        input_output_aliases={t: 2 + t for t in range(nt)}, compiler_params=_SPLIT_COPY,
    )(*_in_hbm(lands))
    return out[0], out[1], list(out[2:2 + nt]), out[-1]


def _gather_forward_wait(send_sems, recv_sems, lands, layouts, after, name):
    nt = len(lands)

    def body(*refs):
        for send, arrival in _forward_copies(refs[:nt], layouts, refs[nt], refs[nt + 1]):
            send.wait_send()
            arrival.wait_recv()

    return list(pl.pallas_call(
        body, name=name, out_shape=tuple(pltpu.HBM(a.shape, a.dtype) for a in lands),
        in_specs=[_HBM] * nt + [_SEM, _SEM, _ANY], out_specs=tuple([_HBM] * nt),
        input_output_aliases={t: t for t in range(nt)}, compiler_params=_SPLIT_COPY,
    )(*lands, send_sems, recv_sems, after))


def _pair_copies(grads, lands, send_sems, recv_sems):
    x, y, c = _place()
    out = []
    for t, (g, land) in enumerate(zip(grads, lands)):
        h = g.shape[1] // 2
        out.append(pltpu.make_async_remote_copy(
            src_ref=g.at[:, pl.ds((1 - c) * h, h), :], dst_ref=land, send_sem=send_sems.at[t],
            recv_sem=recv_sems.at[t], device_id=(x, y, 1 - c), device_id_type=_MESH))
    return out


def _pair_start(grads, after, name):
    nt = len(grads)
    lands = [lax.empty((N_CHIPS, g.shape[1] // 2, g.shape[2]), g.dtype) for g in grads]
    order = [] if after is None else [after]

    def body(*refs):
        send_sems, recv_sems = refs[2 * nt + len(order):2 * nt + len(order) + 2]
        token = refs[-1]
        for cp in _pair_copies(refs[:nt], refs[nt:2 * nt], send_sems, recv_sems):
            cp.start()
        token[...] = jnp.zeros_like(token)

    out = pl.pallas_call(
        body, name=name,
        out_shape=(pltpu.SemaphoreType.DMA((nt,)), pltpu.SemaphoreType.DMA((nt,)),
                   *[pltpu.HBM(a.shape, a.dtype) for a in grads + lands], _TOKEN),
        in_specs=[_HBM] * (2 * nt) + [_ANY] * len(order),
        out_specs=(_SEM, _SEM, *[_HBM] * (2 * nt), pl.BlockSpec(memory_space=pltpu.VMEM)),
        input_output_aliases={t: 2 + t for t in range(2 * nt)}, compiler_params=_SPLIT_COPY,
    )(*_in_hbm(grads + lands), *order)
    return out[0], out[1], list(out[2:2 + nt]), list(out[2 + nt:2 + 2 * nt]), out[-1]


def _pair_wait(send_sems, recv_sems, grads, lands, after, name):
    nt = len(grads)

    def body(*refs):
        for cp in _pair_copies(refs[:nt], refs[nt:2 * nt], *refs[2 * nt:2 * nt + 2]):
            cp.wait_send()
            cp.wait_recv()

    out = pl.pallas_call(
        body, name=name, out_shape=tuple(pltpu.HBM(a.shape, a.dtype) for a in grads + lands),
        in_specs=[_HBM] * (2 * nt) + [_SEM, _SEM, _ANY], out_specs=tuple([_HBM] * (2 * nt)),
        input_output_aliases={t: t for t in range(2 * nt)}, compiler_params=_SPLIT_COPY,
    )(*grads, *lands, send_sems, recv_sems, after)
    return list(out[:nt]), list(out[nt:])


def _pair_sum(own, recv, c_idx, name):
    _, h, cols = recv.shape

    def body(c_ref, own_ref, recv_ref, o_ref):
        o_ref[...] = (own_ref[...] + recv_ref[...]).astype(BF16)

    return pl.pallas_call(
        body, name=name,
        grid_spec=pltpu.PrefetchScalarGridSpec(
            num_scalar_prefetch=1, grid=(N_CHIPS,),
            in_specs=[pl.BlockSpec((None, h, cols), lambda k, c_ref: (k, c_ref[0], 0)),
                      pl.BlockSpec((None, h, cols), lambda k, c_ref: (k, 0, 0))],
            out_specs=pl.BlockSpec((None, h, cols), lambda k, c_ref: (k, 0, 0))),
        out_shape=jax.ShapeDtypeStruct(recv.shape, BF16), compiler_params=_cparams(("arbitrary",)),
    )(c_idx, own, recv)


def _chip_copies(parts, lands, send_sems, recv_sems):
    x, y, c = _place()
    out = []
    for t, (part, land) in enumerate(zip(parts, lands)):
        for j, (px, py) in enumerate(_other_chips(x, y)):
            out.append(pltpu.make_async_remote_copy(
                src_ref=part.at[2 * px + py], dst_ref=land.at[j], send_sem=send_sems.at[3 * t + j],
                recv_sem=recv_sems.at[3 * t + j], device_id=(px, py, c), device_id_type=_MESH))
    return out


def _chip_send_start(parts, after, name):
    nt = len(parts)
    lands = [lax.empty((N_CHIPS - 1,) + p.shape[1:], p.dtype) for p in parts]
    order = [] if after is None else [after]

    def body(*refs):
        send_sems, recv_sems = refs[2 * nt + len(order):2 * nt + len(order) + 2]
        token = refs[-1]
        for cp in _chip_copies(refs[:nt], refs[nt:2 * nt], send_sems, recv_sems):
            cp.start()
        token[...] = jnp.zeros_like(token)

    out = pl.pallas_call(
        body, name=name,
        out_shape=(pltpu.SemaphoreType.DMA((3 * nt,)), pltpu.SemaphoreType.DMA((3 * nt,)),
                   *[pltpu.HBM(a.shape, a.dtype) for a in parts + lands], _TOKEN),
        in_specs=[_HBM] * (2 * nt) + [_ANY] * len(order),
        out_specs=(_SEM, _SEM, *[_HBM] * (2 * nt), pl.BlockSpec(memory_space=pltpu.VMEM)),
        input_output_aliases={t: 2 + t for t in range(2 * nt)}, compiler_params=_SPLIT_COPY,
    )(*_in_hbm(parts + lands), *order)
    return out[0], out[1], list(out[2:2 + nt]), list(out[2 + nt:2 + 2 * nt]), out[-1]


def _chip_send_wait(send_sems, recv_sems, parts, lands, after, name):
    nt = len(parts)

    def body(*refs):
        for cp in _chip_copies(refs[:nt], refs[nt:2 * nt], *refs[2 * nt:2 * nt + 2]):
            cp.wait_send()
            cp.wait_recv()

    out = pl.pallas_call(
        body, name=name, out_shape=tuple(pltpu.HBM(a.shape, a.dtype) for a in parts + lands),
        in_specs=[_HBM] * (2 * nt) + [_SEM, _SEM, _ANY], out_specs=tuple([_HBM] * (2 * nt)),
        input_output_aliases={t: t for t in range(2 * nt)}, compiler_params=_SPLIT_COPY,
    )(*parts, *lands, send_sems, recv_sems, after)
    return list(out[:nt]), list(out[nt:])


def _chip_sum(part, arrived, into, lead, place_idx, name):
    _, h, cols = part.shape

    def body(idx_ref, own_ref, arr_ref, into_ref, o_ref):
        acc = own_ref[...].astype(F32)
        for k in range(N_CHIPS - 1):
            acc = acc + arr_ref[k].astype(F32)
        o_ref[...] = acc

    return pl.pallas_call(
        body, name=name,
        grid_spec=pltpu.PrefetchScalarGridSpec(
            num_scalar_prefetch=1, grid=(1,),
            in_specs=[pl.BlockSpec((None, h, cols), lambda g, idx: (idx[1], 0, 0)),
                      pl.BlockSpec((N_CHIPS - 1, h, cols), lambda g, idx: (0, 0, 0)), _ANY],
            out_specs=pl.BlockSpec((None,) * len(lead) + (h, cols), lambda g, idx: (*lead, idx[0], 0))),
        out_shape=jax.ShapeDtypeStruct(into.shape, F32), input_output_aliases={3: 0},
        compiler_params=_cparams(("arbitrary",)),
    )(place_idx, part, arrived, into)


def _pair_gather(bufs, homes, name):
    nt, nb = len(homes), len(bufs)

    def body(*refs):
        outs = refs[nb:2 * nb]
        send_sems, recv_sems = refs[2 * nb:]
        x, y, c = _place()

        def home(t, pc):
            o, lead, rows = homes[t]
            return outs[o].at[(*lead, pl.ds(pc * (rows // 2), rows // 2), slice(None))]

        def copy(t, pc):
            return pltpu.make_async_remote_copy(src_ref=home(t, pc), dst_ref=home(t, pc), send_sem=send_sems.at[t],
                                                recv_sem=recv_sems.at[t], device_id=(x, y, 1 - c), device_id_type=_MESH)

        sends = [copy(t, c) for t in range(nt)]
        for cp in sends:
            cp.start()
        for t in range(nt):
            copy(t, 1 - c).wait_recv()
        for cp in sends:
            cp.wait_send()

    return pl.pallas_call(
        body, name=name, out_shape=[jax.ShapeDtypeStruct(b.shape, b.dtype) for b in bufs],
        in_specs=[_ANY] * nb, out_specs=[_ANY] * nb, input_output_aliases={o: o for o in range(nb)},
        scratch_shapes=[pltpu.SemaphoreType.DMA((nt,)), pltpu.SemaphoreType.DMA((nt,))],
        compiler_params=_cparams(),
    )(*bufs)


def _sum_devices(g, after, name):
    def body(g_ref, after_ref, o_ref):
        acc = g_ref[0:1, :]
        for d in range(1, N_DEV):
            acc = acc + g_ref[d:d + 1, :]
        o_ref[...] = acc
    vmem = pl.BlockSpec(memory_space=pltpu.VMEM)
    return pl.pallas_call(body, name=name, out_shape=jax.ShapeDtypeStruct((1, g.shape[1]), F32),
                          in_specs=[vmem, _ANY], out_specs=vmem, compiler_params=_cparams())(g, after)


_WEIGHTS = ("w_cond", "b_cond", "norm_pre", "norm_post", "w_ffn_in", "w_ffn_out", "fox_w_in", "fox_b_f",
            "fox_w_out", "sconv_w_in", "sconv_conv_w", "sconv_w_out", "lru_w_in", "lru_conv_w", "lru_conv_b",
            "lru_w_a", "lru_b_a", "lru_w_x", "lru_b_x", "lru_lambda", "lru_w_out")
_BIG = (("w_ffn_in", False), ("w_ffn_out", True), ("fox_w_in", False), ("fox_w_out", True),
        ("sconv_w_in", False), ("sconv_w_out", True), ("lru_w_in", False), ("lru_w_out", True))
_SMALL = tuple(n for n in _WEIGHTS if n != "w_cond" and n not in dict(_BIG))
_COL_SHARDED_SMALL = ("norm_pre", "norm_post", "sconv_conv_w", "lru_conv_w", "lru_conv_b", "lru_lambda")


def _pack_rows(parts, rows=8):
    flat = jnp.concatenate([p.reshape(-1) for p in parts])
    width = -(-flat.size // (rows * 128)) * 128
    return jnp.pad(flat, (0, rows * width - flat.size)).reshape(rows, width)


def _unpack(flat, shapes):
    out, off = [], 0
    for shp in shapes:
        n = math.prod(shp)
        out.append(flat[off:off + n].reshape(shp))
        off += n
    return out


def _join_chips(g):
    g = jnp.moveaxis(g, 0, -2)
    return g.reshape(g.shape[:-2] + (g.shape[-2] * g.shape[-1],))


def _my_columns(full, chip):
    n = full.shape[-1] // N_CHIPS
    return lax.dynamic_slice_in_dim(full, chip * n, n, axis=full.ndim - 1)


def _block_diag(w):
    eye = jnp.eye(LRU_BLOCKS, dtype=w.dtype)
    return jnp.einsum("nij,nm->nimj", w, eye).reshape(D_MODEL, D_MODEL)


def _step(x, c, target, wts, mom, var):
    ix, iy, ic = _place()
    chip = 2 * ix + iy
    dev = 2 * chip + ic
    n_cond = wts["w_cond"].shape[2]

    small_shapes = [(D_MODEL,)] + [wts[n].shape for n in _COL_SHARDED_SMALL]
    g1 = _allgather8(_pack_rows([c[0]] + [wts[n] for n in _COL_SHARDED_SMALL]), "gather_small").reshape(N_DEV, -1)
    c_all = g1[:, :D_MODEL]
    per_chip = [jnp.stack(col) for col in zip(*[_unpack(g1[2 * k], small_shapes) for k in range(N_CHIPS)])]
    small_full = {n: _join_chips(v) for n, v in zip(_COL_SHARDED_SMALL, per_chip[1:])}

    c_pad = jnp.pad(c_all, ((0, COND_ROWS - N_DEV), (0, 0)))
    b_shard = _my_columns(wts["b_cond"], chip)[:, None, :]
    mod_part = _cond_fwd(c_pad, wts["w_cond"], b_shard, "cond_fwd")
    g2 = _allgather8(mod_part[:, :N_DEV].transpose(1, 0, 2).reshape(N_DEV, DEPTH * n_cond), "gather_mod")
    g2 = g2.reshape(N_DEV, N_DEV, DEPTH, n_cond)[0::2]
    mod = _join_chips(lax.dynamic_index_in_dim(g2, dev, axis=1, keepdims=False)).reshape(DEPTH, N_SUB, 3, D_MODEL)

    mixer_names = [("fox_w_in", "fox_w_out"), ("sconv_w_in", "sconv_w_out"), ("lru_w_in", "lru_w_out")]

    def shards_of(i, sub):
        if sub == 1:
            return [wts[n][i // 3] for n in mixer_names[i % 3]]
        return [wts["w_ffn_in"][i, sub // 2], wts["w_ffn_out"][i, sub // 2]]

    chunks = [[(0, sub)] for sub in range(N_SUB)] + [[(i, sub) for sub in range(N_SUB)] for i in range(1, DEPTH)]
    in_flight, chunk_of, token = [], {}, mod
    for k, members in enumerate(chunks):
        shards = [s for i, sub in members for s in shards_of(i, sub)]
        layouts = [_Gathered(s.shape, 0) for s in shards]
        if k:
            shards = [s + token[0, 0] for s in shards]
        lands = [_own_block_placed(s.astype(BF16), lay, chip) for s, lay in zip(shards, layouts)]
        send_sems, recv_sems, lands, token = _gather_start(lands, layouts, token, f"gather_start_{k}")
        in_flight.append([send_sems, recv_sems, lands, layouts, False])
        chunk_of.update({m: (k, 2 * pos) for pos, m in enumerate(members)})
    lru_ax = jnp.concatenate([_block_diag(wts["lru_w_a"][0]), _block_diag(wts["lru_w_x"][0])], axis=1).astype(BF16)

    prefetch_at = {(i, N_SUB - 1): i + N_SUB for i in range(DEPTH - 1)}

    def layer_params(i, sub, x_in):
        k, pos = chunk_of[(i, sub)]
        send_sems, recv_sems, lands, layouts, state = in_flight[k]
        if state == "passing":
            in_flight[k][2:] = [_gather_forward_wait(send_sems, recv_sems, lands, layouts, x_in, f"gather_pass_wait_{k}"),
                                layouts, "here"]
        elif state != "here":
            lands = _gather_wait(send_sems, recv_sems, lands, layouts, x_in, f"gather_wait_{k}")
            in_flight[k][2:] = [_gather_forward(lands, layouts, f"gather_forward_{k}"), layouts, "here"]
        nxt = prefetch_at.get((i, sub))
        started = None
        if nxt is not None:
            send_sems, recv_sems, lands, layouts, _ = in_flight[nxt]
            lands = _gather_wait(send_sems, recv_sems, lands, layouts, x_in, f"gather_wait_{nxt}")
            send_sems, recv_sems, lands, started = _gather_forward_start(lands, layouts, f"gather_pass_start_{nxt}")
            in_flight[nxt] = [send_sems, recv_sems, lands, layouts, "passing"]
        w_in, w_out = in_flight[k][2][pos:pos + 2]
        w_out = w_out.reshape(-1, w_out.shape[-1])
        if sub != 1:
            out = {"ffn_in": [_W(w_in, (), True)], "ffn_out": [_W(w_out)], "after": started}
            if sub == 0:
                out.update(norm_pre=small_full["norm_pre"][i], norm_post=small_full["norm_post"][i])
            return out
        j = i // 3
        if i % 3 == 0:
            w_in = jnp.pad(_join_chips(w_in), ((0, 0), (0, FOX_PAD - 3 * D_MODEL - FOX_HEADS)))
            return {"mixer": {"w_in": _W(w_in), "w_out": _W(w_out), "b_f": wts["fox_b_f"][j][:, None]}}
        if i % 3 == 1:
            return {"mixer": {"w_in": _W(w_in, (), True), "w_out": _W(w_out), "conv_w": small_full["sconv_conv_w"][j]}}
        return {"mixer": {"w_in": _W(w_in, (), True), "w_out": _W(w_out), "conv_w": small_full["lru_conv_w"][j],
                          "conv_b": small_full["lru_conv_b"], "w_ax": _W(lru_ax),
                          "b_a": wts["lru_b_a"].reshape(1, D_MODEL), "b_x": wts["lru_b_x"].reshape(1, D_MODEL),
                          "lam": small_full["lru_lambda"]}}

    place_idx = jnp.stack([ic, chip]).astype(jnp.int32)
    c_idx = place_idx[:1]
    big_index = {n: o for o, (n, _) in enumerate(_BIG)}
    exchanges, pending = [], []

    def to_chips(after):
        i, send_sems, recv_sems, tensors, lands, homes = pending.pop()
        tensors, recv = _pair_wait(send_sems, recv_sems, tensors, lands, after, f"grads_pair_wait_l{i}")
        parts = [_pair_sum(t, r, c_idx, f"grads_pair_sum_l{i}_{k}") for k, (t, r) in enumerate(zip(tensors, recv))]
        send_sems, recv_sems, parts, lands, tok = _chip_send_start(parts, None, f"grads_chip_start_l{i}")
        exchanges.append((i, send_sems, recv_sems, parts, lands, homes))
        return tok

    def chip_blocks(g, by_rows, width):
        if by_rows:
            return g.reshape(N_CHIPS, g.shape[0] // N_CHIPS, g.shape[1])
        if g.ndim == 3:
            return g
        return g[:, :width * N_CHIPS].reshape(g.shape[0], N_CHIPS, width).transpose(1, 0, 2)

    def on_mid(i, dx):
        return to_chips(dx) if pending else None

    def on_grads(i, g, dx):
        n_in, n_out = mixer_names[i % 3]
        items = [("w_ffn_in", (i, k), g["ffn_in"][k]) for k in range(2)]
        items += [("w_ffn_out", (i, k), g["ffn_out"][k]) for k in range(2)]
        items += [(n_in, (i // 3,), g["mixer"]["w_in"]), (n_out, (i // 3,), g["mixer"]["w_out"])]
        tensors = [chip_blocks(t, dict(_BIG)[n], wts[n].shape[-1]) for n, _, t in items]
        homes = [(big_index[n], lead, wts[n].shape[-2]) for n, lead, _ in items]
        send_sems, recv_sems, tensors, lands, tok = _pair_start(tensors, None, f"grads_pair_start_l{i}")
        pending.append((i, send_sems, recv_sems, tensors, lands, homes))
        pair_tokens.append(tok)
        return tok

    pair_tokens = []
    loss_row, grad_x, dmod, lg = _local_step(x[0], target[0], mod, layer_params, on_grads, on_mid, token)
    loss = lax.psum(loss_row[0, 0], ("x", "y", "c"))
    dmod = dmod + pair_tokens[-1][0, 0]

    fox_layers = [i for i in range(DEPTH) if i % 3 == 0]
    sconv_g, lru_g = lg[1]["mixer"], lg[2]["mixer"]
    small_g = {
        "dmod": dmod, "norm_pre": jnp.stack([g["norm_pre"] for g in lg]), "norm_post": jnp.stack([g["norm_post"] for g in lg]),
        "fox_b_f": jnp.stack([lg[i]["mixer"]["b_f"][:, 0] for i in fox_layers]),
        "sconv_conv_w": sconv_g["conv_w"][None], "lru_conv_w": lru_g["conv_w"][None], "lru_conv_b": lru_g["conv_b"],
        "lru_w_a": lru_g["w_a"][None], "lru_b_a": lru_g["b_a"].reshape(1, LRU_BLOCKS, LRU_BLOCK_DIM),
        "lru_w_x": lru_g["w_x"][None], "lru_b_x": lru_g["b_x"].reshape(1, LRU_BLOCKS, LRU_BLOCK_DIM),
        "lru_lambda": lru_g["lam"]}
    g4 = _allgather8(_pack_rows(list(small_g.values())), "gather_small_grads").reshape(N_DEV, -1)
    last_start = to_chips(g4)
    summed = _sum_devices(g4, last_start, "sum_small_grads")[0]
    summed = dict(zip(small_g, _unpack(summed, [v.shape for v in small_g.values()])))
    grads = {n: (_my_columns(summed[n], chip) if n in _COL_SHARDED_SMALL else summed[n]) for n in _SMALL if n != "b_cond"}
    grads["b_cond"] = summed["dmod"].reshape(DEPTH, N_SUB * 3 * D_MODEL)

    dmod_all = (g4[:, :dmod.size] + last_start[0, 0]).reshape(N_DEV, DEPTH, N_SUB * 3 * D_MODEL)
    dmod_s = jnp.pad(_my_columns(dmod_all, chip).transpose(1, 0, 2), ((0, 0), (0, COND_PAD - N_DEV), (0, 0))).astype(BF16)
    c_t = jnp.pad(c_all.T, ((0, 0), (0, COND_PAD - N_DEV)))
    grads["w_cond"], d_cond, m_cond, v_cond = _cond_bwd_adamw(c_t, dmod_s, wts["w_cond"], mom["w_cond"],
                                                              var["w_cond"], "cond_bwd_adamw")

    big = [n for n, _ in _BIG]
    two_d = lambda a: a.reshape(-1, a.shape[-1])
    bufs = [lax.empty(wts[n].shape, F32) for n in big]
    updates = [[lax.empty(two_d(wts[n]).shape, F32) for _ in range(3)] for n in big]
    follows = d_cond
    for i, send_sems, recv_sems, parts, lands, homes in exchanges:
        parts, lands = _chip_send_wait(send_sems, recv_sems, parts, lands, follows, f"grads_chip_wait_l{i}")
        for k, (part, land, (o, lead, _)) in enumerate(zip(parts, lands, homes)):
            bufs[o] = _chip_sum(part, land, bufs[o], lead, place_idx, f"grads_chip_sum_l{i}_{k}")
        bufs = list(_pair_gather(bufs, homes, f"grads_pair_gather_l{i}"))
        for o in sorted({o for o, _, _ in homes}):
            n = big[o]
            starts = [sum(a * math.prod(wts[n].shape[d + 1:-1]) for d, a in enumerate(lead))
                      for oo, lead, _ in homes if oo == o]
            rows = wts[n].shape[-2]
            *updates[o], g_out = _adamw_rows(two_d(wts[n]), two_d(bufs[o]), two_d(mom[n]), two_d(var[n]), updates[o],
                                             min(starts), max(starts) + rows - min(starts), f"adamw_{n}_l{i}")
            bufs[o] = g_out.reshape(wts[n].shape)
        follows = updates[0][0]
    grads.update(zip(big, bufs))

    delta, new_m, new_v = {"w_cond": d_cond}, {"w_cond": m_cond}, {"w_cond": v_cond}
    for n, (d, nm, nv) in zip(big, updates):
        delta[n], new_m[n], new_v[n] = (a.reshape(wts[n].shape) for a in (d, nm, nv))
    shapes = [wts[n].shape for n in _SMALL]
    packed = [_pack_rows([src[n] for n in _SMALL]) for src in (wts, grads, mom, var)]
    for dst, out in zip((delta, new_m, new_v), _adamw(*packed, "adamw_small")):
        dst.update(zip(_SMALL, _unpack(out.reshape(-1), shapes)))

    return (loss, grad_x[None], *[grads[n] for n in _WEIGHTS], *[delta[n] for n in _WEIGHTS],
            *[new_m[n] for n in _WEIGHTS], *[new_v[n] for n in _WEIGHTS])


def kernel(x, c, w_cond, b_cond, norm_pre, norm_post, w_ffn_in, w_ffn_out, fox_w_in, fox_b_f, fox_w_out, sconv_w_in, sconv_conv_w, sconv_w_out, lru_w_in, lru_conv_w, lru_conv_b, lru_w_a, lru_b_a, lru_w_x, lru_b_x, lru_lambda, lru_w_out, loss_target, m_w_cond, m_b_cond, m_norm_pre, m_norm_post, m_w_ffn_in, m_w_ffn_out, m_fox_w_in, m_fox_b_f, m_fox_w_out, m_sconv_w_in, m_sconv_conv_w, m_sconv_w_out, m_lru_w_in, m_lru_conv_w, m_lru_conv_b, m_lru_w_a, m_lru_b_a, m_lru_w_x, m_lru_b_x, m_lru_lambda, m_lru_w_out, v_w_cond, v_b_cond, v_norm_pre, v_norm_post, v_w_ffn_in, v_w_ffn_out, v_fox_w_in, v_fox_b_f, v_fox_w_out, v_sconv_w_in, v_sconv_conv_w, v_sconv_w_out, v_lru_w_in, v_lru_conv_w, v_lru_conv_b, v_lru_w_a, v_lru_b_a, v_lru_w_x, v_lru_b_x, v_lru_lambda, v_lru_w_out):
    given = dict(locals())
    wts = {n: given[n] for n in _WEIGHTS}
    mom = {n: given["m_" + n] for n in _WEIGHTS}
    var = {n: given["v_" + n] for n in _WEIGHTS}
    return _step(x, c, loss_target, wts, mom, var)
```

```python
import functools
import math
from typing import NamedTuple

import jax
import jax.numpy as jnp
from jax import lax
from jax.experimental import pallas as pl
from jax.experimental.pallas import tpu as pltpu

F32 = jnp.float32
BF16 = jnp.bfloat16

D_MODEL = 1024
DEPTH = 4
N_SUB = 3
D_FF = 2816
RMS_EPS = 1e-6
FOX_HEADS = 16
FOX_HEAD_DIM = 64
FOX_PAD = 3200
LRU_BLOCKS = 16
LRU_BLOCK_DIM = 64
LRU_C = 8.0
N_CHIPS = 4
N_DEV = 8

ADAM_LR = 0.001
ADAM_B1 = 0.9
ADAM_B2 = 0.999
ADAM_EPS = 1e-08
ADAM_WD = 0.01
ADAM_STEP = 10

VMEM_LIMIT_V7X = 56 * 1024 * 1024
ROW_TILE = 512
COL_TILE = 256
ATT_TILE = 256
ATT_WIDE = 512
MM_ROWS = 1024


def _cparams(sem=None):
    return pltpu.CompilerParams(vmem_limit_bytes=VMEM_LIMIT_V7X, dimension_semantics=sem)


def _sigmoid(z):
    return 1.0 / (1.0 + jnp.exp(-z))


def _softplus(z):
    return jnp.maximum(z, 0.0) + jnp.log(1.0 + jnp.exp(-jnp.abs(z)))


def _rows_sum(v):
    return jnp.sum(v, axis=0, keepdims=True)


class _W(NamedTuple):
    arr: jax.Array
    prefix: tuple = ()
    blocked: bool = False


def _w_spec(w, block2, pos):
    lead = (None,) * (len(w.prefix) + (1 if w.blocked else 0))
    if w.blocked:
        return pl.BlockSpec(lead + block2, lambda *g: (pos(*g)[0], *w.prefix, pos(*g)[1], pos(*g)[2]))
    return pl.BlockSpec(lead + block2, lambda *g: (*w.prefix, pos(*g)[1], pos(*g)[2]))


def _mm_nn(a, b, name, tn=None, cols=None, out_dtype=F32):
    m, k = a.shape
    if b.blocked:
        steps, bn = b.arr.shape[0], b.arr.shape[-1]
        b_spec = _w_spec(b, (k, bn), lambda n: (n, 0, 0))
    else:
        first, last = (0, b.arr.shape[-1]) if cols is None else cols
        n_total = last - first
        bn = n_total if tn is None else tn
        steps = n_total // bn
        assert steps * bn == n_total and first % bn == 0
        b_spec = _w_spec(b, (k, bn), lambda n: (0, 0, n + first // bn))
    tm = min(MM_ROWS, m)

    def body(a_ref, b_ref, o_ref):
        def step(i, carry):
            r = pl.ds(pl.multiple_of(i * tm, tm), tm)
            o_ref[r, :] = jnp.dot(a_ref[r, :], b_ref[...], preferred_element_type=F32).astype(out_dtype)
            return carry
        lax.fori_loop(0, m // tm, step, 0)

    return pl.pallas_call(
        body, name=name, grid=(steps,),
        in_specs=[pl.BlockSpec((m, k), lambda n: (0, 0)), b_spec],
        out_specs=pl.BlockSpec((m, bn), lambda n: (0, n)),
        out_shape=jax.ShapeDtypeStruct((m, steps * bn), out_dtype),
        compiler_params=_cparams(("arbitrary",)),
    )(a, b.arr)


def _cols_shape(dy):
    return (dy.shape[0], dy.shape[1]) if dy.ndim == 2 else (dy.shape[1], 2 * dy.shape[2])


def _cols_spec(dy, bn):
    if dy.ndim == 2:
        return pl.BlockSpec((dy.shape[0], bn), lambda kt, n: (0, n))
    per = dy.shape[2] // bn
    assert per * bn == dy.shape[2]
    return pl.BlockSpec((None, dy.shape[1], bn), lambda kt, n: (n // per, 0, n % per))


def _mm_nt(dy, w, name, tk=None, tn=None):
    m, n_total = _cols_shape(dy)
    k = w.arr.shape[-2]
    if w.blocked:
        bk, bn = k, w.arr.shape[-1]
        grid = (1, w.arr.shape[0])
        w_spec = _w_spec(w, (k, bn), lambda kt, n: (n, 0, 0))
    else:
        bk = k if tk is None else tk
        bn = n_total if tn is None else tn
        grid = (k // bk, n_total // bn)
        assert grid[0] * bk == k and grid[1] * bn == n_total
        w_spec = _w_spec(w, (bk, bn), lambda kt, n: (0, kt, n))
    tm = min(MM_ROWS, m)

    reduce_steps = grid[1]

    def body(dy_ref, w_ref, o_ref):
        def step(i, carry):
            r = pl.ds(pl.multiple_of(i * tm, tm), tm)
            part = lax.dot_general(dy_ref[r, :], w_ref[...], (((1,), (1,)), ((), ())), preferred_element_type=F32)
            if reduce_steps == 1:
                o_ref[r, :] = part
            else:
                o_ref[r, :] += part
            return carry

        if reduce_steps > 1:
            @pl.when(pl.program_id(1) == 0)
            def _():
                o_ref[...] = jnp.zeros_like(o_ref)
        lax.fori_loop(0, m // tm, step, 0)

    return pl.pallas_call(
        body, name=name, grid=grid,
        in_specs=[_cols_spec(dy, bn), w_spec],
        out_specs=pl.BlockSpec((m, bk), lambda kt, n: (0, kt)),
        out_shape=jax.ShapeDtypeStruct((m, k), F32),
        compiler_params=_cparams(("arbitrary", "arbitrary")),
    )(dy, w.arr)


def _mm_tn(x, dy, name, tk=None, tn=None, blocked_out=False):
    s, k = x.shape
    n_total = _cols_shape(dy)[1]
    bk = k if tk is None else tk
    bn = n_total if tn is None else tn
    grid = (k // bk, n_total // bn)
    assert grid[0] * bk == k and grid[1] * bn == n_total
    ck = next(c for c in (512, 256, 128) if bk % c == 0)

    def body(x_ref, dy_ref, o_ref):
        def step(i, carry):
            c = pl.ds(pl.multiple_of(i * ck, ck), ck)
            o_ref[c, :] = lax.dot_general(x_ref[:, c], dy_ref[...], (((0,), (0,)), ((), ())),
                                          preferred_element_type=F32)
            return carry
        lax.fori_loop(0, bk // ck, step, 0)

    if blocked_out:
        assert grid[0] == 1
        out_spec = pl.BlockSpec((None, bk, bn), lambda kt, n: (n, 0, 0))
        out_shape = jax.ShapeDtypeStruct((grid[1], k, bn), F32)
    else:
        out_spec = pl.BlockSpec((bk, bn), lambda kt, n: (kt, n))
        out_shape = jax.ShapeDtypeStruct((k, n_total), F32)
    return pl.pallas_call(
        body, name=name, grid=grid,
        in_specs=[pl.BlockSpec((s, bk), lambda kt, n: (0, kt)), _cols_spec(dy, bn)],
        out_specs=out_spec, out_shape=out_shape,
        compiler_params=_cparams(("arbitrary", "arbitrary")),
    )(x, dy)


def _row_call(name, body, rows, fulls, row_outs, acc_outs, tr=ROW_TILE, after=None):
    s = rows[0].shape[0]
    tr = min(tr, s)
    in_specs = [pl.BlockSpec((tr, a.shape[1]), lambda i: (i, 0)) for a in rows]
    in_specs += [pl.BlockSpec(a.shape, lambda i: (0, 0)) for a in fulls]
    n_in = len(in_specs)
    order = [] if after is None else [after]
    in_specs += [pl.BlockSpec(memory_space=pl.ANY)] * len(order)
    out_specs = [pl.BlockSpec((tr, c), lambda i: (i, 0)) for c, _ in row_outs]
    out_specs += [pl.BlockSpec((1, c), lambda i: (0, 0)) for c, _ in acc_outs]
    out_shape = [jax.ShapeDtypeStruct((s, c), dt) for c, dt in row_outs]
    out_shape += [jax.ShapeDtypeStruct((1, c), dt) for c, dt in acc_outs]
    n_acc = len(acc_outs)

    def wrapped(*refs):
        refs = refs[:n_in] + refs[n_in + len(order):]
        if n_acc:
            @pl.when(pl.program_id(0) == 0)
            def _():
                for r in refs[len(refs) - n_acc:]:
                    r[...] = jnp.zeros_like(r)
        body(*refs)

    return pl.pallas_call(
        wrapped, name=name, grid=(s // tr,), in_specs=in_specs, out_specs=out_specs, out_shape=out_shape,
        compiler_params=_cparams(("arbitrary",)),
    )(*rows, *fulls, *order)


def _rms(v):
    return lax.rsqrt(jnp.mean(v * v, axis=-1, keepdims=True) + RMS_EPS)


def _pre_norm(x, g_pre, scale, shift, name, after=None):
    def body(x_ref, g_ref, sc_ref, sh_ref, h_ref):
        xv = x_ref[...]
        h = (xv * _rms(xv)) * g_ref[...] * (1.0 + sc_ref[...]) + sh_ref[...]
        h_ref[...] = h.astype(BF16)
    return _row_call(name, body, [x], [g_pre, scale, shift], [(D_MODEL, BF16)], [], after=after)[0]


def _post_norm(x, y, g_post, gate, coef, name):
    def body(x_ref, y_ref, g_ref, gate_ref, o_ref):
        yv = y_ref[...]
        o_ref[...] = x_ref[...] + (coef * gate_ref[...]) * ((yv * _rms(yv)) * g_ref[...])
    return _row_call(name, body, [x, y], [g_post, gate], [(D_MODEL, F32)], [])[0]


def _post_norm_bwd(dxo, y, g_post, gate, coef, name, after=None):
    def body(dxo_ref, y_ref, g_ref, gate_ref, dy_ref, dgate_ref, dg_ref):
        yv = y_ref[...]
        r2 = _rms(yv)
        yn = yv * r2
        dxo_v = dxo_ref[...]
        dgate_ref[...] += _rows_sum(dxo_v * (yn * g_ref[...])) * coef
        dz = dxo_v * (coef * gate_ref[...])
        dg_ref[...] += _rows_sum(dz * yn)
        dyn = dz * g_ref[...]
        dy = r2 * (dyn - yn * jnp.mean(dyn * yn, axis=-1, keepdims=True))
        dy_ref[...] = dy.astype(BF16)
    return _row_call(name, body, [dxo, y], [g_post, gate], [(D_MODEL, BF16)], [(D_MODEL, F32), (D_MODEL, F32)],
                     after=after)


def _pre_norm_bwd(dxo, dh, x, g_pre, scale, name):
    def body(dxo_ref, dh_ref, x_ref, g_ref, sc_ref, dx_ref, dshift_ref, dscale_ref, dg_ref):
        xv = x_ref[...]
        r = _rms(xv)
        xn = xv * r
        dh_v = dh_ref[...]
        one_sc = 1.0 + sc_ref[...]
        dshift_ref[...] += _rows_sum(dh_v)
        dscale_ref[...] += _rows_sum(dh_v * (xn * g_ref[...]))
        dg_ref[...] += _rows_sum(dh_v * xn * one_sc)
        dxn = dh_v * (g_ref[...] * one_sc)
        dx_ref[...] = dxo_ref[...] + r * (dxn - xn * jnp.mean(dxn * xn, axis=-1, keepdims=True))
    return _row_call(name, body, [dxo, dh, x], [g_pre, scale], [(D_MODEL, F32)],
                     [(D_MODEL, F32), (D_MODEL, F32), (D_MODEL, F32)])


FFN_COLS = 1408


def _ffn_in_act(h, w_in, name, after=None):
    m, k = h.shape
    half, bn = w_in.arr.shape[0] // 2, w_in.arr.shape[-1]
    assert bn == FFN_COLS and half * bn == D_FF
    tm = min(MM_ROWS, m)
    order = [] if after is None else [after]

    def body(h_ref, wg_ref, wu_ref, *rest):
        g_ref, u_ref, a_ref = rest[len(order):]
        g = jnp.dot(h_ref[...], wg_ref[...], preferred_element_type=F32)
        g_ref[...] = g
        u = jnp.dot(h_ref[...], wu_ref[...], preferred_element_type=F32)
        u_ref[...] = u
        a_ref[...] = (g * _sigmoid(g) * u).astype(BF16)

    tile = pl.BlockSpec((tm, bn), lambda t, i: (i, t))
    return pl.pallas_call(
        body, name=name, grid=(half, m // tm),
        in_specs=[pl.BlockSpec((tm, k), lambda t, i: (i, 0)),
                  _w_spec(w_in, (k, bn), lambda t, i: (t, 0, 0)),
                  _w_spec(w_in, (k, bn), lambda t, i: (half + t, 0, 0))] + [pl.BlockSpec(memory_space=pl.ANY)] * len(order),
        out_specs=[tile, tile, tile],
        out_shape=[jax.ShapeDtypeStruct((m, D_FF), F32)] * 2 + [jax.ShapeDtypeStruct((m, D_FF), BF16)],
        compiler_params=_cparams(("arbitrary", "arbitrary")),
    )(h, w_in.arr, w_in.arr, *order)


def _ffn_out_bx_act(dy, w_out, g, u, name):
    m = dy.shape[0]
    tr = min(MM_ROWS, m)

    def body(dy_ref, w_ref, g_ref, u_ref, dgu_ref):
        da = lax.dot_general(dy_ref[...], w_ref[...], _NT, preferred_element_type=F32)
        gv = g_ref[...]
        sg = _sigmoid(gv)
        dgu_ref[0] = (da * u_ref[...] * (sg * (1.0 + gv * (1.0 - sg)))).astype(BF16)
        dgu_ref[1] = (da * (gv * sg)).astype(BF16)

    tile = pl.BlockSpec((tr, FFN_COLS), lambda i, c: (i, c))
    return pl.pallas_call(
        body, name=name, grid=(m // tr, D_FF // FFN_COLS),
        in_specs=[pl.BlockSpec((tr, D_MODEL), lambda i, c: (i, 0)),
                  _w_spec(w_out, (FFN_COLS, D_MODEL), lambda i, c: (0, c, 0)), tile, tile],
        out_specs=pl.BlockSpec((2, tr, FFN_COLS), lambda i, c: (0, i, c)),
        out_shape=jax.ShapeDtypeStruct((2, m, D_FF), BF16),
        compiler_params=_cparams(("arbitrary", "arbitrary")),
    )(dy, w_out.arr, g, u)


def _ffn_in_bwd(dgu, h, w_in, name):
    m, k = h.shape
    nb, bn = w_in.arr.shape[0], w_in.arr.shape[-1]
    per = dgu.shape[2] // bn
    tm = min(MM_ROWS, m)
    ck = next(c for c in (512, 256, 128) if k % c == 0)
    once = pl.Buffered(1)

    def body(dgu_ref, h_ref, w_ref, dh_ref, dw_ref):
        @pl.when(pl.program_id(0) == 0)
        def _():
            dh_ref[...] = jnp.zeros_like(dh_ref)

        def rows(i, carry):
            r = pl.ds(pl.multiple_of(i * tm, tm), tm)
            dh_ref[r, :] += lax.dot_general(dgu_ref[r, :], w_ref[...], _NT, preferred_element_type=F32)
            return carry
        lax.fori_loop(0, m // tm, rows, 0)

        def cols(i, carry):
            c = pl.ds(pl.multiple_of(i * ck, ck), ck)
            dw_ref[c, :] = lax.dot_general(h_ref[:, c], dgu_ref[...], (((0,), (0,)), ((), ())),
                                           preferred_element_type=F32)
            return carry
        lax.fori_loop(0, k // ck, cols, 0)

    return pl.pallas_call(
        body, name=name, grid=(nb,),
        in_specs=[pl.BlockSpec((None, m, bn), lambda n: (n // per, 0, n % per)),
                  pl.BlockSpec((m, k), lambda n: (0, 0), pipeline_mode=once),
                  _w_spec(w_in, (k, bn), lambda n: (n, 0, 0))],
        out_specs=[pl.BlockSpec((m, k), lambda n: (0, 0), pipeline_mode=once),
                   pl.BlockSpec((None, k, bn), lambda n: (n, 0, 0))],
        out_shape=[jax.ShapeDtypeStruct((m, k), F32), jax.ShapeDtypeStruct((nb, k, bn), F32)],
        compiler_params=_cparams(("arbitrary",)),
    )(dgu, h, w_in.arr)


def _loss_head(y, target, name):
    def body(y_ref, t_ref, dy_ref, loss_ref):
        e = y_ref[...] - t_ref[...]
        dy_ref[...] = e * (1.0 / D_MODEL)
        part = jnp.sum(jnp.mean(e * e, axis=-1, keepdims=True), axis=0, keepdims=True) * 0.5
        loss_ref[...] += jnp.broadcast_to(part, loss_ref.shape)
    return _row_call(name, body, [y, target], [], [(D_MODEL, F32)], [(128, F32)])


def _lane_scan(v, reverse):
    s = v.shape[1]
    lane = lax.broadcasted_iota(jnp.int32, v.shape, 1)
    d = 1
    while d < s:
        if reverse:
            v = v + jnp.where(lane < s - d, pltpu.roll(v, s - d, 1), 0.0)
        else:
            v = v + jnp.where(lane >= d, pltpu.roll(v, d, 1), 0.0)
        d *= 2
    return v


def _fox_gate(flt, b_f, name):
    def body(f_ref, b_ref, cum_ref):
        z = f_ref[...] + b_ref[...]
        cum_ref[...] = _lane_scan(-_softplus(-z), reverse=False)
    return pl.pallas_call(body, name=name, out_shape=jax.ShapeDtypeStruct(flt.shape, F32),
                          compiler_params=_cparams())(flt, b_f)


def _fox_gate_bwd(dcum_q, dcum_k, flt, b_f, name):
    def body(dq_ref, dk_ref, f_ref, b_ref, df_ref, db_ref):
        z = f_ref[...] + b_ref[...]
        df = _lane_scan(dq_ref[...] + dk_ref[...], reverse=True) * _sigmoid(-z)
        df_ref[...] = df
        db_ref[...] = jnp.sum(df, axis=1, keepdims=True)
    h = flt.shape[0]
    return pl.pallas_call(body, name=name,
                          out_shape=(jax.ShapeDtypeStruct(flt.shape, F32), jax.ShapeDtypeStruct((h, 1), F32)),
                          compiler_params=_cparams())(dcum_q, dcum_k, flt, b_f)


def _pick_head(block, h):
    lane = lax.broadcasted_iota(jnp.int32, block.shape, 1)
    return jnp.sum(jnp.where(lane == h, block, 0.0), axis=1, keepdims=True)


def _put_head(ref, col, h):
    @pl.when(h == 0)
    def _():
        ref[...] = jnp.zeros_like(ref)
    lane = lax.broadcasted_iota(jnp.int32, ref.shape, 1)
    ref[...] = jnp.where(lane == h, col, ref[...])


_NT = (((1,), (1,)), ((), ()))
_FOX_SCALE = FOX_HEAD_DIM ** -0.5


HEAD_PAIRS = FOX_HEADS // 2
PAIR_W = 2 * FOX_HEAD_DIM


def _low_half(shape):
    return lax.broadcasted_iota(jnp.int32, shape, 1) < FOX_HEAD_DIM


def _fox_attn_fwd(qkv, cum, cum_t, name):
    s = qkv.shape[0]
    t = min(ATT_TILE, s)
    wide = min(ATT_WIDE, s)

    def body(q_ref, k_ref, v_ref, cum_ref, cumt_ref, o_ref, ob_ref, lse_ref):
        i = pl.program_id(0)
        hp = pl.program_id(1)
        lo = _low_half((t, PAIR_W))
        qv = q_ref[...]
        zero = jnp.zeros_like(qv)
        q2 = (jnp.where(lo, qv, zero), jnp.where(lo, zero, qv))
        cum_v = cum_ref[...]
        cq2 = (_pick_head(cum_v, 2 * hp), _pick_head(cum_v, 2 * hp + 1))

        def step(j, carry, masked):
            ks = pl.ds(pl.multiple_of(j * wide, wide), wide)
            kj = k_ref[ks, :]
            vj = v_ref[ks, :]
            out = []
            for e in range(2):
                m, l, acc = carry[e]
                sc = lax.dot_general(q2[e], kj, _NT, preferred_element_type=F32) * _FOX_SCALE
                sc = sc + cq2[e] - cumt_ref[e:e + 1, ks]
                if masked:
                    q_pos = i * t + lax.broadcasted_iota(jnp.int32, (t, wide), 0)
                    k_pos = j * wide + lax.broadcasted_iota(jnp.int32, (t, wide), 1)
                    sc = jnp.where(k_pos <= q_pos, sc, -jnp.inf)
                m_new = jnp.maximum(m, jnp.max(sc, axis=1, keepdims=True))
                alpha = jnp.exp(m - m_new)
                p = jnp.exp(sc - m_new)
                l = alpha * l + jnp.sum(p, axis=1, keepdims=True)
                acc = alpha * acc + jnp.dot(p.astype(BF16), vj, preferred_element_type=F32)
                out.append((m_new, l, acc))
            return tuple(out)

        one = (jnp.full((t, 1), -jnp.inf, F32), jnp.zeros((t, 1), F32), jnp.zeros((t, PAIR_W), F32))
        whole = (i * t) // wide
        carry = lax.fori_loop(0, whole, lambda j, c: step(j, c, False), (one, one))
        (m0, l0, a0), (m1, l1, a1) = step(whole, carry, True)
        o = jnp.where(lo, a0 / l0, a1 / l1)
        o_ref[...] = o
        ob_ref[...] = o.astype(BF16)
        _put_head(lse_ref, m0 + jnp.log(l0), 2 * hp)
        _put_head(lse_ref, m1 + jnp.log(l1), 2 * hp + 1)

    nat_tile = pl.BlockSpec((t, FOX_HEADS), lambda i, hp: (i, 0))
    out_tile = pl.BlockSpec((t, PAIR_W), lambda i, hp: (i, hp))
    return pl.pallas_call(
        body, name=name, grid=(s // t, HEAD_PAIRS),
        in_specs=[pl.BlockSpec((t, PAIR_W), lambda i, hp: (i, hp)),
                  pl.BlockSpec((s, PAIR_W), lambda i, hp: (0, HEAD_PAIRS + hp)),
                  pl.BlockSpec((s, PAIR_W), lambda i, hp: (0, 2 * HEAD_PAIRS + hp)),
                  nat_tile, pl.BlockSpec((None, 2, s), lambda i, hp: (hp, 0, 0))],
        out_specs=[out_tile, out_tile, nat_tile],
        out_shape=[jax.ShapeDtypeStruct((s, D_MODEL), F32), jax.ShapeDtypeStruct((s, D_MODEL), BF16),
                   jax.ShapeDtypeStruct((s, FOX_HEADS), F32)],
        compiler_params=_cparams(("arbitrary", "arbitrary")),
    )(qkv, qkv, qkv, cum, cum_t)


def _fox_delta(do, o, expand, name):
    def body(do_ref, o_ref, e_ref, d_ref):
        prod = do_ref[...] * o_ref[...]
        hi = prod.astype(BF16)
        lo = (prod - hi.astype(F32)).astype(BF16)
        tot = (jnp.dot(hi, e_ref[...], preferred_element_type=F32)
               + jnp.dot(lo, e_ref[...], preferred_element_type=F32))
        d_ref[...] = tot[:, :FOX_HEADS]
    return _row_call(name, body, [do, o], [expand], [(FOX_HEADS, F32)], [])[0]


def _fox_attn_bwd(qkv, do, cum, cum_t, lse_t, delta_t, name):
    s = qkv.shape[0]
    t = min(ATT_TILE, s)
    wide = min(ATT_WIDE, s)
    nq = s // t
    tn_dims = (((0,), (0,)), ((), ()))

    def body(q_ref, k_ref, v_ref, do_ref, cum_ref, cumt_ref, lset_ref, deltat_ref,
             dq_ref, dk_ref, dv_ref, dck_ref, dcq_ref):
        hp = pl.program_id(0)
        j = pl.program_id(1)

        @pl.when(j == 0)
        def _():
            dq_ref[...] = jnp.zeros_like(dq_ref)
            dcq_ref[...] = jnp.zeros_like(dcq_ref)
        dk_ref[...] = jnp.zeros_like(dk_ref)
        dv_ref[...] = jnp.zeros_like(dv_ref)

        lo = _low_half((t, PAIR_W))
        lane = lax.broadcasted_iota(jnp.int32, (t, PAIR_W), 1)
        kv = k_ref[...]
        vv = v_ref[...]
        zero = jnp.zeros_like(kv)
        k2 = (jnp.where(lo, kv, zero), jnp.where(lo, zero, kv))
        v2 = (jnp.where(lo, vv, zero), jnp.where(lo, zero, vv))
        cum_v = cum_ref[...]
        ck2 = (_pick_head(cum_v, 2 * hp), _pick_head(cum_v, 2 * hp + 1))

        def step(i, dck, masked):
            qs = pl.ds(pl.multiple_of(i * wide, wide), wide)
            qi = q_ref[qs, :]
            do_i = do_ref[qs, :].astype(BF16)
            dv_p, dk_p, dq_p = [], [], []
            for e in range(2):
                st = lax.dot_general(k2[e], qi, _NT, preferred_element_type=F32) * _FOX_SCALE
                st = st + cumt_ref[e:e + 1, qs] - ck2[e]
                if masked:
                    k_pos = j * t + lax.broadcasted_iota(jnp.int32, (t, wide), 0)
                    q_pos = i * wide + lax.broadcasted_iota(jnp.int32, (t, wide), 1)
                    st = jnp.where(k_pos <= q_pos, st, -jnp.inf)
                pt = jnp.exp(st - lset_ref[e:e + 1, qs])
                dv_p.append(jnp.dot(pt.astype(BF16), do_i, preferred_element_type=F32))
                dpt = lax.dot_general(v2[e], do_i, _NT, preferred_element_type=F32)
                dst = pt * (dpt - deltat_ref[e:e + 1, qs])
                dsb = dst.astype(BF16)
                dk_p.append(jnp.dot(dsb, qi, preferred_element_type=F32))
                dq_p.append(lax.dot_general(dsb, kv, tn_dims, preferred_element_type=F32))
                dck = dck - jnp.where(lane == e, jnp.sum(dst, axis=1, keepdims=True), 0.0)
                dcq_ref[e:e + 1, qs] += jnp.sum(dst, axis=0, keepdims=True)
            dv_ref[...] += jnp.where(lo, dv_p[0], dv_p[1])
            dk_ref[...] += jnp.where(lo, dk_p[0], dk_p[1])
            dq_ref[qs, :] += jnp.where(_low_half((wide, PAIR_W)), dq_p[0], dq_p[1]) * _FOX_SCALE
            return dck

        first = (j * t) // wide
        dck = step(first, jnp.zeros((t, PAIR_W), F32), True)
        dck = lax.fori_loop(first + 1, s // wide, lambda i, c: step(i, c, False), dck)
        dk_ref[...] = dk_ref[...] * _FOX_SCALE
        dck_ref[...] = dck

    pair_full = lambda part: pl.BlockSpec((s, PAIR_W), lambda hp, j: (0, part * HEAD_PAIRS + hp))
    pair_tile = lambda part: pl.BlockSpec((t, PAIR_W), lambda hp, j: (j, part * HEAD_PAIRS + hp))
    rows = pl.BlockSpec((None, 2, s), lambda hp, j: (hp, 0, 0))
    return pl.pallas_call(
        body, name=name, grid=(HEAD_PAIRS, nq),
        in_specs=[pair_full(0), pair_tile(1), pair_tile(2), pair_full(0),
                  pl.BlockSpec((t, FOX_HEADS), lambda hp, j: (j, 0)), rows, rows, rows],
        out_specs=[pair_full(0), pair_tile(0), pair_tile(0),
                   pl.BlockSpec((None, t, PAIR_W), lambda hp, j: (hp, j, 0)), rows],
        out_shape=[jax.ShapeDtypeStruct((s, D_MODEL), F32)] * 3
        + [jax.ShapeDtypeStruct((HEAD_PAIRS, s, PAIR_W), F32), jax.ShapeDtypeStruct((HEAD_PAIRS, 2, s), F32)],
        compiler_params=_cparams(("arbitrary", "arbitrary")),
    )(qkv, qkv, qkv, do, cum, cum_t, lse_t, delta_t)


def _shift_down(v, d):
    row = lax.broadcasted_iota(jnp.int32, v.shape, 0)
    return jnp.where(row >= d, pltpu.roll(v, d, 0), 0.0)


def _shift_up(v, d):
    s = v.shape[0]
    row = lax.broadcasted_iota(jnp.int32, v.shape, 0)
    return jnp.where(row < s - d, pltpu.roll(v, s - d, 0), 0.0)


def _conv_taps(v, cw_ref, width):
    out = cw_ref[width - 1:width, :] * v
    for k in range(width - 1):
        out = out + cw_ref[k:k + 1, :] * _shift_down(v, width - 1 - k)
    return out


def _conv_taps_bwd(dout, v, cw_ref, dcw_ref, width):
    dv = cw_ref[width - 1:width, :] * dout
    dcw_ref[width - 1:width, :] = _rows_sum(dout * v)
    for k in range(width - 1):
        d = width - 1 - k
        dv = dv + cw_ref[k:k + 1, :] * _shift_up(dout, d)
        dcw_ref[k:k + 1, :] = _rows_sum(dout * _shift_down(v, d))
    return dv


def _col_spec(s, tc, part=0):
    off = part * (D_MODEL // tc)
    return pl.BlockSpec((s, tc), lambda c: (0, c + off))


def _small_spec(rows, tc):
    return pl.BlockSpec((rows, tc), lambda c: (0, c))


def _col_call(name, body, in_arrays, in_specs, out_rows, s, tc):
    return pl.pallas_call(
        body, name=name, grid=(D_MODEL // tc,), in_specs=in_specs,
        out_specs=[pl.BlockSpec((r, tc), lambda c: (0, c)) for r, _ in out_rows],
        out_shape=[jax.ShapeDtypeStruct((r, D_MODEL), dt) for r, dt in out_rows],
        compiler_params=_cparams(("arbitrary",)),
    )(*in_arrays)


def _sconv_fwd(proj, conv_w, name):
    s = proj.shape[0]
    tc = COL_TILE

    def body(b_ref, c_ref, x_ref, cw_ref, y_ref):
        y_ref[...] = (b_ref[...] * _conv_taps(c_ref[...] * x_ref[...], cw_ref, 3)).astype(BF16)

    return _col_call(name, body, [proj, proj, proj, conv_w],
                     [_col_spec(s, tc, 0), _col_spec(s, tc, 1), _col_spec(s, tc, 2), _small_spec(3, tc)],
                     [(s, BF16)], s, tc)[0]


def _sconv_bwd(dy, proj, conv_w, name):
    s = proj.shape[0]
    tc = COL_TILE

    def body(dy_ref, b_ref, c_ref, x_ref, cw_ref, db_ref, dc_ref, dx_ref, dcw_ref):
        w = c_ref[...] * x_ref[...]
        dy_v = dy_ref[...]
        db_ref[...] = (dy_v * _conv_taps(w, cw_ref, 3)).astype(BF16)
        dw = _conv_taps_bwd(dy_v * b_ref[...], w, cw_ref, dcw_ref, 3)
        dc_ref[...] = (dw * x_ref[...]).astype(BF16)
        dx_ref[...] = (dw * c_ref[...]).astype(BF16)

    return _col_call(name, body, [dy, proj, proj, proj, conv_w],
                     [_col_spec(s, tc), _col_spec(s, tc, 0), _col_spec(s, tc, 1), _col_spec(s, tc, 2),
                      _small_spec(3, tc)],
                     [(s, BF16), (s, BF16), (s, BF16), (3, F32)], s, tc)


def _lru_conv(proj, conv_w, conv_b, name):
    s = proj.shape[0]
    tc = COL_TILE

    def body(x_ref, cw_ref, cb_ref, xb_ref, xbb_ref):
        xb = _conv_taps(x_ref[...], cw_ref, 4) + cb_ref[...]
        xb_ref[...] = xb
        xbb_ref[...] = xb.astype(BF16)

    return _col_call(name, body, [proj, conv_w, conv_b],
                     [_col_spec(s, tc, 1), _small_spec(4, tc), _small_spec(1, tc)],
                     [(s, F32), (s, BF16)], s, tc)


def _lru_conv_bwd(dxb1, dxb2, proj, conv_w, name):
    s = proj.shape[0]
    tc = COL_TILE

    def body(d1_ref, d2_ref, x_ref, cw_ref, dx_ref, dcw_ref, dcb_ref):
        dxb = d1_ref[...] + d2_ref[...]
        dcb_ref[...] = _rows_sum(dxb)
        dx_ref[...] = _conv_taps_bwd(dxb, x_ref[...], cw_ref, dcw_ref, 4).astype(BF16)

    return _col_call(name, body, [dxb1, dxb2, proj, conv_w],
                     [_col_spec(s, tc), _col_spec(s, tc), _col_spec(s, tc, 1), _small_spec(4, tc)],
                     [(s, BF16), (4, F32), (1, F32)], s, tc)


_GELU_C = math.sqrt(2.0 / math.pi)


def _gelu_parts(g):
    inner = _GELU_C * (g + 0.044715 * g * g * g)
    th = jnp.tanh(inner)
    val = 0.5 * g * (1.0 + th)
    der = 0.5 * (1.0 + th) + 0.5 * g * (1.0 - th * th) * (_GELU_C * (1.0 + 3.0 * 0.044715 * g * g))
    return val, der


def _lru_gates(pa_ref, px_ref, ba_ref, bx_ref, lam_ref):
    r = _sigmoid(pa_ref[...] + ba_ref[...])
    ig = _sigmoid(px_ref[...] + bx_ref[...])
    sp = _softplus(-lam_ref[...])
    log_a = (-LRU_C) * r * sp
    a = jnp.exp(log_a)
    z = 2.0 * log_a
    one_m_a2 = jnp.where(z > -1e-3, -(z * (1.0 + z * (0.5 + z * (1.0 / 6.0)))), 1.0 - jnp.exp(z))
    return r, ig, sp, a, jnp.sqrt(one_m_a2)


def _lru_scan(pre, xb, proj, b_a, b_x, lam, name):
    s = xb.shape[0]
    tc = COL_TILE

    def body(pa_ref, px_ref, xb_ref, g_ref, ba_ref, bx_ref, lam_ref, y_ref, hs_ref):
        _, ig, _, a, mult = _lru_gates(pa_ref, px_ref, ba_ref, bx_ref, lam_ref)
        b = mult * (ig * xb_ref[...])
        d = 1
        while d < s:
            row = lax.broadcasted_iota(jnp.int32, a.shape, 0)
            keep = row >= d
            b = b + a * jnp.where(keep, pltpu.roll(b, d, 0), 0.0)
            a = a * jnp.where(keep, pltpu.roll(a, d, 0), 1.0)
            d *= 2
        hs_ref[...] = b
        y_ref[...] = (b * _gelu_parts(g_ref[...])[0]).astype(BF16)

    return _col_call(name, body, [pre, pre, xb, proj, b_a, b_x, lam],
                     [_col_spec(s, tc, 0), _col_spec(s, tc, 1), _col_spec(s, tc), _col_spec(s, tc, 0),
                      _small_spec(1, tc), _small_spec(1, tc), _small_spec(1, tc)],
                     [(s, BF16), (s, F32)], s, tc)


def _lru_scan_bwd(dy, pre, xb, proj, hs, b_a, b_x, lam, name):
    s = xb.shape[0]
    tc = COL_TILE

    def body(dy_ref, pa_ref, px_ref, xb_ref, g_ref, hs_ref, ba_ref, bx_ref, lam_ref,
             dg_ref, dpa_ref, dpx_ref, dxb_ref, dba_ref, dbx_ref, dlam_ref):
        r, ig, sp, a, mult = _lru_gates(pa_ref, px_ref, ba_ref, bx_ref, lam_ref)
        gl, gl_der = _gelu_parts(g_ref[...])
        dy_v = dy_ref[...]
        hs_v = hs_ref[...]
        dg_ref[...] = (dy_v * hs_v * gl_der).astype(BF16)
        lam_t = dy_v * gl
        coef = _shift_up(a, 1)
        d = 1
        while d < s:
            row = lax.broadcasted_iota(jnp.int32, coef.shape, 0)
            keep = row < s - d
            lam_t = lam_t + coef * jnp.where(keep, pltpu.roll(lam_t, s - d, 0), 0.0)
            coef = coef * jnp.where(keep, pltpu.roll(coef, s - d, 0), 1.0)
            d *= 2
        xb_v = xb_ref[...]
        da = lam_t * _shift_down(hs_v, 1)
        dmult = lam_t * (ig * xb_v)
        dig = lam_t * mult * xb_v
        dxb_ref[...] = lam_t * mult * ig
        dlog_a = da * a - dmult * (a * a) / mult
        dr = dlog_a * ((-LRU_C) * sp)
        dsp = _rows_sum(dlog_a * ((-LRU_C) * r))
        dlam_ref[...] = -dsp * _sigmoid(-lam_ref[...])
        dpa = dr * r * (1.0 - r)
        dpx = dig * ig * (1.0 - ig)
        dba_ref[...] = _rows_sum(dpa)
        dbx_ref[...] = _rows_sum(dpx)
        dpa_ref[...] = dpa.astype(BF16)
        dpx_ref[...] = dpx.astype(BF16)

    return _col_call(name, body, [dy, pre, pre, xb, proj, hs, b_a, b_x, lam],
                     [_col_spec(s, tc), _col_spec(s, tc, 0), _col_spec(s, tc, 1), _col_spec(s, tc),
                      _col_spec(s, tc, 0), _col_spec(s, tc),
                      _small_spec(1, tc), _small_spec(1, tc), _small_spec(1, tc)],
                     [(s, BF16), (s, BF16), (s, BF16), (s, F32), (1, F32), (1, F32), (1, F32)], s, tc)


def _post_pre(x, y, g_post, gate, coef, g_pre, scale, shift, name):
    def body(x_ref, y_ref, gq_ref, gate_ref, gp_ref, sc_ref, sh_ref, xo_ref, h_ref):
        yv = y_ref[...]
        xo = x_ref[...] + (coef * gate_ref[...]) * ((yv * _rms(yv)) * gq_ref[...])
        xo_ref[...] = xo
        h_ref[...] = ((xo * _rms(xo)) * gp_ref[...] * (1.0 + sc_ref[...]) + sh_ref[...]).astype(BF16)
    return _row_call(name, body, [x, y], [g_post, gate, g_pre, scale, shift], [(D_MODEL, F32), (D_MODEL, BF16)], [])


def _pre_post_bwd(dxo, dh, x, g_pre, scale, y, g_post, gate, coef, name, after=None):
    def body(dxo_ref, dh_ref, x_ref, y_ref, gp_ref, sc_ref, gq_ref, gate_ref,
             dx_ref, dy_ref, dshift_ref, dscale_ref, dgp_ref, dgate_ref, dgq_ref):
        xv = x_ref[...]
        r = _rms(xv)
        xn = xv * r
        dh_v = dh_ref[...]
        one_sc = 1.0 + sc_ref[...]
        dshift_ref[...] += _rows_sum(dh_v)
        dscale_ref[...] += _rows_sum(dh_v * (xn * gp_ref[...]))
        dgp_ref[...] += _rows_sum(dh_v * xn * one_sc)
        dxn = dh_v * (gp_ref[...] * one_sc)
        dx = dxo_ref[...] + r * (dxn - xn * jnp.mean(dxn * xn, axis=-1, keepdims=True))
        dx_ref[...] = dx
        yv = y_ref[...]
        r2 = _rms(yv)
        yn = yv * r2
        dgate_ref[...] += _rows_sum(dx * (yn * gq_ref[...])) * coef
        dz = dx * (coef * gate_ref[...])
        dgq_ref[...] += _rows_sum(dz * yn)
        dyn = dz * gq_ref[...]
        dy_ref[...] = (r2 * (dyn - yn * jnp.mean(dyn * yn, axis=-1, keepdims=True))).astype(BF16)
    return _row_call(name, body, [dxo, dh, x, y], [g_pre, scale, g_post, gate], [(D_MODEL, F32), (D_MODEL, BF16)],
                     [(D_MODEL, F32)] * 5, after=after)


def _ffn_core(h, w_in, w_out, tag, after=None):
    g, u, a = _ffn_in_act(h, w_in, tag + "_in", after=after)
    y = _mm_nn(a, w_out, tag + "_out", tn=512)
    return y, (h, g, u, a)


def _ffn_core_bwd(dy, saved, w_in, w_out, tag):
    h, g, u, a = saved
    dw_out = _mm_tn(a, dy, tag + "_out_bw", tn=512)
    dgu = _ffn_out_bx_act(dy, w_out, g, u, tag + "_out_bx")
    dh, dw_in = _ffn_in_bwd(dgu, h, w_in, tag + "_in_b")
    return dh, dw_in, dw_out


def _pair_rows(v):
    return v.T.reshape(HEAD_PAIRS, 2, v.shape[0])


def _fox_fwd(h, p, tag):
    s = h.shape[0]
    qkv = _mm_nn(h, p["w_in"], tag + "_in", tn=768, cols=(0, 3 * D_MODEL), out_dtype=BF16)
    gates = _mm_nn(h, p["w_in"], tag + "_in_f", cols=(3 * D_MODEL, FOX_PAD))
    flt = gates[:, :FOX_HEADS].T
    cum_t = _fox_gate(flt, p["b_f"], tag + "_gate")
    cum = cum_t.T
    cum_t2 = cum_t.reshape(HEAD_PAIRS, 2, s)
    o, ob, lse = _fox_attn_fwd(qkv, cum, cum_t2, tag + "_attn")
    y = _mm_nn(ob, p["w_out"], tag + "_out")
    return y, (qkv, flt, cum, cum_t2, o, ob, lse)


def _fox_bwd(dy, h, saved, p, tag):
    qkv, flt, cum, cum_t2, o, ob, lse = saved
    s = h.shape[0]
    do = _mm_nt(dy, p["w_out"], tag + "_out_bx")
    dw_out = _mm_tn(ob, dy, tag + "_out_bw")
    expand = jnp.pad(jnp.repeat(jnp.eye(FOX_HEADS, dtype=BF16), FOX_HEAD_DIM, axis=0),
                     ((0, 0), (0, PAIR_W - FOX_HEADS)))
    delta = _fox_delta(do, o, expand, tag + "_attn_delta")
    dq, dk, dv, dck, dcq = _fox_attn_bwd(qkv, do, cum, cum_t2, _pair_rows(lse), _pair_rows(delta), tag + "_attn_b")
    dcum_k = dck[:, :, :2].transpose(0, 2, 1).reshape(FOX_HEADS, s)
    dflt, db_f = _fox_gate_bwd(dcq.reshape(FOX_HEADS, s), dcum_k, flt, p["b_f"], tag + "_gate_b")
    dproj = jnp.concatenate(
        [dq, dk, dv, dflt.T, jnp.zeros((s, FOX_PAD - 3 * D_MODEL - FOX_HEADS), F32)], axis=1).astype(BF16)
    dh = _mm_nt(dproj, p["w_in"], tag + "_in_bx", tn=640)
    dw_in = _mm_tn(h, dproj, tag + "_in_bw", tn=640)
    return dh, {"w_in": dw_in, "w_out": dw_out, "b_f": db_f}


def _sconv_mix_fwd(h, p, tag):
    proj = _mm_nn(h, p["w_in"], tag + "_in")
    yb = _sconv_fwd(proj, p["conv_w"], tag + "_conv")
    y = _mm_nn(yb, p["w_out"], tag + "_out")
    return y, (proj, yb)


def _sconv_mix_bwd(dy, h, saved, p, tag):
    proj, yb = saved
    dyb = _mm_nt(dy, p["w_out"], tag + "_out_bx")
    dw_out = _mm_tn(yb, dy, tag + "_out_bw")
    db, dc, dxv, dcw = _sconv_bwd(dyb, proj, p["conv_w"], tag + "_conv_b")
    dproj = jnp.concatenate([db, dc, dxv], axis=1)
    dh = _mm_nt(dproj, p["w_in"], tag + "_in_bx")
    dw_in = _mm_tn(h, dproj, tag + "_in_bw", tn=p["w_in"].arr.shape[-1], blocked_out=True)
    return dh, {"w_in": dw_in, "w_out": dw_out, "conv_w": dcw}


def _lru_mix_fwd(h, p, tag):
    proj = _mm_nn(h, p["w_in"], tag + "_in")
    xb, xbb = _lru_conv(proj, p["conv_w"], p["conv_b"], tag + "_conv")
    pre = _mm_nn(xbb, p["w_ax"], tag + "_gates", tn=D_MODEL)
    yb, hs = _lru_scan(pre, xb, proj, p["b_a"], p["b_x"], p["lam"], tag + "_scan")
    y = _mm_nn(yb, p["w_out"], tag + "_out")
    return y, (proj, xb, xbb, pre, yb, hs)


def _diag_blocks(m):
    return jnp.stack([m[LRU_BLOCK_DIM * n:LRU_BLOCK_DIM * (n + 1), LRU_BLOCK_DIM * n:LRU_BLOCK_DIM * (n + 1)]
                      for n in range(LRU_BLOCKS)])


def _lru_mix_bwd(dy, h, saved, p, tag):
    proj, xb, xbb, pre, yb, hs = saved
    dyb = _mm_nt(dy, p["w_out"], tag + "_out_bx")
    dw_out = _mm_tn(yb, dy, tag + "_out_bw")
    dg, dpa, dpx, dxb1, dba, dbx, dlam = _lru_scan_bwd(dyb, pre, xb, proj, hs, p["b_a"], p["b_x"], p["lam"],
                                                       tag + "_scan_b")
    dpre = jnp.concatenate([dpa, dpx], axis=1)
    dxb2 = _mm_nt(dpre, p["w_ax"], tag + "_gates_bx", tn=D_MODEL)
    dw_ax = _mm_tn(xbb, dpre, tag + "_gates_bw", tn=D_MODEL)
    dx0, dcw, dcb = _lru_conv_bwd(dxb1, dxb2, proj, p["conv_w"], tag + "_conv_b")
    dproj = jnp.concatenate([dg, dx0], axis=1)
    dh = _mm_nt(dproj, p["w_in"], tag + "_in_bx")
    dw_in = _mm_tn(h, dproj, tag + "_in_bw", tn=p["w_in"].arr.shape[-1], blocked_out=True)
    grads = {"w_in": dw_in, "w_out": dw_out, "conv_w": dcw, "conv_b": dcb,
             "w_a": _diag_blocks(dw_ax[:, :D_MODEL]), "w_x": _diag_blocks(dw_ax[:, D_MODEL:]),
             "b_a": dba, "b_x": dbx, "lam": dlam}
    return dh, grads


_MIXERS = ((_fox_fwd, _fox_bwd), (_sconv_mix_fwd, _sconv_mix_bwd), (_lru_mix_fwd, _lru_mix_bwd))


def _local_step(x, target, mod, layer_params, on_grads=None, on_mid=None, first_after=None):
    layers = []
    tape = []
    for i in range(DEPTH):
        lp = dict(layer_params(i, 0, x))
        layers.append(lp)
        row = lambda v: v[None, :]
        m = lambda sub, what: mod[i, sub, what][None, :]
        gp, gq = lp["norm_pre"], lp["norm_post"]
        h0 = _pre_norm(x, row(gp[0]), m(0, 1), m(0, 0), f"l{i}_ffn0_pre", after=first_after if i == 0 else None)
        y0, sv0 = _ffn_core(h0, lp["ffn_in"][0], lp["ffn_out"][0], f"l{i}_ffn0")
        x1, h1 = _post_pre(x, y0, row(gq[0]), m(0, 2), 0.5, row(gp[1]), m(1, 1), m(1, 0), f"l{i}_ffn0_post")
        lp.update(layer_params(i, 1, x1))
        y1, svm = _MIXERS[i % 3][0](h1, lp["mixer"], f"l{i}_mix")
        x2, h2 = _post_pre(x1, y1, row(gq[1]), m(1, 2), 1.0, row(gp[2]), m(2, 1), m(2, 0), f"l{i}_mix_post")
        second = layer_params(i, 2, x2)
        lp["ffn_in"] = lp["ffn_in"] + second["ffn_in"]
        lp["ffn_out"] = lp["ffn_out"] + second["ffn_out"]
        y2, sv2 = _ffn_core(h2, lp["ffn_in"][1], lp["ffn_out"][1], f"l{i}_ffn1", after=second.get("after"))
        x3 = _post_norm(x2, y2, row(gq[2]), m(2, 2), 0.5, f"l{i}_ffn1_post")
        tape.append((x, y0, sv0, x1, h1, y1, svm, x2, y2, sv2))
        x = x3
    dx, loss_row = _loss_head(x, target, "loss_head")

    layer_grads = [None] * DEPTH
    dmod = [None] * DEPTH
    after = None
    for i in reversed(range(DEPTH)):
        lp = layers[i]
        row = lambda v: v[None, :]
        m = lambda sub, what: mod[i, sub, what][None, :]
        gp, gq = lp["norm_pre"], lp["norm_post"]
        x0, y0, sv0, x1, h1, y1, svm, x2, y2, sv2 = tape[i]
        dy2, dgate2, dgq2 = _post_norm_bwd(dx, y2, row(gq[2]), m(2, 2), 0.5, f"l{i}_ffn1_post_b", after=after)
        dh2, dw_in1, dw_out1 = _ffn_core_bwd(dy2, sv2, lp["ffn_in"][1], lp["ffn_out"][1], f"l{i}_ffn1")
        after = on_mid(i, dh2) if on_mid is not None else None
        dx, dy1, dshift2, dscale2, dgp2, dgate1, dgq1 = _pre_post_bwd(
            dx, dh2, x2, row(gp[2]), m(2, 1), y1, row(gq[1]), m(1, 2), 1.0, f"l{i}_mix_post_b", after=after)
        dh1, mg = _MIXERS[i % 3][1](dy1, h1, svm, lp["mixer"], f"l{i}_mix")
        dx, dy0, dshift1, dscale1, dgp1, dgate0, dgq0 = _pre_post_bwd(
            dx, dh1, x1, row(gp[1]), m(1, 1), y0, row(gq[0]), m(0, 2), 0.5, f"l{i}_ffn0_post_b")
        dh0, dw_in0, dw_out0 = _ffn_core_bwd(dy0, sv0, lp["ffn_in"][0], lp["ffn_out"][0], f"l{i}_ffn0")
        dx, dshift0, dscale0, dgp0 = _pre_norm_bwd(dx, dh0, x0, row(gp[0]), m(0, 1), f"l{i}_ffn0_pre_b")
        dmod[i] = jnp.concatenate([dshift0, dscale0, dgate0, dshift1, dscale1, dgate1, dshift2, dscale2, dgate2],
                                  axis=0).reshape(N_SUB, 3, D_MODEL)
        layer_grads[i] = {"ffn_in": (dw_in0, dw_in1), "ffn_out": (dw_out0, dw_out1),
                          "norm_pre": jnp.concatenate([dgp0, dgp1, dgp2], axis=0),
                          "norm_post": jnp.concatenate([dgq0, dgq1, dgq2], axis=0), "mixer": mg}
        if on_grads is not None:
            after = on_grads(i, layer_grads[i], dx)
    return loss_row, dx, jnp.stack(dmod), layer_grads


COND_ROWS = 16
COND_PAD = 128


def _cond_fwd(c_pad, w_cond, b_shard, name):
    nl, d, n = w_cond.shape
    tn = 768

    def body(c_ref, w_ref, b_ref, o_ref):
        cv = c_ref[...]
        act = (cv * _sigmoid(cv)).astype(BF16)
        o_ref[...] = jnp.dot(act, w_ref[...].astype(BF16), preferred_element_type=F32) + b_ref[...]

    return pl.pallas_call(
        body, name=name, grid=(nl, n // tn),
        in_specs=[pl.BlockSpec((COND_ROWS, d), lambda i, j: (0, 0)),
                  pl.BlockSpec((None, d, tn), lambda i, j: (i, 0, j)),
                  pl.BlockSpec((None, 1, tn), lambda i, j: (i, 0, j))],
        out_specs=pl.BlockSpec((None, COND_ROWS, tn), lambda i, j: (i, 0, j)),
        out_shape=jax.ShapeDtypeStruct((nl, COND_ROWS, n), F32),
        compiler_params=_cparams(("arbitrary", "arbitrary")),
    )(c_pad, w_cond, b_shard)


def _adam_math(w, g, m, v):
    nm = ADAM_B1 * m + (1.0 - ADAM_B1) * g
    nv = ADAM_B2 * v + (1.0 - ADAM_B2) * (g * g)
    m_hat = nm / (1.0 - ADAM_B1 ** ADAM_STEP)
    v_hat = nv / (1.0 - ADAM_B2 ** ADAM_STEP)
    delta = (-ADAM_LR) * (m_hat / (jnp.sqrt(v_hat) + ADAM_EPS) + ADAM_WD * w)
    return delta, nm, nv


def _cond_bwd_adamw(c_t, dmod_s, w, m, v, name):
    nl, d, n = w.shape
    tn = 384
    blk = pl.BlockSpec((None, d, tn), lambda i, j: (i, 0, j))

    def body(c_ref, dm_ref, w_ref, m_ref, v_ref, g_ref, d_ref, nm_ref, nv_ref):
        cv = c_ref[...]
        g = jnp.dot((cv * _sigmoid(cv)).astype(BF16), dm_ref[...], preferred_element_type=F32)
        g_ref[...] = g
        d_ref[...], nm_ref[...], nv_ref[...] = _adam_math(w_ref[...], g, m_ref[...], v_ref[...])

    return pl.pallas_call(
        body, name=name, grid=(nl, n // tn),
        in_specs=[pl.BlockSpec((d, COND_PAD), lambda i, j: (0, 0)),
                  pl.BlockSpec((None, COND_PAD, tn), lambda i, j: (i, 0, j)), blk, blk, blk],
        out_specs=[blk] * 4, out_shape=[jax.ShapeDtypeStruct(w.shape, F32)] * 4,
        compiler_params=_cparams(("arbitrary", "arbitrary")),
    )(c_t, dmod_s, w, m, v)


def _adamw(w, g, m, v, name):
    rows, cols = w.shape
    tr = next(t for t in (256, 176, 128, 64, 32, 16, 8) if rows % t == 0)
    blk = pl.BlockSpec((tr, cols), lambda i: (i, 0))

    def body(w_ref, g_ref, m_ref, v_ref, d_ref, nm_ref, nv_ref):
        d_ref[...], nm_ref[...], nv_ref[...] = _adam_math(w_ref[...], g_ref[...], m_ref[...], v_ref[...])

    return pl.pallas_call(
        body, name=name, grid=(rows // tr,), in_specs=[blk] * 4, out_specs=[blk] * 3,
        out_shape=[jax.ShapeDtypeStruct(w.shape, F32)] * 3, compiler_params=_cparams(("arbitrary",)),
    )(w, g, m, v)


def _adamw_rows(w, g, m, v, outs, row0, nrows, name):
    cols = w.shape[1]
    tr = next(t for t in (512, 256, 128, 64, 32, 16, 8) if nrows % t == 0 and row0 % t == 0)
    blk = pl.BlockSpec((tr, cols), lambda i: (i + row0 // tr, 0))
    anywhere = pl.BlockSpec(memory_space=pl.ANY)

    def body(w_ref, g_ref, m_ref, v_ref, d_in, nm_in, nv_in, d_ref, nm_ref, nv_ref, g_out):
        d_ref[...], nm_ref[...], nv_ref[...] = _adam_math(w_ref[...], g_ref[...], m_ref[...], v_ref[...])

    return pl.pallas_call(
        body, name=name, grid=(nrows // tr,), in_specs=[blk] * 4 + [anywhere] * 3,
        out_specs=[blk] * 3 + [anywhere], out_shape=[jax.ShapeDtypeStruct(w.shape, F32)] * 4,
        input_output_aliases={4: 0, 5: 1, 6: 2, 1: 3}, compiler_params=_cparams(("arbitrary",)),
    )(w, g, m, v, *outs)


_MESH = pl.DeviceIdType.MESH
_ANY = pl.BlockSpec(memory_space=pl.ANY)


def _place():
    return lax.axis_index("x"), lax.axis_index("y"), lax.axis_index("c")


def _other_chips(x, y):
    return [(1 - x, y), (x, 1 - y), (1 - x, 1 - y)]


def _allgather8(block, name):
    m_per, n = block.shape

    def body(x_ref, out_ref, send_sems, recv_sems, local_sem):
        x, y, c = _place()
        me, sibling = (x, y, c), (x, y, 1 - c)
        chips = _other_chips(x, y)

        def rows(px, py, pc):
            return out_ref.at[pl.ds((4 * px + 2 * py + pc) * m_per, m_per), :]

        def copy(k, blk, to, src=None):
            return pltpu.make_async_remote_copy(
                src_ref=rows(*blk) if src is None else src, dst_ref=rows(*blk),
                send_sem=send_sems.at[k], recv_sem=recv_sems.at[k], device_id=to, device_id_type=_MESH)

        mine = pltpu.make_async_copy(x_ref, rows(*me), local_sem)
        mine.start()
        first = [copy(0, me, sibling, src=x_ref)]
        first += [copy(1 + j, me, (*chip, c), src=x_ref) for j, chip in enumerate(chips)]
        for cp in first:
            cp.start()
        passed = [copy(4 + j, (*chip, c), sibling) for j, chip in enumerate(chips)]
        for j, chip in enumerate(chips):
            copy(1 + j, (*chip, c), me).wait_recv()
            passed[j].start()
        copy(0, sibling, me).wait_recv()
        for j, chip in enumerate(chips):
            copy(4 + j, (*chip, 1 - c), me).wait_recv()
        for cp in first + passed:
            cp.wait_send()
        mine.wait()

    return pl.pallas_call(
        body, name=name, out_shape=jax.ShapeDtypeStruct((N_DEV * m_per, n), block.dtype),
        in_specs=[pl.BlockSpec(memory_space=pltpu.VMEM)], out_specs=pl.BlockSpec(memory_space=pltpu.VMEM),
        scratch_shapes=[pltpu.SemaphoreType.DMA((7,)), pltpu.SemaphoreType.DMA((7,)), pltpu.SemaphoreType.DMA],
        compiler_params=_cparams(),
    )(block)


def _split_axis(shape):
    return next(a for a, n in enumerate(shape) if n > 1)


_HBM = pl.BlockSpec(memory_space=pltpu.HBM)
_SEM = pl.BlockSpec(memory_space=pltpu.SEMAPHORE)
_SPLIT_COPY = pltpu.CompilerParams(has_side_effects=pltpu.SideEffectType.DATAFLOW_SIDE_EFFECTING)
_TOKEN = jax.ShapeDtypeStruct((8, 128), F32)


def _in_hbm(arrays):
    return [pltpu.with_memory_space_constraint(a, pltpu.HBM) for a in arrays]


class _Gathered(NamedTuple):
    shard_shape: tuple
    chip_axis: int

    @property
    def shape(self):
        return self.shard_shape[:self.chip_axis] + (N_CHIPS,) + self.shard_shape[self.chip_axis:]

    def half(self, ref, chip, pc):
        cut = _split_axis(self.shard_shape)
        n = self.shard_shape[cut] // 2
        idx = [slice(None)] * len(self.shard_shape)
        idx[cut] = pl.ds(pc * n, n)
        idx.insert(self.chip_axis, chip)
        return ref.at[tuple(idx)]


def _own_block_placed(shard, layout, chip):
    return lax.dynamic_update_slice_in_dim(lax.empty(layout.shape, shard.dtype),
                                           jnp.expand_dims(shard, layout.chip_axis), chip, axis=layout.chip_axis)


def _gather_copies(lands, layouts, send_sems, recv_sems):
    x, y, c = _place()
    out = []
    for t, (land, lay) in enumerate(zip(lands, layouts)):
        for j, (px, py) in enumerate(_other_chips(x, y)):
            def copy(chip, t=t, j=j, px=px, py=py, land=land, lay=lay):
                return pltpu.make_async_remote_copy(
                    src_ref=lay.half(land, chip, c), dst_ref=lay.half(land, chip, c),
                    send_sem=send_sems.at[3 * t + j], recv_sem=recv_sems.at[3 * t + j],
                    device_id=(px, py, c), device_id_type=_MESH)
            out.append((copy(2 * x + y), copy(2 * px + py)))
    return out


def _gather_start(lands, layouts, after, name):
    nt = len(lands)
    order = [] if after is None else [after]

    def body(*refs):
        land_refs = refs[:nt]
        send_sems, recv_sems = refs[nt + len(order):nt + len(order) + 2]
        token = refs[-1]
        for send, _ in _gather_copies(land_refs, layouts, send_sems, recv_sems):
            send.start()
        token[...] = jnp.zeros_like(token)

    out = pl.pallas_call(
        body, name=name,
        out_shape=(pltpu.SemaphoreType.DMA((3 * nt,)), pltpu.SemaphoreType.DMA((3 * nt,)),
                   *[pltpu.HBM(a.shape, a.dtype) for a in lands], _TOKEN),
        in_specs=[_HBM] * nt + [_ANY] * len(order),
        out_specs=(_SEM, _SEM, *[_HBM] * nt, pl.BlockSpec(memory_space=pltpu.VMEM)),
        input_output_aliases={t: 2 + t for t in range(nt)}, compiler_params=_SPLIT_COPY,
    )(*_in_hbm(lands), *order)
    return out[0], out[1], list(out[2:2 + nt]), out[-1]


def _gather_wait(send_sems, recv_sems, lands, layouts, after, name):
    nt = len(lands)

    def body(*refs):
        land_refs = refs[:nt]
        sems = refs[nt:nt + 2]
        for send, arrival in _gather_copies(land_refs, layouts, *sems):
            send.wait_send()
            arrival.wait_recv()

    return list(pl.pallas_call(
        body, name=name, out_shape=tuple(pltpu.HBM(a.shape, a.dtype) for a in lands),
        in_specs=[_HBM] * nt + [_SEM, _SEM, _ANY], out_specs=tuple([_HBM] * nt),
        input_output_aliases={t: t for t in range(nt)}, compiler_params=_SPLIT_COPY,
    )(*lands, send_sems, recv_sems, after))


def _gather_forward(lands, layouts, name):
    nt = len(lands)

    def body(*refs):
        outs = refs[nt:2 * nt]
        send_sems, recv_sems = refs[2 * nt:]
        x, y, c = _place()
        sends, arrivals = [], []
        for t, lay in enumerate(layouts):
            for j, (px, py) in enumerate(_other_chips(x, y)):
                for pc, group in ((c, sends), (1 - c, arrivals)):
                    part = lay.half(outs[t], 2 * px + py, pc)
                    group.append(pltpu.make_async_remote_copy(
                        src_ref=part, dst_ref=part, send_sem=send_sems.at[3 * t + j], recv_sem=recv_sems.at[3 * t + j],
                        device_id=(x, y, 1 - c), device_id_type=_MESH))
        for cp in sends:
            cp.start()
        for cp in arrivals:
            cp.wait_recv()
        for cp in sends:
            cp.wait_send()

    return list(pl.pallas_call(
        body, name=name, out_shape=[jax.ShapeDtypeStruct(a.shape, a.dtype) for a in lands],
        in_specs=[_ANY] * nt, out_specs=[_ANY] * nt, input_output_aliases={t: t for t in range(nt)},
        scratch_shapes=[pltpu.SemaphoreType.DMA((3 * nt,)), pltpu.SemaphoreType.DMA((3 * nt,))],
        compiler_params=_cparams(),
    )(*lands))


def _forward_copies(lands, layouts, send_sems, recv_sems):
    x, y, c = _place()
    out = []
    for t, (land, lay) in enumerate(zip(lands, layouts)):
        for j, (px, py) in enumerate(_other_chips(x, y)):
            def copy(pc, t=t, j=j, px=px, py=py, land=land, lay=lay):
                part = lay.half(land, 2 * px + py, pc)
                return pltpu.make_async_remote_copy(
                    src_ref=part, dst_ref=part, send_sem=send_sems.at[3 * t + j], recv_sem=recv_sems.at[3 * t + j],
                    device_id=(x, y, 1 - c), device_id_type=_MESH)
            out.append((copy(c), copy(1 - c)))
    return out


def _gather_forward_start(lands, layouts, name):
    nt = len(lands)

    def body(*refs):
        for send, _ in _forward_copies(refs[:nt], layouts, refs[nt], refs[nt + 1]):
            send.start()
        refs[-1][...] = jnp.zeros_like(refs[-1])

    out = pl.pallas_call(
        body, name=name,
        out_shape=(pltpu.SemaphoreType.DMA((3 * nt,)), pltpu.SemaphoreType.DMA((3 * nt,)),
                   *[pltpu.HBM(a.shape, a.dtype) for a in lands], _TOKEN),
        in_specs=[_HBM] * nt, out_specs=(_SEM, _SEM, *[_HBM] * nt, pl.BlockSpec(memory_space=pltpu.VMEM)),
        input_output_aliases={t: 2 + t for t in range(nt)}, compiler_params=_SPLIT_COPY,
    )(*_in_hbm(lands))
    return out[0], out[1], list(out[2:2 + nt]), out[-1]


def _gather_forward_wait(send_sems, recv_sems, lands, layouts, after, name):
    nt = len(lands)

    def body(*refs):
        for send, arrival in _forward_copies(refs[:nt], layouts, refs[nt], refs[nt + 1]):
            send.wait_send()
            arrival.wait_recv()

    return list(pl.pallas_call(
        body, name=name, out_shape=tuple(pltpu.HBM(a.shape, a.dtype) for a in lands),
        in_specs=[_HBM] * nt + [_SEM, _SEM, _ANY], out_specs=tuple([_HBM] * nt),
        input_output_aliases={t: t for t in range(nt)}, compiler_params=_SPLIT_COPY,
    )(*lands, send_sems, recv_sems, after))


def _pair_copies(grads, lands, send_sems, recv_sems):
    x, y, c = _place()
    out = []
    for t, (g, land) in enumerate(zip(grads, lands)):
        h = g.shape[1] // 2
        out.append(pltpu.make_async_remote_copy(
            src_ref=g.at[:, pl.ds((1 - c) * h, h), :], dst_ref=land, send_sem=send_sems.at[t],
            recv_sem=recv_sems.at[t], device_id=(x, y, 1 - c), device_id_type=_MESH))
    return out


def _pair_start(grads, after, name):
    nt = len(grads)
    lands = [lax.empty((N_CHIPS, g.shape[1] // 2, g.shape[2]), g.dtype) for g in grads]
    order = [] if after is None else [after]

    def body(*refs):
        send_sems, recv_sems = refs[2 * nt + len(order):2 * nt + len(order) + 2]
        token = refs[-1]
        for cp in _pair_copies(refs[:nt], refs[nt:2 * nt], send_sems, recv_sems):
            cp.start()
        token[...] = jnp.zeros_like(token)

    out = pl.pallas_call(
        body, name=name,
        out_shape=(pltpu.SemaphoreType.DMA((nt,)), pltpu.SemaphoreType.DMA((nt,)),
                   *[pltpu.HBM(a.shape, a.dtype) for a in grads + lands], _TOKEN),
        in_specs=[_HBM] * (2 * nt) + [_ANY] * len(order),
        out_specs=(_SEM, _SEM, *[_HBM] * (2 * nt), pl.BlockSpec(memory_space=pltpu.VMEM)),
        input_output_aliases={t: 2 + t for t in range(2 * nt)}, compiler_params=_SPLIT_COPY,
    )(*_in_hbm(grads + lands), *order)
    return out[0], out[1], list(out[2:2 + nt]), list(out[2 + nt:2 + 2 * nt]), out[-1]


def _pair_wait(send_sems, recv_sems, grads, lands, after, name):
    nt = len(grads)

    def body(*refs):
        for cp in _pair_copies(refs[:nt], refs[nt:2 * nt], *refs[2 * nt:2 * nt + 2]):
            cp.wait_send()
            cp.wait_recv()

    out = pl.pallas_call(
        body, name=name, out_shape=tuple(pltpu.HBM(a.shape, a.dtype) for a in grads + lands),
        in_specs=[_HBM] * (2 * nt) + [_SEM, _SEM, _ANY], out_specs=tuple([_HBM] * (2 * nt)),
        input_output_aliases={t: t for t in range(2 * nt)}, compiler_params=_SPLIT_COPY,
    )(*grads, *lands, send_sems, recv_sems, after)
    return list(out[:nt]), list(out[nt:])


def _pair_sum(own, recv, c_idx, name):
    _, h, cols = recv.shape

    def body(c_ref, own_ref, recv_ref, o_ref):
        o_ref[...] = (own_ref[...] + recv_ref[...]).astype(BF16)

    return pl.pallas_call(
        body, name=name,
        grid_spec=pltpu.PrefetchScalarGridSpec(
            num_scalar_prefetch=1, grid=(N_CHIPS,),
            in_specs=[pl.BlockSpec((None, h, cols), lambda k, c_ref: (k, c_ref[0], 0)),
                      pl.BlockSpec((None, h, cols), lambda k, c_ref: (k, 0, 0))],
            out_specs=pl.BlockSpec((None, h, cols), lambda k, c_ref: (k, 0, 0))),
        out_shape=jax.ShapeDtypeStruct(recv.shape, BF16), compiler_params=_cparams(("arbitrary",)),
    )(c_idx, own, recv)


def _chip_copies(parts, lands, send_sems, recv_sems):
    x, y, c = _place()
    out = []
    for t, (part, land) in enumerate(zip(parts, lands)):
        for j, (px, py) in enumerate(_other_chips(x, y)):
            out.append(pltpu.make_async_remote_copy(
                src_ref=part.at[2 * px + py], dst_ref=land.at[j], send_sem=send_sems.at[3 * t + j],
                recv_sem=recv_sems.at[3 * t + j], device_id=(px, py, c), device_id_type=_MESH))
    return out


def _chip_send_start(parts, after, name):
    nt = len(parts)
    lands = [lax.empty((N_CHIPS - 1,) + p.shape[1:], p.dtype) for p in parts]
    order = [] if after is None else [after]

    def body(*refs):
        send_sems, recv_sems = refs[2 * nt + len(order):2 * nt + len(order) + 2]
        token = refs[-1]
        for cp in _chip_copies(refs[:nt], refs[nt:2 * nt], send_sems, recv_sems):
            cp.start()
        token[...] = jnp.zeros_like(token)

    out = pl.pallas_call(
        body, name=name,
        out_shape=(pltpu.SemaphoreType.DMA((3 * nt,)), pltpu.SemaphoreType.DMA((3 * nt,)),
                   *[pltpu.HBM(a.shape, a.dtype) for a in parts + lands], _TOKEN),
        in_specs=[_HBM] * (2 * nt) + [_ANY] * len(order),
        out_specs=(_SEM, _SEM, *[_HBM] * (2 * nt), pl.BlockSpec(memory_space=pltpu.VMEM)),
        input_output_aliases={t: 2 + t for t in range(2 * nt)}, compiler_params=_SPLIT_COPY,
    )(*_in_hbm(parts + lands), *order)
    return out[0], out[1], list(out[2:2 + nt]), list(out[2 + nt:2 + 2 * nt]), out[-1]


def _chip_send_wait(send_sems, recv_sems, parts, lands, after, name):
    nt = len(parts)

    def body(*refs):
        for cp in _chip_copies(refs[:nt], refs[nt:2 * nt], *refs[2 * nt:2 * nt + 2]):
            cp.wait_send()
            cp.wait_recv()

    out = pl.pallas_call(
        body, name=name, out_shape=tuple(pltpu.HBM(a.shape, a.dtype) for a in parts + lands),
        in_specs=[_HBM] * (2 * nt) + [_SEM, _SEM, _ANY], out_specs=tuple([_HBM] * (2 * nt)),
        input_output_aliases={t: t for t in range(2 * nt)}, compiler_params=_SPLIT_COPY,
    )(*parts, *lands, send_sems, recv_sems, after)
    return list(out[:nt]), list(out[nt:])


def _chip_sum(part, arrived, into, lead, place_idx, name):
    _, h, cols = part.shape

    def body(idx_ref, own_ref, arr_ref, into_ref, o_ref):
        acc = own_ref[...].astype(F32)
        for k in range(N_CHIPS - 1):
            acc = acc + arr_ref[k].astype(F32)
        o_ref[...] = acc

    return pl.pallas_call(
        body, name=name,
        grid_spec=pltpu.PrefetchScalarGridSpec(
            num_scalar_prefetch=1, grid=(1,),
            in_specs=[pl.BlockSpec((None, h, cols), lambda g, idx: (idx[1], 0, 0)),
                      pl.BlockSpec((N_CHIPS - 1, h, cols), lambda g, idx: (0, 0, 0)), _ANY],
            out_specs=pl.BlockSpec((None,) * len(lead) + (h, cols), lambda g, idx: (*lead, idx[0], 0))),
        out_shape=jax.ShapeDtypeStruct(into.shape, F32), input_output_aliases={3: 0},
        compiler_params=_cparams(("arbitrary",)),
    )(place_idx, part, arrived, into)


def _pair_gather(bufs, homes, name):
    nt, nb = len(homes), len(bufs)

    def body(*refs):
        outs = refs[nb:2 * nb]
        send_sems, recv_sems = refs[2 * nb:]
        x, y, c = _place()

        def home(t, pc):
            o, lead, rows = homes[t]
            return outs[o].at[(*lead, pl.ds(pc * (rows // 2), rows // 2), slice(None))]

        def copy(t, pc):
            return pltpu.make_async_remote_copy(src_ref=home(t, pc), dst_ref=home(t, pc), send_sem=send_sems.at[t],
                                                recv_sem=recv_sems.at[t], device_id=(x, y, 1 - c), device_id_type=_MESH)

        sends = [copy(t, c) for t in range(nt)]
        for cp in sends:
            cp.start()
        for t in range(nt):
            copy(t, 1 - c).wait_recv()
        for cp in sends:
            cp.wait_send()

    return pl.pallas_call(
        body, name=name, out_shape=[jax.ShapeDtypeStruct(b.shape, b.dtype) for b in bufs],
        in_specs=[_ANY] * nb, out_specs=[_ANY] * nb, input_output_aliases={o: o for o in range(nb)},
        scratch_shapes=[pltpu.SemaphoreType.DMA((nt,)), pltpu.SemaphoreType.DMA((nt,))],
        compiler_params=_cparams(),
    )(*bufs)


def _sum_devices(g, after, name):
    def body(g_ref, after_ref, o_ref):
        acc = g_ref[0:1, :]
        for d in range(1, N_DEV):
            acc = acc + g_ref[d:d + 1, :]
        o_ref[...] = acc
    vmem = pl.BlockSpec(memory_space=pltpu.VMEM)
    return pl.pallas_call(body, name=name, out_shape=jax.ShapeDtypeStruct((1, g.shape[1]), F32),
                          in_specs=[vmem, _ANY], out_specs=vmem, compiler_params=_cparams())(g, after)


_WEIGHTS = ("w_cond", "b_cond", "norm_pre", "norm_post", "w_ffn_in", "w_ffn_out", "fox_w_in", "fox_b_f",
            "fox_w_out", "sconv_w_in", "sconv_conv_w", "sconv_w_out", "lru_w_in", "lru_conv_w", "lru_conv_b",
            "lru_w_a", "lru_b_a", "lru_w_x", "lru_b_x", "lru_lambda", "lru_w_out")
_BIG = (("w_ffn_in", False), ("w_ffn_out", True), ("fox_w_in", False), ("fox_w_out", True),
        ("sconv_w_in", False), ("sconv_w_out", True), ("lru_w_in", False), ("lru_w_out", True))
_SMALL = tuple(n for n in _WEIGHTS if n != "w_cond" and n not in dict(_BIG))
_COL_SHARDED_SMALL = ("norm_pre", "norm_post", "sconv_conv_w", "lru_conv_w", "lru_conv_b", "lru_lambda")


def _pack_rows(parts, rows=8):
    flat = jnp.concatenate([p.reshape(-1) for p in parts])
    width = -(-flat.size // (rows * 128)) * 128
    return jnp.pad(flat, (0, rows * width - flat.size)).reshape(rows, width)


def _unpack(flat, shapes):
    out, off = [], 0
    for shp in shapes:
        n = math.prod(shp)
        out.append(flat[off:off + n].reshape(shp))
        off += n
    return out


def _join_chips(g):
    g = jnp.moveaxis(g, 0, -2)
    return g.reshape(g.shape[:-2] + (g.shape[-2] * g.shape[-1],))


def _my_columns(full, chip):
    n = full.shape[-1] // N_CHIPS
    return lax.dynamic_slice_in_dim(full, chip * n, n, axis=full.ndim - 1)


def _block_diag(w):
    eye = jnp.eye(LRU_BLOCKS, dtype=w.dtype)
    return jnp.einsum("nij,nm->nimj", w, eye).reshape(D_MODEL, D_MODEL)


def _step(x, c, target, wts, mom, var):
    ix, iy, ic = _place()
    chip = 2 * ix + iy
    dev = 2 * chip + ic
    n_cond = wts["w_cond"].shape[2]

    small_shapes = [(D_MODEL,)] + [wts[n].shape for n in _COL_SHARDED_SMALL]
    g1 = _allgather8(_pack_rows([c[0]] + [wts[n] for n in _COL_SHARDED_SMALL]), "gather_small").reshape(N_DEV, -1)
    c_all = g1[:, :D_MODEL]
    per_chip = [jnp.stack(col) for col in zip(*[_unpack(g1[2 * k], small_shapes) for k in range(N_CHIPS)])]
    small_full = {n: _join_chips(v) for n, v in zip(_COL_SHARDED_SMALL, per_chip[1:])}

    c_pad = jnp.pad(c_all, ((0, COND_ROWS - N_DEV), (0, 0)))
    b_shard = _my_columns(wts["b_cond"], chip)[:, None, :]
    mod_part = _cond_fwd(c_pad, wts["w_cond"], b_shard, "cond_fwd")
    g2 = _allgather8(mod_part[:, :N_DEV].transpose(1, 0, 2).reshape(N_DEV, DEPTH * n_cond), "gather_mod")
    g2 = g2.reshape(N_DEV, N_DEV, DEPTH, n_cond)[0::2]
    mod = _join_chips(lax.dynamic_index_in_dim(g2, dev, axis=1, keepdims=False)).reshape(DEPTH, N_SUB, 3, D_MODEL)

    mixer_names = [("fox_w_in", "fox_w_out"), ("sconv_w_in", "sconv_w_out"), ("lru_w_in", "lru_w_out")]

    def shards_of(i, sub):
        if sub == 1:
            return [wts[n][i // 3] for n in mixer_names[i % 3]]
        return [wts["w_ffn_in"][i, sub // 2], wts["w_ffn_out"][i, sub // 2]]

    chunks = [[(0, sub)] for sub in range(N_SUB)] + [[(i, sub) for sub in range(N_SUB)] for i in range(1, DEPTH)]
    in_flight, chunk_of, token = [], {}, mod
    for k, members in enumerate(chunks):
        shards = [s for i, sub in members for s in shards_of(i, sub)]
        layouts = [_Gathered(s.shape, 0) for s in shards]
        if k:
            shards = [s + token[0, 0] for s in shards]
        lands = [_own_block_placed(s.astype(BF16), lay, chip) for s, lay in zip(shards, layouts)]
        send_sems, recv_sems, lands, token = _gather_start(lands, layouts, token, f"gather_start_{k}")
        in_flight.append([send_sems, recv_sems, lands, layouts, False])
        chunk_of.update({m: (k, 2 * pos) for pos, m in enumerate(members)})
    lru_ax = jnp.concatenate([_block_diag(wts["lru_w_a"][0]), _block_diag(wts["lru_w_x"][0])], axis=1).astype(BF16)

    prefetch_at = {(i, N_SUB - 1): i + N_SUB for i in range(DEPTH - 1)}

    def layer_params(i, sub, x_in):
        k, pos = chunk_of[(i, sub)]
        send_sems, recv_sems, lands, layouts, state = in_flight[k]
        if state == "passing":
            in_flight[k][2:] = [_gather_forward_wait(send_sems, recv_sems, lands, layouts, x_in, f"gather_pass_wait_{k}"),
                                layouts, "here"]
        elif state != "here":
            lands = _gather_wait(send_sems, recv_sems, lands, layouts, x_in, f"gather_wait_{k}")
            in_flight[k][2:] = [_gather_forward(lands, layouts, f"gather_forward_{k}"), layouts, "here"]
        nxt = prefetch_at.get((i, sub))
        started = None
        if nxt is not None:
            send_sems, recv_sems, lands, layouts, _ = in_flight[nxt]
            lands = _gather_wait(send_sems, recv_sems, lands, layouts, x_in, f"gather_wait_{nxt}")
            send_sems, recv_sems, lands, started = _gather_forward_start(lands, layouts, f"gather_pass_start_{nxt}")
            in_flight[nxt] = [send_sems, recv_sems, lands, layouts, "passing"]
        w_in, w_out = in_flight[k][2][pos:pos + 2]
        w_out = w_out.reshape(-1, w_out.shape[-1])
        if sub != 1:
            out = {"ffn_in": [_W(w_in, (), True)], "ffn_out": [_W(w_out)], "after": started}
            if sub == 0:
                out.update(norm_pre=small_full["norm_pre"][i], norm_post=small_full["norm_post"][i])
            return out
        j = i // 3
        if i % 3 == 0:
            w_in = jnp.pad(_join_chips(w_in), ((0, 0), (0, FOX_PAD - 3 * D_MODEL - FOX_HEADS)))
            return {"mixer": {"w_in": _W(w_in), "w_out": _W(w_out), "b_f": wts["fox_b_f"][j][:, None]}}
        if i % 3 == 1:
            return {"mixer": {"w_in": _W(w_in, (), True), "w_out": _W(w_out), "conv_w": small_full["sconv_conv_w"][j]}}
        return {"mixer": {"w_in": _W(w_in, (), True), "w_out": _W(w_out), "conv_w": small_full["lru_conv_w"][j],
                          "conv_b": small_full["lru_conv_b"], "w_ax": _W(lru_ax),
                          "b_a": wts["lru_b_a"].reshape(1, D_MODEL), "b_x": wts["lru_b_x"].reshape(1, D_MODEL),
                          "lam": small_full["lru_lambda"]}}

    place_idx = jnp.stack([ic, chip]).astype(jnp.int32)
    c_idx = place_idx[:1]
    big_index = {n: o for o, (n, _) in enumerate(_BIG)}
    exchanges, pending = [], []

    def to_chips(after):
        i, send_sems, recv_sems, tensors, lands, homes = pending.pop()
        tensors, recv = _pair_wait(send_sems, recv_sems, tensors, lands, after, f"grads_pair_wait_l{i}")
        parts = [_pair_sum(t, r, c_idx, f"grads_pair_sum_l{i}_{k}") for k, (t, r) in enumerate(zip(tensors, recv))]
        send_sems, recv_sems, parts, lands, tok = _chip_send_start(parts, None, f"grads_chip_start_l{i}")
        exchanges.append((i, send_sems, recv_sems, parts, lands, homes))
        return tok

    def chip_blocks(g, by_rows, width):
        if by_rows:
            return g.reshape(N_CHIPS, g.shape[0] // N_CHIPS, g.shape[1])
        if g.ndim == 3:
            return g
        return g[:, :width * N_CHIPS].reshape(g.shape[0], N_CHIPS, width).transpose(1, 0, 2)

    def on_mid(i, dx):
        return to_chips(dx) if pending else None

    def on_grads(i, g, dx):
        n_in, n_out = mixer_names[i % 3]
        items = [("w_ffn_in", (i, k), g["ffn_in"][k]) for k in range(2)]
        items += [("w_ffn_out", (i, k), g["ffn_out"][k]) for k in range(2)]
        items += [(n_in, (i // 3,), g["mixer"]["w_in"]), (n_out, (i // 3,), g["mixer"]["w_out"])]
        tensors = [chip_blocks(t, dict(_BIG)[n], wts[n].shape[-1]) for n, _, t in items]
        homes = [(big_index[n], lead, wts[n].shape[-2]) for n, lead, _ in items]
        send_sems, recv_sems, tensors, lands, tok = _pair_start(tensors, None, f"grads_pair_start_l{i}")
        pending.append((i, send_sems, recv_sems, tensors, lands, homes))
        pair_tokens.append(tok)
        return tok

    pair_tokens = []
    loss_row, grad_x, dmod, lg = _local_step(x[0], target[0], mod, layer_params, on_grads, on_mid, token)
    loss = lax.psum(loss_row[0, 0], ("x", "y", "c"))
    dmod = dmod + pair_tokens[-1][0, 0]

    fox_layers = [i for i in range(DEPTH) if i % 3 == 0]
    sconv_g, lru_g = lg[1]["mixer"], lg[2]["mixer"]
    small_g = {
        "dmod": dmod, "norm_pre": jnp.stack([g["norm_pre"] for g in lg]), "norm_post": jnp.stack([g["norm_post"] for g in lg]),
        "fox_b_f": jnp.stack([lg[i]["mixer"]["b_f"][:, 0] for i in fox_layers]),
        "sconv_conv_w": sconv_g["conv_w"][None], "lru_conv_w": lru_g["conv_w"][None], "lru_conv_b": lru_g["conv_b"],
        "lru_w_a": lru_g["w_a"][None], "lru_b_a": lru_g["b_a"].reshape(1, LRU_BLOCKS, LRU_BLOCK_DIM),
        "lru_w_x": lru_g["w_x"][None], "lru_b_x": lru_g["b_x"].reshape(1, LRU_BLOCKS, LRU_BLOCK_DIM),
        "lru_lambda": lru_g["lam"]}
    g4 = _allgather8(_pack_rows(list(small_g.values())), "gather_small_grads").reshape(N_DEV, -1)
    last_start = to_chips(g4)
    summed = _sum_devices(g4, last_start, "sum_small_grads")[0]
    summed = dict(zip(small_g, _unpack(summed, [v.shape for v in small_g.values()])))
    grads = {n: (_my_columns(summed[n], chip) if n in _COL_SHARDED_SMALL else summed[n]) for n in _SMALL if n != "b_cond"}
    grads["b_cond"] = summed["dmod"].reshape(DEPTH, N_SUB * 3 * D_MODEL)

    dmod_all = (g4[:, :dmod.size] + last_start[0, 0]).reshape(N_DEV, DEPTH, N_SUB * 3 * D_MODEL)
    dmod_s = jnp.pad(_my_columns(dmod_all, chip).transpose(1, 0, 2), ((0, 0), (0, COND_PAD - N_DEV), (0, 0))).astype(BF16)
    c_t = jnp.pad(c_all.T, ((0, 0), (0, COND_PAD - N_DEV)))
    grads["w_cond"], d_cond, m_cond, v_cond = _cond_bwd_adamw(c_t, dmod_s, wts["w_cond"], mom["w_cond"],
                                                              var["w_cond"], "cond_bwd_adamw")

    big = [n for n, _ in _BIG]
    two_d = lambda a: a.reshape(-1, a.shape[-1])
    bufs = [lax.empty(wts[n].shape, F32) for n in big]
    updates = [[lax.empty(two_d(wts[n]).shape, F32) for _ in range(3)] for n in big]
    follows = d_cond
    for i, send_sems, recv_sems, parts, lands, homes in exchanges:
        parts, lands = _chip_send_wait(send_sems, recv_sems, parts, lands, follows, f"grads_chip_wait_l{i}")
        for k, (part, land, (o, lead, _)) in enumerate(zip(parts, lands, homes)):
            bufs[o] = _chip_sum(part, land, bufs[o], lead, place_idx, f"grads_chip_sum_l{i}_{k}")
        bufs = list(_pair_gather(bufs, homes, f"grads_pair_gather_l{i}"))
        for o in sorted({o for o, _, _ in homes}):
            n = big[o]
            starts = [sum(a * math.prod(wts[n].shape[d + 1:-1]) for d, a in enumerate(lead))
                      for oo, lead, _ in homes if oo == o]
            rows = wts[n].shape[-2]
            *updates[o], g_out = _adamw_rows(two_d(wts[n]), two_d(bufs[o]), two_d(mom[n]), two_d(var[n]), updates[o],
                                             min(starts), max(starts) + rows - min(starts), f"adamw_{n}_l{i}")
            bufs[o] = g_out.reshape(wts[n].shape)
        follows = updates[0][0]
    grads.update(zip(big, bufs))

    delta, new_m, new_v = {"w_cond": d_cond}, {"w_cond": m_cond}, {"w_cond": v_cond}
    for n, (d, nm, nv) in zip(big, updates):
        delta[n], new_m[n], new_v[n] = (a.reshape(wts[n].shape) for a in (d, nm, nv))
    shapes = [wts[n].shape for n in _SMALL]
    packed = [_pack_rows([src[n] for n in _SMALL]) for src in (wts, grads, mom, var)]
    for dst, out in zip((delta, new_m, new_v), _adamw(*packed, "adamw_small")):
        dst.update(zip(_SMALL, _unpack(out.reshape(-1), shapes)))

    return (loss, grad_x[None], *[grads[n] for n in _WEIGHTS], *[delta[n] for n in _WEIGHTS],
            *[new_m[n] for n in _WEIGHTS], *[new_v[n] for n in _WEIGHTS])


def kernel(x, c, w_cond, b_cond, norm_pre, norm_post, w_ffn_in, w_ffn_out, fox_w_in, fox_b_f, fox_w_out, sconv_w_in, sconv_conv_w, sconv_w_out, lru_w_in, lru_conv_w, lru_conv_b, lru_w_a, lru_b_a, lru_w_x, lru_b_x, lru_lambda, lru_w_out, loss_target, m_w_cond, m_b_cond, m_norm_pre, m_norm_post, m_w_ffn_in, m_w_ffn_out, m_fox_w_in, m_fox_b_f, m_fox_w_out, m_sconv_w_in, m_sconv_conv_w, m_sconv_w_out, m_lru_w_in, m_lru_conv_w, m_lru_conv_b, m_lru_w_a, m_lru_b_a, m_lru_w_x, m_lru_b_x, m_lru_lambda, m_lru_w_out, v_w_cond, v_b_cond, v_norm_pre, v_norm_post, v_w_ffn_in, v_w_ffn_out, v_fox_w_in, v_fox_b_f, v_fox_w_out, v_sconv_w_in, v_sconv_conv_w, v_sconv_w_out, v_lru_w_in, v_lru_conv_w, v_lru_conv_b, v_lru_w_a, v_lru_b_a, v_lru_w_x, v_lru_b_x, v_lru_lambda, v_lru_w_out):
    given = dict(locals())
    wts = {n: given[n] for n in _WEIGHTS}
    mom = {n: given["m_" + n] for n in _WEIGHTS}
    var = {n: given["v_" + n] for n in _WEIGHTS}
    return _step(x, c, loss_target, wts, mom, var)
```

```python
import functools
import math
from typing import NamedTuple

import jax
import jax.numpy as jnp
from jax import lax
from jax.experimental import pallas as pl
from jax.experimental.pallas import tpu as pltpu

F32 = jnp.float32
BF16 = jnp.bfloat16

D_MODEL = 1024
DEPTH = 4
N_SUB = 3
D_FF = 2816
RMS_EPS = 1e-6
FOX_HEADS = 16
FOX_HEAD_DIM = 64
FOX_PAD = 3200
LRU_BLOCKS = 16
LRU_BLOCK_DIM = 64
LRU_C = 8.0
N_CHIPS = 4
N_DEV = 8

ADAM_LR = 0.001
ADAM_B1 = 0.9
ADAM_B2 = 0.999
ADAM_EPS = 1e-08
ADAM_WD = 0.01
ADAM_STEP = 10

VMEM_LIMIT_V7X = 56 * 1024 * 1024
ROW_TILE = 512
COL_TILE = 256
ATT_TILE = 256
ATT_WIDE = 512
MM_ROWS = 1024


def _cparams(sem=None):
    return pltpu.CompilerParams(vmem_limit_bytes=VMEM_LIMIT_V7X, dimension_semantics=sem)


def _sigmoid(z):
    return 1.0 / (1.0 + jnp.exp(-z))


def _softplus(z):
    return jnp.maximum(z, 0.0) + jnp.log(1.0 + jnp.exp(-jnp.abs(z)))


def _rows_sum(v):
    return jnp.sum(v, axis=0, keepdims=True)


class _W(NamedTuple):
    arr: jax.Array
    prefix: tuple = ()
    blocked: bool = False


def _w_spec(w, block2, pos):
    lead = (None,) * (len(w.prefix) + (1 if w.blocked else 0))
    if w.blocked:
        return pl.BlockSpec(lead + block2, lambda *g: (pos(*g)[0], *w.prefix, pos(*g)[1], pos(*g)[2]))
    return pl.BlockSpec(lead + block2, lambda *g: (*w.prefix, pos(*g)[1], pos(*g)[2]))


def _mm_nn(a, b, name, tn=None, cols=None, out_dtype=F32):
    m, k = a.shape
    if b.blocked:
        steps, bn = b.arr.shape[0], b.arr.shape[-1]
        b_spec = _w_spec(b, (k, bn), lambda n: (n, 0, 0))
    else:
        first, last = (0, b.arr.shape[-1]) if cols is None else cols
        n_total = last - first
        bn = n_total if tn is None else tn
        steps = n_total // bn
        assert steps * bn == n_total and first % bn == 0
        b_spec = _w_spec(b, (k, bn), lambda n: (0, 0, n + first // bn))
    tm = min(MM_ROWS, m)

    def body(a_ref, b_ref, o_ref):
        def step(i, carry):
            r = pl.ds(pl.multiple_of(i * tm, tm), tm)
            o_ref[r, :] = jnp.dot(a_ref[r, :], b_ref[...], preferred_element_type=F32).astype(out_dtype)
            return carry
        lax.fori_loop(0, m // tm, step, 0)

    return pl.pallas_call(
        body, name=name, grid=(steps,),
        in_specs=[pl.BlockSpec((m, k), lambda n: (0, 0)), b_spec],
        out_specs=pl.BlockSpec((m, bn), lambda n: (0, n)),
        out_shape=jax.ShapeDtypeStruct((m, steps * bn), out_dtype),
        compiler_params=_cparams(("arbitrary",)),
    )(a, b.arr)


def _cols_shape(dy):
    return (dy.shape[0], dy.shape[1]) if dy.ndim == 2 else (dy.shape[1], 2 * dy.shape[2])


def _cols_spec(dy, bn):
    if dy.ndim == 2:
        return pl.BlockSpec((dy.shape[0], bn), lambda kt, n: (0, n))
    per = dy.shape[2] // bn
    assert per * bn == dy.shape[2]
    return pl.BlockSpec((None, dy.shape[1], bn), lambda kt, n: (n // per, 0, n % per))


def _mm_nt(dy, w, name, tk=None, tn=None):
    m, n_total = _cols_shape(dy)
    k = w.arr.shape[-2]
    if w.blocked:
        bk, bn = k, w.arr.shape[-1]
        grid = (1, w.arr.shape[0])
        w_spec = _w_spec(w, (k, bn), lambda kt, n: (n, 0, 0))
    else:
        bk = k if tk is None else tk
        bn = n_total if tn is None else tn
        grid = (k // bk, n_total // bn)
        assert grid[0] * bk == k and grid[1] * bn == n_total
        w_spec = _w_spec(w, (bk, bn), lambda kt, n: (0, kt, n))
    tm = min(MM_ROWS, m)

    reduce_steps = grid[1]

    def body(dy_ref, w_ref, o_ref):
        def step(i, carry):
            r = pl.ds(pl.multiple_of(i * tm, tm), tm)
            part = lax.dot_general(dy_ref[r, :], w_ref[...], (((1,), (1,)), ((), ())), preferred_element_type=F32)
            if reduce_steps == 1:
                o_ref[r, :] = part
            else:
                o_ref[r, :] += part
            return carry

        if reduce_steps > 1:
            @pl.when(pl.program_id(1) == 0)
            def _():
                o_ref[...] = jnp.zeros_like(o_ref)
        lax.fori_loop(0, m // tm, step, 0)

    return pl.pallas_call(
        body, name=name, grid=grid,
        in_specs=[_cols_spec(dy, bn), w_spec],
        out_specs=pl.BlockSpec((m, bk), lambda kt, n: (0, kt)),
        out_shape=jax.ShapeDtypeStruct((m, k), F32),
        compiler_params=_cparams(("arbitrary", "arbitrary")),
    )(dy, w.arr)


def _mm_tn(x, dy, name, tk=None, tn=None, blocked_out=False):
    s, k = x.shape
    n_total = _cols_shape(dy)[1]
    bk = k if tk is None else tk
    bn = n_total if tn is None else tn
    grid = (k // bk, n_total // bn)
    assert grid[0] * bk == k and grid[1] * bn == n_total
    ck = next(c for c in (512, 256, 128) if bk % c == 0)

    def body(x_ref, dy_ref, o_ref):
        def step(i, carry):
            c = pl.ds(pl.multiple_of(i * ck, ck), ck)
            o_ref[c, :] = lax.dot_general(x_ref[:, c], dy_ref[...], (((0,), (0,)), ((), ())),
                                          preferred_element_type=F32)
            return carry
        lax.fori_loop(0, bk // ck, step, 0)

    if blocked_out:
        assert grid[0] == 1
        out_spec = pl.BlockSpec((None, bk, bn), lambda kt, n: (n, 0, 0))
        out_shape = jax.ShapeDtypeStruct((grid[1], k, bn), F32)
    else:
        out_spec = pl.BlockSpec((bk, bn), lambda kt, n: (kt, n))
        out_shape = jax.ShapeDtypeStruct((k, n_total), F32)
    return pl.pallas_call(
        body, name=name, grid=grid,
        in_specs=[pl.BlockSpec((s, bk), lambda kt, n: (0, kt)), _cols_spec(dy, bn)],
        out_specs=out_spec, out_shape=out_shape,
        compiler_params=_cparams(("arbitrary", "arbitrary")),
    )(x, dy)


def _row_call(name, body, rows, fulls, row_outs, acc_outs, tr=ROW_TILE, after=None):
    s = rows[0].shape[0]
    tr = min(tr, s)
    in_specs = [pl.BlockSpec((tr, a.shape[1]), lambda i: (i, 0)) for a in rows]
    in_specs += [pl.BlockSpec(a.shape, lambda i: (0, 0)) for a in fulls]
    n_in = len(in_specs)
    order = [] if after is None else [after]
    in_specs += [pl.BlockSpec(memory_space=pl.ANY)] * len(order)
    out_specs = [pl.BlockSpec((tr, c), lambda i: (i, 0)) for c, _ in row_outs]
    out_specs += [pl.BlockSpec((1, c), lambda i: (0, 0)) for c, _ in acc_outs]
    out_shape = [jax.ShapeDtypeStruct((s, c), dt) for c, dt in row_outs]
    out_shape += [jax.ShapeDtypeStruct((1, c), dt) for c, dt in acc_outs]
    n_acc = len(acc_outs)

    def wrapped(*refs):
        refs = refs[:n_in] + refs[n_in + len(order):]
        if n_acc:
            @pl.when(pl.program_id(0) == 0)
            def _():
                for r in refs[len(refs) - n_acc:]:
                    r[...] = jnp.zeros_like(r)
        body(*refs)

    return pl.pallas_call(
        wrapped, name=name, grid=(s // tr,), in_specs=in_specs, out_specs=out_specs, out_shape=out_shape,
        compiler_params=_cparams(("arbitrary",)),
    )(*rows, *fulls, *order)


def _rms(v):
    return lax.rsqrt(jnp.mean(v * v, axis=-1, keepdims=True) + RMS_EPS)


def _pre_norm(x, g_pre, scale, shift, name, after=None):
    def body(x_ref, g_ref, sc_ref, sh_ref, h_ref):
        xv = x_ref[...]
        h = (xv * _rms(xv)) * g_ref[...] * (1.0 + sc_ref[...]) + sh_ref[...]
        h_ref[...] = h.astype(BF16)
    return _row_call(name, body, [x], [g_pre, scale, shift], [(D_MODEL, BF16)], [], after=after)[0]


def _post_norm(x, y, g_post, gate, coef, name):
    def body(x_ref, y_ref, g_ref, gate_ref, o_ref):
        yv = y_ref[...]
        o_ref[...] = x_ref[...] + (coef * gate_ref[...]) * ((yv * _rms(yv)) * g_ref[...])
    return _row_call(name, body, [x, y], [g_post, gate], [(D_MODEL, F32)], [])[0]


def _post_norm_bwd(dxo, y, g_post, gate, coef, name, after=None):
    def body(dxo_ref, y_ref, g_ref, gate_ref, dy_ref, dgate_ref, dg_ref):
        yv = y_ref[...]
        r2 = _rms(yv)
        yn = yv * r2
        dxo_v = dxo_ref[...]
        dgate_ref[...] += _rows_sum(dxo_v * (yn * g_ref[...])) * coef
        dz = dxo_v * (coef * gate_ref[...])
        dg_ref[...] += _rows_sum(dz * yn)
        dyn = dz * g_ref[...]
        dy = r2 * (dyn - yn * jnp.mean(dyn * yn, axis=-1, keepdims=True))
        dy_ref[...] = dy.astype(BF16)
    return _row_call(name, body, [dxo, y], [g_post, gate], [(D_MODEL, BF16)], [(D_MODEL, F32), (D_MODEL, F32)],
                     after=after)


def _pre_norm_bwd(dxo, dh, x, g_pre, scale, name):
    def body(dxo_ref, dh_ref, x_ref, g_ref, sc_ref, dx_ref, dshift_ref, dscale_ref, dg_ref):
        xv = x_ref[...]
        r = _rms(xv)
        xn = xv * r
        dh_v = dh_ref[...]
        one_sc = 1.0 + sc_ref[...]
        dshift_ref[...] += _rows_sum(dh_v)
        dscale_ref[...] += _rows_sum(dh_v * (xn * g_ref[...]))
        dg_ref[...] += _rows_sum(dh_v * xn * one_sc)
        dxn = dh_v * (g_ref[...] * one_sc)
        dx_ref[...] = dxo_ref[...] + r * (dxn - xn * jnp.mean(dxn * xn, axis=-1, keepdims=True))
    return _row_call(name, body, [dxo, dh, x], [g_pre, scale], [(D_MODEL, F32)],
                     [(D_MODEL, F32), (D_MODEL, F32), (D_MODEL, F32)])


FFN_COLS = 1408


def _ffn_in_act(h, w_in, name, after=None):
    m, k = h.shape
    half, bn = w_in.arr.shape[0] // 2, w_in.arr.shape[-1]
    assert bn == FFN_COLS and half * bn == D_FF
    tm = min(MM_ROWS, m)
    order = [] if after is None else [after]

    def body(h_ref, wg_ref, wu_ref, *rest):
        g_ref, u_ref, a_ref = rest[len(order):]
        g = jnp.dot(h_ref[...], wg_ref[...], preferred_element_type=F32)
        g_ref[...] = g
        u = jnp.dot(h_ref[...], wu_ref[...], preferred_element_type=F32)
        u_ref[...] = u
        a_ref[...] = (g * _sigmoid(g) * u).astype(BF16)

    tile = pl.BlockSpec((tm, bn), lambda t, i: (i, t))
    return pl.pallas_call(
        body, name=name, grid=(half, m // tm),
        in_specs=[pl.BlockSpec((tm, k), lambda t, i: (i, 0)),
                  _w_spec(w_in, (k, bn), lambda t, i: (t, 0, 0)),
                  _w_spec(w_in, (k, bn), lambda t, i: (half + t, 0, 0))] + [pl.BlockSpec(memory_space=pl.ANY)] * len(order),
        out_specs=[tile, tile, tile],
        out_shape=[jax.ShapeDtypeStruct((m, D_FF), F32)] * 2 + [jax.ShapeDtypeStruct((m, D_FF), BF16)],
        compiler_params=_cparams(("arbitrary", "arbitrary")),
    )(h, w_in.arr, w_in.arr, *order)


def _ffn_out_bx_act(dy, w_out, g, u, name, after=None):
    m = dy.shape[0]
    tr = min(MM_ROWS, m)
    order = [] if after is None else [after]

    def body(dy_ref, w_ref, g_ref, u_ref, *rest):
        dgu_ref = rest[-1]
        da = lax.dot_general(dy_ref[...], w_ref[...], _NT, preferred_element_type=F32)
        gv = g_ref[...]
        sg = _sigmoid(gv)
        dgu_ref[0] = (da * u_ref[...] * (sg * (1.0 + gv * (1.0 - sg)))).astype(BF16)
        dgu_ref[1] = (da * (gv * sg)).astype(BF16)

    tile = pl.BlockSpec((tr, FFN_COLS), lambda i, c: (i, c))
    return pl.pallas_call(
        body, name=name, grid=(m // tr, D_FF // FFN_COLS),
        in_specs=[pl.BlockSpec((tr, D_MODEL), lambda i, c: (i, 0)),
                  _w_spec(w_out, (FFN_COLS, D_MODEL), lambda i, c: (0, c, 0)), tile, tile]
        + [pl.BlockSpec(memory_space=pl.ANY)] * len(order),
        out_specs=pl.BlockSpec((2, tr, FFN_COLS), lambda i, c: (0, i, c)),
        out_shape=jax.ShapeDtypeStruct((2, m, D_FF), BF16),
        compiler_params=_cparams(("arbitrary", "arbitrary")),
    )(dy, w_out.arr, g, u, *order)


def _ffn_in_bwd(dgu, h, w_in, name):
    m, k = h.shape
    nb, bn = w_in.arr.shape[0], w_in.arr.shape[-1]
    per = dgu.shape[2] // bn
    tm = min(MM_ROWS, m)
    ck = next(c for c in (512, 256, 128) if k % c == 0)
    once = pl.Buffered(1)

    def body(dgu_ref, h_ref, w_ref, dh_ref, dw_ref):
        @pl.when(pl.program_id(0) == 0)
        def _():
            dh_ref[...] = jnp.zeros_like(dh_ref)

        def rows(i, carry):
            r = pl.ds(pl.multiple_of(i * tm, tm), tm)
            dh_ref[r, :] += lax.dot_general(dgu_ref[r, :], w_ref[...], _NT, preferred_element_type=F32)
            return carry
        lax.fori_loop(0, m // tm, rows, 0)

        def cols(i, carry):
            c = pl.ds(pl.multiple_of(i * ck, ck), ck)
            dw_ref[c, :] = lax.dot_general(h_ref[:, c], dgu_ref[...], (((0,), (0,)), ((), ())),
                                           preferred_element_type=F32)
            return carry
        lax.fori_loop(0, k // ck, cols, 0)

    return pl.pallas_call(
        body, name=name, grid=(nb,),
        in_specs=[pl.BlockSpec((None, m, bn), lambda n: (n // per, 0, n % per)),
                  pl.BlockSpec((m, k), lambda n: (0, 0), pipeline_mode=once),
                  _w_spec(w_in, (k, bn), lambda n: (n, 0, 0))],
        out_specs=[pl.BlockSpec((m, k), lambda n: (0, 0), pipeline_mode=once),
                   pl.BlockSpec((None, k, bn), lambda n: (n, 0, 0))],
        out_shape=[jax.ShapeDtypeStruct((m, k), F32), jax.ShapeDtypeStruct((nb, k, bn), F32)],
        compiler_params=_cparams(("arbitrary",)),
    )(dgu, h, w_in.arr)


def _loss_head(y, target, name):
    def body(y_ref, t_ref, dy_ref, loss_ref):
        e = y_ref[...] - t_ref[...]
        dy_ref[...] = e * (1.0 / D_MODEL)
        part = jnp.sum(jnp.mean(e * e, axis=-1, keepdims=True), axis=0, keepdims=True) * 0.5
        loss_ref[...] += jnp.broadcast_to(part, loss_ref.shape)
    return _row_call(name, body, [y, target], [], [(D_MODEL, F32)], [(128, F32)])


def _lane_scan(v, reverse):
    s = v.shape[1]
    lane = lax.broadcasted_iota(jnp.int32, v.shape, 1)
    d = 1
    while d < s:
        if reverse:
            v = v + jnp.where(lane < s - d, pltpu.roll(v, s - d, 1), 0.0)
        else:
            v = v + jnp.where(lane >= d, pltpu.roll(v, d, 1), 0.0)
        d *= 2
    return v


def _fox_gate(flt, b_f, name):
    def body(f_ref, b_ref, cum_ref):
        z = f_ref[...] + b_ref[...]
        cum_ref[...] = _lane_scan(-_softplus(-z), reverse=False)
    return pl.pallas_call(body, name=name, out_shape=jax.ShapeDtypeStruct(flt.shape, F32),
                          compiler_params=_cparams())(flt, b_f)


def _fox_gate_bwd(dcum_q, dcum_k, flt, b_f, name):
    def body(dq_ref, dk_ref, f_ref, b_ref, df_ref, db_ref):
        z = f_ref[...] + b_ref[...]
        df = _lane_scan(dq_ref[...] + dk_ref[...], reverse=True) * _sigmoid(-z)
        df_ref[...] = df
        db_ref[...] = jnp.sum(df, axis=1, keepdims=True)
    h = flt.shape[0]
    return pl.pallas_call(body, name=name,
                          out_shape=(jax.ShapeDtypeStruct(flt.shape, F32), jax.ShapeDtypeStruct((h, 1), F32)),
                          compiler_params=_cparams())(dcum_q, dcum_k, flt, b_f)


def _pick_head(block, h):
    lane = lax.broadcasted_iota(jnp.int32, block.shape, 1)
    return jnp.sum(jnp.where(lane == h, block, 0.0), axis=1, keepdims=True)


def _put_head(ref, col, h):
    @pl.when(h == 0)
    def _():
        ref[...] = jnp.zeros_like(ref)
    lane = lax.broadcasted_iota(jnp.int32, ref.shape, 1)
    ref[...] = jnp.where(lane == h, col, ref[...])


_NT = (((1,), (1,)), ((), ()))
_FOX_SCALE = FOX_HEAD_DIM ** -0.5


HEAD_PAIRS = FOX_HEADS // 2
PAIR_W = 2 * FOX_HEAD_DIM


def _low_half(shape):
    return lax.broadcasted_iota(jnp.int32, shape, 1) < FOX_HEAD_DIM


def _fox_attn_fwd(qkv, cum, cum_t, name):
    s = qkv.shape[0]
    t = min(ATT_TILE, s)
    wide = min(ATT_WIDE, s)

    def body(q_ref, k_ref, v_ref, cum_ref, cumt_ref, o_ref, ob_ref, lse_ref):
        i = pl.program_id(0)
        hp = pl.program_id(1)
        lo = _low_half((t, PAIR_W))
        qv = q_ref[...]
        zero = jnp.zeros_like(qv)
        q2 = (jnp.where(lo, qv, zero), jnp.where(lo, zero, qv))
        cum_v = cum_ref[...]
        cq2 = (_pick_head(cum_v, 2 * hp), _pick_head(cum_v, 2 * hp + 1))

        def step(j, carry, masked):
            ks = pl.ds(pl.multiple_of(j * wide, wide), wide)
            kj = k_ref[ks, :]
            vj = v_ref[ks, :]
            out = []
            for e in range(2):
                m, l, acc = carry[e]
                sc = lax.dot_general(q2[e], kj, _NT, preferred_element_type=F32) * _FOX_SCALE
                sc = sc + cq2[e] - cumt_ref[e:e + 1, ks]
                if masked:
                    q_pos = i * t + lax.broadcasted_iota(jnp.int32, (t, wide), 0)
                    k_pos = j * wide + lax.broadcasted_iota(jnp.int32, (t, wide), 1)
                    sc = jnp.where(k_pos <= q_pos, sc, -jnp.inf)
                m_new = jnp.maximum(m, jnp.max(sc, axis=1, keepdims=True))
                alpha = jnp.exp(m - m_new)
                p = jnp.exp(sc - m_new)
                l = alpha * l + jnp.sum(p, axis=1, keepdims=True)
                acc = alpha * acc + jnp.dot(p.astype(BF16), vj, preferred_element_type=F32)
                out.append((m_new, l, acc))
            return tuple(out)

        one = (jnp.full((t, 1), -jnp.inf, F32), jnp.zeros((t, 1), F32), jnp.zeros((t, PAIR_W), F32))
        whole = (i * t) // wide
        carry = lax.fori_loop(0, whole, lambda j, c: step(j, c, False), (one, one))
        (m0, l0, a0), (m1, l1, a1) = step(whole, carry, True)
        o = jnp.where(lo, a0 / l0, a1 / l1)
        o_ref[...] = o
        ob_ref[...] = o.astype(BF16)
        _put_head(lse_ref, m0 + jnp.log(l0), 2 * hp)
        _put_head(lse_ref, m1 + jnp.log(l1), 2 * hp + 1)

    nat_tile = pl.BlockSpec((t, FOX_HEADS), lambda i, hp: (i, 0))
    out_tile = pl.BlockSpec((t, PAIR_W), lambda i, hp: (i, hp))
    return pl.pallas_call(
        body, name=name, grid=(s // t, HEAD_PAIRS),
        in_specs=[pl.BlockSpec((t, PAIR_W), lambda i, hp: (i, hp)),
                  pl.BlockSpec((s, PAIR_W), lambda i, hp: (0, HEAD_PAIRS + hp)),
                  pl.BlockSpec((s, PAIR_W), lambda i, hp: (0, 2 * HEAD_PAIRS + hp)),
                  nat_tile, pl.BlockSpec((None, 2, s), lambda i, hp: (hp, 0, 0))],
        out_specs=[out_tile, out_tile, nat_tile],
        out_shape=[jax.ShapeDtypeStruct((s, D_MODEL), F32), jax.ShapeDtypeStruct((s, D_MODEL), BF16),
                   jax.ShapeDtypeStruct((s, FOX_HEADS), F32)],
        compiler_params=_cparams(("arbitrary", "arbitrary")),
    )(qkv, qkv, qkv, cum, cum_t)


def _fox_delta(do, o, expand, name):
    def body(do_ref, o_ref, e_ref, d_ref):
        prod = do_ref[...] * o_ref[...]
        hi = prod.astype(BF16)
        lo = (prod - hi.astype(F32)).astype(BF16)
        tot = (jnp.dot(hi, e_ref[...], preferred_element_type=F32)
               + jnp.dot(lo, e_ref[...], preferred_element_type=F32))
        d_ref[...] = tot[:, :FOX_HEADS]
    return _row_call(name, body, [do, o], [expand], [(FOX_HEADS, F32)], [])[0]


def _fox_attn_bwd(qkv, do, cum, cum_t, lse_t, delta_t, name):
    s = qkv.shape[0]
    t = min(ATT_TILE, s)
    wide = min(ATT_WIDE, s)
    nq = s // t
    tn_dims = (((0,), (0,)), ((), ()))

    def body(q_ref, k_ref, v_ref, do_ref, cum_ref, cumt_ref, lset_ref, deltat_ref,
             dq_ref, dk_ref, dv_ref, dck_ref, dcq_ref):
        hp = pl.program_id(0)
        j = pl.program_id(1)

        @pl.when(j == 0)
        def _():
            dq_ref[...] = jnp.zeros_like(dq_ref)
            dcq_ref[...] = jnp.zeros_like(dcq_ref)
        dk_ref[...] = jnp.zeros_like(dk_ref)
        dv_ref[...] = jnp.zeros_like(dv_ref)

        lo = _low_half((t, PAIR_W))
        lane = lax.broadcasted_iota(jnp.int32, (t, PAIR_W), 1)
        kv = k_ref[...]
        vv = v_ref[...]
        zero = jnp.zeros_like(kv)
        k2 = (jnp.where(lo, kv, zero), jnp.where(lo, zero, kv))
        v2 = (jnp.where(lo, vv, zero), jnp.where(lo, zero, vv))
        cum_v = cum_ref[...]
        ck2 = (_pick_head(cum_v, 2 * hp), _pick_head(cum_v, 2 * hp + 1))

        def step(i, dck, masked):
            qs = pl.ds(pl.multiple_of(i * wide, wide), wide)
            qi = q_ref[qs, :]
            do_i = do_ref[qs, :].astype(BF16)
            dv_p, dk_p, dq_p = [], [], []
            for e in range(2):
                st = lax.dot_general(k2[e], qi, _NT, preferred_element_type=F32) * _FOX_SCALE
                st = st + cumt_ref[e:e + 1, qs] - ck2[e]
                if masked:
                    k_pos = j * t + lax.broadcasted_iota(jnp.int32, (t, wide), 0)
                    q_pos = i * wide + lax.broadcasted_iota(jnp.int32, (t, wide), 1)
                    st = jnp.where(k_pos <= q_pos, st, -jnp.inf)
                pt = jnp.exp(st - lset_ref[e:e + 1, qs])
                dv_p.append(jnp.dot(pt.astype(BF16), do_i, preferred_element_type=F32))
                dpt = lax.dot_general(v2[e], do_i, _NT, preferred_element_type=F32)
                dst = pt * (dpt - deltat_ref[e:e + 1, qs])
                dsb = dst.astype(BF16)
                dk_p.append(jnp.dot(dsb, qi, preferred_element_type=F32))
                dq_p.append(lax.dot_general(dsb, kv, tn_dims, preferred_element_type=F32))
                dck = dck - jnp.where(lane == e, jnp.sum(dst, axis=1, keepdims=True), 0.0)
                dcq_ref[e:e + 1, qs] += jnp.sum(dst, axis=0, keepdims=True)
            dv_ref[...] += jnp.where(lo, dv_p[0], dv_p[1])
            dk_ref[...] += jnp.where(lo, dk_p[0], dk_p[1])
            dq_ref[qs, :] += jnp.where(_low_half((wide, PAIR_W)), dq_p[0], dq_p[1]) * _FOX_SCALE
            return dck

        first = (j * t) // wide
        dck = step(first, jnp.zeros((t, PAIR_W), F32), True)
        dck = lax.fori_loop(first + 1, s // wide, lambda i, c: step(i, c, False), dck)
        dk_ref[...] = dk_ref[...] * _FOX_SCALE
        dck_ref[...] = dck

    pair_full = lambda part: pl.BlockSpec((s, PAIR_W), lambda hp, j: (0, part * HEAD_PAIRS + hp))
    pair_tile = lambda part: pl.BlockSpec((t, PAIR_W), lambda hp, j: (j, part * HEAD_PAIRS + hp))
    rows = pl.BlockSpec((None, 2, s), lambda hp, j: (hp, 0, 0))
    return pl.pallas_call(
        body, name=name, grid=(HEAD_PAIRS, nq),
        in_specs=[pair_full(0), pair_tile(1), pair_tile(2), pair_full(0),
                  pl.BlockSpec((t, FOX_HEADS), lambda hp, j: (j, 0)), rows, rows, rows],
        out_specs=[pair_full(0), pair_tile(0), pair_tile(0),
                   pl.BlockSpec((None, t, PAIR_W), lambda hp, j: (hp, j, 0)), rows],
        out_shape=[jax.ShapeDtypeStruct((s, D_MODEL), F32)] * 3
        + [jax.ShapeDtypeStruct((HEAD_PAIRS, s, PAIR_W), F32), jax.ShapeDtypeStruct((HEAD_PAIRS, 2, s), F32)],
        compiler_params=_cparams(("arbitrary", "arbitrary")),
    )(qkv, qkv, qkv, do, cum, cum_t, lse_t, delta_t)


def _shift_down(v, d):
    row = lax.broadcasted_iota(jnp.int32, v.shape, 0)
    return jnp.where(row >= d, pltpu.roll(v, d, 0), 0.0)


def _shift_up(v, d):
    s = v.shape[0]
    row = lax.broadcasted_iota(jnp.int32, v.shape, 0)
    return jnp.where(row < s - d, pltpu.roll(v, s - d, 0), 0.0)


def _conv_taps(v, cw_ref, width):
    out = cw_ref[width - 1:width, :] * v
    for k in range(width - 1):
        out = out + cw_ref[k:k + 1, :] * _shift_down(v, width - 1 - k)
    return out


def _conv_taps_bwd(dout, v, cw_ref, dcw_ref, width):
    dv = cw_ref[width - 1:width, :] * dout
    dcw_ref[width - 1:width, :] = _rows_sum(dout * v)
    for k in range(width - 1):
        d = width - 1 - k
        dv = dv + cw_ref[k:k + 1, :] * _shift_up(dout, d)
        dcw_ref[k:k + 1, :] = _rows_sum(dout * _shift_down(v, d))
    return dv


def _col_spec(s, tc, part=0):
    off = part * (D_MODEL // tc)
    return pl.BlockSpec((s, tc), lambda c: (0, c + off))


def _small_spec(rows, tc):
    return pl.BlockSpec((rows, tc), lambda c: (0, c))


def _col_call(name, body, in_arrays, in_specs, out_rows, s, tc):
    return pl.pallas_call(
        body, name=name, grid=(D_MODEL // tc,), in_specs=in_specs,
        out_specs=[pl.BlockSpec((r, tc), lambda c: (0, c)) for r, _ in out_rows],
        out_shape=[jax.ShapeDtypeStruct((r, D_MODEL), dt) for r, dt in out_rows],
        compiler_params=_cparams(("arbitrary",)),
    )(*in_arrays)


def _sconv_fwd(proj, conv_w, name):
    s = proj.shape[0]
    tc = COL_TILE

    def body(b_ref, c_ref, x_ref, cw_ref, y_ref):
        y_ref[...] = (b_ref[...] * _conv_taps(c_ref[...] * x_ref[...], cw_ref, 3)).astype(BF16)

    return _col_call(name, body, [proj, proj, proj, conv_w],
                     [_col_spec(s, tc, 0), _col_spec(s, tc, 1), _col_spec(s, tc, 2), _small_spec(3, tc)],
                     [(s, BF16)], s, tc)[0]


def _sconv_bwd(dy, proj, conv_w, name):
    s = proj.shape[0]
    tc = COL_TILE

    def body(dy_ref, b_ref, c_ref, x_ref, cw_ref, db_ref, dc_ref, dx_ref, dcw_ref):
        w = c_ref[...] * x_ref[...]
        dy_v = dy_ref[...]
        db_ref[...] = (dy_v * _conv_taps(w, cw_ref, 3)).astype(BF16)
        dw = _conv_taps_bwd(dy_v * b_ref[...], w, cw_ref, dcw_ref, 3)
        dc_ref[...] = (dw * x_ref[...]).astype(BF16)
        dx_ref[...] = (dw * c_ref[...]).astype(BF16)

    return _col_call(name, body, [dy, proj, proj, proj, conv_w],
                     [_col_spec(s, tc), _col_spec(s, tc, 0), _col_spec(s, tc, 1), _col_spec(s, tc, 2),
                      _small_spec(3, tc)],
                     [(s, BF16), (s, BF16), (s, BF16), (3, F32)], s, tc)


def _lru_conv(proj, conv_w, conv_b, name):
    s = proj.shape[0]
    tc = COL_TILE

    def body(x_ref, cw_ref, cb_ref, xb_ref, xbb_ref):
        xb = _conv_taps(x_ref[...], cw_ref, 4) + cb_ref[...]
        xb_ref[...] = xb
        xbb_ref[...] = xb.astype(BF16)

    return _col_call(name, body, [proj, conv_w, conv_b],
                     [_col_spec(s, tc, 1), _small_spec(4, tc), _small_spec(1, tc)],
                     [(s, F32), (s, BF16)], s, tc)


def _lru_conv_bwd(dxb1, dxb2, proj, conv_w, name):
    s = proj.shape[0]
    tc = COL_TILE

    def body(d1_ref, d2_ref, x_ref, cw_ref, dx_ref, dcw_ref, dcb_ref):
        dxb = d1_ref[...] + d2_ref[...]
        dcb_ref[...] = _rows_sum(dxb)
        dx_ref[...] = _conv_taps_bwd(dxb, x_ref[...], cw_ref, dcw_ref, 4).astype(BF16)

    return _col_call(name, body, [dxb1, dxb2, proj, conv_w],
                     [_col_spec(s, tc), _col_spec(s, tc), _col_spec(s, tc, 1), _small_spec(4, tc)],
                     [(s, BF16), (4, F32), (1, F32)], s, tc)


_GELU_C = math.sqrt(2.0 / math.pi)


def _gelu_parts(g):
    inner = _GELU_C * (g + 0.044715 * g * g * g)
    th = jnp.tanh(inner)
    val = 0.5 * g * (1.0 + th)
    der = 0.5 * (1.0 + th) + 0.5 * g * (1.0 - th * th) * (_GELU_C * (1.0 + 3.0 * 0.044715 * g * g))
    return val, der


def _lru_gates(pa_ref, px_ref, ba_ref, bx_ref, lam_ref):
    r = _sigmoid(pa_ref[...] + ba_ref[...])
    ig = _sigmoid(px_ref[...] + bx_ref[...])
    sp = _softplus(-lam_ref[...])
    log_a = (-LRU_C) * r * sp
    a = jnp.exp(log_a)
    z = 2.0 * log_a
    one_m_a2 = jnp.where(z > -1e-3, -(z * (1.0 + z * (0.5 + z * (1.0 / 6.0)))), 1.0 - jnp.exp(z))
    return r, ig, sp, a, jnp.sqrt(one_m_a2)


def _lru_scan(pre, xb, proj, b_a, b_x, lam, name):
    s = xb.shape[0]
    tc = COL_TILE

    def body(pa_ref, px_ref, xb_ref, g_ref, ba_ref, bx_ref, lam_ref, y_ref, hs_ref):
        _, ig, _, a, mult = _lru_gates(pa_ref, px_ref, ba_ref, bx_ref, lam_ref)
        b = mult * (ig * xb_ref[...])
        d = 1
        while d < s:
            row = lax.broadcasted_iota(jnp.int32, a.shape, 0)
            keep = row >= d
            b = b + a * jnp.where(keep, pltpu.roll(b, d, 0), 0.0)
            a = a * jnp.where(keep, pltpu.roll(a, d, 0), 1.0)
            d *= 2
        hs_ref[...] = b
        y_ref[...] = (b * _gelu_parts(g_ref[...])[0]).astype(BF16)

    return _col_call(name, body, [pre, pre, xb, proj, b_a, b_x, lam],
                     [_col_spec(s, tc, 0), _col_spec(s, tc, 1), _col_spec(s, tc), _col_spec(s, tc, 0),
                      _small_spec(1, tc), _small_spec(1, tc), _small_spec(1, tc)],
                     [(s, BF16), (s, F32)], s, tc)


def _lru_scan_bwd(dy, pre, xb, proj, hs, b_a, b_x, lam, name):
    s = xb.shape[0]
    tc = COL_TILE

    def body(dy_ref, pa_ref, px_ref, xb_ref, g_ref, hs_ref, ba_ref, bx_ref, lam_ref,
             dg_ref, dpa_ref, dpx_ref, dxb_ref, dba_ref, dbx_ref, dlam_ref):
        r, ig, sp, a, mult = _lru_gates(pa_ref, px_ref, ba_ref, bx_ref, lam_ref)
        gl, gl_der = _gelu_parts(g_ref[...])
        dy_v = dy_ref[...]
        hs_v = hs_ref[...]
        dg_ref[...] = (dy_v * hs_v * gl_der).astype(BF16)
        lam_t = dy_v * gl
        coef = _shift_up(a, 1)
        d = 1
        while d < s:
            row = lax.broadcasted_iota(jnp.int32, coef.shape, 0)
            keep = row < s - d
            lam_t = lam_t + coef * jnp.where(keep, pltpu.roll(lam_t, s - d, 0), 0.0)
            coef = coef * jnp.where(keep, pltpu.roll(coef, s - d, 0), 1.0)
            d *= 2
        xb_v = xb_ref[...]
        da = lam_t * _shift_down(hs_v, 1)
        dmult = lam_t * (ig * xb_v)
        dig = lam_t * mult * xb_v
        dxb_ref[...] = lam_t * mult * ig
        dlog_a = da * a - dmult * (a * a) / mult
        dr = dlog_a * ((-LRU_C) * sp)
        dsp = _rows_sum(dlog_a * ((-LRU_C) * r))
        dlam_ref[...] = -dsp * _sigmoid(-lam_ref[...])
        dpa = dr * r * (1.0 - r)
        dpx = dig * ig * (1.0 - ig)
        dba_ref[...] = _rows_sum(dpa)
        dbx_ref[...] = _rows_sum(dpx)
        dpa_ref[...] = dpa.astype(BF16)
        dpx_ref[...] = dpx.astype(BF16)

    return _col_call(name, body, [dy, pre, pre, xb, proj, hs, b_a, b_x, lam],
                     [_col_spec(s, tc), _col_spec(s, tc, 0), _col_spec(s, tc, 1), _col_spec(s, tc),
                      _col_spec(s, tc, 0), _col_spec(s, tc),
                      _small_spec(1, tc), _small_spec(1, tc), _small_spec(1, tc)],
                     [(s, BF16), (s, BF16), (s, BF16), (s, F32), (1, F32), (1, F32), (1, F32)], s, tc)


def _post_pre(x, y, g_post, gate, coef, g_pre, scale, shift, name):
    def body(x_ref, y_ref, gq_ref, gate_ref, gp_ref, sc_ref, sh_ref, xo_ref, h_ref):
        yv = y_ref[...]
        xo = x_ref[...] + (coef * gate_ref[...]) * ((yv * _rms(yv)) * gq_ref[...])
        xo_ref[...] = xo
        h_ref[...] = ((xo * _rms(xo)) * gp_ref[...] * (1.0 + sc_ref[...]) + sh_ref[...]).astype(BF16)
    return _row_call(name, body, [x, y], [g_post, gate, g_pre, scale, shift], [(D_MODEL, F32), (D_MODEL, BF16)], [])


def _pre_post_bwd(dxo, dh, x, g_pre, scale, y, g_post, gate, coef, name, after=None):
    def body(dxo_ref, dh_ref, x_ref, y_ref, gp_ref, sc_ref, gq_ref, gate_ref,
             dx_ref, dy_ref, dshift_ref, dscale_ref, dgp_ref, dgate_ref, dgq_ref):
        xv = x_ref[...]
        r = _rms(xv)
        xn = xv * r
        dh_v = dh_ref[...]
        one_sc = 1.0 + sc_ref[...]
        dshift_ref[...] += _rows_sum(dh_v)
        dscale_ref[...] += _rows_sum(dh_v * (xn * gp_ref[...]))
        dgp_ref[...] += _rows_sum(dh_v * xn * one_sc)
        dxn = dh_v * (gp_ref[...] * one_sc)
        dx = dxo_ref[...] + r * (dxn - xn * jnp.mean(dxn * xn, axis=-1, keepdims=True))
        dx_ref[...] = dx
        yv = y_ref[...]
        r2 = _rms(yv)
        yn = yv * r2
        dgate_ref[...] += _rows_sum(dx * (yn * gq_ref[...])) * coef
        dz = dx * (coef * gate_ref[...])
        dgq_ref[...] += _rows_sum(dz * yn)
        dyn = dz * gq_ref[...]
        dy_ref[...] = (r2 * (dyn - yn * jnp.mean(dyn * yn, axis=-1, keepdims=True))).astype(BF16)
    return _row_call(name, body, [dxo, dh, x, y], [g_pre, scale, g_post, gate], [(D_MODEL, F32), (D_MODEL, BF16)],
                     [(D_MODEL, F32)] * 5, after=after)


def _ffn_core(h, w_in, w_out, tag, after=None):
    g, u, a = _ffn_in_act(h, w_in, tag + "_in", after=after)
    y = _mm_nn(a, w_out, tag + "_out", tn=512)
    return y, (h, g, u, a)


def _ffn_core_bwd(dy, saved, w_in, w_out, tag, after=None):
    h, g, u, a = saved
    dgu = _ffn_out_bx_act(dy, w_out, g, u, tag + "_out_bx", after=after)
    dw_out = _mm_tn(a, dy, tag + "_out_bw", tn=512)
    dh, dw_in = _ffn_in_bwd(dgu, h, w_in, tag + "_in_b")
    return dh, dw_in, dw_out


def _pair_rows(v):
    return v.T.reshape(HEAD_PAIRS, 2, v.shape[0])


def _fox_fwd(h, p, tag):
    s = h.shape[0]
    qkv = _mm_nn(h, p["w_in"], tag + "_in", tn=768, cols=(0, 3 * D_MODEL), out_dtype=BF16)
    gates = _mm_nn(h, p["w_in"], tag + "_in_f", cols=(3 * D_MODEL, FOX_PAD))
    flt = gates[:, :FOX_HEADS].T
    cum_t = _fox_gate(flt, p["b_f"], tag + "_gate")
    cum = cum_t.T
    cum_t2 = cum_t.reshape(HEAD_PAIRS, 2, s)
    o, ob, lse = _fox_attn_fwd(qkv, cum, cum_t2, tag + "_attn")
    y = _mm_nn(ob, p["w_out"], tag + "_out")
    return y, (qkv, flt, cum, cum_t2, o, ob, lse)


def _fox_bwd(dy, h, saved, p, tag):
    qkv, flt, cum, cum_t2, o, ob, lse = saved
    s = h.shape[0]
    do = _mm_nt(dy, p["w_out"], tag + "_out_bx")
    dw_out = _mm_tn(ob, dy, tag + "_out_bw")
    expand = jnp.pad(jnp.repeat(jnp.eye(FOX_HEADS, dtype=BF16), FOX_HEAD_DIM, axis=0),
                     ((0, 0), (0, PAIR_W - FOX_HEADS)))
    delta = _fox_delta(do, o, expand, tag + "_attn_delta")
    dq, dk, dv, dck, dcq = _fox_attn_bwd(qkv, do, cum, cum_t2, _pair_rows(lse), _pair_rows(delta), tag + "_attn_b")
    dcum_k = dck[:, :, :2].transpose(0, 2, 1).reshape(FOX_HEADS, s)
    dflt, db_f = _fox_gate_bwd(dcq.reshape(FOX_HEADS, s), dcum_k, flt, p["b_f"], tag + "_gate_b")
    dproj = jnp.concatenate(
        [dq, dk, dv, dflt.T, jnp.zeros((s, FOX_PAD - 3 * D_MODEL - FOX_HEADS), F32)], axis=1).astype(BF16)
    dh = _mm_nt(dproj, p["w_in"], tag + "_in_bx", tn=640)
    dw_in = _mm_tn(h, dproj, tag + "_in_bw", tn=640)
    return dh, {"w_in": dw_in, "w_out": dw_out, "b_f": db_f}


def _sconv_mix_fwd(h, p, tag):
    proj = _mm_nn(h, p["w_in"], tag + "_in")
    yb = _sconv_fwd(proj, p["conv_w"], tag + "_conv")
    y = _mm_nn(yb, p["w_out"], tag + "_out")
    return y, (proj, yb)


def _sconv_mix_bwd(dy, h, saved, p, tag):
    proj, yb = saved
    dyb = _mm_nt(dy, p["w_out"], tag + "_out_bx")
    dw_out = _mm_tn(yb, dy, tag + "_out_bw")
    db, dc, dxv, dcw = _sconv_bwd(dyb, proj, p["conv_w"], tag + "_conv_b")
    dproj = jnp.concatenate([db, dc, dxv], axis=1)
    dh = _mm_nt(dproj, p["w_in"], tag + "_in_bx")
    dw_in = _mm_tn(h, dproj, tag + "_in_bw", tn=p["w_in"].arr.shape[-1], blocked_out=True)
    return dh, {"w_in": dw_in, "w_out": dw_out, "conv_w": dcw}


def _lru_mix_fwd(h, p, tag):
    proj = _mm_nn(h, p["w_in"], tag + "_in")
    xb, xbb = _lru_conv(proj, p["conv_w"], p["conv_b"], tag + "_conv")
    pre = _mm_nn(xbb, p["w_ax"], tag + "_gates", tn=D_MODEL)
    yb, hs = _lru_scan(pre, xb, proj, p["b_a"], p["b_x"], p["lam"], tag + "_scan")
    y = _mm_nn(yb, p["w_out"], tag + "_out")
    return y, (proj, xb, xbb, pre, yb, hs)


def _diag_blocks(m):
    return jnp.stack([m[LRU_BLOCK_DIM * n:LRU_BLOCK_DIM * (n + 1), LRU_BLOCK_DIM * n:LRU_BLOCK_DIM * (n + 1)]
                      for n in range(LRU_BLOCKS)])


def _lru_mix_bwd(dy, h, saved, p, tag):
    proj, xb, xbb, pre, yb, hs = saved
    dyb = _mm_nt(dy, p["w_out"], tag + "_out_bx")
    dw_out = _mm_tn(yb, dy, tag + "_out_bw")
    dg, dpa, dpx, dxb1, dba, dbx, dlam = _lru_scan_bwd(dyb, pre, xb, proj, hs, p["b_a"], p["b_x"], p["lam"],
                                                       tag + "_scan_b")
    dpre = jnp.concatenate([dpa, dpx], axis=1)
    dxb2 = _mm_nt(dpre, p["w_ax"], tag + "_gates_bx", tn=D_MODEL)
    dw_ax = _mm_tn(xbb, dpre, tag + "_gates_bw", tn=D_MODEL)
    dx0, dcw, dcb = _lru_conv_bwd(dxb1, dxb2, proj, p["conv_w"], tag + "_conv_b")
    dproj = jnp.concatenate([dg, dx0], axis=1)
    dh = _mm_nt(dproj, p["w_in"], tag + "_in_bx")
    dw_in = _mm_tn(h, dproj, tag + "_in_bw", tn=p["w_in"].arr.shape[-1], blocked_out=True)
    grads = {"w_in": dw_in, "w_out": dw_out, "conv_w": dcw, "conv_b": dcb,
             "w_a": _diag_blocks(dw_ax[:, :D_MODEL]), "w_x": _diag_blocks(dw_ax[:, D_MODEL:]),
             "b_a": dba, "b_x": dbx, "lam": dlam}
    return dh, grads


_MIXERS = ((_fox_fwd, _fox_bwd), (_sconv_mix_fwd, _sconv_mix_bwd), (_lru_mix_fwd, _lru_mix_bwd))


def _local_step(x, target, mod, layer_params, on_grads=None, on_mid=None, first_after=None):
    layers = []
    tape = []
    handed = None
    for i in range(DEPTH):
        row = lambda v: v[None, :]
        m = lambda sub, what: mod[i, sub, what][None, :]
        if handed is None:
            lp = dict(layer_params(i, 0, x))
            h0 = _pre_norm(x, row(lp["norm_pre"][0]), m(0, 1), m(0, 0), f"l{i}_ffn0_pre", after=first_after)
        else:
            lp, h0 = handed
        layers.append(lp)
        gp, gq = lp["norm_pre"], lp["norm_post"]
        y0, sv0 = _ffn_core(h0, lp["ffn_in"][0], lp["ffn_out"][0], f"l{i}_ffn0")
        x1, h1 = _post_pre(x, y0, row(gq[0]), m(0, 2), 0.5, row(gp[1]), m(1, 1), m(1, 0), f"l{i}_ffn0_post")
        lp.update(layer_params(i, 1, x1))
        y1, svm = _MIXERS[i % 3][0](h1, lp["mixer"], f"l{i}_mix")
        x2, h2 = _post_pre(x1, y1, row(gq[1]), m(1, 2), 1.0, row(gp[2]), m(2, 1), m(2, 0), f"l{i}_mix_post")
        second = layer_params(i, 2, x2)
        lp["ffn_in"] = lp["ffn_in"] + second["ffn_in"]
        lp["ffn_out"] = lp["ffn_out"] + second["ffn_out"]
        y2, sv2 = _ffn_core(h2, lp["ffn_in"][1], lp["ffn_out"][1], f"l{i}_ffn1", after=second.get("after"))
        if i + 1 < DEPTH:
            nxt = dict(layer_params(i + 1, 0, y2))
            x3, h_next = _post_pre(x2, y2, row(gq[2]), m(2, 2), 0.5, row(nxt["norm_pre"][0]),
                                   mod[i + 1, 0, 1][None, :], mod[i + 1, 0, 0][None, :], f"l{i}_ffn1_post")
            handed = (nxt, h_next)
        else:
            x3 = _post_norm(x2, y2, row(gq[2]), m(2, 2), 0.5, f"l{i}_ffn1_post")
        tape.append((x, y0, sv0, x1, h1, y1, svm, x2, y2, sv2))
        x = x3
    dx, loss_row = _loss_head(x, target, "loss_head")

    layer_grads = [None] * DEPTH
    dmod = [None] * DEPTH
    after = None
    handed_b = None
    for i in reversed(range(DEPTH)):
        lp = layers[i]
        row = lambda v: v[None, :]
        m = lambda sub, what: mod[i, sub, what][None, :]
        gp, gq = lp["norm_pre"], lp["norm_post"]
        x0, y0, sv0, x1, h1, y1, svm, x2, y2, sv2 = tape[i]
        if handed_b is None:
            dy2, dgate2, dgq2 = _post_norm_bwd(dx, y2, row(gq[2]), m(2, 2), 0.5, f"l{i}_ffn1_post_b", after=after)
            after = None
        else:
            dy2, dgate2, dgq2 = handed_b
        dh2, dw_in1, dw_out1 = _ffn_core_bwd(dy2, sv2, lp["ffn_in"][1], lp["ffn_out"][1], f"l{i}_ffn1", after=after)
        after = on_mid(i, dh2) if on_mid is not None else None
        dx, dy1, dshift2, dscale2, dgp2, dgate1, dgq1 = _pre_post_bwd(
            dx, dh2, x2, row(gp[2]), m(2, 1), y1, row(gq[1]), m(1, 2), 1.0, f"l{i}_mix_post_b", after=after)
        dh1, mg = _MIXERS[i % 3][1](dy1, h1, svm, lp["mixer"], f"l{i}_mix")
        dx, dy0, dshift1, dscale1, dgp1, dgate0, dgq0 = _pre_post_bwd(
            dx, dh1, x1, row(gp[1]), m(1, 1), y0, row(gq[0]), m(0, 2), 0.5, f"l{i}_ffn0_post_b")
        dh0, dw_in0, dw_out0 = _ffn_core_bwd(dy0, sv0, lp["ffn_in"][0], lp["ffn_out"][0], f"l{i}_ffn0")
        if i > 0:
            dx, dy_prev, dshift0, dscale0, dgp0, dgate_prev, dgq_prev = _pre_post_bwd(
                dx, dh0, x0, row(gp[0]), m(0, 1), tape[i - 1][8], row(layers[i - 1]["norm_post"][2]),
                mod[i - 1, 2, 2][None, :], 0.5, f"l{i}_ffn0_pre_b")
            handed_b = (dy_prev, dgate_prev, dgq_prev)
        else:
            dx, dshift0, dscale0, dgp0 = _pre_norm_bwd(dx, dh0, x0, row(gp[0]), m(0, 1), f"l{i}_ffn0_pre_b")
        dmod[i] = jnp.concatenate([dshift0, dscale0, dgate0, dshift1, dscale1, dgate1, dshift2, dscale2, dgate2],
                                  axis=0).reshape(N_SUB, 3, D_MODEL)
        layer_grads[i] = {"ffn_in": (dw_in0, dw_in1), "ffn_out": (dw_out0, dw_out1),
                          "norm_pre": jnp.concatenate([dgp0, dgp1, dgp2], axis=0),
                          "norm_post": jnp.concatenate([dgq0, dgq1, dgq2], axis=0), "mixer": mg}
        if on_grads is not None:
            after = on_grads(i, layer_grads[i], dx)
    return loss_row, dx, jnp.stack(dmod), layer_grads


COND_ROWS = 16
COND_PAD = 128


def _cond_fwd(c_pad, w_cond, b_shard, name):
    nl, d, n = w_cond.shape
    tn = 768

    def body(c_ref, w_ref, b_ref, o_ref):
        cv = c_ref[...]
        act = (cv * _sigmoid(cv)).astype(BF16)
        o_ref[...] = jnp.dot(act, w_ref[...].astype(BF16), preferred_element_type=F32) + b_ref[...]

    return pl.pallas_call(
        body, name=name, grid=(nl, n // tn),
        in_specs=[pl.BlockSpec((COND_ROWS, d), lambda i, j: (0, 0)),
                  pl.BlockSpec((None, d, tn), lambda i, j: (i, 0, j)),
                  pl.BlockSpec((None, 1, tn), lambda i, j: (i, 0, j))],
        out_specs=pl.BlockSpec((None, COND_ROWS, tn), lambda i, j: (i, 0, j)),
        out_shape=jax.ShapeDtypeStruct((nl, COND_ROWS, n), F32),
        compiler_params=_cparams(("arbitrary", "arbitrary")),
    )(c_pad, w_cond, b_shard)


def _adam_math(w, g, m, v):
    nm = ADAM_B1 * m + (1.0 - ADAM_B1) * g
    nv = ADAM_B2 * v + (1.0 - ADAM_B2) * (g * g)
    m_hat = nm / (1.0 - ADAM_B1 ** ADAM_STEP)
    v_hat = nv / (1.0 - ADAM_B2 ** ADAM_STEP)
    delta = (-ADAM_LR) * (m_hat / (jnp.sqrt(v_hat) + ADAM_EPS) + ADAM_WD * w)
    return delta, nm, nv


def _cond_bwd_adamw(c_t, dmod_s, w, m, v, name):
    nl, d, n = w.shape
    tn = 384
    blk = pl.BlockSpec((None, d, tn), lambda i, j: (i, 0, j))

    def body(c_ref, dm_ref, w_ref, m_ref, v_ref, g_ref, d_ref, nm_ref, nv_ref):
        cv = c_ref[...]
        g = jnp.dot((cv * _sigmoid(cv)).astype(BF16), dm_ref[...], preferred_element_type=F32)
        g_ref[...] = g
        d_ref[...], nm_ref[...], nv_ref[...] = _adam_math(w_ref[...], g, m_ref[...], v_ref[...])

    return pl.pallas_call(
        body, name=name, grid=(nl, n // tn),
        in_specs=[pl.BlockSpec((d, COND_PAD), lambda i, j: (0, 0)),
                  pl.BlockSpec((None, COND_PAD, tn), lambda i, j: (i, 0, j)), blk, blk, blk],
        out_specs=[blk] * 4, out_shape=[jax.ShapeDtypeStruct(w.shape, F32)] * 4,
        compiler_params=_cparams(("arbitrary", "arbitrary")),
    )(c_t, dmod_s, w, m, v)


def _adamw(w, g, m, v, name):
    rows, cols = w.shape
    tr = next(t for t in (256, 176, 128, 64, 32, 16, 8) if rows % t == 0)
    blk = pl.BlockSpec((tr, cols), lambda i: (i, 0))

    def body(w_ref, g_ref, m_ref, v_ref, d_ref, nm_ref, nv_ref):
        d_ref[...], nm_ref[...], nv_ref[...] = _adam_math(w_ref[...], g_ref[...], m_ref[...], v_ref[...])

    return pl.pallas_call(
        body, name=name, grid=(rows // tr,), in_specs=[blk] * 4, out_specs=[blk] * 3,
        out_shape=[jax.ShapeDtypeStruct(w.shape, F32)] * 3, compiler_params=_cparams(("arbitrary",)),
    )(w, g, m, v)


def _adamw_rows(w, g, m, v, outs, row0, nrows, name):
    cols = w.shape[1]
    tr = next(t for t in (512, 256, 128, 64, 32, 16, 8) if nrows % t == 0 and row0 % t == 0)
    blk = pl.BlockSpec((tr, cols), lambda i: (i + row0 // tr, 0))
    anywhere = pl.BlockSpec(memory_space=pl.ANY)

    def body(w_ref, g_ref, m_ref, v_ref, d_in, nm_in, nv_in, d_ref, nm_ref, nv_ref, g_out):
        d_ref[...], nm_ref[...], nv_ref[...] = _adam_math(w_ref[...], g_ref[...], m_ref[...], v_ref[...])

    return pl.pallas_call(
        body, name=name, grid=(nrows // tr,), in_specs=[blk] * 4 + [anywhere] * 3,
        out_specs=[blk] * 3 + [anywhere], out_shape=[jax.ShapeDtypeStruct(w.shape, F32)] * 4,
        input_output_aliases={4: 0, 5: 1, 6: 2, 1: 3}, compiler_params=_cparams(("arbitrary",)),
    )(w, g, m, v, *outs)


_MESH = pl.DeviceIdType.MESH
_ANY = pl.BlockSpec(memory_space=pl.ANY)


def _place():
    return lax.axis_index("x"), lax.axis_index("y"), lax.axis_index("c")


def _other_chips(x, y):
    return [(1 - x, y), (x, 1 - y), (1 - x, 1 - y)]


def _allgather8(block, name):
    m_per, n = block.shape

    def body(x_ref, out_ref, send_sems, recv_sems, local_sem):
        x, y, c = _place()
        me, sibling = (x, y, c), (x, y, 1 - c)
        chips = _other_chips(x, y)

        def rows(px, py, pc):
            return out_ref.at[pl.ds((4 * px + 2 * py + pc) * m_per, m_per), :]

        def copy(k, blk, to, src=None):
            return pltpu.make_async_remote_copy(
                src_ref=rows(*blk) if src is None else src, dst_ref=rows(*blk),
                send_sem=send_sems.at[k], recv_sem=recv_sems.at[k], device_id=to, device_id_type=_MESH)

        mine = pltpu.make_async_copy(x_ref, rows(*me), local_sem)
        mine.start()
        first = [copy(0, me, sibling, src=x_ref)]
        first += [copy(1 + j, me, (*chip, c), src=x_ref) for j, chip in enumerate(chips)]
        for cp in first:
            cp.start()
        passed = [copy(4 + j, (*chip, c), sibling) for j, chip in enumerate(chips)]
        for j, chip in enumerate(chips):
            copy(1 + j, (*chip, c), me).wait_recv()
            passed[j].start()
        copy(0, sibling, me).wait_recv()
        for j, chip in enumerate(chips):
            copy(4 + j, (*chip, 1 - c), me).wait_recv()
        for cp in first + passed:
            cp.wait_send()
        mine.wait()

    return pl.pallas_call(
        body, name=name, out_shape=jax.ShapeDtypeStruct((N_DEV * m_per, n), block.dtype),
        in_specs=[pl.BlockSpec(memory_space=pltpu.VMEM)], out_specs=pl.BlockSpec(memory_space=pltpu.VMEM),
        scratch_shapes=[pltpu.SemaphoreType.DMA((7,)), pltpu.SemaphoreType.DMA((7,)), pltpu.SemaphoreType.DMA],
        compiler_params=_cparams(),
    )(block)


def _split_axis(shape):
    return next(a for a, n in enumerate(shape) if n > 1)


_HBM = pl.BlockSpec(memory_space=pltpu.HBM)
_SEM = pl.BlockSpec(memory_space=pltpu.SEMAPHORE)
_SPLIT_COPY = pltpu.CompilerParams(has_side_effects=pltpu.SideEffectType.DATAFLOW_SIDE_EFFECTING)
_TOKEN = jax.ShapeDtypeStruct((8, 128), F32)


def _in_hbm(arrays):
    return [pltpu.with_memory_space_constraint(a, pltpu.HBM) for a in arrays]


class _Gathered(NamedTuple):
    shard_shape: tuple
    chip_axis: int

    @property
    def shape(self):
        return self.shard_shape[:self.chip_axis] + (N_CHIPS,) + self.shard_shape[self.chip_axis:]

    def half(self, ref, chip, pc):
        cut = _split_axis(self.shard_shape)
        n = self.shard_shape[cut] // 2
        idx = [slice(None)] * len(self.shard_shape)
        idx[cut] = pl.ds(pc * n, n)
        idx.insert(self.chip_axis, chip)
        return ref.at[tuple(idx)]


def _own_block_placed(shard, layout, chip):
    return lax.dynamic_update_slice_in_dim(lax.empty(layout.shape, shard.dtype),
                                           jnp.expand_dims(shard, layout.chip_axis), chip, axis=layout.chip_axis)


def _gather_copies(lands, layouts, send_sems, recv_sems):
    x, y, c = _place()
    out = []
    for t, (land, lay) in enumerate(zip(lands, layouts)):
        for j, (px, py) in enumerate(_other_chips(x, y)):
            def copy(chip, t=t, j=j, px=px, py=py, land=land, lay=lay):
                return pltpu.make_async_remote_copy(
                    src_ref=lay.half(land, chip, c), dst_ref=lay.half(land, chip, c),
                    send_sem=send_sems.at[3 * t + j], recv_sem=recv_sems.at[3 * t + j],
                    device_id=(px, py, c), device_id_type=_MESH)
            out.append((copy(2 * x + y), copy(2 * px + py)))
    return out


def _gather_start(lands, layouts, after, name):
    nt = len(lands)
    order = [] if after is None else [after]

    def body(*refs):
        land_refs = refs[:nt]
        send_sems, recv_sems = refs[nt + len(order):nt + len(order) + 2]
        token = refs[-1]
        for send, _ in _gather_copies(land_refs, layouts, send_sems, recv_sems):
            send.start()
        token[...] = jnp.zeros_like(token)

    out = pl.pallas_call(
        body, name=name,
        out_shape=(pltpu.SemaphoreType.DMA((3 * nt,)), pltpu.SemaphoreType.DMA((3 * nt,)),
                   *[pltpu.HBM(a.shape, a.dtype) for a in lands], _TOKEN),
        in_specs=[_HBM] * nt + [_ANY] * len(order),
        out_specs=(_SEM, _SEM, *[_HBM] * nt, pl.BlockSpec(memory_space=pltpu.VMEM)),
        input_output_aliases={t: 2 + t for t in range(nt)}, compiler_params=_SPLIT_COPY,
    )(*_in_hbm(lands), *order)
    return out[0], out[1], list(out[2:2 + nt]), out[-1]


def _gather_wait(send_sems, recv_sems, lands, layouts, after, name):
    nt = len(lands)

    def body(*refs):
        land_refs = refs[:nt]
        sems = refs[nt:nt + 2]
        for send, arrival in _gather_copies(land_refs, layouts, *sems):
            send.wait_send()
            arrival.wait_recv()

    return list(pl.pallas_call(
        body, name=name, out_shape=tuple(pltpu.HBM(a.shape, a.dtype) for a in lands),
        in_specs=[_HBM] * nt + [_SEM, _SEM, _ANY], out_specs=tuple([_HBM] * nt),
        input_output_aliases={t: t for t in range(nt)}, compiler_params=_SPLIT_COPY,
    )(*lands, send_sems, recv_sems, after))


def _gather_forward(lands, layouts, name):
    nt = len(lands)

    def body(*refs):
        outs = refs[nt:2 * nt]
        send_sems, recv_sems = refs[2 * nt:]
        x, y, c = _place()
        sends, arrivals = [], []
        for t, lay in enumerate(layouts):
            for j, (px, py) in enumerate(_other_chips(x, y)):
                for pc, group in ((c, sends), (1 - c, arrivals)):
                    part = lay.half(outs[t], 2 * px + py, pc)
                    group.append(pltpu.make_async_remote_copy(
                        src_ref=part, dst_ref=part, send_sem=send_sems.at[3 * t + j], recv_sem=recv_sems.at[3 * t + j],
                        device_id=(x, y, 1 - c), device_id_type=_MESH))
        for cp in sends:
            cp.start()
        for cp in arrivals:
            cp.wait_recv()
        for cp in sends:
            cp.wait_send()

    return list(pl.pallas_call(
        body, name=name, out_shape=[jax.ShapeDtypeStruct(a.shape, a.dtype) for a in lands],
        in_specs=[_ANY] * nt, out_specs=[_ANY] * nt, input_output_aliases={t: t for t in range(nt)},
        scratch_shapes=[pltpu.SemaphoreType.DMA((3 * nt,)), pltpu.SemaphoreType.DMA((3 * nt,))],
        compiler_params=_cparams(),
    )(*lands))


def _forward_copies(lands, layouts, send_sems, recv_sems):
    x, y, c = _place()
    out = []
    for t, (land, lay) in enumerate(zip(lands, layouts)):
        for j, (px, py) in enumerate(_other_chips(x, y)):
            def copy(pc, t=t, j=j, px=px, py=py, land=land, lay=lay):
                part = lay.half(land, 2 * px + py, pc)
                return pltpu.make_async_remote_copy(
                    src_ref=part, dst_ref=part, send_sem=send_sems.at[3 * t + j], recv_sem=recv_sems.at[3 * t + j],
                    device_id=(x, y, 1 - c), device_id_type=_MESH)
            out.append((copy(c), copy(1 - c)))
    return out


def _gather_forward_start(lands, layouts, name):
    nt = len(lands)

    def body(*refs):
        for send, _ in _forward_copies(refs[:nt], layouts, refs[nt], refs[nt + 1]):
            send.start()
        refs[-1][...] = jnp.zeros_like(refs[-1])

    out = pl.pallas_call(
        body, name=name,
        out_shape=(pltpu.SemaphoreType.DMA((3 * nt,)), pltpu.SemaphoreType.DMA((3 * nt,)),
                   *[pltpu.HBM(a.shape, a.dtype) for a in lands], _TOKEN),
        in_specs=[_HBM] * nt, out_specs=(_SEM, _SEM, *[_HBM] * nt, pl.BlockSpec(memory_space=pltpu.VMEM)),
        input_output_aliases={t: 2 + t for t in range(nt)}, compiler_params=_SPLIT_COPY,
    )(*_in_hbm(lands))
    return out[0], out[1], list(out[2:2 + nt]), out[-1]


def _gather_forward_wait(send_sems, recv_sems, lands, layouts, after, name):
    nt = len(lands)

    def body(*refs):
        for send, arrival in _forward_copies(refs[:nt], layouts, refs[nt], refs[nt + 1]):
            send.wait_send()
            arrival.wait_recv()

    return list(pl.pallas_call(
        body, name=name, out_shape=tuple(pltpu.HBM(a.shape, a.dtype) for a in lands),
        in_specs=[_HBM] * nt + [_SEM, _SEM, _ANY], out_specs=tuple([_HBM] * nt),
        input_output_aliases={t: t for t in range(nt)}, compiler_params=_SPLIT_COPY,
    )(*lands, send_sems, recv_sems, after))


def _pair_copies(grads, lands, send_sems, recv_sems):
    x, y, c = _place()
    out = []
    for t, (g, land) in enumerate(zip(grads, lands)):
        h = g.shape[1] // 2
        out.append(pltpu.make_async_remote_copy(
            src_ref=g.at[:, pl.ds((1 - c) * h, h), :], dst_ref=land, send_sem=send_sems.at[t],
            recv_sem=recv_sems.at[t], device_id=(x, y, 1 - c), device_id_type=_MESH))
    return out


def _pair_start(grads, after, name):
    nt = len(grads)
    lands = [lax.empty((N_CHIPS, g.shape[1] // 2, g.shape[2]), g.dtype) for g in grads]
    order = [] if after is None else [after]

    def body(*refs):
        send_sems, recv_sems = refs[2 * nt + len(order):2 * nt + len(order) + 2]
        token = refs[-1]
        for cp in _pair_copies(refs[:nt], refs[nt:2 * nt], send_sems, recv_sems):
            cp.start()
        token[...] = jnp.zeros_like(token)

    out = pl.pallas_call(
        body, name=name,
        out_shape=(pltpu.SemaphoreType.DMA((nt,)), pltpu.SemaphoreType.DMA((nt,)),
                   *[pltpu.HBM(a.shape, a.dtype) for a in grads + lands], _TOKEN),
        in_specs=[_HBM] * (2 * nt) + [_ANY] * len(order),
        out_specs=(_SEM, _SEM, *[_HBM] * (2 * nt), pl.BlockSpec(memory_space=pltpu.VMEM)),
        input_output_aliases={t: 2 + t for t in range(2 * nt)}, compiler_params=_SPLIT_COPY,
    )(*_in_hbm(grads + lands), *order)
    return out[0], out[1], list(out[2:2 + nt]), list(out[2 + nt:2 + 2 * nt]), out[-1]


def _pair_wait(send_sems, recv_sems, grads, lands, after, name):
    nt = len(grads)

    def body(*refs):
        for cp in _pair_copies(refs[:nt], refs[nt:2 * nt], *refs[2 * nt:2 * nt + 2]):
            cp.wait_send()
            cp.wait_recv()

    out = pl.pallas_call(
        body, name=name, out_shape=tuple(pltpu.HBM(a.shape, a.dtype) for a in grads + lands),
        in_specs=[_HBM] * (2 * nt) + [_SEM, _SEM, _ANY], out_specs=tuple([_HBM] * (2 * nt)),
        input_output_aliases={t: t for t in range(2 * nt)}, compiler_params=_SPLIT_COPY,
    )(*grads, *lands, send_sems, recv_sems, after)
    return list(out[:nt]), list(out[nt:])


def _pair_sum(own, recv, c_idx, name):
    _, h, cols = recv.shape

    def body(c_ref, own_ref, recv_ref, o_ref):
        o_ref[...] = (own_ref[...] + recv_ref[...]).astype(BF16)

    return pl.pallas_call(
        body, name=name,
        grid_spec=pltpu.PrefetchScalarGridSpec(
            num_scalar_prefetch=1, grid=(N_CHIPS,),
            in_specs=[pl.BlockSpec((None, h, cols), lambda k, c_ref: (k, c_ref[0], 0)),
                      pl.BlockSpec((None, h, cols), lambda k, c_ref: (k, 0, 0))],
            out_specs=pl.BlockSpec((None, h, cols), lambda k, c_ref: (k, 0, 0))),
        out_shape=jax.ShapeDtypeStruct(recv.shape, BF16), compiler_params=_cparams(("arbitrary",)),
    )(c_idx, own, recv)


def _chip_copies(parts, lands, send_sems, recv_sems):
    x, y, c = _place()
    out = []
    for t, (part, land) in enumerate(zip(parts, lands)):
        for j, (px, py) in enumerate(_other_chips(x, y)):
            out.append(pltpu.make_async_remote_copy(
                src_ref=part.at[2 * px + py], dst_ref=land.at[j], send_sem=send_sems.at[3 * t + j],
                recv_sem=recv_sems.at[3 * t + j], device_id=(px, py, c), device_id_type=_MESH))
    return out


def _chip_send_start(parts, after, name):
    nt = len(parts)
    lands = [lax.empty((N_CHIPS - 1,) + p.shape[1:], p.dtype) for p in parts]
    order = [] if after is None else [after]

    def body(*refs):
        send_sems, recv_sems = refs[2 * nt + len(order):2 * nt + len(order) + 2]
        token = refs[-1]
        for cp in _chip_copies(refs[:nt], refs[nt:2 * nt], send_sems, recv_sems):
            cp.start()
        token[...] = jnp.zeros_like(token)

    out = pl.pallas_call(
        body, name=name,
        out_shape=(pltpu.SemaphoreType.DMA((3 * nt,)), pltpu.SemaphoreType.DMA((3 * nt,)),
                   *[pltpu.HBM(a.shape, a.dtype) for a in parts + lands], _TOKEN),
        in_specs=[_HBM] * (2 * nt) + [_ANY] * len(order),
        out_specs=(_SEM, _SEM, *[_HBM] * (2 * nt), pl.BlockSpec(memory_space=pltpu.VMEM)),
        input_output_aliases={t: 2 + t for t in range(2 * nt)}, compiler_params=_SPLIT_COPY,
    )(*_in_hbm(parts + lands), *order)
    return out[0], out[1], list(out[2:2 + nt]), list(out[2 + nt:2 + 2 * nt]), out[-1]


def _chip_send_wait(send_sems, recv_sems, parts, lands, after, name):
    nt = len(parts)

    def body(*refs):
        for cp in _chip_copies(refs[:nt], refs[nt:2 * nt], *refs[2 * nt:2 * nt + 2]):
            cp.wait_send()
            cp.wait_recv()

    out = pl.pallas_call(
        body, name=name, out_shape=tuple(pltpu.HBM(a.shape, a.dtype) for a in parts + lands),
        in_specs=[_HBM] * (2 * nt) + [_SEM, _SEM, _ANY], out_specs=tuple([_HBM] * (2 * nt)),
        input_output_aliases={t: t for t in range(2 * nt)}, compiler_params=_SPLIT_COPY,
    )(*parts, *lands, send_sems, recv_sems, after)
    return list(out[:nt]), list(out[nt:])


def _chip_sum(part, arrived, into, lead, place_idx, name):
    _, h, cols = part.shape

    def body(idx_ref, own_ref, arr_ref, into_ref, o_ref):
        acc = own_ref[...].astype(F32)
        for k in range(N_CHIPS - 1):
            acc = acc + arr_ref[k].astype(F32)
        o_ref[...] = acc

    return pl.pallas_call(
        body, name=name,
        grid_spec=pltpu.PrefetchScalarGridSpec(
            num_scalar_prefetch=1, grid=(1,),
            in_specs=[pl.BlockSpec((None, h, cols), lambda g, idx: (idx[1], 0, 0)),
                      pl.BlockSpec((N_CHIPS - 1, h, cols), lambda g, idx: (0, 0, 0)), _ANY],
            out_specs=pl.BlockSpec((None,) * len(lead) + (h, cols), lambda g, idx: (*lead, idx[0], 0))),
        out_shape=jax.ShapeDtypeStruct(into.shape, F32), input_output_aliases={3: 0},
        compiler_params=_cparams(("arbitrary",)),
    )(place_idx, part, arrived, into)


def _pair_gather(bufs, homes, name):
    nt, nb = len(homes), len(bufs)

    def body(*refs):
        outs = refs[nb:2 * nb]
        send_sems, recv_sems = refs[2 * nb:]
        x, y, c = _place()

        def home(t, pc):
            o, lead, rows = homes[t]
            return outs[o].at[(*lead, pl.ds(pc * (rows // 2), rows // 2), slice(None))]

        def copy(t, pc):
            return pltpu.make_async_remote_copy(src_ref=home(t, pc), dst_ref=home(t, pc), send_sem=send_sems.at[t],
                                                recv_sem=recv_sems.at[t], device_id=(x, y, 1 - c), device_id_type=_MESH)

        sends = [copy(t, c) for t in range(nt)]
        for cp in sends:
            cp.start()
        for t in range(nt):
            copy(t, 1 - c).wait_recv()
        for cp in sends:
            cp.wait_send()

    return pl.pallas_call(
        body, name=name, out_shape=[jax.ShapeDtypeStruct(b.shape, b.dtype) for b in bufs],
        in_specs=[_ANY] * nb, out_specs=[_ANY] * nb, input_output_aliases={o: o for o in range(nb)},
        scratch_shapes=[pltpu.SemaphoreType.DMA((nt,)), pltpu.SemaphoreType.DMA((nt,))],
        compiler_params=_cparams(),
    )(*bufs)


def _sum_devices(g, after, name):
    def body(g_ref, after_ref, o_ref):
        acc = g_ref[0:1, :]
        for d in range(1, N_DEV):
            acc = acc + g_ref[d:d + 1, :]
        o_ref[...] = acc
    vmem = pl.BlockSpec(memory_space=pltpu.VMEM)
    return pl.pallas_call(body, name=name, out_shape=jax.ShapeDtypeStruct((1, g.shape[1]), F32),
                          in_specs=[vmem, _ANY], out_specs=vmem, compiler_params=_cparams())(g, after)


_WEIGHTS = ("w_cond", "b_cond", "norm_pre", "norm_post", "w_ffn_in", "w_ffn_out", "fox_w_in", "fox_b_f",
            "fox_w_out", "sconv_w_in", "sconv_conv_w", "sconv_w_out", "lru_w_in", "lru_conv_w", "lru_conv_b",
            "lru_w_a", "lru_b_a", "lru_w_x", "lru_b_x", "lru_lambda", "lru_w_out")
_BIG = (("w_ffn_in", False), ("w_ffn_out", True), ("fox_w_in", False), ("fox_w_out", True),
        ("sconv_w_in", False), ("sconv_w_out", True), ("lru_w_in", False), ("lru_w_out", True))
_SMALL = tuple(n for n in _WEIGHTS if n != "w_cond" and n not in dict(_BIG))
_COL_SHARDED_SMALL = ("norm_pre", "norm_post", "sconv_conv_w", "lru_conv_w", "lru_conv_b", "lru_lambda")


def _pack_rows(parts, rows=8):
    flat = jnp.concatenate([p.reshape(-1) for p in parts])
    width = -(-flat.size // (rows * 128)) * 128
    return jnp.pad(flat, (0, rows * width - flat.size)).reshape(rows, width)


def _unpack(flat, shapes):
    out, off = [], 0
    for shp in shapes:
        n = math.prod(shp)
        out.append(flat[off:off + n].reshape(shp))
        off += n
    return out


def _join_chips(g):
    g = jnp.moveaxis(g, 0, -2)
    return g.reshape(g.shape[:-2] + (g.shape[-2] * g.shape[-1],))


def _my_columns(full, chip):
    n = full.shape[-1] // N_CHIPS
    return lax.dynamic_slice_in_dim(full, chip * n, n, axis=full.ndim - 1)


def _block_diag(w):
    eye = jnp.eye(LRU_BLOCKS, dtype=w.dtype)
    return jnp.einsum("nij,nm->nimj", w, eye).reshape(D_MODEL, D_MODEL)


def _step(x, c, target, wts, mom, var):
    ix, iy, ic = _place()
    chip = 2 * ix + iy
    dev = 2 * chip + ic
    n_cond = wts["w_cond"].shape[2]

    small_shapes = [(D_MODEL,)] + [wts[n].shape for n in _COL_SHARDED_SMALL]
    g1 = _allgather8(_pack_rows([c[0]] + [wts[n] for n in _COL_SHARDED_SMALL]), "gather_small").reshape(N_DEV, -1)
    c_all = g1[:, :D_MODEL]
    per_chip = [jnp.stack(col) for col in zip(*[_unpack(g1[2 * k], small_shapes) for k in range(N_CHIPS)])]
    small_full = {n: _join_chips(v) for n, v in zip(_COL_SHARDED_SMALL, per_chip[1:])}

    c_pad = jnp.pad(c_all, ((0, COND_ROWS - N_DEV), (0, 0)))
    b_shard = _my_columns(wts["b_cond"], chip)[:, None, :]
    mod_part = _cond_fwd(c_pad, wts["w_cond"], b_shard, "cond_fwd")
    g2 = _allgather8(mod_part[:, :N_DEV].transpose(1, 0, 2).reshape(N_DEV, DEPTH * n_cond), "gather_mod")
    g2 = g2.reshape(N_DEV, N_DEV, DEPTH, n_cond)[0::2]
    mod = _join_chips(lax.dynamic_index_in_dim(g2, dev, axis=1, keepdims=False)).reshape(DEPTH, N_SUB, 3, D_MODEL)

    mixer_names = [("fox_w_in", "fox_w_out"), ("sconv_w_in", "sconv_w_out"), ("lru_w_in", "lru_w_out")]

    def shards_of(i, sub):
        if sub == 1:
            return [wts[n][i // 3] for n in mixer_names[i % 3]]
        return [wts["w_ffn_in"][i, sub // 2], wts["w_ffn_out"][i, sub // 2]]

    chunks = [[(0, sub)] for sub in range(N_SUB)] + [[(i, sub) for sub in range(N_SUB)] for i in range(1, DEPTH)]
    in_flight, chunk_of, token = [], {}, mod
    for k, members in enumerate(chunks):
        shards = [s for i, sub in members for s in shards_of(i, sub)]
        layouts = [_Gathered(s.shape, 0) for s in shards]
        if k:
            shards = [s + token[0, 0] for s in shards]
        lands = [_own_block_placed(s.astype(BF16), lay, chip) for s, lay in zip(shards, layouts)]
        send_sems, recv_sems, lands, token = _gather_start(lands, layouts, token, f"gather_start_{k}")
        in_flight.append([send_sems, recv_sems, lands, layouts, False])
        chunk_of.update({m: (k, 2 * pos) for pos, m in enumerate(members)})
    lru_ax = jnp.concatenate([_block_diag(wts["lru_w_a"][0]), _block_diag(wts["lru_w_x"][0])], axis=1).astype(BF16)

    prefetch_at = {(i, N_SUB - 1): i + N_SUB for i in range(DEPTH - 1)}

    def layer_params(i, sub, x_in):
        k, pos = chunk_of[(i, sub)]
        send_sems, recv_sems, lands, layouts, state = in_flight[k]
        if state == "passing":
            in_flight[k][2:] = [_gather_forward_wait(send_sems, recv_sems, lands, layouts, x_in, f"gather_pass_wait_{k}"),
                                layouts, "here"]
        elif state != "here":
            lands = _gather_wait(send_sems, recv_sems, lands, layouts, x_in, f"gather_wait_{k}")
            in_flight[k][2:] = [_gather_forward(lands, layouts, f"gather_forward_{k}"), layouts, "here"]
        nxt = prefetch_at.get((i, sub))
        started = None
        if nxt is not None:
            send_sems, recv_sems, lands, layouts, _ = in_flight[nxt]
            lands = _gather_wait(send_sems, recv_sems, lands, layouts, x_in, f"gather_wait_{nxt}")
            send_sems, recv_sems, lands, started = _gather_forward_start(lands, layouts, f"gather_pass_start_{nxt}")
            in_flight[nxt] = [send_sems, recv_sems, lands, layouts, "passing"]
        w_in, w_out = in_flight[k][2][pos:pos + 2]
        w_out = w_out.reshape(-1, w_out.shape[-1])
        if sub != 1:
            out = {"ffn_in": [_W(w_in, (), True)], "ffn_out": [_W(w_out)], "after": started}
            if sub == 0:
                out.update(norm_pre=small_full["norm_pre"][i], norm_post=small_full["norm_post"][i])
            return out
        j = i // 3
        if i % 3 == 0:
            w_in = jnp.pad(_join_chips(w_in), ((0, 0), (0, FOX_PAD - 3 * D_MODEL - FOX_HEADS)))
            return {"mixer": {"w_in": _W(w_in), "w_out": _W(w_out), "b_f": wts["fox_b_f"][j][:, None]}}
        if i % 3 == 1:
            return {"mixer": {"w_in": _W(w_in, (), True), "w_out": _W(w_out), "conv_w": small_full["sconv_conv_w"][j]}}
        return {"mixer": {"w_in": _W(w_in, (), True), "w_out": _W(w_out), "conv_w": small_full["lru_conv_w"][j],
                          "conv_b": small_full["lru_conv_b"], "w_ax": _W(lru_ax),
                          "b_a": wts["lru_b_a"].reshape(1, D_MODEL), "b_x": wts["lru_b_x"].reshape(1, D_MODEL),
                          "lam": small_full["lru_lambda"]}}

    place_idx = jnp.stack([ic, chip]).astype(jnp.int32)
    c_idx = place_idx[:1]
    big_index = {n: o for o, (n, _) in enumerate(_BIG)}
    exchanges, pending = [], []

    def to_chips(after):
        i, send_sems, recv_sems, tensors, lands, homes = pending.pop()
        tensors, recv = _pair_wait(send_sems, recv_sems, tensors, lands, after, f"grads_pair_wait_l{i}")
        parts = [_pair_sum(t, r, c_idx, f"grads_pair_sum_l{i}_{k}") for k, (t, r) in enumerate(zip(tensors, recv))]
        send_sems, recv_sems, parts, lands, tok = _chip_send_start(parts, None, f"grads_chip_start_l{i}")
        exchanges.append((i, send_sems, recv_sems, parts, lands, homes))
        return tok

    def chip_blocks(g, by_rows, width):
        if by_rows:
            return g.reshape(N_CHIPS, g.shape[0] // N_CHIPS, g.shape[1])
        if g.ndim == 3:
            return g
        return g[:, :width * N_CHIPS].reshape(g.shape[0], N_CHIPS, width).transpose(1, 0, 2)

    def on_mid(i, dx):
        return to_chips(dx) if pending else None

    def on_grads(i, g, dx):
        n_in, n_out = mixer_names[i % 3]
        items = [("w_ffn_in", (i, k), g["ffn_in"][k]) for k in range(2)]
        items += [("w_ffn_out", (i, k), g["ffn_out"][k]) for k in range(2)]
        items += [(n_in, (i // 3,), g["mixer"]["w_in"]), (n_out, (i // 3,), g["mixer"]["w_out"])]
        tensors = [chip_blocks(t, dict(_BIG)[n], wts[n].shape[-1]) for n, _, t in items]
        homes = [(big_index[n], lead, wts[n].shape[-2]) for n, lead, _ in items]
        send_sems, recv_sems, tensors, lands, tok = _pair_start(tensors, None, f"grads_pair_start_l{i}")
        pending.append((i, send_sems, recv_sems, tensors, lands, homes))
        pair_tokens.append(tok)
        return tok

    pair_tokens = []
    loss_row, grad_x, dmod, lg = _local_step(x[0], target[0], mod, layer_params, on_grads, on_mid, token)
    loss = lax.psum(loss_row[0, 0], ("x", "y", "c"))
    dmod = dmod + pair_tokens[-1][0, 0]

    fox_layers = [i for i in range(DEPTH) if i % 3 == 0]
    sconv_g, lru_g = lg[1]["mixer"], lg[2]["mixer"]
    small_g = {
        "dmod": dmod, "norm_pre": jnp.stack([g["norm_pre"] for g in lg]), "norm_post": jnp.stack([g["norm_post"] for g in lg]),
        "fox_b_f": jnp.stack([lg[i]["mixer"]["b_f"][:, 0] for i in fox_layers]),
        "sconv_conv_w": sconv_g["conv_w"][None], "lru_conv_w": lru_g["conv_w"][None], "lru_conv_b": lru_g["conv_b"],
        "lru_w_a": lru_g["w_a"][None], "lru_b_a": lru_g["b_a"].reshape(1, LRU_BLOCKS, LRU_BLOCK_DIM),
        "lru_w_x": lru_g["w_x"][None], "lru_b_x": lru_g["b_x"].reshape(1, LRU_BLOCKS, LRU_BLOCK_DIM),
        "lru_lambda": lru_g["lam"]}
    g4 = _allgather8(_pack_rows(list(small_g.values())), "gather_small_grads").reshape(N_DEV, -1)
    last_start = to_chips(g4)
    summed = _sum_devices(g4, last_start, "sum_small_grads")[0]
    summed = dict(zip(small_g, _unpack(summed, [v.shape for v in small_g.values()])))
    grads = {n: (_my_columns(summed[n], chip) if n in _COL_SHARDED_SMALL else summed[n]) for n in _SMALL if n != "b_cond"}
    grads["b_cond"] = summed["dmod"].reshape(DEPTH, N_SUB * 3 * D_MODEL)

    dmod_all = (g4[:, :dmod.size] + last_start[0, 0]).reshape(N_DEV, DEPTH, N_SUB * 3 * D_MODEL)
    dmod_s = jnp.pad(_my_columns(dmod_all, chip).transpose(1, 0, 2), ((0, 0), (0, COND_PAD - N_DEV), (0, 0))).astype(BF16)
    c_t = jnp.pad(c_all.T, ((0, 0), (0, COND_PAD - N_DEV)))
    grads["w_cond"], d_cond, m_cond, v_cond = _cond_bwd_adamw(c_t, dmod_s, wts["w_cond"], mom["w_cond"],
                                                              var["w_cond"], "cond_bwd_adamw")

    big = [n for n, _ in _BIG]
    two_d = lambda a: a.reshape(-1, a.shape[-1])
    bufs = [lax.empty(wts[n].shape, F32) for n in big]
    updates = [[lax.empty(two_d(wts[n]).shape, F32) for _ in range(3)] for n in big]
    follows = d_cond
    for i, send_sems, recv_sems, parts, lands, homes in exchanges:
        parts, lands = _chip_send_wait(send_sems, recv_sems, parts, lands, follows, f"grads_chip_wait_l{i}")
        for k, (part, land, (o, lead, _)) in enumerate(zip(parts, lands, homes)):
            bufs[o] = _chip_sum(part, land, bufs[o], lead, place_idx, f"grads_chip_sum_l{i}_{k}")
        bufs = list(_pair_gather(bufs, homes, f"grads_pair_gather_l{i}"))
        for o in sorted({o for o, _, _ in homes}):
            n = big[o]
            starts = [sum(a * math.prod(wts[n].shape[d + 1:-1]) for d, a in enumerate(lead))
                      for oo, lead, _ in homes if oo == o]
            rows = wts[n].shape[-2]
            *updates[o], g_out = _adamw_rows(two_d(wts[n]), two_d(bufs[o]), two_d(mom[n]), two_d(var[n]), updates[o],
                                             min(starts), max(starts) + rows - min(starts), f"adamw_{n}_l{i}")
            bufs[o] = g_out.reshape(wts[n].shape)
        follows = updates[0][0]
    grads.update(zip(big, bufs))

    delta, new_m, new_v = {"w_cond": d_cond}, {"w_cond": m_cond}, {"w_cond": v_cond}
    for n, (d, nm, nv) in zip(big, updates):
        delta[n], new_m[n], new_v[n] = (a.reshape(wts[n].shape) for a in (d, nm, nv))
    shapes = [wts[n].shape for n in _SMALL]
    packed = [_pack_rows([src[n] for n in _SMALL]) for src in (wts, grads, mom, var)]
    for dst, out in zip((delta, new_m, new_v), _adamw(*packed, "adamw_small")):
        dst.update(zip(_SMALL, _unpack(out.reshape(-1), shapes)))

    return (loss, grad_x[None], *[grads[n] for n in _WEIGHTS], *[delta[n] for n in _WEIGHTS],
            *[new_m[n] for n in _WEIGHTS], *[new_v[n] for n in _WEIGHTS])


def kernel(x, c, w_cond, b_cond, norm_pre, norm_post, w_ffn_in, w_ffn_out, fox_w_in, fox_b_f, fox_w_out, sconv_w_in, sconv_conv_w, sconv_w_out, lru_w_in, lru_conv_w, lru_conv_b, lru_w_a, lru_b_a, lru_w_x, lru_b_x, lru_lambda, lru_w_out, loss_target, m_w_cond, m_b_cond, m_norm_pre, m_norm_post, m_w_ffn_in, m_w_ffn_out, m_fox_w_in, m_fox_b_f, m_fox_w_out, m_sconv_w_in, m_sconv_conv_w, m_sconv_w_out, m_lru_w_in, m_lru_conv_w, m_lru_conv_b, m_lru_w_a, m_lru_b_a, m_lru_w_x, m_lru_b_x, m_lru_lambda, m_lru_w_out, v_w_cond, v_b_cond, v_norm_pre, v_norm_post, v_w_ffn_in, v_w_ffn_out, v_fox_w_in, v_fox_b_f, v_fox_w_out, v_sconv_w_in, v_sconv_conv_w, v_sconv_w_out, v_lru_w_in, v_lru_conv_w, v_lru_conv_b, v_lru_w_a, v_lru_b_a, v_lru_w_x, v_lru_b_x, v_lru_lambda, v_lru_w_out):
    given = dict(locals())
    wts = {n: given[n] for n in _WEIGHTS}
    mom = {n: given["m_" + n] for n in _WEIGHTS}
    var = {n: given["v_" + n] for n in _WEIGHTS}
    return _step(x, c, loss_target, wts, mom, var)
```

```python
import functools
import math
from typing import NamedTuple

import jax
import jax.numpy as jnp
from jax import lax
from jax.experimental import pallas as pl
from jax.experimental.pallas import tpu as pltpu

F32 = jnp.float32
BF16 = jnp.bfloat16

D_MODEL = 1024
DEPTH = 4
N_SUB = 3
D_FF = 2816
RMS_EPS = 1e-6
FOX_HEADS = 16
FOX_HEAD_DIM = 64
FOX_PAD = 3200
LRU_BLOCKS = 16
LRU_BLOCK_DIM = 64
LRU_C = 8.0
N_CHIPS = 4
N_DEV = 8

ADAM_LR = 0.001
ADAM_B1 = 0.9
ADAM_B2 = 0.999
ADAM_EPS = 1e-08
ADAM_WD = 0.01
ADAM_STEP = 10

VMEM_LIMIT_V7X = 56 * 1024 * 1024
ROW_TILE = 512
COL_TILE = 256
ATT_TILE = 512
ATT_WIDE = 512
MM_ROWS = 1024


def _cparams(sem=None):
    return pltpu.CompilerParams(vmem_limit_bytes=VMEM_LIMIT_V7X, dimension_semantics=sem)


def _sigmoid(z):
    return 1.0 / (1.0 + jnp.exp(-z))


def _softplus(z):
    return jnp.maximum(z, 0.0) + jnp.log(1.0 + jnp.exp(-jnp.abs(z)))


def _rows_sum(v):
    return jnp.sum(v, axis=0, keepdims=True)


class _W(NamedTuple):
    arr: jax.Array
    prefix: tuple = ()
    blocked: bool = False


def _w_spec(w, block2, pos):
    lead = (None,) * (len(w.prefix) + (1 if w.blocked else 0))
    if w.blocked:
        return pl.BlockSpec(lead + block2, lambda *g: (pos(*g)[0], *w.prefix, pos(*g)[1], pos(*g)[2]))
    return pl.BlockSpec(lead + block2, lambda *g: (*w.prefix, pos(*g)[1], pos(*g)[2]))


def _mm_nn(a, b, name, tn=None, cols=None, out_dtype=F32):
    m, k = a.shape
    if b.blocked:
        steps, bn = b.arr.shape[0], b.arr.shape[-1]
        b_spec = _w_spec(b, (k, bn), lambda n: (n, 0, 0))
    else:
        first, last = (0, b.arr.shape[-1]) if cols is None else cols
        n_total = last - first
        bn = n_total if tn is None else tn
        steps = n_total // bn
        assert steps * bn == n_total and first % bn == 0
        b_spec = _w_spec(b, (k, bn), lambda n: (0, 0, n + first // bn))
    tm = min(MM_ROWS, m)

    def body(a_ref, b_ref, o_ref):
        def step(i, carry):
            r = pl.ds(pl.multiple_of(i * tm, tm), tm)
            o_ref[r, :] = jnp.dot(a_ref[r, :], b_ref[...], preferred_element_type=F32).astype(out_dtype)
            return carry
        lax.fori_loop(0, m // tm, step, 0)

    return pl.pallas_call(
        body, name=name, grid=(steps,),
        in_specs=[pl.BlockSpec((m, k), lambda n: (0, 0)), b_spec],
        out_specs=pl.BlockSpec((m, bn), lambda n: (0, n)),
        out_shape=jax.ShapeDtypeStruct((m, steps * bn), out_dtype),
        compiler_params=_cparams(("arbitrary",)),
    )(a, b.arr)


def _cols_shape(dy):
    return (dy.shape[0], dy.shape[1]) if dy.ndim == 2 else (dy.shape[1], 2 * dy.shape[2])


def _cols_spec(dy, bn):
    if dy.ndim == 2:
        return pl.BlockSpec((dy.shape[0], bn), lambda kt, n: (0, n))
    per = dy.shape[2] // bn
    assert per * bn == dy.shape[2]
    return pl.BlockSpec((None, dy.shape[1], bn), lambda kt, n: (n // per, 0, n % per))


def _mm_nt(dy, w, name, tk=None, tn=None):
    m, n_total = _cols_shape(dy)
    k = w.arr.shape[-2]
    if w.blocked:
        bk, bn = k, w.arr.shape[-1]
        grid = (1, w.arr.shape[0])
        w_spec = _w_spec(w, (k, bn), lambda kt, n: (n, 0, 0))
    else:
        bk = k if tk is None else tk
        bn = n_total if tn is None else tn
        grid = (k // bk, n_total // bn)
        assert grid[0] * bk == k and grid[1] * bn == n_total
        w_spec = _w_spec(w, (bk, bn), lambda kt, n: (0, kt, n))
    tm = min(MM_ROWS, m)

    reduce_steps = grid[1]

    def body(dy_ref, w_ref, o_ref):
        def step(i, carry):
            r = pl.ds(pl.multiple_of(i * tm, tm), tm)
            part = lax.dot_general(dy_ref[r, :], w_ref[...], (((1,), (1,)), ((), ())), preferred_element_type=F32)
            if reduce_steps == 1:
                o_ref[r, :] = part
            else:
                o_ref[r, :] += part
            return carry

        if reduce_steps > 1:
            @pl.when(pl.program_id(1) == 0)
            def _():
                o_ref[...] = jnp.zeros_like(o_ref)
        lax.fori_loop(0, m // tm, step, 0)

    return pl.pallas_call(
        body, name=name, grid=grid,
        in_specs=[_cols_spec(dy, bn), w_spec],
        out_specs=pl.BlockSpec((m, bk), lambda kt, n: (0, kt)),
        out_shape=jax.ShapeDtypeStruct((m, k), F32),
        compiler_params=_cparams(("arbitrary", "arbitrary")),
    )(dy, w.arr)


def _mm_tn(x, dy, name, tk=None, tn=None, blocked_out=False):
    s, k = x.shape
    n_total = _cols_shape(dy)[1]
    bk = k if tk is None else tk
    bn = n_total if tn is None else tn
    grid = (k // bk, n_total // bn)
    assert grid[0] * bk == k and grid[1] * bn == n_total
    ck = next(c for c in (512, 256, 128) if bk % c == 0)

    def body(x_ref, dy_ref, o_ref):
        def step(i, carry):
            c = pl.ds(pl.multiple_of(i * ck, ck), ck)
            o_ref[c, :] = lax.dot_general(x_ref[:, c], dy_ref[...], (((0,), (0,)), ((), ())),
                                          preferred_element_type=F32)
            return carry
        lax.fori_loop(0, bk // ck, step, 0)

    if blocked_out:
        assert grid[0] == 1
        out_spec = pl.BlockSpec((None, bk, bn), lambda kt, n: (n, 0, 0))
        out_shape = jax.ShapeDtypeStruct((grid[1], k, bn), F32)
    else:
        out_spec = pl.BlockSpec((bk, bn), lambda kt, n: (kt, n))
        out_shape = jax.ShapeDtypeStruct((k, n_total), F32)
    return pl.pallas_call(
        body, name=name, grid=grid,
        in_specs=[pl.BlockSpec((s, bk), lambda kt, n: (0, kt)), _cols_spec(dy, bn)],
        out_specs=out_spec, out_shape=out_shape,
        compiler_params=_cparams(("arbitrary", "arbitrary")),
    )(x, dy)


def _row_call(name, body, rows, fulls, row_outs, acc_outs, tr=ROW_TILE, after=None):
    s = rows[0].shape[0]
    tr = min(tr, s)
    in_specs = [pl.BlockSpec((tr, a.shape[1]), lambda i: (i, 0)) for a in rows]
    in_specs += [pl.BlockSpec(a.shape, lambda i: (0, 0)) for a in fulls]
    n_in = len(in_specs)
    order = [] if after is None else [after]
    in_specs += [pl.BlockSpec(memory_space=pl.ANY)] * len(order)
    out_specs = [pl.BlockSpec((tr, c), lambda i: (i, 0)) for c, _ in row_outs]
    out_specs += [pl.BlockSpec((1, c), lambda i: (0, 0)) for c, _ in acc_outs]
    out_shape = [jax.ShapeDtypeStruct((s, c), dt) for c, dt in row_outs]
    out_shape += [jax.ShapeDtypeStruct((1, c), dt) for c, dt in acc_outs]
    n_acc = len(acc_outs)

    def wrapped(*refs):
        refs = refs[:n_in] + refs[n_in + len(order):]
        if n_acc:
            @pl.when(pl.program_id(0) == 0)
            def _():
                for r in refs[len(refs) - n_acc:]:
                    r[...] = jnp.zeros_like(r)
        body(*refs)

    return pl.pallas_call(
        wrapped, name=name, grid=(s // tr,), in_specs=in_specs, out_specs=out_specs, out_shape=out_shape,
        compiler_params=_cparams(("arbitrary",)),
    )(*rows, *fulls, *order)


def _rms(v):
    return lax.rsqrt(jnp.mean(v * v, axis=-1, keepdims=True) + RMS_EPS)


def _pre_norm(x, g_pre, scale, shift, name, after=None):
    def body(x_ref, g_ref, sc_ref, sh_ref, h_ref):
        xv = x_ref[...]
        h = (xv * _rms(xv)) * g_ref[...] * (1.0 + sc_ref[...]) + sh_ref[...]
        h_ref[...] = h.astype(BF16)
    return _row_call(name, body, [x], [g_pre, scale, shift], [(D_MODEL, BF16)], [], after=after)[0]


def _post_norm(x, y, g_post, gate, coef, name):
    def body(x_ref, y_ref, g_ref, gate_ref, o_ref):
        yv = y_ref[...]
        o_ref[...] = x_ref[...] + (coef * gate_ref[...]) * ((yv * _rms(yv)) * g_ref[...])
    return _row_call(name, body, [x, y], [g_post, gate], [(D_MODEL, F32)], [])[0]


def _post_norm_bwd(dxo, y, g_post, gate, coef, name, after=None):
    def body(dxo_ref, y_ref, g_ref, gate_ref, dy_ref, dgate_ref, dg_ref):
        yv = y_ref[...]
        r2 = _rms(yv)
        yn = yv * r2
        dxo_v = dxo_ref[...]
        dgate_ref[...] += _rows_sum(dxo_v * (yn * g_ref[...])) * coef
        dz = dxo_v * (coef * gate_ref[...])
        dg_ref[...] += _rows_sum(dz * yn)
        dyn = dz * g_ref[...]
        dy = r2 * (dyn - yn * jnp.mean(dyn * yn, axis=-1, keepdims=True))
        dy_ref[...] = dy.astype(BF16)
    return _row_call(name, body, [dxo, y], [g_post, gate], [(D_MODEL, BF16)], [(D_MODEL, F32), (D_MODEL, F32)],
                     after=after)


def _pre_norm_bwd(dxo, dh, x, g_pre, scale, name):
    def body(dxo_ref, dh_ref, x_ref, g_ref, sc_ref, dx_ref, dshift_ref, dscale_ref, dg_ref):
        xv = x_ref[...]
        r = _rms(xv)
        xn = xv * r
        dh_v = dh_ref[...]
        one_sc = 1.0 + sc_ref[...]
        dshift_ref[...] += _rows_sum(dh_v)
        dscale_ref[...] += _rows_sum(dh_v * (xn * g_ref[...]))
        dg_ref[...] += _rows_sum(dh_v * xn * one_sc)
        dxn = dh_v * (g_ref[...] * one_sc)
        dx_ref[...] = dxo_ref[...] + r * (dxn - xn * jnp.mean(dxn * xn, axis=-1, keepdims=True))
    return _row_call(name, body, [dxo, dh, x], [g_pre, scale], [(D_MODEL, F32)],
                     [(D_MODEL, F32), (D_MODEL, F32), (D_MODEL, F32)])


FFN_COLS = 1408


def _ffn_in_act(h, w_in, name, after=None):
    m, k = h.shape
    half, bn = w_in.arr.shape[0] // 2, w_in.arr.shape[-1]
    assert bn == FFN_COLS and half * bn == D_FF
    tm = min(MM_ROWS, m)
    order = [] if after is None else [after]

    def body(h_ref, wg_ref, wu_ref, *rest):
        g_ref, u_ref, a_ref = rest[len(order):]
        g = jnp.dot(h_ref[...], wg_ref[...], preferred_element_type=F32)
        g_ref[...] = g
        u = jnp.dot(h_ref[...], wu_ref[...], preferred_element_type=F32)
        u_ref[...] = u
        a_ref[...] = (g * _sigmoid(g) * u).astype(BF16)

    tile = pl.BlockSpec((tm, bn), lambda t, i: (i, t))
    return pl.pallas_call(
        body, name=name, grid=(half, m // tm),
        in_specs=[pl.BlockSpec((tm, k), lambda t, i: (i, 0)),
                  _w_spec(w_in, (k, bn), lambda t, i: (t, 0, 0)),
                  _w_spec(w_in, (k, bn), lambda t, i: (half + t, 0, 0))] + [pl.BlockSpec(memory_space=pl.ANY)] * len(order),
        out_specs=[tile, tile, tile],
        out_shape=[jax.ShapeDtypeStruct((m, D_FF), F32)] * 2 + [jax.ShapeDtypeStruct((m, D_FF), BF16)],
        compiler_params=_cparams(("arbitrary", "arbitrary")),
    )(h, w_in.arr, w_in.arr, *order)


def _ffn_out_bx_act(dy, w_out, g, u, name, after=None):
    m = dy.shape[0]
    tr = min(MM_ROWS, m)
    order = [] if after is None else [after]

    def body(dy_ref, w_ref, g_ref, u_ref, *rest):
        dgu_ref = rest[-1]
        da = lax.dot_general(dy_ref[...], w_ref[...], _NT, preferred_element_type=F32)
        gv = g_ref[...]
        sg = _sigmoid(gv)
        dgu_ref[0] = (da * u_ref[...] * (sg * (1.0 + gv * (1.0 - sg)))).astype(BF16)
        dgu_ref[1] = (da * (gv * sg)).astype(BF16)

    tile = pl.BlockSpec((tr, FFN_COLS), lambda i, c: (i, c))
    return pl.pallas_call(
        body, name=name, grid=(m // tr, D_FF // FFN_COLS),
        in_specs=[pl.BlockSpec((tr, D_MODEL), lambda i, c: (i, 0)),
                  _w_spec(w_out, (FFN_COLS, D_MODEL), lambda i, c: (0, c, 0)), tile, tile]
        + [pl.BlockSpec(memory_space=pl.ANY)] * len(order),
        out_specs=pl.BlockSpec((2, tr, FFN_COLS), lambda i, c: (0, i, c)),
        out_shape=jax.ShapeDtypeStruct((2, m, D_FF), BF16),
        compiler_params=_cparams(("arbitrary", "arbitrary")),
    )(dy, w_out.arr, g, u, *order)


def _ffn_in_bwd(dgu, h, w_in, name):
    m, k = h.shape
    nb, bn = w_in.arr.shape[0], w_in.arr.shape[-1]
    per = dgu.shape[2] // bn
    tm = min(MM_ROWS, m)
    ck = next(c for c in (512, 256, 128) if k % c == 0)
    once = pl.Buffered(1)

    def body(dgu_ref, h_ref, w_ref, dh_ref, dw_ref):
        @pl.when(pl.program_id(0) == 0)
        def _():
            dh_ref[...] = jnp.zeros_like(dh_ref)

        def rows(i, carry):
            r = pl.ds(pl.multiple_of(i * tm, tm), tm)
            dh_ref[r, :] += lax.dot_general(dgu_ref[r, :], w_ref[...], _NT, preferred_element_type=F32)
            return carry
        lax.fori_loop(0, m // tm, rows, 0)

        def cols(i, carry):
            c = pl.ds(pl.multiple_of(i * ck, ck), ck)
            dw_ref[c, :] = lax.dot_general(h_ref[:, c], dgu_ref[...], (((0,), (0,)), ((), ())),
                                           preferred_element_type=F32)
            return carry
        lax.fori_loop(0, k // ck, cols, 0)

    return pl.pallas_call(
        body, name=name, grid=(nb,),
        in_specs=[pl.BlockSpec((None, m, bn), lambda n: (n // per, 0, n % per)),
                  pl.BlockSpec((m, k), lambda n: (0, 0), pipeline_mode=once),
                  _w_spec(w_in, (k, bn), lambda n: (n, 0, 0))],
        out_specs=[pl.BlockSpec((m, k), lambda n: (0, 0), pipeline_mode=once),
                   pl.BlockSpec((None, k, bn), lambda n: (n, 0, 0))],
        out_shape=[jax.ShapeDtypeStruct((m, k), F32), jax.ShapeDtypeStruct((nb, k, bn), F32)],
        compiler_params=_cparams(("arbitrary",)),
    )(dgu, h, w_in.arr)


def _loss_head(y, target, name):
    def body(y_ref, t_ref, dy_ref, loss_ref):
        e = y_ref[...] - t_ref[...]
        dy_ref[...] = e * (1.0 / D_MODEL)
        part = jnp.sum(jnp.mean(e * e, axis=-1, keepdims=True), axis=0, keepdims=True) * 0.5
        loss_ref[...] += jnp.broadcast_to(part, loss_ref.shape)
    return _row_call(name, body, [y, target], [], [(D_MODEL, F32)], [(128, F32)])


def _lane_scan(v, reverse):
    s = v.shape[1]
    lane = lax.broadcasted_iota(jnp.int32, v.shape, 1)
    d = 1
    while d < s:
        if reverse:
            v = v + jnp.where(lane < s - d, pltpu.roll(v, s - d, 1), 0.0)
        else:
            v = v + jnp.where(lane >= d, pltpu.roll(v, d, 1), 0.0)
        d *= 2
    return v


def _fox_gate(flt, b_f, name):
    def body(f_ref, b_ref, cum_ref):
        z = f_ref[...] + b_ref[...]
        cum_ref[...] = _lane_scan(-_softplus(-z), reverse=False)
    return pl.pallas_call(body, name=name, out_shape=jax.ShapeDtypeStruct(flt.shape, F32),
                          compiler_params=_cparams())(flt, b_f)


def _fox_gate_bwd(dcum_q, dcum_k, flt, b_f, name):
    def body(dq_ref, dk_ref, f_ref, b_ref, df_ref, db_ref):
        z = f_ref[...] + b_ref[...]
        df = _lane_scan(dq_ref[...] + dk_ref[...], reverse=True) * _sigmoid(-z)
        df_ref[...] = df
        db_ref[...] = jnp.sum(df, axis=1, keepdims=True)
    h = flt.shape[0]
    return pl.pallas_call(body, name=name,
                          out_shape=(jax.ShapeDtypeStruct(flt.shape, F32), jax.ShapeDtypeStruct((h, 1), F32)),
                          compiler_params=_cparams())(dcum_q, dcum_k, flt, b_f)


def _pick_head(block, h):
    lane = lax.broadcasted_iota(jnp.int32, block.shape, 1)
    return jnp.sum(jnp.where(lane == h, block, 0.0), axis=1, keepdims=True)


def _put_head(ref, col, h):
    @pl.when(h == 0)
    def _():
        ref[...] = jnp.zeros_like(ref)
    lane = lax.broadcasted_iota(jnp.int32, ref.shape, 1)
    ref[...] = jnp.where(lane == h, col, ref[...])


_NT = (((1,), (1,)), ((), ()))
_FOX_SCALE = FOX_HEAD_DIM ** -0.5


HEAD_PAIRS = FOX_HEADS // 2
PAIR_W = 2 * FOX_HEAD_DIM


def _low_half(shape):
    return lax.broadcasted_iota(jnp.int32, shape, 1) < FOX_HEAD_DIM


def _fox_attn_fwd(qkv, cum, cum_t, name):
    s = qkv.shape[0]
    t = min(ATT_TILE, s)
    wide = min(ATT_WIDE, s)

    def body(q_ref, k_ref, v_ref, cum_ref, cumt_ref, o_ref, ob_ref, lse_ref):
        i = pl.program_id(0)
        hp = pl.program_id(1)
        lo = _low_half((t, PAIR_W))
        qv = q_ref[...]
        zero = jnp.zeros_like(qv)
        q2 = (jnp.where(lo, qv, zero), jnp.where(lo, zero, qv))
        cum_v = cum_ref[...]
        cq2 = (_pick_head(cum_v, 2 * hp), _pick_head(cum_v, 2 * hp + 1))

        def step(j, carry, masked):
            ks = pl.ds(pl.multiple_of(j * wide, wide), wide)
            kj = k_ref[ks, :]
            vj = v_ref[ks, :]
            out = []
            for e in range(2):
                m, l, acc = carry[e]
                sc = lax.dot_general(q2[e], kj, _NT, preferred_element_type=F32) * _FOX_SCALE
                sc = sc + cq2[e] - cumt_ref[e:e + 1, ks]
                if masked:
                    q_pos = i * t + lax.broadcasted_iota(jnp.int32, (t, wide), 0)
                    k_pos = j * wide + lax.broadcasted_iota(jnp.int32, (t, wide), 1)
                    sc = jnp.where(k_pos <= q_pos, sc, -jnp.inf)
                m_new = jnp.maximum(m, jnp.max(sc, axis=1, keepdims=True))
                alpha = jnp.exp(m - m_new)
                p = jnp.exp(sc - m_new)
                l = alpha * l + jnp.sum(p, axis=1, keepdims=True)
                acc = alpha * acc + jnp.dot(p.astype(BF16), vj, preferred_element_type=F32)
                out.append((m_new, l, acc))
            return tuple(out)

        one = (jnp.full((t, 1), -jnp.inf, F32), jnp.zeros((t, 1), F32), jnp.zeros((t, PAIR_W), F32))
        whole = (i * t) // wide
        carry = lax.fori_loop(0, whole, lambda j, c: step(j, c, False), (one, one))
        (m0, l0, a0), (m1, l1, a1) = step(whole, carry, True)
        o = jnp.where(lo, a0 / l0, a1 / l1)
        o_ref[...] = o
        ob_ref[...] = o.astype(BF16)
        _put_head(lse_ref, m0 + jnp.log(l0), 2 * hp)
        _put_head(lse_ref, m1 + jnp.log(l1), 2 * hp + 1)

    nat_tile = pl.BlockSpec((t, FOX_HEADS), lambda i, hp: (i, 0))
    out_tile = pl.BlockSpec((t, PAIR_W), lambda i, hp: (i, hp))
    return pl.pallas_call(
        body, name=name, grid=(s // t, HEAD_PAIRS),
        in_specs=[pl.BlockSpec((t, PAIR_W), lambda i, hp: (i, hp)),
                  pl.BlockSpec((s, PAIR_W), lambda i, hp: (0, HEAD_PAIRS + hp)),
                  pl.BlockSpec((s, PAIR_W), lambda i, hp: (0, 2 * HEAD_PAIRS + hp)),
                  nat_tile, pl.BlockSpec((None, 2, s), lambda i, hp: (hp, 0, 0))],
        out_specs=[out_tile, out_tile, nat_tile],
        out_shape=[jax.ShapeDtypeStruct((s, D_MODEL), F32), jax.ShapeDtypeStruct((s, D_MODEL), BF16),
                   jax.ShapeDtypeStruct((s, FOX_HEADS), F32)],
        compiler_params=_cparams(("arbitrary", "arbitrary")),
    )(qkv, qkv, qkv, cum, cum_t)


def _fox_delta(do, o, expand, name):
    def body(do_ref, o_ref, e_ref, d_ref):
        prod = do_ref[...] * o_ref[...]
        hi = prod.astype(BF16)
        lo = (prod - hi.astype(F32)).astype(BF16)
        tot = (jnp.dot(hi, e_ref[...], preferred_element_type=F32)
               + jnp.dot(lo, e_ref[...], preferred_element_type=F32))
        d_ref[...] = tot[:, :FOX_HEADS]
    return _row_call(name, body, [do, o], [expand], [(FOX_HEADS, F32)], [])[0]


def _fox_attn_bwd(qkv, do, cum, cum_t, lse_t, delta_t, name):
    s = qkv.shape[0]
    t = min(ATT_TILE, s)
    wide = min(ATT_WIDE, s)
    nq = s // t
    tn_dims = (((0,), (0,)), ((), ()))

    def body(q_ref, k_ref, v_ref, do_ref, cum_ref, cumt_ref, lset_ref, deltat_ref,
             dq_ref, dk_ref, dv_ref, dck_ref, dcq_ref):
        hp = pl.program_id(0)
        j = pl.program_id(1)

        @pl.when(j == 0)
        def _():
            dq_ref[...] = jnp.zeros_like(dq_ref)
            dcq_ref[...] = jnp.zeros_like(dcq_ref)
        dk_ref[...] = jnp.zeros_like(dk_ref)
        dv_ref[...] = jnp.zeros_like(dv_ref)

        lo = _low_half((t, PAIR_W))
        lane = lax.broadcasted_iota(jnp.int32, (t, PAIR_W), 1)
        kv = k_ref[...]
        vv = v_ref[...]
        zero = jnp.zeros_like(kv)
        k2 = (jnp.where(lo, kv, zero), jnp.where(lo, zero, kv))
        v2 = (jnp.where(lo, vv, zero), jnp.where(lo, zero, vv))
        cum_v = cum_ref[...]
        ck2 = (_pick_head(cum_v, 2 * hp), _pick_head(cum_v, 2 * hp + 1))

        def step(i, dck, masked):
            qs = pl.ds(pl.multiple_of(i * wide, wide), wide)
            qi = q_ref[qs, :]
            do_i = do_ref[qs, :].astype(BF16)
            dv_p, dk_p, dq_p = [], [], []
            for e in range(2):
                st = lax.dot_general(k2[e], qi, _NT, preferred_element_type=F32) * _FOX_SCALE
                st = st + cumt_ref[e:e + 1, qs] - ck2[e]
                if masked:
                    k_pos = j * t + lax.broadcasted_iota(jnp.int32, (t, wide), 0)
                    q_pos = i * wide + lax.broadcasted_iota(jnp.int32, (t, wide), 1)
                    st = jnp.where(k_pos <= q_pos, st, -jnp.inf)
                pt = jnp.exp(st - lset_ref[e:e + 1, qs])
                dv_p.append(jnp.dot(pt.astype(BF16), do_i, preferred_element_type=F32))
                dpt = lax.dot_general(v2[e], do_i, _NT, preferred_element_type=F32)
                dst = pt * (dpt - deltat_ref[e:e + 1, qs])
                dsb = dst.astype(BF16)
                dk_p.append(jnp.dot(dsb, qi, preferred_element_type=F32))
                dq_p.append(lax.dot_general(dsb, kv, tn_dims, preferred_element_type=F32))
                dck = dck - jnp.where(lane == e, jnp.sum(dst, axis=1, keepdims=True), 0.0)
                dcq_ref[e:e + 1, qs] += jnp.sum(dst, axis=0, keepdims=True)
            dv_ref[...] += jnp.where(lo, dv_p[0], dv_p[1])
            dk_ref[...] += jnp.where(lo, dk_p[0], dk_p[1])
            dq_ref[qs, :] += jnp.where(_low_half((wide, PAIR_W)), dq_p[0], dq_p[1]) * _FOX_SCALE
            return dck

        first = (j * t) // wide
        dck = step(first, jnp.zeros((t, PAIR_W), F32), True)
        dck = lax.fori_loop(first + 1, s // wide, lambda i, c: step(i, c, False), dck)
        dk_ref[...] = dk_ref[...] * _FOX_SCALE
        dck_ref[...] = dck

    pair_full = lambda part: pl.BlockSpec((s, PAIR_W), lambda hp, j: (0, part * HEAD_PAIRS + hp))
    pair_tile = lambda part: pl.BlockSpec((t, PAIR_W), lambda hp, j: (j, part * HEAD_PAIRS + hp))
    rows = pl.BlockSpec((None, 2, s), lambda hp, j: (hp, 0, 0))
    return pl.pallas_call(
        body, name=name, grid=(HEAD_PAIRS, nq),
        in_specs=[pair_full(0), pair_tile(1), pair_tile(2), pair_full(0),
                  pl.BlockSpec((t, FOX_HEADS), lambda hp, j: (j, 0)), rows, rows, rows],
        out_specs=[pair_full(0), pair_tile(0), pair_tile(0),
                   pl.BlockSpec((None, t, PAIR_W), lambda hp, j: (hp, j, 0)), rows],
        out_shape=[jax.ShapeDtypeStruct((s, D_MODEL), F32)] * 3
        + [jax.ShapeDtypeStruct((HEAD_PAIRS, s, PAIR_W), F32), jax.ShapeDtypeStruct((HEAD_PAIRS, 2, s), F32)],
        compiler_params=_cparams(("arbitrary", "arbitrary")),
    )(qkv, qkv, qkv, do, cum, cum_t, lse_t, delta_t)


def _shift_down(v, d):
    row = lax.broadcasted_iota(jnp.int32, v.shape, 0)
    return jnp.where(row >= d, pltpu.roll(v, d, 0), 0.0)


def _shift_up(v, d):
    s = v.shape[0]
    row = lax.broadcasted_iota(jnp.int32, v.shape, 0)
    return jnp.where(row < s - d, pltpu.roll(v, s - d, 0), 0.0)


def _conv_taps(v, cw_ref, width):
    out = cw_ref[width - 1:width, :] * v
    for k in range(width - 1):
        out = out + cw_ref[k:k + 1, :] * _shift_down(v, width - 1 - k)
    return out


def _conv_taps_bwd(dout, v, cw_ref, dcw_ref, width):
    dv = cw_ref[width - 1:width, :] * dout
    dcw_ref[width - 1:width, :] = _rows_sum(dout * v)
    for k in range(width - 1):
        d = width - 1 - k
        dv = dv + cw_ref[k:k + 1, :] * _shift_up(dout, d)
        dcw_ref[k:k + 1, :] = _rows_sum(dout * _shift_down(v, d))
    return dv


def _col_spec(s, tc, part=0):
    off = part * (D_MODEL // tc)
    return pl.BlockSpec((s, tc), lambda c: (0, c + off))


def _small_spec(rows, tc):
    return pl.BlockSpec((rows, tc), lambda c: (0, c))


def _col_call(name, body, in_arrays, in_specs, out_rows, s, tc):
    return pl.pallas_call(
        body, name=name, grid=(D_MODEL // tc,), in_specs=in_specs,
        out_specs=[pl.BlockSpec((r, tc), lambda c: (0, c)) for r, _ in out_rows],
        out_shape=[jax.ShapeDtypeStruct((r, D_MODEL), dt) for r, dt in out_rows],
        compiler_params=_cparams(("arbitrary",)),
    )(*in_arrays)


def _sconv_fwd(proj, conv_w, name):
    s = proj.shape[0]
    tc = COL_TILE

    def body(b_ref, c_ref, x_ref, cw_ref, y_ref):
        y_ref[...] = (b_ref[...] * _conv_taps(c_ref[...] * x_ref[...], cw_ref, 3)).astype(BF16)

    return _col_call(name, body, [proj, proj, proj, conv_w],
                     [_col_spec(s, tc, 0), _col_spec(s, tc, 1), _col_spec(s, tc, 2), _small_spec(3, tc)],
                     [(s, BF16)], s, tc)[0]


def _sconv_bwd(dy, proj, conv_w, name):
    s = proj.shape[0]
    tc = COL_TILE

    def body(dy_ref, b_ref, c_ref, x_ref, cw_ref, db_ref, dc_ref, dx_ref, dcw_ref):
        w = c_ref[...] * x_ref[...]
        dy_v = dy_ref[...]
        db_ref[...] = (dy_v * _conv_taps(w, cw_ref, 3)).astype(BF16)
        dw = _conv_taps_bwd(dy_v * b_ref[...], w, cw_ref, dcw_ref, 3)
        dc_ref[...] = (dw * x_ref[...]).astype(BF16)
        dx_ref[...] = (dw * c_ref[...]).astype(BF16)

    return _col_call(name, body, [dy, proj, proj, proj, conv_w],
                     [_col_spec(s, tc), _col_spec(s, tc, 0), _col_spec(s, tc, 1), _col_spec(s, tc, 2),
                      _small_spec(3, tc)],
                     [(s, BF16), (s, BF16), (s, BF16), (3, F32)], s, tc)


def _lru_conv(proj, conv_w, conv_b, name):
    s = proj.shape[0]
    tc = COL_TILE

    def body(x_ref, cw_ref, cb_ref, xb_ref, xbb_ref):
        xb = _conv_taps(x_ref[...], cw_ref, 4) + cb_ref[...]
        xb_ref[...] = xb
        xbb_ref[...] = xb.astype(BF16)

    return _col_call(name, body, [proj, conv_w, conv_b],
                     [_col_spec(s, tc, 1), _small_spec(4, tc), _small_spec(1, tc)],
                     [(s, F32), (s, BF16)], s, tc)


def _lru_conv_bwd(dxb1, dxb2, proj, conv_w, name):
    s = proj.shape[0]
    tc = COL_TILE

    def body(d1_ref, d2_ref, x_ref, cw_ref, dx_ref, dcw_ref, dcb_ref):
        dxb = d1_ref[...] + d2_ref[...]
        dcb_ref[...] = _rows_sum(dxb)
        dx_ref[...] = _conv_taps_bwd(dxb, x_ref[...], cw_ref, dcw_ref, 4).astype(BF16)

    return _col_call(name, body, [dxb1, dxb2, proj, conv_w],
                     [_col_spec(s, tc), _col_spec(s, tc), _col_spec(s, tc, 1), _small_spec(4, tc)],
                     [(s, BF16), (4, F32), (1, F32)], s, tc)


_GELU_C = math.sqrt(2.0 / math.pi)


def _gelu_parts(g):
    inner = _GELU_C * (g + 0.044715 * g * g * g)
    th = jnp.tanh(inner)
    val = 0.5 * g * (1.0 + th)
    der = 0.5 * (1.0 + th) + 0.5 * g * (1.0 - th * th) * (_GELU_C * (1.0 + 3.0 * 0.044715 * g * g))
    return val, der


def _lru_gates(pa_ref, px_ref, ba_ref, bx_ref, lam_ref):
    r = _sigmoid(pa_ref[...] + ba_ref[...])
    ig = _sigmoid(px_ref[...] + bx_ref[...])
    sp = _softplus(-lam_ref[...])
    log_a = (-LRU_C) * r * sp
    a = jnp.exp(log_a)
    z = 2.0 * log_a
    one_m_a2 = jnp.where(z > -1e-3, -(z * (1.0 + z * (0.5 + z * (1.0 / 6.0)))), 1.0 - jnp.exp(z))
    return r, ig, sp, a, jnp.sqrt(one_m_a2)


def _lru_scan(pre, xb, proj, b_a, b_x, lam, name):
    s = xb.shape[0]
    tc = COL_TILE

    def body(pa_ref, px_ref, xb_ref, g_ref, ba_ref, bx_ref, lam_ref, y_ref, hs_ref):
        _, ig, _, a, mult = _lru_gates(pa_ref, px_ref, ba_ref, bx_ref, lam_ref)
        b = mult * (ig * xb_ref[...])
        d = 1
        while d < s:
            row = lax.broadcasted_iota(jnp.int32, a.shape, 0)
            keep = row >= d
            b = b + a * jnp.where(keep, pltpu.roll(b, d, 0), 0.0)
            a = a * jnp.where(keep, pltpu.roll(a, d, 0), 1.0)
            d *= 2
        hs_ref[...] = b
        y_ref[...] = (b * _gelu_parts(g_ref[...])[0]).astype(BF16)

    return _col_call(name, body, [pre, pre, xb, proj, b_a, b_x, lam],
                     [_col_spec(s, tc, 0), _col_spec(s, tc, 1), _col_spec(s, tc), _col_spec(s, tc, 0),
                      _small_spec(1, tc), _small_spec(1, tc), _small_spec(1, tc)],
                     [(s, BF16), (s, F32)], s, tc)


def _lru_scan_bwd(dy, pre, xb, proj, hs, b_a, b_x, lam, name):
    s = xb.shape[0]
    tc = COL_TILE

    def body(dy_ref, pa_ref, px_ref, xb_ref, g_ref, hs_ref, ba_ref, bx_ref, lam_ref,
             dg_ref, dpa_ref, dpx_ref, dxb_ref, dba_ref, dbx_ref, dlam_ref):
        r, ig, sp, a, mult = _lru_gates(pa_ref, px_ref, ba_ref, bx_ref, lam_ref)
        gl, gl_der = _gelu_parts(g_ref[...])
        dy_v = dy_ref[...]
        hs_v = hs_ref[...]
        dg_ref[...] = (dy_v * hs_v * gl_der).astype(BF16)
        lam_t = dy_v * gl
        coef = _shift_up(a, 1)
        d = 1
        while d < s:
            row = lax.broadcasted_iota(jnp.int32, coef.shape, 0)
            keep = row < s - d
            lam_t = lam_t + coef * jnp.where(keep, pltpu.roll(lam_t, s - d, 0), 0.0)
            coef = coef * jnp.where(keep, pltpu.roll(coef, s - d, 0), 1.0)
            d *= 2
        xb_v = xb_ref[...]
        da = lam_t * _shift_down(hs_v, 1)
        dmult = lam_t * (ig * xb_v)
        dig = lam_t * mult * xb_v
        dxb_ref[...] = lam_t * mult * ig
        dlog_a = da * a - dmult * (a * a) / mult
        dr = dlog_a * ((-LRU_C) * sp)
        dsp = _rows_sum(dlog_a * ((-LRU_C) * r))
        dlam_ref[...] = -dsp * _sigmoid(-lam_ref[...])
        dpa = dr * r * (1.0 - r)
        dpx = dig * ig * (1.0 - ig)
        dba_ref[...] = _rows_sum(dpa)
        dbx_ref[...] = _rows_sum(dpx)
        dpa_ref[...] = dpa.astype(BF16)
        dpx_ref[...] = dpx.astype(BF16)

    return _col_call(name, body, [dy, pre, pre, xb, proj, hs, b_a, b_x, lam],
                     [_col_spec(s, tc), _col_spec(s, tc, 0), _col_spec(s, tc, 1), _col_spec(s, tc),
                      _col_spec(s, tc, 0), _col_spec(s, tc),
                      _small_spec(1, tc), _small_spec(1, tc), _small_spec(1, tc)],
                     [(s, BF16), (s, BF16), (s, BF16), (s, F32), (1, F32), (1, F32), (1, F32)], s, tc)


def _post_pre(x, y, g_post, gate, coef, g_pre, scale, shift, name):
    def body(x_ref, y_ref, gq_ref, gate_ref, gp_ref, sc_ref, sh_ref, xo_ref, h_ref):
        yv = y_ref[...]
        xo = x_ref[...] + (coef * gate_ref[...]) * ((yv * _rms(yv)) * gq_ref[...])
        xo_ref[...] = xo
        h_ref[...] = ((xo * _rms(xo)) * gp_ref[...] * (1.0 + sc_ref[...]) + sh_ref[...]).astype(BF16)
    return _row_call(name, body, [x, y], [g_post, gate, g_pre, scale, shift], [(D_MODEL, F32), (D_MODEL, BF16)], [])


def _pre_post_bwd(dxo, dh, x, g_pre, scale, y, g_post, gate, coef, name, after=None):
    def body(dxo_ref, dh_ref, x_ref, y_ref, gp_ref, sc_ref, gq_ref, gate_ref,
             dx_ref, dy_ref, dshift_ref, dscale_ref, dgp_ref, dgate_ref, dgq_ref):
        xv = x_ref[...]
        r = _rms(xv)
        xn = xv * r
        dh_v = dh_ref[...]
        one_sc = 1.0 + sc_ref[...]
        dshift_ref[...] += _rows_sum(dh_v)
        dscale_ref[...] += _rows_sum(dh_v * (xn * gp_ref[...]))
        dgp_ref[...] += _rows_sum(dh_v * xn * one_sc)
        dxn = dh_v * (gp_ref[...] * one_sc)
        dx = dxo_ref[...] + r * (dxn - xn * jnp.mean(dxn * xn, axis=-1, keepdims=True))
        dx_ref[...] = dx
        yv = y_ref[...]
        r2 = _rms(yv)
        yn = yv * r2
        dgate_ref[...] += _rows_sum(dx * (yn * gq_ref[...])) * coef
        dz = dx * (coef * gate_ref[...])
        dgq_ref[...] += _rows_sum(dz * yn)
        dyn = dz * gq_ref[...]
        dy_ref[...] = (r2 * (dyn - yn * jnp.mean(dyn * yn, axis=-1, keepdims=True))).astype(BF16)
    return _row_call(name, body, [dxo, dh, x, y], [g_pre, scale, g_post, gate], [(D_MODEL, F32), (D_MODEL, BF16)],
                     [(D_MODEL, F32)] * 5, after=after)


def _ffn_core(h, w_in, w_out, tag, after=None):
    g, u, a = _ffn_in_act(h, w_in, tag + "_in", after=after)
    y = _mm_nn(a, w_out, tag + "_out", tn=512)
    return y, (h, g, u, a)


def _ffn_core_bwd(dy, saved, w_in, w_out, tag, after=None):
    h, g, u, a = saved
    dgu = _ffn_out_bx_act(dy, w_out, g, u, tag + "_out_bx", after=after)
    dw_out = _mm_tn(a, dy, tag + "_out_bw", tn=512)
    dh, dw_in = _ffn_in_bwd(dgu, h, w_in, tag + "_in_b")
    return dh, dw_in, dw_out


def _pair_rows(v):
    return v.T.reshape(HEAD_PAIRS, 2, v.shape[0])


def _fox_fwd(h, p, tag):
    s = h.shape[0]
    qkv = _mm_nn(h, p["w_in"], tag + "_in", tn=768, cols=(0, 3 * D_MODEL), out_dtype=BF16)
    gates = _mm_nn(h, p["w_in"], tag + "_in_f", cols=(3 * D_MODEL, FOX_PAD))
    flt = gates[:, :FOX_HEADS].T
    cum_t = _fox_gate(flt, p["b_f"], tag + "_gate")
    cum = cum_t.T
    cum_t2 = cum_t.reshape(HEAD_PAIRS, 2, s)
    o, ob, lse = _fox_attn_fwd(qkv, cum, cum_t2, tag + "_attn")
    y = _mm_nn(ob, p["w_out"], tag + "_out")
    return y, (qkv, flt, cum, cum_t2, o, ob, lse)


def _fox_bwd(dy, h, saved, p, tag):
    qkv, flt, cum, cum_t2, o, ob, lse = saved
    s = h.shape[0]
    do = _mm_nt(dy, p["w_out"], tag + "_out_bx")
    dw_out = _mm_tn(ob, dy, tag + "_out_bw")
    expand = jnp.pad(jnp.repeat(jnp.eye(FOX_HEADS, dtype=BF16), FOX_HEAD_DIM, axis=0),
                     ((0, 0), (0, PAIR_W - FOX_HEADS)))
    delta = _fox_delta(do, o, expand, tag + "_attn_delta")
    dq, dk, dv, dck, dcq = _fox_attn_bwd(qkv, do, cum, cum_t2, _pair_rows(lse), _pair_rows(delta), tag + "_attn_b")
    dcum_k = dck[:, :, :2].transpose(0, 2, 1).reshape(FOX_HEADS, s)
    dflt, db_f = _fox_gate_bwd(dcq.reshape(FOX_HEADS, s), dcum_k, flt, p["b_f"], tag + "_gate_b")
    dproj = jnp.concatenate(
        [dq, dk, dv, dflt.T, jnp.zeros((s, FOX_PAD - 3 * D_MODEL - FOX_HEADS), F32)], axis=1).astype(BF16)
    dh = _mm_nt(dproj, p["w_in"], tag + "_in_bx", tn=640)
    dw_in = _mm_tn(h, dproj, tag + "_in_bw", tn=640)
    return dh, {"w_in": dw_in, "w_out": dw_out, "b_f": db_f}


def _sconv_mix_fwd(h, p, tag):
    proj = _mm_nn(h, p["w_in"], tag + "_in")
    yb = _sconv_fwd(proj, p["conv_w"], tag + "_conv")
    y = _mm_nn(yb, p["w_out"], tag + "_out")
    return y, (proj, yb)


def _sconv_mix_bwd(dy, h, saved, p, tag):
    proj, yb = saved
    dyb = _mm_nt(dy, p["w_out"], tag + "_out_bx")
    dw_out = _mm_tn(yb, dy, tag + "_out_bw")
    db, dc, dxv, dcw = _sconv_bwd(dyb, proj, p["conv_w"], tag + "_conv_b")
    dproj = jnp.concatenate([db, dc, dxv], axis=1)
    dh = _mm_nt(dproj, p["w_in"], tag + "_in_bx")
    dw_in = _mm_tn(h, dproj, tag + "_in_bw", tn=p["w_in"].arr.shape[-1], blocked_out=True)
    return dh, {"w_in": dw_in, "w_out": dw_out, "conv_w": dcw}


def _lru_mix_fwd(h, p, tag):
    proj = _mm_nn(h, p["w_in"], tag + "_in")
    xb, xbb = _lru_conv(proj, p["conv_w"], p["conv_b"], tag + "_conv")
    pre = _mm_nn(xbb, p["w_ax"], tag + "_gates", tn=D_MODEL)
    yb, hs = _lru_scan(pre, xb, proj, p["b_a"], p["b_x"], p["lam"], tag + "_scan")
    y = _mm_nn(yb, p["w_out"], tag + "_out")
    return y, (proj, xb, xbb, pre, yb, hs)


def _diag_blocks(m):
    return jnp.stack([m[LRU_BLOCK_DIM * n:LRU_BLOCK_DIM * (n + 1), LRU_BLOCK_DIM * n:LRU_BLOCK_DIM * (n + 1)]
                      for n in range(LRU_BLOCKS)])


def _lru_mix_bwd(dy, h, saved, p, tag):
    proj, xb, xbb, pre, yb, hs = saved
    dyb = _mm_nt(dy, p["w_out"], tag + "_out_bx")
    dw_out = _mm_tn(yb, dy, tag + "_out_bw")
    dg, dpa, dpx, dxb1, dba, dbx, dlam = _lru_scan_bwd(dyb, pre, xb, proj, hs, p["b_a"], p["b_x"], p["lam"],
                                                       tag + "_scan_b")
    dpre = jnp.concatenate([dpa, dpx], axis=1)
    dxb2 = _mm_nt(dpre, p["w_ax"], tag + "_gates_bx", tn=D_MODEL)
    dw_ax = _mm_tn(xbb, dpre, tag + "_gates_bw", tn=D_MODEL)
    dx0, dcw, dcb = _lru_conv_bwd(dxb1, dxb2, proj, p["conv_w"], tag + "_conv_b")
    dproj = jnp.concatenate([dg, dx0], axis=1)
    dh = _mm_nt(dproj, p["w_in"], tag + "_in_bx")
    dw_in = _mm_tn(h, dproj, tag + "_in_bw", tn=p["w_in"].arr.shape[-1], blocked_out=True)
    grads = {"w_in": dw_in, "w_out": dw_out, "conv_w": dcw, "conv_b": dcb,
             "w_a": _diag_blocks(dw_ax[:, :D_MODEL]), "w_x": _diag_blocks(dw_ax[:, D_MODEL:]),
             "b_a": dba, "b_x": dbx, "lam": dlam}
    return dh, grads


_MIXERS = ((_fox_fwd, _fox_bwd), (_sconv_mix_fwd, _sconv_mix_bwd), (_lru_mix_fwd, _lru_mix_bwd))


def _local_step(x, target, mod, layer_params, on_grads=None, on_mid=None, first_after=None):
    layers = []
    tape = []
    handed = None
    for i in range(DEPTH):
        row = lambda v: v[None, :]
        m = lambda sub, what: mod[i, sub, what][None, :]
        if handed is None:
            lp = dict(layer_params(i, 0, x))
            h0 = _pre_norm(x, row(lp["norm_pre"][0]), m(0, 1), m(0, 0), f"l{i}_ffn0_pre", after=first_after)
        else:
            lp, h0 = handed
        layers.append(lp)
        gp, gq = lp["norm_pre"], lp["norm_post"]
        y0, sv0 = _ffn_core(h0, lp["ffn_in"][0], lp["ffn_out"][0], f"l{i}_ffn0")
        x1, h1 = _post_pre(x, y0, row(gq[0]), m(0, 2), 0.5, row(gp[1]), m(1, 1), m(1, 0), f"l{i}_ffn0_post")
        lp.update(layer_params(i, 1, x1))
        y1, svm = _MIXERS[i % 3][0](h1, lp["mixer"], f"l{i}_mix")
        x2, h2 = _post_pre(x1, y1, row(gq[1]), m(1, 2), 1.0, row(gp[2]), m(2, 1), m(2, 0), f"l{i}_mix_post")
        second = layer_params(i, 2, x2)
        lp["ffn_in"] = lp["ffn_in"] + second["ffn_in"]
        lp["ffn_out"] = lp["ffn_out"] + second["ffn_out"]
        y2, sv2 = _ffn_core(h2, lp["ffn_in"][1], lp["ffn_out"][1], f"l{i}_ffn1", after=second.get("after"))
        if i + 1 < DEPTH:
            nxt = dict(layer_params(i + 1, 0, y2))
            x3, h_next = _post_pre(x2, y2, row(gq[2]), m(2, 2), 0.5, row(nxt["norm_pre"][0]),
                                   mod[i + 1, 0, 1][None, :], mod[i + 1, 0, 0][None, :], f"l{i}_ffn1_post")
            handed = (nxt, h_next)
        else:
            x3 = _post_norm(x2, y2, row(gq[2]), m(2, 2), 0.5, f"l{i}_ffn1_post")
        tape.append((x, y0, sv0, x1, h1, y1, svm, x2, y2, sv2))
        x = x3
    dx, loss_row = _loss_head(x, target, "loss_head")

    layer_grads = [None] * DEPTH
    dmod = [None] * DEPTH
    after = None
    handed_b = None
    for i in reversed(range(DEPTH)):
        lp = layers[i]
        row = lambda v: v[None, :]
        m = lambda sub, what: mod[i, sub, what][None, :]
        gp, gq = lp["norm_pre"], lp["norm_post"]
        x0, y0, sv0, x1, h1, y1, svm, x2, y2, sv2 = tape[i]
        if handed_b is None:
            dy2, dgate2, dgq2 = _post_norm_bwd(dx, y2, row(gq[2]), m(2, 2), 0.5, f"l{i}_ffn1_post_b", after=after)
            after = None
        else:
            dy2, dgate2, dgq2 = handed_b
        dh2, dw_in1, dw_out1 = _ffn_core_bwd(dy2, sv2, lp["ffn_in"][1], lp["ffn_out"][1], f"l{i}_ffn1", after=after)
        after = on_mid(i, dh2) if on_mid is not None else None
        dx, dy1, dshift2, dscale2, dgp2, dgate1, dgq1 = _pre_post_bwd(
            dx, dh2, x2, row(gp[2]), m(2, 1), y1, row(gq[1]), m(1, 2), 1.0, f"l{i}_mix_post_b", after=after)
        dh1, mg = _MIXERS[i % 3][1](dy1, h1, svm, lp["mixer"], f"l{i}_mix")
        dx, dy0, dshift1, dscale1, dgp1, dgate0, dgq0 = _pre_post_bwd(
            dx, dh1, x1, row(gp[1]), m(1, 1), y0, row(gq[0]), m(0, 2), 0.5, f"l{i}_ffn0_post_b")
        dh0, dw_in0, dw_out0 = _ffn_core_bwd(dy0, sv0, lp["ffn_in"][0], lp["ffn_out"][0], f"l{i}_ffn0")
        if i > 0:
            dx, dy_prev, dshift0, dscale0, dgp0, dgate_prev, dgq_prev = _pre_post_bwd(
                dx, dh0, x0, row(gp[0]), m(0, 1), tape[i - 1][8], row(layers[i - 1]["norm_post"][2]),
                mod[i - 1, 2, 2][None, :], 0.5, f"l{i}_ffn0_pre_b")
            handed_b = (dy_prev, dgate_prev, dgq_prev)
        else:
            dx, dshift0, dscale0, dgp0 = _pre_norm_bwd(dx, dh0, x0, row(gp[0]), m(0, 1), f"l{i}_ffn0_pre_b")
        dmod[i] = jnp.concatenate([dshift0, dscale0, dgate0, dshift1, dscale1, dgate1, dshift2, dscale2, dgate2],
                                  axis=0).reshape(N_SUB, 3, D_MODEL)
        layer_grads[i] = {"ffn_in": (dw_in0, dw_in1), "ffn_out": (dw_out0, dw_out1),
                          "norm_pre": jnp.concatenate([dgp0, dgp1, dgp2], axis=0),
                          "norm_post": jnp.concatenate([dgq0, dgq1, dgq2], axis=0), "mixer": mg}
        if on_grads is not None:
            after = on_grads(i, layer_grads[i], dx)
    return loss_row, dx, jnp.stack(dmod), layer_grads


COND_ROWS = 16
COND_PAD = 128


def _cond_fwd(c_pad, w_cond, b_shard, name):
    nl, d, n = w_cond.shape
    tn = 768

    def body(c_ref, w_ref, b_ref, o_ref):
        cv = c_ref[...]
        act = (cv * _sigmoid(cv)).astype(BF16)
        o_ref[...] = jnp.dot(act, w_ref[...].astype(BF16), preferred_element_type=F32) + b_ref[...]

    return pl.pallas_call(
        body, name=name, grid=(nl, n // tn),
        in_specs=[pl.BlockSpec((COND_ROWS, d), lambda i, j: (0, 0)),
                  pl.BlockSpec((None, d, tn), lambda i, j: (i, 0, j)),
                  pl.BlockSpec((None, 1, tn), lambda i, j: (i, 0, j))],
        out_specs=pl.BlockSpec((None, COND_ROWS, tn), lambda i, j: (i, 0, j)),
        out_shape=jax.ShapeDtypeStruct((nl, COND_ROWS, n), F32),
        compiler_params=_cparams(("arbitrary", "arbitrary")),
    )(c_pad, w_cond, b_shard)


def _adam_math(w, g, m, v):
    nm = ADAM_B1 * m + (1.0 - ADAM_B1) * g
    nv = ADAM_B2 * v + (1.0 - ADAM_B2) * (g * g)
    m_hat = nm / (1.0 - ADAM_B1 ** ADAM_STEP)
    v_hat = nv / (1.0 - ADAM_B2 ** ADAM_STEP)
    delta = (-ADAM_LR) * (m_hat / (jnp.sqrt(v_hat) + ADAM_EPS) + ADAM_WD * w)
    return delta, nm, nv


def _cond_bwd_adamw(c_t, dmod_s, w, m, v, name):
    nl, d, n = w.shape
    tn = 384
    blk = pl.BlockSpec((None, d, tn), lambda i, j: (i, 0, j))

    def body(c_ref, dm_ref, w_ref, m_ref, v_ref, g_ref, d_ref, nm_ref, nv_ref):
        cv = c_ref[...]
        g = jnp.dot((cv * _sigmoid(cv)).astype(BF16), dm_ref[...], preferred_element_type=F32)
        g_ref[...] = g
        d_ref[...], nm_ref[...], nv_ref[...] = _adam_math(w_ref[...], g, m_ref[...], v_ref[...])

    return pl.pallas_call(
        body, name=name, grid=(nl, n // tn),
        in_specs=[pl.BlockSpec((d, COND_PAD), lambda i, j: (0, 0)),
                  pl.BlockSpec((None, COND_PAD, tn), lambda i, j: (i, 0, j)), blk, blk, blk],
        out_specs=[blk] * 4, out_shape=[jax.ShapeDtypeStruct(w.shape, F32)] * 4,
        compiler_params=_cparams(("arbitrary", "arbitrary")),
    )(c_t, dmod_s, w, m, v)


def _adamw(w, g, m, v, name):
    rows, cols = w.shape
    tr = next(t for t in (256, 176, 128, 64, 32, 16, 8) if rows % t == 0)
    blk = pl.BlockSpec((tr, cols), lambda i: (i, 0))

    def body(w_ref, g_ref, m_ref, v_ref, d_ref, nm_ref, nv_ref):
        d_ref[...], nm_ref[...], nv_ref[...] = _adam_math(w_ref[...], g_ref[...], m_ref[...], v_ref[...])

    return pl.pallas_call(
        body, name=name, grid=(rows // tr,), in_specs=[blk] * 4, out_specs=[blk] * 3,
        out_shape=[jax.ShapeDtypeStruct(w.shape, F32)] * 3, compiler_params=_cparams(("arbitrary",)),
    )(w, g, m, v)


def _adamw_rows(w, g, m, v, outs, row0, nrows, name):
    cols = w.shape[1]
    tr = next(t for t in (512, 256, 128, 64, 32, 16, 8) if nrows % t == 0 and row0 % t == 0)
    blk = pl.BlockSpec((tr, cols), lambda i: (i + row0 // tr, 0))
    anywhere = pl.BlockSpec(memory_space=pl.ANY)

    def body(w_ref, g_ref, m_ref, v_ref, d_in, nm_in, nv_in, d_ref, nm_ref, nv_ref, g_out):
        d_ref[...], nm_ref[...], nv_ref[...] = _adam_math(w_ref[...], g_ref[...], m_ref[...], v_ref[...])

    return pl.pallas_call(
        body, name=name, grid=(nrows // tr,), in_specs=[blk] * 4 + [anywhere] * 3,
        out_specs=[blk] * 3 + [anywhere], out_shape=[jax.ShapeDtypeStruct(w.shape, F32)] * 4,
        input_output_aliases={4: 0, 5: 1, 6: 2, 1: 3}, compiler_params=_cparams(("arbitrary",)),
    )(w, g, m, v, *outs)


_MESH = pl.DeviceIdType.MESH
_ANY = pl.BlockSpec(memory_space=pl.ANY)


def _place():
    return lax.axis_index("x"), lax.axis_index("y"), lax.axis_index("c")


def _other_chips(x, y):
    return [(1 - x, y), (x, 1 - y), (1 - x, 1 - y)]


def _allgather8(block, name):
    m_per, n = block.shape

    def body(x_ref, out_ref, send_sems, recv_sems, local_sem):
        x, y, c = _place()
        me, sibling = (x, y, c), (x, y, 1 - c)
        chips = _other_chips(x, y)

        def rows(px, py, pc):
            return out_ref.at[pl.ds((4 * px + 2 * py + pc) * m_per, m_per), :]

        def copy(k, blk, to, src=None):
            return pltpu.make_async_remote_copy(
                src_ref=rows(*blk) if src is None else src, dst_ref=rows(*blk),
                send_sem=send_sems.at[k], recv_sem=recv_sems.at[k], device_id=to, device_id_type=_MESH)

        mine = pltpu.make_async_copy(x_ref, rows(*me), local_sem)
        mine.start()
        first = [copy(0, me, sibling, src=x_ref)]
        first += [copy(1 + j, me, (*chip, c), src=x_ref) for j, chip in enumerate(chips)]
        for cp in first:
            cp.start()
        passed = [copy(4 + j, (*chip, c), sibling) for j, chip in enumerate(chips)]
        for j, chip in enumerate(chips):
            copy(1 + j, (*chip, c), me).wait_recv()
            passed[j].start()
        copy(0, sibling, me).wait_recv()
        for j, chip in enumerate(chips):
            copy(4 + j, (*chip, 1 - c), me).wait_recv()
        for cp in first + passed:
            cp.wait_send()
        mine.wait()

    return pl.pallas_call(
        body, name=name, out_shape=jax.ShapeDtypeStruct((N_DEV * m_per, n), block.dtype),
        in_specs=[pl.BlockSpec(memory_space=pltpu.VMEM)], out_specs=pl.BlockSpec(memory_space=pltpu.VMEM),
        scratch_shapes=[pltpu.SemaphoreType.DMA((7,)), pltpu.SemaphoreType.DMA((7,)), pltpu.SemaphoreType.DMA],
        compiler_params=_cparams(),
    )(block)


def _split_axis(shape):
    return next(a for a, n in enumerate(shape) if n > 1)


_HBM = pl.BlockSpec(memory_space=pltpu.HBM)
_SEM = pl.BlockSpec(memory_space=pltpu.SEMAPHORE)
_SPLIT_COPY = pltpu.CompilerParams(has_side_effects=pltpu.SideEffectType.DATAFLOW_SIDE_EFFECTING)
_TOKEN = jax.ShapeDtypeStruct((8, 128), F32)


def _in_hbm(arrays):
    return [pltpu.with_memory_space_constraint(a, pltpu.HBM) for a in arrays]


class _Gathered(NamedTuple):
    shard_shape: tuple
    chip_axis: int

    @property
    def shape(self):
        return self.shard_shape[:self.chip_axis] + (N_CHIPS,) + self.shard_shape[self.chip_axis:]

    def half(self, ref, chip, pc):
        cut = _split_axis(self.shard_shape)
        n = self.shard_shape[cut] // 2
        idx = [slice(None)] * len(self.shard_shape)
        idx[cut] = pl.ds(pc * n, n)
        idx.insert(self.chip_axis, chip)
        return ref.at[tuple(idx)]


def _own_block_placed(shard, layout, chip):
    return lax.dynamic_update_slice_in_dim(lax.empty(layout.shape, shard.dtype),
                                           jnp.expand_dims(shard, layout.chip_axis), chip, axis=layout.chip_axis)


def _gather_copies(lands, layouts, send_sems, recv_sems):
    x, y, c = _place()
    out = []
    for t, (land, lay) in enumerate(zip(lands, layouts)):
        for j, (px, py) in enumerate(_other_chips(x, y)):
            def copy(chip, t=t, j=j, px=px, py=py, land=land, lay=lay):
                return pltpu.make_async_remote_copy(
                    src_ref=lay.half(land, chip, c), dst_ref=lay.half(land, chip, c),
                    send_sem=send_sems.at[3 * t + j], recv_sem=recv_sems.at[3 * t + j],
                    device_id=(px, py, c), device_id_type=_MESH)
            out.append((copy(2 * x + y), copy(2 * px + py)))
    return out


def _gather_start(lands, layouts, after, name):
    nt = len(lands)
    order = [] if after is None else [after]

    def body(*refs):
        land_refs = refs[:nt]
        send_sems, recv_sems = refs[nt + len(order):nt + len(order) + 2]
        token = refs[-1]
        for send, _ in _gather_copies(land_refs, layouts, send_sems, recv_sems):
            send.start()
        token[...] = jnp.zeros_like(token)

    out = pl.pallas_call(
        body, name=name,
        out_shape=(pltpu.SemaphoreType.DMA((3 * nt,)), pltpu.SemaphoreType.DMA((3 * nt,)),
                   *[pltpu.HBM(a.shape, a.dtype) for a in lands], _TOKEN),
        in_specs=[_HBM] * nt + [_ANY] * len(order),
        out_specs=(_SEM, _SEM, *[_HBM] * nt, pl.BlockSpec(memory_space=pltpu.VMEM)),
        input_output_aliases={t: 2 + t for t in range(nt)}, compiler_params=_SPLIT_COPY,
    )(*_in_hbm(lands), *order)
    return out[0], out[1], list(out[2:2 + nt]), out[-1]


def _gather_wait(send_sems, recv_sems, lands, layouts, after, name):
    nt = len(lands)

    def body(*refs):
        land_refs = refs[:nt]
        sems = refs[nt:nt + 2]
        for send, arrival in _gather_copies(land_refs, layouts, *sems):
            send.wait_send()
            arrival.wait_recv()

    return list(pl.pallas_call(
        body, name=name, out_shape=tuple(pltpu.HBM(a.shape, a.dtype) for a in lands),
        in_specs=[_HBM] * nt + [_SEM, _SEM, _ANY], out_specs=tuple([_HBM] * nt),
        input_output_aliases={t: t for t in range(nt)}, compiler_params=_SPLIT_COPY,
    )(*lands, send_sems, recv_sems, after))


def _gather_forward(lands, layouts, name):
    nt = len(lands)

    def body(*refs):
        outs = refs[nt:2 * nt]
        send_sems, recv_sems = refs[2 * nt:]
        x, y, c = _place()
        sends, arrivals = [], []
        for t, lay in enumerate(layouts):
            for j, (px, py) in enumerate(_other_chips(x, y)):
                for pc, group in ((c, sends), (1 - c, arrivals)):
                    part = lay.half(outs[t], 2 * px + py, pc)
                    group.append(pltpu.make_async_remote_copy(
                        src_ref=part, dst_ref=part, send_sem=send_sems.at[3 * t + j], recv_sem=recv_sems.at[3 * t + j],
                        device_id=(x, y, 1 - c), device_id_type=_MESH))
        for cp in sends:
            cp.start()
        for cp in arrivals:
            cp.wait_recv()
        for cp in sends:
            cp.wait_send()

    return list(pl.pallas_call(
        body, name=name, out_shape=[jax.ShapeDtypeStruct(a.shape, a.dtype) for a in lands],
        in_specs=[_ANY] * nt, out_specs=[_ANY] * nt, input_output_aliases={t: t for t in range(nt)},
        scratch_shapes=[pltpu.SemaphoreType.DMA((3 * nt,)), pltpu.SemaphoreType.DMA((3 * nt,))],
        compiler_params=_cparams(),
    )(*lands))


def _forward_copies(lands, layouts, send_sems, recv_sems):
    x, y, c = _place()
    out = []
    for t, (land, lay) in enumerate(zip(lands, layouts)):
        for j, (px, py) in enumerate(_other_chips(x, y)):
            def copy(pc, t=t, j=j, px=px, py=py, land=land, lay=lay):
                part = lay.half(land, 2 * px + py, pc)
                return pltpu.make_async_remote_copy(
                    src_ref=part, dst_ref=part, send_sem=send_sems.at[3 * t + j], recv_sem=recv_sems.at[3 * t + j],
                    device_id=(x, y, 1 - c), device_id_type=_MESH)
            out.append((copy(c), copy(1 - c)))
    return out


def _gather_forward_start(lands, layouts, name):
    nt = len(lands)

    def body(*refs):
        for send, _ in _forward_copies(refs[:nt], layouts, refs[nt], refs[nt + 1]):
            send.start()
        refs[-1][...] = jnp.zeros_like(refs[-1])

    out = pl.pallas_call(
        body, name=name,
        out_shape=(pltpu.SemaphoreType.DMA((3 * nt,)), pltpu.SemaphoreType.DMA((3 * nt,)),
                   *[pltpu.HBM(a.shape, a.dtype) for a in lands], _TOKEN),
        in_specs=[_HBM] * nt, out_specs=(_SEM, _SEM, *[_HBM] * nt, pl.BlockSpec(memory_space=pltpu.VMEM)),
        input_output_aliases={t: 2 + t for t in range(nt)}, compiler_params=_SPLIT_COPY,
    )(*_in_hbm(lands))
    return out[0], out[1], list(out[2:2 + nt]), out[-1]


def _gather_forward_wait(send_sems, recv_sems, lands, layouts, after, name):
    nt = len(lands)

    def body(*refs):
        for send, arrival in _forward_copies(refs[:nt], layouts, refs[nt], refs[nt + 1]):
            send.wait_send()
            arrival.wait_recv()

    return list(pl.pallas_call(
        body, name=name, out_shape=tuple(pltpu.HBM(a.shape, a.dtype) for a in lands),
        in_specs=[_HBM] * nt + [_SEM, _SEM, _ANY], out_specs=tuple([_HBM] * nt),
        input_output_aliases={t: t for t in range(nt)}, compiler_params=_SPLIT_COPY,
    )(*lands, send_sems, recv_sems, after))


def _pair_copies(grads, lands, send_sems, recv_sems):
    x, y, c = _place()
    out = []
    for t, (g, land) in enumerate(zip(grads, lands)):
        h = g.shape[1] // 2
        out.append(pltpu.make_async_remote_copy(
            src_ref=g.at[:, pl.ds((1 - c) * h, h), :], dst_ref=land, send_sem=send_sems.at[t],
            recv_sem=recv_sems.at[t], device_id=(x, y, 1 - c), device_id_type=_MESH))
    return out


def _pair_start(grads, after, name):
    nt = len(grads)
    lands = [lax.empty((N_CHIPS, g.shape[1] // 2, g.shape[2]), g.dtype) for g in grads]
    order = [] if after is None else [after]

    def body(*refs):
        send_sems, recv_sems = refs[2 * nt + len(order):2 * nt + len(order) + 2]
        token = refs[-1]
        for cp in _pair_copies(refs[:nt], refs[nt:2 * nt], send_sems, recv_sems):
            cp.start()
        token[...] = jnp.zeros_like(token)

    out = pl.pallas_call(
        body, name=name,
        out_shape=(pltpu.SemaphoreType.DMA((nt,)), pltpu.SemaphoreType.DMA((nt,)),
                   *[pltpu.HBM(a.shape, a.dtype) for a in grads + lands], _TOKEN),
        in_specs=[_HBM] * (2 * nt) + [_ANY] * len(order),
        out_specs=(_SEM, _SEM, *[_HBM] * (2 * nt), pl.BlockSpec(memory_space=pltpu.VMEM)),
        input_output_aliases={t: 2 + t for t in range(2 * nt)}, compiler_params=_SPLIT_COPY,
    )(*_in_hbm(grads + lands), *order)
    return out[0], out[1], list(out[2:2 + nt]), list(out[2 + nt:2 + 2 * nt]), out[-1]


def _pair_wait(send_sems, recv_sems, grads, lands, after, name):
    nt = len(grads)

    def body(*refs):
        for cp in _pair_copies(refs[:nt], refs[nt:2 * nt], *refs[2 * nt:2 * nt + 2]):
            cp.wait_send()
            cp.wait_recv()

    out = pl.pallas_call(
        body, name=name, out_shape=tuple(pltpu.HBM(a.shape, a.dtype) for a in grads + lands),
        in_specs=[_HBM] * (2 * nt) + [_SEM, _SEM, _ANY], out_specs=tuple([_HBM] * (2 * nt)),
        input_output_aliases={t: t for t in range(2 * nt)}, compiler_params=_SPLIT_COPY,
    )(*grads, *lands, send_sems, recv_sems, after)
    return list(out[:nt]), list(out[nt:])


def _pair_sum(own, recv, c_idx, name):
    _, h, cols = recv.shape

    def body(c_ref, own_ref, recv_ref, o_ref):
        o_ref[...] = (own_ref[...] + recv_ref[...]).astype(BF16)

    return pl.pallas_call(
        body, name=name,
        grid_spec=pltpu.PrefetchScalarGridSpec(
            num_scalar_prefetch=1, grid=(N_CHIPS,),
            in_specs=[pl.BlockSpec((None, h, cols), lambda k, c_ref: (k, c_ref[0], 0)),
                      pl.BlockSpec((None, h, cols), lambda k, c_ref: (k, 0, 0))],
            out_specs=pl.BlockSpec((None, h, cols), lambda k, c_ref: (k, 0, 0))),
        out_shape=jax.ShapeDtypeStruct(recv.shape, BF16), compiler_params=_cparams(("arbitrary",)),
    )(c_idx, own, recv)


def _chip_copies(parts, lands, send_sems, recv_sems):
    x, y, c = _place()
    out = []
    for t, (part, land) in enumerate(zip(parts, lands)):
        for j, (px, py) in enumerate(_other_chips(x, y)):
            out.append(pltpu.make_async_remote_copy(
                src_ref=part.at[2 * px + py], dst_ref=land.at[j], send_sem=send_sems.at[3 * t + j],
                recv_sem=recv_sems.at[3 * t + j], device_id=(px, py, c), device_id_type=_MESH))
    return out


def _chip_send_start(parts, after, name):
    nt = len(parts)
    lands = [lax.empty((N_CHIPS - 1,) + p.shape[1:], p.dtype) for p in parts]
    order = [] if after is None else [after]

    def body(*refs):
        send_sems, recv_sems = refs[2 * nt + len(order):2 * nt + len(order) + 2]
        token = refs[-1]
        for cp in _chip_copies(refs[:nt], refs[nt:2 * nt], send_sems, recv_sems):
            cp.start()
        token[...] = jnp.zeros_like(token)

    out = pl.pallas_call(
        body, name=name,
        out_shape=(pltpu.SemaphoreType.DMA((3 * nt,)), pltpu.SemaphoreType.DMA((3 * nt,)),
                   *[pltpu.HBM(a.shape, a.dtype) for a in parts + lands], _TOKEN),
        in_specs=[_HBM] * (2 * nt) + [_ANY] * len(order),
        out_specs=(_SEM, _SEM, *[_HBM] * (2 * nt), pl.BlockSpec(memory_space=pltpu.VMEM)),
        input_output_aliases={t: 2 + t for t in range(2 * nt)}, compiler_params=_SPLIT_COPY,
    )(*_in_hbm(parts + lands), *order)
    return out[0], out[1], list(out[2:2 + nt]), list(out[2 + nt:2 + 2 * nt]), out[-1]


def _chip_send_wait(send_sems, recv_sems, parts, lands, after, name):
    nt = len(parts)

    def body(*refs):
        for cp in _chip_copies(refs[:nt], refs[nt:2 * nt], *refs[2 * nt:2 * nt + 2]):
            cp.wait_send()
            cp.wait_recv()

    out = pl.pallas_call(
        body, name=name, out_shape=tuple(pltpu.HBM(a.shape, a.dtype) for a in parts + lands),
        in_specs=[_HBM] * (2 * nt) + [_SEM, _SEM, _ANY], out_specs=tuple([_HBM] * (2 * nt)),
        input_output_aliases={t: t for t in range(2 * nt)}, compiler_params=_SPLIT_COPY,
    )(*parts, *lands, send_sems, recv_sems, after)
    return list(out[:nt]), list(out[nt:])


def _chip_sum(part, arrived, into, lead, place_idx, name):
    _, h, cols = part.shape

    def body(idx_ref, own_ref, arr_ref, into_ref, o_ref):
        acc = own_ref[...].astype(F32)
        for k in range(N_CHIPS - 1):
            acc = acc + arr_ref[k].astype(F32)
        o_ref[...] = acc

    return pl.pallas_call(
        body, name=name,
        grid_spec=pltpu.PrefetchScalarGridSpec(
            num_scalar_prefetch=1, grid=(1,),
            in_specs=[pl.BlockSpec((None, h, cols), lambda g, idx: (idx[1], 0, 0)),
                      pl.BlockSpec((N_CHIPS - 1, h, cols), lambda g, idx: (0, 0, 0)), _ANY],
            out_specs=pl.BlockSpec((None,) * len(lead) + (h, cols), lambda g, idx: (*lead, idx[0], 0))),
        out_shape=jax.ShapeDtypeStruct(into.shape, F32), input_output_aliases={3: 0},
        compiler_params=_cparams(("arbitrary",)),
    )(place_idx, part, arrived, into)


def _pair_gather(bufs, homes, name):
    nt, nb = len(homes), len(bufs)

    def body(*refs):
        outs = refs[nb:2 * nb]
        send_sems, recv_sems = refs[2 * nb:]
        x, y, c = _place()

        def home(t, pc):
            o, lead, rows = homes[t]
            return outs[o].at[(*lead, pl.ds(pc * (rows // 2), rows // 2), slice(None))]

        def copy(t, pc):
            return pltpu.make_async_remote_copy(src_ref=home(t, pc), dst_ref=home(t, pc), send_sem=send_sems.at[t],
                                                recv_sem=recv_sems.at[t], device_id=(x, y, 1 - c), device_id_type=_MESH)

        sends = [copy(t, c) for t in range(nt)]
        for cp in sends:
            cp.start()
        for t in range(nt):
            copy(t, 1 - c).wait_recv()
        for cp in sends:
            cp.wait_send()

    return pl.pallas_call(
        body, name=name, out_shape=[jax.ShapeDtypeStruct(b.shape, b.dtype) for b in bufs],
        in_specs=[_ANY] * nb, out_specs=[_ANY] * nb, input_output_aliases={o: o for o in range(nb)},
        scratch_shapes=[pltpu.SemaphoreType.DMA((nt,)), pltpu.SemaphoreType.DMA((nt,))],
        compiler_params=_cparams(),
    )(*bufs)


def _sum_devices(g, after, name):
    def body(g_ref, after_ref, o_ref):
        acc = g_ref[0:1, :]
        for d in range(1, N_DEV):
            acc = acc + g_ref[d:d + 1, :]
        o_ref[...] = acc
    vmem = pl.BlockSpec(memory_space=pltpu.VMEM)
    return pl.pallas_call(body, name=name, out_shape=jax.ShapeDtypeStruct((1, g.shape[1]), F32),
                          in_specs=[vmem, _ANY], out_specs=vmem, compiler_params=_cparams())(g, after)


_WEIGHTS = ("w_cond", "b_cond", "norm_pre", "norm_post", "w_ffn_in", "w_ffn_out", "fox_w_in", "fox_b_f",
            "fox_w_out", "sconv_w_in", "sconv_conv_w", "sconv_w_out", "lru_w_in", "lru_conv_w", "lru_conv_b",
            "lru_w_a", "lru_b_a", "lru_w_x", "lru_b_x", "lru_lambda", "lru_w_out")
_BIG = (("w_ffn_in", False), ("w_ffn_out", True), ("fox_w_in", False), ("fox_w_out", True),
        ("sconv_w_in", False), ("sconv_w_out", True), ("lru_w_in", False), ("lru_w_out", True))
_SMALL = tuple(n for n in _WEIGHTS if n != "w_cond" and n not in dict(_BIG))
_COL_SHARDED_SMALL = ("norm_pre", "norm_post", "sconv_conv_w", "lru_conv_w", "lru_conv_b", "lru_lambda")


def _pack_rows(parts, rows=8):
    flat = jnp.concatenate([p.reshape(-1) for p in parts])
    width = -(-flat.size // (rows * 128)) * 128
    return jnp.pad(flat, (0, rows * width - flat.size)).reshape(rows, width)


def _unpack(flat, shapes):
    out, off = [], 0
    for shp in shapes:
        n = math.prod(shp)
        out.append(flat[off:off + n].reshape(shp))
        off += n
    return out


def _join_chips(g):
    g = jnp.moveaxis(g, 0, -2)
    return g.reshape(g.shape[:-2] + (g.shape[-2] * g.shape[-1],))


def _my_columns(full, chip):
    n = full.shape[-1] // N_CHIPS
    return lax.dynamic_slice_in_dim(full, chip * n, n, axis=full.ndim - 1)


def _block_diag(w):
    eye = jnp.eye(LRU_BLOCKS, dtype=w.dtype)
    return jnp.einsum("nij,nm->nimj", w, eye).reshape(D_MODEL, D_MODEL)


def _step(x, c, target, wts, mom, var):
    ix, iy, ic = _place()
    chip = 2 * ix + iy
    dev = 2 * chip + ic
    n_cond = wts["w_cond"].shape[2]

    small_shapes = [(D_MODEL,)] + [wts[n].shape for n in _COL_SHARDED_SMALL]
    g1 = _allgather8(_pack_rows([c[0]] + [wts[n] for n in _COL_SHARDED_SMALL]), "gather_small").reshape(N_DEV, -1)
    c_all = g1[:, :D_MODEL]
    per_chip = [jnp.stack(col) for col in zip(*[_unpack(g1[2 * k], small_shapes) for k in range(N_CHIPS)])]
    small_full = {n: _join_chips(v) for n, v in zip(_COL_SHARDED_SMALL, per_chip[1:])}

    c_pad = jnp.pad(c_all, ((0, COND_ROWS - N_DEV), (0, 0)))
    b_shard = _my_columns(wts["b_cond"], chip)[:, None, :]
    mod_part = _cond_fwd(c_pad, wts["w_cond"], b_shard, "cond_fwd")
    g2 = _allgather8(mod_part[:, :N_DEV].transpose(1, 0, 2).reshape(N_DEV, DEPTH * n_cond), "gather_mod")
    g2 = g2.reshape(N_DEV, N_DEV, DEPTH, n_cond)[0::2]
    mod = _join_chips(lax.dynamic_index_in_dim(g2, dev, axis=1, keepdims=False)).reshape(DEPTH, N_SUB, 3, D_MODEL)

    mixer_names = [("fox_w_in", "fox_w_out"), ("sconv_w_in", "sconv_w_out"), ("lru_w_in", "lru_w_out")]

    def shards_of(i, sub):
        if sub == 1:
            return [wts[n][i // 3] for n in mixer_names[i % 3]]
        return [wts["w_ffn_in"][i, sub // 2], wts["w_ffn_out"][i, sub // 2]]

    chunks = [[(0, sub)] for sub in range(N_SUB)] + [[(i, sub) for sub in range(N_SUB)] for i in range(1, DEPTH)]
    in_flight, chunk_of, token = [], {}, mod
    for k, members in enumerate(chunks):
        shards = [s for i, sub in members for s in shards_of(i, sub)]
        layouts = [_Gathered(s.shape, 0) for s in shards]
        if k:
            shards = [s + token[0, 0] for s in shards]
        lands = [_own_block_placed(s.astype(BF16), lay, chip) for s, lay in zip(shards, layouts)]
        send_sems, recv_sems, lands, token = _gather_start(lands, layouts, token, f"gather_start_{k}")
        in_flight.append([send_sems, recv_sems, lands, layouts, False])
        chunk_of.update({m: (k, 2 * pos) for pos, m in enumerate(members)})
    lru_ax = jnp.concatenate([_block_diag(wts["lru_w_a"][0]), _block_diag(wts["lru_w_x"][0])], axis=1).astype(BF16)

    prefetch_at = {(i, N_SUB - 1): i + N_SUB for i in range(DEPTH - 1)}

    def layer_params(i, sub, x_in):
        k, pos = chunk_of[(i, sub)]
        send_sems, recv_sems, lands, layouts, state = in_flight[k]
        if state == "passing":
            in_flight[k][2:] = [_gather_forward_wait(send_sems, recv_sems, lands, layouts, x_in, f"gather_pass_wait_{k}"),
                                layouts, "here"]
        elif state != "here":
            lands = _gather_wait(send_sems, recv_sems, lands, layouts, x_in, f"gather_wait_{k}")
            in_flight[k][2:] = [_gather_forward(lands, layouts, f"gather_forward_{k}"), layouts, "here"]
        nxt = prefetch_at.get((i, sub))
        started = None
        if nxt is not None:
            send_sems, recv_sems, lands, layouts, _ = in_flight[nxt]
            lands = _gather_wait(send_sems, recv_sems, lands, layouts, x_in, f"gather_wait_{nxt}")
            send_sems, recv_sems, lands, started = _gather_forward_start(lands, layouts, f"gather_pass_start_{nxt}")
            in_flight[nxt] = [send_sems, recv_sems, lands, layouts, "passing"]
        w_in, w_out = in_flight[k][2][pos:pos + 2]
        w_out = w_out.reshape(-1, w_out.shape[-1])
        if sub != 1:
            out = {"ffn_in": [_W(w_in, (), True)], "ffn_out": [_W(w_out)], "after": started}
            if sub == 0:
                out.update(norm_pre=small_full["norm_pre"][i], norm_post=small_full["norm_post"][i])
            return out
        j = i // 3
        if i % 3 == 0:
            w_in = jnp.pad(_join_chips(w_in), ((0, 0), (0, FOX_PAD - 3 * D_MODEL - FOX_HEADS)))
            return {"mixer": {"w_in": _W(w_in), "w_out": _W(w_out), "b_f": wts["fox_b_f"][j][:, None]}}
        if i % 3 == 1:
            return {"mixer": {"w_in": _W(w_in, (), True), "w_out": _W(w_out), "conv_w": small_full["sconv_conv_w"][j]}}
        return {"mixer": {"w_in": _W(w_in, (), True), "w_out": _W(w_out), "conv_w": small_full["lru_conv_w"][j],
                          "conv_b": small_full["lru_conv_b"], "w_ax": _W(lru_ax),
                          "b_a": wts["lru_b_a"].reshape(1, D_MODEL), "b_x": wts["lru_b_x"].reshape(1, D_MODEL),
                          "lam": small_full["lru_lambda"]}}

    place_idx = jnp.stack([ic, chip]).astype(jnp.int32)
    c_idx = place_idx[:1]
    big_index = {n: o for o, (n, _) in enumerate(_BIG)}
    exchanges, pending = [], []

    def to_chips(after):
        i, send_sems, recv_sems, tensors, lands, homes = pending.pop()
        tensors, recv = _pair_wait(send_sems, recv_sems, tensors, lands, after, f"grads_pair_wait_l{i}")
        parts = [_pair_sum(t, r, c_idx, f"grads_pair_sum_l{i}_{k}") for k, (t, r) in enumerate(zip(tensors, recv))]
        send_sems, recv_sems, parts, lands, tok = _chip_send_start(parts, None, f"grads_chip_start_l{i}")
        exchanges.append((i, send_sems, recv_sems, parts, lands, homes))
        return tok

    def chip_blocks(g, by_rows, width):
        if by_rows:
            return g.reshape(N_CHIPS, g.shape[0] // N_CHIPS, g.shape[1])
        if g.ndim == 3:
            return g
        return g[:, :width * N_CHIPS].reshape(g.shape[0], N_CHIPS, width).transpose(1, 0, 2)

    def on_mid(i, dx):
        return to_chips(dx) if pending else None

    def on_grads(i, g, dx):
        n_in, n_out = mixer_names[i % 3]
        items = [("w_ffn_in", (i, k), g["ffn_in"][k]) for k in range(2)]
        items += [("w_ffn_out", (i, k), g["ffn_out"][k]) for k in range(2)]
        items += [(n_in, (i // 3,), g["mixer"]["w_in"]), (n_out, (i // 3,), g["mixer"]["w_out"])]
        tensors = [chip_blocks(t, dict(_BIG)[n], wts[n].shape[-1]) for n, _, t in items]
        homes = [(big_index[n], lead, wts[n].shape[-2]) for n, lead, _ in items]
        send_sems, recv_sems, tensors, lands, tok = _pair_start(tensors, None, f"grads_pair_start_l{i}")
        pending.append((i, send_sems, recv_sems, tensors, lands, homes))
        pair_tokens.append(tok)
        return tok

    pair_tokens = []
    loss_row, grad_x, dmod, lg = _local_step(x[0], target[0], mod, layer_params, on_grads, on_mid, token)
    loss = lax.psum(loss_row[0, 0], ("x", "y", "c"))
    dmod = dmod + pair_tokens[-1][0, 0]

    fox_layers = [i for i in range(DEPTH) if i % 3 == 0]
    sconv_g, lru_g = lg[1]["mixer"], lg[2]["mixer"]
    small_g = {
        "dmod": dmod, "norm_pre": jnp.stack([g["norm_pre"] for g in lg]), "norm_post": jnp.stack([g["norm_post"] for g in lg]),
        "fox_b_f": jnp.stack([lg[i]["mixer"]["b_f"][:, 0] for i in fox_layers]),
        "sconv_conv_w": sconv_g["conv_w"][None], "lru_conv_w": lru_g["conv_w"][None], "lru_conv_b": lru_g["conv_b"],
        "lru_w_a": lru_g["w_a"][None], "lru_b_a": lru_g["b_a"].reshape(1, LRU_BLOCKS, LRU_BLOCK_DIM),
        "lru_w_x": lru_g["w_x"][None], "lru_b_x": lru_g["b_x"].reshape(1, LRU_BLOCKS, LRU_BLOCK_DIM),
        "lru_lambda": lru_g["lam"]}
    g4 = _allgather8(_pack_rows(list(small_g.values())), "gather_small_grads").reshape(N_DEV, -1)
    last_start = to_chips(g4)
    summed = _sum_devices(g4, last_start, "sum_small_grads")[0]
    summed = dict(zip(small_g, _unpack(summed, [v.shape for v in small_g.values()])))
    grads = {n: (_my_columns(summed[n], chip) if n in _COL_SHARDED_SMALL else summed[n]) for n in _SMALL if n != "b_cond"}
    grads["b_cond"] = summed["dmod"].reshape(DEPTH, N_SUB * 3 * D_MODEL)

    dmod_all = (g4[:, :dmod.size] + last_start[0, 0]).reshape(N_DEV, DEPTH, N_SUB * 3 * D_MODEL)
    dmod_s = jnp.pad(_my_columns(dmod_all, chip).transpose(1, 0, 2), ((0, 0), (0, COND_PAD - N_DEV), (0, 0))).astype(BF16)
    c_t = jnp.pad(c_all.T, ((0, 0), (0, COND_PAD - N_DEV)))
    grads["w_cond"], d_cond, m_cond, v_cond = _cond_bwd_adamw(c_t, dmod_s, wts["w_cond"], mom["w_cond"],
                                                              var["w_cond"], "cond_bwd_adamw")

    big = [n for n, _ in _BIG]
    two_d = lambda a: a.reshape(-1, a.shape[-1])
    bufs = [lax.empty(wts[n].shape, F32) for n in big]
    updates = [[lax.empty(two_d(wts[n]).shape, F32) for _ in range(3)] for n in big]
    follows = d_cond
    for i, send_sems, recv_sems, parts, lands, homes in exchanges:
        parts, lands = _chip_send_wait(send_sems, recv_sems, parts, lands, follows, f"grads_chip_wait_l{i}")
        for k, (part, land, (o, lead, _)) in enumerate(zip(parts, lands, homes)):
            bufs[o] = _chip_sum(part, land, bufs[o], lead, place_idx, f"grads_chip_sum_l{i}_{k}")
        bufs = list(_pair_gather(bufs, homes, f"grads_pair_gather_l{i}"))
        for o in sorted({o for o, _, _ in homes}):
            n = big[o]
            starts = [sum(a * math.prod(wts[n].shape[d + 1:-1]) for d, a in enumerate(lead))
                      for oo, lead, _ in homes if oo == o]
            rows = wts[n].shape[-2]
            *updates[o], g_out = _adamw_rows(two_d(wts[n]), two_d(bufs[o]), two_d(mom[n]), two_d(var[n]), updates[o],
                                             min(starts), max(starts) + rows - min(starts), f"adamw_{n}_l{i}")
            bufs[o] = g_out.reshape(wts[n].shape)
        follows = updates[0][0]
    grads.update(zip(big, bufs))

    delta, new_m, new_v = {"w_cond": d_cond}, {"w_cond": m_cond}, {"w_cond": v_cond}
    for n, (d, nm, nv) in zip(big, updates):
        delta[n], new_m[n], new_v[n] = (a.reshape(wts[n].shape) for a in (d, nm, nv))
    shapes = [wts[n].shape for n in _SMALL]
    packed = [_pack_rows([src[n] for n in _SMALL]) for src in (wts, grads, mom, var)]
    for dst, out in zip((delta, new_m, new_v), _adamw(*packed, "adamw_small")):
        dst.update(zip(_SMALL, _unpack(out.reshape(-1), shapes)))

    return (loss, grad_x[None], *[grads[n] for n in _WEIGHTS], *[delta[n] for n in _WEIGHTS],
            *[new_m[n] for n in _WEIGHTS], *[new_v[n] for n in _WEIGHTS])


def kernel(x, c, w_cond, b_cond, norm_pre, norm_post, w_ffn_in, w_ffn_out, fox_w_in, fox_b_f, fox_w_out, sconv_w_in, sconv_conv_w, sconv_w_out, lru_w_in, lru_conv_w, lru_conv_b, lru_w_a, lru_b_a, lru_w_x, lru_b_x, lru_lambda, lru_w_out, loss_target, m_w_cond, m_b_cond, m_norm_pre, m_norm_post, m_w_ffn_in, m_w_ffn_out, m_fox_w_in, m_fox_b_f, m_fox_w_out, m_sconv_w_in, m_sconv_conv_w, m_sconv_w_out, m_lru_w_in, m_lru_conv_w, m_lru_conv_b, m_lru_w_a, m_lru_b_a, m_lru_w_x, m_lru_b_x, m_lru_lambda, m_lru_w_out, v_w_cond, v_b_cond, v_norm_pre, v_norm_post, v_w_ffn_in, v_w_ffn_out, v_fox_w_in, v_fox_b_f, v_fox_w_out, v_sconv_w_in, v_sconv_conv_w, v_sconv_w_out, v_lru_w_in, v_lru_conv_w, v_lru_conv_b, v_lru_w_a, v_lru_b_a, v_lru_w_x, v_lru_b_x, v_lru_lambda, v_lru_w_out):
    given = dict(locals())
    wts = {n: given[n] for n in _WEIGHTS}
    mom = {n: given["m_" + n] for n in _WEIGHTS}
    var = {n: given["v_" + n] for n in _WEIGHTS}
    return _step(x, c, loss_target, wts, mom, var)
```
